```python
import jax, jax.numpy as jnp
from jax import lax
import numpy as np

D_MODEL = 1024
BATCH = 8
SEQ = 4096
DEPTH = 4

CHUNK = 64
N_MIXERS = 4
N_HEADS = 16
HEAD_DIM = D_MODEL // N_HEADS
SB_BLOCK = 128
GMLP_BLOCK = 128
GMLP_WIDTH = 2 * D_MODEL
GMLP_GROUPS = 8
CONV_WIDTH = 3
BAND_CHUNKS = 8
REL_CLIP = 128
D_FF = -(-8 * D_MODEL // (3 * 256)) * 256
EPS = 1e-6

kernel_name = "hybrid_chunk_causal_trunk"


def rms_norm(x, g):
    xf = x.astype(jnp.float32)
    y = xf * lax.rsqrt(jnp.mean(xf * xf, axis=-1, keepdims=True) + EPS)
    return (y * g.astype(jnp.float32)).astype(x.dtype)


def layer_norm(x, g):
    xf = x.astype(jnp.float32)
    mu = jnp.mean(xf, axis=-1, keepdims=True)
    xc = xf - mu
    var = jnp.mean(xc * xc, axis=-1, keepdims=True)
    return (xc * lax.rsqrt(var + EPS) * g.astype(jnp.float32)).astype(x.dtype)


def modulate(h, shift, scale):
    return h * (1 + scale) + shift


def stick_breaking_mixer(h, w_qkv, w_o):
    b, s, _ = h.shape
    qkv = (h @ w_qkv).reshape(b, s, 3, N_HEADS, HEAD_DIM)
    q, k, v = qkv[:, :, 0], qkv[:, :, 1], qkv[:, :, 2]
    scale = HEAD_DIM ** -0.5
    outs = []
    for q0 in range(0, s, SB_BLOCK):
        end = q0 + SB_BLOCK
        z = jnp.einsum('bthd,bshd->bhts', q[:, q0:end], k[:, :end]).astype(jnp.float32) * scale
        t_pos = q0 + jnp.arange(SB_BLOCK)[:, None]
        s_pos = jnp.arange(end)[None, :]
        mask = s_pos < t_pos
        sp = jnp.where(mask, jax.nn.softplus(z), 0.0)
        rest = lax.cumsum(sp, axis=3, reverse=True)
        log_a = jnp.where(mask, z - rest, -jnp.inf)
        a = jnp.exp(log_a).astype(v.dtype)
        outs.append(jnp.einsum('bhts,bshd->bthd', a, v[:, :end]))
    o = jnp.concatenate(outs, axis=1).reshape(b, s, D_MODEL)
    return o @ w_o


def spatial_gating_mixer(h, w_in, ln_g, w_s, s_bias, w_out):
    b, s, _ = h.shape
    z = jax.nn.gelu(h @ w_in)
    u, v = jnp.split(z, 2, axis=-1)
    v = layer_norm(v, ln_g)
    nb = s // GMLP_BLOCK
    gc = GMLP_WIDTH // GMLP_GROUPS
    v = v.reshape(b, nb, GMLP_BLOCK, GMLP_GROUPS, gc)
    pos_chunk = jnp.arange(GMLP_BLOCK) // CHUNK
    mask = pos_chunk[:, None] >= pos_chunk[None, :]
    ws = jnp.where(mask[None], w_s, 0.0).astype(v.dtype)
    sv = jnp.einsum('gij,bnjgc->bnigc', ws, v) + s_bias.T[:, :, None].astype(v.dtype)
    y = u * sv.reshape(b, s, GMLP_WIDTH)
    return y @ w_out


def short_conv_mixer(h, w_in, conv_w, w_out):
    gb, gc, xt = jnp.split(h @ w_in, 3, axis=-1)
    y = gc * xt
    yc = lax.conv_general_dilated(
        y, conv_w.reshape(CONV_WIDTH, 1, D_MODEL).astype(y.dtype),
        window_strides=(1,), padding=[(CONV_WIDTH - 1, 0)],
        dimension_numbers=('NWC', 'WIO', 'NWC'), feature_group_count=D_MODEL)
    return (gb * yc) @ w_out


def chunk_band_attention_mixer(h, w_qkv, rel_bias, w_o):
    b, s, _ = h.shape
    nc = s // CHUNK
    nband = BAND_CHUNKS + 1
    qkv = (h @ w_qkv).reshape(b, s, 3, N_HEADS, HEAD_DIM)
    q = qkv[:, :, 0].reshape(b, nc, CHUNK, N_HEADS, HEAD_DIM)
    pad = ((0, 0), (BAND_CHUNKS * CHUNK, 0), (0, 0), (0, 0))
    kp = jnp.pad(qkv[:, :, 1], pad).reshape(b, nc + BAND_CHUNKS, CHUNK, N_HEADS, HEAD_DIM)
    vp = jnp.pad(qkv[:, :, 2], pad).reshape(b, nc + BAND_CHUNKS, CHUNK, N_HEADS, HEAD_DIM)
    scale = HEAD_DIM ** -0.5
    scores = jnp.concatenate(
        [jnp.einsum('bcihd,bcjhd->bhcij', q, kp[:, o:o + nc]) for o in range(nband)],
        axis=-1).astype(jnp.float32) * scale
    i_pos = jnp.arange(CHUNK)[:, None]
    j_pos = jnp.arange(nband * CHUNK)[None, :]
    dist = i_pos + BAND_CHUNKS * CHUNK - j_pos
    idx = jnp.clip(dist, -REL_CLIP, REL_CLIP) + REL_CLIP
    bias = rel_bias[:, idx].astype(jnp.float32)
    key_chunk = jnp.arange(nc)[:, None] - BAND_CHUNKS + j_pos // CHUNK
    valid = (key_chunk >= 0)[:, None, :]
    scores = jnp.where(valid, scores + bias[:, None], -jnp.inf)
    p = jax.nn.softmax(scores, axis=-1).astype(vp.dtype)
    out = jnp.einsum('bhcij,bcjhd->bcihd', p[..., :CHUNK], vp[:, 0:nc])
    for o in range(1, nband):
        out = out + jnp.einsum('bhcij,bcjhd->bcihd', p[..., o * CHUNK:(o + 1) * CHUNK], vp[:, o:o + nc])
    return out.reshape(b, s, D_MODEL) @ w_o


def swiglu_ffn(h, w_in, w_out):
    g, u = jnp.split(h @ w_in, 2, axis=-1)
    return (jax.nn.silu(g) * u) @ w_out


def _n_layers_of(m):
    return len(range(m, DEPTH, N_MIXERS))


def _dense(key, shape, fan_in, gain=1.0):
    return jax.random.normal(key, shape, jnp.float32) * (gain * fan_in ** -0.5)


def _fwd_setup_inputs(seed: int = 0) -> dict:
    key = jax.random.key(seed)
    ks = jax.random.split(key, 21)
    na, nb, ncv, nd = (_n_layers_of(m) for m in range(N_MIXERS))
    D = D_MODEL
    nrm = lambda k, shape: jax.random.normal(k, shape, jnp.float32)
    return {
        "x": nrm(ks[0], (BATCH, SEQ, D)),
        "c": nrm(ks[1], (BATCH, D)),
        "ada_w": _dense(ks[2], (DEPTH, D, 6 * D), D, 0.5),
        "ada_b": 0.02 * nrm(ks[3], (DEPTH, 6 * D)),
        "norm_g": 1.0 + 0.1 * nrm(ks[4], (DEPTH, 4, D)),
        "ffn_w_in": _dense(ks[5], (DEPTH, D, 2 * D_FF), D),
        "ffn_w_out": _dense(ks[6], (DEPTH, D_FF, D), D_FF),
        "sb_w_qkv": _dense(ks[7], (na, D, 3 * D), D),
        "sb_w_o": _dense(ks[8], (na, D, D), D),
        "sg_w_in": _dense(ks[9], (nb, D, 2 * GMLP_WIDTH), D),
        "sg_ln_g": 1.0 + 0.1 * nrm(ks[10], (nb, GMLP_WIDTH)),
        "sg_w_s": _dense(ks[11], (nb, GMLP_GROUPS, GMLP_BLOCK, GMLP_BLOCK), GMLP_BLOCK),
        "sg_bias": 1.0 + 0.1 * nrm(ks[12], (nb, GMLP_GROUPS, GMLP_BLOCK)),
        "sg_w_out": _dense(ks[13], (nb, GMLP_WIDTH, D), GMLP_WIDTH),
        "sc_w_in": _dense(ks[14], (ncv, D, 3 * D), D),
        "sc_conv_w": _dense(ks[15], (ncv, CONV_WIDTH, D), CONV_WIDTH),
        "sc_w_out": _dense(ks[16], (ncv, D, D), D),
        "cb_w_qkv": _dense(ks[17], (nd, D, 3 * D), D),
        "cb_rel_bias": 0.5 * nrm(ks[18], (nd, N_HEADS, 2 * REL_CLIP + 1)),
        "cb_w_o": _dense(ks[19], (nd, D, D), D),
    }


def _fwd_reference(x, c, ada_w, ada_b, norm_g, ffn_w_in, ffn_w_out,
              sb_w_qkv, sb_w_o,
              sg_w_in, sg_ln_g, sg_w_s, sg_bias, sg_w_out,
              sc_w_in, sc_conv_w, sc_w_out,
              cb_w_qkv, cb_rel_bias, cb_w_o):
    mod_all = jnp.einsum('bd,lde->lbe', jax.nn.silu(c), ada_w) + ada_b[:, None]
    for i in range(DEPTH):
        m, r = i % N_MIXERS, i // N_MIXERS
        sh_m, sc_m, gt_m, sh_f, sc_f, gt_f = jnp.split(mod_all[i][:, None, :], 6, axis=-1)
        h = modulate(rms_norm(x, norm_g[i, 0]), sh_m, sc_m)
        if m == 0:
            y = stick_breaking_mixer(h, sb_w_qkv[r], sb_w_o[r])
        elif m == 1:
            y = spatial_gating_mixer(h, sg_w_in[r], sg_ln_g[r], sg_w_s[r], sg_bias[r], sg_w_out[r])
        elif m == 2:
            y = short_conv_mixer(h, sc_w_in[r], sc_conv_w[r], sc_w_out[r])
        else:
            y = chunk_band_attention_mixer(h, cb_w_qkv[r], cb_rel_bias[r], cb_w_o[r])
        x = x + gt_m * rms_norm(y, norm_g[i, 1])
        h = modulate(rms_norm(x, norm_g[i, 2]), sh_f, sc_f)
        y = swiglu_ffn(h, ffn_w_in[i], ffn_w_out[i])
        x = x + gt_f * rms_norm(y, norm_g[i, 3])
    return x


import jax as _jax
import jax.numpy as _jnp

TWIN_FORMAT = 'train_step'
FWD_PARAMS = ['x', 'c', 'ada_w', 'ada_b', 'norm_g', 'ffn_w_in', 'ffn_w_out', 'sb_w_qkv', 'sb_w_o', 'sg_w_in', 'sg_ln_g', 'sg_w_s', 'sg_bias', 'sg_w_out', 'sc_w_in', 'sc_conv_w', 'sc_w_out', 'cb_w_qkv', 'cb_rel_bias', 'cb_w_o']
TWIN_WEIGHTS = ['ada_w', 'ada_b', 'norm_g', 'ffn_w_in', 'ffn_w_out', 'sb_w_qkv', 'sb_w_o', 'sg_w_in', 'sg_ln_g', 'sg_w_s', 'sg_bias', 'sg_w_out', 'sc_w_in', 'sc_conv_w', 'sc_w_out', 'cb_w_qkv', 'cb_rel_bias', 'cb_w_o']
TWIN_DIFF_INPUT = 'x'
TWIN_INPUTS = ['x', 'c', 'ada_w', 'ada_b', 'norm_g', 'ffn_w_in', 'ffn_w_out', 'sb_w_qkv', 'sb_w_o', 'sg_w_in', 'sg_ln_g', 'sg_w_s', 'sg_bias', 'sg_w_out', 'sc_w_in', 'sc_conv_w', 'sc_w_out', 'cb_w_qkv', 'cb_rel_bias', 'cb_w_o', 'loss_target', 'm_ada_w', 'm_ada_b', 'm_norm_g', 'm_ffn_w_in', 'm_ffn_w_out', 'm_sb_w_qkv', 'm_sb_w_o', 'm_sg_w_in', 'm_sg_ln_g', 'm_sg_w_s', 'm_sg_bias', 'm_sg_w_out', 'm_sc_w_in', 'm_sc_conv_w', 'm_sc_w_out', 'm_cb_w_qkv', 'm_cb_rel_bias', 'm_cb_w_o', 'v_ada_w', 'v_ada_b', 'v_norm_g', 'v_ffn_w_in', 'v_ffn_w_out', 'v_sb_w_qkv', 'v_sb_w_o', 'v_sg_w_in', 'v_sg_ln_g', 'v_sg_w_s', 'v_sg_bias', 'v_sg_w_out', 'v_sc_w_in', 'v_sc_conv_w', 'v_sc_w_out', 'v_cb_w_qkv', 'v_cb_rel_bias', 'v_cb_w_o']
TWIN_OUTPUTS = ['loss', 'grad_x', 'grad_ada_w', 'grad_ada_b', 'grad_norm_g', 'grad_ffn_w_in', 'grad_ffn_w_out', 'grad_sb_w_qkv', 'grad_sb_w_o', 'grad_sg_w_in', 'grad_sg_ln_g', 'grad_sg_w_s', 'grad_sg_bias', 'grad_sg_w_out', 'grad_sc_w_in', 'grad_sc_conv_w', 'grad_sc_w_out', 'grad_cb_w_qkv', 'grad_cb_rel_bias', 'grad_cb_w_o', 'delta_ada_w', 'delta_ada_b', 'delta_norm_g', 'delta_ffn_w_in', 'delta_ffn_w_out', 'delta_sb_w_qkv', 'delta_sb_w_o', 'delta_sg_w_in', 'delta_sg_ln_g', 'delta_sg_w_s', 'delta_sg_bias', 'delta_sg_w_out', 'delta_sc_w_in', 'delta_sc_conv_w', 'delta_sc_w_out', 'delta_cb_w_qkv', 'delta_cb_rel_bias', 'delta_cb_w_o', 'new_m_ada_w', 'new_m_ada_b', 'new_m_norm_g', 'new_m_ffn_w_in', 'new_m_ffn_w_out', 'new_m_sb_w_qkv', 'new_m_sb_w_o', 'new_m_sg_w_in', 'new_m_sg_ln_g', 'new_m_sg_w_s', 'new_m_sg_bias', 'new_m_sg_w_out', 'new_m_sc_w_in', 'new_m_sc_conv_w', 'new_m_sc_w_out', 'new_m_cb_w_qkv', 'new_m_cb_rel_bias', 'new_m_cb_w_o', 'new_v_ada_w', 'new_v_ada_b', 'new_v_norm_g', 'new_v_ffn_w_in', 'new_v_ffn_w_out', 'new_v_sb_w_qkv', 'new_v_sb_w_o', 'new_v_sg_w_in', 'new_v_sg_ln_g', 'new_v_sg_w_s', 'new_v_sg_bias', 'new_v_sg_w_out', 'new_v_sc_w_in', 'new_v_sc_conv_w', 'new_v_sc_w_out', 'new_v_cb_w_qkv', 'new_v_cb_rel_bias', 'new_v_cb_w_o']
TWIN_LEAF_KINDS = {'loss': 'loss', 'grad_x': 'grad_x', 'grad_ada_w': 'grad_w', 'grad_ada_b': 'grad_w', 'grad_norm_g': 'grad_w', 'grad_ffn_w_in': 'grad_w', 'grad_ffn_w_out': 'grad_w', 'grad_sb_w_qkv': 'grad_w', 'grad_sb_w_o': 'grad_w', 'grad_sg_w_in': 'grad_w', 'grad_sg_ln_g': 'grad_w', 'grad_sg_w_s': 'grad_w', 'grad_sg_bias': 'grad_w', 'grad_sg_w_out': 'grad_w', 'grad_sc_w_in': 'grad_w', 'grad_sc_conv_w': 'grad_w', 'grad_sc_w_out': 'grad_w', 'grad_cb_w_qkv': 'grad_w', 'grad_cb_rel_bias': 'grad_w', 'grad_cb_w_o': 'grad_w', 'delta_ada_w': 'delta_w', 'delta_ada_b': 'delta_w', 'delta_norm_g': 'delta_w', 'delta_ffn_w_in': 'delta_w', 'delta_ffn_w_out': 'delta_w', 'delta_sb_w_qkv': 'delta_w', 'delta_sb_w_o': 'delta_w', 'delta_sg_w_in': 'delta_w', 'delta_sg_ln_g': 'delta_w', 'delta_sg_w_s': 'delta_w', 'delta_sg_bias': 'delta_w', 'delta_sg_w_out': 'delta_w', 'delta_sc_w_in': 'delta_w', 'delta_sc_conv_w': 'delta_w', 'delta_sc_w_out': 'delta_w', 'delta_cb_w_qkv': 'delta_w', 'delta_cb_rel_bias': 'delta_w', 'delta_cb_w_o': 'delta_w', 'new_m_ada_w': 'new_m', 'new_m_ada_b': 'new_m', 'new_m_norm_g': 'new_m', 'new_m_ffn_w_in': 'new_m', 'new_m_ffn_w_out': 'new_m', 'new_m_sb_w_qkv': 'new_m', 'new_m_sb_w_o': 'new_m', 'new_m_sg_w_in': 'new_m', 'new_m_sg_ln_g': 'new_m', 'new_m_sg_w_s': 'new_m', 'new_m_sg_bias': 'new_m', 'new_m_sg_w_out': 'new_m', 'new_m_sc_w_in': 'new_m', 'new_m_sc_conv_w': 'new_m', 'new_m_sc_w_out': 'new_m', 'new_m_cb_w_qkv': 'new_m', 'new_m_cb_rel_bias': 'new_m', 'new_m_cb_w_o': 'new_m', 'new_v_ada_w': 'new_v', 'new_v_ada_b': 'new_v', 'new_v_norm_g': 'new_v', 'new_v_ffn_w_in': 'new_v', 'new_v_ffn_w_out': 'new_v', 'new_v_sb_w_qkv': 'new_v', 'new_v_sb_w_o': 'new_v', 'new_v_sg_w_in': 'new_v', 'new_v_sg_ln_g': 'new_v', 'new_v_sg_w_s': 'new_v', 'new_v_sg_bias': 'new_v', 'new_v_sg_w_out': 'new_v', 'new_v_sc_w_in': 'new_v', 'new_v_sc_conv_w': 'new_v', 'new_v_sc_w_out': 'new_v', 'new_v_cb_w_qkv': 'new_v', 'new_v_cb_rel_bias': 'new_v', 'new_v_cb_w_o': 'new_v'}


def _forward(args):
    return _fwd_reference(*[args[k] for k in FWD_PARAMS])


def _output_shape():
    def fwd():
        inp = _fwd_setup_inputs(0)
        return _fwd_reference(*[inp[k] for k in FWD_PARAMS])
    out = _jax.eval_shape(fwd)
    return out.shape, out.dtype

N_MICROBATCH = 1
ADAM_LR = 0.001
ADAM_B1 = 0.9
ADAM_B2 = 0.999
ADAM_EPS = 1e-08
ADAM_WD = 0.01
ADAM_STEP = 10
PER_EXAMPLE_BATCH_AXIS = {'x': 0, 'c': 0, 'loss_target': 0}
SHARED_INPUTS = []
_WEIGHT_DTYPES = {'ada_w': _jnp.float32, 'ada_b': _jnp.float32, 'norm_g': _jnp.float32, 'ffn_w_in': _jnp.float32, 'ffn_w_out': _jnp.float32, 'sb_w_qkv': _jnp.float32, 'sb_w_o': _jnp.float32, 'sg_w_in': _jnp.float32, 'sg_ln_g': _jnp.float32, 'sg_w_s': _jnp.float32, 'sg_bias': _jnp.float32, 'sg_w_out': _jnp.float32, 'sc_w_in': _jnp.float32, 'sc_conv_w': _jnp.float32, 'sc_w_out': _jnp.float32, 'cb_w_qkv': _jnp.float32, 'cb_rel_bias': _jnp.float32, 'cb_w_o': _jnp.float32}
MOMENT_SCALE = {'ada_w': 1.602599e+00, 'ada_b': 3.201769e+00, 'norm_g': 2.436087e+00, 'ffn_w_in': 1.133125e-01, 'ffn_w_out': 2.191576e-01, 'sb_w_qkv': 3.632762e-01, 'sb_w_o': 6.402402e-01, 'sg_w_in': 1.443467e-01, 'sg_ln_g': 5.546539e-02, 'sg_w_s': 7.969685e-02, 'sg_bias': 1.007621e-01, 'sg_w_out': 6.529260e-01, 'sc_w_in': 1.110894e-01, 'sc_conv_w': 1.150290e-01, 'sc_w_out': 1.201479e-01, 'cb_w_qkv': 8.282616e-01, 'cb_rel_bias': 1.971342e-02, 'cb_w_o': 1.386398e+00}


def _to_microbatches(a, axis):
    t = _jnp.moveaxis(a, axis, 0)
    t = t.reshape((N_MICROBATCH, t.shape[0] // N_MICROBATCH) + t.shape[1:])
    return _jnp.moveaxis(t, 1, axis + 1)


def setup_inputs(seed: int = 0) -> dict:
    inp = _fwd_setup_inputs(seed)
    key = _jax.random.fold_in(_jax.random.key(seed), 7919)
    shape, _ = _output_shape()
    out = dict(inp)
    out["loss_target"] = _jax.random.normal(_jax.random.fold_in(key, 0), shape, _jnp.float32)
    for i, name in enumerate(TWIN_WEIGHTS):
        w = inp[name].astype(_jnp.float32)
        if MOMENT_SCALE is None:
            s = _jnp.sqrt(_jnp.mean(_jnp.square(w)) + 1e-30)
        else:
            s = MOMENT_SCALE[name]
        km, kv = _jax.random.split(_jax.random.fold_in(key, i + 1))
        out[name] = w
        out["m_" + name] = s * _jax.random.normal(km, w.shape, _jnp.float32)
        out["v_" + name] = (s * s) * _jax.random.uniform(kv, w.shape, _jnp.float32, 0.5, 1.5)
    if N_MICROBATCH > 1:
        for name, axis in PER_EXAMPLE_BATCH_AXIS.items():
            out[name] = _to_microbatches(out[name], axis)
    return {'x': out['x'], 'c': out['c'], 'ada_w': out['ada_w'], 'ada_b': out['ada_b'], 'norm_g': out['norm_g'], 'ffn_w_in': out['ffn_w_in'], 'ffn_w_out': out['ffn_w_out'], 'sb_w_qkv': out['sb_w_qkv'], 'sb_w_o': out['sb_w_o'], 'sg_w_in': out['sg_w_in'], 'sg_ln_g': out['sg_ln_g'], 'sg_w_s': out['sg_w_s'], 'sg_bias': out['sg_bias'], 'sg_w_out': out['sg_w_out'], 'sc_w_in': out['sc_w_in'], 'sc_conv_w': out['sc_conv_w'], 'sc_w_out': out['sc_w_out'], 'cb_w_qkv': out['cb_w_qkv'], 'cb_rel_bias': out['cb_rel_bias'], 'cb_w_o': out['cb_w_o'], 'loss_target': out['loss_target'], 'm_ada_w': out['m_ada_w'], 'm_ada_b': out['m_ada_b'], 'm_norm_g': out['m_norm_g'], 'm_ffn_w_in': out['m_ffn_w_in'], 'm_ffn_w_out': out['m_ffn_w_out'], 'm_sb_w_qkv': out['m_sb_w_qkv'], 'm_sb_w_o': out['m_sb_w_o'], 'm_sg_w_in': out['m_sg_w_in'], 'm_sg_ln_g': out['m_sg_ln_g'], 'm_sg_w_s': out['m_sg_w_s'], 'm_sg_bias': out['m_sg_bias'], 'm_sg_w_out': out['m_sg_w_out'], 'm_sc_w_in': out['m_sc_w_in'], 'm_sc_conv_w': out['m_sc_conv_w'], 'm_sc_w_out': out['m_sc_w_out'], 'm_cb_w_qkv': out['m_cb_w_qkv'], 'm_cb_rel_bias': out['m_cb_rel_bias'], 'm_cb_w_o': out['m_cb_w_o'], 'v_ada_w': out['v_ada_w'], 'v_ada_b': out['v_ada_b'], 'v_norm_g': out['v_norm_g'], 'v_ffn_w_in': out['v_ffn_w_in'], 'v_ffn_w_out': out['v_ffn_w_out'], 'v_sb_w_qkv': out['v_sb_w_qkv'], 'v_sb_w_o': out['v_sb_w_o'], 'v_sg_w_in': out['v_sg_w_in'], 'v_sg_ln_g': out['v_sg_ln_g'], 'v_sg_w_s': out['v_sg_w_s'], 'v_sg_bias': out['v_sg_bias'], 'v_sg_w_out': out['v_sg_w_out'], 'v_sc_w_in': out['v_sc_w_in'], 'v_sc_conv_w': out['v_sc_conv_w'], 'v_sc_w_out': out['v_sc_w_out'], 'v_cb_w_qkv': out['v_cb_w_qkv'], 'v_cb_rel_bias': out['v_cb_rel_bias'], 'v_cb_w_o': out['v_cb_w_o']}


def _loss(weights, diff, rest, loss_target):
    with _jax.named_scope("forward"):
        args = {**rest, TWIN_DIFF_INPUT: diff, **{k: w.astype(_WEIGHT_DTYPES[k]) for k, w in weights.items()}}
        y = _forward(args)
    with _jax.named_scope("loss_head"):
        err = _jnp.square(y.astype(_jnp.float32) - loss_target)
        return 0.5 * _jnp.sum(_jnp.mean(err, axis=-1)) if err.ndim else 0.5 * err


def _adamw(w, g, m, v):
    m = ADAM_B1 * m + (1.0 - ADAM_B1) * g
    v = ADAM_B2 * v + (1.0 - ADAM_B2) * _jnp.square(g)
    m_hat = m / (1.0 - ADAM_B1 ** ADAM_STEP)
    v_hat = v / (1.0 - ADAM_B2 ** ADAM_STEP)
    delta = -ADAM_LR * (m_hat / (_jnp.sqrt(v_hat) + ADAM_EPS) + ADAM_WD * w)
    return delta, m, v


def reference(x, c, ada_w, ada_b, norm_g, ffn_w_in, ffn_w_out, sb_w_qkv, sb_w_o, sg_w_in, sg_ln_g, sg_w_s, sg_bias, sg_w_out, sc_w_in, sc_conv_w, sc_w_out, cb_w_qkv, cb_rel_bias, cb_w_o, loss_target, m_ada_w, m_ada_b, m_norm_g, m_ffn_w_in, m_ffn_w_out, m_sb_w_qkv, m_sb_w_o, m_sg_w_in, m_sg_ln_g, m_sg_w_s, m_sg_bias, m_sg_w_out, m_sc_w_in, m_sc_conv_w, m_sc_w_out, m_cb_w_qkv, m_cb_rel_bias, m_cb_w_o, v_ada_w, v_ada_b, v_norm_g, v_ffn_w_in, v_ffn_w_out, v_sb_w_qkv, v_sb_w_o, v_sg_w_in, v_sg_ln_g, v_sg_w_s, v_sg_bias, v_sg_w_out, v_sc_w_in, v_sc_conv_w, v_sc_w_out, v_cb_w_qkv, v_cb_rel_bias, v_cb_w_o):
    given = dict(x=x, c=c, ada_w=ada_w, ada_b=ada_b, norm_g=norm_g, ffn_w_in=ffn_w_in, ffn_w_out=ffn_w_out, sb_w_qkv=sb_w_qkv, sb_w_o=sb_w_o, sg_w_in=sg_w_in, sg_ln_g=sg_ln_g, sg_w_s=sg_w_s, sg_bias=sg_bias, sg_w_out=sg_w_out, sc_w_in=sc_w_in, sc_conv_w=sc_conv_w, sc_w_out=sc_w_out, cb_w_qkv=cb_w_qkv, cb_rel_bias=cb_rel_bias, cb_w_o=cb_w_o, loss_target=loss_target, m_ada_w=m_ada_w, m_ada_b=m_ada_b, m_norm_g=m_norm_g, m_ffn_w_in=m_ffn_w_in, m_ffn_w_out=m_ffn_w_out, m_sb_w_qkv=m_sb_w_qkv, m_sb_w_o=m_sb_w_o, m_sg_w_in=m_sg_w_in, m_sg_ln_g=m_sg_ln_g, m_sg_w_s=m_sg_w_s, m_sg_bias=m_sg_bias, m_sg_w_out=m_sg_w_out, m_sc_w_in=m_sc_w_in, m_sc_conv_w=m_sc_conv_w, m_sc_w_out=m_sc_w_out, m_cb_w_qkv=m_cb_w_qkv, m_cb_rel_bias=m_cb_rel_bias, m_cb_w_o=m_cb_w_o, v_ada_w=v_ada_w, v_ada_b=v_ada_b, v_norm_g=v_norm_g, v_ffn_w_in=v_ffn_w_in, v_ffn_w_out=v_ffn_w_out, v_sb_w_qkv=v_sb_w_qkv, v_sb_w_o=v_sb_w_o, v_sg_w_in=v_sg_w_in, v_sg_ln_g=v_sg_ln_g, v_sg_w_s=v_sg_w_s, v_sg_bias=v_sg_bias, v_sg_w_out=v_sg_w_out, v_sc_w_in=v_sc_w_in, v_sc_conv_w=v_sc_conv_w, v_sc_w_out=v_sc_w_out, v_cb_w_qkv=v_cb_w_qkv, v_cb_rel_bias=v_cb_rel_bias, v_cb_w_o=v_cb_w_o)
    weights = {n: given[n] for n in TWIN_WEIGHTS}
    shared = {n: given[n] for n in SHARED_INPUTS}
    per_example = {n: given[n] for n in ['x', 'c']}
    grad_fn = _jax.value_and_grad(_loss, argnums=(0, 1))

    def one_microbatch(ex, loss_target):
        ex = dict(ex)
        diff = ex.pop(TWIN_DIFF_INPUT)
        return grad_fn(weights, diff, {**shared, **ex}, loss_target)

    if N_MICROBATCH == 1:
        loss, (grad_w, grad_x) = one_microbatch(per_example, given["loss_target"])
    else:
        def body(carry, xs):
            loss_sum, grad_sum = carry
            l_k, (gw_k, gx_k) = one_microbatch(xs[0], xs[1])
            with _jax.named_scope("update"):
                return (loss_sum + l_k, _jax.tree.map(_jnp.add, grad_sum, gw_k)), gx_k

        init = (_jnp.zeros((), _jnp.float32), _jax.tree.map(_jnp.zeros_like, weights))
        (loss, grad_w), grad_x = _jax.lax.scan(body, init, (per_example, given["loss_target"]))
    with _jax.named_scope("update"):
        delta_w, new_m, new_v = {}, {}, {}
        for n in TWIN_WEIGHTS:
            delta_w[n], new_m[n], new_v[n] = _adamw(weights[n], grad_w[n], given["m_" + n], given["v_" + n])
    return (loss, grad_x, *[grad_w[n] for n in TWIN_WEIGHTS], *[delta_w[n] for n in TWIN_WEIGHTS],
            *[new_m[n] for n in TWIN_WEIGHTS], *[new_v[n] for n in TWIN_WEIGHTS])
```

```python
import jax
import jax.numpy as jnp
from jax import lax
from jax.experimental import pallas as pl
from jax.experimental.pallas import tpu as pltpu

F32 = jnp.float32
BF16 = jnp.bfloat16
MESH = pl.DeviceIdType.MESH

N_DEV = 8
D_MODEL = 1024
N_HEADS = 16
HEAD_DIM = 64
QK_SCALE = HEAD_DIM ** -0.5
BLK = 128
BAND_BLOCKS = 5
BAND_W = BAND_BLOCKS * BLK
REL_CLIP = 128
EPS = 1e-6
NEG = -1e30
GELU_C0 = 0.7978845608028654
GELU_C1 = 0.044715
ADAM_LR = 0.001
ADAM_B1 = 0.9
ADAM_B2 = 0.999
ADAM_EPS = 1e-08
ADAM_WD = 0.01
ADAM_STEP = 10
VMEM_LIMIT = 56 * 1024 * 1024


def _call(body, **kw):
    return pl.pallas_call(body, **kw)


def _params(*sem):
    return pltpu.CompilerParams(dimension_semantics=sem, vmem_limit_bytes=VMEM_LIMIT)


def _sds(shape, dtype):
    return jax.ShapeDtypeStruct(tuple(shape), dtype)


def _row_tile(t):
    return min(512, t)


def _me():
    x, y, c = lax.axis_index("x"), lax.axis_index("y"), lax.axis_index("c")
    return x, y, c


def _all_gather(arrs, name):
    n = len(arrs)

    def body(*refs):
        x_refs, o_refs = refs[:n], refs[n:2 * n]
        send_sems, recv_sems, local_sems = refs[2 * n:]
        x, y, c = _me()
        me, sibling = (x, y, c), (x, y, 1 - c)
        chips = [(1 - x, y), (x, 1 - y), (1 - x, 1 - y)]
        waits = []
        for a in range(n):
            x_ref, o_ref = x_refs[a], o_refs[a]

            def rows(px, py, pc, o_ref=o_ref):
                return o_ref.at[4 * px + 2 * py + pc]

            def copy(k, block, to, src=None, a=a, rows=rows):
                return pltpu.make_async_remote_copy(
                    src_ref=rows(*block) if src is None else src,
                    dst_ref=rows(*block),
                    send_sem=send_sems.at[a, k],
                    recv_sem=recv_sems.at[a, k],
                    device_id=to,
                    device_id_type=MESH,
                )

            mine = pltpu.make_async_copy(x_ref, rows(*me), local_sems.at[a])
            mine.start()
            first = [copy(0, me, sibling, src=x_ref)]
            first += [copy(1 + j, me, (*chip, c), src=x_ref) for j, chip in enumerate(chips)]
            for cp in first:
                cp.start()
            waits.append((copy, mine, first))
        sends = []
        for a in range(n):
            copy, mine, first = waits[a]
            passed = [copy(4 + j, (*chip, c), sibling) for j, chip in enumerate(chips)]
            for j, chip in enumerate(chips):
                copy(1 + j, (*chip, c), me).wait_recv()
                passed[j].start()
            sends.append(first + passed)
        for a in range(n):
            copy, mine, first = waits[a]
            copy(0, sibling, me).wait_recv()
            for j, chip in enumerate(chips):
                copy(4 + j, (*chip, 1 - c), me).wait_recv()
            for cp in sends[a]:
                cp.wait_send()
            mine.wait()

    any_spec = pl.BlockSpec(memory_space=pl.ANY)
    outs = _call(
        body,
        name=name,
        out_shape=[_sds((N_DEV,) + a.shape, a.dtype) for a in arrs],
        in_specs=[any_spec] * n,
        out_specs=[any_spec] * n,
        scratch_shapes=[
            pltpu.SemaphoreType.DMA((n, 7)),
            pltpu.SemaphoreType.DMA((n, 7)),
            pltpu.SemaphoreType.DMA((n,)),
        ],
    )(*arrs)
    return list(outs)


def _all_to_all(arrs, name):
    n = len(arrs)

    def body(*refs):
        x_refs, o_refs = refs[:n], refs[n:2 * n]
        send_sems, recv_sems, local_sems = refs[2 * n:]
        x, y, c = _me()
        me = 4 * x + 2 * y + c
        started = []
        for a in range(n):
            x_ref, o_ref = x_refs[a], o_refs[a]
            mine = pltpu.make_async_copy(x_ref.at[me], o_ref.at[me], local_sems.at[a])
            mine.start()
            cps = []
            for k in range(1, N_DEV):
                px = 1 - x if k & 4 else x
                py = 1 - y if k & 2 else y
                pc = 1 - c if k & 1 else c
                peer = 4 * px + 2 * py + pc
                send = pltpu.make_async_remote_copy(
                    src_ref=x_ref.at[peer], dst_ref=o_ref.at[me],
                    send_sem=send_sems.at[a, k - 1], recv_sem=recv_sems.at[a, k - 1],
                    device_id=(px, py, pc), device_id_type=MESH)
                send.start()
                recv = pltpu.make_async_remote_copy(
                    src_ref=x_ref.at[peer], dst_ref=o_ref.at[peer],
                    send_sem=send_sems.at[a, k - 1], recv_sem=recv_sems.at[a, k - 1],
                    device_id=(px, py, pc), device_id_type=MESH)
                cps.append((send, recv))
            started.append((mine, cps))
        for mine, cps in started:
            for send, recv in cps:
                recv.wait_recv()
            for send, recv in cps:
                send.wait_send()
            mine.wait()

    any_spec = pl.BlockSpec(memory_space=pl.ANY)
    outs = _call(
        body,
        name=name,
        out_shape=[_sds(a.shape, a.dtype) for a in arrs],
        in_specs=[any_spec] * n,
        out_specs=[any_spec] * n,
        scratch_shapes=[
            pltpu.SemaphoreType.DMA((n, 7)),
            pltpu.SemaphoreType.DMA((n, 7)),
            pltpu.SemaphoreType.DMA((n,)),
        ],
    )(*arrs)
    return list(outs)


NN = (((1,), (0,)), ((), ()))
NT = (((1,), (1,)), ((), ()))
TN = (((0,), (0,)), ((), ()))


def _gemm(a, b, out_shape, out_dtype, grid, a_spec, b_spec, o_spec, acc_shape, dims, name):
    nk = grid[2]

    if nk == 1:
        def body(a_ref, b_ref, o_ref):
            r = lax.dot_general(a_ref[...].astype(BF16), b_ref[...].astype(BF16), dims,
                                preferred_element_type=F32)
            o_ref[...] = r.astype(o_ref.dtype)
        scratch = []
    else:
        def body(a_ref, b_ref, o_ref, acc_ref):
            k = pl.program_id(2)

            @pl.when(k == 0)
            def _():
                acc_ref[...] = jnp.zeros_like(acc_ref)

            acc_ref[...] += lax.dot_general(a_ref[...].astype(BF16), b_ref[...].astype(BF16), dims,
                                            preferred_element_type=F32)

            @pl.when(k == nk - 1)
            def _():
                o_ref[...] = acc_ref[...].astype(o_ref.dtype)
        scratch = [pltpu.VMEM(acc_shape, F32)]

    return _call(
        body, name=name, out_shape=_sds(out_shape, out_dtype), grid=grid,
        in_specs=[a_spec, b_spec], out_specs=o_spec, scratch_shapes=scratch,
        compiler_params=_params("parallel", "parallel", "arbitrary"),
    )(a, b)


def _div_tile(n, want):
    if n <= want:
        return n
    t = want - want % 128
    while n % t:
        t -= 128
    return t


def mm(a, b, name, out_dtype=F32, tm=512, tn=1024, tk=1024):
    m, k = a.shape
    n = b.shape[1]
    tm, tn, tk = _div_tile(m, tm), _div_tile(n, tn), _div_tile(k, tk)
    return _gemm(a, b, (m, n), out_dtype, (m // tm, n // tn, k // tk),
                 pl.BlockSpec((tm, tk), lambda i, j, kk: (i, kk)),
                 pl.BlockSpec((tk, tn), lambda i, j, kk: (kk, j)),
                 pl.BlockSpec((tm, tn), lambda i, j, kk: (i, j)),
                 (tm, tn), NN, name)


def mm_nt(a, b, name, out_dtype=F32, tm=512, tn=1024, tk=1024):
    m, n = a.shape
    k = b.shape[0]
    tm, tk_out, tred = _div_tile(m, tm), _div_tile(k, tn), _div_tile(n, tk)
    return _gemm(a, b, (m, k), out_dtype, (m // tm, k // tk_out, n // tred),
                 pl.BlockSpec((tm, tred), lambda i, j, kk: (i, kk)),
                 pl.BlockSpec((tk_out, tred), lambda i, j, kk: (j, kk)),
                 pl.BlockSpec((tm, tk_out), lambda i, j, kk: (i, j)),
                 (tm, tk_out), NT, name)


def mm_tn(a, b, name, out_dtype=F32, tm=512, tn=1024, tk=1024):
    m, k = a.shape
    n = b.shape[1]
    tk_out, tn, tred = _div_tile(k, tk), _div_tile(n, tn), _div_tile(m, tm)
    return _gemm(a, b, (k, n), out_dtype, (k // tk_out, n // tn, m // tred),
                 pl.BlockSpec((tred, tk_out), lambda i, j, kk: (kk, i)),
                 pl.BlockSpec((tred, tn), lambda i, j, kk: (kk, j)),
                 pl.BlockSpec((tk_out, tn), lambda i, j, kk: (i, j)),
                 (tk_out, tn), TN, name)


def mm_cs(a, wg, name, act_major=False, out_dtype=F32, tm=1024):
    m, k = a.shape
    s, _, n = wg.shape
    tm = _div_tile(m, tm)
    if act_major:
        out_shape, o_spec = (s, m, n), pl.BlockSpec((None, tm, n), lambda i, j, kk: (j, i, 0))
    else:
        out_shape, o_spec = (m, s * n), pl.BlockSpec((tm, n), lambda i, j, kk: (i, j))
    return _gemm(a, wg, out_shape, out_dtype, (m // tm, s, 1),
                 pl.BlockSpec((tm, k), lambda i, j, kk: (i, 0)),
                 pl.BlockSpec((None, k, n), lambda i, j, kk: (j, 0, 0)),
                 o_spec, (tm, n), NN, name)


def mm_cs_dx(da, wg, name, act_major=False, out_dtype=F32, tm=1024):
    s, k, n = wg.shape
    m = da.shape[1] if act_major else da.shape[0]
    tm = _div_tile(m, tm)
    if act_major:
        a_spec = pl.BlockSpec((None, tm, n), lambda i, j, kk: (kk, i, 0))
    else:
        a_spec = pl.BlockSpec((tm, n), lambda i, j, kk: (i, kk))
    return _gemm(da, wg, (m, k), out_dtype, (m // tm, 1, s), a_spec,
                 pl.BlockSpec((None, k, n), lambda i, j, kk: (kk, 0, 0)),
                 pl.BlockSpec((tm, k), lambda i, j, kk: (i, 0)),
                 (tm, k), NT, name)


def mm_cs_dw(a, da, name, act_major=False, out_dtype=F32, tm=1024):
    m, k = a.shape
    if act_major:
        s, _, n = da.shape
    else:
        s, n = N_DEV, da.shape[1] // N_DEV
    tm = _div_tile(m, tm)
    if act_major:
        b_spec = pl.BlockSpec((None, tm, n), lambda i, j, kk: (i, kk, 0))
    else:
        b_spec = pl.BlockSpec((tm, n), lambda i, j, kk: (kk, i))
    return _gemm(a, da, (s, k, n), out_dtype, (s, 1, m // tm),
                 pl.BlockSpec((tm, k), lambda i, j, kk: (kk, 0)), b_spec,
                 pl.BlockSpec((None, k, n), lambda i, j, kk: (i, 0, 0)),
                 (k, n), TN, name)


def mm_rs(s3, w3, name, out_dtype=F32, tm=1024):
    s, m, n = s3.shape
    nn = w3.shape[2]
    tm = _div_tile(m, tm)
    return _gemm(s3, w3, (m, nn), out_dtype, (m // tm, 1, s),
                 pl.BlockSpec((None, tm, n), lambda i, j, kk: (kk, i, 0)),
                 pl.BlockSpec((None, n, nn), lambda i, j, kk: (kk, 0, 0)),
                 pl.BlockSpec((tm, nn), lambda i, j, kk: (i, 0)),
                 (tm, nn), NN, name)


def mm_rs_dx(dy, w3, name, out_dtype=F32, tm=1024):
    m, nn = dy.shape
    s, n, _ = w3.shape
    tm = _div_tile(m, tm)
    return _gemm(dy, w3, (s, m, n), out_dtype, (m // tm, s, 1),
                 pl.BlockSpec((tm, nn), lambda i, j, kk: (i, 0)),
                 pl.BlockSpec((None, n, nn), lambda i, j, kk: (j, 0, 0)),
                 pl.BlockSpec((None, tm, n), lambda i, j, kk: (j, i, 0)),
                 (tm, n), NT, name)


def mm_rs_dw(s3, dy, name, out_dtype=F32, tm=1024):
    s, m, n = s3.shape
    nn = dy.shape[1]
    tm = _div_tile(m, tm)
    return _gemm(s3, dy, (s, n, nn), out_dtype, (s, 1, m // tm),
                 pl.BlockSpec((None, tm, n), lambda i, j, kk: (i, kk, 0)),
                 pl.BlockSpec((tm, nn), lambda i, j, kk: (kk, 0)),
                 pl.BlockSpec((None, n, nn), lambda i, j, kk: (i, 0, 0)),
                 (n, nn), TN, name)


def _colsum8(v):
    tr, d = v.shape
    return v.reshape(tr // 8, 8, d).sum(axis=0)


def _rstd(v):
    return lax.rsqrt(jnp.mean(v * v, axis=-1, keepdims=True) + EPS)


def _vec_spec(d):
    return pl.BlockSpec((1, d), lambda i: (0, 0))


def _acc_spec(d):
    return pl.BlockSpec((8, d), lambda i: (0, 0))


def pre_fwd(x, g, shift, scale, name):
    t, d = x.shape
    tr = _row_tile(t)

    def body(x_ref, g_ref, sh_ref, sc_ref, h_ref):
        xv = x_ref[...]
        xn = (xv * _rstd(xv)) * g_ref[...]
        h_ref[...] = (xn * (1 + sc_ref[...]) + sh_ref[...]).astype(BF16)

    row = pl.BlockSpec((tr, d), lambda i: (i, 0))
    return _call(body, name=name, out_shape=_sds((t, d), BF16), grid=(t // tr,),
                 in_specs=[row, _vec_spec(d), _vec_spec(d), _vec_spec(d)], out_specs=row,
                 compiler_params=_params("parallel"))(x, g, shift, scale)


def post_fwd(x, y, g, gate, name):
    t, d = x.shape
    tr = _row_tile(t)

    def body(x_ref, y_ref, g_ref, gt_ref, o_ref):
        yv = y_ref[...]
        o_ref[...] = x_ref[...] + gt_ref[...] * ((yv * _rstd(yv)) * g_ref[...])

    row = pl.BlockSpec((tr, d), lambda i: (i, 0))
    return _call(body, name=name, out_shape=_sds((t, d), F32), grid=(t // tr,),
                 in_specs=[row, row, _vec_spec(d), _vec_spec(d)], out_specs=row,
                 compiler_params=_params("parallel"))(x, y, g, gate)


def post_bwd(dxn, y, g, gate, name):
    t, d = y.shape
    tr = _row_tile(t)

    def body(dx_ref, y_ref, g_ref, gt_ref, dy_ref, dgate_ref, dg_ref):
        @pl.when(pl.program_id(0) == 0)
        def _():
            dgate_ref[...] = jnp.zeros_like(dgate_ref)
            dg_ref[...] = jnp.zeros_like(dg_ref)

        dxv, yv, gv = dx_ref[...], y_ref[...], g_ref[...]
        r = _rstd(yv)
        yhat = yv * r
        dgate_ref[...] += _colsum8(dxv * (yhat * gv))
        dyn = gt_ref[...] * dxv
        dg_ref[...] += _colsum8(dyn * yhat)
        dyhat = dyn * gv
        dy = r * (dyhat - yhat * jnp.mean(dyhat * yhat, axis=-1, keepdims=True))
        dy_ref[...] = dy.astype(BF16)

    row = pl.BlockSpec((tr, d), lambda i: (i, 0))
    return _call(body, name=name,
                 out_shape=[_sds((t, d), BF16), _sds((8, d), F32), _sds((8, d), F32)],
                 grid=(t // tr,), in_specs=[row, row, _vec_spec(d), _vec_spec(d)],
                 out_specs=[row, _acc_spec(d), _acc_spec(d)],
                 compiler_params=_params("arbitrary"))(dxn, y, g, gate)


def pre_bwd(dh, x, g, scale, dxn, name):
    t, d = x.shape
    tr = _row_tile(t)

    def body(dh_ref, x_ref, g_ref, sc_ref, dxn_ref, dx_ref, dsh_ref, dsc_ref, dg_ref):
        @pl.when(pl.program_id(0) == 0)
        def _():
            dsh_ref[...] = jnp.zeros_like(dsh_ref)
            dsc_ref[...] = jnp.zeros_like(dsc_ref)
            dg_ref[...] = jnp.zeros_like(dg_ref)

        dhv, xv, gv = dh_ref[...].astype(F32), x_ref[...], g_ref[...]
        r = _rstd(xv)
        xhat = xv * r
        dsh_ref[...] += _colsum8(dhv)
        dsc_ref[...] += _colsum8(dhv * (xhat * gv))
        dxn_ = dhv * (1 + sc_ref[...])
        dg_ref[...] += _colsum8(dxn_ * xhat)
        dxhat = dxn_ * gv
        dx_ref[...] = r * (dxhat - xhat * jnp.mean(dxhat * xhat, axis=-1, keepdims=True)) + dxn_ref[...]

    row = pl.BlockSpec((tr, d), lambda i: (i, 0))
    return _call(body, name=name,
                 out_shape=[_sds((t, d), F32)] + [_sds((8, d), F32)] * 3,
                 grid=(t // tr,), in_specs=[row, row, _vec_spec(d), _vec_spec(d), row],
                 out_specs=[row, _acc_spec(d), _acc_spec(d), _acc_spec(d)],
                 compiler_params=_params("arbitrary"))(dh, x, g, scale, dxn)


def loss_fwd_bwd(y, target, name):
    t, d = y.shape
    tr = _row_tile(t)

    def body(y_ref, t_ref, dy_ref, l_ref):
        @pl.when(pl.program_id(0) == 0)
        def _():
            l_ref[...] = jnp.zeros_like(l_ref)

        err = y_ref[...] - t_ref[...]
        l_ref[...] += _colsum8(err * err)
        dy_ref[...] = err * (1.0 / d)

    row = pl.BlockSpec((tr, d), lambda i: (i, 0))
    return _call(body, name=name, out_shape=[_sds((t, d), F32), _sds((8, d), F32)],
                 grid=(t // tr,), in_specs=[row, row], out_specs=[row, _acc_spec(d)],
                 compiler_params=_params("arbitrary"))(y, target)


def _sigmoid(x):
    return 1.0 / (1.0 + jnp.exp(-x))


def swiglu_fwd(a3, name):
    s, t, n = a3.shape
    half = s // 2
    tr = _row_tile(t)

    def body(g_ref, u_ref, o_ref):
        g = g_ref[...]
        o_ref[...] = ((g * _sigmoid(g)) * u_ref[...]).astype(BF16)

    return _call(body, name=name, out_shape=_sds((half, t, n), BF16), grid=(half, t // tr),
                 in_specs=[pl.BlockSpec((None, tr, n), lambda j, i: (j, i, 0)),
                           pl.BlockSpec((None, tr, n), lambda j, i: (j + half, i, 0))],
                 out_specs=pl.BlockSpec((None, tr, n), lambda j, i: (j, i, 0)),
                 compiler_params=_params("parallel", "parallel"))(a3, a3)


def swiglu_bwd(a3, ds3, name):
    s, t, n = a3.shape
    half = s // 2
    tr = _row_tile(t)

    def body(g_ref, u_ref, ds_ref, o_ref):
        g, u, ds = g_ref[...], u_ref[...], ds_ref[...]
        sig = _sigmoid(g)
        o_ref[0] = (ds * u * (sig * (1 + g * (1 - sig)))).astype(BF16)
        o_ref[1] = (ds * (g * sig)).astype(BF16)

    out = _call(body, name=name, out_shape=_sds((2, half, t, n), BF16), grid=(half, t // tr),
                in_specs=[pl.BlockSpec((None, tr, n), lambda j, i: (j, i, 0)),
                          pl.BlockSpec((None, tr, n), lambda j, i: (j + half, i, 0)),
                          pl.BlockSpec((None, tr, n), lambda j, i: (j, i, 0))],
                out_specs=pl.BlockSpec((2, None, tr, n), lambda j, i: (0, j, i, 0)),
                compiler_params=_params("parallel", "parallel"))(a3, a3, ds3)
    return out.reshape(s, t, n)


def _head_masks():
    lane = lax.broadcasted_iota(jnp.int32, (BLK, BLK), 1)
    m0 = (lane < HEAD_DIM).astype(F32)
    return m0, 1.0 - m0


def _split_hi_lo(v):
    hi = v.astype(BF16)
    lo = (v - hi.astype(F32)).astype(BF16)
    return hi, lo


def _tri_dot(tri, v):
    hi, lo = _split_hi_lo(v)
    return (jnp.dot(tri, hi, preferred_element_type=F32)
            + jnp.dot(tri, lo, preferred_element_type=F32))


def _sb_specs(t):
    nq = t // BLK
    npair = N_HEADS // 2
    q_spec = pl.BlockSpec((BLK, BLK), lambda p, qb: (qb, p))
    k_spec = pl.BlockSpec((t, BLK), lambda p, qb: (0, npair + p))
    v_spec = pl.BlockSpec((t, BLK), lambda p, qb: (0, 2 * npair + p))
    c_spec = pl.BlockSpec((2, nq, 8, BLK), lambda p, qb: (p, 0, 0, qb))
    return nq, npair, q_spec, k_spec, v_spec, c_spec


def sb_fwd(qkv, name):
    t = qkv.shape[0]
    nq, npair, q_spec, k_spec, v_spec, c_spec = _sb_specs(t)

    def body(q_ref, k_ref, v_ref, o_ref, c_ref, oacc):
        qb = pl.program_id(1)
        row = lax.broadcasted_iota(jnp.int32, (BLK, BLK), 0)
        col = lax.broadcasted_iota(jnp.int32, (BLK, BLK), 1)
        tri = (col >= row).astype(BF16)
        m0, m1 = _head_masks()
        qf = q_ref[...].astype(F32)
        qh = [(qf * m0).astype(BF16), (qf * m1).astype(BF16)]
        c_ref[...] = jnp.zeros_like(c_ref)
        oacc[...] = jnp.zeros_like(oacc)

        def step(it, carry):
            kb = qb - it
            ks = pl.multiple_of(kb * BLK, BLK)
            k2 = k_ref[pl.ds(ks, BLK), :]
            v2 = v_ref[pl.ds(ks, BLK), :]
            valid = (kb * BLK + row) < (qb * BLK + col)
            new = []
            for hh in range(2):
                cr = carry[hh]
                z = lax.dot_general(k2, qh[hh], NT, preferred_element_type=F32) * QK_SCALE
                e = jnp.exp(-jnp.abs(z))
                sp = jnp.where(valid, jnp.maximum(z, 0.0) + jnp.log(1.0 + e), 0.0)
                c_ref[hh, kb] = jnp.broadcast_to(cr, (8, BLK))
                rest = _tri_dot(tri, sp) + cr
                a = jnp.where(valid, jnp.exp(z - rest), 0.0)
                oacc[hh] += lax.dot_general(a.astype(BF16), v2, TN, preferred_element_type=F32)
                new.append(cr + jnp.sum(sp, axis=0, keepdims=True))
            return tuple(new)

        zero = jnp.zeros((1, BLK), F32)
        lax.fori_loop(0, qb + 1, step, (zero, zero))
        o_ref[...] = (oacc[0] * m0 + oacc[1] * m1).astype(BF16)

    return _call(
        body, name=name,
        out_shape=[_sds((t, D_MODEL), BF16), _sds((N_HEADS, nq, 8, t), F32)],
        grid=(npair, nq), in_specs=[q_spec, k_spec, v_spec],
        out_specs=[pl.BlockSpec((BLK, BLK), lambda p, qb: (qb, p)), c_spec],
        scratch_shapes=[pltpu.VMEM((2, BLK, BLK), F32)],
        compiler_params=_params("parallel", "arbitrary"),
    )(qkv, qkv, qkv)


def sb_bwd(qkv, do, cmass, name):
    t = qkv.shape[0]
    nq, npair, q_spec, k_spec, v_spec, c_spec = _sb_specs(t)

    def body(q_ref, k_ref, v_ref, do_ref, c_ref, dq_ref, dk_ref, dv_ref, dqacc, dkacc, dvacc):
        qb = pl.program_id(1)

        @pl.when(qb == 0)
        def _():
            dkacc[...] = jnp.zeros_like(dkacc)
            dvacc[...] = jnp.zeros_like(dvacc)

        row = lax.broadcasted_iota(jnp.int32, (BLK, BLK), 0)
        col = lax.broadcasted_iota(jnp.int32, (BLK, BLK), 1)
        tri_suf = (col >= row).astype(BF16)
        tri_pre = (col <= row).astype(BF16)
        m0, m1 = _head_masks()
        qf = q_ref[...].astype(F32)
        dof = do_ref[...].astype(F32)
        qh = [(qf * m0).astype(BF16), (qf * m1).astype(BF16)]
        doh = [(dof * m0).astype(BF16), (dof * m1).astype(BF16)]
        dqacc[...] = jnp.zeros_like(dqacc)

        def step(kb, carry):
            ks = pl.multiple_of(kb * BLK, BLK)
            k2 = k_ref[pl.ds(ks, BLK), :]
            v2 = v_ref[pl.ds(ks, BLK), :]
            valid = (kb * BLK + row) < (qb * BLK + col)
            new = []
            for hh in range(2):
                gc = carry[hh]
                z = lax.dot_general(k2, qh[hh], NT, preferred_element_type=F32) * QK_SCALE
                e = jnp.exp(-jnp.abs(z))
                inv = 1.0 / (1.0 + e)
                sig = jnp.where(z >= 0, inv, e * inv)
                sp = jnp.where(valid, jnp.maximum(z, 0.0) + jnp.log(1.0 + e), 0.0)
                rest = _tri_dot(tri_suf, sp) + c_ref[hh, kb, 0:1, :]
                a = jnp.where(valid, jnp.exp(z - rest), 0.0)
                da = lax.dot_general(v2, doh[hh], NT, preferred_element_type=F32)
                g = da * a
                cum = _tri_dot(tri_pre, g) + gc
                dz = jnp.where(valid, (g - sig * cum) * QK_SCALE, 0.0).astype(BF16)
                dkacc[pl.ds(ks, BLK), :] += jnp.dot(dz, qh[hh], preferred_element_type=F32)
                dqacc[hh] += lax.dot_general(dz, k2, TN, preferred_element_type=F32)
                dvacc[pl.ds(ks, BLK), :] += jnp.dot(a.astype(BF16), doh[hh], preferred_element_type=F32)
                new.append(gc + jnp.sum(g, axis=0, keepdims=True))
            return tuple(new)

        zero = jnp.zeros((1, BLK), F32)
        lax.fori_loop(0, qb + 1, step, (zero, zero))
        dq_ref[...] = (dqacc[0] * m0 + dqacc[1] * m1).astype(BF16)

        @pl.when(qb == nq - 1)
        def _():
            dk_ref[...] = dkacc[...].astype(BF16)
            dv_ref[...] = dvacc[...].astype(BF16)

    col_spec = pl.BlockSpec((t, BLK), lambda p, qb: (0, p))
    blk_spec = pl.BlockSpec((BLK, BLK), lambda p, qb: (qb, p))
    return _call(
        body, name=name,
        out_shape=[_sds((t, D_MODEL), BF16)] * 3,
        grid=(npair, nq), in_specs=[q_spec, k_spec, v_spec, blk_spec, c_spec],
        out_specs=[blk_spec, col_spec, col_spec],
        scratch_shapes=[pltpu.VMEM((2, BLK, BLK), F32), pltpu.VMEM((t, BLK), F32), pltpu.VMEM((t, BLK), F32)],
        compiler_params=_params("parallel", "arbitrary"),
    )(qkv, qkv, qkv, do, cmass)


def _band_static_mask(jj):
    row = lax.broadcasted_iota(jnp.int32, (BLK, BLK), 0)
    col = lax.broadcasted_iota(jnp.int32, (BLK, BLK), 1)
    qc = row // 64
    kc = 2 * jj + col // 64
    return (kc >= qc) & (kc <= qc + 8)


def _band_scores(q_h, k_ref, bias_ref, hh, qb):
    blocks = []
    for jj in range(BAND_BLOCKS):
        kb = qb - (BAND_BLOCKS - 1) + jj
        ks = pl.multiple_of(jnp.maximum(kb, 0) * BLK, BLK)
        k2 = k_ref[pl.ds(ks, BLK), :]
        s = lax.dot_general(q_h, k2, NT, preferred_element_type=F32) * QK_SCALE
        s = s + bias_ref[hh, :, jj * BLK:(jj + 1) * BLK]
        ok = _band_static_mask(jj) & (kb >= 0)
        blocks.append(jnp.where(ok, s, NEG))
    return jnp.concatenate(blocks, axis=1)


def _softmax_rows(s):
    m = jnp.max(s, axis=-1, keepdims=True)
    e = jnp.exp(s - m)
    return e / jnp.sum(e, axis=-1, keepdims=True)


def _band_specs(t):
    npair = N_HEADS // 2
    q_spec = pl.BlockSpec((BLK, BLK), lambda p, qb: (qb, p))
    k_spec = pl.BlockSpec((t, BLK), lambda p, qb: (0, npair + p))
    v_spec = pl.BlockSpec((t, BLK), lambda p, qb: (0, 2 * npair + p))
    b_spec = pl.BlockSpec((2, BLK, BAND_W), lambda p, qb: (p, 0, 0))
    return npair, q_spec, k_spec, v_spec, b_spec


def band_fwd(qkv, bias, name):
    t = qkv.shape[0]
    nq = t // BLK
    npair, q_spec, k_spec, v_spec, b_spec = _band_specs(t)

    def body(q_ref, k_ref, v_ref, b_ref, o_ref):
        qb = pl.program_id(1)
        m0, m1 = _head_masks()
        qf = q_ref[...].astype(F32)
        out = []
        for hh, mh in ((0, m0), (1, m1)):
            q_h = (qf * mh).astype(BF16)
            p = _softmax_rows(_band_scores(q_h, k_ref, b_ref, hh, qb))
            acc = jnp.zeros((BLK, BLK), F32)
            for jj in range(BAND_BLOCKS):
                kb = qb - (BAND_BLOCKS - 1) + jj
                ks = pl.multiple_of(jnp.maximum(kb, 0) * BLK, BLK)
                v2 = v_ref[pl.ds(ks, BLK), :]
                acc += jnp.dot(p[:, jj * BLK:(jj + 1) * BLK].astype(BF16), v2, preferred_element_type=F32)
            out.append(acc * mh)
        o_ref[...] = (out[0] + out[1]).astype(BF16)

    return _call(
        body, name=name, out_shape=_sds((t, D_MODEL), BF16), grid=(npair, nq),
        in_specs=[q_spec, k_spec, v_spec, b_spec],
        out_specs=pl.BlockSpec((BLK, BLK), lambda p, qb: (qb, p)),
        compiler_params=_params("parallel", "parallel"),
    )(qkv, qkv, qkv, bias)


def band_bwd(qkv, do, bias, name):
    t = qkv.shape[0]
    nq = t // BLK
    npair, q_spec, k_spec, v_spec, b_spec = _band_specs(t)

    def body(q_ref, k_ref, v_ref, do_ref, b_ref, dq_ref, dk_ref, dv_ref, db_ref, dkacc, dvacc):
        qb = pl.program_id(1)

        @pl.when(qb == 0)
        def _():
            dkacc[...] = jnp.zeros_like(dkacc)
            dvacc[...] = jnp.zeros_like(dvacc)
            db_ref[...] = jnp.zeros_like(db_ref)

        m0, m1 = _head_masks()
        qf = q_ref[...].astype(F32)
        dof = do_ref[...].astype(F32)
        dq_out = []
        for hh, mh in ((0, m0), (1, m1)):
            q_h = (qf * mh).astype(BF16)
            do_h = (dof * mh).astype(BF16)
            p = _softmax_rows(_band_scores(q_h, k_ref, b_ref, hh, qb))
            dps = []
            for jj in range(BAND_BLOCKS):
                kb = qb - (BAND_BLOCKS - 1) + jj
                ks = pl.multiple_of(jnp.maximum(kb, 0) * BLK, BLK)
                dps.append(lax.dot_general(do_h, v_ref[pl.ds(ks, BLK), :], NT, preferred_element_type=F32))
            dp = jnp.concatenate(dps, axis=1)
            ds = p * (dp - jnp.sum(p * dp, axis=-1, keepdims=True))
            db_ref[hh] += ds
            dqa = jnp.zeros((BLK, BLK), F32)
            for jj in range(BAND_BLOCKS):
                kb = qb - (BAND_BLOCKS - 1) + jj
                ks = pl.multiple_of(jnp.maximum(kb, 0) * BLK, BLK)
                dsb = (ds[:, jj * BLK:(jj + 1) * BLK] * QK_SCALE).astype(BF16)
                pb = p[:, jj * BLK:(jj + 1) * BLK].astype(BF16)
                dqa += jnp.dot(dsb, k_ref[pl.ds(ks, BLK), :], preferred_element_type=F32)
                dkacc[pl.ds(ks, BLK), :] += lax.dot_general(dsb, q_h, TN, preferred_element_type=F32)
                dvacc[pl.ds(ks, BLK), :] += lax.dot_general(pb, do_h, TN, preferred_element_type=F32)
            dq_out.append(dqa * mh)
        dq_ref[...] = (dq_out[0] + dq_out[1]).astype(BF16)

        @pl.when(qb == nq - 1)
        def _():
            dk_ref[...] = dkacc[...].astype(BF16)
            dv_ref[...] = dvacc[...].astype(BF16)

    col_spec = pl.BlockSpec((t, BLK), lambda p, qb: (0, p))
    blk_spec = pl.BlockSpec((BLK, BLK), lambda p, qb: (qb, p))
    return _call(
        body, name=name,
        out_shape=[_sds((t, D_MODEL), BF16)] * 3 + [_sds((N_HEADS, BLK, BAND_W), F32)],
        grid=(npair, nq), in_specs=[q_spec, k_spec, v_spec, blk_spec, b_spec],
        out_specs=[blk_spec, col_spec, col_spec, b_spec],
        scratch_shapes=[pltpu.VMEM((t, BLK), F32), pltpu.VMEM((t, BLK), F32)],
        compiler_params=_params("parallel", "arbitrary"),
    )(qkv, qkv, qkv, do, bias)


def band_bias_window(rel_bias):
    ext = jnp.concatenate(
        [rel_bias[:, 1:2 * REL_CLIP + 1],
         jnp.broadcast_to(rel_bias[:, 2 * REL_CLIP:], (N_HEADS, BAND_W + BLK - 1 - 2 * REL_CLIP))], axis=1)
    hank = jnp.stack([ext[:, i:i + BAND_W] for i in range(BLK)], axis=1)
    return hank[:, :, ::-1]


def band_bias_window_grad(dwin):
    width = BAND_W + BLK
    rev = jnp.pad(dwin[:, :, ::-1], ((0, 0), (0, 0), (0, BLK)))
    skew = rev.reshape(N_HEADS, BLK * width)[:, :BLK * (width - 1)].reshape(N_HEADS, BLK, width - 1)
    dext = jnp.sum(skew, axis=1)
    return jnp.concatenate(
        [jnp.zeros((N_HEADS, 1), F32), dext[:, :2 * REL_CLIP - 1],
         jnp.sum(dext[:, 2 * REL_CLIP - 1:], axis=1, keepdims=True)], axis=1)


SG_GROUPS = 8


def _gelu_parts(x):
    inner = GELU_C0 * (x + GELU_C1 * (x * x * x))
    th = jnp.tanh(inner)
    return th, 0.5 * x * (1.0 + th)


def _sg_gate_mask():
    row = lax.broadcasted_iota(jnp.int32, (BLK, BLK), 0)
    col = lax.broadcasted_iota(jnp.int32, (BLK, BLK), 1)
    return (row // 64) >= (col // 64)


def _sg_forward_parts(a, lng):
    w = a.shape[1] // 2
    th, z = _gelu_parts(a)
    u, v = z[:, :w], z[:, w:]
    mu = jnp.mean(v, axis=-1, keepdims=True)
    xc = v - mu
    rstd = lax.rsqrt(jnp.mean(xc * xc, axis=-1, keepdims=True) + EPS)
    vhat = xc * rstd
    return th, u, vhat, rstd, vhat * lng


def sg_fwd(a, lng, ws, bias_t, name):
    t, w2 = a.shape
    w = w2 // 2
    gc = w // SG_GROUPS

    def body(a_ref, lng_ref, ws_ref, bt_ref, y_ref):
        _, u, _, _, vln = _sg_forward_parts(a_ref[...], lng_ref[...])
        mask = _sg_gate_mask()
        bt = bt_ref[...]
        lane = lax.broadcasted_iota(jnp.int32, (BLK, BLK), 1)
        for g in range(SG_GROUPS):
            wg = jnp.where(mask, ws_ref[g], 0.0).astype(BF16)
            sv = jnp.dot(wg, vln[:, g * gc:(g + 1) * gc].astype(BF16), preferred_element_type=F32)
            bg = jnp.sum(jnp.where(lane == g, bt, 0.0), axis=-1, keepdims=True)
            y_ref[:, g * gc:(g + 1) * gc] = (u[:, g * gc:(g + 1) * gc] * (sv + bg)).astype(BF16)

    return _call(
        body, name=name, out_shape=_sds((t, w), BF16), grid=(t // BLK,),
        in_specs=[pl.BlockSpec((BLK, w2), lambda i: (i, 0)), pl.BlockSpec((1, w), lambda i: (0, 0)),
                  pl.BlockSpec((SG_GROUPS, BLK, BLK), lambda i: (0, 0, 0)),
                  pl.BlockSpec((BLK, BLK), lambda i: (0, 0))],
        out_specs=pl.BlockSpec((BLK, w), lambda i: (i, 0)),
        compiler_params=_params("parallel"),
    )(a, lng, ws, bias_t)


def sg_bwd(a, dy, lng, ws, bias_t, name):
    t, w2 = a.shape
    w = w2 // 2
    gc = w // SG_GROUPS

    def body(a_ref, dy_ref, lng_ref, ws_ref, bt_ref, da_ref, dlng_ref, dws_ref, dbt_ref):
        @pl.when(pl.program_id(0) == 0)
        def _():
            dlng_ref[...] = jnp.zeros_like(dlng_ref)
            dws_ref[...] = jnp.zeros_like(dws_ref)
            dbt_ref[...] = jnp.zeros_like(dbt_ref)

        av, lng = a_ref[...], lng_ref[...]
        th, u, vhat, rstd, vln = _sg_forward_parts(av, lng)
        mask = _sg_gate_mask()
        bt = bt_ref[...]
        lane = lax.broadcasted_iota(jnp.int32, (BLK, BLK), 1)
        dyv = dy_ref[...]
        du_parts, dvln_parts = [], []
        dbt = jnp.zeros((BLK, BLK), F32)
        for g in range(SG_GROUPS):
            sl = slice(g * gc, (g + 1) * gc)
            wg = jnp.where(mask, ws_ref[g], 0.0).astype(BF16)
            vg = vln[:, sl].astype(BF16)
            sv = jnp.dot(wg, vg, preferred_element_type=F32)
            bg = jnp.sum(jnp.where(lane == g, bt, 0.0), axis=-1, keepdims=True)
            dyg = dyv[:, sl]
            du_parts.append(dyg * (sv + bg))
            dsv = dyg * u[:, sl]
            dbt += jnp.where(lane == g, jnp.sum(dsv, axis=-1, keepdims=True), 0.0)
            dsvb = dsv.astype(BF16)
            dws_ref[g] += jnp.where(mask, lax.dot_general(dsvb, vg, NT, preferred_element_type=F32), 0.0)
            dvln_parts.append(lax.dot_general(wg, dsvb, TN, preferred_element_type=F32))
        dbt_ref[...] += dbt
        du = jnp.concatenate(du_parts, axis=1)
        dvln = jnp.concatenate(dvln_parts, axis=1)
        dlng_ref[...] += _colsum8(dvln * vhat)
        dvhat = dvln * lng
        dv = rstd * (dvhat - jnp.mean(dvhat, axis=-1, keepdims=True)
                     - vhat * jnp.mean(dvhat * vhat, axis=-1, keepdims=True))
        dz = jnp.concatenate([du, dv], axis=1)
        dgelu = 0.5 * (1.0 + th) + (0.5 * av) * (1.0 - th * th) * (GELU_C0 * (1.0 + 3.0 * GELU_C1 * (av * av)))
        da_ref[...] = (dz * dgelu).astype(BF16)

    return _call(
        body, name=name,
        out_shape=[_sds((t, w2), BF16), _sds((8, w), F32), _sds((SG_GROUPS, BLK, BLK), F32), _sds((BLK, BLK), F32)],
        grid=(t // BLK,),
        in_specs=[pl.BlockSpec((BLK, w2), lambda i: (i, 0)), pl.BlockSpec((BLK, w), lambda i: (i, 0)),
                  pl.BlockSpec((1, w), lambda i: (0, 0)),
                  pl.BlockSpec((SG_GROUPS, BLK, BLK), lambda i: (0, 0, 0)),
                  pl.BlockSpec((BLK, BLK), lambda i: (0, 0))],
        out_specs=[pl.BlockSpec((BLK, w2), lambda i: (i, 0)), pl.BlockSpec((8, w), lambda i: (0, 0)),
                   pl.BlockSpec((SG_GROUPS, BLK, BLK), lambda i: (0, 0, 0)),
                   pl.BlockSpec((BLK, BLK), lambda i: (0, 0))],
        compiler_params=_params("arbitrary"),
    )(a, dy, lng, ws, bias_t)


def _shift_down(cat, n, tr):
    return pltpu.roll(cat, n, 0)[8:8 + tr]


def _shift_up(cat, n, tr):
    return pltpu.roll(cat, tr + 8 - n, 0)[0:tr]


def conv_fwd(p, cw, name):
    t, d3 = p.shape
    d = d3 // 3
    tr = min(256, t)
    hb = tr // 8

    def body(p_ref, ph_ref, cw_ref, o_ref):
        i = pl.program_id(0)
        pv = p_ref[...]
        y = pv[:, d:2 * d] * pv[:, 2 * d:]
        ph = ph_ref[...]
        yh = jnp.where(i > 0, ph[:, d:2 * d] * ph[:, 2 * d:], 0.0)
        cat = jnp.concatenate([yh, y], axis=0)
        yc = (cw_ref[0:1, :] * _shift_down(cat, 2, tr) + cw_ref[1:2, :] * _shift_down(cat, 1, tr)
              + cw_ref[2:3, :] * y)
        o_ref[...] = (pv[:, :d] * yc).astype(BF16)

    return _call(
        body, name=name, out_shape=_sds((t, d), BF16), grid=(t // tr,),
        in_specs=[pl.BlockSpec((tr, d3), lambda i: (i, 0)),
                  pl.BlockSpec((8, d3), lambda i: (jnp.maximum(i * hb - 1, 0), 0)),
                  pl.BlockSpec((8, d), lambda i: (0, 0))],
        out_specs=pl.BlockSpec((tr, d), lambda i: (i, 0)),
        compiler_params=_params("parallel"),
    )(p, p, cw)


def conv_bwd(p, dz, cw, name):
    t, d3 = p.shape
    d = d3 // 3
    tr = min(256, t)
    hb = tr // 8
    nt = t // tr

    def body(p_ref, ph_ref, pn_ref, dz_ref, dzn_ref, cw_ref, dp_ref, dcw_ref):
        i = pl.program_id(0)

        @pl.when(i == 0)
        def _():
            dcw_ref[...] = jnp.zeros_like(dcw_ref)

        pv = p_ref[...]
        gb, gcv, xt = pv[:, :d], pv[:, d:2 * d], pv[:, 2 * d:]
        y = gcv * xt
        ph = ph_ref[...]
        yh = jnp.where(i > 0, ph[:, d:2 * d] * ph[:, 2 * d:], 0.0)
        cat = jnp.concatenate([yh, y], axis=0)
        y2, y1 = _shift_down(cat, 2, tr), _shift_down(cat, 1, tr)
        w0, w1, w2 = cw_ref[0:1, :], cw_ref[1:2, :], cw_ref[2:3, :]
        yc = w0 * y2 + w1 * y1 + w2 * y
        dzv = dz_ref[...]
        dyc = dzv * gb
        dcw_ref[0] += _colsum8(dyc * y2)
        dcw_ref[1] += _colsum8(dyc * y1)
        dcw_ref[2] += _colsum8(dyc * y)
        dycn = jnp.where(i < nt - 1, dzn_ref[...] * pn_ref[...][:, :d], 0.0)
        catn = jnp.concatenate([dyc, dycn], axis=0)
        dy = w2 * dyc + w1 * _shift_up(catn, 1, tr) + w0 * _shift_up(catn, 2, tr)
        dp_ref[:, :d] = (dzv * yc).astype(BF16)
        dp_ref[:, d:2 * d] = (dy * xt).astype(BF16)
        dp_ref[:, 2 * d:] = (dy * gcv).astype(BF16)

    nxt = lambda i: (jnp.minimum((i + 1) * hb, t // 8 - 1), 0)
    return _call(
        body, name=name, out_shape=[_sds((t, d3), BF16), _sds((3, 8, d), F32)], grid=(nt,),
        in_specs=[pl.BlockSpec((tr, d3), lambda i: (i, 0)),
                  pl.BlockSpec((8, d3), lambda i: (jnp.maximum(i * hb - 1, 0), 0)),
                  pl.BlockSpec((8, d3), nxt),
                  pl.BlockSpec((tr, d), lambda i: (i, 0)),
                  pl.BlockSpec((8, d), nxt),
                  pl.BlockSpec((8, d), lambda i: (0, 0))],
        out_specs=[pl.BlockSpec((tr, d3), lambda i: (i, 0)), pl.BlockSpec((3, 8, d), lambda i: (0, 0, 0))],
        compiler_params=_params("arbitrary"),
    )(p, p, p, dz, dz, cw)


def ada_fwd(c_all, w, b, name):
    nl, d, n = w.shape

    def body(c_ref, w_ref, b_ref, o_ref):
        cv = c_ref[...]
        s = (cv * _sigmoid(cv)).astype(BF16)
        o_ref[...] = jnp.dot(s, w_ref[...].astype(BF16), preferred_element_type=F32) + b_ref[...]

    return _call(
        body, name=name, out_shape=_sds((nl, N_DEV, n), F32), grid=(nl,),
        in_specs=[pl.BlockSpec((N_DEV, d), lambda l: (0, 0)), pl.BlockSpec((None, d, n), lambda l: (l, 0, 0)),
                  pl.BlockSpec((None, 1, n), lambda l: (l, 0, 0))],
        out_specs=pl.BlockSpec((None, N_DEV, n), lambda l: (l, 0, 0)),
        compiler_params=_params("parallel"),
    )(c_all, w, b)


def ada_bwd(c_all, dmod, name):
    nl, _, n = dmod.shape
    d = c_all.shape[1]

    def body(c_ref, dm_ref, o_ref):
        cv = c_ref[...]
        s = (cv * _sigmoid(cv)).astype(BF16)
        o_ref[...] = lax.dot_general(s, dm_ref[...].astype(BF16), TN, preferred_element_type=F32)

    return _call(
        body, name=name, out_shape=_sds((nl, d, n), F32), grid=(nl,),
        in_specs=[pl.BlockSpec((N_DEV, d), lambda l: (0, 0)), pl.BlockSpec((None, N_DEV, n), lambda l: (l, 0, 0))],
        out_specs=pl.BlockSpec((None, d, n), lambda l: (l, 0, 0)),
        compiler_params=_params("parallel"),
    )(c_all, dmod)


def adamw(pieces, w, m, v, name):
    npc, r, c = pieces.shape
    tr = r
    for cand in (1024, 512, 256, 128, 64, 32, 16, 8):
        if r % cand == 0 and cand * c * 4 <= (1 << 20):
            tr = cand
            break

    def body(p_ref, w_ref, m_ref, v_ref, g_ref, d_ref, nm_ref, nv_ref):
        g = p_ref[0].astype(F32)
        for i in range(1, npc):
            g = g + p_ref[i].astype(F32)
        wv = w_ref[...]
        nm = ADAM_B1 * m_ref[...] + (1.0 - ADAM_B1) * g
        nv = ADAM_B2 * v_ref[...] + (1.0 - ADAM_B2) * (g * g)
        m_hat = nm / (1.0 - ADAM_B1 ** ADAM_STEP)
        v_hat = nv / (1.0 - ADAM_B2 ** ADAM_STEP)
        g_ref[...] = g
        d_ref[...] = -ADAM_LR * (m_hat / (jnp.sqrt(v_hat) + ADAM_EPS) + ADAM_WD * wv)
        nm_ref[...] = nm
        nv_ref[...] = nv

    row = pl.BlockSpec((tr, c), lambda i: (i, 0))
    return _call(
        body, name=name, out_shape=[_sds((r, c), F32)] * 4, grid=(r // tr,),
        in_specs=[pl.BlockSpec((npc, tr, c), lambda i: (0, i, 0)), row, row, row],
        out_specs=[row] * 4, compiler_params=_params("parallel"),
    )(pieces, w, m, v)


def sum_pieces(pieces, name):
    npc, r, c = pieces.shape

    def body(p_ref, o_ref):
        g = p_ref[0]
        for i in range(1, npc):
            g = g + p_ref[i]
        o_ref[...] = g

    return _call(body, name=name, out_shape=_sds((r, c), F32),
                 in_specs=[pl.BlockSpec(memory_space=pltpu.VMEM)],
                 out_specs=pl.BlockSpec(memory_space=pltpu.VMEM),
                 compiler_params=pltpu.CompilerParams(vmem_limit_bytes=VMEM_LIMIT))(pieces)


PACK_W = 1024


def _pack(arrs):
    flat = jnp.concatenate([a.reshape(-1).astype(F32) for a in arrs])
    rows = -(-flat.shape[0] // (8 * PACK_W)) * 8
    return jnp.pad(flat, (0, rows * PACK_W - flat.shape[0])).reshape(rows, PACK_W)


def _unpack(slab, shapes):
    flat = slab.reshape(-1)
    out, off = [], 0
    for s in shapes:
        n = 1
        for q in s:
            n *= q
        out.append(flat[off:off + n].reshape(s))
        off += n
    return out


def kernel(x, c, ada_w, ada_b, norm_g, ffn_w_in, ffn_w_out, sb_w_qkv, sb_w_o, sg_w_in, sg_ln_g, sg_w_s, sg_bias, sg_w_out, sc_w_in, sc_conv_w, sc_w_out, cb_w_qkv, cb_rel_bias, cb_w_o, loss_target, m_ada_w, m_ada_b, m_norm_g, m_ffn_w_in, m_ffn_w_out, m_sb_w_qkv, m_sb_w_o, m_sg_w_in, m_sg_ln_g, m_sg_w_s, m_sg_bias, m_sg_w_out, m_sc_w_in, m_sc_conv_w, m_sc_w_out, m_cb_w_qkv, m_cb_rel_bias, m_cb_w_o, v_ada_w, v_ada_b, v_norm_g, v_ffn_w_in, v_ffn_w_out, v_sb_w_qkv, v_sb_w_o, v_sg_w_in, v_sg_ln_g, v_sg_w_s, v_sg_bias, v_sg_w_out, v_sc_w_in, v_sc_conv_w, v_sc_w_out, v_cb_w_qkv, v_cb_rel_bias, v_cb_w_o):
    depth = ada_w.shape[0]
    d = D_MODEL
    xi, yi, ci = lax.axis_index("x"), lax.axis_index("y"), lax.axis_index("c")
    me = 4 * xi + 2 * yi + ci
    x0 = x[0]
    t = x0.shape[0]
    target = loss_target[0]

    c_all = _all_gather([jnp.pad(c, ((0, 7), (0, 0)))], "gather_c")[0][:, 0, :]
    na = ada_w.shape[2]
    b_cols = lax.dynamic_slice_in_dim(ada_b, me * na, na, axis=1)[:, None, :]
    mod_part = ada_fwd(c_all, ada_w, b_cols, "ada_fwd")
    mod_g = _all_gather([mod_part.reshape(depth * N_DEV, na)], "gather_mod")[0]
    mod_g = mod_g.reshape(N_DEV, depth, N_DEV, na)
    mod_me = lax.dynamic_index_in_dim(mod_g, me, axis=2, keepdims=False)
    mod = jnp.transpose(mod_me, (1, 0, 2)).reshape(depth, 6, 1, d)

    ng = _all_gather([norm_g.reshape(depth * 4, d // N_DEV)], "gather_norm_g")[0]
    norm_full = jnp.transpose(ng, (1, 0, 2)).reshape(depth, 4, 1, d)
    small = _all_gather([_pack([sg_ln_g, sc_conv_w])], "gather_small")[0].reshape(N_DEV, -1)
    nl_g = sg_ln_g.shape[1]
    ln_full = small[:, :nl_g].reshape(1, N_DEV * nl_g)
    cwn = sc_conv_w.shape[2]
    cw_sh = small[:, nl_g:nl_g + 3 * cwn].reshape(N_DEV, 3, cwn)
    cw_full = jnp.transpose(cw_sh, (1, 0, 2)).reshape(3, d)
    cw_pad = jnp.pad(cw_full, ((0, 5), (0, 0)))

    bf = lambda a: a.astype(BF16)
    mixers = [
        [bf(sb_w_qkv[0]), bf(sb_w_o[0])],
        [bf(sg_w_in[0]), bf(sg_w_out[0])],
        [bf(sc_w_in[0]), bf(sc_w_out[0])],
        [bf(cb_w_qkv[0]), bf(cb_w_o[0])],
    ]
    gathered = []
    for i in range(depth):
        gathered.append(_all_gather([bf(ffn_w_in[i]), bf(ffn_w_out[i])] + mixers[i % 4], "gather_w%d" % i))

    bias_win = band_bias_window(cb_rel_bias[0])
    ws = sg_w_s[0]
    bias_t = jnp.pad(sg_bias[0].T, ((0, 0), (0, BLK - SG_GROUPS)))

    saved = []
    xcur = x0
    for i in range(depth):
        wfi, wfo, wmi, wmo = gathered[i]
        wfo4 = wfo.reshape(4, -1, d)
        wmo2 = wmo.reshape(-1, d)
        mi = i % 4
        sh_m, sc_m, gt_m, sh_f, sc_f, gt_f = [mod[i, j] for j in range(6)]
        g0, g1, g2, g3 = [norm_full[i, j] for j in range(4)]
        tag = "L%d_" % i
        sv = {"x_in": xcur}
        h = pre_fwd(xcur, g0, sh_m, sc_m, tag + "pre_m")
        sv["h_m"] = h
        if mi == 0:
            qkv = mm_cs(h, wmi, tag + "qkv", out_dtype=BF16)
            o, cmass = sb_fwd(qkv, tag + "sb_fwd")
            sv.update(qkv=qkv, o=o, cmass=cmass)
            y = mm(o, wmo2, tag + "wo")
        elif mi == 1:
            a = mm_cs(h, wmi, tag + "sg_in")
            yy = sg_fwd(a, ln_full, ws, bias_t, tag + "sg_fwd")
            sv.update(a=a, yy=yy)
            y = mm(yy, wmo2, tag + "sg_out")
        elif mi == 2:
            p = mm_cs(h, wmi, tag + "sc_in")
            gz = conv_fwd(p, cw_pad, tag + "conv_fwd")
            sv.update(p=p, gz=gz)
            y = mm(gz, wmo2, tag + "sc_out")
        else:
            qkv = mm_cs(h, wmi, tag + "qkv", out_dtype=BF16)
            o = band_fwd(qkv, bias_win, tag + "band_fwd")
            sv.update(qkv=qkv, o=o)
            y = mm(o, wmo2, tag + "wo")
        sv["y_m"] = y
        xmid = post_fwd(xcur, y, g1, gt_m, tag + "post_m")
        sv["x_mid"] = xmid
        h2 = pre_fwd(xmid, g2, sh_f, sc_f, tag + "pre_f")
        a3 = mm_cs(h2, wfi, tag + "ffn_in", act_major=True)
        s3 = swiglu_fwd(a3, tag + "swiglu")
        y2 = mm_rs(s3, wfo4, tag + "ffn_out")
        sv.update(h_f=h2, a3=a3, s3=s3, y_f=y2)
        xcur = post_fwd(xmid, y2, g3, gt_f, tag + "post_f")
        saved.append(sv)

    dx, lpart = loss_fwd_bwd(xcur, target, "loss")
    loss = lax.psum(0.5 * jnp.sum(lpart) / d, ("x", "y", "c"))

    dmod_rows = [None] * depth
    dnorm_rows = [None] * depth
    big_pieces = [None] * depth
    small_grads = {}
    for i in reversed(range(depth)):
        wfi, wfo, wmi, wmo = gathered[i]
        wfo4 = wfo.reshape(4, -1, d)
        wmo2 = wmo.reshape(-1, d)
        mi = i % 4
        sh_m, sc_m, gt_m, sh_f, sc_f, gt_f = [mod[i, j] for j in range(6)]
        g0, g1, g2, g3 = [norm_full[i, j] for j in range(4)]
        tag = "L%d_b_" % i
        sv = saved[i]
        dy2, dgt_f, dg3 = post_bwd(dx, sv["y_f"], g3, gt_f, tag + "post_f")
        ds3 = mm_rs_dx(dy2, wfo4, tag + "ffn_out_dx")
        dwfo = mm_rs_dw(sv["s3"], dy2, tag + "ffn_out_dw", out_dtype=BF16)
        da3 = swiglu_bwd(sv["a3"], ds3, tag + "swiglu")
        dh2 = mm_cs_dx(da3, wfi, tag + "ffn_in_dx", act_major=True)
        dwfi = mm_cs_dw(sv["h_f"], da3, tag + "ffn_in_dw", act_major=True, out_dtype=BF16)
        dx, dsh_f, dsc_f, dg2 = pre_bwd(dh2, sv["x_mid"], g2, sc_f, dx, tag + "pre_f")
        dy, dgt_m, dg1 = post_bwd(dx, sv["y_m"], g1, gt_m, tag + "post_m")
        if mi == 0:
            do = mm_nt(dy, wmo2, tag + "wo_dx", out_dtype=BF16)
            dwmo = mm_tn(sv["o"], dy, tag + "wo_dw", out_dtype=BF16)
            dq, dk, dv = sb_bwd(sv["qkv"], do, sv["cmass"], tag + "sb_bwd")
            dmid = jnp.concatenate([dq, dk, dv], axis=1)
        elif mi == 1:
            dyy = mm_nt(dy, wmo2, tag + "sg_out_dx")
            dwmo = mm_tn(sv["yy"], dy, tag + "sg_out_dw", out_dtype=BF16)
            dmid, dlng, dws, dbt = sg_bwd(sv["a"], dyy, ln_full, ws, bias_t, tag + "sg_bwd")
            small_grads.update(ln_g=jnp.sum(dlng, axis=0), w_s=dws, bias=dbt[:, :SG_GROUPS].T)
        elif mi == 2:
            dgz = mm_nt(dy, wmo2, tag + "sc_out_dx")
            dwmo = mm_tn(sv["gz"], dy, tag + "sc_out_dw", out_dtype=BF16)
            dmid, dcw = conv_bwd(sv["p"], dgz, cw_pad, tag + "conv_bwd")
            small_grads.update(conv_w=jnp.sum(dcw, axis=1))
        else:
            do = mm_nt(dy, wmo2, tag + "wo_dx", out_dtype=BF16)
            dwmo = mm_tn(sv["o"], dy, tag + "wo_dw", out_dtype=BF16)
            dq, dk, dv, dwin = band_bwd(sv["qkv"], do, bias_win, tag + "band_bwd")
            dmid = jnp.concatenate([dq, dk, dv], axis=1)
            small_grads.update(rel_bias=band_bias_window_grad(dwin))
        dh = mm_cs_dx(dmid, wmi, tag + "mix_in_dx")
        dwmi = mm_cs_dw(sv["h_m"], dmid, tag + "mix_in_dw", out_dtype=BF16)
        dx, dsh_m, dsc_m, dg0 = pre_bwd(dh, sv["x_in"], g0, sc_m, dx, tag + "pre_m")
        dmod_rows[i] = jnp.stack([jnp.sum(q, axis=0) for q in (dsh_m, dsc_m, dgt_m, dsh_f, dsc_f, dgt_f)])
        dnorm_rows[i] = jnp.stack([jnp.sum(q, axis=0) for q in (dg0, dg1, dg2, dg3)])
        big_pieces[i] = _all_to_all(
            [dwfi, dwfo.reshape(N_DEV, -1, d), dwmi, dwmo.reshape(N_DEV, -1, d)], "exchange_w%d" % i)

    grad_x = dx[None]

    dmod_mine = jnp.stack(dmod_rows).reshape(depth, 6 * d)
    dnorm_mine = jnp.stack(dnorm_rows)
    small_list = [dnorm_mine, small_grads["ln_g"], small_grads["w_s"], small_grads["bias"],
                  small_grads["conv_w"], small_grads["rel_bias"]]
    small_shapes = [dmod_mine.shape] + [a.shape for a in small_list]
    slab = _pack([dmod_mine] + small_list)
    slab_g = _all_gather([slab], "gather_small_grads")[0]
    tot = sum_pieces(slab_g, "sum_small_grads")
    g_ada_b_full, g_norm, g_ln, g_ws, g_sbias, g_cw, g_rb = _unpack(tot, small_shapes)
    dmod_all = slab_g.reshape(N_DEV, -1)[:, :depth * 6 * d].reshape(N_DEV, depth, 6 * d)
    dmod_cols = lax.dynamic_slice_in_dim(dmod_all, me * na, na, axis=2)
    g_ada_w = ada_bwd(c_all, jnp.transpose(dmod_cols, (1, 0, 2)), "ada_bwd")

    nsh = d // N_DEV
    g_norm_sh = lax.dynamic_slice_in_dim(g_norm, me * nsh, nsh, axis=2)
    g_ln_sh = lax.dynamic_slice_in_dim(g_ln.reshape(1, -1), me * nl_g, nl_g, axis=1)
    g_cw_sh = lax.dynamic_slice_in_dim(g_cw, me * cwn, cwn, axis=1)[None]

    out_g, out_d, out_m, out_v = {}, {}, {}, {}

    def upd(name, pieces, w, m, v):
        r_c = pieces.shape[1:]
        g, dl, nm, nv = adamw(pieces, w.reshape(r_c), m.reshape(r_c), v.reshape(r_c), "adamw_" + name)
        out_g[name], out_d[name] = g.reshape(w.shape), dl.reshape(w.shape)
        out_m[name], out_v[name] = nm.reshape(w.shape), nv.reshape(w.shape)

    upd("ada_w", g_ada_w.reshape(1, depth * d, na), ada_w, m_ada_w, v_ada_w)
    nfi = ffn_w_in.shape[2]
    nfo = ffn_w_out.shape[1]
    pfi = jnp.concatenate([big_pieces[i][0] for i in range(depth)], axis=1)
    pfo = jnp.concatenate([big_pieces[i][1] for i in range(depth)], axis=1)
    upd("ffn_w_in", pfi, ffn_w_in, m_ffn_w_in, v_ffn_w_in)
    upd("ffn_w_out", pfo, ffn_w_out, m_ffn_w_out, v_ffn_w_out)
    upd("sb_w_qkv", big_pieces[0][2], sb_w_qkv, m_sb_w_qkv, v_sb_w_qkv)
    upd("sb_w_o", big_pieces[0][3], sb_w_o, m_sb_w_o, v_sb_w_o)
    upd("sg_w_in", big_pieces[1][2], sg_w_in, m_sg_w_in, v_sg_w_in)
    upd("sg_w_out", big_pieces[1][3], sg_w_out, m_sg_w_out, v_sg_w_out)
    upd("sc_w_in", big_pieces[2][2], sc_w_in, m_sc_w_in, v_sc_w_in)
    upd("sc_w_out", big_pieces[2][3], sc_w_out, m_sc_w_out, v_sc_w_out)
    upd("cb_w_qkv", big_pieces[3][2], cb_w_qkv, m_cb_w_qkv, v_cb_w_qkv)
    upd("cb_w_o", big_pieces[3][3], cb_w_o, m_cb_w_o, v_cb_w_o)

    small_names = ["ada_b", "norm_g", "sg_ln_g", "sg_w_s", "sg_bias", "sc_conv_w", "cb_rel_bias"]
    small_g = [g_ada_b_full, g_norm_sh, g_ln_sh, g_ws[None], g_sbias[None], g_cw_sh, g_rb[None]]
    small_w = [ada_b, norm_g, sg_ln_g, sg_w_s, sg_bias, sc_conv_w, cb_rel_bias]
    small_m = [m_ada_b, m_norm_g, m_sg_ln_g, m_sg_w_s, m_sg_bias, m_sc_conv_w, m_cb_rel_bias]
    small_v = [v_ada_b, v_norm_g, v_sg_ln_g, v_sg_w_s, v_sg_bias, v_sc_conv_w, v_cb_rel_bias]
    shapes = [w.shape for w in small_w]
    res = adamw(_pack(small_g)[None], _pack(small_w), _pack(small_m), _pack(small_v), "adamw_small")
    for nm_, gs, ds_, ms, vs in zip(small_names, *[_unpack(r, shapes) for r in res]):
        out_g[nm_], out_d[nm_], out_m[nm_], out_v[nm_] = gs, ds_, ms, vs

    order = ["ada_w", "ada_b", "norm_g", "ffn_w_in", "ffn_w_out", "sb_w_qkv", "sb_w_o", "sg_w_in", "sg_ln_g",
             "sg_w_s", "sg_bias", "sg_w_out", "sc_w_in", "sc_conv_w", "sc_w_out", "cb_w_qkv", "cb_rel_bias", "cb_w_o"]
    return (loss, grad_x, *[out_g[n] for n in order], *[out_d[n] for n in order],
            *[out_m[n] for n in order], *[out_v[n] for n in order])
```

```python
import jax
import jax.numpy as jnp
from jax import lax
from jax.experimental import pallas as pl
from jax.experimental.pallas import tpu as pltpu

F32 = jnp.float32
BF16 = jnp.bfloat16
MESH = pl.DeviceIdType.MESH

N_DEV = 8
D_MODEL = 1024
N_HEADS = 16
HEAD_DIM = 64
QK_SCALE = HEAD_DIM ** -0.5
BLK = 128
BAND_BLOCKS = 5
BAND_W = BAND_BLOCKS * BLK
REL_CLIP = 128
EPS = 1e-6
NEG = -1e30
GELU_C0 = 0.7978845608028654
GELU_C1 = 0.044715
ADAM_LR = 0.001
ADAM_B1 = 0.9
ADAM_B2 = 0.999
ADAM_EPS = 1e-08
ADAM_WD = 0.01
ADAM_STEP = 10
VMEM_LIMIT = 56 * 1024 * 1024


def _call(body, **kw):
    return pl.pallas_call(body, **kw)


def _params(*sem):
    return pltpu.CompilerParams(dimension_semantics=sem, vmem_limit_bytes=VMEM_LIMIT)


def _sds(shape, dtype):
    return jax.ShapeDtypeStruct(tuple(shape), dtype)


def _row_tile(t):
    return min(512, t)


def _me():
    x, y, c = lax.axis_index("x"), lax.axis_index("y"), lax.axis_index("c")
    return x, y, c


def _all_gather(arrs, name):
    n = len(arrs)

    def body(*refs):
        gather = _Gather(refs[:n], refs[n:2 * n], *refs[2 * n:])
        gather.start()
        gather.forward()
        gather.finish()

    any_spec = pl.BlockSpec(memory_space=pl.ANY)
    outs = _call(
        body,
        name=name,
        out_shape=_Gather.out_shapes(arrs),
        in_specs=[any_spec] * n,
        out_specs=[any_spec] * n,
        scratch_shapes=_comm_sems(n),
    )(*arrs)
    return list(outs)


def _comm_sems(n):
    if n == 0:
        return []
    return [pltpu.SemaphoreType.DMA((n, 7)), pltpu.SemaphoreType.DMA((n, 7)), pltpu.SemaphoreType.DMA((n,))]


class _Gather:
    def __init__(self, x_refs, o_refs, send_sems, recv_sems, local_sems):
        self.x_refs, self.o_refs = x_refs, o_refs
        self.send_sems, self.recv_sems, self.local_sems = send_sems, recv_sems, local_sems
        x, y, c = _me()
        self.c = c
        self.me, self.sibling = (x, y, c), (x, y, 1 - c)
        self.chips = [(1 - x, y), (x, 1 - y), (1 - x, 1 - y)]

    @staticmethod
    def out_shapes(arrs):
        return [_sds((N_DEV,) + a.shape, a.dtype) for a in arrs]

    def rows(self, a, block):
        px, py, pc = block
        return self.o_refs[a].at[4 * px + 2 * py + pc]

    def copy(self, a, k, block, to, own=False):
        return pltpu.make_async_remote_copy(
            src_ref=self.x_refs[a] if own else self.rows(a, block),
            dst_ref=self.rows(a, block),
            send_sem=self.send_sems.at[a, k],
            recv_sem=self.recv_sems.at[a, k],
            device_id=to,
            device_id_type=MESH,
        )

    def local(self, a):
        return pltpu.make_async_copy(self.x_refs[a], self.rows(a, self.me), self.local_sems.at[a])

    def first(self, a):
        cps = [self.copy(a, 0, self.me, self.sibling, own=True)]
        return cps + [self.copy(a, 1 + j, self.me, (*chip, self.c), own=True) for j, chip in enumerate(self.chips)]

    def passed(self, a):
        return [self.copy(a, 4 + j, (*chip, self.c), self.sibling) for j, chip in enumerate(self.chips)]

    def start(self):
        for a in range(len(self.x_refs)):
            self.local(a).start()
            for cp in self.first(a):
                cp.start()

    def forward(self):
        for a in range(len(self.x_refs)):
            passed = self.passed(a)
            for j, chip in enumerate(self.chips):
                self.copy(a, 1 + j, (*chip, self.c), self.me).wait_recv()
                passed[j].start()

    def finish(self):
        for a in range(len(self.x_refs)):
            self.copy(a, 0, self.sibling, self.me).wait_recv()
            for j, chip in enumerate(self.chips):
                self.copy(a, 4 + j, (*chip, 1 - self.c), self.me).wait_recv()
            for cp in self.first(a) + self.passed(a):
                cp.wait_send()
            self.local(a).wait()


class _Exchange:
    def __init__(self, x_refs, o_refs, send_sems, recv_sems, local_sems):
        self.x_refs, self.o_refs = x_refs, o_refs
        self.send_sems, self.recv_sems, self.local_sems = send_sems, recv_sems, local_sems
        x, y, c = _me()
        self.me = 4 * x + 2 * y + c
        self.peers = []
        for k in range(1, N_DEV):
            px = 1 - x if k & 4 else x
            py = 1 - y if k & 2 else y
            pc = 1 - c if k & 1 else c
            self.peers.append((px, py, pc))

    def local(self, a):
        return pltpu.make_async_copy(self.x_refs[a].at[self.me], self.o_refs[a].at[self.me], self.local_sems.at[a])

    def copy(self, a, k, send):
        px, py, pc = self.peers[k]
        peer = 4 * px + 2 * py + pc
        return pltpu.make_async_remote_copy(
            src_ref=self.x_refs[a].at[peer],
            dst_ref=self.o_refs[a].at[self.me if send else peer],
            send_sem=self.send_sems.at[a, k], recv_sem=self.recv_sems.at[a, k],
            device_id=(px, py, pc), device_id_type=MESH)

    def start(self):
        for a in range(len(self.x_refs)):
            self.local(a).start()
            for k in range(N_DEV - 1):
                self.copy(a, k, True).start()

    def finish(self):
        for a in range(len(self.x_refs)):
            for k in range(N_DEV - 1):
                self.copy(a, k, False).wait_recv()
            for k in range(N_DEV - 1):
                self.copy(a, k, True).wait_send()
            self.local(a).wait()


def _all_to_all(arrs, name):
    n = len(arrs)

    def body(*refs):
        exchange = _Exchange(refs[:n], refs[n:2 * n], *refs[2 * n:])
        exchange.start()
        exchange.finish()

    any_spec = pl.BlockSpec(memory_space=pl.ANY)
    outs = _call(
        body,
        name=name,
        out_shape=[_sds(a.shape, a.dtype) for a in arrs],
        in_specs=[any_spec] * n,
        out_specs=[any_spec] * n,
        scratch_shapes=_comm_sems(n),
    )(*arrs)
    return list(outs)


NN = (((1,), (0,)), ((), ()))
NT = (((1,), (1,)), ((), ()))
TN = (((0,), (0,)), ((), ()))


def _gemm(a, b, out_shape, out_dtype, grid, a_spec, b_spec, o_spec, acc_shape, dims, name):
    nk = grid[2]

    if nk == 1:
        def body(a_ref, b_ref, o_ref):
            r = lax.dot_general(a_ref[...].astype(BF16), b_ref[...].astype(BF16), dims,
                                preferred_element_type=F32)
            o_ref[...] = r.astype(o_ref.dtype)
        scratch = []
    else:
        def body(a_ref, b_ref, o_ref, acc_ref):
            k = pl.program_id(2)

            @pl.when(k == 0)
            def _():
                acc_ref[...] = jnp.zeros_like(acc_ref)

            acc_ref[...] += lax.dot_general(a_ref[...].astype(BF16), b_ref[...].astype(BF16), dims,
                                            preferred_element_type=F32)

            @pl.when(k == nk - 1)
            def _():
                o_ref[...] = acc_ref[...].astype(o_ref.dtype)
        scratch = [pltpu.VMEM(acc_shape, F32)]

    return _call(
        body, name=name, out_shape=_sds(out_shape, out_dtype), grid=grid,
        in_specs=[a_spec, b_spec], out_specs=o_spec, scratch_shapes=scratch,
        compiler_params=_params("parallel", "parallel", "arbitrary"),
    )(a, b)


def _div_tile(n, want):
    if n <= want:
        return n
    t = want - want % 128
    while n % t:
        t -= 128
    return t


def mm(a, b, name, out_dtype=F32, tm=512, tn=1024, tk=1024):
    m, k = a.shape
    n = b.shape[1]
    tm, tn, tk = _div_tile(m, tm), _div_tile(n, tn), _div_tile(k, tk)
    return _gemm(a, b, (m, n), out_dtype, (m // tm, n // tn, k // tk),
                 pl.BlockSpec((tm, tk), lambda i, j, kk: (i, kk)),
                 pl.BlockSpec((tk, tn), lambda i, j, kk: (kk, j)),
                 pl.BlockSpec((tm, tn), lambda i, j, kk: (i, j)),
                 (tm, tn), NN, name)


def mm_nt(a, b, name, out_dtype=F32, tm=512, tn=1024, tk=1024):
    m, n = a.shape
    k = b.shape[0]
    tm, tk_out, tred = _div_tile(m, tm), _div_tile(k, tn), _div_tile(n, tk)
    return _gemm(a, b, (m, k), out_dtype, (m // tm, k // tk_out, n // tred),
                 pl.BlockSpec((tm, tred), lambda i, j, kk: (i, kk)),
                 pl.BlockSpec((tk_out, tred), lambda i, j, kk: (j, kk)),
                 pl.BlockSpec((tm, tk_out), lambda i, j, kk: (i, j)),
                 (tm, tk_out), NT, name)


def mm_tn(a, b, name, out_dtype=F32, tm=512, tn=1024, tk=1024):
    m, k = a.shape
    n = b.shape[1]
    tk_out, tn, tred = _div_tile(k, tk), _div_tile(n, tn), _div_tile(m, tm)
    return _gemm(a, b, (k, n), out_dtype, (k // tk_out, n // tn, m // tred),
                 pl.BlockSpec((tred, tk_out), lambda i, j, kk: (kk, i)),
                 pl.BlockSpec((tred, tn), lambda i, j, kk: (kk, j)),
                 pl.BlockSpec((tk_out, tn), lambda i, j, kk: (i, j)),
                 (tk_out, tn), TN, name)


def mm_cs(a, wg, name, act_major=False, out_dtype=F32, tm=1024):
    m, k = a.shape
    s, _, n = wg.shape
    tm = _div_tile(m, tm)
    if act_major:
        out_shape, o_spec = (s, m, n), pl.BlockSpec((None, tm, n), lambda i, j, kk: (j, i, 0))
    else:
        out_shape, o_spec = (m, s * n), pl.BlockSpec((tm, n), lambda i, j, kk: (i, j))
    return _gemm(a, wg, out_shape, out_dtype, (m // tm, s, 1),
                 pl.BlockSpec((tm, k), lambda i, j, kk: (i, 0)),
                 pl.BlockSpec((None, k, n), lambda i, j, kk: (j, 0, 0)),
                 o_spec, (tm, n), NN, name)


def mm_cs_dx(da, wg, name, act_major=False, out_dtype=F32, tm=1024):
    s, k, n = wg.shape
    m = da.shape[1] if act_major else da.shape[0]
    tm = _div_tile(m, tm)
    if act_major:
        a_spec = pl.BlockSpec((None, tm, n), lambda i, j, kk: (kk, i, 0))
    else:
        a_spec = pl.BlockSpec((tm, n), lambda i, j, kk: (i, kk))
    return _gemm(da, wg, (m, k), out_dtype, (m // tm, 1, s), a_spec,
                 pl.BlockSpec((None, k, n), lambda i, j, kk: (kk, 0, 0)),
                 pl.BlockSpec((tm, k), lambda i, j, kk: (i, 0)),
                 (tm, k), NT, name)


def mm_cs_dw(a, da, name, act_major=False, out_dtype=F32, tm=1024):
    m, k = a.shape
    if act_major:
        s, _, n = da.shape
    else:
        s, n = N_DEV, da.shape[1] // N_DEV
    tm = _div_tile(m, tm)
    if act_major:
        b_spec = pl.BlockSpec((None, tm, n), lambda i, j, kk: (i, kk, 0))
    else:
        b_spec = pl.BlockSpec((tm, n), lambda i, j, kk: (kk, i))
    return _gemm(a, da, (s, k, n), out_dtype, (s, 1, m // tm),
                 pl.BlockSpec((tm, k), lambda i, j, kk: (kk, 0)), b_spec,
                 pl.BlockSpec((None, k, n), lambda i, j, kk: (i, 0, 0)),
                 (k, n), TN, name)


def mm_rs(s3, w3, name, out_dtype=F32, tm=1024):
    s, m, n = s3.shape
    nn = w3.shape[2]
    tm = _div_tile(m, tm)
    return _gemm(s3, w3, (m, nn), out_dtype, (m // tm, 1, s),
                 pl.BlockSpec((None, tm, n), lambda i, j, kk: (kk, i, 0)),
                 pl.BlockSpec((None, n, nn), lambda i, j, kk: (kk, 0, 0)),
                 pl.BlockSpec((tm, nn), lambda i, j, kk: (i, 0)),
                 (tm, nn), NN, name)


def mm_rs_dx(dy, w3, name, out_dtype=F32, tm=1024):
    m, nn = dy.shape
    s, n, _ = w3.shape
    tm = _div_tile(m, tm)
    return _gemm(dy, w3, (s, m, n), out_dtype, (m // tm, s, 1),
                 pl.BlockSpec((tm, nn), lambda i, j, kk: (i, 0)),
                 pl.BlockSpec((None, n, nn), lambda i, j, kk: (j, 0, 0)),
                 pl.BlockSpec((None, tm, n), lambda i, j, kk: (j, i, 0)),
                 (tm, n), NT, name)


def mm_rs_dw(s3, dy, name, out_dtype=F32, tm=1024):
    s, m, n = s3.shape
    nn = dy.shape[1]
    tm = _div_tile(m, tm)
    return _gemm(s3, dy, (s, n, nn), out_dtype, (s, 1, m // tm),
                 pl.BlockSpec((None, tm, n), lambda i, j, kk: (i, kk, 0)),
                 pl.BlockSpec((tm, nn), lambda i, j, kk: (kk, 0)),
                 pl.BlockSpec((None, n, nn), lambda i, j, kk: (i, 0, 0)),
                 (n, nn), TN, name)


def _colsum8(v):
    tr, d = v.shape
    return v.reshape(tr // 8, 8, d).sum(axis=0)


def _rstd(v):
    return lax.rsqrt(jnp.mean(v * v, axis=-1, keepdims=True) + EPS)


def _vec_spec(d):
    return pl.BlockSpec((1, d), lambda i: (0, 0))


def _acc_spec(d):
    return pl.BlockSpec((8, d), lambda i: (0, 0))


def pre_fwd(x, g, shift, scale, name):
    t, d = x.shape
    tr = _row_tile(t)

    def body(x_ref, g_ref, sh_ref, sc_ref, h_ref):
        xv = x_ref[...]
        xn = (xv * _rstd(xv)) * g_ref[...]
        h_ref[...] = (xn * (1 + sc_ref[...]) + sh_ref[...]).astype(BF16)

    row = pl.BlockSpec((tr, d), lambda i: (i, 0))
    return _call(body, name=name, out_shape=_sds((t, d), BF16), grid=(t // tr,),
                 in_specs=[row, _vec_spec(d), _vec_spec(d), _vec_spec(d)], out_specs=row,
                 compiler_params=_params("parallel"))(x, g, shift, scale)


def post_fwd(x, y, g, gate, name):
    t, d = x.shape
    tr = _row_tile(t)

    def body(x_ref, y_ref, g_ref, gt_ref, o_ref):
        yv = y_ref[...]
        o_ref[...] = x_ref[...] + gt_ref[...] * ((yv * _rstd(yv)) * g_ref[...])

    row = pl.BlockSpec((tr, d), lambda i: (i, 0))
    return _call(body, name=name, out_shape=_sds((t, d), F32), grid=(t // tr,),
                 in_specs=[row, row, _vec_spec(d), _vec_spec(d)], out_specs=row,
                 compiler_params=_params("parallel"))(x, y, g, gate)


def post_bwd(dxn, y, g, gate, name):
    t, d = y.shape
    tr = _row_tile(t)

    def body(dx_ref, y_ref, g_ref, gt_ref, dy_ref, dgate_ref, dg_ref):
        @pl.when(pl.program_id(0) == 0)
        def _():
            dgate_ref[...] = jnp.zeros_like(dgate_ref)
            dg_ref[...] = jnp.zeros_like(dg_ref)

        dxv, yv, gv = dx_ref[...], y_ref[...], g_ref[...]
        r = _rstd(yv)
        yhat = yv * r
        dgate_ref[...] += _colsum8(dxv * (yhat * gv))
        dyn = gt_ref[...] * dxv
        dg_ref[...] += _colsum8(dyn * yhat)
        dyhat = dyn * gv
        dy = r * (dyhat - yhat * jnp.mean(dyhat * yhat, axis=-1, keepdims=True))
        dy_ref[...] = dy.astype(BF16)

    row = pl.BlockSpec((tr, d), lambda i: (i, 0))
    return _call(body, name=name,
                 out_shape=[_sds((t, d), BF16), _sds((8, d), F32), _sds((8, d), F32)],
                 grid=(t // tr,), in_specs=[row, row, _vec_spec(d), _vec_spec(d)],
                 out_specs=[row, _acc_spec(d), _acc_spec(d)],
                 compiler_params=_params("arbitrary"))(dxn, y, g, gate)


def pre_bwd(dh, x, g, scale, dxn, name):
    t, d = x.shape
    tr = _row_tile(t)

    def body(dh_ref, x_ref, g_ref, sc_ref, dxn_ref, dx_ref, dsh_ref, dsc_ref, dg_ref):
        @pl.when(pl.program_id(0) == 0)
        def _():
            dsh_ref[...] = jnp.zeros_like(dsh_ref)
            dsc_ref[...] = jnp.zeros_like(dsc_ref)
            dg_ref[...] = jnp.zeros_like(dg_ref)

        dhv, xv, gv = dh_ref[...].astype(F32), x_ref[...], g_ref[...]
        r = _rstd(xv)
        xhat = xv * r
        dsh_ref[...] += _colsum8(dhv)
        dsc_ref[...] += _colsum8(dhv * (xhat * gv))
        dxn_ = dhv * (1 + sc_ref[...])
        dg_ref[...] += _colsum8(dxn_ * xhat)
        dxhat = dxn_ * gv
        dx_ref[...] = r * (dxhat - xhat * jnp.mean(dxhat * xhat, axis=-1, keepdims=True)) + dxn_ref[...]

    row = pl.BlockSpec((tr, d), lambda i: (i, 0))
    return _call(body, name=name,
                 out_shape=[_sds((t, d), F32)] + [_sds((8, d), F32)] * 3,
                 grid=(t // tr,), in_specs=[row, row, _vec_spec(d), _vec_spec(d), row],
                 out_specs=[row, _acc_spec(d), _acc_spec(d), _acc_spec(d)],
                 compiler_params=_params("arbitrary"))(dh, x, g, scale, dxn)


def loss_fwd_bwd(y, target, name):
    t, d = y.shape
    tr = _row_tile(t)

    def body(y_ref, t_ref, dy_ref, l_ref):
        @pl.when(pl.program_id(0) == 0)
        def _():
            l_ref[...] = jnp.zeros_like(l_ref)

        err = y_ref[...] - t_ref[...]
        l_ref[...] += _colsum8(err * err)
        dy_ref[...] = err * (1.0 / d)

    row = pl.BlockSpec((tr, d), lambda i: (i, 0))
    return _call(body, name=name, out_shape=[_sds((t, d), F32), _sds((8, d), F32)],
                 grid=(t // tr,), in_specs=[row, row], out_specs=[row, _acc_spec(d)],
                 compiler_params=_params("arbitrary"))(y, target)


def _sigmoid(x):
    return 1.0 / (1.0 + jnp.exp(-x))


def swiglu_fwd(a3, name):
    s, t, n = a3.shape
    half = s // 2
    tr = _row_tile(t)

    def body(g_ref, u_ref, o_ref):
        g = g_ref[...]
        o_ref[...] = ((g * _sigmoid(g)) * u_ref[...]).astype(BF16)

    return _call(body, name=name, out_shape=_sds((half, t, n), BF16), grid=(half, t // tr),
                 in_specs=[pl.BlockSpec((None, tr, n), lambda j, i: (j, i, 0)),
                           pl.BlockSpec((None, tr, n), lambda j, i: (j + half, i, 0))],
                 out_specs=pl.BlockSpec((None, tr, n), lambda j, i: (j, i, 0)),
                 compiler_params=_params("parallel", "parallel"))(a3, a3)


def swiglu_bwd(a3, ds3, name):
    s, t, n = a3.shape
    half = s // 2
    tr = _row_tile(t)

    def body(g_ref, u_ref, ds_ref, o_ref):
        g, u, ds = g_ref[...], u_ref[...], ds_ref[...]
        sig = _sigmoid(g)
        o_ref[0] = (ds * u * (sig * (1 + g * (1 - sig)))).astype(BF16)
        o_ref[1] = (ds * (g * sig)).astype(BF16)

    out = _call(body, name=name, out_shape=_sds((2, half, t, n), BF16), grid=(half, t // tr),
                in_specs=[pl.BlockSpec((None, tr, n), lambda j, i: (j, i, 0)),
                          pl.BlockSpec((None, tr, n), lambda j, i: (j + half, i, 0)),
                          pl.BlockSpec((None, tr, n), lambda j, i: (j, i, 0))],
                out_specs=pl.BlockSpec((2, None, tr, n), lambda j, i: (0, j, i, 0)),
                compiler_params=_params("parallel", "parallel"))(a3, a3, ds3)
    return out.reshape(s, t, n)


def _head_masks():
    lane = lax.broadcasted_iota(jnp.int32, (BLK, BLK), 1)
    m0 = (lane < HEAD_DIM).astype(F32)
    return m0, 1.0 - m0


def _split_hi_lo(v):
    hi = v.astype(BF16)
    lo = (v - hi.astype(F32)).astype(BF16)
    return hi, lo


def _tri_dot(tri, v):
    hi, lo = _split_hi_lo(v)
    return (jnp.dot(tri, hi, preferred_element_type=F32)
            + jnp.dot(tri, lo, preferred_element_type=F32))


SB_G = 4
SB_GW = SB_G * BLK


def _sb_specs(t):
    nq = t // BLK
    npair = N_HEADS // 2
    q_spec = pl.BlockSpec((BLK, BLK), lambda p, qb: (qb, p))
    k_spec = pl.BlockSpec((t, BLK), lambda p, qb: (0, npair + p))
    v_spec = pl.BlockSpec((t, BLK), lambda p, qb: (0, 2 * npair + p))
    kt_spec = pl.BlockSpec((t // SB_GW, BLK, SB_GW), lambda p, qb: (0, npair + p, 0))
    vt_spec = pl.BlockSpec((t // SB_GW, BLK, SB_GW), lambda p, qb: (0, 2 * npair + p, 0))
    c_spec = pl.BlockSpec((None, nq, 8, 2 * BLK), lambda p, qb: (p, 0, 0, qb))
    return nq, npair, q_spec, k_spec, v_spec, kt_spec, vt_spec, c_spec


def _sb_consts():
    row = lax.broadcasted_iota(jnp.int32, (BLK, BLK), 0)
    col = lax.broadcasted_iota(jnp.int32, (BLK, BLK), 1)
    lane0 = (col < HEAD_DIM).astype(F32)
    sub0 = (row < HEAD_DIM).astype(F32)
    return row, col, lane0, sub0


def _sb_two_heads(x_ref, lane0, scale):
    xf = x_ref[...].astype(F32)
    if scale != 1.0:
        xf = xf * scale
    return jnp.concatenate([xf * lane0, xf * (1.0 - lane0)], axis=0).astype(BF16)


def _sb_valid(ks, qb):
    row = lax.broadcasted_iota(jnp.int32, (SB_GW, 2 * BLK), 0)
    col = lax.broadcasted_iota(jnp.int32, (SB_GW, 2 * BLK), 1)
    return (ks + row) < (qb * BLK + (col & (BLK - 1)))


def _blocks_on_lanes(v4):
    return jnp.concatenate([v4[b * BLK:(b + 1) * BLK] for b in range(SB_G)], axis=1)


def _tri2_dot(tri2, v):
    hi, lo = _split_hi_lo(v)
    return jnp.dot(tri2, jnp.concatenate([hi, lo], axis=0), preferred_element_type=F32)


def _sb_pair_loop(first, count, step, group, skip, carry):
    def pair(it, cy):
        g1 = first + 2 * step * it
        cy = group(g1, 0, 1, cy)
        return lax.cond(2 * it + 1 < count, lambda c: group(g1 + step, 1, 0, c), skip, cy)
    return lax.fori_loop(0, (count + 1) // 2, pair, carry)


def sb_fwd(qkv, qkv_t, riders, name):
    t = qkv.shape[0]
    assert t % SB_GW == 0
    nq, npair, q_spec, k_spec, _, _, vt_spec, c_spec = _sb_specs(t)
    nr = len(riders)

    def body(*refs):
        q_ref, k_ref, vt_ref = refs[:3]
        o_ref, c_ref = refs[3 + nr:5 + nr]
        oacc, zbuf0, zbuf1 = refs[5 + 2 * nr:8 + 2 * nr]
        pp = pl.program_id(0)
        qb = pl.program_id(1)
        if nr:
            gather = _Gather(refs[3:3 + nr], refs[5 + nr:5 + 2 * nr], *refs[8 + 2 * nr:])
            pl.when((pp == 0) & (qb == 0))(gather.start)
            pl.when((pp == npair // 2) & (qb == 0))(gather.forward)
        _sb_fwd_step(q_ref, k_ref, vt_ref, o_ref, c_ref, oacc, zbuf0, zbuf1, qb)
        if nr:
            pl.when((pp == npair - 1) & (qb == nq - 1))(gather.finish)

    any_spec = pl.BlockSpec(memory_space=pl.ANY)
    outs = _call(
        body, name=name,
        out_shape=[_sds((t, D_MODEL), BF16), _sds((npair, nq, 8, 2 * t), F32)] + _Gather.out_shapes(riders),
        grid=(npair, nq), in_specs=[q_spec, k_spec, vt_spec] + [any_spec] * nr,
        out_specs=[pl.BlockSpec((BLK, BLK), lambda p, qb: (qb, p)), c_spec] + [any_spec] * nr,
        scratch_shapes=[pltpu.VMEM((BLK, 2 * BLK), F32), pltpu.VMEM((SB_GW, 2 * BLK), F32),
                        pltpu.VMEM((SB_GW, 2 * BLK), F32)] + _comm_sems(nr),
        compiler_params=_params("arbitrary", "arbitrary"),
    )(qkv, qkv, qkv_t, *riders)
    return outs[0], outs[1], list(outs[2:])


def _sb_fwd_step(q_ref, k_ref, vt_ref, o_ref, c_ref, oacc, zbuf0, zbuf1, qb):
    row, col, lane0, sub0 = _sb_consts()
    tri = (col >= row).astype(BF16)
    tri2 = jnp.concatenate([tri, tri], axis=1)
    q2 = _sb_two_heads(q_ref, lane0, QK_SCALE)
    zbufs = (zbuf0, zbuf1)
    c_ref[...] = jnp.zeros_like(c_ref)
    oacc[...] = jnp.zeros_like(oacc)

    def scores(g):
        ks = pl.multiple_of(g * SB_GW, SB_GW)
        return lax.dot_general(k_ref[pl.ds(ks, SB_GW), :], q2, NT, preferred_element_type=F32)

    def group(g, cur, nxt, cr, masked=False):
        z = zbufs[cur][...]
        zbufs[nxt][...] = scores(jnp.maximum(g - 1, 0))
        e = jnp.exp(-jnp.abs(z))
        sp = jnp.maximum(z, 0.0) + jnp.log(1.0 + e)
        if masked:
            valid = _sb_valid(g * SB_GW, qb)
            sp = jnp.where(valid, sp, 0.0)
        loc = _tri2_dot(tri2, _blocks_on_lanes(sp))
        parts = [None] * SB_G
        for b in reversed(range(SB_G)):
            rows = slice(b * BLK, (b + 1) * BLK)
            c_ref[g * SB_G + b] = jnp.broadcast_to(cr, (8, 2 * BLK))
            a = jnp.exp(z[rows] - (loc[:, 2 * b * BLK:2 * (b + 1) * BLK] + cr))
            if masked:
                a = jnp.where(valid[rows], a, 0.0)
            parts[b] = a.astype(BF16)
            cr = cr + jnp.sum(sp[rows], axis=0, keepdims=True)
        oacc[...] += jnp.dot(vt_ref[g], jnp.concatenate(parts, axis=0), preferred_element_type=F32)
        return cr

    last = qb // SB_G
    zbuf1[...] = scores(last)
    cr = group(last, 1, 0, jnp.zeros((1, 2 * BLK), F32), masked=True)
    _sb_pair_loop(last - 1, last, -1, group, lambda c: c, cr)
    o_t = oacc[:, :BLK] * sub0 + oacc[:, BLK:] * (1.0 - sub0)
    o_ref[...] = o_t.T.astype(BF16)


def sb_bwd(qkv, qkv_t, do, cmass, riders, name):
    t = qkv.shape[0]
    nq, npair, q_spec, k_spec, v_spec, kt_spec, _, c_spec = _sb_specs(t)
    nr = len(riders)

    def body(*refs):
        pp = pl.program_id(0)
        qb = pl.program_id(1)
        if nr:
            exchange = _Exchange(refs[6:6 + nr], refs[9 + nr:9 + 2 * nr], *refs[14 + 2 * nr:])
            pl.when((pp == 0) & (qb == 0))(exchange.start)
        step(*refs[:6], *refs[6 + nr:9 + nr], *refs[9 + 2 * nr:14 + 2 * nr])
        if nr:
            pl.when((pp == npair - 1) & (qb == nq - 1))(exchange.finish)

    def step(q_ref, k_ref, kt_ref, v_ref, do_ref, c_ref, dq_ref, dk_ref, dv_ref, dqacc, dkacc, dvacc,
             zbuf0, zbuf1):
        qb = pl.program_id(1)

        @pl.when(qb == 0)
        def _():
            dkacc[...] = jnp.zeros_like(dkacc)
            dvacc[...] = jnp.zeros_like(dvacc)

        row, col, lane0, sub0 = _sb_consts()
        tri_suf = (col >= row).astype(BF16)
        tri_pre = (col <= row).astype(BF16)
        tri2_suf = jnp.concatenate([tri_suf, tri_suf], axis=1)
        tri2_pre = jnp.concatenate([tri_pre, tri_pre], axis=1)
        q2 = _sb_two_heads(q_ref, lane0, QK_SCALE)
        do2 = _sb_two_heads(do_ref, lane0, 1.0)
        zbufs = (zbuf0, zbuf1)
        dqacc[...] = jnp.zeros_like(dqacc)
        last = qb // SB_G

        def scores(g):
            ks = pl.multiple_of(g * SB_GW, SB_GW)
            return lax.dot_general(k_ref[pl.ds(ks, SB_GW), :], q2, NT, preferred_element_type=F32)

        def group(g, cur, nxt, gc, masked=False):
            ks = pl.multiple_of(g * SB_GW, SB_GW)
            z = zbufs[cur][...]
            zbufs[nxt][...] = scores(jnp.minimum(g + 1, last))
            e = jnp.exp(-jnp.abs(z))
            inv = 1.0 / (1.0 + e)
            sig = jnp.where(z >= 0, inv, 1.0 - inv)
            sp = jnp.maximum(z, 0.0) + jnp.log(1.0 + e)
            if masked:
                valid = _sb_valid(ks, qb)
                sp = jnp.where(valid, sp, 0.0)
            loc = _tri2_dot(tri2_suf, _blocks_on_lanes(sp))
            parts = []
            for b in range(SB_G):
                rows = slice(b * BLK, (b + 1) * BLK)
                mass = loc[:, 2 * b * BLK:2 * (b + 1) * BLK] + c_ref[g * SB_G + b, 0:1, :]
                parts.append(jnp.exp(z[rows] - mass))
            a = jnp.concatenate(parts, axis=0)
            if masked:
                a = jnp.where(valid, a, 0.0)
            gr = lax.dot_general(v_ref[pl.ds(ks, SB_GW), :], do2, NT, preferred_element_type=F32) * a
            pre = _tri2_dot(tri2_pre, _blocks_on_lanes(gr))
            parts = []
            for b in range(SB_G):
                rows = slice(b * BLK, (b + 1) * BLK)
                parts.append(pre[:, 2 * b * BLK:2 * (b + 1) * BLK] + gc)
                gc = gc + jnp.sum(gr[rows], axis=0, keepdims=True)
            dz = gr - sig * jnp.concatenate(parts, axis=0)
            if masked:
                dz = jnp.where(valid, dz, 0.0)
            dz = dz.astype(BF16)
            dkacc[pl.ds(ks, SB_GW), :] += jnp.dot(dz, q2, preferred_element_type=F32)
            dqacc[...] += jnp.dot(kt_ref[g], dz, preferred_element_type=F32)
            dvacc[pl.ds(ks, SB_GW), :] += jnp.dot(a.astype(BF16), do2, preferred_element_type=F32)
            return gc

        def skip(gc):
            zbuf0[...] = zbuf1[...]
            return gc

        zbuf0[...] = scores(0)
        gc = _sb_pair_loop(0, last, 1, group, skip, jnp.zeros((1, 2 * BLK), F32))
        group(last, 0, 1, gc, masked=True)
        dq_t = (dqacc[:, :BLK] * sub0 + dqacc[:, BLK:] * (1.0 - sub0)) * QK_SCALE
        dq_ref[...] = dq_t.T.astype(BF16)

        @pl.when(qb == nq - 1)
        def _():
            dk_ref[...] = dkacc[...].astype(BF16)
            dv_ref[...] = dvacc[...].astype(BF16)

    col_spec = pl.BlockSpec((t, BLK), lambda p, qb: (0, p))
    blk_spec = pl.BlockSpec((BLK, BLK), lambda p, qb: (qb, p))
    any_spec = pl.BlockSpec(memory_space=pl.ANY)
    outs = _call(
        body, name=name,
        out_shape=[_sds((t, D_MODEL), BF16)] * 3 + [_sds(r.shape, r.dtype) for r in riders],
        grid=(npair, nq), in_specs=[q_spec, k_spec, kt_spec, v_spec, blk_spec, c_spec] + [any_spec] * nr,
        out_specs=[blk_spec, col_spec, col_spec] + [any_spec] * nr,
        scratch_shapes=[pltpu.VMEM((BLK, 2 * BLK), F32), pltpu.VMEM((t, BLK), F32), pltpu.VMEM((t, BLK), F32),
                        pltpu.VMEM((SB_GW, 2 * BLK), F32), pltpu.VMEM((SB_GW, 2 * BLK), F32)] + _comm_sems(nr),
        compiler_params=_params("arbitrary", "arbitrary"),
    )(qkv, qkv, qkv_t, qkv, do, cmass, *riders)
    return outs[0], outs[1], outs[2], list(outs[3:])


def _band_static_mask(jj):
    row = lax.broadcasted_iota(jnp.int32, (BLK, BLK), 0)
    col = lax.broadcasted_iota(jnp.int32, (BLK, BLK), 1)
    qc = row // 64
    kc = 2 * jj + col // 64
    return (kc >= qc) & (kc <= qc + 8)


def _band_scores(q_h, k_ref, bias_ref, hh, qb):
    blocks = []
    for jj in range(BAND_BLOCKS):
        kb = qb - (BAND_BLOCKS - 1) + jj
        ks = pl.multiple_of(jnp.maximum(kb, 0) * BLK, BLK)
        k2 = k_ref[pl.ds(ks, BLK), :]
        s = lax.dot_general(q_h, k2, NT, preferred_element_type=F32) * QK_SCALE
        s = s + bias_ref[hh, :, jj * BLK:(jj + 1) * BLK]
        ok = _band_static_mask(jj) & (kb >= 0)
        blocks.append(jnp.where(ok, s, NEG))
    return jnp.concatenate(blocks, axis=1)


def _softmax_rows(s):
    m = jnp.max(s, axis=-1, keepdims=True)
    e = jnp.exp(s - m)
    return e / jnp.sum(e, axis=-1, keepdims=True)


def _band_specs(t):
    npair = N_HEADS // 2
    q_spec = pl.BlockSpec((BLK, BLK), lambda p, qb: (qb, p))
    k_spec = pl.BlockSpec((t, BLK), lambda p, qb: (0, npair + p))
    v_spec = pl.BlockSpec((t, BLK), lambda p, qb: (0, 2 * npair + p))
    b_spec = pl.BlockSpec((2, BLK, BAND_W), lambda p, qb: (p, 0, 0))
    return npair, q_spec, k_spec, v_spec, b_spec


def band_fwd(qkv, bias, name):
    t = qkv.shape[0]
    nq = t // BLK
    npair, q_spec, k_spec, v_spec, b_spec = _band_specs(t)

    def body(q_ref, k_ref, v_ref, b_ref, o_ref):
        qb = pl.program_id(1)
        m0, m1 = _head_masks()
        qf = q_ref[...].astype(F32)
        out = []
        for hh, mh in ((0, m0), (1, m1)):
            q_h = (qf * mh).astype(BF16)
            p = _softmax_rows(_band_scores(q_h, k_ref, b_ref, hh, qb))
            acc = jnp.zeros((BLK, BLK), F32)
            for jj in range(BAND_BLOCKS):
                kb = qb - (BAND_BLOCKS - 1) + jj
                ks = pl.multiple_of(jnp.maximum(kb, 0) * BLK, BLK)
                v2 = v_ref[pl.ds(ks, BLK), :]
                acc += jnp.dot(p[:, jj * BLK:(jj + 1) * BLK].astype(BF16), v2, preferred_element_type=F32)
            out.append(acc * mh)
        o_ref[...] = (out[0] + out[1]).astype(BF16)

    return _call(
        body, name=name, out_shape=_sds((t, D_MODEL), BF16), grid=(npair, nq),
        in_specs=[q_spec, k_spec, v_spec, b_spec],
        out_specs=pl.BlockSpec((BLK, BLK), lambda p, qb: (qb, p)),
        compiler_params=_params("parallel", "parallel"),
    )(qkv, qkv, qkv, bias)


def band_bwd(qkv, do, bias, name):
    t = qkv.shape[0]
    nq = t // BLK
    npair, q_spec, k_spec, v_spec, b_spec = _band_specs(t)

    def body(q_ref, k_ref, v_ref, do_ref, b_ref, dq_ref, dk_ref, dv_ref, db_ref, dkacc, dvacc):
        qb = pl.program_id(1)

        @pl.when(qb == 0)
        def _():
            dkacc[...] = jnp.zeros_like(dkacc)
            dvacc[...] = jnp.zeros_like(dvacc)
            db_ref[...] = jnp.zeros_like(db_ref)

        m0, m1 = _head_masks()
        qf = q_ref[...].astype(F32)
        dof = do_ref[...].astype(F32)
        dq_out = []
        for hh, mh in ((0, m0), (1, m1)):
            q_h = (qf * mh).astype(BF16)
            do_h = (dof * mh).astype(BF16)
            p = _softmax_rows(_band_scores(q_h, k_ref, b_ref, hh, qb))
            dps = []
            for jj in range(BAND_BLOCKS):
                kb = qb - (BAND_BLOCKS - 1) + jj
                ks = pl.multiple_of(jnp.maximum(kb, 0) * BLK, BLK)
                dps.append(lax.dot_general(do_h, v_ref[pl.ds(ks, BLK), :], NT, preferred_element_type=F32))
            dp = jnp.concatenate(dps, axis=1)
            ds = p * (dp - jnp.sum(p * dp, axis=-1, keepdims=True))
            db_ref[hh] += ds
            dqa = jnp.zeros((BLK, BLK), F32)
            for jj in range(BAND_BLOCKS):
                kb = qb - (BAND_BLOCKS - 1) + jj
                ks = pl.multiple_of(jnp.maximum(kb, 0) * BLK, BLK)
                dsb = (ds[:, jj * BLK:(jj + 1) * BLK] * QK_SCALE).astype(BF16)
                pb = p[:, jj * BLK:(jj + 1) * BLK].astype(BF16)
                dqa += jnp.dot(dsb, k_ref[pl.ds(ks, BLK), :], preferred_element_type=F32)
                dkacc[pl.ds(ks, BLK), :] += lax.dot_general(dsb, q_h, TN, preferred_element_type=F32)
                dvacc[pl.ds(ks, BLK), :] += lax.dot_general(pb, do_h, TN, preferred_element_type=F32)
            dq_out.append(dqa * mh)
        dq_ref[...] = (dq_out[0] + dq_out[1]).astype(BF16)

        @pl.when(qb == nq - 1)
        def _():
            dk_ref[...] = dkacc[...].astype(BF16)
            dv_ref[...] = dvacc[...].astype(BF16)

    col_spec = pl.BlockSpec((t, BLK), lambda p, qb: (0, p))
    blk_spec = pl.BlockSpec((BLK, BLK), lambda p, qb: (qb, p))
    return _call(
        body, name=name,
        out_shape=[_sds((t, D_MODEL), BF16)] * 3 + [_sds((N_HEADS, BLK, BAND_W), F32)],
        grid=(npair, nq), in_specs=[q_spec, k_spec, v_spec, blk_spec, b_spec],
        out_specs=[blk_spec, col_spec, col_spec, b_spec],
        scratch_shapes=[pltpu.VMEM((t, BLK), F32), pltpu.VMEM((t, BLK), F32)],
        compiler_params=_params("parallel", "arbitrary"),
    )(qkv, qkv, qkv, do, bias)


def band_bias_window(rel_bias):
    ext = jnp.concatenate(
        [rel_bias[:, 1:2 * REL_CLIP + 1],
         jnp.broadcast_to(rel_bias[:, 2 * REL_CLIP:], (N_HEADS, BAND_W + BLK - 1 - 2 * REL_CLIP))], axis=1)
    hank = jnp.stack([ext[:, i:i + BAND_W] for i in range(BLK)], axis=1)
    return hank[:, :, ::-1]


def band_bias_window_grad(dwin):
    width = BAND_W + BLK
    rev = jnp.pad(dwin[:, :, ::-1], ((0, 0), (0, 0), (0, BLK)))
    skew = rev.reshape(N_HEADS, BLK * width)[:, :BLK * (width - 1)].reshape(N_HEADS, BLK, width - 1)
    dext = jnp.sum(skew, axis=1)
    return jnp.concatenate(
        [jnp.zeros((N_HEADS, 1), F32), dext[:, :2 * REL_CLIP - 1],
         jnp.sum(dext[:, 2 * REL_CLIP - 1:], axis=1, keepdims=True)], axis=1)


SG_GROUPS = 8


def _gelu_parts(x):
    inner = GELU_C0 * (x + GELU_C1 * (x * x * x))
    th = jnp.tanh(inner)
    return th, 0.5 * x * (1.0 + th)


def _sg_gate_mask():
    row = lax.broadcasted_iota(jnp.int32, (BLK, BLK), 0)
    col = lax.broadcasted_iota(jnp.int32, (BLK, BLK), 1)
    return (row // 64) >= (col // 64)


def _sg_forward_parts(a, lng):
    w = a.shape[1] // 2
    th, z = _gelu_parts(a)
    u, v = z[:, :w], z[:, w:]
    mu = jnp.mean(v, axis=-1, keepdims=True)
    xc = v - mu
    rstd = lax.rsqrt(jnp.mean(xc * xc, axis=-1, keepdims=True) + EPS)
    vhat = xc * rstd
    return th, u, vhat, rstd, vhat * lng


def sg_fwd(a, lng, ws, bias_t, name):
    t, w2 = a.shape
    w = w2 // 2
    gc = w // SG_GROUPS

    def body(a_ref, lng_ref, ws_ref, bt_ref, y_ref):
        _, u, _, _, vln = _sg_forward_parts(a_ref[...], lng_ref[...])
        mask = _sg_gate_mask()
        bt = bt_ref[...]
        lane = lax.broadcasted_iota(jnp.int32, (BLK, BLK), 1)
        for g in range(SG_GROUPS):
            wg = jnp.where(mask, ws_ref[g], 0.0).astype(BF16)
            sv = jnp.dot(wg, vln[:, g * gc:(g + 1) * gc].astype(BF16), preferred_element_type=F32)
            bg = jnp.sum(jnp.where(lane == g, bt, 0.0), axis=-1, keepdims=True)
            y_ref[:, g * gc:(g + 1) * gc] = (u[:, g * gc:(g + 1) * gc] * (sv + bg)).astype(BF16)

    return _call(
        body, name=name, out_shape=_sds((t, w), BF16), grid=(t // BLK,),
        in_specs=[pl.BlockSpec((BLK, w2), lambda i: (i, 0)), pl.BlockSpec((1, w), lambda i: (0, 0)),
                  pl.BlockSpec((SG_GROUPS, BLK, BLK), lambda i: (0, 0, 0)),
                  pl.BlockSpec((BLK, BLK), lambda i: (0, 0))],
        out_specs=pl.BlockSpec((BLK, w), lambda i: (i, 0)),
        compiler_params=_params("parallel"),
    )(a, lng, ws, bias_t)


def sg_bwd(a, dy, lng, ws, bias_t, name):
    t, w2 = a.shape
    w = w2 // 2
    gc = w // SG_GROUPS

    def body(a_ref, dy_ref, lng_ref, ws_ref, bt_ref, da_ref, dlng_ref, dws_ref, dbt_ref):
        @pl.when(pl.program_id(0) == 0)
        def _():
            dlng_ref[...] = jnp.zeros_like(dlng_ref)
            dws_ref[...] = jnp.zeros_like(dws_ref)
            dbt_ref[...] = jnp.zeros_like(dbt_ref)

        av, lng = a_ref[...], lng_ref[...]
        th, u, vhat, rstd, vln = _sg_forward_parts(av, lng)
        mask = _sg_gate_mask()
        bt = bt_ref[...]
        lane = lax.broadcasted_iota(jnp.int32, (BLK, BLK), 1)
        dyv = dy_ref[...]
        du_parts, dvln_parts = [], []
        dbt = jnp.zeros((BLK, BLK), F32)
        for g in range(SG_GROUPS):
            sl = slice(g * gc, (g + 1) * gc)
            wg = jnp.where(mask, ws_ref[g], 0.0).astype(BF16)
            vg = vln[:, sl].astype(BF16)
            sv = jnp.dot(wg, vg, preferred_element_type=F32)
            bg = jnp.sum(jnp.where(lane == g, bt, 0.0), axis=-1, keepdims=True)
            dyg = dyv[:, sl]
            du_parts.append(dyg * (sv + bg))
            dsv = dyg * u[:, sl]
            dbt += jnp.where(lane == g, jnp.sum(dsv, axis=-1, keepdims=True), 0.0)
            dsvb = dsv.astype(BF16)
            dws_ref[g] += jnp.where(mask, lax.dot_general(dsvb, vg, NT, preferred_element_type=F32), 0.0)
            dvln_parts.append(lax.dot_general(wg, dsvb, TN, preferred_element_type=F32))
        dbt_ref[...] += dbt
        du = jnp.concatenate(du_parts, axis=1)
        dvln = jnp.concatenate(dvln_parts, axis=1)
        dlng_ref[...] += _colsum8(dvln * vhat)
        dvhat = dvln * lng
        dv = rstd * (dvhat - jnp.mean(dvhat, axis=-1, keepdims=True)
                     - vhat * jnp.mean(dvhat * vhat, axis=-1, keepdims=True))
        dz = jnp.concatenate([du, dv], axis=1)
        dgelu = 0.5 * (1.0 + th) + (0.5 * av) * (1.0 - th * th) * (GELU_C0 * (1.0 + 3.0 * GELU_C1 * (av * av)))
        da_ref[...] = (dz * dgelu).astype(BF16)

    return _call(
        body, name=name,
        out_shape=[_sds((t, w2), BF16), _sds((8, w), F32), _sds((SG_GROUPS, BLK, BLK), F32), _sds((BLK, BLK), F32)],
        grid=(t // BLK,),
        in_specs=[pl.BlockSpec((BLK, w2), lambda i: (i, 0)), pl.BlockSpec((BLK, w), lambda i: (i, 0)),
                  pl.BlockSpec((1, w), lambda i: (0, 0)),
                  pl.BlockSpec((SG_GROUPS, BLK, BLK), lambda i: (0, 0, 0)),
                  pl.BlockSpec((BLK, BLK), lambda i: (0, 0))],
        out_specs=[pl.BlockSpec((BLK, w2), lambda i: (i, 0)), pl.BlockSpec((8, w), lambda i: (0, 0)),
                   pl.BlockSpec((SG_GROUPS, BLK, BLK), lambda i: (0, 0, 0)),
                   pl.BlockSpec((BLK, BLK), lambda i: (0, 0))],
        compiler_params=_params("arbitrary"),
    )(a, dy, lng, ws, bias_t)


def _shift_down(cat, n, tr):
    return pltpu.roll(cat, n, 0)[8:8 + tr]


def _shift_up(cat, n, tr):
    return pltpu.roll(cat, tr + 8 - n, 0)[0:tr]


def conv_fwd(p, cw, name):
    t, d3 = p.shape
    d = d3 // 3
    tr = min(256, t)
    hb = tr // 8

    def body(p_ref, ph_ref, cw_ref, o_ref):
        i = pl.program_id(0)
        pv = p_ref[...]
        y = pv[:, d:2 * d] * pv[:, 2 * d:]
        ph = ph_ref[...]
        yh = jnp.where(i > 0, ph[:, d:2 * d] * ph[:, 2 * d:], 0.0)
        cat = jnp.concatenate([yh, y], axis=0)
        yc = (cw_ref[0:1, :] * _shift_down(cat, 2, tr) + cw_ref[1:2, :] * _shift_down(cat, 1, tr)
              + cw_ref[2:3, :] * y)
        o_ref[...] = (pv[:, :d] * yc).astype(BF16)

    return _call(
        body, name=name, out_shape=_sds((t, d), BF16), grid=(t // tr,),
        in_specs=[pl.BlockSpec((tr, d3), lambda i: (i, 0)),
                  pl.BlockSpec((8, d3), lambda i: (jnp.maximum(i * hb - 1, 0), 0)),
                  pl.BlockSpec((8, d), lambda i: (0, 0))],
        out_specs=pl.BlockSpec((tr, d), lambda i: (i, 0)),
        compiler_params=_params("parallel"),
    )(p, p, cw)


def conv_bwd(p, dz, cw, name):
    t, d3 = p.shape
    d = d3 // 3
    tr = min(256, t)
    hb = tr // 8
    nt = t // tr

    def body(p_ref, ph_ref, pn_ref, dz_ref, dzn_ref, cw_ref, dp_ref, dcw_ref):
        i = pl.program_id(0)

        @pl.when(i == 0)
        def _():
            dcw_ref[...] = jnp.zeros_like(dcw_ref)

        pv = p_ref[...]
        gb, gcv, xt = pv[:, :d], pv[:, d:2 * d], pv[:, 2 * d:]
        y = gcv * xt
        ph = ph_ref[...]
        yh = jnp.where(i > 0, ph[:, d:2 * d] * ph[:, 2 * d:], 0.0)
        cat = jnp.concatenate([yh, y], axis=0)
        y2, y1 = _shift_down(cat, 2, tr), _shift_down(cat, 1, tr)
        w0, w1, w2 = cw_ref[0:1, :], cw_ref[1:2, :], cw_ref[2:3, :]
        yc = w0 * y2 + w1 * y1 + w2 * y
        dzv = dz_ref[...]
        dyc = dzv * gb
        dcw_ref[0] += _colsum8(dyc * y2)
        dcw_ref[1] += _colsum8(dyc * y1)
        dcw_ref[2] += _colsum8(dyc * y)
        dycn = jnp.where(i < nt - 1, dzn_ref[...] * pn_ref[...][:, :d], 0.0)
        catn = jnp.concatenate([dyc, dycn], axis=0)
        dy = w2 * dyc + w1 * _shift_up(catn, 1, tr) + w0 * _shift_up(catn, 2, tr)
        dp_ref[:, :d] = (dzv * yc).astype(BF16)
        dp_ref[:, d:2 * d] = (dy * xt).astype(BF16)
        dp_ref[:, 2 * d:] = (dy * gcv).astype(BF16)

    nxt = lambda i: (jnp.minimum((i + 1) * hb, t // 8 - 1), 0)
    return _call(
        body, name=name, out_shape=[_sds((t, d3), BF16), _sds((3, 8, d), F32)], grid=(nt,),
        in_specs=[pl.BlockSpec((tr, d3), lambda i: (i, 0)),
                  pl.BlockSpec((8, d3), lambda i: (jnp.maximum(i * hb - 1, 0), 0)),
                  pl.BlockSpec((8, d3), nxt),
                  pl.BlockSpec((tr, d), lambda i: (i, 0)),
                  pl.BlockSpec((8, d), nxt),
                  pl.BlockSpec((8, d), lambda i: (0, 0))],
        out_specs=[pl.BlockSpec((tr, d3), lambda i: (i, 0)), pl.BlockSpec((3, 8, d), lambda i: (0, 0, 0))],
        compiler_params=_params("arbitrary"),
    )(p, p, p, dz, dz, cw)


def ada_fwd(c_all, w, b, name):
    nl, d, n = w.shape

    def body(c_ref, w_ref, b_ref, o_ref):
        cv = c_ref[...]
        s = (cv * _sigmoid(cv)).astype(BF16)
        o_ref[...] = jnp.dot(s, w_ref[...].astype(BF16), preferred_element_type=F32) + b_ref[...]

    return _call(
        body, name=name, out_shape=_sds((nl, N_DEV, n), F32), grid=(nl,),
        in_specs=[pl.BlockSpec((N_DEV, d), lambda l: (0, 0)), pl.BlockSpec((None, d, n), lambda l: (l, 0, 0)),
                  pl.BlockSpec((None, 1, n), lambda l: (l, 0, 0))],
        out_specs=pl.BlockSpec((None, N_DEV, n), lambda l: (l, 0, 0)),
        compiler_params=_params("parallel"),
    )(c_all, w, b)


def ada_bwd(c_all, dmod, name):
    nl, _, n = dmod.shape
    d = c_all.shape[1]

    def body(c_ref, dm_ref, o_ref):
        cv = c_ref[...]
        s = (cv * _sigmoid(cv)).astype(BF16)
        o_ref[...] = lax.dot_general(s, dm_ref[...].astype(BF16), TN, preferred_element_type=F32)

    return _call(
        body, name=name, out_shape=_sds((nl, d, n), F32), grid=(nl,),
        in_specs=[pl.BlockSpec((N_DEV, d), lambda l: (0, 0)), pl.BlockSpec((None, N_DEV, n), lambda l: (l, 0, 0))],
        out_specs=pl.BlockSpec((None, d, n), lambda l: (l, 0, 0)),
        compiler_params=_params("parallel"),
    )(c_all, dmod)


def adamw(pieces, w, m, v, name):
    npc, r, c = pieces.shape
    tr = r
    for cand in (1024, 512, 256, 128, 64, 32, 16, 8):
        if r % cand == 0 and cand * c * 4 <= (1 << 20):
            tr = cand
            break

    def body(p_ref, w_ref, m_ref, v_ref, g_ref, d_ref, nm_ref, nv_ref):
        g = p_ref[0].astype(F32)
        for i in range(1, npc):
            g = g + p_ref[i].astype(F32)
        wv = w_ref[...]
        nm = ADAM_B1 * m_ref[...] + (1.0 - ADAM_B1) * g
        nv = ADAM_B2 * v_ref[...] + (1.0 - ADAM_B2) * (g * g)
        m_hat = nm / (1.0 - ADAM_B1 ** ADAM_STEP)
        v_hat = nv / (1.0 - ADAM_B2 ** ADAM_STEP)
        g_ref[...] = g
        d_ref[...] = -ADAM_LR * (m_hat / (jnp.sqrt(v_hat) + ADAM_EPS) + ADAM_WD * wv)
        nm_ref[...] = nm
        nv_ref[...] = nv

    row = pl.BlockSpec((tr, c), lambda i: (i, 0))
    return _call(
        body, name=name, out_shape=[_sds((r, c), F32)] * 4, grid=(r // tr,),
        in_specs=[pl.BlockSpec((npc, tr, c), lambda i: (0, i, 0)), row, row, row],
        out_specs=[row] * 4, compiler_params=_params("parallel"),
    )(pieces, w, m, v)


def sum_pieces(pieces, name):
    npc, r, c = pieces.shape

    def body(p_ref, o_ref):
        g = p_ref[0]
        for i in range(1, npc):
            g = g + p_ref[i]
        o_ref[...] = g

    return _call(body, name=name, out_shape=_sds((r, c), F32),
                 in_specs=[pl.BlockSpec(memory_space=pltpu.VMEM)],
                 out_specs=pl.BlockSpec(memory_space=pltpu.VMEM),
                 compiler_params=pltpu.CompilerParams(vmem_limit_bytes=VMEM_LIMIT))(pieces)


PACK_W = 1024


def _pack(arrs):
    flat = jnp.concatenate([a.reshape(-1).astype(F32) for a in arrs])
    rows = -(-flat.shape[0] // (8 * PACK_W)) * 8
    return jnp.pad(flat, (0, rows * PACK_W - flat.shape[0])).reshape(rows, PACK_W)


def _unpack(slab, shapes):
    flat = slab.reshape(-1)
    out, off = [], 0
    for s in shapes:
        n = 1
        for q in s:
            n *= q
        out.append(flat[off:off + n].reshape(s))
        off += n
    return out


def kernel(x, c, ada_w, ada_b, norm_g, ffn_w_in, ffn_w_out, sb_w_qkv, sb_w_o, sg_w_in, sg_ln_g, sg_w_s, sg_bias, sg_w_out, sc_w_in, sc_conv_w, sc_w_out, cb_w_qkv, cb_rel_bias, cb_w_o, loss_target, m_ada_w, m_ada_b, m_norm_g, m_ffn_w_in, m_ffn_w_out, m_sb_w_qkv, m_sb_w_o, m_sg_w_in, m_sg_ln_g, m_sg_w_s, m_sg_bias, m_sg_w_out, m_sc_w_in, m_sc_conv_w, m_sc_w_out, m_cb_w_qkv, m_cb_rel_bias, m_cb_w_o, v_ada_w, v_ada_b, v_norm_g, v_ffn_w_in, v_ffn_w_out, v_sb_w_qkv, v_sb_w_o, v_sg_w_in, v_sg_ln_g, v_sg_w_s, v_sg_bias, v_sg_w_out, v_sc_w_in, v_sc_conv_w, v_sc_w_out, v_cb_w_qkv, v_cb_rel_bias, v_cb_w_o):
    depth = ada_w.shape[0]
    d = D_MODEL
    xi, yi, ci = lax.axis_index("x"), lax.axis_index("y"), lax.axis_index("c")
    me = 4 * xi + 2 * yi + ci
    x0 = x[0]
    t = x0.shape[0]
    target = loss_target[0]

    c_all = _all_gather([jnp.pad(c, ((0, 7), (0, 0)))], "gather_c")[0][:, 0, :]
    na = ada_w.shape[2]
    b_cols = lax.dynamic_slice_in_dim(ada_b, me * na, na, axis=1)[:, None, :]
    mod_part = ada_fwd(c_all, ada_w, b_cols, "ada_fwd")
    mod_g = _all_gather([mod_part.reshape(depth * N_DEV, na)], "gather_mod")[0]
    mod_g = mod_g.reshape(N_DEV, depth, N_DEV, na)
    mod_me = lax.dynamic_index_in_dim(mod_g, me, axis=2, keepdims=False)
    mod = jnp.transpose(mod_me, (1, 0, 2)).reshape(depth, 6, 1, d)

    ng = _all_gather([norm_g.reshape(depth * 4, d // N_DEV)], "gather_norm_g")[0]
    norm_full = jnp.transpose(ng, (1, 0, 2)).reshape(depth, 4, 1, d)
    small = _all_gather([_pack([sg_ln_g, sc_conv_w])], "gather_small")[0].reshape(N_DEV, -1)
    nl_g = sg_ln_g.shape[1]
    ln_full = small[:, :nl_g].reshape(1, N_DEV * nl_g)
    cwn = sc_conv_w.shape[2]
    cw_sh = small[:, nl_g:nl_g + 3 * cwn].reshape(N_DEV, 3, cwn)
    cw_full = jnp.transpose(cw_sh, (1, 0, 2)).reshape(3, d)
    cw_pad = jnp.pad(cw_full, ((0, 5), (0, 0)))

    bf = lambda a: a.astype(BF16)
    mixers = [
        [bf(sb_w_qkv[0]), bf(sb_w_o[0])],
        [bf(sg_w_in[0]), bf(sg_w_out[0])],
        [bf(sc_w_in[0]), bf(sc_w_out[0])],
        [bf(cb_w_qkv[0]), bf(cb_w_o[0])],
    ]
    shards = [[bf(ffn_w_in[i]), bf(ffn_w_out[i])] + mixers[i % 4] for i in range(depth)]
    gathered = [None] * depth
    gathered[0] = _all_gather(shards[0], "gather_w0")
    later_shards = [s for i in range(1, depth) for s in shards[i]]

    bias_win = band_bias_window(cb_rel_bias[0])
    ws = sg_w_s[0]
    bias_t = jnp.pad(sg_bias[0].T, ((0, 0), (0, BLK - SG_GROUPS)))

    saved = []
    xcur = x0
    for i in range(depth):
        wfi, wfo, wmi, wmo = gathered[i]
        wfo4 = wfo.reshape(4, -1, d)
        wmo2 = wmo.reshape(-1, d)
        mi = i % 4
        sh_m, sc_m, gt_m, sh_f, sc_f, gt_f = [mod[i, j] for j in range(6)]
        g0, g1, g2, g3 = [norm_full[i, j] for j in range(4)]
        tag = "L%d_" % i
        sv = {"x_in": xcur}
        h = pre_fwd(xcur, g0, sh_m, sc_m, tag + "pre_m")
        sv["h_m"] = h
        if mi == 0:
            qkv = mm_cs(h, wmi, tag + "qkv", out_dtype=BF16)
            qkv_t = jnp.transpose(qkv.reshape(t // SB_GW, SB_GW, 3 * d), (0, 2, 1))
            o, cmass, later = sb_fwd(qkv, qkv_t, later_shards, tag + "sb_fwd")
            for j in range(1, depth):
                gathered[j] = later[4 * (j - 1):4 * j]
            sv.update(qkv=qkv, qkv_t=qkv_t, o=o, cmass=cmass)
            y = mm(o, wmo2, tag + "wo")
        elif mi == 1:
            a = mm_cs(h, wmi, tag + "sg_in")
            yy = sg_fwd(a, ln_full, ws, bias_t, tag + "sg_fwd")
            sv.update(a=a, yy=yy)
            y = mm(yy, wmo2, tag + "sg_out")
        elif mi == 2:
            p = mm_cs(h, wmi, tag + "sc_in")
            gz = conv_fwd(p, cw_pad, tag + "conv_fwd")
            sv.update(p=p, gz=gz)
            y = mm(gz, wmo2, tag + "sc_out")
        else:
            qkv = mm_cs(h, wmi, tag + "qkv", out_dtype=BF16)
            o = band_fwd(qkv, bias_win, tag + "band_fwd")
            sv.update(qkv=qkv, o=o)
            y = mm(o, wmo2, tag + "wo")
        sv["y_m"] = y
        xmid = post_fwd(xcur, y, g1, gt_m, tag + "post_m")
        sv["x_mid"] = xmid
        h2 = pre_fwd(xmid, g2, sh_f, sc_f, tag + "pre_f")
        a3 = mm_cs(h2, wfi, tag + "ffn_in", act_major=True)
        s3 = swiglu_fwd(a3, tag + "swiglu")
        y2 = mm_rs(s3, wfo4, tag + "ffn_out")
        sv.update(h_f=h2, a3=a3, s3=s3, y_f=y2)
        xcur = post_fwd(xmid, y2, g3, gt_f, tag + "post_f")
        saved.append(sv)

    dx, lpart = loss_fwd_bwd(xcur, target, "loss")
    loss = lax.psum(0.5 * jnp.sum(lpart) / d, ("x", "y", "c"))

    dmod_rows = [None] * depth
    dnorm_rows = [None] * depth
    big_pieces = [None] * depth
    small_grads = {}
    for i in reversed(range(depth)):
        wfi, wfo, wmi, wmo = gathered[i]
        wfo4 = wfo.reshape(4, -1, d)
        wmo2 = wmo.reshape(-1, d)
        mi = i % 4
        sh_m, sc_m, gt_m, sh_f, sc_f, gt_f = [mod[i, j] for j in range(6)]
        g0, g1, g2, g3 = [norm_full[i, j] for j in range(4)]
        tag = "L%d_b_" % i
        sv = saved[i]
        dy2, dgt_f, dg3 = post_bwd(dx, sv["y_f"], g3, gt_f, tag + "post_f")
        ds3 = mm_rs_dx(dy2, wfo4, tag + "ffn_out_dx")
        dwfo = mm_rs_dw(sv["s3"], dy2, tag + "ffn_out_dw", out_dtype=BF16)
        da3 = swiglu_bwd(sv["a3"], ds3, tag + "swiglu")
        dh2 = mm_cs_dx(da3, wfi, tag + "ffn_in_dx", act_major=True)
        dwfi = mm_cs_dw(sv["h_f"], da3, tag + "ffn_in_dw", act_major=True, out_dtype=BF16)
        dx, dsh_f, dsc_f, dg2 = pre_bwd(dh2, sv["x_mid"], g2, sc_f, dx, tag + "pre_f")
        dy, dgt_m, dg1 = post_bwd(dx, sv["y_m"], g1, gt_m, tag + "post_m")
        if mi == 0:
            do = mm_nt(dy, wmo2, tag + "wo_dx", out_dtype=BF16)
            dwmo = mm_tn(sv["o"], dy, tag + "wo_dw", out_dtype=BF16)
            riders = [piece for j in range(depth - 1, 0, -1) for piece in big_pieces[j]]
            dq, dk, dv, exchanged = sb_bwd(sv["qkv"], sv["qkv_t"], do, sv["cmass"], riders, tag + "sb_bwd")
            for n, j in enumerate(range(depth - 1, 0, -1)):
                big_pieces[j] = exchanged[4 * n:4 * (n + 1)]
            dmid = jnp.concatenate([dq, dk, dv], axis=1)
        elif mi == 1:
            dyy = mm_nt(dy, wmo2, tag + "sg_out_dx")
            dwmo = mm_tn(sv["yy"], dy, tag + "sg_out_dw", out_dtype=BF16)
            dmid, dlng, dws, dbt = sg_bwd(sv["a"], dyy, ln_full, ws, bias_t, tag + "sg_bwd")
            small_grads.update(ln_g=jnp.sum(dlng, axis=0), w_s=dws, bias=dbt[:, :SG_GROUPS].T)
        elif mi == 2:
            dgz = mm_nt(dy, wmo2, tag + "sc_out_dx")
            dwmo = mm_tn(sv["gz"], dy, tag + "sc_out_dw", out_dtype=BF16)
            dmid, dcw = conv_bwd(sv["p"], dgz, cw_pad, tag + "conv_bwd")
            small_grads.update(conv_w=jnp.sum(dcw, axis=1))
        else:
            do = mm_nt(dy, wmo2, tag + "wo_dx", out_dtype=BF16)
            dwmo = mm_tn(sv["o"], dy, tag + "wo_dw", out_dtype=BF16)
            dq, dk, dv, dwin = band_bwd(sv["qkv"], do, bias_win, tag + "band_bwd")
            dmid = jnp.concatenate([dq, dk, dv], axis=1)
            small_grads.update(rel_bias=band_bias_window_grad(dwin))
        dh = mm_cs_dx(dmid, wmi, tag + "mix_in_dx")
        dwmi = mm_cs_dw(sv["h_m"], dmid, tag + "mix_in_dw", out_dtype=BF16)
        dx, dsh_m, dsc_m, dg0 = pre_bwd(dh, sv["x_in"], g0, sc_m, dx, tag + "pre_m")
        dmod_rows[i] = jnp.stack([jnp.sum(q, axis=0) for q in (dsh_m, dsc_m, dgt_m, dsh_f, dsc_f, dgt_f)])
        dnorm_rows[i] = jnp.stack([jnp.sum(q, axis=0) for q in (dg0, dg1, dg2, dg3)])
        big_pieces[i] = [dwfi, dwfo.reshape(N_DEV, -1, d), dwmi, dwmo.reshape(N_DEV, -1, d)]
    big_pieces[0] = _all_to_all(big_pieces[0], "exchange_w0")

    grad_x = dx[None]

    dmod_mine = jnp.stack(dmod_rows).reshape(depth, 6 * d)
    dnorm_mine = jnp.stack(dnorm_rows)
    small_list = [dnorm_mine, small_grads["ln_g"], small_grads["w_s"], small_grads["bias"],
                  small_grads["conv_w"], small_grads["rel_bias"]]
    small_shapes = [dmod_mine.shape] + [a.shape for a in small_list]
    slab = _pack([dmod_mine] + small_list)
    slab_g = _all_gather([slab], "gather_small_grads")[0]
    tot = sum_pieces(slab_g, "sum_small_grads")
    g_ada_b_full, g_norm, g_ln, g_ws, g_sbias, g_cw, g_rb = _unpack(tot, small_shapes)
    dmod_all = slab_g.reshape(N_DEV, -1)[:, :depth * 6 * d].reshape(N_DEV, depth, 6 * d)
    dmod_cols = lax.dynamic_slice_in_dim(dmod_all, me * na, na, axis=2)
    g_ada_w = ada_bwd(c_all, jnp.transpose(dmod_cols, (1, 0, 2)), "ada_bwd")

    nsh = d // N_DEV
    g_norm_sh = lax.dynamic_slice_in_dim(g_norm, me * nsh, nsh, axis=2)
    g_ln_sh = lax.dynamic_slice_in_dim(g_ln.reshape(1, -1), me * nl_g, nl_g, axis=1)
    g_cw_sh = lax.dynamic_slice_in_dim(g_cw, me * cwn, cwn, axis=1)[None]

    out_g, out_d, out_m, out_v = {}, {}, {}, {}

    def upd(name, pieces, w, m, v):
        r_c = pieces.shape[1:]
        g, dl, nm, nv = adamw(pieces, w.reshape(r_c), m.reshape(r_c), v.reshape(r_c), "adamw_" + name)
        out_g[name], out_d[name] = g.reshape(w.shape), dl.reshape(w.shape)
        out_m[name], out_v[name] = nm.reshape(w.shape), nv.reshape(w.shape)

    upd("ada_w", g_ada_w.reshape(1, depth * d, na), ada_w, m_ada_w, v_ada_w)
    nfi = ffn_w_in.shape[2]
    nfo = ffn_w_out.shape[1]
    pfi = jnp.concatenate([big_pieces[i][0] for i in range(depth)], axis=1)
    pfo = jnp.concatenate([big_pieces[i][1] for i in range(depth)], axis=1)
    upd("ffn_w_in", pfi, ffn_w_in, m_ffn_w_in, v_ffn_w_in)
    upd("ffn_w_out", pfo, ffn_w_out, m_ffn_w_out, v_ffn_w_out)
    upd("sb_w_qkv", big_pieces[0][2], sb_w_qkv, m_sb_w_qkv, v_sb_w_qkv)
    upd("sb_w_o", big_pieces[0][3], sb_w_o, m_sb_w_o, v_sb_w_o)
    upd("sg_w_in", big_pieces[1][2], sg_w_in, m_sg_w_in, v_sg_w_in)
    upd("sg_w_out", big_pieces[1][3], sg_w_out, m_sg_w_out, v_sg_w_out)
    upd("sc_w_in", big_pieces[2][2], sc_w_in, m_sc_w_in, v_sc_w_in)
    upd("sc_w_out", big_pieces[2][3], sc_w_out, m_sc_w_out, v_sc_w_out)
    upd("cb_w_qkv", big_pieces[3][2], cb_w_qkv, m_cb_w_qkv, v_cb_w_qkv)
    upd("cb_w_o", big_pieces[3][3], cb_w_o, m_cb_w_o, v_cb_w_o)

    small_names = ["ada_b", "norm_g", "sg_ln_g", "sg_w_s", "sg_bias", "sc_conv_w", "cb_rel_bias"]
    small_g = [g_ada_b_full, g_norm_sh, g_ln_sh, g_ws[None], g_sbias[None], g_cw_sh, g_rb[None]]
    small_w = [ada_b, norm_g, sg_ln_g, sg_w_s, sg_bias, sc_conv_w, cb_rel_bias]
    small_m = [m_ada_b, m_norm_g, m_sg_ln_g, m_sg_w_s, m_sg_bias, m_sc_conv_w, m_cb_rel_bias]
    small_v = [v_ada_b, v_norm_g, v_sg_ln_g, v_sg_w_s, v_sg_bias, v_sc_conv_w, v_cb_rel_bias]
    shapes = [w.shape for w in small_w]
    res = adamw(_pack(small_g)[None], _pack(small_w), _pack(small_m), _pack(small_v), "adamw_small")
    for nm_, gs, ds_, ms, vs in zip(small_names, *[_unpack(r, shapes) for r in res]):
        out_g[nm_], out_d[nm_], out_m[nm_], out_v[nm_] = gs, ds_, ms, vs

    order = ["ada_w", "ada_b", "norm_g", "ffn_w_in", "ffn_w_out", "sb_w_qkv", "sb_w_o", "sg_w_in", "sg_ln_g",
             "sg_w_s", "sg_bias", "sg_w_out", "sc_w_in", "sc_conv_w", "sc_w_out", "cb_w_qkv", "cb_rel_bias", "cb_w_o"]
    return (loss, grad_x, *[out_g[n] for n in order], *[out_d[n] for n in order],
            *[out_m[n] for n in order], *[out_v[n] for n in order])
```

```python
import jax
import jax.numpy as jnp
from jax import lax
from jax.experimental import pallas as pl
from jax.experimental.pallas import tpu as pltpu

F32 = jnp.float32
BF16 = jnp.bfloat16
MESH = pl.DeviceIdType.MESH

N_DEV = 8
D_MODEL = 1024
N_HEADS = 16
HEAD_DIM = 64
QK_SCALE = HEAD_DIM ** -0.5
BLK = 128
BAND_BLOCKS = 5
BAND_W = BAND_BLOCKS * BLK
REL_CLIP = 128
EPS = 1e-6
NEG = -1e30
GELU_C0 = 0.7978845608028654
GELU_C1 = 0.044715
ADAM_LR = 0.001
ADAM_B1 = 0.9
ADAM_B2 = 0.999
ADAM_EPS = 1e-08
ADAM_WD = 0.01
ADAM_STEP = 10
VMEM_LIMIT = 56 * 1024 * 1024


def _call(body, **kw):
    return pl.pallas_call(body, **kw)


def _params(*sem):
    return pltpu.CompilerParams(dimension_semantics=sem, vmem_limit_bytes=VMEM_LIMIT)


def _sds(shape, dtype):
    return jax.ShapeDtypeStruct(tuple(shape), dtype)


def _row_tile(t):
    return min(512, t)


def _me():
    x, y, c = lax.axis_index("x"), lax.axis_index("y"), lax.axis_index("c")
    return x, y, c


def _all_gather(arrs, name):
    n = len(arrs)

    def body(*refs):
        gather = _Gather(refs[:n], refs[n:2 * n], *refs[2 * n:])
        gather.start()
        gather.forward()
        gather.finish()

    any_spec = pl.BlockSpec(memory_space=pl.ANY)
    outs = _call(
        body,
        name=name,
        out_shape=_Gather.out_shapes(arrs),
        in_specs=[any_spec] * n,
        out_specs=[any_spec] * n,
        scratch_shapes=_comm_sems(n),
    )(*arrs)
    return list(outs)


def _comm_sems(n):
    if n == 0:
        return []
    return [pltpu.SemaphoreType.DMA((n, 7)), pltpu.SemaphoreType.DMA((n, 7)), pltpu.SemaphoreType.DMA((n,))]


class _Gather:
    def __init__(self, x_refs, o_refs, send_sems, recv_sems, local_sems):
        self.x_refs, self.o_refs = x_refs, o_refs
        self.send_sems, self.recv_sems, self.local_sems = send_sems, recv_sems, local_sems
        x, y, c = _me()
        self.c = c
        self.me, self.sibling = (x, y, c), (x, y, 1 - c)
        self.chips = [(1 - x, y), (x, 1 - y), (1 - x, 1 - y)]

    @staticmethod
    def out_shapes(arrs):
        return [_sds((N_DEV,) + a.shape, a.dtype) for a in arrs]

    def rows(self, a, block):
        px, py, pc = block
        return self.o_refs[a].at[4 * px + 2 * py + pc]

    def copy(self, a, k, block, to, own=False):
        return pltpu.make_async_remote_copy(
            src_ref=self.x_refs[a] if own else self.rows(a, block),
            dst_ref=self.rows(a, block),
            send_sem=self.send_sems.at[a, k],
            recv_sem=self.recv_sems.at[a, k],
            device_id=to,
            device_id_type=MESH,
        )

    def local(self, a):
        return pltpu.make_async_copy(self.x_refs[a], self.rows(a, self.me), self.local_sems.at[a])

    def first(self, a):
        cps = [self.copy(a, 0, self.me, self.sibling, own=True)]
        return cps + [self.copy(a, 1 + j, self.me, (*chip, self.c), own=True) for j, chip in enumerate(self.chips)]

    def passed(self, a):
        return [self.copy(a, 4 + j, (*chip, self.c), self.sibling) for j, chip in enumerate(self.chips)]

    def start(self):
        for a in range(len(self.x_refs)):
            self.local(a).start()
            for cp in self.first(a):
                cp.start()

    def forward(self):
        for a in range(len(self.x_refs)):
            passed = self.passed(a)
            for j, chip in enumerate(self.chips):
                self.copy(a, 1 + j, (*chip, self.c), self.me).wait_recv()
                passed[j].start()

    def finish(self):
        for a in range(len(self.x_refs)):
            self.copy(a, 0, self.sibling, self.me).wait_recv()
            for j, chip in enumerate(self.chips):
                self.copy(a, 4 + j, (*chip, 1 - self.c), self.me).wait_recv()
            for cp in self.first(a) + self.passed(a):
                cp.wait_send()
            self.local(a).wait()


class _Exchange:
    def __init__(self, x_refs, o_refs, send_sems, recv_sems, local_sems):
        self.x_refs, self.o_refs = x_refs, o_refs
        self.send_sems, self.recv_sems, self.local_sems = send_sems, recv_sems, local_sems
        x, y, c = _me()
        self.me = 4 * x + 2 * y + c
        self.peers = []
        for k in range(1, N_DEV):
            px = 1 - x if k & 4 else x
            py = 1 - y if k & 2 else y
            pc = 1 - c if k & 1 else c
            self.peers.append((px, py, pc))

    def local(self, a):
        return pltpu.make_async_copy(self.x_refs[a].at[self.me], self.o_refs[a].at[self.me], self.local_sems.at[a])

    def copy(self, a, k, send):
        px, py, pc = self.peers[k]
        peer = 4 * px + 2 * py + pc
        return pltpu.make_async_remote_copy(
            src_ref=self.x_refs[a].at[peer],
            dst_ref=self.o_refs[a].at[self.me if send else peer],
            send_sem=self.send_sems.at[a, k], recv_sem=self.recv_sems.at[a, k],
            device_id=(px, py, pc), device_id_type=MESH)

    def start(self):
        for a in range(len(self.x_refs)):
            self.local(a).start()
            for k in range(N_DEV - 1):
                self.copy(a, k, True).start()

    def finish(self):
        for a in range(len(self.x_refs)):
            for k in range(N_DEV - 1):
                self.copy(a, k, False).wait_recv()
            for k in range(N_DEV - 1):
                self.copy(a, k, True).wait_send()
            self.local(a).wait()


def _all_to_all(arrs, name):
    n = len(arrs)

    def body(*refs):
        exchange = _Exchange(refs[:n], refs[n:2 * n], *refs[2 * n:])
        exchange.start()
        exchange.finish()

    any_spec = pl.BlockSpec(memory_space=pl.ANY)
    outs = _call(
        body,
        name=name,
        out_shape=[_sds(a.shape, a.dtype) for a in arrs],
        in_specs=[any_spec] * n,
        out_specs=[any_spec] * n,
        scratch_shapes=_comm_sems(n),
    )(*arrs)
    return list(outs)


NN = (((1,), (0,)), ((), ()))
NT = (((1,), (1,)), ((), ()))
TN = (((0,), (0,)), ((), ()))


def _gemm(a, b, out_shape, out_dtype, grid, a_spec, b_spec, o_spec, acc_shape, dims, name):
    nk = grid[2]

    if nk == 1:
        def body(a_ref, b_ref, o_ref):
            r = lax.dot_general(a_ref[...].astype(BF16), b_ref[...].astype(BF16), dims,
                                preferred_element_type=F32)
            o_ref[...] = r.astype(o_ref.dtype)
        scratch = []
    else:
        def body(a_ref, b_ref, o_ref, acc_ref):
            k = pl.program_id(2)

            @pl.when(k == 0)
            def _():
                acc_ref[...] = jnp.zeros_like(acc_ref)

            acc_ref[...] += lax.dot_general(a_ref[...].astype(BF16), b_ref[...].astype(BF16), dims,
                                            preferred_element_type=F32)

            @pl.when(k == nk - 1)
            def _():
                o_ref[...] = acc_ref[...].astype(o_ref.dtype)
        scratch = [pltpu.VMEM(acc_shape, F32)]

    return _call(
        body, name=name, out_shape=_sds(out_shape, out_dtype), grid=grid,
        in_specs=[a_spec, b_spec], out_specs=o_spec, scratch_shapes=scratch,
        compiler_params=_params("parallel", "parallel", "arbitrary"),
    )(a, b)


def _div_tile(n, want):
    if n <= want:
        return n
    t = want - want % 128
    while n % t:
        t -= 128
    return t


def mm(a, b, name, out_dtype=F32, tm=512, tn=1024, tk=1024):
    m, k = a.shape
    n = b.shape[1]
    tm, tn, tk = _div_tile(m, tm), _div_tile(n, tn), _div_tile(k, tk)
    return _gemm(a, b, (m, n), out_dtype, (m // tm, n // tn, k // tk),
                 pl.BlockSpec((tm, tk), lambda i, j, kk: (i, kk)),
                 pl.BlockSpec((tk, tn), lambda i, j, kk: (kk, j)),
                 pl.BlockSpec((tm, tn), lambda i, j, kk: (i, j)),
                 (tm, tn), NN, name)


def mm_nt(a, b, name, out_dtype=F32, tm=512, tn=1024, tk=1024):
    m, n = a.shape
    k = b.shape[0]
    tm, tk_out, tred = _div_tile(m, tm), _div_tile(k, tn), _div_tile(n, tk)
    return _gemm(a, b, (m, k), out_dtype, (m // tm, k // tk_out, n // tred),
                 pl.BlockSpec((tm, tred), lambda i, j, kk: (i, kk)),
                 pl.BlockSpec((tk_out, tred), lambda i, j, kk: (j, kk)),
                 pl.BlockSpec((tm, tk_out), lambda i, j, kk: (i, j)),
                 (tm, tk_out), NT, name)


def mm_tn(a, b, name, out_dtype=F32, tm=512, tn=1024, tk=1024):
    m, k = a.shape
    n = b.shape[1]
    tk_out, tn, tred = _div_tile(k, tk), _div_tile(n, tn), _div_tile(m, tm)
    return _gemm(a, b, (k, n), out_dtype, (k // tk_out, n // tn, m // tred),
                 pl.BlockSpec((tred, tk_out), lambda i, j, kk: (kk, i)),
                 pl.BlockSpec((tred, tn), lambda i, j, kk: (kk, j)),
                 pl.BlockSpec((tk_out, tn), lambda i, j, kk: (i, j)),
                 (tk_out, tn), TN, name)


def mm_cs(a, wg, name, act_major=False, out_dtype=F32, tm=1024):
    m, k = a.shape
    s, _, n = wg.shape
    tm = _div_tile(m, tm)
    if act_major:
        out_shape, o_spec = (s, m, n), pl.BlockSpec((None, tm, n), lambda i, j, kk: (j, i, 0))
    else:
        out_shape, o_spec = (m, s * n), pl.BlockSpec((tm, n), lambda i, j, kk: (i, j))
    return _gemm(a, wg, out_shape, out_dtype, (m // tm, s, 1),
                 pl.BlockSpec((tm, k), lambda i, j, kk: (i, 0)),
                 pl.BlockSpec((None, k, n), lambda i, j, kk: (j, 0, 0)),
                 o_spec, (tm, n), NN, name)


def mm_cs_dx(da, wg, name, act_major=False, out_dtype=F32, tm=1024):
    s, k, n = wg.shape
    m = da.shape[1] if act_major else da.shape[0]
    tm = _div_tile(m, tm)
    if act_major:
        a_spec = pl.BlockSpec((None, tm, n), lambda i, j, kk: (kk, i, 0))
    else:
        a_spec = pl.BlockSpec((tm, n), lambda i, j, kk: (i, kk))
    return _gemm(da, wg, (m, k), out_dtype, (m // tm, 1, s), a_spec,
                 pl.BlockSpec((None, k, n), lambda i, j, kk: (kk, 0, 0)),
                 pl.BlockSpec((tm, k), lambda i, j, kk: (i, 0)),
                 (tm, k), NT, name)


def mm_cs_dw(a, da, name, act_major=False, out_dtype=F32, tm=1024):
    m, k = a.shape
    if act_major:
        s, _, n = da.shape
    else:
        s, n = N_DEV, da.shape[1] // N_DEV
    tm = _div_tile(m, tm)
    if act_major:
        b_spec = pl.BlockSpec((None, tm, n), lambda i, j, kk: (i, kk, 0))
    else:
        b_spec = pl.BlockSpec((tm, n), lambda i, j, kk: (kk, i))
    return _gemm(a, da, (s, k, n), out_dtype, (s, 1, m // tm),
                 pl.BlockSpec((tm, k), lambda i, j, kk: (kk, 0)), b_spec,
                 pl.BlockSpec((None, k, n), lambda i, j, kk: (i, 0, 0)),
                 (k, n), TN, name)


def mm_rs(s3, w3, name, out_dtype=F32, tm=1024):
    s, m, n = s3.shape
    nn = w3.shape[2]
    tm = _div_tile(m, tm)
    return _gemm(s3, w3, (m, nn), out_dtype, (m // tm, 1, s),
                 pl.BlockSpec((None, tm, n), lambda i, j, kk: (kk, i, 0)),
                 pl.BlockSpec((None, n, nn), lambda i, j, kk: (kk, 0, 0)),
                 pl.BlockSpec((tm, nn), lambda i, j, kk: (i, 0)),
                 (tm, nn), NN, name)


def mm_rs_dx(dy, w3, name, out_dtype=F32, tm=1024):
    m, nn = dy.shape
    s, n, _ = w3.shape
    tm = _div_tile(m, tm)
    return _gemm(dy, w3, (s, m, n), out_dtype, (m // tm, s, 1),
                 pl.BlockSpec((tm, nn), lambda i, j, kk: (i, 0)),
                 pl.BlockSpec((None, n, nn), lambda i, j, kk: (j, 0, 0)),
                 pl.BlockSpec((None, tm, n), lambda i, j, kk: (j, i, 0)),
                 (tm, n), NT, name)


def mm_rs_dw(s3, dy, name, out_dtype=F32, tm=1024):
    s, m, n = s3.shape
    nn = dy.shape[1]
    tm = _div_tile(m, tm)
    return _gemm(s3, dy, (s, n, nn), out_dtype, (s, 1, m // tm),
                 pl.BlockSpec((None, tm, n), lambda i, j, kk: (i, kk, 0)),
                 pl.BlockSpec((tm, nn), lambda i, j, kk: (kk, 0)),
                 pl.BlockSpec((None, n, nn), lambda i, j, kk: (i, 0, 0)),
                 (n, nn), TN, name)


def _colsum8(v):
    tr, d = v.shape
    return v.reshape(tr // 8, 8, d).sum(axis=0)


def _rstd(v):
    return lax.rsqrt(jnp.mean(v * v, axis=-1, keepdims=True) + EPS)


def _vec_spec(d):
    return pl.BlockSpec((1, d), lambda i: (0, 0))


def _acc_spec(d):
    return pl.BlockSpec((8, d), lambda i: (0, 0))


def pre_fwd(x, g, shift, scale, name):
    t, d = x.shape
    tr = _row_tile(t)

    def body(x_ref, g_ref, sh_ref, sc_ref, h_ref):
        xv = x_ref[...]
        xn = (xv * _rstd(xv)) * g_ref[...]
        h_ref[...] = (xn * (1 + sc_ref[...]) + sh_ref[...]).astype(BF16)

    row = pl.BlockSpec((tr, d), lambda i: (i, 0))
    return _call(body, name=name, out_shape=_sds((t, d), BF16), grid=(t // tr,),
                 in_specs=[row, _vec_spec(d), _vec_spec(d), _vec_spec(d)], out_specs=row,
                 compiler_params=_params("parallel"))(x, g, shift, scale)


def post_fwd(x, y, g, gate, name):
    t, d = x.shape
    tr = _row_tile(t)

    def body(x_ref, y_ref, g_ref, gt_ref, o_ref):
        yv = y_ref[...]
        o_ref[...] = x_ref[...] + gt_ref[...] * ((yv * _rstd(yv)) * g_ref[...])

    row = pl.BlockSpec((tr, d), lambda i: (i, 0))
    return _call(body, name=name, out_shape=_sds((t, d), F32), grid=(t // tr,),
                 in_specs=[row, row, _vec_spec(d), _vec_spec(d)], out_specs=row,
                 compiler_params=_params("parallel"))(x, y, g, gate)


def post_bwd(dxn, y, g, gate, name):
    t, d = y.shape
    tr = _row_tile(t)

    def body(dx_ref, y_ref, g_ref, gt_ref, dy_ref, dgate_ref, dg_ref):
        @pl.when(pl.program_id(0) == 0)
        def _():
            dgate_ref[...] = jnp.zeros_like(dgate_ref)
            dg_ref[...] = jnp.zeros_like(dg_ref)

        dxv, yv, gv = dx_ref[...], y_ref[...], g_ref[...]
        r = _rstd(yv)
        yhat = yv * r
        dgate_ref[...] += _colsum8(dxv * (yhat * gv))
        dyn = gt_ref[...] * dxv
        dg_ref[...] += _colsum8(dyn * yhat)
        dyhat = dyn * gv
        dy = r * (dyhat - yhat * jnp.mean(dyhat * yhat, axis=-1, keepdims=True))
        dy_ref[...] = dy.astype(BF16)

    row = pl.BlockSpec((tr, d), lambda i: (i, 0))
    return _call(body, name=name,
                 out_shape=[_sds((t, d), BF16), _sds((8, d), F32), _sds((8, d), F32)],
                 grid=(t // tr,), in_specs=[row, row, _vec_spec(d), _vec_spec(d)],
                 out_specs=[row, _acc_spec(d), _acc_spec(d)],
                 compiler_params=_params("arbitrary"))(dxn, y, g, gate)


def pre_bwd(dh, x, g, scale, dxn, name):
    t, d = x.shape
    tr = _row_tile(t)

    def body(dh_ref, x_ref, g_ref, sc_ref, dxn_ref, dx_ref, dsh_ref, dsc_ref, dg_ref):
        @pl.when(pl.program_id(0) == 0)
        def _():
            dsh_ref[...] = jnp.zeros_like(dsh_ref)
            dsc_ref[...] = jnp.zeros_like(dsc_ref)
            dg_ref[...] = jnp.zeros_like(dg_ref)

        dhv, xv, gv = dh_ref[...].astype(F32), x_ref[...], g_ref[...]
        r = _rstd(xv)
        xhat = xv * r
        dsh_ref[...] += _colsum8(dhv)
        dsc_ref[...] += _colsum8(dhv * (xhat * gv))
        dxn_ = dhv * (1 + sc_ref[...])
        dg_ref[...] += _colsum8(dxn_ * xhat)
        dxhat = dxn_ * gv
        dx_ref[...] = r * (dxhat - xhat * jnp.mean(dxhat * xhat, axis=-1, keepdims=True)) + dxn_ref[...]

    row = pl.BlockSpec((tr, d), lambda i: (i, 0))
    return _call(body, name=name,
                 out_shape=[_sds((t, d), F32)] + [_sds((8, d), F32)] * 3,
                 grid=(t // tr,), in_specs=[row, row, _vec_spec(d), _vec_spec(d), row],
                 out_specs=[row, _acc_spec(d), _acc_spec(d), _acc_spec(d)],
                 compiler_params=_params("arbitrary"))(dh, x, g, scale, dxn)


def loss_fwd_bwd(y, target, name):
    t, d = y.shape
    tr = _row_tile(t)

    def body(y_ref, t_ref, dy_ref, l_ref):
        @pl.when(pl.program_id(0) == 0)
        def _():
            l_ref[...] = jnp.zeros_like(l_ref)

        err = y_ref[...] - t_ref[...]
        l_ref[...] += _colsum8(err * err)
        dy_ref[...] = err * (1.0 / d)

    row = pl.BlockSpec((tr, d), lambda i: (i, 0))
    return _call(body, name=name, out_shape=[_sds((t, d), F32), _sds((8, d), F32)],
                 grid=(t // tr,), in_specs=[row, row], out_specs=[row, _acc_spec(d)],
                 compiler_params=_params("arbitrary"))(y, target)


def _sigmoid(x):
    return 1.0 / (1.0 + jnp.exp(-x))


def ffn_in_swiglu(h, wg, name, tm=1024):
    m, k = h.shape
    s, _, n = wg.shape
    half = s // 2
    tm = _div_tile(m, tm)

    def body(h_ref, wg_ref, wu_ref, g_ref, u_ref, s_ref):
        hv = h_ref[...]
        g = jnp.dot(hv, wg_ref[...], preferred_element_type=F32)
        u = jnp.dot(hv, wu_ref[...], preferred_element_type=F32)
        g_ref[...] = g
        u_ref[...] = u
        s_ref[...] = ((g * _sigmoid(g)) * u).astype(BF16)

    act = pl.BlockSpec((None, tm, n), lambda i, j: (j, i, 0))
    return _call(
        body, name=name, out_shape=[_sds((half, m, n), F32), _sds((half, m, n), F32), _sds((half, m, n), BF16)],
        grid=(m // tm, half),
        in_specs=[pl.BlockSpec((tm, k), lambda i, j: (i, 0)),
                  pl.BlockSpec((None, k, n), lambda i, j: (j, 0, 0)),
                  pl.BlockSpec((None, k, n), lambda i, j: (j + half, 0, 0))],
        out_specs=[act, act, act], compiler_params=_params("parallel", "parallel"),
    )(h, wg, wg)


def ffn_out_dx_swiglu(dy, w4, gate, up, name, tm=1024):
    m, nn = dy.shape
    half, n, _ = w4.shape
    tm = _div_tile(m, tm)

    def body(dy_ref, w_ref, g_ref, u_ref, o_ref):
        ds = lax.dot_general(dy_ref[...], w_ref[...], NT, preferred_element_type=F32)
        g, u = g_ref[...], u_ref[...]
        sig = _sigmoid(g)
        o_ref[0] = (ds * u * (sig * (1 + g * (1 - sig)))).astype(BF16)
        o_ref[1] = (ds * (g * sig)).astype(BF16)

    act = pl.BlockSpec((None, tm, n), lambda i, j: (j, i, 0))
    out = _call(
        body, name=name, out_shape=_sds((2, half, m, n), BF16), grid=(m // tm, half),
        in_specs=[pl.BlockSpec((tm, nn), lambda i, j: (i, 0)),
                  pl.BlockSpec((None, n, nn), lambda i, j: (j, 0, 0)), act, act],
        out_specs=pl.BlockSpec((2, None, tm, n), lambda i, j: (0, j, i, 0)),
        compiler_params=_params("parallel", "parallel"),
    )(dy, w4, gate, up)
    return out.reshape(2 * half, m, n)


def _split_hi_lo(v):
    hi = v.astype(BF16)
    lo = (v - hi.astype(F32)).astype(BF16)
    return hi, lo


SB_G = 4
SB_GW = SB_G * BLK


def _sb_specs(t):
    nq = t // BLK
    npair = N_HEADS // 2
    q_spec = pl.BlockSpec((BLK, BLK), lambda p, qb: (qb, p))
    k_spec = pl.BlockSpec((t, BLK), lambda p, qb: (0, npair + p))
    v_spec = pl.BlockSpec((t, BLK), lambda p, qb: (0, 2 * npair + p))
    kt_spec = pl.BlockSpec((t // SB_GW, BLK, SB_GW), lambda p, qb: (0, npair + p, 0))
    vt_spec = pl.BlockSpec((t // SB_GW, BLK, SB_GW), lambda p, qb: (0, 2 * npair + p, 0))
    c_spec = pl.BlockSpec((None, nq, 8, 2 * BLK), lambda p, qb: (p, 0, 0, qb))
    return nq, npair, q_spec, k_spec, v_spec, kt_spec, vt_spec, c_spec


def _sb_consts():
    row = lax.broadcasted_iota(jnp.int32, (BLK, BLK), 0)
    col = lax.broadcasted_iota(jnp.int32, (BLK, BLK), 1)
    lane0 = (col < HEAD_DIM).astype(F32)
    sub0 = (row < HEAD_DIM).astype(F32)
    return row, col, lane0, sub0


def _sb_valid(ks, qb):
    row = lax.broadcasted_iota(jnp.int32, (SB_GW, 2 * BLK), 0)
    col = lax.broadcasted_iota(jnp.int32, (SB_GW, 2 * BLK), 1)
    return (ks + row) < (qb * BLK + (col & (BLK - 1)))


def _blocks_on_lanes(v4):
    return jnp.concatenate([v4[b * BLK:(b + 1) * BLK] for b in range(SB_G)], axis=1)


def _tri2_dot(tri2, v):
    hi, lo = _split_hi_lo(v)
    return jnp.dot(tri2, jnp.concatenate([hi, lo], axis=0), preferred_element_type=F32)


def _sb_pair_loop(first, count, step, group, skip, carry):
    def pair(it, cy):
        g1 = first + 2 * step * it
        cy = group(g1, 0, 1, cy)
        return lax.cond(2 * it + 1 < count, lambda c: group(g1 + step, 1, 0, c), skip, cy)
    return lax.fori_loop(0, (count + 1) // 2, pair, carry)


def sb_fwd(qkv, qkv_t, riders, name):
    t = qkv.shape[0]
    assert t % SB_GW == 0
    nq, npair, q_spec, k_spec, _, _, vt_spec, c_spec = _sb_specs(t)
    nr = len(riders)

    def body(*refs):
        q_ref, k_ref, vt_ref = refs[:3]
        o_ref, c_ref = refs[3 + nr:5 + nr]
        oacc, zbuf0, zbuf1 = refs[5 + 2 * nr:8 + 2 * nr]
        pp = pl.program_id(0)
        qb = pl.program_id(1)
        if nr:
            gather = _Gather(refs[3:3 + nr], refs[5 + nr:5 + 2 * nr], *refs[8 + 2 * nr:])
            pl.when((pp == 0) & (qb == 0))(gather.start)
            pl.when((pp == npair // 2) & (qb == 0))(gather.forward)
        _sb_fwd_step(q_ref, k_ref, vt_ref, o_ref, c_ref, oacc, zbuf0, zbuf1, qb)
        if nr:
            pl.when((pp == npair - 1) & (qb == nq - 1))(gather.finish)

    any_spec = pl.BlockSpec(memory_space=pl.ANY)
    outs = _call(
        body, name=name,
        out_shape=[_sds((t, D_MODEL), BF16), _sds((npair, nq, 8, 2 * t), F32)] + _Gather.out_shapes(riders),
        grid=(npair, nq), in_specs=[q_spec, k_spec, vt_spec] + [any_spec] * nr,
        out_specs=[pl.BlockSpec((BLK, BLK), lambda p, qb: (qb, p)), c_spec] + [any_spec] * nr,
        scratch_shapes=[pltpu.VMEM((BLK, 2 * BLK), F32), pltpu.VMEM((SB_GW, 2 * BLK), F32),
                        pltpu.VMEM((SB_GW, 2 * BLK), F32)] + _comm_sems(nr),
        compiler_params=_params("arbitrary", "arbitrary"),
    )(qkv, qkv, qkv_t, *riders)
    return outs[0], outs[1], list(outs[2:])


def _sb_fwd_step(q_ref, k_ref, vt_ref, o_ref, c_ref, oacc, zbuf0, zbuf1, qb):
    row, col, lane0, sub0 = _sb_consts()
    tri = (col >= row).astype(BF16)
    tri2 = jnp.concatenate([tri, tri], axis=1)
    q2 = _two_heads(q_ref[...], lane0, QK_SCALE)
    zbufs = (zbuf0, zbuf1)
    c_ref[...] = jnp.zeros_like(c_ref)
    oacc[...] = jnp.zeros_like(oacc)

    def scores(g):
        ks = pl.multiple_of(g * SB_GW, SB_GW)
        return lax.dot_general(k_ref[pl.ds(ks, SB_GW), :], q2, NT, preferred_element_type=F32)

    def group(g, cur, nxt, cr, masked=False):
        z = zbufs[cur][...]
        zbufs[nxt][...] = scores(jnp.maximum(g - 1, 0))
        e = jnp.exp(-jnp.abs(z))
        sp = jnp.maximum(z, 0.0) + jnp.log(1.0 + e)
        if masked:
            valid = _sb_valid(g * SB_GW, qb)
            sp = jnp.where(valid, sp, 0.0)
        loc = _tri2_dot(tri2, _blocks_on_lanes(sp))
        parts = [None] * SB_G
        for b in reversed(range(SB_G)):
            rows = slice(b * BLK, (b + 1) * BLK)
            c_ref[g * SB_G + b] = jnp.broadcast_to(cr, (8, 2 * BLK))
            a = jnp.exp(z[rows] - (loc[:, 2 * b * BLK:2 * (b + 1) * BLK] + cr))
            if masked:
                a = jnp.where(valid[rows], a, 0.0)
            parts[b] = a.astype(BF16)
            cr = cr + jnp.sum(sp[rows], axis=0, keepdims=True)
        oacc[...] += jnp.dot(vt_ref[g], jnp.concatenate(parts, axis=0), preferred_element_type=F32)
        return cr

    last = qb // SB_G
    zbuf1[...] = scores(last)
    cr = group(last, 1, 0, jnp.zeros((1, 2 * BLK), F32), masked=True)
    _sb_pair_loop(last - 1, last, -1, group, lambda c: c, cr)
    o_t = oacc[:, :BLK] * sub0 + oacc[:, BLK:] * (1.0 - sub0)
    o_ref[...] = o_t.T.astype(BF16)


def sb_bwd(qkv, qkv_t, do, cmass, riders, name):
    t = qkv.shape[0]
    nq, npair, q_spec, k_spec, v_spec, kt_spec, _, c_spec = _sb_specs(t)
    nr = len(riders)

    def body(*refs):
        pp = pl.program_id(0)
        qb = pl.program_id(1)
        if nr:
            exchange = _Exchange(refs[6:6 + nr], refs[9 + nr:9 + 2 * nr], *refs[14 + 2 * nr:])
            pl.when((pp == 0) & (qb == 0))(exchange.start)
        step(*refs[:6], *refs[6 + nr:9 + nr], *refs[9 + 2 * nr:14 + 2 * nr])
        if nr:
            pl.when((pp == npair - 1) & (qb == nq - 1))(exchange.finish)

    def step(q_ref, k_ref, kt_ref, v_ref, do_ref, c_ref, dq_ref, dk_ref, dv_ref, dqacc, dkacc, dvacc,
             zbuf0, zbuf1):
        qb = pl.program_id(1)

        @pl.when(qb == 0)
        def _():
            dkacc[...] = jnp.zeros_like(dkacc)
            dvacc[...] = jnp.zeros_like(dvacc)

        row, col, lane0, sub0 = _sb_consts()
        tri_suf = (col >= row).astype(BF16)
        tri_pre = (col <= row).astype(BF16)
        tri2_suf = jnp.concatenate([tri_suf, tri_suf], axis=1)
        tri2_pre = jnp.concatenate([tri_pre, tri_pre], axis=1)
        q2 = _two_heads(q_ref[...], lane0, QK_SCALE)
        do2 = _two_heads(do_ref[...], lane0, 1.0)
        zbufs = (zbuf0, zbuf1)
        dqacc[...] = jnp.zeros_like(dqacc)
        last = qb // SB_G

        def scores(g):
            ks = pl.multiple_of(g * SB_GW, SB_GW)
            return lax.dot_general(k_ref[pl.ds(ks, SB_GW), :], q2, NT, preferred_element_type=F32)

        def group(g, cur, nxt, gc, masked=False):
            ks = pl.multiple_of(g * SB_GW, SB_GW)
            z = zbufs[cur][...]
            zbufs[nxt][...] = scores(jnp.minimum(g + 1, last))
            e = jnp.exp(-jnp.abs(z))
            inv = 1.0 / (1.0 + e)
            sig = jnp.where(z >= 0, inv, 1.0 - inv)
            sp = jnp.maximum(z, 0.0) + jnp.log(1.0 + e)
            if masked:
                valid = _sb_valid(ks, qb)
                sp = jnp.where(valid, sp, 0.0)
            loc = _tri2_dot(tri2_suf, _blocks_on_lanes(sp))
            parts = []
            for b in range(SB_G):
                rows = slice(b * BLK, (b + 1) * BLK)
                mass = loc[:, 2 * b * BLK:2 * (b + 1) * BLK] + c_ref[g * SB_G + b, 0:1, :]
                parts.append(jnp.exp(z[rows] - mass))
            a = jnp.concatenate(parts, axis=0)
            if masked:
                a = jnp.where(valid, a, 0.0)
            gr = lax.dot_general(v_ref[pl.ds(ks, SB_GW), :], do2, NT, preferred_element_type=F32) * a
            pre = _tri2_dot(tri2_pre, _blocks_on_lanes(gr))
            parts = []
            for b in range(SB_G):
                rows = slice(b * BLK, (b + 1) * BLK)
                parts.append(pre[:, 2 * b * BLK:2 * (b + 1) * BLK] + gc)
                gc = gc + jnp.sum(gr[rows], axis=0, keepdims=True)
            dz = gr - sig * jnp.concatenate(parts, axis=0)
            if masked:
                dz = jnp.where(valid, dz, 0.0)
            dz = dz.astype(BF16)
            dkacc[pl.ds(ks, SB_GW), :] += jnp.dot(dz, q2, preferred_element_type=F32)
            dqacc[...] += jnp.dot(kt_ref[g], dz, preferred_element_type=F32)
            dvacc[pl.ds(ks, SB_GW), :] += jnp.dot(a.astype(BF16), do2, preferred_element_type=F32)
            return gc

        def skip(gc):
            zbuf0[...] = zbuf1[...]
            return gc

        zbuf0[...] = scores(0)
        gc = _sb_pair_loop(0, last, 1, group, skip, jnp.zeros((1, 2 * BLK), F32))
        group(last, 0, 1, gc, masked=True)
        dq_t = (dqacc[:, :BLK] * sub0 + dqacc[:, BLK:] * (1.0 - sub0)) * QK_SCALE
        dq_ref[...] = dq_t.T.astype(BF16)

        @pl.when(qb == nq - 1)
        def _():
            dk_ref[...] = dkacc[...].astype(BF16)
            dv_ref[...] = dvacc[...].astype(BF16)

    col_spec = pl.BlockSpec((t, BLK), lambda p, qb: (0, p))
    blk_spec = pl.BlockSpec((BLK, BLK), lambda p, qb: (qb, p))
    any_spec = pl.BlockSpec(memory_space=pl.ANY)
    outs = _call(
        body, name=name,
        out_shape=[_sds((t, D_MODEL), BF16)] * 3 + [_sds(r.shape, r.dtype) for r in riders],
        grid=(npair, nq), in_specs=[q_spec, k_spec, kt_spec, v_spec, blk_spec, c_spec] + [any_spec] * nr,
        out_specs=[blk_spec, col_spec, col_spec] + [any_spec] * nr,
        scratch_shapes=[pltpu.VMEM((BLK, 2 * BLK), F32), pltpu.VMEM((t, BLK), F32), pltpu.VMEM((t, BLK), F32),
                        pltpu.VMEM((SB_GW, 2 * BLK), F32), pltpu.VMEM((SB_GW, 2 * BLK), F32)] + _comm_sems(nr),
        compiler_params=_params("arbitrary", "arbitrary"),
    )(qkv, qkv, qkv_t, qkv, do, cmass, *riders)
    return outs[0], outs[1], outs[2], list(outs[3:])


BAND_QPS = 2


def _band_static_mask(jj):
    row = lax.broadcasted_iota(jnp.int32, (2 * BLK, BLK), 0)
    col = lax.broadcasted_iota(jnp.int32, (2 * BLK, BLK), 1)
    qc = (row & (BLK - 1)) // 64
    kc = 2 * jj + col // 64
    return (kc >= qc) & (kc <= qc + 8)


def _band_key_start(qb, jj):
    kb = qb - (BAND_BLOCKS - 1) + jj
    return kb, pl.multiple_of(jnp.maximum(kb, 0) * BLK, BLK)


def _band_probs(q2, k_ref, bias, qb):
    blocks = []
    for jj in range(BAND_BLOCKS):
        kb, ks = _band_key_start(qb, jj)
        s = lax.dot_general(q2, k_ref[pl.ds(ks, BLK), :], NT, preferred_element_type=F32)
        s = s + bias[:, jj * BLK:(jj + 1) * BLK]
        ok = (kb >= 0) if 0 < jj < BAND_BLOCKS - 1 else _band_static_mask(jj) & (kb >= 0)
        blocks.append(jnp.where(ok, s, NEG))
    s = jnp.concatenate(blocks, axis=1)
    m = jnp.max(s, axis=-1, keepdims=True)
    e = jnp.exp(s - m)
    return e / jnp.sum(e, axis=-1, keepdims=True)


def _band_specs(t):
    npair = N_HEADS // 2
    rows = BAND_QPS * BLK
    q_spec = pl.BlockSpec((rows, BLK), lambda p, i: (i, p))
    k_spec = pl.BlockSpec((t, BLK), lambda p, i: (0, npair + p))
    v_spec = pl.BlockSpec((t, BLK), lambda p, i: (0, 2 * npair + p))
    b_spec = pl.BlockSpec((2, BLK, BAND_W), lambda p, i: (p, 0, 0))
    return npair, t // rows, q_spec, k_spec, v_spec, b_spec


def _two_heads(xv, lane0, scale):
    xf = xv.astype(F32)
    if scale != 1.0:
        xf = xf * scale
    return jnp.concatenate([xf * lane0, xf * (1.0 - lane0)], axis=0).astype(BF16)


def _one_of_two_heads(r, lane0):
    return r[:BLK] * lane0 + r[BLK:] * (1.0 - lane0)


def band_fwd(qkv, bias, name):
    t = qkv.shape[0]
    assert t % (BAND_QPS * BLK) == 0
    npair, nsteps, q_spec, k_spec, v_spec, b_spec = _band_specs(t)

    def body(q_ref, k_ref, v_ref, b_ref, o_ref):
        step = pl.program_id(1)
        _, _, lane0, _ = _sb_consts()
        bias2 = b_ref[...].reshape(2 * BLK, BAND_W)
        for u in range(BAND_QPS):
            qb = step * BAND_QPS + u
            rows = slice(u * BLK, (u + 1) * BLK)
            q2 = _two_heads(q_ref[rows, :], lane0, QK_SCALE)
            p = _band_probs(q2, k_ref, bias2, qb)
            acc = jnp.zeros((2 * BLK, BLK), F32)
            for jj in range(BAND_BLOCKS):
                _, ks = _band_key_start(qb, jj)
                acc += jnp.dot(p[:, jj * BLK:(jj + 1) * BLK].astype(BF16), v_ref[pl.ds(ks, BLK), :],
                               preferred_element_type=F32)
            o_ref[rows, :] = _one_of_two_heads(acc, lane0).astype(BF16)

    return _call(
        body, name=name, out_shape=_sds((t, D_MODEL), BF16), grid=(npair, nsteps),
        in_specs=[q_spec, k_spec, v_spec, b_spec],
        out_specs=pl.BlockSpec((BAND_QPS * BLK, BLK), lambda p, i: (i, p)),
        compiler_params=_params("parallel", "parallel"),
    )(qkv, qkv, qkv, bias)


def band_bwd(qkv, do, bias, name):
    t = qkv.shape[0]
    npair, nsteps, q_spec, k_spec, v_spec, b_spec = _band_specs(t)

    def body(q_ref, k_ref, v_ref, do_ref, b_ref, dq_ref, dk_ref, dv_ref, db_ref, dkacc, dvacc):
        step = pl.program_id(1)

        @pl.when(step == 0)
        def _():
            dkacc[...] = jnp.zeros_like(dkacc)
            dvacc[...] = jnp.zeros_like(dvacc)
            db_ref[...] = jnp.zeros_like(db_ref)

        _, _, lane0, _ = _sb_consts()
        bias2 = b_ref[...].reshape(2 * BLK, BAND_W)
        updates = []
        for u in range(BAND_QPS):
            qb = step * BAND_QPS + u
            rows = slice(u * BLK, (u + 1) * BLK)
            q2 = _two_heads(q_ref[rows, :], lane0, QK_SCALE)
            do2 = _two_heads(do_ref[rows, :], lane0, 1.0)
            p = _band_probs(q2, k_ref, bias2, qb)
            dp = jnp.concatenate(
                [lax.dot_general(do2, v_ref[pl.ds(_band_key_start(qb, jj)[1], BLK), :], NT,
                                 preferred_element_type=F32) for jj in range(BAND_BLOCKS)], axis=1)
            ds = p * (dp - jnp.sum(p * dp, axis=-1, keepdims=True))
            db_ref[...] += ds.reshape(2, BLK, BAND_W)
            dqa = jnp.zeros((2 * BLK, BLK), F32)
            for jj in range(BAND_BLOCKS):
                _, ks = _band_key_start(qb, jj)
                dsb = ds[:, jj * BLK:(jj + 1) * BLK].astype(BF16)
                pb = p[:, jj * BLK:(jj + 1) * BLK].astype(BF16)
                dqa += jnp.dot(dsb, k_ref[pl.ds(ks, BLK), :], preferred_element_type=F32)
                updates.append((ks, lax.dot_general(dsb, q2, TN, preferred_element_type=F32),
                                lax.dot_general(pb, do2, TN, preferred_element_type=F32)))
            dq_ref[rows, :] = (_one_of_two_heads(dqa, lane0) * QK_SCALE).astype(BF16)
        for ks, dk_part, dv_part in updates:
            dkacc[pl.ds(ks, BLK), :] += dk_part
            dvacc[pl.ds(ks, BLK), :] += dv_part

        @pl.when(step == nsteps - 1)
        def _():
            dk_ref[...] = dkacc[...].astype(BF16)
            dv_ref[...] = dvacc[...].astype(BF16)

    col_spec = pl.BlockSpec((t, BLK), lambda p, i: (0, p))
    blk_spec = pl.BlockSpec((BAND_QPS * BLK, BLK), lambda p, i: (i, p))
    return _call(
        body, name=name,
        out_shape=[_sds((t, D_MODEL), BF16)] * 3 + [_sds((N_HEADS, BLK, BAND_W), F32)],
        grid=(npair, nsteps), in_specs=[q_spec, k_spec, v_spec, blk_spec, b_spec],
        out_specs=[blk_spec, col_spec, col_spec, b_spec],
        scratch_shapes=[pltpu.VMEM((t, BLK), F32), pltpu.VMEM((t, BLK), F32)],
        compiler_params=_params("parallel", "arbitrary"),
    )(qkv, qkv, qkv, do, bias)


def band_bias_window(rel_bias):
    far = BAND_W + BLK - 1 - 2 * REL_CLIP
    ext = jnp.concatenate(
        [jnp.broadcast_to(rel_bias[:, 2 * REL_CLIP:], (N_HEADS, far)), rel_bias[:, 2 * REL_CLIP:0:-1]], axis=1)
    return jnp.stack([ext[:, BLK - 1 - i:BLK - 1 - i + BAND_W] for i in range(BLK)], axis=1)


def band_bias_window_grad(dwin):
    width = BAND_W + BLK
    far = BAND_W + BLK - 1 - 2 * REL_CLIP
    flat = jnp.pad(dwin, ((0, 0), (0, 0), (0, BLK))).reshape(N_HEADS, BLK * width)
    skew = jnp.pad(flat, ((0, 0), (BLK - 1, 1))).reshape(N_HEADS, BLK, width + 1)
    dext = jnp.sum(skew, axis=1)[:, :width - 1]
    return jnp.concatenate(
        [jnp.zeros((N_HEADS, 1), F32), dext[:, :far - 1:-1][:, :2 * REL_CLIP - 1],
         dext[:, far:far + 1] + jnp.sum(dext[:, :far], axis=1, keepdims=True)], axis=1)


SG_GROUPS = 8


def _gelu_parts(x):
    inner = GELU_C0 * (x + GELU_C1 * (x * x * x))
    th = jnp.tanh(inner)
    return th, 0.5 * x * (1.0 + th)


def _sg_gate_mask():
    row = lax.broadcasted_iota(jnp.int32, (BLK, BLK), 0)
    col = lax.broadcasted_iota(jnp.int32, (BLK, BLK), 1)
    return (row // 64) >= (col // 64)


def _sg_forward_parts(a, lng):
    w = a.shape[1] // 2
    th, z = _gelu_parts(a)
    u, v = z[:, :w], z[:, w:]
    mu = jnp.mean(v, axis=-1, keepdims=True)
    xc = v - mu
    rstd = lax.rsqrt(jnp.mean(xc * xc, axis=-1, keepdims=True) + EPS)
    vhat = xc * rstd
    return th, u, vhat, rstd, vhat * lng


def sg_fwd(a, lng, ws, bias_t, name):
    t, w2 = a.shape
    w = w2 // 2
    gc = w // SG_GROUPS

    def body(a_ref, lng_ref, ws_ref, bt_ref, y_ref):
        _, u, _, _, vln = _sg_forward_parts(a_ref[...], lng_ref[...])
        mask = _sg_gate_mask()
        bt = bt_ref[...]
        lane = lax.broadcasted_iota(jnp.int32, (BLK, BLK), 1)
        for g in range(SG_GROUPS):
            wg = jnp.where(mask, ws_ref[g], 0.0).astype(BF16)
            sv = jnp.dot(wg, vln[:, g * gc:(g + 1) * gc].astype(BF16), preferred_element_type=F32)
            bg = jnp.sum(jnp.where(lane == g, bt, 0.0), axis=-1, keepdims=True)
            y_ref[:, g * gc:(g + 1) * gc] = (u[:, g * gc:(g + 1) * gc] * (sv + bg)).astype(BF16)

    return _call(
        body, name=name, out_shape=_sds((t, w), BF16), grid=(t // BLK,),
        in_specs=[pl.BlockSpec((BLK, w2), lambda i: (i, 0)), pl.BlockSpec((1, w), lambda i: (0, 0)),
                  pl.BlockSpec((SG_GROUPS, BLK, BLK), lambda i: (0, 0, 0)),
                  pl.BlockSpec((BLK, BLK), lambda i: (0, 0))],
        out_specs=pl.BlockSpec((BLK, w), lambda i: (i, 0)),
        compiler_params=_params("parallel"),
    )(a, lng, ws, bias_t)


def sg_bwd(a, dy, lng, ws, bias_t, name):
    t, w2 = a.shape
    w = w2 // 2
    gc = w // SG_GROUPS

    def body(a_ref, dy_ref, lng_ref, ws_ref, bt_ref, da_ref, dlng_ref, dws_ref, dbt_ref):
        @pl.when(pl.program_id(0) == 0)
        def _():
            dlng_ref[...] = jnp.zeros_like(dlng_ref)
            dws_ref[...] = jnp.zeros_like(dws_ref)
            dbt_ref[...] = jnp.zeros_like(dbt_ref)

        av, lng = a_ref[...], lng_ref[...]
        th, u, vhat, rstd, vln = _sg_forward_parts(av, lng)
        mask = _sg_gate_mask()
        bt = bt_ref[...]
        lane = lax.broadcasted_iota(jnp.int32, (BLK, BLK), 1)
        dyv = dy_ref[...]
        du_parts, dvln_parts = [], []
        dbt = jnp.zeros((BLK, BLK), F32)
        for g in range(SG_GROUPS):
            sl = slice(g * gc, (g + 1) * gc)
            wg = jnp.where(mask, ws_ref[g], 0.0).astype(BF16)
            vg = vln[:, sl].astype(BF16)
            sv = jnp.dot(wg, vg, preferred_element_type=F32)
            bg = jnp.sum(jnp.where(lane == g, bt, 0.0), axis=-1, keepdims=True)
            dyg = dyv[:, sl]
            du_parts.append(dyg * (sv + bg))
            dsv = dyg * u[:, sl]
            dbt += jnp.where(lane == g, jnp.sum(dsv, axis=-1, keepdims=True), 0.0)
            dsvb = dsv.astype(BF16)
            dws_ref[g] += jnp.where(mask, lax.dot_general(dsvb, vg, NT, preferred_element_type=F32), 0.0)
            dvln_parts.append(lax.dot_general(wg, dsvb, TN, preferred_element_type=F32))
        dbt_ref[...] += dbt
        du = jnp.concatenate(du_parts, axis=1)
        dvln = jnp.concatenate(dvln_parts, axis=1)
        dlng_ref[...] += _colsum8(dvln * vhat)
        dvhat = dvln * lng
        dv = rstd * (dvhat - jnp.mean(dvhat, axis=-1, keepdims=True)
                     - vhat * jnp.mean(dvhat * vhat, axis=-1, keepdims=True))
        dz = jnp.concatenate([du, dv], axis=1)
        dgelu = 0.5 * (1.0 + th) + (0.5 * av) * (1.0 - th * th) * (GELU_C0 * (1.0 + 3.0 * GELU_C1 * (av * av)))
        da_ref[...] = (dz * dgelu).astype(BF16)

    return _call(
        body, name=name,
        out_shape=[_sds((t, w2), BF16), _sds((8, w), F32), _sds((SG_GROUPS, BLK, BLK), F32), _sds((BLK, BLK), F32)],
        grid=(t // BLK,),
        in_specs=[pl.BlockSpec((BLK, w2), lambda i: (i, 0)), pl.BlockSpec((BLK, w), lambda i: (i, 0)),
                  pl.BlockSpec((1, w), lambda i: (0, 0)),
                  pl.BlockSpec((SG_GROUPS, BLK, BLK), lambda i: (0, 0, 0)),
                  pl.BlockSpec((BLK, BLK), lambda i: (0, 0))],
        out_specs=[pl.BlockSpec((BLK, w2), lambda i: (i, 0)), pl.BlockSpec((8, w), lambda i: (0, 0)),
                   pl.BlockSpec((SG_GROUPS, BLK, BLK), lambda i: (0, 0, 0)),
                   pl.BlockSpec((BLK, BLK), lambda i: (0, 0))],
        compiler_params=_params("arbitrary"),
    )(a, dy, lng, ws, bias_t)


def _shift_down(cat, n, tr):
    return pltpu.roll(cat, n, 0)[8:8 + tr]


def _shift_up(cat, n, tr):
    return pltpu.roll(cat, tr + 8 - n, 0)[0:tr]


def conv_fwd(p, cw, name):
    t, d3 = p.shape
    d = d3 // 3
    tr = min(256, t)
    hb = tr // 8

    def body(p_ref, ph_ref, cw_ref, o_ref):
        i = pl.program_id(0)
        pv = p_ref[...]
        y = pv[:, d:2 * d] * pv[:, 2 * d:]
        ph = ph_ref[...]
        yh = jnp.where(i > 0, ph[:, d:2 * d] * ph[:, 2 * d:], 0.0)
        cat = jnp.concatenate([yh, y], axis=0)
        yc = (cw_ref[0:1, :] * _shift_down(cat, 2, tr) + cw_ref[1:2, :] * _shift_down(cat, 1, tr)
              + cw_ref[2:3, :] * y)
        o_ref[...] = (pv[:, :d] * yc).astype(BF16)

    return _call(
        body, name=name, out_shape=_sds((t, d), BF16), grid=(t // tr,),
        in_specs=[pl.BlockSpec((tr, d3), lambda i: (i, 0)),
                  pl.BlockSpec((8, d3), lambda i: (jnp.maximum(i * hb - 1, 0), 0)),
                  pl.BlockSpec((8, d), lambda i: (0, 0))],
        out_specs=pl.BlockSpec((tr, d), lambda i: (i, 0)),
        compiler_params=_params("parallel"),
    )(p, p, cw)


def conv_bwd(p, dz, cw, name):
    t, d3 = p.shape
    d = d3 // 3
    tr = min(256, t)
    hb = tr // 8
    nt = t // tr

    def body(p_ref, ph_ref, pn_ref, dz_ref, dzn_ref, cw_ref, dp_ref, dcw_ref):
        i = pl.program_id(0)

        @pl.when(i == 0)
        def _():
            dcw_ref[...] = jnp.zeros_like(dcw_ref)

        pv = p_ref[...]
        gb, gcv, xt = pv[:, :d], pv[:, d:2 * d], pv[:, 2 * d:]
        y = gcv * xt
        ph = ph_ref[...]
        yh = jnp.where(i > 0, ph[:, d:2 * d] * ph[:, 2 * d:], 0.0)
        cat = jnp.concatenate([yh, y], axis=0)
        y2, y1 = _shift_down(cat, 2, tr), _shift_down(cat, 1, tr)
        w0, w1, w2 = cw_ref[0:1, :], cw_ref[1:2, :], cw_ref[2:3, :]
        yc = w0 * y2 + w1 * y1 + w2 * y
        dzv = dz_ref[...]
        dyc = dzv * gb
        dcw_ref[0] += _colsum8(dyc * y2)
        dcw_ref[1] += _colsum8(dyc * y1)
        dcw_ref[2] += _colsum8(dyc * y)
        dycn = jnp.where(i < nt - 1, dzn_ref[...] * pn_ref[...][:, :d], 0.0)
        catn = jnp.concatenate([dyc, dycn], axis=0)
        dy = w2 * dyc + w1 * _shift_up(catn, 1, tr) + w0 * _shift_up(catn, 2, tr)
        dp_ref[:, :d] = (dzv * yc).astype(BF16)
        dp_ref[:, d:2 * d] = (dy * xt).astype(BF16)
        dp_ref[:, 2 * d:] = (dy * gcv).astype(BF16)

    nxt = lambda i: (jnp.minimum((i + 1) * hb, t // 8 - 1), 0)
    return _call(
        body, name=name, out_shape=[_sds((t, d3), BF16), _sds((3, 8, d), F32)], grid=(nt,),
        in_specs=[pl.BlockSpec((tr, d3), lambda i: (i, 0)),
                  pl.BlockSpec((8, d3), lambda i: (jnp.maximum(i * hb - 1, 0), 0)),
                  pl.BlockSpec((8, d3), nxt),
                  pl.BlockSpec((tr, d), lambda i: (i, 0)),
                  pl.BlockSpec((8, d), nxt),
                  pl.BlockSpec((8, d), lambda i: (0, 0))],
        out_specs=[pl.BlockSpec((tr, d3), lambda i: (i, 0)), pl.BlockSpec((3, 8, d), lambda i: (0, 0, 0))],
        compiler_params=_params("arbitrary"),
    )(p, p, p, dz, dz, cw)


def ada_fwd(c_all, w, b, name):
    nl, d, n = w.shape

    def body(c_ref, w_ref, b_ref, o_ref):
        cv = c_ref[...]
        s = (cv * _sigmoid(cv)).astype(BF16)
        o_ref[...] = jnp.dot(s, w_ref[...].astype(BF16), preferred_element_type=F32) + b_ref[...]

    return _call(
        body, name=name, out_shape=_sds((nl, N_DEV, n), F32), grid=(nl,),
        in_specs=[pl.BlockSpec((N_DEV, d), lambda l: (0, 0)), pl.BlockSpec((None, d, n), lambda l: (l, 0, 0)),
                  pl.BlockSpec((None, 1, n), lambda l: (l, 0, 0))],
        out_specs=pl.BlockSpec((None, N_DEV, n), lambda l: (l, 0, 0)),
        compiler_params=_params("parallel"),
    )(c_all, w, b)


def ada_bwd(c_all, dmod, name):
    nl, _, n = dmod.shape
    d = c_all.shape[1]

    def body(c_ref, dm_ref, o_ref):
        cv = c_ref[...]
        s = (cv * _sigmoid(cv)).astype(BF16)
        o_ref[...] = lax.dot_general(s, dm_ref[...].astype(BF16), TN, preferred_element_type=F32)

    return _call(
        body, name=name, out_shape=_sds((nl, d, n), F32), grid=(nl,),
        in_specs=[pl.BlockSpec((N_DEV, d), lambda l: (0, 0)), pl.BlockSpec((None, N_DEV, n), lambda l: (l, 0, 0))],
        out_specs=pl.BlockSpec((None, d, n), lambda l: (l, 0, 0)),
        compiler_params=_params("parallel"),
    )(c_all, dmod)


def adamw(pieces, w, m, v, name):
    npc, r, c = pieces.shape
    tr = r
    for cand in (1024, 512, 256, 128, 64, 32, 16, 8):
        if r % cand == 0 and cand * c * 4 <= (1 << 20):
            tr = cand
            break

    def body(p_ref, w_ref, m_ref, v_ref, g_ref, d_ref, nm_ref, nv_ref):
        g = p_ref[0].astype(F32)
        for i in range(1, npc):
            g = g + p_ref[i].astype(F32)
        wv = w_ref[...]
        nm = ADAM_B1 * m_ref[...] + (1.0 - ADAM_B1) * g
        nv = ADAM_B2 * v_ref[...] + (1.0 - ADAM_B2) * (g * g)
        m_hat = nm / (1.0 - ADAM_B1 ** ADAM_STEP)
        v_hat = nv / (1.0 - ADAM_B2 ** ADAM_STEP)
        g_ref[...] = g
        d_ref[...] = -ADAM_LR * (m_hat / (jnp.sqrt(v_hat) + ADAM_EPS) + ADAM_WD * wv)
        nm_ref[...] = nm
        nv_ref[...] = nv

    row = pl.BlockSpec((tr, c), lambda i: (i, 0))
    return _call(
        body, name=name, out_shape=[_sds((r, c), F32)] * 4, grid=(r // tr,),
        in_specs=[pl.BlockSpec((npc, tr, c), lambda i: (0, i, 0)), row, row, row],
        out_specs=[row] * 4, compiler_params=_params("parallel"),
    )(pieces, w, m, v)


def sum_pieces(pieces, name):
    npc, r, c = pieces.shape

    def body(p_ref, o_ref):
        g = p_ref[0]
        for i in range(1, npc):
            g = g + p_ref[i]
        o_ref[...] = g

    return _call(body, name=name, out_shape=_sds((r, c), F32),
                 in_specs=[pl.BlockSpec(memory_space=pltpu.VMEM)],
                 out_specs=pl.BlockSpec(memory_space=pltpu.VMEM),
                 compiler_params=pltpu.CompilerParams(vmem_limit_bytes=VMEM_LIMIT))(pieces)


PACK_W = 1024


def _pack(arrs):
    flat = jnp.concatenate([a.reshape(-1).astype(F32) for a in arrs])
    rows = -(-flat.shape[0] // (8 * PACK_W)) * 8
    return jnp.pad(flat, (0, rows * PACK_W - flat.shape[0])).reshape(rows, PACK_W)


def _unpack(slab, shapes):
    flat = slab.reshape(-1)
    out, off = [], 0
    for s in shapes:
        n = 1
        for q in s:
            n *= q
        out.append(flat[off:off + n].reshape(s))
        off += n
    return out


def kernel(x, c, ada_w, ada_b, norm_g, ffn_w_in, ffn_w_out, sb_w_qkv, sb_w_o, sg_w_in, sg_ln_g, sg_w_s, sg_bias, sg_w_out, sc_w_in, sc_conv_w, sc_w_out, cb_w_qkv, cb_rel_bias, cb_w_o, loss_target, m_ada_w, m_ada_b, m_norm_g, m_ffn_w_in, m_ffn_w_out, m_sb_w_qkv, m_sb_w_o, m_sg_w_in, m_sg_ln_g, m_sg_w_s, m_sg_bias, m_sg_w_out, m_sc_w_in, m_sc_conv_w, m_sc_w_out, m_cb_w_qkv, m_cb_rel_bias, m_cb_w_o, v_ada_w, v_ada_b, v_norm_g, v_ffn_w_in, v_ffn_w_out, v_sb_w_qkv, v_sb_w_o, v_sg_w_in, v_sg_ln_g, v_sg_w_s, v_sg_bias, v_sg_w_out, v_sc_w_in, v_sc_conv_w, v_sc_w_out, v_cb_w_qkv, v_cb_rel_bias, v_cb_w_o):
    depth = ada_w.shape[0]
    d = D_MODEL
    xi, yi, ci = lax.axis_index("x"), lax.axis_index("y"), lax.axis_index("c")
    me = 4 * xi + 2 * yi + ci
    x0 = x[0]
    t = x0.shape[0]
    target = loss_target[0]

    c_all = _all_gather([jnp.pad(c, ((0, 7), (0, 0)))], "gather_c")[0][:, 0, :]
    na = ada_w.shape[2]
    b_cols = lax.dynamic_slice_in_dim(ada_b, me * na, na, axis=1)[:, None, :]
    mod_part = ada_fwd(c_all, ada_w, b_cols, "ada_fwd")
    mod_g = _all_gather([mod_part.reshape(depth * N_DEV, na)], "gather_mod")[0]
    mod_g = mod_g.reshape(N_DEV, depth, N_DEV, na)
    mod_me = lax.dynamic_index_in_dim(mod_g, me, axis=2, keepdims=False)
    mod = jnp.transpose(mod_me, (1, 0, 2)).reshape(depth, 6, 1, d)

    ng = _all_gather([norm_g.reshape(depth * 4, d // N_DEV)], "gather_norm_g")[0]
    norm_full = jnp.transpose(ng, (1, 0, 2)).reshape(depth, 4, 1, d)
    small = _all_gather([_pack([sg_ln_g, sc_conv_w])], "gather_small")[0].reshape(N_DEV, -1)
    nl_g = sg_ln_g.shape[1]
    ln_full = small[:, :nl_g].reshape(1, N_DEV * nl_g)
    cwn = sc_conv_w.shape[2]
    cw_sh = small[:, nl_g:nl_g + 3 * cwn].reshape(N_DEV, 3, cwn)
    cw_full = jnp.transpose(cw_sh, (1, 0, 2)).reshape(3, d)
    cw_pad = jnp.pad(cw_full, ((0, 5), (0, 0)))

    bf = lambda a: a.astype(BF16)
    mixers = [
        [bf(sb_w_qkv[0]), bf(sb_w_o[0])],
        [bf(sg_w_in[0]), bf(sg_w_out[0])],
        [bf(sc_w_in[0]), bf(sc_w_out[0])],
        [bf(cb_w_qkv[0]), bf(cb_w_o[0])],
    ]
    shards = [[bf(ffn_w_in[i]), bf(ffn_w_out[i])] + mixers[i % 4] for i in range(depth)]
    gathered = [None] * depth
    w_qkv0 = _all_gather([shards[0][2]], "gather_w_qkv0")[0]
    riding_shards = [shards[0][0], shards[0][1], shards[0][3]] + [s for i in range(1, depth) for s in shards[i]]

    bias_win = band_bias_window(cb_rel_bias[0])
    ws = sg_w_s[0]
    bias_t = jnp.pad(sg_bias[0].T, ((0, 0), (0, BLK - SG_GROUPS)))

    saved = []
    xcur = x0
    for i in range(depth):
        mi = i % 4
        sh_m, sc_m, gt_m, sh_f, sc_f, gt_f = [mod[i, j] for j in range(6)]
        g0, g1, g2, g3 = [norm_full[i, j] for j in range(4)]
        tag = "L%d_" % i
        sv = {"x_in": xcur}
        h = pre_fwd(xcur, g0, sh_m, sc_m, tag + "pre_m")
        sv["h_m"] = h
        if mi == 0:
            qkv = mm_cs(h, w_qkv0, tag + "qkv", out_dtype=BF16)
            qkv_t = jnp.transpose(qkv.reshape(t // SB_GW, SB_GW, 3 * d), (0, 2, 1))
            o, cmass, riding = sb_fwd(qkv, qkv_t, riding_shards, tag + "sb_fwd")
            gathered[0] = [riding[0], riding[1], w_qkv0, riding[2]]
            for j in range(1, depth):
                gathered[j] = riding[4 * j - 1:4 * j + 3]
            sv.update(qkv=qkv, qkv_t=qkv_t, o=o, cmass=cmass)
            mixed = o
        elif mi == 1:
            a = mm_cs(h, gathered[i][2], tag + "sg_in")
            mixed = sg_fwd(a, ln_full, ws, bias_t, tag + "sg_fwd")
            sv.update(a=a, yy=mixed)
        elif mi == 2:
            p = mm_cs(h, gathered[i][2], tag + "sc_in")
            mixed = conv_fwd(p, cw_pad, tag + "conv_fwd")
            sv.update(p=p, gz=mixed)
        else:
            qkv = mm_cs(h, gathered[i][2], tag + "qkv", out_dtype=BF16)
            mixed = band_fwd(qkv, bias_win, tag + "band_fwd")
            sv.update(qkv=qkv, o=mixed)
        wfi, wfo, _, wmo = gathered[i]
        wfo4 = wfo.reshape(4, -1, d)
        y = mm(mixed, wmo.reshape(-1, d), tag + "mix_out")
        sv["y_m"] = y
        xmid = post_fwd(xcur, y, g1, gt_m, tag + "post_m")
        sv["x_mid"] = xmid
        h2 = pre_fwd(xmid, g2, sh_f, sc_f, tag + "pre_f")
        ag, au, s3 = ffn_in_swiglu(h2, wfi, tag + "ffn_in")
        y2 = mm_rs(s3, wfo4, tag + "ffn_out")
        sv.update(h_f=h2, ag=ag, au=au, s3=s3, y_f=y2)
        xcur = post_fwd(xmid, y2, g3, gt_f, tag + "post_f")
        saved.append(sv)

    dx, lpart = loss_fwd_bwd(xcur, target, "loss")
    loss = lax.psum(0.5 * jnp.sum(lpart) / d, ("x", "y", "c"))

    dmod_rows = [None] * depth
    dnorm_rows = [None] * depth
    big_pieces = [None] * depth
    small_grads = {}
    for i in reversed(range(depth)):
        wfi, wfo, wmi, wmo = gathered[i]
        wfo4 = wfo.reshape(4, -1, d)
        wmo2 = wmo.reshape(-1, d)
        mi = i % 4
        sh_m, sc_m, gt_m, sh_f, sc_f, gt_f = [mod[i, j] for j in range(6)]
        g0, g1, g2, g3 = [norm_full[i, j] for j in range(4)]
        tag = "L%d_b_" % i
        sv = saved[i]
        dy2, dgt_f, dg3 = post_bwd(dx, sv["y_f"], g3, gt_f, tag + "post_f")
        da3 = ffn_out_dx_swiglu(dy2, wfo4, sv["ag"], sv["au"], tag + "ffn_out_dx")
        dwfo = mm_rs_dw(sv["s3"], dy2, tag + "ffn_out_dw", out_dtype=BF16)
        dh2 = mm_cs_dx(da3, wfi, tag + "ffn_in_dx", act_major=True)
        dwfi = mm_cs_dw(sv["h_f"], da3, tag + "ffn_in_dw", act_major=True, out_dtype=BF16)
        dx, dsh_f, dsc_f, dg2 = pre_bwd(dh2, sv["x_mid"], g2, sc_f, dx, tag + "pre_f")
        dy, dgt_m, dg1 = post_bwd(dx, sv["y_m"], g1, gt_m, tag + "post_m")
        if mi == 0:
            do = mm_nt(dy, wmo2, tag + "wo_dx", out_dtype=BF16)
            dwmo = mm_tn(sv["o"], dy, tag + "wo_dw", out_dtype=BF16)
            riders = [dwfi, dwfo.reshape(N_DEV, -1, d), dwmo.reshape(N_DEV, -1, d)]
            riders += [piece for j in range(depth - 1, 0, -1) for piece in big_pieces[j]]
            dq, dk, dv, exchanged = sb_bwd(sv["qkv"], sv["qkv_t"], do, sv["cmass"], riders, tag + "sb_bwd")
            for n, j in enumerate(range(depth - 1, 0, -1)):
                big_pieces[j] = exchanged[3 + 4 * n:3 + 4 * (n + 1)]
            dmid = jnp.concatenate([dq, dk, dv], axis=1)
        elif mi == 1:
            dyy = mm_nt(dy, wmo2, tag + "sg_out_dx")
            dwmo = mm_tn(sv["yy"], dy, tag + "sg_out_dw", out_dtype=BF16)
            dmid, dlng, dws, dbt = sg_bwd(sv["a"], dyy, ln_full, ws, bias_t, tag + "sg_bwd")
            small_grads.update(ln_g=jnp.sum(dlng, axis=0), w_s=dws, bias=dbt[:, :SG_GROUPS].T)
        elif mi == 2:
            dgz = mm_nt(dy, wmo2, tag + "sc_out_dx")
            dwmo = mm_tn(sv["gz"], dy, tag + "sc_out_dw", out_dtype=BF16)
            dmid, dcw = conv_bwd(sv["p"], dgz, cw_pad, tag + "conv_bwd")
            small_grads.update(conv_w=jnp.sum(dcw, axis=1))
        else:
            do = mm_nt(dy, wmo2, tag + "wo_dx", out_dtype=BF16)
            dwmo = mm_tn(sv["o"], dy, tag + "wo_dw", out_dtype=BF16)
            dq, dk, dv, dwin = band_bwd(sv["qkv"], do, bias_win, tag + "band_bwd")
            dmid = jnp.concatenate([dq, dk, dv], axis=1)
            small_grads.update(rel_bias=band_bias_window_grad(dwin))
        dh = mm_cs_dx(dmid, wmi, tag + "mix_in_dx")
        dwmi = mm_cs_dw(sv["h_m"], dmid, tag + "mix_in_dw", out_dtype=BF16)
        dx, dsh_m, dsc_m, dg0 = pre_bwd(dh, sv["x_in"], g0, sc_m, dx, tag + "pre_m")
        dmod_rows[i] = jnp.stack([jnp.sum(q, axis=0) for q in (dsh_m, dsc_m, dgt_m, dsh_f, dsc_f, dgt_f)])
        dnorm_rows[i] = jnp.stack([jnp.sum(q, axis=0) for q in (dg0, dg1, dg2, dg3)])
        big_pieces[i] = [dwfi, dwfo.reshape(N_DEV, -1, d), dwmi, dwmo.reshape(N_DEV, -1, d)]
    big_pieces[0] = [exchanged[0], exchanged[1], _all_to_all([big_pieces[0][2]], "exchange_w_qkv0")[0], exchanged[2]]

    grad_x = dx[None]

    dmod_mine = jnp.stack(dmod_rows).reshape(depth, 6 * d)
    dnorm_mine = jnp.stack(dnorm_rows)
    small_list = [dnorm_mine, small_grads["ln_g"], small_grads["w_s"], small_grads["bias"],
                  small_grads["conv_w"], small_grads["rel_bias"]]
    small_shapes = [dmod_mine.shape] + [a.shape for a in small_list]
    slab = _pack([dmod_mine] + small_list)
    slab_g = _all_gather([slab], "gather_small_grads")[0]
    tot = sum_pieces(slab_g, "sum_small_grads")
    g_ada_b_full, g_norm, g_ln, g_ws, g_sbias, g_cw, g_rb = _unpack(tot, small_shapes)
    dmod_all = slab_g.reshape(N_DEV, -1)[:, :depth * 6 * d].reshape(N_DEV, depth, 6 * d)
    dmod_cols = lax.dynamic_slice_in_dim(dmod_all, me * na, na, axis=2)
    g_ada_w = ada_bwd(c_all, jnp.transpose(dmod_cols, (1, 0, 2)), "ada_bwd")

    nsh = d // N_DEV
    g_norm_sh = lax.dynamic_slice_in_dim(g_norm, me * nsh, nsh, axis=2)
    g_ln_sh = lax.dynamic_slice_in_dim(g_ln.reshape(1, -1), me * nl_g, nl_g, axis=1)
    g_cw_sh = lax.dynamic_slice_in_dim(g_cw, me * cwn, cwn, axis=1)[None]

    out_g, out_d, out_m, out_v = {}, {}, {}, {}

    def upd(name, pieces, w, m, v):
        r_c = pieces.shape[1:]
        g, dl, nm, nv = adamw(pieces, w.reshape(r_c), m.reshape(r_c), v.reshape(r_c), "adamw_" + name)
        out_g[name], out_d[name] = g.reshape(w.shape), dl.reshape(w.shape)
        out_m[name], out_v[name] = nm.reshape(w.shape), nv.reshape(w.shape)

    upd("ada_w", g_ada_w.reshape(1, depth * d, na), ada_w, m_ada_w, v_ada_w)
    nfi = ffn_w_in.shape[2]
    nfo = ffn_w_out.shape[1]
    pfi = jnp.concatenate([big_pieces[i][0] for i in range(depth)], axis=1)
    pfo = jnp.concatenate([big_pieces[i][1] for i in range(depth)], axis=1)
    upd("ffn_w_in", pfi, ffn_w_in, m_ffn_w_in, v_ffn_w_in)
    upd("ffn_w_out", pfo, ffn_w_out, m_ffn_w_out, v_ffn_w_out)
    upd("sb_w_qkv", big_pieces[0][2], sb_w_qkv, m_sb_w_qkv, v_sb_w_qkv)
    upd("sb_w_o", big_pieces[0][3], sb_w_o, m_sb_w_o, v_sb_w_o)
    upd("sg_w_in", big_pieces[1][2], sg_w_in, m_sg_w_in, v_sg_w_in)
    upd("sg_w_out", big_pieces[1][3], sg_w_out, m_sg_w_out, v_sg_w_out)
    upd("sc_w_in", big_pieces[2][2], sc_w_in, m_sc_w_in, v_sc_w_in)
    upd("sc_w_out", big_pieces[2][3], sc_w_out, m_sc_w_out, v_sc_w_out)
    upd("cb_w_qkv", big_pieces[3][2], cb_w_qkv, m_cb_w_qkv, v_cb_w_qkv)
    upd("cb_w_o", big_pieces[3][3], cb_w_o, m_cb_w_o, v_cb_w_o)

    small_names = ["ada_b", "norm_g", "sg_ln_g", "sg_w_s", "sg_bias", "sc_conv_w", "cb_rel_bias"]
    small_g = [g_ada_b_full, g_norm_sh, g_ln_sh, g_ws[None], g_sbias[None], g_cw_sh, g_rb[None]]
    small_w = [ada_b, norm_g, sg_ln_g, sg_w_s, sg_bias, sc_conv_w, cb_rel_bias]
    small_m = [m_ada_b, m_norm_g, m_sg_ln_g, m_sg_w_s, m_sg_bias, m_sc_conv_w, m_cb_rel_bias]
    small_v = [v_ada_b, v_norm_g, v_sg_ln_g, v_sg_w_s, v_sg_bias, v_sc_conv_w, v_cb_rel_bias]
    shapes = [w.shape for w in small_w]
    res = adamw(_pack(small_g)[None], _pack(small_w), _pack(small_m), _pack(small_v), "adamw_small")
    for nm_, gs, ds_, ms, vs in zip(small_names, *[_unpack(r, shapes) for r in res]):
        out_g[nm_], out_d[nm_], out_m[nm_], out_v[nm_] = gs, ds_, ms, vs

    order = ["ada_w", "ada_b", "norm_g", "ffn_w_in", "ffn_w_out", "sb_w_qkv", "sb_w_o", "sg_w_in", "sg_ln_g",
             "sg_w_s", "sg_bias", "sg_w_out", "sc_w_in", "sc_conv_w", "sc_w_out", "cb_w_qkv", "cb_rel_bias", "cb_w_o"]
    return (loss, grad_x, *[out_g[n] for n in order], *[out_d[n] for n in order],
            *[out_m[n] for n in order], *[out_v[n] for n in order])
```

```python
import jax
import jax.numpy as jnp
from jax import lax
from jax.experimental import pallas as pl
from jax.experimental.pallas import tpu as pltpu

F32 = jnp.float32
BF16 = jnp.bfloat16
MESH = pl.DeviceIdType.MESH

N_DEV = 8
D_MODEL = 1024
N_HEADS = 16
HEAD_DIM = 64
QK_SCALE = HEAD_DIM ** -0.5
BLK = 128
BAND_BLOCKS = 5
BAND_W = BAND_BLOCKS * BLK
REL_CLIP = 128
EPS = 1e-6
NEG = -1e30
GELU_C0 = 0.7978845608028654
GELU_C1 = 0.044715
ADAM_LR = 0.001
ADAM_B1 = 0.9
ADAM_B2 = 0.999
ADAM_EPS = 1e-08
ADAM_WD = 0.01
ADAM_STEP = 10
VMEM_LIMIT = 56 * 1024 * 1024


def _call(body, **kw):
    return pl.pallas_call(body, **kw)


def _params(*sem):
    return pltpu.CompilerParams(dimension_semantics=sem, vmem_limit_bytes=VMEM_LIMIT)


def _sds(shape, dtype):
    return jax.ShapeDtypeStruct(tuple(shape), dtype)


def _row_tile(t):
    return min(512, t)


def _me():
    x, y, c = lax.axis_index("x"), lax.axis_index("y"), lax.axis_index("c")
    return x, y, c


def _all_gather(arrs, name):
    n = len(arrs)

    def body(*refs):
        gather = _Gather(refs[:n], refs[n:2 * n], *refs[2 * n:])
        gather.start()
        gather.forward()
        gather.finish()

    any_spec = pl.BlockSpec(memory_space=pl.ANY)
    outs = _call(
        body,
        name=name,
        out_shape=_Gather.out_shapes(arrs),
        in_specs=[any_spec] * n,
        out_specs=[any_spec] * n,
        scratch_shapes=_comm_sems(n),
    )(*arrs)
    return list(outs)


def _comm_sems(n):
    if n == 0:
        return []
    return [pltpu.SemaphoreType.DMA((n, 7)), pltpu.SemaphoreType.DMA((n, 7)), pltpu.SemaphoreType.DMA((n,))]


class _Gather:
    def __init__(self, x_refs, o_refs, send_sems, recv_sems, local_sems):
        self.x_refs, self.o_refs = x_refs, o_refs
        self.send_sems, self.recv_sems, self.local_sems = send_sems, recv_sems, local_sems
        x, y, c = _me()
        self.c = c
        self.me, self.sibling = (x, y, c), (x, y, 1 - c)
        self.chips = [(1 - x, y), (x, 1 - y), (1 - x, 1 - y)]

    @staticmethod
    def out_shapes(arrs):
        return [_sds((N_DEV,) + a.shape, a.dtype) for a in arrs]

    def rows(self, a, block):
        px, py, pc = block
        return self.o_refs[a].at[4 * px + 2 * py + pc]

    def copy(self, a, k, block, to, own=False):
        return pltpu.make_async_remote_copy(
            src_ref=self.x_refs[a] if own else self.rows(a, block),
            dst_ref=self.rows(a, block),
            send_sem=self.send_sems.at[a, k],
            recv_sem=self.recv_sems.at[a, k],
            device_id=to,
            device_id_type=MESH,
        )

    def local(self, a):
        return pltpu.make_async_copy(self.x_refs[a], self.rows(a, self.me), self.local_sems.at[a])

    def first(self, a):
        cps = [self.copy(a, 0, self.me, self.sibling, own=True)]
        return cps + [self.copy(a, 1 + j, self.me, (*chip, self.c), own=True) for j, chip in enumerate(self.chips)]

    def passed(self, a):
        return [self.copy(a, 4 + j, (*chip, self.c), self.sibling) for j, chip in enumerate(self.chips)]

    def start(self):
        for a in range(len(self.x_refs)):
            self.local(a).start()
            for cp in self.first(a):
                cp.start()

    def forward(self):
        for a in range(len(self.x_refs)):
            passed = self.passed(a)
            for j, chip in enumerate(self.chips):
                self.copy(a, 1 + j, (*chip, self.c), self.me).wait_recv()
                passed[j].start()

    def finish(self):
        for a in range(len(self.x_refs)):
            self.copy(a, 0, self.sibling, self.me).wait_recv()
            for j, chip in enumerate(self.chips):
                self.copy(a, 4 + j, (*chip, 1 - self.c), self.me).wait_recv()
            for cp in self.first(a) + self.passed(a):
                cp.wait_send()
            self.local(a).wait()


class _Exchange:
    def __init__(self, x_refs, o_refs, send_sems, recv_sems, local_sems):
        self.x_refs, self.o_refs = x_refs, o_refs
        self.send_sems, self.recv_sems, self.local_sems = send_sems, recv_sems, local_sems
        x, y, c = _me()
        self.me = 4 * x + 2 * y + c
        self.peers = []
        for k in range(1, N_DEV):
            px = 1 - x if k & 4 else x
            py = 1 - y if k & 2 else y
            pc = 1 - c if k & 1 else c
            self.peers.append((px, py, pc))

    def local(self, a):
        return pltpu.make_async_copy(self.x_refs[a].at[self.me], self.o_refs[a].at[self.me], self.local_sems.at[a])

    def copy(self, a, k, send):
        px, py, pc = self.peers[k]
        peer = 4 * px + 2 * py + pc
        return pltpu.make_async_remote_copy(
            src_ref=self.x_refs[a].at[peer],
            dst_ref=self.o_refs[a].at[self.me if send else peer],
            send_sem=self.send_sems.at[a, k], recv_sem=self.recv_sems.at[a, k],
            device_id=(px, py, pc), device_id_type=MESH)

    def start(self):
        for a in range(len(self.x_refs)):
            self.local(a).start()
            for k in range(N_DEV - 1):
                self.copy(a, k, True).start()

    def finish(self):
        for a in range(len(self.x_refs)):
            for k in range(N_DEV - 1):
                self.copy(a, k, False).wait_recv()
            for k in range(N_DEV - 1):
                self.copy(a, k, True).wait_send()
            self.local(a).wait()


def _all_to_all(arrs, name):
    n = len(arrs)

    def body(*refs):
        exchange = _Exchange(refs[:n], refs[n:2 * n], *refs[2 * n:])
        exchange.start()
        exchange.finish()

    any_spec = pl.BlockSpec(memory_space=pl.ANY)
    outs = _call(
        body,
        name=name,
        out_shape=[_sds(a.shape, a.dtype) for a in arrs],
        in_specs=[any_spec] * n,
        out_specs=[any_spec] * n,
        scratch_shapes=_comm_sems(n),
    )(*arrs)
    return list(outs)


NN = (((1,), (0,)), ((), ()))
NT = (((1,), (1,)), ((), ()))
TN = (((0,), (0,)), ((), ()))


def _gemm(a, b, out_shape, out_dtype, grid, a_spec, b_spec, o_spec, acc_shape, dims, name):
    nk = grid[2]

    if nk == 1:
        def body(a_ref, b_ref, o_ref):
            r = lax.dot_general(a_ref[...].astype(BF16), b_ref[...].astype(BF16), dims,
                                preferred_element_type=F32)
            o_ref[...] = r.astype(o_ref.dtype)
        scratch = []
    else:
        def body(a_ref, b_ref, o_ref, acc_ref):
            k = pl.program_id(2)

            @pl.when(k == 0)
            def _():
                acc_ref[...] = jnp.zeros_like(acc_ref)

            acc_ref[...] += lax.dot_general(a_ref[...].astype(BF16), b_ref[...].astype(BF16), dims,
                                            preferred_element_type=F32)

            @pl.when(k == nk - 1)
            def _():
                o_ref[...] = acc_ref[...].astype(o_ref.dtype)
        scratch = [pltpu.VMEM(acc_shape, F32)]

    return _call(
        body, name=name, out_shape=_sds(out_shape, out_dtype), grid=grid,
        in_specs=[a_spec, b_spec], out_specs=o_spec, scratch_shapes=scratch,
        compiler_params=_params("parallel", "parallel", "arbitrary"),
    )(a, b)


def _div_tile(n, want):
    if n <= want:
        return n
    t = want - want % 128
    while n % t:
        t -= 128
    return t


def mm(a, b, name, out_dtype=F32, tm=512, tn=1024, tk=1024):
    m, k = a.shape
    n = b.shape[1]
    tm, tn, tk = _div_tile(m, tm), _div_tile(n, tn), _div_tile(k, tk)
    return _gemm(a, b, (m, n), out_dtype, (m // tm, n // tn, k // tk),
                 pl.BlockSpec((tm, tk), lambda i, j, kk: (i, kk)),
                 pl.BlockSpec((tk, tn), lambda i, j, kk: (kk, j)),
                 pl.BlockSpec((tm, tn), lambda i, j, kk: (i, j)),
                 (tm, tn), NN, name)


def mm_nt(a, b, name, out_dtype=F32, tm=512, tn=1024, tk=1024):
    m, n = a.shape
    k = b.shape[0]
    tm, tk_out, tred = _div_tile(m, tm), _div_tile(k, tn), _div_tile(n, tk)
    return _gemm(a, b, (m, k), out_dtype, (m // tm, k // tk_out, n // tred),
                 pl.BlockSpec((tm, tred), lambda i, j, kk: (i, kk)),
                 pl.BlockSpec((tk_out, tred), lambda i, j, kk: (j, kk)),
                 pl.BlockSpec((tm, tk_out), lambda i, j, kk: (i, j)),
                 (tm, tk_out), NT, name)


def mm_tn(a, b, name, out_dtype=F32, tm=512, tn=1024, tk=1024):
    m, k = a.shape
    n = b.shape[1]
    tk_out, tn, tred = _div_tile(k, tk), _div_tile(n, tn), _div_tile(m, tm)
    return _gemm(a, b, (k, n), out_dtype, (k // tk_out, n // tn, m // tred),
                 pl.BlockSpec((tred, tk_out), lambda i, j, kk: (kk, i)),
                 pl.BlockSpec((tred, tn), lambda i, j, kk: (kk, j)),
                 pl.BlockSpec((tk_out, tn), lambda i, j, kk: (i, j)),
                 (tk_out, tn), TN, name)


def mm_cs(a, wg, name, act_major=False, out_dtype=F32, tm=1024):
    m, k = a.shape
    s, _, n = wg.shape
    tm = _div_tile(m, tm)
    if act_major:
        out_shape, o_spec = (s, m, n), pl.BlockSpec((None, tm, n), lambda i, j, kk: (j, i, 0))
    else:
        out_shape, o_spec = (m, s * n), pl.BlockSpec((tm, n), lambda i, j, kk: (i, j))
    return _gemm(a, wg, out_shape, out_dtype, (m // tm, s, 1),
                 pl.BlockSpec((tm, k), lambda i, j, kk: (i, 0)),
                 pl.BlockSpec((None, k, n), lambda i, j, kk: (j, 0, 0)),
                 o_spec, (tm, n), NN, name)


def mm_cs_dx(da, wg, name, act_major=False, out_dtype=F32, tm=1024):
    s, k, n = wg.shape
    m = da.shape[1] if act_major else da.shape[0]
    tm = _div_tile(m, tm)
    if act_major:
        a_spec = pl.BlockSpec((None, tm, n), lambda i, j, kk: (kk, i, 0))
    else:
        a_spec = pl.BlockSpec((tm, n), lambda i, j, kk: (i, kk))
    return _gemm(da, wg, (m, k), out_dtype, (m // tm, 1, s), a_spec,
                 pl.BlockSpec((None, k, n), lambda i, j, kk: (kk, 0, 0)),
                 pl.BlockSpec((tm, k), lambda i, j, kk: (i, 0)),
                 (tm, k), NT, name)


def mm_cs_dw(a, da, name, act_major=False, out_dtype=F32, tm=1024):
    m, k = a.shape
    if act_major:
        s, _, n = da.shape
    else:
        s, n = N_DEV, da.shape[1] // N_DEV
    tm = _div_tile(m, tm)
    if act_major:
        b_spec = pl.BlockSpec((None, tm, n), lambda i, j, kk: (i, kk, 0))
    else:
        b_spec = pl.BlockSpec((tm, n), lambda i, j, kk: (kk, i))
    return _gemm(a, da, (s, k, n), out_dtype, (s, 1, m // tm),
                 pl.BlockSpec((tm, k), lambda i, j, kk: (kk, 0)), b_spec,
                 pl.BlockSpec((None, k, n), lambda i, j, kk: (i, 0, 0)),
                 (k, n), TN, name)


def mm_rs(s3, w3, name, out_dtype=F32, tm=1024):
    s, m, n = s3.shape
    nn = w3.shape[2]
    tm = _div_tile(m, tm)
    return _gemm(s3, w3, (m, nn), out_dtype, (m // tm, 1, s),
                 pl.BlockSpec((None, tm, n), lambda i, j, kk: (kk, i, 0)),
                 pl.BlockSpec((None, n, nn), lambda i, j, kk: (kk, 0, 0)),
                 pl.BlockSpec((tm, nn), lambda i, j, kk: (i, 0)),
                 (tm, nn), NN, name)


def mm_rs_dx(dy, w3, name, out_dtype=F32, tm=1024):
    m, nn = dy.shape
    s, n, _ = w3.shape
    tm = _div_tile(m, tm)
    return _gemm(dy, w3, (s, m, n), out_dtype, (m // tm, s, 1),
                 pl.BlockSpec((tm, nn), lambda i, j, kk: (i, 0)),
                 pl.BlockSpec((None, n, nn), lambda i, j, kk: (j, 0, 0)),
                 pl.BlockSpec((None, tm, n), lambda i, j, kk: (j, i, 0)),
                 (tm, n), NT, name)


def mm_rs_dw(s3, dy, name, out_dtype=F32, tm=1024):
    s, m, n = s3.shape
    nn = dy.shape[1]
    tm = _div_tile(m, tm)
    return _gemm(s3, dy, (s, n, nn), out_dtype, (s, 1, m // tm),
                 pl.BlockSpec((None, tm, n), lambda i, j, kk: (i, kk, 0)),
                 pl.BlockSpec((tm, nn), lambda i, j, kk: (kk, 0)),
                 pl.BlockSpec((None, n, nn), lambda i, j, kk: (i, 0, 0)),
                 (n, nn), TN, name)


def _colsum8(v):
    tr, d = v.shape
    return v.reshape(tr // 8, 8, d).sum(axis=0)


def _rstd(v):
    return lax.rsqrt(jnp.mean(v * v, axis=-1, keepdims=True) + EPS)


def _vec_spec(d):
    return pl.BlockSpec((1, d), lambda i: (0, 0))


def _acc_spec(d):
    return pl.BlockSpec((8, d), lambda i: (0, 0))


def _pre_rows(xv, g, shift, scale):
    return ((xv * _rstd(xv)) * g) * (1 + scale) + shift


def _post_rows(xv, yv, g, gate):
    return xv + gate * ((yv * _rstd(yv)) * g)


def _post_bwd_rows(dxv, yv, g, gate):
    r = _rstd(yv)
    yhat = yv * r
    dgate = _colsum8(dxv * (yhat * g))
    dyn = gate * dxv
    dg = _colsum8(dyn * yhat)
    dyhat = dyn * g
    dy = r * (dyhat - yhat * jnp.mean(dyhat * yhat, axis=-1, keepdims=True))
    return dy, dgate, dg


def _pre_bwd_rows(dhv, xv, g, scale, dxn):
    r = _rstd(xv)
    xhat = xv * r
    dshift = _colsum8(dhv)
    dscale = _colsum8(dhv * (xhat * g))
    dmod = dhv * (1 + scale)
    dg = _colsum8(dmod * xhat)
    dxhat = dmod * g
    dx = r * (dxhat - xhat * jnp.mean(dxhat * xhat, axis=-1, keepdims=True)) + dxn
    return dx, dshift, dscale, dg


def _row_call(body, name, t, d, rows_in, vecs_in, rows_out, n_acc):
    tr = _row_tile(t)
    nri, nvi, nro = len(rows_in), len(vecs_in), len(rows_out)

    def wrapped(*refs):
        accs = refs[nri + nvi + nro:]
        if n_acc:
            @pl.when(pl.program_id(0) == 0)
            def _():
                for acc in accs:
                    acc[...] = jnp.zeros_like(acc)
        body(*refs)

    row = pl.BlockSpec((tr, d), lambda i: (i, 0))
    return _call(wrapped, name=name,
                 out_shape=[_sds((t, d), dt) for dt in rows_out] + [_sds((8, d), F32)] * n_acc,
                 grid=(t // tr,), in_specs=[row] * nri + [_vec_spec(d)] * nvi,
                 out_specs=[row] * nro + [_acc_spec(d)] * n_acc,
                 compiler_params=_params("arbitrary" if n_acc else "parallel"))(*rows_in, *vecs_in)


def pre_fwd(x, g, shift, scale, name):
    def body(x_ref, g_ref, sh_ref, sc_ref, h_ref):
        h_ref[...] = _pre_rows(x_ref[...], g_ref[...], sh_ref[...], sc_ref[...]).astype(BF16)

    return _row_call(body, name, *x.shape, [x], [g, shift, scale], [BF16], 0)[0]


def post_pre_fwd(x, y, g_post, gate, g_pre, shift, scale, name):
    def body(x_ref, y_ref, gp_ref, gt_ref, g_ref, sh_ref, sc_ref, xn_ref, h_ref):
        xn = _post_rows(x_ref[...], y_ref[...], gp_ref[...], gt_ref[...])
        xn_ref[...] = xn
        h_ref[...] = _pre_rows(xn, g_ref[...], sh_ref[...], sc_ref[...]).astype(BF16)

    return _row_call(body, name, *x.shape, [x, y], [g_post, gate, g_pre, shift, scale], [F32, BF16], 0)


def post_loss_bwd(x, y, g, gate, target, name):
    d = x.shape[1]

    def body(x_ref, y_ref, t_ref, g_ref, gt_ref, dx_ref, dy_ref, l_ref, dgate_ref, dg_ref):
        yv, gv, gate_v = y_ref[...], g_ref[...], gt_ref[...]
        err = _post_rows(x_ref[...], yv, gv, gate_v) - t_ref[...]
        l_ref[...] += _colsum8(err * err)
        dxv = err * (1.0 / d)
        dx_ref[...] = dxv
        dy, dgate, dg = _post_bwd_rows(dxv, yv, gv, gate_v)
        dy_ref[...] = dy.astype(BF16)
        dgate_ref[...] += dgate
        dg_ref[...] += dg

    return _row_call(body, name, *x.shape, [x, y, target], [g, gate], [F32, BF16], 3)


def pre_post_bwd(dh, x, g_pre, scale, dxn, y, g_post, gate, name):
    def body(dh_ref, x_ref, dxn_ref, y_ref, g_ref, sc_ref, gp_ref, gt_ref,
             dx_ref, dy_ref, dsh_ref, dsc_ref, dg_ref, dgate_ref, dgp_ref):
        dx, dsh, dsc, dg = _pre_bwd_rows(dh_ref[...].astype(F32), x_ref[...], g_ref[...], sc_ref[...], dxn_ref[...])
        dx_ref[...] = dx
        dsh_ref[...] += dsh
        dsc_ref[...] += dsc
        dg_ref[...] += dg
        dy, dgate, dgp = _post_bwd_rows(dx, y_ref[...], gp_ref[...], gt_ref[...])
        dy_ref[...] = dy.astype(BF16)
        dgate_ref[...] += dgate
        dgp_ref[...] += dgp

    return _row_call(body, name, *x.shape, [dh, x, dxn, y], [g_pre, scale, g_post, gate], [F32, BF16], 5)


def pre_bwd(dh, x, g, scale, dxn, name):
    def body(dh_ref, x_ref, dxn_ref, g_ref, sc_ref, dx_ref, dsh_ref, dsc_ref, dg_ref):
        dx, dsh, dsc, dg = _pre_bwd_rows(dh_ref[...].astype(F32), x_ref[...], g_ref[...], sc_ref[...], dxn_ref[...])
        dx_ref[...] = dx
        dsh_ref[...] += dsh
        dsc_ref[...] += dsc
        dg_ref[...] += dg

    return _row_call(body, name, *x.shape, [dh, x, dxn], [g, scale], [F32], 3)


def _sigmoid(x):
    return 1.0 / (1.0 + jnp.exp(-x))


def ffn_in_swiglu(h, wg, name, tm=1024):
    m, k = h.shape
    s, _, n = wg.shape
    half = s // 2
    tm = _div_tile(m, tm)

    def body(h_ref, wg_ref, wu_ref, g_ref, u_ref, s_ref):
        hv = h_ref[...]
        g = jnp.dot(hv, wg_ref[...], preferred_element_type=F32)
        u = jnp.dot(hv, wu_ref[...], preferred_element_type=F32)
        g_ref[...] = g
        u_ref[...] = u
        s_ref[...] = ((g * _sigmoid(g)) * u).astype(BF16)

    act = pl.BlockSpec((None, tm, n), lambda i, j: (j, i, 0))
    return _call(
        body, name=name, out_shape=[_sds((half, m, n), F32), _sds((half, m, n), F32), _sds((half, m, n), BF16)],
        grid=(m // tm, half),
        in_specs=[pl.BlockSpec((tm, k), lambda i, j: (i, 0)),
                  pl.BlockSpec((None, k, n), lambda i, j: (j, 0, 0)),
                  pl.BlockSpec((None, k, n), lambda i, j: (j + half, 0, 0))],
        out_specs=[act, act, act], compiler_params=_params("parallel", "parallel"),
    )(h, wg, wg)


def ffn_out_dx_swiglu(dy, w4, gate, up, name, tm=1024):
    m, nn = dy.shape
    half, n, _ = w4.shape
    tm = _div_tile(m, tm)

    def body(dy_ref, w_ref, g_ref, u_ref, o_ref):
        ds = lax.dot_general(dy_ref[...], w_ref[...], NT, preferred_element_type=F32)
        g, u = g_ref[...], u_ref[...]
        sig = _sigmoid(g)
        o_ref[0] = (ds * u * (sig * (1 + g * (1 - sig)))).astype(BF16)
        o_ref[1] = (ds * (g * sig)).astype(BF16)

    act = pl.BlockSpec((None, tm, n), lambda i, j: (j, i, 0))
    out = _call(
        body, name=name, out_shape=_sds((2, half, m, n), BF16), grid=(m // tm, half),
        in_specs=[pl.BlockSpec((tm, nn), lambda i, j: (i, 0)),
                  pl.BlockSpec((None, n, nn), lambda i, j: (j, 0, 0)), act, act],
        out_specs=pl.BlockSpec((2, None, tm, n), lambda i, j: (0, j, i, 0)),
        compiler_params=_params("parallel", "parallel"),
    )(dy, w4, gate, up)
    return out.reshape(2 * half, m, n)


def _split_hi_lo(v):
    hi = v.astype(BF16)
    lo = (v - hi.astype(F32)).astype(BF16)
    return hi, lo


SB_G = 4
SB_GW = SB_G * BLK


def _sb_specs(t):
    nq = t // BLK
    npair = N_HEADS // 2
    q_spec = pl.BlockSpec((BLK, BLK), lambda p, qb: (qb, p))
    k_spec = pl.BlockSpec((t, BLK), lambda p, qb: (0, npair + p))
    v_spec = pl.BlockSpec((t, BLK), lambda p, qb: (0, 2 * npair + p))
    kt_spec = pl.BlockSpec((t // SB_GW, BLK, SB_GW), lambda p, qb: (0, npair + p, 0))
    vt_spec = pl.BlockSpec((t // SB_GW, BLK, SB_GW), lambda p, qb: (0, 2 * npair + p, 0))
    c_spec = pl.BlockSpec((None, nq, 8, 2 * BLK), lambda p, qb: (p, 0, 0, qb))
    return nq, npair, q_spec, k_spec, v_spec, kt_spec, vt_spec, c_spec


def _sb_consts():
    row = lax.broadcasted_iota(jnp.int32, (BLK, BLK), 0)
    col = lax.broadcasted_iota(jnp.int32, (BLK, BLK), 1)
    lane0 = (col < HEAD_DIM).astype(F32)
    sub0 = (row < HEAD_DIM).astype(F32)
    return row, col, lane0, sub0


def _sb_valid(ks, qb):
    row = lax.broadcasted_iota(jnp.int32, (SB_GW, 2 * BLK), 0)
    col = lax.broadcasted_iota(jnp.int32, (SB_GW, 2 * BLK), 1)
    return (ks + row) < (qb * BLK + (col & (BLK - 1)))


def _blocks_on_lanes(v4):
    return jnp.concatenate([v4[b * BLK:(b + 1) * BLK] for b in range(SB_G)], axis=1)


def _tri2_dot(tri2, v):
    hi, lo = _split_hi_lo(v)
    return jnp.dot(tri2, jnp.concatenate([hi, lo], axis=0), preferred_element_type=F32)


def _sb_pair_loop(first, count, step, group, skip, carry):
    def pair(it, cy):
        g1 = first + 2 * step * it
        cy = group(g1, 0, 1, cy)
        return lax.cond(2 * it + 1 < count, lambda c: group(g1 + step, 1, 0, c), skip, cy)
    return lax.fori_loop(0, (count + 1) // 2, pair, carry)


def sb_fwd(qkv, qkv_t, riders, name):
    t = qkv.shape[0]
    assert t % SB_GW == 0
    nq, npair, q_spec, k_spec, _, _, vt_spec, c_spec = _sb_specs(t)
    nr = len(riders)

    def body(*refs):
        q_ref, k_ref, vt_ref = refs[:3]
        o_ref, c_ref = refs[3 + nr:5 + nr]
        oacc, zbuf0, zbuf1 = refs[5 + 2 * nr:8 + 2 * nr]
        pp = pl.program_id(0)
        qb = pl.program_id(1)
        if nr:
            gather = _Gather(refs[3:3 + nr], refs[5 + nr:5 + 2 * nr], *refs[8 + 2 * nr:])
            pl.when((pp == 0) & (qb == 0))(gather.start)
            pl.when((pp == npair - 2) & (qb == 0))(gather.forward)
        _sb_fwd_step(q_ref, k_ref, vt_ref, o_ref, c_ref, oacc, zbuf0, zbuf1, qb)
        if nr:
            pl.when((pp == npair - 1) & (qb == nq - 1))(gather.finish)

    any_spec = pl.BlockSpec(memory_space=pl.ANY)
    outs = _call(
        body, name=name,
        out_shape=[_sds((t, D_MODEL), BF16), _sds((npair, nq, 8, 2 * t), F32)] + _Gather.out_shapes(riders),
        grid=(npair, nq), in_specs=[q_spec, k_spec, vt_spec] + [any_spec] * nr,
        out_specs=[pl.BlockSpec((BLK, BLK), lambda p, qb: (qb, p)), c_spec] + [any_spec] * nr,
        scratch_shapes=[pltpu.VMEM((BLK, 2 * BLK), F32), pltpu.VMEM((SB_GW, 2 * BLK), F32),
                        pltpu.VMEM((SB_GW, 2 * BLK), F32)] + _comm_sems(nr),
        compiler_params=_params("arbitrary", "arbitrary"),
    )(qkv, qkv, qkv_t, *riders)
    return outs[0], outs[1], list(outs[2:])


def _sb_fwd_step(q_ref, k_ref, vt_ref, o_ref, c_ref, oacc, zbuf0, zbuf1, qb):
    row, col, lane0, sub0 = _sb_consts()
    tri = (col >= row).astype(BF16)
    tri2 = jnp.concatenate([tri, tri], axis=1)
    q2 = _two_heads(q_ref[...], lane0, QK_SCALE)
    zbufs = (zbuf0, zbuf1)
    c_ref[...] = jnp.zeros_like(c_ref)
    oacc[...] = jnp.zeros_like(oacc)

    def scores(g):
        ks = pl.multiple_of(g * SB_GW, SB_GW)
        return lax.dot_general(k_ref[pl.ds(ks, SB_GW), :], q2, NT, preferred_element_type=F32)

    def group(g, cur, nxt, cr, masked=False):
        z = zbufs[cur][...]
        zbufs[nxt][...] = scores(jnp.maximum(g - 1, 0))
        e = jnp.exp(-jnp.abs(z))
        sp = jnp.maximum(z, 0.0) + jnp.log(1.0 + e)
        if masked:
            valid = _sb_valid(g * SB_GW, qb)
            sp = jnp.where(valid, sp, 0.0)
        loc = _tri2_dot(tri2, _blocks_on_lanes(sp))
        parts = [None] * SB_G
        for b in reversed(range(SB_G)):
            rows = slice(b * BLK, (b + 1) * BLK)
            c_ref[g * SB_G + b] = jnp.broadcast_to(cr, (8, 2 * BLK))
            a = jnp.exp(z[rows] - (loc[:, 2 * b * BLK:2 * (b + 1) * BLK] + cr))
            if masked:
                a = jnp.where(valid[rows], a, 0.0)
            parts[b] = a.astype(BF16)
            cr = cr + jnp.sum(sp[rows], axis=0, keepdims=True)
        oacc[...] += jnp.dot(vt_ref[g], jnp.concatenate(parts, axis=0), preferred_element_type=F32)
        return cr

    last = qb // SB_G
    zbuf1[...] = scores(last)
    cr = group(last, 1, 0, jnp.zeros((1, 2 * BLK), F32), masked=True)
    _sb_pair_loop(last - 1, last, -1, group, lambda c: c, cr)
    o_t = oacc[:, :BLK] * sub0 + oacc[:, BLK:] * (1.0 - sub0)
    o_ref[...] = o_t.T.astype(BF16)


def sb_bwd(qkv, qkv_t, do, cmass, riders, name):
    t = qkv.shape[0]
    nq, npair, q_spec, k_spec, v_spec, kt_spec, _, c_spec = _sb_specs(t)
    nr = len(riders)

    def body(*refs):
        pp = pl.program_id(0)
        qb = pl.program_id(1)
        if nr:
            exchange = _Exchange(refs[6:6 + nr], refs[9 + nr:9 + 2 * nr], *refs[14 + 2 * nr:])
            pl.when((pp == 0) & (qb == 0))(exchange.start)
        step(*refs[:6], *refs[6 + nr:9 + nr], *refs[9 + 2 * nr:14 + 2 * nr])
        if nr:
            pl.when((pp == npair - 1) & (qb == nq - 1))(exchange.finish)

    def step(q_ref, k_ref, kt_ref, v_ref, do_ref, c_ref, dq_ref, dk_ref, dv_ref, dqacc, dkacc, dvacc,
             zbuf0, zbuf1):
        qb = pl.program_id(1)

        @pl.when(qb == 0)
        def _():
            dkacc[...] = jnp.zeros_like(dkacc)
            dvacc[...] = jnp.zeros_like(dvacc)

        row, col, lane0, sub0 = _sb_consts()
        tri_suf = (col >= row).astype(BF16)
        tri_pre = (col <= row).astype(BF16)
        tri2_suf = jnp.concatenate([tri_suf, tri_suf], axis=1)
        tri2_pre = jnp.concatenate([tri_pre, tri_pre], axis=1)
        q2 = _two_heads(q_ref[...], lane0, QK_SCALE)
        do2 = _two_heads(do_ref[...], lane0, 1.0)
        zbufs = (zbuf0, zbuf1)
        dqacc[...] = jnp.zeros_like(dqacc)
        last = qb // SB_G

        def scores(g):
            ks = pl.multiple_of(g * SB_GW, SB_GW)
            return lax.dot_general(k_ref[pl.ds(ks, SB_GW), :], q2, NT, preferred_element_type=F32)

        def group(g, cur, nxt, gc, masked=False):
            ks = pl.multiple_of(g * SB_GW, SB_GW)
            z = zbufs[cur][...]
            zbufs[nxt][...] = scores(jnp.minimum(g + 1, last))
            e = jnp.exp(-jnp.abs(z))
            sig = 0.5 * jnp.tanh(0.5 * z) + 0.5
            sp = jnp.maximum(z, 0.0) + jnp.log(1.0 + e)
            if masked:
                valid = _sb_valid(ks, qb)
                sp = jnp.where(valid, sp, 0.0)
            loc = _tri2_dot(tri2_suf, _blocks_on_lanes(sp))
            parts = []
            for b in range(SB_G):
                rows = slice(b * BLK, (b + 1) * BLK)
                mass = loc[:, 2 * b * BLK:2 * (b + 1) * BLK] + c_ref[g * SB_G + b, 0:1, :]
                parts.append(jnp.exp(z[rows] - mass))
            a = jnp.concatenate(parts, axis=0)
            if masked:
                a = jnp.where(valid, a, 0.0)
            gr = lax.dot_general(v_ref[pl.ds(ks, SB_GW), :], do2, NT, preferred_element_type=F32) * a
            pre = _tri2_dot(tri2_pre, _blocks_on_lanes(gr))
            parts = []
            for b in range(SB_G):
                rows = slice(b * BLK, (b + 1) * BLK)
                parts.append(pre[:, 2 * b * BLK:2 * (b + 1) * BLK] + gc)
                gc = gc + jnp.sum(gr[rows], axis=0, keepdims=True)
            dz = gr - sig * jnp.concatenate(parts, axis=0)
            if masked:
                dz = jnp.where(valid, dz, 0.0)
            dz = dz.astype(BF16)
            dkacc[pl.ds(ks, SB_GW), :] += jnp.dot(dz, q2, preferred_element_type=F32)
            dqacc[...] += jnp.dot(kt_ref[g], dz, preferred_element_type=F32)
            dvacc[pl.ds(ks, SB_GW), :] += jnp.dot(a.astype(BF16), do2, preferred_element_type=F32)
            return gc

        def skip(gc):
            zbuf0[...] = zbuf1[...]
            return gc

        zbuf0[...] = scores(0)
        gc = _sb_pair_loop(0, last, 1, group, skip, jnp.zeros((1, 2 * BLK), F32))
        group(last, 0, 1, gc, masked=True)
        dq_t = (dqacc[:, :BLK] * sub0 + dqacc[:, BLK:] * (1.0 - sub0)) * QK_SCALE
        dq_ref[...] = dq_t.T.astype(BF16)

        @pl.when(qb == nq - 1)
        def _():
            dk_ref[...] = dkacc[...].astype(BF16)
            dv_ref[...] = dvacc[...].astype(BF16)

    col_spec = pl.BlockSpec((t, BLK), lambda p, qb: (0, p))
    blk_spec = pl.BlockSpec((BLK, BLK), lambda p, qb: (qb, p))
    any_spec = pl.BlockSpec(memory_space=pl.ANY)
    outs = _call(
        body, name=name,
        out_shape=[_sds((t, D_MODEL), BF16)] * 3 + [_sds(r.shape, r.dtype) for r in riders],
        grid=(npair, nq), in_specs=[q_spec, k_spec, kt_spec, v_spec, blk_spec, c_spec] + [any_spec] * nr,
        out_specs=[blk_spec, col_spec, col_spec] + [any_spec] * nr,
        scratch_shapes=[pltpu.VMEM((BLK, 2 * BLK), F32), pltpu.VMEM((t, BLK), F32), pltpu.VMEM((t, BLK), F32),
                        pltpu.VMEM((SB_GW, 2 * BLK), F32), pltpu.VMEM((SB_GW, 2 * BLK), F32)] + _comm_sems(nr),
        compiler_params=_params("arbitrary", "arbitrary"),
    )(qkv, qkv, qkv_t, qkv, do, cmass, *riders)
    return outs[0], outs[1], outs[2], list(outs[3:])


BAND_QPS = 2


def _band_static_mask(jj):
    row = lax.broadcasted_iota(jnp.int32, (2 * BLK, BLK), 0)
    col = lax.broadcasted_iota(jnp.int32, (2 * BLK, BLK), 1)
    qc = (row & (BLK - 1)) // 64
    kc = 2 * jj + col // 64
    return (kc >= qc) & (kc <= qc + 8)


def _band_key_start(qb, jj):
    kb = qb - (BAND_BLOCKS - 1) + jj
    return kb, pl.multiple_of(jnp.maximum(kb, 0) * BLK, BLK)


def _band_probs(q2, k_ref, bias, qb):
    blocks = []
    for jj in range(BAND_BLOCKS):
        kb, ks = _band_key_start(qb, jj)
        s = lax.dot_general(q2, k_ref[pl.ds(ks, BLK), :], NT, preferred_element_type=F32)
        s = s + bias[:, jj * BLK:(jj + 1) * BLK]
        ok = (kb >= 0) if 0 < jj < BAND_BLOCKS - 1 else _band_static_mask(jj) & (kb >= 0)
        blocks.append(jnp.where(ok, s, NEG))
    s = jnp.concatenate(blocks, axis=1)
    m = jnp.max(s, axis=-1, keepdims=True)
    e = jnp.exp(s - m)
    return e / jnp.sum(e, axis=-1, keepdims=True)


def _band_specs(t):
    npair = N_HEADS // 2
    rows = BAND_QPS * BLK
    q_spec = pl.BlockSpec((rows, BLK), lambda p, i: (i, p))
    k_spec = pl.BlockSpec((t, BLK), lambda p, i: (0, npair + p))
    v_spec = pl.BlockSpec((t, BLK), lambda p, i: (0, 2 * npair + p))
    b_spec = pl.BlockSpec((2, BLK, BAND_W), lambda p, i: (p, 0, 0))
    return npair, t // rows, q_spec, k_spec, v_spec, b_spec


def _two_heads(xv, lane0, scale):
    xf = xv.astype(F32)
    if scale != 1.0:
        xf = xf * scale
    return jnp.concatenate([xf * lane0, xf * (1.0 - lane0)], axis=0).astype(BF16)


def _one_of_two_heads(r, lane0):
    return r[:BLK] * lane0 + r[BLK:] * (1.0 - lane0)


def band_fwd(qkv, bias, name):
    t = qkv.shape[0]
    assert t % (BAND_QPS * BLK) == 0
    npair, nsteps, q_spec, k_spec, v_spec, b_spec = _band_specs(t)

    def body(q_ref, k_ref, v_ref, b_ref, o_ref):
        step = pl.program_id(1)
        _, _, lane0, _ = _sb_consts()
        bias2 = b_ref[...].reshape(2 * BLK, BAND_W)
        for u in range(BAND_QPS):
            qb = step * BAND_QPS + u
            rows = slice(u * BLK, (u + 1) * BLK)
            q2 = _two_heads(q_ref[rows, :], lane0, QK_SCALE)
            p = _band_probs(q2, k_ref, bias2, qb)
            acc = jnp.zeros((2 * BLK, BLK), F32)
            for jj in range(BAND_BLOCKS):
                _, ks = _band_key_start(qb, jj)
                acc += jnp.dot(p[:, jj * BLK:(jj + 1) * BLK].astype(BF16), v_ref[pl.ds(ks, BLK), :],
                               preferred_element_type=F32)
            o_ref[rows, :] = _one_of_two_heads(acc, lane0).astype(BF16)

    return _call(
        body, name=name, out_shape=_sds((t, D_MODEL), BF16), grid=(npair, nsteps),
        in_specs=[q_spec, k_spec, v_spec, b_spec],
        out_specs=pl.BlockSpec((BAND_QPS * BLK, BLK), lambda p, i: (i, p)),
        compiler_params=_params("parallel", "parallel"),
    )(qkv, qkv, qkv, bias)


def band_bwd(qkv, do, bias, name):
    t = qkv.shape[0]
    npair, nsteps, q_spec, k_spec, v_spec, b_spec = _band_specs(t)

    def body(q_ref, k_ref, v_ref, do_ref, b_ref, dq_ref, dk_ref, dv_ref, db_ref, dkacc, dvacc):
        step = pl.program_id(1)

        @pl.when(step == 0)
        def _():
            dkacc[...] = jnp.zeros_like(dkacc)
            dvacc[...] = jnp.zeros_like(dvacc)
            db_ref[...] = jnp.zeros_like(db_ref)

        _, _, lane0, _ = _sb_consts()
        bias2 = b_ref[...].reshape(2 * BLK, BAND_W)
        updates = []
        for u in range(BAND_QPS):
            qb = step * BAND_QPS + u
            rows = slice(u * BLK, (u + 1) * BLK)
            q2 = _two_heads(q_ref[rows, :], lane0, QK_SCALE)
            do2 = _two_heads(do_ref[rows, :], lane0, 1.0)
            p = _band_probs(q2, k_ref, bias2, qb)
            dp = jnp.concatenate(
                [lax.dot_general(do2, v_ref[pl.ds(_band_key_start(qb, jj)[1], BLK), :], NT,
                                 preferred_element_type=F32) for jj in range(BAND_BLOCKS)], axis=1)
            ds = p * (dp - jnp.sum(p * dp, axis=-1, keepdims=True))
            db_ref[...] += ds.reshape(2, BLK, BAND_W)
            dqa = jnp.zeros((2 * BLK, BLK), F32)
            for jj in range(BAND_BLOCKS):
                _, ks = _band_key_start(qb, jj)
                dsb = ds[:, jj * BLK:(jj + 1) * BLK].astype(BF16)
                pb = p[:, jj * BLK:(jj + 1) * BLK].astype(BF16)
                dqa += jnp.dot(dsb, k_ref[pl.ds(ks, BLK), :], preferred_element_type=F32)
                updates.append((ks, lax.dot_general(dsb, q2, TN, preferred_element_type=F32),
                                lax.dot_general(pb, do2, TN, preferred_element_type=F32)))
            dq_ref[rows, :] = (_one_of_two_heads(dqa, lane0) * QK_SCALE).astype(BF16)
        for ks, dk_part, dv_part in updates:
            dkacc[pl.ds(ks, BLK), :] += dk_part
            dvacc[pl.ds(ks, BLK), :] += dv_part

        @pl.when(step == nsteps - 1)
        def _():
            dk_ref[...] = dkacc[...].astype(BF16)
            dv_ref[...] = dvacc[...].astype(BF16)

    col_spec = pl.BlockSpec((t, BLK), lambda p, i: (0, p))
    blk_spec = pl.BlockSpec((BAND_QPS * BLK, BLK), lambda p, i: (i, p))
    return _call(
        body, name=name,
        out_shape=[_sds((t, D_MODEL), BF16)] * 3 + [_sds((N_HEADS, BLK, BAND_W), F32)],
        grid=(npair, nsteps), in_specs=[q_spec, k_spec, v_spec, blk_spec, b_spec],
        out_specs=[blk_spec, col_spec, col_spec, b_spec],
        scratch_shapes=[pltpu.VMEM((t, BLK), F32), pltpu.VMEM((t, BLK), F32)],
        compiler_params=_params("parallel", "arbitrary"),
    )(qkv, qkv, qkv, do, bias)


def band_bias_window(rel_bias):
    far = BAND_W + BLK - 1 - 2 * REL_CLIP
    width = BAND_W + BLK
    ext = jnp.concatenate(
        [jnp.broadcast_to(rel_bias[:, 2 * REL_CLIP:], (N_HEADS, far)), rel_bias[:, 2 * REL_CLIP:0:-1],
         jnp.zeros((N_HEADS, 2), F32)], axis=1)
    tiled = jnp.broadcast_to(ext[:, None, :], (N_HEADS, BLK, width + 1)).reshape(N_HEADS, BLK * (width + 1))
    return tiled[:, BLK - 1:BLK - 1 + BLK * width].reshape(N_HEADS, BLK, width)[:, :, :BAND_W]


def band_bias_window_grad(dwin):
    width = BAND_W + BLK
    far = BAND_W + BLK - 1 - 2 * REL_CLIP
    flat = jnp.pad(dwin, ((0, 0), (0, 0), (0, BLK))).reshape(N_HEADS, BLK * width)
    skew = jnp.pad(flat, ((0, 0), (BLK - 1, 1))).reshape(N_HEADS, BLK, width + 1)
    dext = jnp.sum(skew, axis=1)[:, :width - 1]
    return jnp.concatenate(
        [jnp.zeros((N_HEADS, 1), F32), dext[:, :far - 1:-1][:, :2 * REL_CLIP - 1],
         dext[:, far:far + 1] + jnp.sum(dext[:, :far], axis=1, keepdims=True)], axis=1)


SG_GROUPS = 8


def _gelu_parts(x):
    inner = GELU_C0 * (x + GELU_C1 * (x * x * x))
    th = jnp.tanh(inner)
    return th, 0.5 * x * (1.0 + th)


def _sg_gate_mask():
    row = lax.broadcasted_iota(jnp.int32, (BLK, BLK), 0)
    col = lax.broadcasted_iota(jnp.int32, (BLK, BLK), 1)
    return (row // 64) >= (col // 64)


def _sg_forward_parts(a, lng):
    w = a.shape[1] // 2
    th, z = _gelu_parts(a)
    u, v = z[:, :w], z[:, w:]
    mu = jnp.mean(v, axis=-1, keepdims=True)
    xc = v - mu
    rstd = lax.rsqrt(jnp.mean(xc * xc, axis=-1, keepdims=True) + EPS)
    vhat = xc * rstd
    return th, u, vhat, rstd, vhat * lng


def sg_fwd(a, lng, ws, bias_t, name):
    t, w2 = a.shape
    w = w2 // 2
    gc = w // SG_GROUPS

    def body(a_ref, lng_ref, ws_ref, bt_ref, y_ref):
        _, u, _, _, vln = _sg_forward_parts(a_ref[...], lng_ref[...])
        mask = _sg_gate_mask()
        bt = bt_ref[...]
        lane = lax.broadcasted_iota(jnp.int32, (BLK, BLK), 1)
        for g in range(SG_GROUPS):
            wg = jnp.where(mask, ws_ref[g], 0.0).astype(BF16)
            sv = jnp.dot(wg, vln[:, g * gc:(g + 1) * gc].astype(BF16), preferred_element_type=F32)
            bg = jnp.sum(jnp.where(lane == g, bt, 0.0), axis=-1, keepdims=True)
            y_ref[:, g * gc:(g + 1) * gc] = (u[:, g * gc:(g + 1) * gc] * (sv + bg)).astype(BF16)

    return _call(
        body, name=name, out_shape=_sds((t, w), BF16), grid=(t // BLK,),
        in_specs=[pl.BlockSpec((BLK, w2), lambda i: (i, 0)), pl.BlockSpec((1, w), lambda i: (0, 0)),
                  pl.BlockSpec((SG_GROUPS, BLK, BLK), lambda i: (0, 0, 0)),
                  pl.BlockSpec((BLK, BLK), lambda i: (0, 0))],
        out_specs=pl.BlockSpec((BLK, w), lambda i: (i, 0)),
        compiler_params=_params("parallel"),
    )(a, lng, ws, bias_t)


def sg_bwd(a, dy, lng, ws, bias_t, name):
    t, w2 = a.shape
    w = w2 // 2
    gc = w // SG_GROUPS

    def body(a_ref, dy_ref, lng_ref, ws_ref, bt_ref, da_ref, dlng_ref, dws_ref, dbt_ref):
        @pl.when(pl.program_id(0) == 0)
        def _():
            dlng_ref[...] = jnp.zeros_like(dlng_ref)
            dws_ref[...] = jnp.zeros_like(dws_ref)
            dbt_ref[...] = jnp.zeros_like(dbt_ref)

        av, lng = a_ref[...], lng_ref[...]
        th, u, vhat, rstd, vln = _sg_forward_parts(av, lng)
        mask = _sg_gate_mask()
        bt = bt_ref[...]
        lane = lax.broadcasted_iota(jnp.int32, (BLK, BLK), 1)
        dyv = dy_ref[...]
        du_parts, dvln_parts = [], []
        dbt = jnp.zeros((BLK, BLK), F32)
        for g in range(SG_GROUPS):
            sl = slice(g * gc, (g + 1) * gc)
            wg = jnp.where(mask, ws_ref[g], 0.0).astype(BF16)
            vg = vln[:, sl].astype(BF16)
            sv = jnp.dot(wg, vg, preferred_element_type=F32)
            bg = jnp.sum(jnp.where(lane == g, bt, 0.0), axis=-1, keepdims=True)
            dyg = dyv[:, sl]
            du_parts.append(dyg * (sv + bg))
            dsv = dyg * u[:, sl]
            dbt += jnp.where(lane == g, jnp.sum(dsv, axis=-1, keepdims=True), 0.0)
            dsvb = dsv.astype(BF16)
            dws_ref[g] += jnp.where(mask, lax.dot_general(dsvb, vg, NT, preferred_element_type=F32), 0.0)
            dvln_parts.append(lax.dot_general(wg, dsvb, TN, preferred_element_type=F32))
        dbt_ref[...] += dbt
        du = jnp.concatenate(du_parts, axis=1)
        dvln = jnp.concatenate(dvln_parts, axis=1)
        dlng_ref[...] += _colsum8(dvln * vhat)
        dvhat = dvln * lng
        dv = rstd * (dvhat - jnp.mean(dvhat, axis=-1, keepdims=True)
                     - vhat * jnp.mean(dvhat * vhat, axis=-1, keepdims=True))
        dz = jnp.concatenate([du, dv], axis=1)
        dgelu = 0.5 * (1.0 + th) + (0.5 * av) * (1.0 - th * th) * (GELU_C0 * (1.0 + 3.0 * GELU_C1 * (av * av)))
        da_ref[...] = (dz * dgelu).astype(BF16)

    return _call(
        body, name=name,
        out_shape=[_sds((t, w2), BF16), _sds((8, w), F32), _sds((SG_GROUPS, BLK, BLK), F32), _sds((BLK, BLK), F32)],
        grid=(t // BLK,),
        in_specs=[pl.BlockSpec((BLK, w2), lambda i: (i, 0)), pl.BlockSpec((BLK, w), lambda i: (i, 0)),
                  pl.BlockSpec((1, w), lambda i: (0, 0)),
                  pl.BlockSpec((SG_GROUPS, BLK, BLK), lambda i: (0, 0, 0)),
                  pl.BlockSpec((BLK, BLK), lambda i: (0, 0))],
        out_specs=[pl.BlockSpec((BLK, w2), lambda i: (i, 0)), pl.BlockSpec((8, w), lambda i: (0, 0)),
                   pl.BlockSpec((SG_GROUPS, BLK, BLK), lambda i: (0, 0, 0)),
                   pl.BlockSpec((BLK, BLK), lambda i: (0, 0))],
        compiler_params=_params("arbitrary"),
    )(a, dy, lng, ws, bias_t)


def _shift_down(cat, n, tr):
    return pltpu.roll(cat, n, 0)[8:8 + tr]


def _shift_up(cat, n, tr):
    return pltpu.roll(cat, tr + 8 - n, 0)[0:tr]


def conv_fwd(p, cw, name):
    t, d3 = p.shape
    d = d3 // 3
    tr = min(256, t)
    hb = tr // 8

    def body(p_ref, ph_ref, cw_ref, o_ref):
        i = pl.program_id(0)
        pv = p_ref[...]
        y = pv[:, d:2 * d] * pv[:, 2 * d:]
        ph = ph_ref[...]
        yh = jnp.where(i > 0, ph[:, d:2 * d] * ph[:, 2 * d:], 0.0)
        cat = jnp.concatenate([yh, y], axis=0)
        yc = (cw_ref[0:1, :] * _shift_down(cat, 2, tr) + cw_ref[1:2, :] * _shift_down(cat, 1, tr)
              + cw_ref[2:3, :] * y)
        o_ref[...] = (pv[:, :d] * yc).astype(BF16)

    return _call(
        body, name=name, out_shape=_sds((t, d), BF16), grid=(t // tr,),
        in_specs=[pl.BlockSpec((tr, d3), lambda i: (i, 0)),
                  pl.BlockSpec((8, d3), lambda i: (jnp.maximum(i * hb - 1, 0), 0)),
                  pl.BlockSpec((8, d), lambda i: (0, 0))],
        out_specs=pl.BlockSpec((tr, d), lambda i: (i, 0)),
        compiler_params=_params("parallel"),
    )(p, p, cw)


def conv_bwd(p, dz, cw, name):
    t, d3 = p.shape
    d = d3 // 3
    tr = min(256, t)
    hb = tr // 8
    nt = t // tr

    def body(p_ref, ph_ref, pn_ref, dz_ref, dzn_ref, cw_ref, dp_ref, dcw_ref):
        i = pl.program_id(0)

        @pl.when(i == 0)
        def _():
            dcw_ref[...] = jnp.zeros_like(dcw_ref)

        pv = p_ref[...]
        gb, gcv, xt = pv[:, :d], pv[:, d:2 * d], pv[:, 2 * d:]
        y = gcv * xt
        ph = ph_ref[...]
        yh = jnp.where(i > 0, ph[:, d:2 * d] * ph[:, 2 * d:], 0.0)
        cat = jnp.concatenate([yh, y], axis=0)
        y2, y1 = _shift_down(cat, 2, tr), _shift_down(cat, 1, tr)
        w0, w1, w2 = cw_ref[0:1, :], cw_ref[1:2, :], cw_ref[2:3, :]
        yc = w0 * y2 + w1 * y1 + w2 * y
        dzv = dz_ref[...]
        dyc = dzv * gb
        dcw_ref[0] += _colsum8(dyc * y2)
        dcw_ref[1] += _colsum8(dyc * y1)
        dcw_ref[2] += _colsum8(dyc * y)
        dycn = jnp.where(i < nt - 1, dzn_ref[...] * pn_ref[...][:, :d], 0.0)
        catn = jnp.concatenate([dyc, dycn], axis=0)
        dy = w2 * dyc + w1 * _shift_up(catn, 1, tr) + w0 * _shift_up(catn, 2, tr)
        dp_ref[:, :d] = (dzv * yc).astype(BF16)
        dp_ref[:, d:2 * d] = (dy * xt).astype(BF16)
        dp_ref[:, 2 * d:] = (dy * gcv).astype(BF16)

    nxt = lambda i: (jnp.minimum((i + 1) * hb, t // 8 - 1), 0)
    return _call(
        body, name=name, out_shape=[_sds((t, d3), BF16), _sds((3, 8, d), F32)], grid=(nt,),
        in_specs=[pl.BlockSpec((tr, d3), lambda i: (i, 0)),
                  pl.BlockSpec((8, d3), lambda i: (jnp.maximum(i * hb - 1, 0), 0)),
                  pl.BlockSpec((8, d3), nxt),
                  pl.BlockSpec((tr, d), lambda i: (i, 0)),
                  pl.BlockSpec((8, d), nxt),
                  pl.BlockSpec((8, d), lambda i: (0, 0))],
        out_specs=[pl.BlockSpec((tr, d3), lambda i: (i, 0)), pl.BlockSpec((3, 8, d), lambda i: (0, 0, 0))],
        compiler_params=_params("arbitrary"),
    )(p, p, p, dz, dz, cw)


def ada_fwd(c_all, w, b, name):
    nl, d, n = w.shape

    def body(c_ref, w_ref, b_ref, o_ref):
        cv = c_ref[...]
        s = (cv * _sigmoid(cv)).astype(BF16)
        o_ref[...] = jnp.dot(s, w_ref[...].astype(BF16), preferred_element_type=F32) + b_ref[...]

    return _call(
        body, name=name, out_shape=_sds((nl, N_DEV, n), F32), grid=(nl,),
        in_specs=[pl.BlockSpec((N_DEV, d), lambda l: (0, 0)), pl.BlockSpec((None, d, n), lambda l: (l, 0, 0)),
                  pl.BlockSpec((None, 1, n), lambda l: (l, 0, 0))],
        out_specs=pl.BlockSpec((None, N_DEV, n), lambda l: (l, 0, 0)),
        compiler_params=_params("parallel"),
    )(c_all, w, b)


def ada_bwd(c_all, dmod, name):
    nl, _, n = dmod.shape
    d = c_all.shape[1]

    def body(c_ref, dm_ref, o_ref):
        cv = c_ref[...]
        s = (cv * _sigmoid(cv)).astype(BF16)
        o_ref[...] = lax.dot_general(s, dm_ref[...].astype(BF16), TN, preferred_element_type=F32)

    return _call(
        body, name=name, out_shape=_sds((nl, d, n), F32), grid=(nl,),
        in_specs=[pl.BlockSpec((N_DEV, d), lambda l: (0, 0)), pl.BlockSpec((None, N_DEV, n), lambda l: (l, 0, 0))],
        out_specs=pl.BlockSpec((None, d, n), lambda l: (l, 0, 0)),
        compiler_params=_params("parallel"),
    )(c_all, dmod)


def adamw(pieces, w, m, v, name):
    npc, r, c = pieces.shape
    tr = r
    for cand in (1024, 512, 256, 128, 64, 32, 16, 8):
        if r % cand == 0 and cand * c * 4 <= (1 << 20):
            tr = cand
            break

    def body(p_ref, w_ref, m_ref, v_ref, g_ref, d_ref, nm_ref, nv_ref):
        g = p_ref[0].astype(F32)
        for i in range(1, npc):
            g = g + p_ref[i].astype(F32)
        wv = w_ref[...]
        nm = ADAM_B1 * m_ref[...] + (1.0 - ADAM_B1) * g
        nv = ADAM_B2 * v_ref[...] + (1.0 - ADAM_B2) * (g * g)
        m_hat = nm / (1.0 - ADAM_B1 ** ADAM_STEP)
        v_hat = nv / (1.0 - ADAM_B2 ** ADAM_STEP)
        g_ref[...] = g
        d_ref[...] = -ADAM_LR * (m_hat / (jnp.sqrt(v_hat) + ADAM_EPS) + ADAM_WD * wv)
        nm_ref[...] = nm
        nv_ref[...] = nv

    row = pl.BlockSpec((tr, c), lambda i: (i, 0))
    return _call(
        body, name=name, out_shape=[_sds((r, c), F32)] * 4, grid=(r // tr,),
        in_specs=[pl.BlockSpec((npc, tr, c), lambda i: (0, i, 0)), row, row, row],
        out_specs=[row] * 4, compiler_params=_params("parallel"),
    )(pieces, w, m, v)


def sum_pieces(pieces, name):
    npc, r, c = pieces.shape

    def body(p_ref, o_ref):
        g = p_ref[0]
        for i in range(1, npc):
            g = g + p_ref[i]
        o_ref[...] = g

    return _call(body, name=name, out_shape=_sds((r, c), F32),
                 in_specs=[pl.BlockSpec(memory_space=pltpu.VMEM)],
                 out_specs=pl.BlockSpec(memory_space=pltpu.VMEM),
                 compiler_params=pltpu.CompilerParams(vmem_limit_bytes=VMEM_LIMIT))(pieces)


PACK_W = 1024


def _pack(arrs):
    flat = jnp.concatenate([a.reshape(-1).astype(F32) for a in arrs])
    rows = -(-flat.shape[0] // (8 * PACK_W)) * 8
    return jnp.pad(flat, (0, rows * PACK_W - flat.shape[0])).reshape(rows, PACK_W)


def _unpack(slab, shapes):
    flat = slab.reshape(-1)
    out, off = [], 0
    for s in shapes:
        n = 1
        for q in s:
            n *= q
        out.append(flat[off:off + n].reshape(s))
        off += n
    return out


def kernel(x, c, ada_w, ada_b, norm_g, ffn_w_in, ffn_w_out, sb_w_qkv, sb_w_o, sg_w_in, sg_ln_g, sg_w_s, sg_bias, sg_w_out, sc_w_in, sc_conv_w, sc_w_out, cb_w_qkv, cb_rel_bias, cb_w_o, loss_target, m_ada_w, m_ada_b, m_norm_g, m_ffn_w_in, m_ffn_w_out, m_sb_w_qkv, m_sb_w_o, m_sg_w_in, m_sg_ln_g, m_sg_w_s, m_sg_bias, m_sg_w_out, m_sc_w_in, m_sc_conv_w, m_sc_w_out, m_cb_w_qkv, m_cb_rel_bias, m_cb_w_o, v_ada_w, v_ada_b, v_norm_g, v_ffn_w_in, v_ffn_w_out, v_sb_w_qkv, v_sb_w_o, v_sg_w_in, v_sg_ln_g, v_sg_w_s, v_sg_bias, v_sg_w_out, v_sc_w_in, v_sc_conv_w, v_sc_w_out, v_cb_w_qkv, v_cb_rel_bias, v_cb_w_o):
    depth = ada_w.shape[0]
    d = D_MODEL
    xi, yi, ci = lax.axis_index("x"), lax.axis_index("y"), lax.axis_index("c")
    me = 4 * xi + 2 * yi + ci
    x0 = x[0]
    t = x0.shape[0]
    target = loss_target[0]

    c_all = _all_gather([jnp.pad(c, ((0, 7), (0, 0)))], "gather_c")[0][:, 0, :]
    na = ada_w.shape[2]
    b_cols = lax.dynamic_slice_in_dim(ada_b, me * na, na, axis=1)[:, None, :]
    mod_part = ada_fwd(c_all, ada_w, b_cols, "ada_fwd")
    mod_g = _all_gather([mod_part.reshape(depth * N_DEV, na)], "gather_mod")[0]
    mod_g = mod_g.reshape(N_DEV, depth, N_DEV, na)
    mod_me = lax.dynamic_index_in_dim(mod_g, me, axis=2, keepdims=False)
    mod = jnp.transpose(mod_me, (1, 0, 2)).reshape(depth, 6, 1, d)

    ng = _all_gather([norm_g.reshape(depth * 4, d // N_DEV)], "gather_norm_g")[0]
    norm_full = jnp.transpose(ng, (1, 0, 2)).reshape(depth, 4, 1, d)
    small = _all_gather([_pack([sg_ln_g, sc_conv_w])], "gather_small")[0].reshape(N_DEV, -1)
    nl_g = sg_ln_g.shape[1]
    ln_full = small[:, :nl_g].reshape(1, N_DEV * nl_g)
    cwn = sc_conv_w.shape[2]
    cw_sh = small[:, nl_g:nl_g + 3 * cwn].reshape(N_DEV, 3, cwn)
    cw_full = jnp.transpose(cw_sh, (1, 0, 2)).reshape(3, d)
    cw_pad = jnp.pad(cw_full, ((0, 5), (0, 0)))

    bf = lambda a: a.astype(BF16)
    mixers = [
        [bf(sb_w_qkv[0]), bf(sb_w_o[0])],
        [bf(sg_w_in[0]), bf(sg_w_out[0])],
        [bf(sc_w_in[0]), bf(sc_w_out[0])],
        [bf(cb_w_qkv[0]), bf(cb_w_o[0])],
    ]
    shards = [[bf(ffn_w_in[i]), bf(ffn_w_out[i])] + mixers[i % 4] for i in range(depth)]
    gathered = [None] * depth
    w_qkv0 = _all_gather([shards[0][2]], "gather_w_qkv0")[0]
    riding_shards = [shards[0][0], shards[0][1], shards[0][3]] + [s for i in range(1, depth) for s in shards[i]]

    bias_win = band_bias_window(cb_rel_bias[0])
    ws = sg_w_s[0]
    bias_t = jnp.pad(sg_bias[0].T, ((0, 0), (0, BLK - SG_GROUPS)))

    saved = []
    xcur = x0
    h = pre_fwd(x0, norm_full[0, 0], mod[0, 0], mod[0, 1], "L0_pre_m")
    for i in range(depth):
        mi = i % 4
        sh_m, sc_m, gt_m, sh_f, sc_f, gt_f = [mod[i, j] for j in range(6)]
        g0, g1, g2, g3 = [norm_full[i, j] for j in range(4)]
        tag = "L%d_" % i
        sv = {"x_in": xcur, "h_m": h}
        if mi == 0:
            qkv = mm_cs(h, w_qkv0, tag + "qkv", out_dtype=BF16)
            qkv_t = jnp.transpose(qkv.reshape(t // SB_GW, SB_GW, 3 * d), (0, 2, 1))
            o, cmass, riding = sb_fwd(qkv, qkv_t, riding_shards, tag + "sb_fwd")
            gathered[0] = [riding[0], riding[1], w_qkv0, riding[2]]
            for j in range(1, depth):
                gathered[j] = riding[4 * j - 1:4 * j + 3]
            sv.update(qkv=qkv, qkv_t=qkv_t, o=o, cmass=cmass)
            mixed = o
        elif mi == 1:
            a = mm_cs(h, gathered[i][2], tag + "sg_in")
            mixed = sg_fwd(a, ln_full, ws, bias_t, tag + "sg_fwd")
            sv.update(a=a, yy=mixed)
        elif mi == 2:
            p = mm_cs(h, gathered[i][2], tag + "sc_in")
            mixed = conv_fwd(p, cw_pad, tag + "conv_fwd")
            sv.update(p=p, gz=mixed)
        else:
            qkv = mm_cs(h, gathered[i][2], tag + "qkv", out_dtype=BF16)
            mixed = band_fwd(qkv, bias_win, tag + "band_fwd")
            sv.update(qkv=qkv, o=mixed)
        wfi, wfo, _, wmo = gathered[i]
        wfo4 = wfo.reshape(4, -1, d)
        y = mm(mixed, wmo.reshape(-1, d), tag + "mix_out")
        sv["y_m"] = y
        xmid, h2 = post_pre_fwd(xcur, y, g1, gt_m, g2, sh_f, sc_f, tag + "post_m_pre_f")
        sv["x_mid"] = xmid
        ag, au, s3 = ffn_in_swiglu(h2, wfi, tag + "ffn_in")
        y2 = mm_rs(s3, wfo4, tag + "ffn_out")
        sv.update(h_f=h2, ag=ag, au=au, s3=s3, y_f=y2)
        saved.append(sv)
        if i + 1 < depth:
            xcur, h = post_pre_fwd(xmid, y2, g3, gt_f, norm_full[i + 1, 0], mod[i + 1, 0], mod[i + 1, 1],
                                   tag + "post_f_pre_m")

    last = depth - 1
    dx, dy2, lpart, dgt_f, dg3 = post_loss_bwd(saved[last]["x_mid"], saved[last]["y_f"], norm_full[last, 3],
                                               mod[last, 5], target, "loss")
    loss = lax.psum(0.5 * jnp.sum(lpart) / d, ("x", "y", "c"))

    dmod_rows = [None] * depth
    dnorm_rows = [None] * depth
    big_pieces = [None] * depth
    small_grads = {}
    for i in reversed(range(depth)):
        wfi, wfo, wmi, wmo = gathered[i]
        wfo4 = wfo.reshape(4, -1, d)
        wmo2 = wmo.reshape(-1, d)
        mi = i % 4
        sh_m, sc_m, gt_m, sh_f, sc_f, gt_f = [mod[i, j] for j in range(6)]
        g0, g1, g2, g3 = [norm_full[i, j] for j in range(4)]
        tag = "L%d_b_" % i
        sv = saved[i]
        da3 = ffn_out_dx_swiglu(dy2, wfo4, sv["ag"], sv["au"], tag + "ffn_out_dx")
        dwfo = mm_rs_dw(sv["s3"], dy2, tag + "ffn_out_dw", out_dtype=BF16)
        dh2 = mm_cs_dx(da3, wfi, tag + "ffn_in_dx", act_major=True)
        dwfi = mm_cs_dw(sv["h_f"], da3, tag + "ffn_in_dw", act_major=True, out_dtype=BF16)
        dx, dy, dsh_f, dsc_f, dg2, dgt_m, dg1 = pre_post_bwd(
            dh2, sv["x_mid"], g2, sc_f, dx, sv["y_m"], g1, gt_m, tag + "pre_f_post_m")
        if mi == 0:
            do = mm_nt(dy, wmo2, tag + "wo_dx", out_dtype=BF16)
            dwmo = mm_tn(sv["o"], dy, tag + "wo_dw", out_dtype=BF16)
            riders = [dwfi, dwfo.reshape(N_DEV, -1, d), dwmo.reshape(N_DEV, -1, d)]
            riders += [piece for j in range(depth - 1, 0, -1) for piece in big_pieces[j]]
            dq, dk, dv, exchanged = sb_bwd(sv["qkv"], sv["qkv_t"], do, sv["cmass"], riders, tag + "sb_bwd")
            for n, j in enumerate(range(depth - 1, 0, -1)):
                big_pieces[j] = exchanged[3 + 4 * n:3 + 4 * (n + 1)]
            dmid = jnp.concatenate([dq, dk, dv], axis=1)
        elif mi == 1:
            dyy = mm_nt(dy, wmo2, tag + "sg_out_dx")
            dwmo = mm_tn(sv["yy"], dy, tag + "sg_out_dw", out_dtype=BF16)
            dmid, dlng, dws, dbt = sg_bwd(sv["a"], dyy, ln_full, ws, bias_t, tag + "sg_bwd")
            small_grads.update(ln_g=jnp.sum(dlng, axis=0), w_s=dws, bias=dbt[:, :SG_GROUPS].T)
        elif mi == 2:
            dgz = mm_nt(dy, wmo2, tag + "sc_out_dx")
            dwmo = mm_tn(sv["gz"], dy, tag + "sc_out_dw", out_dtype=BF16)
            dmid, dcw = conv_bwd(sv["p"], dgz, cw_pad, tag + "conv_bwd")
            small_grads.update(conv_w=jnp.sum(dcw, axis=1))
        else:
            do = mm_nt(dy, wmo2, tag + "wo_dx", out_dtype=BF16)
            dwmo = mm_tn(sv["o"], dy, tag + "wo_dw", out_dtype=BF16)
            dq, dk, dv, dwin = band_bwd(sv["qkv"], do, bias_win, tag + "band_bwd")
            dmid = jnp.concatenate([dq, dk, dv], axis=1)
            small_grads.update(rel_bias=band_bias_window_grad(dwin))
        dh = mm_cs_dx(dmid, wmi, tag + "mix_in_dx")
        dwmi = mm_cs_dw(sv["h_m"], dmid, tag + "mix_in_dw", out_dtype=BF16)
        if i > 0:
            dx, dy2_prev, dsh_m, dsc_m, dg0, dgt_f_prev, dg3_prev = pre_post_bwd(
                dh, sv["x_in"], g0, sc_m, dx, saved[i - 1]["y_f"], norm_full[i - 1, 3], mod[i - 1, 5],
                tag + "pre_m_post_f")
        else:
            dx, dsh_m, dsc_m, dg0 = pre_bwd(dh, sv["x_in"], g0, sc_m, dx, tag + "pre_m")
        dmod_rows[i] = jnp.stack([jnp.sum(q, axis=0) for q in (dsh_m, dsc_m, dgt_m, dsh_f, dsc_f, dgt_f)])
        dnorm_rows[i] = jnp.stack([jnp.sum(q, axis=0) for q in (dg0, dg1, dg2, dg3)])
        if i > 0:
            dy2, dgt_f, dg3 = dy2_prev, dgt_f_prev, dg3_prev
        big_pieces[i] = [dwfi, dwfo.reshape(N_DEV, -1, d), dwmi, dwmo.reshape(N_DEV, -1, d)]
    big_pieces[0] = [exchanged[0], exchanged[1], _all_to_all([big_pieces[0][2]], "exchange_w_qkv0")[0], exchanged[2]]

    grad_x = dx[None]

    dmod_mine = jnp.stack(dmod_rows).reshape(depth, 6 * d)
    dnorm_mine = jnp.stack(dnorm_rows)
    small_list = [dnorm_mine, small_grads["ln_g"], small_grads["w_s"], small_grads["bias"],
                  small_grads["conv_w"], small_grads["rel_bias"]]
    small_shapes = [dmod_mine.shape] + [a.shape for a in small_list]
    slab = _pack([dmod_mine] + small_list)
    slab_g = _all_gather([slab], "gather_small_grads")[0]
    tot = sum_pieces(slab_g, "sum_small_grads")
    g_ada_b_full, g_norm, g_ln, g_ws, g_sbias, g_cw, g_rb = _unpack(tot, small_shapes)
    dmod_all = slab_g.reshape(N_DEV, -1)[:, :depth * 6 * d].reshape(N_DEV, depth, 6 * d)
    dmod_cols = lax.dynamic_slice_in_dim(dmod_all, me * na, na, axis=2)
    g_ada_w = ada_bwd(c_all, jnp.transpose(dmod_cols, (1, 0, 2)), "ada_bwd")

    nsh = d // N_DEV
    g_norm_sh = lax.dynamic_slice_in_dim(g_norm, me * nsh, nsh, axis=2)
    g_ln_sh = lax.dynamic_slice_in_dim(g_ln.reshape(1, -1), me * nl_g, nl_g, axis=1)
    g_cw_sh = lax.dynamic_slice_in_dim(g_cw, me * cwn, cwn, axis=1)[None]

    out_g, out_d, out_m, out_v = {}, {}, {}, {}

    def upd(name, pieces, w, m, v):
        r_c = pieces.shape[1:]
        g, dl, nm, nv = adamw(pieces, w.reshape(r_c), m.reshape(r_c), v.reshape(r_c), "adamw_" + name)
        out_g[name], out_d[name] = g.reshape(w.shape), dl.reshape(w.shape)
        out_m[name], out_v[name] = nm.reshape(w.shape), nv.reshape(w.shape)

    upd("ada_w", g_ada_w.reshape(1, depth * d, na), ada_w, m_ada_w, v_ada_w)
    nfi = ffn_w_in.shape[2]
    nfo = ffn_w_out.shape[1]
    pfi = jnp.concatenate([big_pieces[i][0] for i in range(depth)], axis=1)
    pfo = jnp.concatenate([big_pieces[i][1] for i in range(depth)], axis=1)
    upd("ffn_w_in", pfi, ffn_w_in, m_ffn_w_in, v_ffn_w_in)
    upd("ffn_w_out", pfo, ffn_w_out, m_ffn_w_out, v_ffn_w_out)
    upd("sb_w_qkv", big_pieces[0][2], sb_w_qkv, m_sb_w_qkv, v_sb_w_qkv)
    upd("sb_w_o", big_pieces[0][3], sb_w_o, m_sb_w_o, v_sb_w_o)
    upd("sg_w_in", big_pieces[1][2], sg_w_in, m_sg_w_in, v_sg_w_in)
    upd("sg_w_out", big_pieces[1][3], sg_w_out, m_sg_w_out, v_sg_w_out)
    upd("sc_w_in", big_pieces[2][2], sc_w_in, m_sc_w_in, v_sc_w_in)
    upd("sc_w_out", big_pieces[2][3], sc_w_out, m_sc_w_out, v_sc_w_out)
    upd("cb_w_qkv", big_pieces[3][2], cb_w_qkv, m_cb_w_qkv, v_cb_w_qkv)
    upd("cb_w_o", big_pieces[3][3], cb_w_o, m_cb_w_o, v_cb_w_o)

    small_names = ["ada_b", "norm_g", "sg_ln_g", "sg_w_s", "sg_bias", "sc_conv_w", "cb_rel_bias"]
    small_g = [g_ada_b_full, g_norm_sh, g_ln_sh, g_ws[None], g_sbias[None], g_cw_sh, g_rb[None]]
    small_w = [ada_b, norm_g, sg_ln_g, sg_w_s, sg_bias, sc_conv_w, cb_rel_bias]
    small_m = [m_ada_b, m_norm_g, m_sg_ln_g, m_sg_w_s, m_sg_bias, m_sc_conv_w, m_cb_rel_bias]
    small_v = [v_ada_b, v_norm_g, v_sg_ln_g, v_sg_w_s, v_sg_bias, v_sc_conv_w, v_cb_rel_bias]
    shapes = [w.shape for w in small_w]
    res = adamw(_pack(small_g)[None], _pack(small_w), _pack(small_m), _pack(small_v), "adamw_small")
    for nm_, gs, ds_, ms, vs in zip(small_names, *[_unpack(r, shapes) for r in res]):
        out_g[nm_], out_d[nm_], out_m[nm_], out_v[nm_] = gs, ds_, ms, vs

    order = ["ada_w", "ada_b", "norm_g", "ffn_w_in", "ffn_w_out", "sb_w_qkv", "sb_w_o", "sg_w_in", "sg_ln_g",
             "sg_w_s", "sg_bias", "sg_w_out", "sc_w_in", "sc_conv_w", "sc_w_out", "cb_w_qkv", "cb_rel_bias", "cb_w_o"]
    return (loss, grad_x, *[out_g[n] for n in order], *[out_d[n] for n in order],
            *[out_m[n] for n in order], *[out_v[n] for n in order])
```

```python
import jax
import jax.numpy as jnp
from jax import lax
from jax.experimental import pallas as pl
from jax.experimental.pallas import tpu as pltpu

F32 = jnp.float32
BF16 = jnp.bfloat16
MESH = pl.DeviceIdType.MESH

N_DEV = 8
D_MODEL = 1024
N_HEADS = 16
HEAD_DIM = 64
QK_SCALE = HEAD_DIM ** -0.5
BLK = 128
BAND_BLOCKS = 5
BAND_W = BAND_BLOCKS * BLK
REL_CLIP = 128
EPS = 1e-6
NEG = -1e30
GELU_C0 = 0.7978845608028654
GELU_C1 = 0.044715
ADAM_LR = 0.001
ADAM_B1 = 0.9
ADAM_B2 = 0.999
ADAM_EPS = 1e-08
ADAM_WD = 0.01
ADAM_STEP = 10
VMEM_LIMIT = 56 * 1024 * 1024


def _call(body, **kw):
    return pl.pallas_call(body, **kw)


def _params(*sem):
    return pltpu.CompilerParams(dimension_semantics=sem, vmem_limit_bytes=VMEM_LIMIT)


def _sds(shape, dtype):
    return jax.ShapeDtypeStruct(tuple(shape), dtype)


def _row_tile(t):
    return min(512, t)


def _me():
    x, y, c = lax.axis_index("x"), lax.axis_index("y"), lax.axis_index("c")
    return x, y, c


def _all_gather(arrs, name):
    n = len(arrs)

    def body(*refs):
        gather = _Gather(refs[:n], refs[n:2 * n], *refs[2 * n:])
        gather.start()
        gather.forward()
        gather.finish()

    any_spec = pl.BlockSpec(memory_space=pl.ANY)
    outs = _call(
        body,
        name=name,
        out_shape=_Gather.out_shapes(arrs),
        in_specs=[any_spec] * n,
        out_specs=[any_spec] * n,
        scratch_shapes=_comm_sems(n),
    )(*arrs)
    return list(outs)


def _comm_sems(n):
    if n == 0:
        return []
    return [pltpu.SemaphoreType.DMA((n, 7)), pltpu.SemaphoreType.DMA((n, 7)), pltpu.SemaphoreType.DMA((n,))]


class _Gather:
    def __init__(self, x_refs, o_refs, send_sems, recv_sems, local_sems):
        self.x_refs, self.o_refs = x_refs, o_refs
        self.send_sems, self.recv_sems, self.local_sems = send_sems, recv_sems, local_sems
        x, y, c = _me()
        self.c = c
        self.me, self.sibling = (x, y, c), (x, y, 1 - c)
        self.chips = [(1 - x, y), (x, 1 - y), (1 - x, 1 - y)]

    @staticmethod
    def out_shapes(arrs):
        return [_sds((N_DEV,) + a.shape, a.dtype) for a in arrs]

    def rows(self, a, block):
        px, py, pc = block
        return self.o_refs[a].at[4 * px + 2 * py + pc]

    def copy(self, a, k, block, to, own=False):
        return pltpu.make_async_remote_copy(
            src_ref=self.x_refs[a] if own else self.rows(a, block),
            dst_ref=self.rows(a, block),
            send_sem=self.send_sems.at[a, k],
            recv_sem=self.recv_sems.at[a, k],
            device_id=to,
            device_id_type=MESH,
        )

    def local(self, a):
        return pltpu.make_async_copy(self.x_refs[a], self.rows(a, self.me), self.local_sems.at[a])

    def first(self, a):
        cps = [self.copy(a, 0, self.me, self.sibling, own=True)]
        return cps + [self.copy(a, 1 + j, self.me, (*chip, self.c), own=True) for j, chip in enumerate(self.chips)]

    def passed(self, a):
        return [self.copy(a, 4 + j, (*chip, self.c), self.sibling) for j, chip in enumerate(self.chips)]

    def start(self):
        for a in range(len(self.x_refs)):
            self.local(a).start()
            for cp in self.first(a):
                cp.start()

    def forward(self):
        for a in range(len(self.x_refs)):
            passed = self.passed(a)
            for j, chip in enumerate(self.chips):
                self.copy(a, 1 + j, (*chip, self.c), self.me).wait_recv()
                passed[j].start()

    def finish(self):
        for a in range(len(self.x_refs)):
            self.copy(a, 0, self.sibling, self.me).wait_recv()
            for j, chip in enumerate(self.chips):
                self.copy(a, 4 + j, (*chip, 1 - self.c), self.me).wait_recv()
            for cp in self.first(a) + self.passed(a):
                cp.wait_send()
            self.local(a).wait()


class _Exchange:
    def __init__(self, x_refs, o_refs, send_sems, recv_sems, local_sems):
        self.x_refs, self.o_refs = x_refs, o_refs
        self.send_sems, self.recv_sems, self.local_sems = send_sems, recv_sems, local_sems
        x, y, c = _me()
        self.me = 4 * x + 2 * y + c
        self.peers = []
        for k in range(1, N_DEV):
            px = 1 - x if k & 4 else x
            py = 1 - y if k & 2 else y
            pc = 1 - c if k & 1 else c
            self.peers.append((px, py, pc))

    @staticmethod
    def out_shapes(arrs):
        return [_sds(a.shape, a.dtype) for a in arrs]

    def local(self, a):
        return pltpu.make_async_copy(self.x_refs[a].at[self.me], self.o_refs[a].at[self.me], self.local_sems.at[a])

    def copy(self, a, k, send):
        px, py, pc = self.peers[k]
        peer = 4 * px + 2 * py + pc
        return pltpu.make_async_remote_copy(
            src_ref=self.x_refs[a].at[peer],
            dst_ref=self.o_refs[a].at[self.me if send else peer],
            send_sem=self.send_sems.at[a, k], recv_sem=self.recv_sems.at[a, k],
            device_id=(px, py, pc), device_id_type=MESH)

    def start(self):
        for a in range(len(self.x_refs)):
            self.local(a).start()
            for k in range(N_DEV - 1):
                self.copy(a, k, True).start()

    def finish(self):
        for a in range(len(self.x_refs)):
            for k in range(N_DEV - 1):
                self.copy(a, k, False).wait_recv()
            for k in range(N_DEV - 1):
                self.copy(a, k, True).wait_send()
            self.local(a).wait()


def _all_to_all(arrs, name):
    n = len(arrs)

    def body(*refs):
        exchange = _Exchange(refs[:n], refs[n:2 * n], *refs[2 * n:])
        exchange.start()
        exchange.finish()

    any_spec = pl.BlockSpec(memory_space=pl.ANY)
    outs = _call(
        body,
        name=name,
        out_shape=[_sds(a.shape, a.dtype) for a in arrs],
        in_specs=[any_spec] * n,
        out_specs=[any_spec] * n,
        scratch_shapes=_comm_sems(n),
    )(*arrs)
    return list(outs)


NN = (((1,), (0,)), ((), ()))
NT = (((1,), (1,)), ((), ()))
TN = (((0,), (0,)), ((), ()))


def _all_of(conds):
    out = conds[0]
    for cond in conds[1:]:
        out = out & cond
    return out


def _rider_call(body, operands, riders, *, name, out_shape, grid, in_specs, out_specs, scratch_shapes, sem):
    if not riders or not riders[1]:
        res = _call(body, name=name, out_shape=out_shape, grid=grid, in_specs=in_specs, out_specs=out_specs,
                    scratch_shapes=list(scratch_shapes), compiler_params=_params(*sem))(*operands)
        return list(res), []
    cls, arrs = riders
    nr, ni, no, ns = len(arrs), len(in_specs), len(out_specs), len(scratch_shapes)
    forward_at = (3 * grid[0]) // 4 if grid[0] >= 4 else None

    def wrapped(*refs):
        ids = [pl.program_id(ax) for ax in range(len(grid))]
        comm = cls(refs[ni:ni + nr], refs[ni + nr + no:ni + 2 * nr + no], *refs[ni + 2 * nr + no + ns:])
        pl.when(_all_of([i == 0 for i in ids]))(comm.start)
        if cls is _Gather and forward_at is not None:
            pl.when(_all_of([ids[0] == forward_at] + [i == 0 for i in ids[1:]]))(comm.forward)
        body(*refs[:ni], *refs[ni + nr:ni + nr + no], *refs[ni + 2 * nr + no:ni + 2 * nr + no + ns])

        def end():
            if cls is _Gather and forward_at is None:
                comm.forward()
            comm.finish()

        pl.when(_all_of([i == g - 1 for i, g in zip(ids, grid)]))(end)

    any_spec = pl.BlockSpec(memory_space=pl.ANY)
    res = _call(wrapped, name=name, out_shape=list(out_shape) + cls.out_shapes(arrs), grid=grid,
                in_specs=list(in_specs) + [any_spec] * nr, out_specs=list(out_specs) + [any_spec] * nr,
                scratch_shapes=list(scratch_shapes) + _comm_sems(nr),
                compiler_params=_params(*(["arbitrary"] * len(grid))))(*operands, *arrs)
    return list(res[:no]), list(res[no:])


def _gemm(a, b, out_shape, out_dtype, grid, a_spec, b_spec, o_spec, acc_shape, dims, name, riders=None):
    nk = grid[2]

    if nk == 1:
        def body(a_ref, b_ref, o_ref):
            r = lax.dot_general(a_ref[...].astype(BF16), b_ref[...].astype(BF16), dims,
                                preferred_element_type=F32)
            o_ref[...] = r.astype(o_ref.dtype)
        scratch = []
    else:
        def body(a_ref, b_ref, o_ref, acc_ref):
            k = pl.program_id(2)

            @pl.when(k == 0)
            def _():
                acc_ref[...] = jnp.zeros_like(acc_ref)

            acc_ref[...] += lax.dot_general(a_ref[...].astype(BF16), b_ref[...].astype(BF16), dims,
                                            preferred_element_type=F32)

            @pl.when(k == nk - 1)
            def _():
                o_ref[...] = acc_ref[...].astype(o_ref.dtype)
        scratch = [pltpu.VMEM(acc_shape, F32)]

    res, ridden = _rider_call(
        body, (a, b), riders, name=name, out_shape=[_sds(out_shape, out_dtype)], grid=grid,
        in_specs=[a_spec, b_spec], out_specs=[o_spec], scratch_shapes=scratch,
        sem=("parallel", "parallel", "arbitrary"))
    return (res[0], ridden) if riders else res[0]


def _div_tile(n, want):
    if n <= want:
        return n
    t = want - want % 128
    while n % t:
        t -= 128
    return t


def mm(a, b, name, out_dtype=F32, tm=512, tn=1024, tk=1024):
    m, k = a.shape
    n = b.shape[1]
    tm, tn, tk = _div_tile(m, tm), _div_tile(n, tn), _div_tile(k, tk)
    return _gemm(a, b, (m, n), out_dtype, (m // tm, n // tn, k // tk),
                 pl.BlockSpec((tm, tk), lambda i, j, kk: (i, kk)),
                 pl.BlockSpec((tk, tn), lambda i, j, kk: (kk, j)),
                 pl.BlockSpec((tm, tn), lambda i, j, kk: (i, j)),
                 (tm, tn), NN, name)


def mm_nt(a, b, name, out_dtype=F32, tm=512, tn=1024, tk=1024):
    m, n = a.shape
    k = b.shape[0]
    tm, tk_out, tred = _div_tile(m, tm), _div_tile(k, tn), _div_tile(n, tk)
    return _gemm(a, b, (m, k), out_dtype, (m // tm, k // tk_out, n // tred),
                 pl.BlockSpec((tm, tred), lambda i, j, kk: (i, kk)),
                 pl.BlockSpec((tk_out, tred), lambda i, j, kk: (j, kk)),
                 pl.BlockSpec((tm, tk_out), lambda i, j, kk: (i, j)),
                 (tm, tk_out), NT, name)


def mm_tn(a, b, name, out_dtype=F32, tm=512, tn=1024, tk=1024):
    m, k = a.shape
    n = b.shape[1]
    tk_out, tn, tred = _div_tile(k, tk), _div_tile(n, tn), _div_tile(m, tm)
    return _gemm(a, b, (k, n), out_dtype, (k // tk_out, n // tn, m // tred),
                 pl.BlockSpec((tred, tk_out), lambda i, j, kk: (kk, i)),
                 pl.BlockSpec((tred, tn), lambda i, j, kk: (kk, j)),
                 pl.BlockSpec((tk_out, tn), lambda i, j, kk: (i, j)),
                 (tk_out, tn), TN, name)


def mm_cs(a, wg, name, act_major=False, out_dtype=F32, tm=1024, riders=None):
    m, k = a.shape
    s, _, n = wg.shape
    tm = _div_tile(m, tm)
    if act_major:
        out_shape, o_spec = (s, m, n), pl.BlockSpec((None, tm, n), lambda i, j, kk: (j, i, 0))
    else:
        out_shape, o_spec = (m, s * n), pl.BlockSpec((tm, n), lambda i, j, kk: (i, j))
    return _gemm(a, wg, out_shape, out_dtype, (m // tm, s, 1),
                 pl.BlockSpec((tm, k), lambda i, j, kk: (i, 0)),
                 pl.BlockSpec((None, k, n), lambda i, j, kk: (j, 0, 0)),
                 o_spec, (tm, n), NN, name, riders)


def mm_cs_dx(da, wg, name, act_major=False, out_dtype=F32, tm=1024, riders=None):
    s, k, n = wg.shape
    m = da.shape[1] if act_major else da.shape[0]
    tm = _div_tile(m, tm)
    if act_major:
        a_spec = pl.BlockSpec((None, tm, n), lambda i, j, kk: (kk, i, 0))
    else:
        a_spec = pl.BlockSpec((tm, n), lambda i, j, kk: (i, kk))
    return _gemm(da, wg, (m, k), out_dtype, (m // tm, 1, s), a_spec,
                 pl.BlockSpec((None, k, n), lambda i, j, kk: (kk, 0, 0)),
                 pl.BlockSpec((tm, k), lambda i, j, kk: (i, 0)),
                 (tm, k), NT, name, riders)


def mm_cs_dw(a, da, name, act_major=False, out_dtype=F32, tm=1024, riders=None):
    m, k = a.shape
    if act_major:
        s, _, n = da.shape
    else:
        s, n = N_DEV, da.shape[1] // N_DEV
    tm = _div_tile(m, tm)
    if act_major:
        b_spec = pl.BlockSpec((None, tm, n), lambda i, j, kk: (i, kk, 0))
    else:
        b_spec = pl.BlockSpec((tm, n), lambda i, j, kk: (kk, i))
    return _gemm(a, da, (s, k, n), out_dtype, (s, 1, m // tm),
                 pl.BlockSpec((tm, k), lambda i, j, kk: (kk, 0)), b_spec,
                 pl.BlockSpec((None, k, n), lambda i, j, kk: (i, 0, 0)),
                 (k, n), TN, name, riders)


def mm_rs(s3, w3, name, out_dtype=F32, tm=1024):
    s, m, n = s3.shape
    nn = w3.shape[2]
    tm = _div_tile(m, tm)
    return _gemm(s3, w3, (m, nn), out_dtype, (m // tm, 1, s),
                 pl.BlockSpec((None, tm, n), lambda i, j, kk: (kk, i, 0)),
                 pl.BlockSpec((None, n, nn), lambda i, j, kk: (kk, 0, 0)),
                 pl.BlockSpec((tm, nn), lambda i, j, kk: (i, 0)),
                 (tm, nn), NN, name)


def mm_rs_dx(dy, w3, name, out_dtype=F32, tm=1024):
    m, nn = dy.shape
    s, n, _ = w3.shape
    tm = _div_tile(m, tm)
    return _gemm(dy, w3, (s, m, n), out_dtype, (m // tm, s, 1),
                 pl.BlockSpec((tm, nn), lambda i, j, kk: (i, 0)),
                 pl.BlockSpec((None, n, nn), lambda i, j, kk: (j, 0, 0)),
                 pl.BlockSpec((None, tm, n), lambda i, j, kk: (j, i, 0)),
                 (tm, n), NT, name)


def mm_rs_dw(s3, dy, name, out_dtype=F32, tm=1024):
    s, m, n = s3.shape
    nn = dy.shape[1]
    tm = _div_tile(m, tm)
    return _gemm(s3, dy, (s, n, nn), out_dtype, (s, 1, m // tm),
                 pl.BlockSpec((None, tm, n), lambda i, j, kk: (i, kk, 0)),
                 pl.BlockSpec((tm, nn), lambda i, j, kk: (kk, 0)),
                 pl.BlockSpec((None, n, nn), lambda i, j, kk: (i, 0, 0)),
                 (n, nn), TN, name)


def _colsum8(v):
    tr, d = v.shape
    return v.reshape(tr // 8, 8, d).sum(axis=0)


def _rstd(v):
    return lax.rsqrt(jnp.mean(v * v, axis=-1, keepdims=True) + EPS)


def _vec_spec(d):
    return pl.BlockSpec((1, d), lambda i: (0, 0))


def _acc_spec(d):
    return pl.BlockSpec((8, d), lambda i: (0, 0))


def _pre_rows(xv, g, shift, scale):
    return ((xv * _rstd(xv)) * g) * (1 + scale) + shift


def _post_rows(xv, yv, g, gate):
    return xv + gate * ((yv * _rstd(yv)) * g)


def _post_bwd_rows(dxv, yv, g, gate):
    r = _rstd(yv)
    yhat = yv * r
    dgate = _colsum8(dxv * (yhat * g))
    dyn = gate * dxv
    dg = _colsum8(dyn * yhat)
    dyhat = dyn * g
    dy = r * (dyhat - yhat * jnp.mean(dyhat * yhat, axis=-1, keepdims=True))
    return dy, dgate, dg


def _pre_bwd_rows(dhv, xv, g, scale, dxn):
    r = _rstd(xv)
    xhat = xv * r
    dshift = _colsum8(dhv)
    dscale = _colsum8(dhv * (xhat * g))
    dmod = dhv * (1 + scale)
    dg = _colsum8(dmod * xhat)
    dxhat = dmod * g
    dx = r * (dxhat - xhat * jnp.mean(dxhat * xhat, axis=-1, keepdims=True)) + dxn
    return dx, dshift, dscale, dg


def _row_call(body, name, t, d, rows_in, vecs_in, rows_out, n_acc):
    tr = _row_tile(t)
    nri, nvi, nro = len(rows_in), len(vecs_in), len(rows_out)

    def wrapped(*refs):
        accs = refs[nri + nvi + nro:]
        if n_acc:
            @pl.when(pl.program_id(0) == 0)
            def _():
                for acc in accs:
                    acc[...] = jnp.zeros_like(acc)
        body(*refs)

    row = pl.BlockSpec((tr, d), lambda i: (i, 0))
    return _call(wrapped, name=name,
                 out_shape=[_sds((t, d), dt) for dt in rows_out] + [_sds((8, d), F32)] * n_acc,
                 grid=(t // tr,), in_specs=[row] * nri + [_vec_spec(d)] * nvi,
                 out_specs=[row] * nro + [_acc_spec(d)] * n_acc,
                 compiler_params=_params("arbitrary" if n_acc else "parallel"))(*rows_in, *vecs_in)


def pre_fwd(x, g, shift, scale, name):
    def body(x_ref, g_ref, sh_ref, sc_ref, h_ref):
        h_ref[...] = _pre_rows(x_ref[...], g_ref[...], sh_ref[...], sc_ref[...]).astype(BF16)

    return _row_call(body, name, *x.shape, [x], [g, shift, scale], [BF16], 0)[0]


def post_pre_fwd(x, y, g_post, gate, g_pre, shift, scale, name):
    def body(x_ref, y_ref, gp_ref, gt_ref, g_ref, sh_ref, sc_ref, xn_ref, h_ref):
        xn = _post_rows(x_ref[...], y_ref[...], gp_ref[...], gt_ref[...])
        xn_ref[...] = xn
        h_ref[...] = _pre_rows(xn, g_ref[...], sh_ref[...], sc_ref[...]).astype(BF16)

    return _row_call(body, name, *x.shape, [x, y], [g_post, gate, g_pre, shift, scale], [F32, BF16], 0)


def post_loss_bwd(x, y, g, gate, target, name):
    d = x.shape[1]

    def body(x_ref, y_ref, t_ref, g_ref, gt_ref, dx_ref, dy_ref, l_ref, dgate_ref, dg_ref):
        yv, gv, gate_v = y_ref[...], g_ref[...], gt_ref[...]
        err = _post_rows(x_ref[...], yv, gv, gate_v) - t_ref[...]
        l_ref[...] += _colsum8(err * err)
        dxv = err * (1.0 / d)
        dx_ref[...] = dxv
        dy, dgate, dg = _post_bwd_rows(dxv, yv, gv, gate_v)
        dy_ref[...] = dy.astype(BF16)
        dgate_ref[...] += dgate
        dg_ref[...] += dg

    return _row_call(body, name, *x.shape, [x, y, target], [g, gate], [F32, BF16], 3)


def pre_post_bwd(dh, x, g_pre, scale, dxn, y, g_post, gate, name):
    def body(dh_ref, x_ref, dxn_ref, y_ref, g_ref, sc_ref, gp_ref, gt_ref,
             dx_ref, dy_ref, dsh_ref, dsc_ref, dg_ref, dgate_ref, dgp_ref):
        dx, dsh, dsc, dg = _pre_bwd_rows(dh_ref[...].astype(F32), x_ref[...], g_ref[...], sc_ref[...], dxn_ref[...])
        dx_ref[...] = dx
        dsh_ref[...] += dsh
        dsc_ref[...] += dsc
        dg_ref[...] += dg
        dy, dgate, dgp = _post_bwd_rows(dx, y_ref[...], gp_ref[...], gt_ref[...])
        dy_ref[...] = dy.astype(BF16)
        dgate_ref[...] += dgate
        dgp_ref[...] += dgp

    return _row_call(body, name, *x.shape, [dh, x, dxn, y], [g_pre, scale, g_post, gate], [F32, BF16], 5)


def pre_bwd(dh, x, g, scale, dxn, name):
    def body(dh_ref, x_ref, dxn_ref, g_ref, sc_ref, dx_ref, dsh_ref, dsc_ref, dg_ref):
        dx, dsh, dsc, dg = _pre_bwd_rows(dh_ref[...].astype(F32), x_ref[...], g_ref[...], sc_ref[...], dxn_ref[...])
        dx_ref[...] = dx
        dsh_ref[...] += dsh
        dsc_ref[...] += dsc
        dg_ref[...] += dg

    return _row_call(body, name, *x.shape, [dh, x, dxn], [g, scale], [F32], 3)


def _sigmoid(x):
    return 1.0 / (1.0 + jnp.exp(-x))


def ffn_in_swiglu(h, wg, name, tm=1024, riders=None):
    m, k = h.shape
    s, _, n = wg.shape
    half = s // 2
    tm = _div_tile(m, tm)

    def body(h_ref, wg_ref, wu_ref, g_ref, u_ref, s_ref):
        hv = h_ref[...]
        g = jnp.dot(hv, wg_ref[...], preferred_element_type=F32)
        u = jnp.dot(hv, wu_ref[...], preferred_element_type=F32)
        g_ref[...] = g
        u_ref[...] = u
        s_ref[...] = ((g * _sigmoid(g)) * u).astype(BF16)

    act = pl.BlockSpec((None, tm, n), lambda i, j: (j, i, 0))
    res, ridden = _rider_call(
        body, (h, wg, wg), riders, name=name,
        out_shape=[_sds((half, m, n), F32), _sds((half, m, n), F32), _sds((half, m, n), BF16)],
        grid=(m // tm, half),
        in_specs=[pl.BlockSpec((tm, k), lambda i, j: (i, 0)),
                  pl.BlockSpec((None, k, n), lambda i, j: (j, 0, 0)),
                  pl.BlockSpec((None, k, n), lambda i, j: (j + half, 0, 0))],
        out_specs=[act, act, act], scratch_shapes=[], sem=("parallel", "parallel"))
    return res[0], res[1], res[2], ridden


def ffn_out_dx_swiglu(dy, w4, gate, up, name, tm=1024, riders=None):
    m, nn = dy.shape
    half, n, _ = w4.shape
    tm = _div_tile(m, tm)

    def body(dy_ref, w_ref, g_ref, u_ref, o_ref):
        ds = lax.dot_general(dy_ref[...], w_ref[...], NT, preferred_element_type=F32)
        g, u = g_ref[...], u_ref[...]
        sig = _sigmoid(g)
        o_ref[0] = (ds * u * (sig * (1 + g * (1 - sig)))).astype(BF16)
        o_ref[1] = (ds * (g * sig)).astype(BF16)

    act = pl.BlockSpec((None, tm, n), lambda i, j: (j, i, 0))
    res, ridden = _rider_call(
        body, (dy, w4, gate, up), riders, name=name, out_shape=[_sds((2, half, m, n), BF16)],
        grid=(m // tm, half),
        in_specs=[pl.BlockSpec((tm, nn), lambda i, j: (i, 0)),
                  pl.BlockSpec((None, n, nn), lambda i, j: (j, 0, 0)), act, act],
        out_specs=[pl.BlockSpec((2, None, tm, n), lambda i, j: (0, j, i, 0))],
        scratch_shapes=[], sem=("parallel", "parallel"))
    return res[0].reshape(2 * half, m, n), ridden


def _split_hi_lo(v):
    hi = v.astype(BF16)
    lo = (v - hi.astype(F32)).astype(BF16)
    return hi, lo


SB_G = 2
SB_EXP_ZERO = 104.0
SB_UNSEEN = 3e38
SB_GW = SB_G * BLK


def _sb_specs(t):
    nq = t // BLK
    npair = N_HEADS // 2
    q_spec = pl.BlockSpec((BLK, BLK), lambda p, qb: (qb, p))
    k_spec = pl.BlockSpec((t, BLK), lambda p, qb: (0, npair + p))
    v_spec = pl.BlockSpec((t, BLK), lambda p, qb: (0, 2 * npair + p))
    kt_spec = pl.BlockSpec((t // SB_GW, BLK, SB_GW), lambda p, qb: (0, npair + p, 0))
    vt_spec = pl.BlockSpec((t // SB_GW, BLK, SB_GW), lambda p, qb: (0, 2 * npair + p, 0))
    c_spec = pl.BlockSpec((None, nq, 8, 2 * BLK), lambda p, qb: (p, 0, 0, qb))
    return nq, npair, q_spec, k_spec, v_spec, kt_spec, vt_spec, c_spec


def _sb_consts():
    row = lax.broadcasted_iota(jnp.int32, (BLK, BLK), 0)
    col = lax.broadcasted_iota(jnp.int32, (BLK, BLK), 1)
    lane0 = (col < HEAD_DIM).astype(F32)
    sub0 = (row < HEAD_DIM).astype(F32)
    return row, col, lane0, sub0


def _sb_valid(ks, qb):
    row = lax.broadcasted_iota(jnp.int32, (SB_GW, 2 * BLK), 0)
    col = lax.broadcasted_iota(jnp.int32, (SB_GW, 2 * BLK), 1)
    return (ks + row) < (qb * BLK + (col & (BLK - 1)))


def _blocks_on_lanes(v4):
    return jnp.concatenate([v4[b * BLK:(b + 1) * BLK] for b in range(SB_G)], axis=1)


def _tri2_dot(tri2, v):
    hi, lo = _split_hi_lo(v)
    return jnp.dot(tri2, jnp.concatenate([hi, lo], axis=0), preferred_element_type=F32)


def _sb_pair_loop(first, count, step, group, skip, carry):
    def pair(it, cy):
        g1 = first + 2 * step * it
        cy = group(g1, 0, 1, cy)
        return lax.cond(2 * it + 1 < count, lambda c: group(g1 + step, 1, 0, c), skip, cy)
    return lax.fori_loop(0, (count + 1) // 2, pair, carry)


def sb_fwd(qkv, qkv_t, riders, name):
    t = qkv.shape[0]
    assert t % SB_GW == 0
    nq, npair, q_spec, k_spec, _, _, vt_spec, c_spec = _sb_specs(t)
    nr = len(riders)

    def body(*refs):
        q_ref, k_ref, vt_ref = refs[:3]
        o_ref, c_ref = refs[3 + nr:5 + nr]
        oacc, zbuf0, zbuf1, kmax = refs[5 + 2 * nr:9 + 2 * nr]
        pp = pl.program_id(0)
        qb = pl.program_id(1)
        if nr:
            gather = _Gather(refs[3:3 + nr], refs[5 + nr:5 + 2 * nr], *refs[9 + 2 * nr:])
            pl.when((pp == 0) & (qb == 0))(gather.start)
            pl.when((pp == npair - 2) & (qb == 0))(gather.forward)
        _sb_fwd_step(q_ref, k_ref, vt_ref, o_ref, c_ref, oacc, zbuf0, zbuf1, kmax, qb)
        if nr:
            pl.when((pp == npair - 1) & (qb == nq - 1))(gather.finish)

    any_spec = pl.BlockSpec(memory_space=pl.ANY)
    outs = _call(
        body, name=name,
        out_shape=[_sds((t, D_MODEL), BF16), _sds((npair, nq, 8, 2 * t), F32)] + _Gather.out_shapes(riders),
        grid=(npair, nq), in_specs=[q_spec, k_spec, vt_spec] + [any_spec] * nr,
        out_specs=[pl.BlockSpec((BLK, BLK), lambda p, qb: (qb, p)), c_spec] + [any_spec] * nr,
        scratch_shapes=[pltpu.VMEM((BLK, 2 * BLK), F32), pltpu.VMEM((SB_GW, 2 * BLK), F32),
                        pltpu.VMEM((SB_GW, 2 * BLK), F32), pltpu.VMEM((8, BLK), F32)] + _comm_sems(nr),
        compiler_params=_params("arbitrary", "arbitrary"),
    )(qkv, qkv, qkv_t, *riders)
    return outs[0], outs[1], list(outs[2:])


def _sb_fwd_step(q_ref, k_ref, vt_ref, o_ref, c_ref, oacc, zbuf0, zbuf1, kmax, qb):
    row, col, lane0, sub0 = _sb_consts()
    tri = (col >= row).astype(BF16)
    tri2 = jnp.concatenate([tri, tri], axis=1)
    q2 = _two_heads(q_ref[...], lane0, QK_SCALE)
    zbufs = (zbuf0, zbuf1)
    c_ref[...] = jnp.full(c_ref.shape, SB_UNSEEN, F32)
    oacc[...] = jnp.zeros_like(oacc)

    @pl.when(qb == 0)
    def _():
        ksq = jnp.square(k_ref[...].astype(F32))
        head0 = (lax.broadcasted_iota(jnp.int32, (1, BLK), 1) < HEAD_DIM).astype(F32)
        norms = jnp.maximum(jnp.sum(ksq * head0, axis=1, keepdims=True),
                            jnp.sum(ksq * (1.0 - head0), axis=1, keepdims=True))
        kmax[...] = jnp.broadcast_to(jnp.max(norms, axis=0, keepdims=True), kmax.shape)

    qsq = jnp.square(q2.astype(F32)).astype(BF16)
    qn2 = jnp.max(lax.dot_general(jnp.ones((8, BLK), BF16), qsq, NT, preferred_element_type=F32),
                  axis=0, keepdims=True)
    kk = kmax[0:1, :]
    zbound = jnp.sqrt(qn2 * jnp.concatenate([kk, kk], axis=1)) * 1.02

    def matters(cr):
        return (jnp.min(cr - zbound) <= SB_EXP_ZERO).astype(jnp.int32)

    def scores(g):
        ks = pl.multiple_of(g * SB_GW, SB_GW)
        return lax.dot_general(k_ref[pl.ds(ks, SB_GW), :], q2, NT, preferred_element_type=F32)

    def group(g, cur, nxt, cr, masked=False):
        z = zbufs[cur][...]
        zbufs[nxt][...] = scores(jnp.maximum(g - 1, 0))
        e = jnp.exp(-jnp.abs(z))
        sp = jnp.maximum(z, 0.0) + jnp.log(1.0 + e)
        if masked:
            valid = _sb_valid(g * SB_GW, qb)
            sp = jnp.where(valid, sp, 0.0)
        loc = _tri2_dot(tri2, _blocks_on_lanes(sp))
        parts = [None] * SB_G
        for b in reversed(range(SB_G)):
            rows = slice(b * BLK, (b + 1) * BLK)
            c_ref[g * SB_G + b] = jnp.broadcast_to(cr, (8, 2 * BLK))
            a = jnp.exp(z[rows] - (loc[:, 2 * b * BLK:2 * (b + 1) * BLK] + cr))
            if masked:
                a = jnp.where(valid[rows], a, 0.0)
            parts[b] = a.astype(BF16)
            cr = cr + jnp.sum(sp[rows], axis=0, keepdims=True)
        oacc[...] += jnp.dot(vt_ref[g], jnp.concatenate(parts, axis=0), preferred_element_type=F32)
        return cr

    last = qb // SB_G
    zbuf1[...] = scores(last)
    cr = group(last, 1, 0, jnp.zeros((1, 2 * BLK), F32), masked=True)

    def pair(state):
        g, cr, _ = state
        cr = group(g, 0, 1, cr)
        more = (g >= 1).astype(jnp.int32) * matters(cr)
        cr = lax.cond(more > 0, lambda c: group(g - 1, 1, 0, c), lambda c: c, cr)
        return jnp.where(more > 0, g - 2, -1), cr, matters(cr)

    lax.while_loop(lambda st: (st[0] >= 0) & (st[2] > 0), pair, (last - 1, cr, matters(cr)))
    o_t = oacc[:, :BLK] * sub0 + oacc[:, BLK:] * (1.0 - sub0)
    o_ref[...] = o_t.T.astype(BF16)


def sb_bwd(qkv, qkv_t, do, cmass, riders, name):
    t = qkv.shape[0]
    nq, npair, q_spec, k_spec, v_spec, kt_spec, _, c_spec = _sb_specs(t)
    nr = len(riders)

    def body(*refs):
        pp = pl.program_id(0)
        qb = pl.program_id(1)
        if nr:
            exchange = _Exchange(refs[6:6 + nr], refs[9 + nr:9 + 2 * nr], *refs[14 + 2 * nr:])
            pl.when((pp == 0) & (qb == 0))(exchange.start)
        step(*refs[:6], *refs[6 + nr:9 + nr], *refs[9 + 2 * nr:14 + 2 * nr])
        if nr:
            pl.when((pp == npair - 1) & (qb == nq - 1))(exchange.finish)

    def step(q_ref, k_ref, kt_ref, v_ref, do_ref, c_ref, dq_ref, dk_ref, dv_ref, dqacc, dkacc, dvacc,
             zbuf0, zbuf1):
        qb = pl.program_id(1)

        @pl.when(qb == 0)
        def _():
            dkacc[...] = jnp.zeros_like(dkacc)
            dvacc[...] = jnp.zeros_like(dvacc)

        row, col, lane0, sub0 = _sb_consts()
        tri_suf = (col >= row).astype(BF16)
        tri_pre = (col <= row).astype(BF16)
        tri2_suf = jnp.concatenate([tri_suf, tri_suf], axis=1)
        tri2_pre = jnp.concatenate([tri_pre, tri_pre], axis=1)
        q2 = _two_heads(q_ref[...], lane0, QK_SCALE)
        do2 = _two_heads(do_ref[...], lane0, 1.0)
        zbufs = (zbuf0, zbuf1)
        dqacc[...] = jnp.zeros_like(dqacc)
        last = qb // SB_G

        def scores(g):
            ks = pl.multiple_of(g * SB_GW, SB_GW)
            return lax.dot_general(k_ref[pl.ds(ks, SB_GW), :], q2, NT, preferred_element_type=F32)

        def group(g, cur, nxt, gc, masked=False):
            ks = pl.multiple_of(g * SB_GW, SB_GW)
            z = zbufs[cur][...]
            zbufs[nxt][...] = scores(jnp.minimum(g + 1, last))
            e = jnp.exp(-jnp.abs(z))
            sig = 0.5 * jnp.tanh(0.5 * z) + 0.5
            sp = jnp.maximum(z, 0.0) + jnp.log(1.0 + e)
            if masked:
                valid = _sb_valid(ks, qb)
                sp = jnp.where(valid, sp, 0.0)
            loc = _tri2_dot(tri2_suf, _blocks_on_lanes(sp))
            parts = []
            for b in range(SB_G):
                rows = slice(b * BLK, (b + 1) * BLK)
                mass = loc[:, 2 * b * BLK:2 * (b + 1) * BLK] + c_ref[g * SB_G + b, 0:1, :]
                parts.append(jnp.exp(z[rows] - mass))
            a = jnp.concatenate(parts, axis=0)
            if masked:
                a = jnp.where(valid, a, 0.0)
            gr = lax.dot_general(v_ref[pl.ds(ks, SB_GW), :], do2, NT, preferred_element_type=F32) * a
            pre = _tri2_dot(tri2_pre, _blocks_on_lanes(gr))
            parts = []
            for b in range(SB_G):
                rows = slice(b * BLK, (b + 1) * BLK)
                parts.append(pre[:, 2 * b * BLK:2 * (b + 1) * BLK] + gc)
                gc = gc + jnp.sum(gr[rows], axis=0, keepdims=True)
            dz = gr - sig * jnp.concatenate(parts, axis=0)
            if masked:
                dz = jnp.where(valid, dz, 0.0)
            dz = dz.astype(BF16)
            dkacc[pl.ds(ks, SB_GW), :] += jnp.dot(dz, q2, preferred_element_type=F32)
            dqacc[...] += jnp.dot(kt_ref[g], dz, preferred_element_type=F32)
            dvacc[pl.ds(ks, SB_GW), :] += jnp.dot(a.astype(BF16), do2, preferred_element_type=F32)
            return gc

        def skip(gc):
            zbuf0[...] = zbuf1[...]
            return gc

        def unseen(g):
            return (jnp.max(c_ref[g * SB_G + SB_G - 1, 0:1, :]) > 0.5 * SB_UNSEEN).astype(jnp.int32)

        first, _ = lax.while_loop(lambda st: (st[0] < last) & (st[1] > 0),
                                  lambda st: (st[0] + 1, unseen(st[0] + 1)), (jnp.int32(0), unseen(0)))
        zbuf0[...] = scores(first)
        gc = _sb_pair_loop(first, last - first, 1, group, skip, jnp.zeros((1, 2 * BLK), F32))
        group(last, 0, 1, gc, masked=True)
        dq_t = (dqacc[:, :BLK] * sub0 + dqacc[:, BLK:] * (1.0 - sub0)) * QK_SCALE
        dq_ref[...] = dq_t.T.astype(BF16)

        @pl.when(qb == nq - 1)
        def _():
            dk_ref[...] = dkacc[...].astype(BF16)
            dv_ref[...] = dvacc[...].astype(BF16)

    col_spec = pl.BlockSpec((t, BLK), lambda p, qb: (0, p))
    blk_spec = pl.BlockSpec((BLK, BLK), lambda p, qb: (qb, p))
    any_spec = pl.BlockSpec(memory_space=pl.ANY)
    outs = _call(
        body, name=name,
        out_shape=[_sds((t, D_MODEL), BF16)] * 3 + [_sds(r.shape, r.dtype) for r in riders],
        grid=(npair, nq), in_specs=[q_spec, k_spec, kt_spec, v_spec, blk_spec, c_spec] + [any_spec] * nr,
        out_specs=[blk_spec, col_spec, col_spec] + [any_spec] * nr,
        scratch_shapes=[pltpu.VMEM((BLK, 2 * BLK), F32), pltpu.VMEM((t, BLK), F32), pltpu.VMEM((t, BLK), F32),
                        pltpu.VMEM((SB_GW, 2 * BLK), F32), pltpu.VMEM((SB_GW, 2 * BLK), F32)] + _comm_sems(nr),
        compiler_params=_params("arbitrary", "arbitrary"),
    )(qkv, qkv, qkv_t, qkv, do, cmass, *riders)
    return outs[0], outs[1], outs[2], list(outs[3:])


BAND_QPS = 2


def _band_static_mask(jj):
    row = lax.broadcasted_iota(jnp.int32, (2 * BLK, BLK), 0)
    col = lax.broadcasted_iota(jnp.int32, (2 * BLK, BLK), 1)
    qc = (row & (BLK - 1)) // 64
    kc = 2 * jj + col // 64
    return (kc >= qc) & (kc <= qc + 8)


def _band_key_start(qb, jj):
    kb = qb - (BAND_BLOCKS - 1) + jj
    return kb, pl.multiple_of(jnp.maximum(kb, 0) * BLK, BLK)


def _band_probs(q2, k_ref, bias, qb):
    blocks = []
    for jj in range(BAND_BLOCKS):
        kb, ks = _band_key_start(qb, jj)
        s = lax.dot_general(q2, k_ref[pl.ds(ks, BLK), :], NT, preferred_element_type=F32)
        s = s + bias[:, jj * BLK:(jj + 1) * BLK]
        ok = (kb >= 0) if 0 < jj < BAND_BLOCKS - 1 else _band_static_mask(jj) & (kb >= 0)
        blocks.append(jnp.where(ok, s, NEG))
    s = jnp.concatenate(blocks, axis=1)
    m = jnp.max(s, axis=-1, keepdims=True)
    e = jnp.exp(s - m)
    return e / jnp.sum(e, axis=-1, keepdims=True)


def _band_specs(t):
    npair = N_HEADS // 2
    rows = BAND_QPS * BLK
    q_spec = pl.BlockSpec((rows, BLK), lambda p, i: (i, p))
    k_spec = pl.BlockSpec((t, BLK), lambda p, i: (0, npair + p))
    v_spec = pl.BlockSpec((t, BLK), lambda p, i: (0, 2 * npair + p))
    b_spec = pl.BlockSpec((2, BLK, BAND_W), lambda p, i: (p, 0, 0))
    return npair, t // rows, q_spec, k_spec, v_spec, b_spec


def _two_heads(xv, lane0, scale):
    xf = xv.astype(F32)
    if scale != 1.0:
        xf = xf * scale
    return jnp.concatenate([xf * lane0, xf * (1.0 - lane0)], axis=0).astype(BF16)


def _one_of_two_heads(r, lane0):
    return r[:BLK] * lane0 + r[BLK:] * (1.0 - lane0)


def band_fwd(qkv, bias, name):
    t = qkv.shape[0]
    assert t % (BAND_QPS * BLK) == 0
    npair, nsteps, q_spec, k_spec, v_spec, b_spec = _band_specs(t)

    def body(q_ref, k_ref, v_ref, b_ref, o_ref):
        step = pl.program_id(1)
        _, _, lane0, _ = _sb_consts()
        bias2 = b_ref[...].reshape(2 * BLK, BAND_W)
        for u in range(BAND_QPS):
            qb = step * BAND_QPS + u
            rows = slice(u * BLK, (u + 1) * BLK)
            q2 = _two_heads(q_ref[rows, :], lane0, QK_SCALE)
            p = _band_probs(q2, k_ref, bias2, qb)
            acc = jnp.zeros((2 * BLK, BLK), F32)
            for jj in range(BAND_BLOCKS):
                _, ks = _band_key_start(qb, jj)
                acc += jnp.dot(p[:, jj * BLK:(jj + 1) * BLK].astype(BF16), v_ref[pl.ds(ks, BLK), :],
                               preferred_element_type=F32)
            o_ref[rows, :] = _one_of_two_heads(acc, lane0).astype(BF16)

    return _call(
        body, name=name, out_shape=_sds((t, D_MODEL), BF16), grid=(npair, nsteps),
        in_specs=[q_spec, k_spec, v_spec, b_spec],
        out_specs=pl.BlockSpec((BAND_QPS * BLK, BLK), lambda p, i: (i, p)),
        compiler_params=_params("parallel", "parallel"),
    )(qkv, qkv, qkv, bias)


def band_bwd(qkv, do, bias, name, riders=None):
    t = qkv.shape[0]
    npair, nsteps, q_spec, k_spec, v_spec, b_spec = _band_specs(t)

    def body(q_ref, k_ref, v_ref, do_ref, b_ref, dq_ref, dk_ref, dv_ref, db_ref, dkacc, dvacc):
        step = pl.program_id(1)

        @pl.when(step == 0)
        def _():
            dkacc[...] = jnp.zeros_like(dkacc)
            dvacc[...] = jnp.zeros_like(dvacc)
            db_ref[...] = jnp.zeros_like(db_ref)

        _, _, lane0, _ = _sb_consts()
        bias2 = b_ref[...].reshape(2 * BLK, BAND_W)
        updates = []
        for u in range(BAND_QPS):
            qb = step * BAND_QPS + u
            rows = slice(u * BLK, (u + 1) * BLK)
            q2 = _two_heads(q_ref[rows, :], lane0, QK_SCALE)
            do2 = _two_heads(do_ref[rows, :], lane0, 1.0)
            p = _band_probs(q2, k_ref, bias2, qb)
            dp = jnp.concatenate(
                [lax.dot_general(do2, v_ref[pl.ds(_band_key_start(qb, jj)[1], BLK), :], NT,
                                 preferred_element_type=F32) for jj in range(BAND_BLOCKS)], axis=1)
            ds = p * (dp - jnp.sum(p * dp, axis=-1, keepdims=True))
            db_ref[...] += ds.reshape(2, BLK, BAND_W)
            dqa = jnp.zeros((2 * BLK, BLK), F32)
            for jj in range(BAND_BLOCKS):
                _, ks = _band_key_start(qb, jj)
                dsb = ds[:, jj * BLK:(jj + 1) * BLK].astype(BF16)
                pb = p[:, jj * BLK:(jj + 1) * BLK].astype(BF16)
                dqa += jnp.dot(dsb, k_ref[pl.ds(ks, BLK), :], preferred_element_type=F32)
                updates.append((ks, lax.dot_general(dsb, q2, TN, preferred_element_type=F32),
                                lax.dot_general(pb, do2, TN, preferred_element_type=F32)))
            dq_ref[rows, :] = (_one_of_two_heads(dqa, lane0) * QK_SCALE).astype(BF16)
        for ks, dk_part, dv_part in updates:
            dkacc[pl.ds(ks, BLK), :] += dk_part
            dvacc[pl.ds(ks, BLK), :] += dv_part

        @pl.when(step == nsteps - 1)
        def _():
            dk_ref[...] = dkacc[...].astype(BF16)
            dv_ref[...] = dvacc[...].astype(BF16)

    col_spec = pl.BlockSpec((t, BLK), lambda p, i: (0, p))
    blk_spec = pl.BlockSpec((BAND_QPS * BLK, BLK), lambda p, i: (i, p))
    res, ridden = _rider_call(
        body, (qkv, qkv, qkv, do, bias), riders, name=name,
        out_shape=[_sds((t, D_MODEL), BF16)] * 3 + [_sds((N_HEADS, BLK, BAND_W), F32)],
        grid=(npair, nsteps), in_specs=[q_spec, k_spec, v_spec, blk_spec, b_spec],
        out_specs=[blk_spec, col_spec, col_spec, b_spec],
        scratch_shapes=[pltpu.VMEM((t, BLK), F32), pltpu.VMEM((t, BLK), F32)],
        sem=("parallel", "arbitrary"))
    return res[0], res[1], res[2], res[3], ridden


def band_bias_window(rel_bias):
    far = BAND_W + BLK - 1 - 2 * REL_CLIP
    width = BAND_W + BLK
    ext = jnp.concatenate(
        [jnp.broadcast_to(rel_bias[:, 2 * REL_CLIP:], (N_HEADS, far)), rel_bias[:, 2 * REL_CLIP:0:-1],
         jnp.zeros((N_HEADS, 2), F32)], axis=1)
    tiled = jnp.broadcast_to(ext[:, None, :], (N_HEADS, BLK, width + 1)).reshape(N_HEADS, BLK * (width + 1))
    return tiled[:, BLK - 1:BLK - 1 + BLK * width].reshape(N_HEADS, BLK, width)[:, :, :BAND_W]


def band_bias_window_grad(dwin):
    width = BAND_W + BLK
    far = BAND_W + BLK - 1 - 2 * REL_CLIP
    flat = jnp.pad(dwin, ((0, 0), (0, 0), (0, BLK))).reshape(N_HEADS, BLK * width)
    skew = jnp.pad(flat, ((0, 0), (BLK - 1, 1))).reshape(N_HEADS, BLK, width + 1)
    dext = jnp.sum(skew, axis=1)[:, :width - 1]
    return jnp.concatenate(
        [jnp.zeros((N_HEADS, 1), F32), dext[:, :far - 1:-1][:, :2 * REL_CLIP - 1],
         dext[:, far:far + 1] + jnp.sum(dext[:, :far], axis=1, keepdims=True)], axis=1)


SG_GROUPS = 8


def _gelu_parts(x):
    inner = GELU_C0 * (x + GELU_C1 * (x * x * x))
    th = jnp.tanh(inner)
    return th, 0.5 * x * (1.0 + th)


def _sg_gate_mask():
    row = lax.broadcasted_iota(jnp.int32, (BLK, BLK), 0)
    col = lax.broadcasted_iota(jnp.int32, (BLK, BLK), 1)
    return (row // 64) >= (col // 64)


def _sg_forward_parts(a, lng):
    w = a.shape[1] // 2
    th, z = _gelu_parts(a)
    u, v = z[:, :w], z[:, w:]
    mu = jnp.mean(v, axis=-1, keepdims=True)
    xc = v - mu
    rstd = lax.rsqrt(jnp.mean(xc * xc, axis=-1, keepdims=True) + EPS)
    vhat = xc * rstd
    return th, u, vhat, rstd, vhat * lng


def sg_fwd(a, lng, ws, bias_t, name):
    t, w2 = a.shape
    w = w2 // 2
    gc = w // SG_GROUPS

    def body(a_ref, lng_ref, ws_ref, bt_ref, y_ref):
        _, u, _, _, vln = _sg_forward_parts(a_ref[...], lng_ref[...])
        mask = _sg_gate_mask()
        bt = bt_ref[...]
        lane = lax.broadcasted_iota(jnp.int32, (BLK, BLK), 1)
        for g in range(SG_GROUPS):
            wg = jnp.where(mask, ws_ref[g], 0.0).astype(BF16)
            sv = jnp.dot(wg, vln[:, g * gc:(g + 1) * gc].astype(BF16), preferred_element_type=F32)
            bg = jnp.sum(jnp.where(lane == g, bt, 0.0), axis=-1, keepdims=True)
            y_ref[:, g * gc:(g + 1) * gc] = (u[:, g * gc:(g + 1) * gc] * (sv + bg)).astype(BF16)

    return _call(
        body, name=name, out_shape=_sds((t, w), BF16), grid=(t // BLK,),
        in_specs=[pl.BlockSpec((BLK, w2), lambda i: (i, 0)), pl.BlockSpec((1, w), lambda i: (0, 0)),
                  pl.BlockSpec((SG_GROUPS, BLK, BLK), lambda i: (0, 0, 0)),
                  pl.BlockSpec((BLK, BLK), lambda i: (0, 0))],
        out_specs=pl.BlockSpec((BLK, w), lambda i: (i, 0)),
        compiler_params=_params("parallel"),
    )(a, lng, ws, bias_t)


def sg_bwd(a, dy, lng, ws, bias_t, name):
    t, w2 = a.shape
    w = w2 // 2
    gc = w // SG_GROUPS

    def body(a_ref, dy_ref, lng_ref, ws_ref, bt_ref, da_ref, dlng_ref, dws_ref, dbt_ref):
        @pl.when(pl.program_id(0) == 0)
        def _():
            dlng_ref[...] = jnp.zeros_like(dlng_ref)
            dws_ref[...] = jnp.zeros_like(dws_ref)
            dbt_ref[...] = jnp.zeros_like(dbt_ref)

        av, lng = a_ref[...], lng_ref[...]
        th, u, vhat, rstd, vln = _sg_forward_parts(av, lng)
        mask = _sg_gate_mask()
        bt = bt_ref[...]
        lane = lax.broadcasted_iota(jnp.int32, (BLK, BLK), 1)
        dyv = dy_ref[...]
        du_parts, dvln_parts = [], []
        dbt = jnp.zeros((BLK, BLK), F32)
        for g in range(SG_GROUPS):
            sl = slice(g * gc, (g + 1) * gc)
            wg = jnp.where(mask, ws_ref[g], 0.0).astype(BF16)
            vg = vln[:, sl].astype(BF16)
            sv = jnp.dot(wg, vg, preferred_element_type=F32)
            bg = jnp.sum(jnp.where(lane == g, bt, 0.0), axis=-1, keepdims=True)
            dyg = dyv[:, sl]
            du_parts.append(dyg * (sv + bg))
            dsv = dyg * u[:, sl]
            dbt += jnp.where(lane == g, jnp.sum(dsv, axis=-1, keepdims=True), 0.0)
            dsvb = dsv.astype(BF16)
            dws_ref[g] += jnp.where(mask, lax.dot_general(dsvb, vg, NT, preferred_element_type=F32), 0.0)
            dvln_parts.append(lax.dot_general(wg, dsvb, TN, preferred_element_type=F32))
        dbt_ref[...] += dbt
        du = jnp.concatenate(du_parts, axis=1)
        dvln = jnp.concatenate(dvln_parts, axis=1)
        dlng_ref[...] += _colsum8(dvln * vhat)
        dvhat = dvln * lng
        dv = rstd * (dvhat - jnp.mean(dvhat, axis=-1, keepdims=True)
                     - vhat * jnp.mean(dvhat * vhat, axis=-1, keepdims=True))
        dz = jnp.concatenate([du, dv], axis=1)
        dgelu = 0.5 * (1.0 + th) + (0.5 * av) * (1.0 - th * th) * (GELU_C0 * (1.0 + 3.0 * GELU_C1 * (av * av)))
        da_ref[...] = (dz * dgelu).astype(BF16)

    return _call(
        body, name=name,
        out_shape=[_sds((t, w2), BF16), _sds((8, w), F32), _sds((SG_GROUPS, BLK, BLK), F32), _sds((BLK, BLK), F32)],
        grid=(t // BLK,),
        in_specs=[pl.BlockSpec((BLK, w2), lambda i: (i, 0)), pl.BlockSpec((BLK, w), lambda i: (i, 0)),
                  pl.BlockSpec((1, w), lambda i: (0, 0)),
                  pl.BlockSpec((SG_GROUPS, BLK, BLK), lambda i: (0, 0, 0)),
                  pl.BlockSpec((BLK, BLK), lambda i: (0, 0))],
        out_specs=[pl.BlockSpec((BLK, w2), lambda i: (i, 0)), pl.BlockSpec((8, w), lambda i: (0, 0)),
                   pl.BlockSpec((SG_GROUPS, BLK, BLK), lambda i: (0, 0, 0)),
                   pl.BlockSpec((BLK, BLK), lambda i: (0, 0))],
        compiler_params=_params("arbitrary"),
    )(a, dy, lng, ws, bias_t)


def _shift_down(cat, n, tr):
    return pltpu.roll(cat, n, 0)[8:8 + tr]


def _shift_up(cat, n, tr):
    return pltpu.roll(cat, tr + 8 - n, 0)[0:tr]


def conv_fwd(p, cw, name):
    t, d3 = p.shape
    d = d3 // 3
    tr = min(256, t)
    hb = tr // 8

    def body(p_ref, ph_ref, cw_ref, o_ref):
        i = pl.program_id(0)
        pv = p_ref[...]
        y = pv[:, d:2 * d] * pv[:, 2 * d:]
        ph = ph_ref[...]
        yh = jnp.where(i > 0, ph[:, d:2 * d] * ph[:, 2 * d:], 0.0)
        cat = jnp.concatenate([yh, y], axis=0)
        yc = (cw_ref[0:1, :] * _shift_down(cat, 2, tr) + cw_ref[1:2, :] * _shift_down(cat, 1, tr)
              + cw_ref[2:3, :] * y)
        o_ref[...] = (pv[:, :d] * yc).astype(BF16)

    return _call(
        body, name=name, out_shape=_sds((t, d), BF16), grid=(t // tr,),
        in_specs=[pl.BlockSpec((tr, d3), lambda i: (i, 0)),
                  pl.BlockSpec((8, d3), lambda i: (jnp.maximum(i * hb - 1, 0), 0)),
                  pl.BlockSpec((8, d), lambda i: (0, 0))],
        out_specs=pl.BlockSpec((tr, d), lambda i: (i, 0)),
        compiler_params=_params("parallel"),
    )(p, p, cw)


def conv_bwd(p, dz, cw, name):
    t, d3 = p.shape
    d = d3 // 3
    tr = min(256, t)
    hb = tr // 8
    nt = t // tr

    def body(p_ref, ph_ref, pn_ref, dz_ref, dzn_ref, cw_ref, dp_ref, dcw_ref):
        i = pl.program_id(0)

        @pl.when(i == 0)
        def _():
            dcw_ref[...] = jnp.zeros_like(dcw_ref)

        pv = p_ref[...]
        gb, gcv, xt = pv[:, :d], pv[:, d:2 * d], pv[:, 2 * d:]
        y = gcv * xt
        ph = ph_ref[...]
        yh = jnp.where(i > 0, ph[:, d:2 * d] * ph[:, 2 * d:], 0.0)
        cat = jnp.concatenate([yh, y], axis=0)
        y2, y1 = _shift_down(cat, 2, tr), _shift_down(cat, 1, tr)
        w0, w1, w2 = cw_ref[0:1, :], cw_ref[1:2, :], cw_ref[2:3, :]
        yc = w0 * y2 + w1 * y1 + w2 * y
        dzv = dz_ref[...]
        dyc = dzv * gb
        dcw_ref[0] += _colsum8(dyc * y2)
        dcw_ref[1] += _colsum8(dyc * y1)
        dcw_ref[2] += _colsum8(dyc * y)
        dycn = jnp.where(i < nt - 1, dzn_ref[...] * pn_ref[...][:, :d], 0.0)
        catn = jnp.concatenate([dyc, dycn], axis=0)
        dy = w2 * dyc + w1 * _shift_up(catn, 1, tr) + w0 * _shift_up(catn, 2, tr)
        dp_ref[:, :d] = (dzv * yc).astype(BF16)
        dp_ref[:, d:2 * d] = (dy * xt).astype(BF16)
        dp_ref[:, 2 * d:] = (dy * gcv).astype(BF16)

    nxt = lambda i: (jnp.minimum((i + 1) * hb, t // 8 - 1), 0)
    return _call(
        body, name=name, out_shape=[_sds((t, d3), BF16), _sds((3, 8, d), F32)], grid=(nt,),
        in_specs=[pl.BlockSpec((tr, d3), lambda i: (i, 0)),
                  pl.BlockSpec((8, d3), lambda i: (jnp.maximum(i * hb - 1, 0), 0)),
                  pl.BlockSpec((8, d3), nxt),
                  pl.BlockSpec((tr, d), lambda i: (i, 0)),
                  pl.BlockSpec((8, d), nxt),
                  pl.BlockSpec((8, d), lambda i: (0, 0))],
        out_specs=[pl.BlockSpec((tr, d3), lambda i: (i, 0)), pl.BlockSpec((3, 8, d), lambda i: (0, 0, 0))],
        compiler_params=_params("arbitrary"),
    )(p, p, p, dz, dz, cw)


def ada_fwd(c_all, w, b, name):
    nl, d, n = w.shape

    def body(c_ref, w_ref, b_ref, o_ref):
        cv = c_ref[...]
        s = (cv * _sigmoid(cv)).astype(BF16)
        o_ref[...] = jnp.dot(s, w_ref[...].astype(BF16), preferred_element_type=F32) + b_ref[...]

    return _call(
        body, name=name, out_shape=_sds((nl, N_DEV, n), F32), grid=(nl,),
        in_specs=[pl.BlockSpec((N_DEV, d), lambda l: (0, 0)), pl.BlockSpec((None, d, n), lambda l: (l, 0, 0)),
                  pl.BlockSpec((None, 1, n), lambda l: (l, 0, 0))],
        out_specs=pl.BlockSpec((None, N_DEV, n), lambda l: (l, 0, 0)),
        compiler_params=_params("parallel"),
    )(c_all, w, b)


def ada_bwd(c_all, dmod, name):
    nl, _, n = dmod.shape
    d = c_all.shape[1]

    def body(c_ref, dm_ref, o_ref):
        cv = c_ref[...]
        s = (cv * _sigmoid(cv)).astype(BF16)
        o_ref[...] = lax.dot_general(s, dm_ref[...].astype(BF16), TN, preferred_element_type=F32)

    return _call(
        body, name=name, out_shape=_sds((nl, d, n), F32), grid=(nl,),
        in_specs=[pl.BlockSpec((N_DEV, d), lambda l: (0, 0)), pl.BlockSpec((None, N_DEV, n), lambda l: (l, 0, 0))],
        out_specs=pl.BlockSpec((None, d, n), lambda l: (l, 0, 0)),
        compiler_params=_params("parallel"),
    )(c_all, dmod)


def adamw(pieces, w, m, v, name):
    npc, r, c = pieces.shape
    tr = r
    for cand in (1024, 512, 256, 128, 64, 32, 16, 8):
        if r % cand == 0 and cand * c * 4 <= (1 << 20):
            tr = cand
            break

    def body(p_ref, w_ref, m_ref, v_ref, g_ref, d_ref, nm_ref, nv_ref):
        g = p_ref[0].astype(F32)
        for i in range(1, npc):
            g = g + p_ref[i].astype(F32)
        wv = w_ref[...]
        nm = ADAM_B1 * m_ref[...] + (1.0 - ADAM_B1) * g
        nv = ADAM_B2 * v_ref[...] + (1.0 - ADAM_B2) * (g * g)
        m_hat = nm / (1.0 - ADAM_B1 ** ADAM_STEP)
        v_hat = nv / (1.0 - ADAM_B2 ** ADAM_STEP)
        g_ref[...] = g
        d_ref[...] = -ADAM_LR * (m_hat / (jnp.sqrt(v_hat) + ADAM_EPS) + ADAM_WD * wv)
        nm_ref[...] = nm
        nv_ref[...] = nv

    row = pl.BlockSpec((tr, c), lambda i: (i, 0))
    return _call(
        body, name=name, out_shape=[_sds((r, c), F32)] * 4, grid=(r // tr,),
        in_specs=[pl.BlockSpec((npc, tr, c), lambda i: (0, i, 0)), row, row, row],
        out_specs=[row] * 4, compiler_params=_params("parallel"),
    )(pieces, w, m, v)


def sum_pieces(pieces, name):
    npc, r, c = pieces.shape

    def body(p_ref, o_ref):
        g = p_ref[0]
        for i in range(1, npc):
            g = g + p_ref[i]
        o_ref[...] = g

    return _call(body, name=name, out_shape=_sds((r, c), F32),
                 in_specs=[pl.BlockSpec(memory_space=pltpu.VMEM)],
                 out_specs=pl.BlockSpec(memory_space=pltpu.VMEM),
                 compiler_params=pltpu.CompilerParams(vmem_limit_bytes=VMEM_LIMIT))(pieces)


PACK_W = 1024


def _pack(arrs):
    flat = jnp.concatenate([a.reshape(-1).astype(F32) for a in arrs])
    rows = -(-flat.shape[0] // (8 * PACK_W)) * 8
    return jnp.pad(flat, (0, rows * PACK_W - flat.shape[0])).reshape(rows, PACK_W)


def _unpack(slab, shapes):
    flat = slab.reshape(-1)
    out, off = [], 0
    for s in shapes:
        n = 1
        for q in s:
            n *= q
        out.append(flat[off:off + n].reshape(s))
        off += n
    return out


def kernel(x, c, ada_w, ada_b, norm_g, ffn_w_in, ffn_w_out, sb_w_qkv, sb_w_o, sg_w_in, sg_ln_g, sg_w_s, sg_bias, sg_w_out, sc_w_in, sc_conv_w, sc_w_out, cb_w_qkv, cb_rel_bias, cb_w_o, loss_target, m_ada_w, m_ada_b, m_norm_g, m_ffn_w_in, m_ffn_w_out, m_sb_w_qkv, m_sb_w_o, m_sg_w_in, m_sg_ln_g, m_sg_w_s, m_sg_bias, m_sg_w_out, m_sc_w_in, m_sc_conv_w, m_sc_w_out, m_cb_w_qkv, m_cb_rel_bias, m_cb_w_o, v_ada_w, v_ada_b, v_norm_g, v_ffn_w_in, v_ffn_w_out, v_sb_w_qkv, v_sb_w_o, v_sg_w_in, v_sg_ln_g, v_sg_w_s, v_sg_bias, v_sg_w_out, v_sc_w_in, v_sc_conv_w, v_sc_w_out, v_cb_w_qkv, v_cb_rel_bias, v_cb_w_o):
    depth = ada_w.shape[0]
    d = D_MODEL
    xi, yi, ci = lax.axis_index("x"), lax.axis_index("y"), lax.axis_index("c")
    me = 4 * xi + 2 * yi + ci
    x0 = x[0]
    t = x0.shape[0]
    target = loss_target[0]

    c_all = _all_gather([jnp.pad(c, ((0, 7), (0, 0)))], "gather_c")[0][:, 0, :]
    na = ada_w.shape[2]
    b_cols = lax.dynamic_slice_in_dim(ada_b, me * na, na, axis=1)[:, None, :]
    mod_part = ada_fwd(c_all, ada_w, b_cols, "ada_fwd")
    mod_g = _all_gather([mod_part.reshape(depth * N_DEV, na)], "gather_mod")[0]
    mod_g = mod_g.reshape(N_DEV, depth, N_DEV, na)
    mod_me = lax.dynamic_index_in_dim(mod_g, me, axis=2, keepdims=False)
    mod = jnp.transpose(mod_me, (1, 0, 2)).reshape(depth, 6, 1, d)

    ng = _all_gather([norm_g.reshape(depth * 4, d // N_DEV)], "gather_norm_g")[0]
    norm_full = jnp.transpose(ng, (1, 0, 2)).reshape(depth, 4, 1, d)
    small = _all_gather([_pack([sg_ln_g, sc_conv_w])], "gather_small")[0].reshape(N_DEV, -1)
    nl_g = sg_ln_g.shape[1]
    ln_full = small[:, :nl_g].reshape(1, N_DEV * nl_g)
    cwn = sc_conv_w.shape[2]
    cw_sh = small[:, nl_g:nl_g + 3 * cwn].reshape(N_DEV, 3, cwn)
    cw_full = jnp.transpose(cw_sh, (1, 0, 2)).reshape(3, d)
    cw_pad = jnp.pad(cw_full, ((0, 5), (0, 0)))

    bf = lambda a: a.astype(BF16)
    mixers = [
        [bf(sb_w_qkv[0]), bf(sb_w_o[0])],
        [bf(sg_w_in[0]), bf(sg_w_out[0])],
        [bf(sc_w_in[0]), bf(sc_w_out[0])],
        [bf(cb_w_qkv[0]), bf(cb_w_o[0])],
    ]
    shards = [[bf(ffn_w_in[i]), bf(ffn_w_out[i])] + mixers[i % 4] for i in range(depth)]
    gathered = [[None] * 4 for _ in range(depth)]
    gathered[0][2] = _all_gather([shards[0][2]], "gather_w_qkv0")[0]
    riding_shards = [shards[0][0], shards[0][1], shards[0][3]] + shards[1]

    bias_win = band_bias_window(cb_rel_bias[0])
    ws = sg_w_s[0]
    bias_t = jnp.pad(sg_bias[0].T, ((0, 0), (0, BLK - SG_GROUPS)))

    saved = []
    xcur = x0
    h = pre_fwd(x0, norm_full[0, 0], mod[0, 0], mod[0, 1], "L0_pre_m")
    for i in range(depth):
        mi = i % 4
        sh_m, sc_m, gt_m, sh_f, sc_f, gt_f = [mod[i, j] for j in range(6)]
        g0, g1, g2, g3 = [norm_full[i, j] for j in range(4)]
        tag = "L%d_" % i
        sv = {"x_in": xcur, "h_m": h}
        nxt = shards[i + 1] if 0 < i < depth - 1 else None
        if mi == 0:
            qkv = mm_cs(h, gathered[0][2], tag + "qkv", out_dtype=BF16)
            qkv_t = jnp.transpose(qkv.reshape(t // SB_GW, SB_GW, 3 * d), (0, 2, 1))
            o, cmass, riding = sb_fwd(qkv, qkv_t, riding_shards, tag + "sb_fwd")
            gathered[0][0], gathered[0][1], gathered[0][3] = riding[:3]
            gathered[1] = riding[3:]
            sv.update(qkv=qkv, qkv_t=qkv_t, o=o, cmass=cmass)
            mixed = o
        else:
            w_in = gathered[i][2]
            out_dtype = BF16 if mi == 3 else F32
            if nxt is not None:
                pre, (gathered[i + 1][0],) = mm_cs(h, w_in, tag + "mix_in", out_dtype=out_dtype,
                                                   riders=(_Gather, [nxt[0]]))
            else:
                pre = mm_cs(h, w_in, tag + "mix_in", out_dtype=out_dtype)
            if mi == 1:
                mixed = sg_fwd(pre, ln_full, ws, bias_t, tag + "sg_fwd")
                sv.update(a=pre, yy=mixed)
            elif mi == 2:
                mixed = conv_fwd(pre, cw_pad, tag + "conv_fwd")
                sv.update(p=pre, gz=mixed)
            else:
                mixed = band_fwd(pre, bias_win, tag + "band_fwd")
                sv.update(qkv=pre, o=mixed)
        wfi, wfo, _, wmo = gathered[i]
        wfo4 = wfo.reshape(4, -1, d)
        y = mm(mixed, wmo.reshape(-1, d), tag + "mix_out")
        sv["y_m"] = y
        xmid, h2 = post_pre_fwd(xcur, y, g1, gt_m, g2, sh_f, sc_f, tag + "post_m_pre_f")
        sv["x_mid"] = xmid
        ag, au, s3, ridden = ffn_in_swiglu(h2, wfi, tag + "ffn_in",
                                           riders=(_Gather, nxt[1:]) if nxt is not None else None)
        if nxt is not None:
            gathered[i + 1][1:] = ridden
        y2 = mm_rs(s3, wfo4, tag + "ffn_out")
        sv.update(h_f=h2, ag=ag, au=au, s3=s3, y_f=y2)
        saved.append(sv)
        if i + 1 < depth:
            xcur, h = post_pre_fwd(xmid, y2, g3, gt_f, norm_full[i + 1, 0], mod[i + 1, 0], mod[i + 1, 1],
                                   tag + "post_f_pre_m")

    last = depth - 1
    dx, dy2, lpart, dgt_f, dg3 = post_loss_bwd(saved[last]["x_mid"], saved[last]["y_f"], norm_full[last, 3],
                                               mod[last, 5], target, "loss")
    loss = lax.psum(0.5 * jnp.sum(lpart) / d, ("x", "y", "c"))

    dmod_rows = [None] * depth
    dnorm_rows = [None] * depth
    big_pieces = [[None] * 4 for _ in range(depth)]
    pending_dwmi, pending_dwfi, pending_layers = None, [], []
    small_grads = {}
    for i in reversed(range(depth)):
        wfi, wfo, wmi, wmo = gathered[i]
        wfo4 = wfo.reshape(4, -1, d)
        wmo2 = wmo.reshape(-1, d)
        mi = i % 4
        sh_m, sc_m, gt_m, sh_f, sc_f, gt_f = [mod[i, j] for j in range(6)]
        g0, g1, g2, g3 = [norm_full[i, j] for j in range(4)]
        tag = "L%d_b_" % i
        sv = saved[i]
        da3, ridden = ffn_out_dx_swiglu(dy2, wfo4, sv["ag"], sv["au"], tag + "ffn_out_dx",
                                        riders=(_Exchange, [pending_dwmi]) if pending_dwmi is not None else None)
        if pending_dwmi is not None:
            big_pieces[i + 1][2] = ridden[0]
        dwfo = mm_rs_dw(sv["s3"], dy2, tag + "ffn_out_dw", out_dtype=BF16).reshape(N_DEV, -1, d)
        dh2, (big_pieces[i][1],) = mm_cs_dx(da3, wfi, tag + "ffn_in_dx", act_major=True, riders=(_Exchange, [dwfo]))
        dwfi = mm_cs_dw(sv["h_f"], da3, tag + "ffn_in_dw", act_major=True, out_dtype=BF16)
        dx, dy, dsh_f, dsc_f, dg2, dgt_m, dg1 = pre_post_bwd(
            dh2, sv["x_mid"], g2, sc_f, dx, sv["y_m"], g1, gt_m, tag + "pre_f_post_m")
        if mi == 0:
            do = mm_nt(dy, wmo2, tag + "wo_dx", out_dtype=BF16)
            dwmo = mm_tn(sv["o"], dy, tag + "wo_dw", out_dtype=BF16).reshape(N_DEV, -1, d)
            dq, dk, dv, ridden = sb_bwd(sv["qkv"], sv["qkv_t"], do, sv["cmass"],
                                        pending_dwfi + [dwfi, dwmo], tag + "sb_bwd")
            for n, j in enumerate(pending_layers):
                big_pieces[j][0] = ridden[n]
            big_pieces[0][0], big_pieces[0][3] = ridden[-2:]
            dmid = jnp.concatenate([dq, dk, dv], axis=1)
        elif mi == 1:
            dyy = mm_nt(dy, wmo2, tag + "sg_out_dx")
            dwmo = mm_tn(sv["yy"], dy, tag + "sg_out_dw", out_dtype=BF16).reshape(N_DEV, -1, d)
            dmid, dlng, dws, dbt = sg_bwd(sv["a"], dyy, ln_full, ws, bias_t, tag + "sg_bwd")
            small_grads.update(ln_g=jnp.sum(dlng, axis=0), w_s=dws, bias=dbt[:, :SG_GROUPS].T)
        elif mi == 2:
            dgz = mm_nt(dy, wmo2, tag + "sc_out_dx")
            dwmo = mm_tn(sv["gz"], dy, tag + "sc_out_dw", out_dtype=BF16).reshape(N_DEV, -1, d)
            dmid, dcw = conv_bwd(sv["p"], dgz, cw_pad, tag + "conv_bwd")
            small_grads.update(conv_w=jnp.sum(dcw, axis=1))
        else:
            do = mm_nt(dy, wmo2, tag + "wo_dx", out_dtype=BF16)
            dwmo = mm_tn(sv["o"], dy, tag + "wo_dw", out_dtype=BF16).reshape(N_DEV, -1, d)
            dq, dk, dv, dwin, (big_pieces[i][0], big_pieces[i][3]) = band_bwd(
                sv["qkv"], do, bias_win, tag + "band_bwd", riders=(_Exchange, [dwfi, dwmo]))
            dmid = jnp.concatenate([dq, dk, dv], axis=1)
            small_grads.update(rel_bias=band_bias_window_grad(dwin))
        if mi in (1, 2):
            dh, (big_pieces[i][3],) = mm_cs_dx(dmid, wmi, tag + "mix_in_dx", riders=(_Exchange, [dwmo]))
            pending_dwfi.append(dwfi)
            pending_layers.append(i)
        else:
            dh = mm_cs_dx(dmid, wmi, tag + "mix_in_dx")
        pending_dwmi = mm_cs_dw(sv["h_m"], dmid, tag + "mix_in_dw", out_dtype=BF16)
        if i > 0:
            dx, dy2_prev, dsh_m, dsc_m, dg0, dgt_f_prev, dg3_prev = pre_post_bwd(
                dh, sv["x_in"], g0, sc_m, dx, saved[i - 1]["y_f"], norm_full[i - 1, 3], mod[i - 1, 5],
                tag + "pre_m_post_f")
        else:
            dx, dsh_m, dsc_m, dg0 = pre_bwd(dh, sv["x_in"], g0, sc_m, dx, tag + "pre_m")
        dmod_rows[i] = jnp.stack([jnp.sum(q, axis=0) for q in (dsh_m, dsc_m, dgt_m, dsh_f, dsc_f, dgt_f)])
        dnorm_rows[i] = jnp.stack([jnp.sum(q, axis=0) for q in (dg0, dg1, dg2, dg3)])
        if i > 0:
            dy2, dgt_f, dg3 = dy2_prev, dgt_f_prev, dg3_prev
    big_pieces[0][2] = _all_to_all([pending_dwmi], "exchange_w_qkv0")[0]

    grad_x = dx[None]

    dmod_mine = jnp.stack(dmod_rows).reshape(depth, 6 * d)
    dnorm_mine = jnp.stack(dnorm_rows)
    small_list = [dnorm_mine, small_grads["ln_g"], small_grads["w_s"], small_grads["bias"],
                  small_grads["conv_w"], small_grads["rel_bias"]]
    small_shapes = [dmod_mine.shape] + [a.shape for a in small_list]
    slab = _pack([dmod_mine] + small_list)
    slab_g = _all_gather([slab], "gather_small_grads")[0]
    tot = sum_pieces(slab_g, "sum_small_grads")
    g_ada_b_full, g_norm, g_ln, g_ws, g_sbias, g_cw, g_rb = _unpack(tot, small_shapes)
    dmod_all = slab_g.reshape(N_DEV, -1)[:, :depth * 6 * d].reshape(N_DEV, depth, 6 * d)
    dmod_cols = lax.dynamic_slice_in_dim(dmod_all, me * na, na, axis=2)
    g_ada_w = ada_bwd(c_all, jnp.transpose(dmod_cols, (1, 0, 2)), "ada_bwd")

    nsh = d // N_DEV
    g_norm_sh = lax.dynamic_slice_in_dim(g_norm, me * nsh, nsh, axis=2)
    g_ln_sh = lax.dynamic_slice_in_dim(g_ln.reshape(1, -1), me * nl_g, nl_g, axis=1)
    g_cw_sh = lax.dynamic_slice_in_dim(g_cw, me * cwn, cwn, axis=1)[None]

    out_g, out_d, out_m, out_v = {}, {}, {}, {}

    def upd(name, pieces, w, m, v):
        r_c = pieces.shape[1:]
        g, dl, nm, nv = adamw(pieces, w.reshape(r_c), m.reshape(r_c), v.reshape(r_c), "adamw_" + name)
        out_g[name], out_d[name] = g.reshape(w.shape), dl.reshape(w.shape)
        out_m[name], out_v[name] = nm.reshape(w.shape), nv.reshape(w.shape)

    upd("ada_w", g_ada_w.reshape(1, depth * d, na), ada_w, m_ada_w, v_ada_w)
    nfi = ffn_w_in.shape[2]
    nfo = ffn_w_out.shape[1]
    pfi = jnp.concatenate([big_pieces[i][0] for i in range(depth)], axis=1)
    pfo = jnp.concatenate([big_pieces[i][1] for i in range(depth)], axis=1)
    upd("ffn_w_in", pfi, ffn_w_in, m_ffn_w_in, v_ffn_w_in)
    upd("ffn_w_out", pfo, ffn_w_out, m_ffn_w_out, v_ffn_w_out)
    upd("sb_w_qkv", big_pieces[0][2], sb_w_qkv, m_sb_w_qkv, v_sb_w_qkv)
    upd("sb_w_o", big_pieces[0][3], sb_w_o, m_sb_w_o, v_sb_w_o)
    upd("sg_w_in", big_pieces[1][2], sg_w_in, m_sg_w_in, v_sg_w_in)
    upd("sg_w_out", big_pieces[1][3], sg_w_out, m_sg_w_out, v_sg_w_out)
    upd("sc_w_in", big_pieces[2][2], sc_w_in, m_sc_w_in, v_sc_w_in)
    upd("sc_w_out", big_pieces[2][3], sc_w_out, m_sc_w_out, v_sc_w_out)
    upd("cb_w_qkv", big_pieces[3][2], cb_w_qkv, m_cb_w_qkv, v_cb_w_qkv)
    upd("cb_w_o", big_pieces[3][3], cb_w_o, m_cb_w_o, v_cb_w_o)

    small_names = ["ada_b", "norm_g", "sg_ln_g", "sg_w_s", "sg_bias", "sc_conv_w", "cb_rel_bias"]
    small_g = [g_ada_b_full, g_norm_sh, g_ln_sh, g_ws[None], g_sbias[None], g_cw_sh, g_rb[None]]
    small_w = [ada_b, norm_g, sg_ln_g, sg_w_s, sg_bias, sc_conv_w, cb_rel_bias]
    small_m = [m_ada_b, m_norm_g, m_sg_ln_g, m_sg_w_s, m_sg_bias, m_sc_conv_w, m_cb_rel_bias]
    small_v = [v_ada_b, v_norm_g, v_sg_ln_g, v_sg_w_s, v_sg_bias, v_sc_conv_w, v_cb_rel_bias]
    shapes = [w.shape for w in small_w]
    res = adamw(_pack(small_g)[None], _pack(small_w), _pack(small_m), _pack(small_v), "adamw_small")
    for nm_, gs, ds_, ms, vs in zip(small_names, *[_unpack(r, shapes) for r in res]):
        out_g[nm_], out_d[nm_], out_m[nm_], out_v[nm_] = gs, ds_, ms, vs

    order = ["ada_w", "ada_b", "norm_g", "ffn_w_in", "ffn_w_out", "sb_w_qkv", "sb_w_o", "sg_w_in", "sg_ln_g",
             "sg_w_s", "sg_bias", "sg_w_out", "sc_w_in", "sc_conv_w", "sc_w_out", "cb_w_qkv", "cb_rel_bias", "cb_w_o"]
    return (loss, grad_x, *[out_g[n] for n in order], *[out_d[n] for n in order],
            *[out_m[n] for n in order], *[out_v[n] for n in order])
```

```python
import jax
import jax.numpy as jnp
from jax import lax
from jax.experimental import pallas as pl
from jax.experimental.pallas import tpu as pltpu

F32 = jnp.float32
BF16 = jnp.bfloat16
MESH = pl.DeviceIdType.MESH

N_DEV = 8
D_MODEL = 1024
N_HEADS = 16
HEAD_DIM = 64
QK_SCALE = HEAD_DIM ** -0.5
BLK = 128
BAND_BLOCKS = 5
BAND_W = BAND_BLOCKS * BLK
REL_CLIP = 128
EPS = 1e-6
NEG = -1e30
GELU_C0 = 0.7978845608028654
GELU_C1 = 0.044715
ADAM_LR = 0.001
ADAM_B1 = 0.9
ADAM_B2 = 0.999
ADAM_EPS = 1e-08
ADAM_WD = 0.01
ADAM_STEP = 10
VMEM_LIMIT = 56 * 1024 * 1024


def _call(body, **kw):
    return pl.pallas_call(body, **kw)


def _params(*sem):
    return pltpu.CompilerParams(dimension_semantics=sem, vmem_limit_bytes=VMEM_LIMIT)


def _sds(shape, dtype):
    return jax.ShapeDtypeStruct(tuple(shape), dtype)


def _row_tile(t):
    return min(512, t)


def _me():
    x, y, c = lax.axis_index("x"), lax.axis_index("y"), lax.axis_index("c")
    return x, y, c


def _all_gather(arrs, name):
    n = len(arrs)

    def body(*refs):
        gather = _Gather(refs[:n], refs[n:2 * n], *refs[2 * n:])
        gather.start()
        gather.forward()
        gather.finish()

    any_spec = pl.BlockSpec(memory_space=pl.ANY)
    outs = _call(
        body,
        name=name,
        out_shape=_Gather.out_shapes(arrs),
        in_specs=[any_spec] * n,
        out_specs=[any_spec] * n,
        scratch_shapes=_comm_sems(n),
    )(*arrs)
    return list(outs)


def _comm_sems(n):
    if n == 0:
        return []
    return [pltpu.SemaphoreType.DMA((n, 7)), pltpu.SemaphoreType.DMA((n, 7)), pltpu.SemaphoreType.DMA((n,))]


class _Gather:
    def __init__(self, x_refs, o_refs, send_sems, recv_sems, local_sems):
        self.x_refs, self.o_refs = x_refs, o_refs
        self.send_sems, self.recv_sems, self.local_sems = send_sems, recv_sems, local_sems
        x, y, c = _me()
        self.c = c
        self.me, self.sibling = (x, y, c), (x, y, 1 - c)
        self.chips = [(1 - x, y), (x, 1 - y), (1 - x, 1 - y)]

    @staticmethod
    def out_shapes(arrs):
        return [_sds((N_DEV,) + a.shape, a.dtype) for a in arrs]

    def rows(self, a, block):
        px, py, pc = block
        return self.o_refs[a].at[4 * px + 2 * py + pc]

    def copy(self, a, k, block, to, own=False):
        return pltpu.make_async_remote_copy(
            src_ref=self.x_refs[a] if own else self.rows(a, block),
            dst_ref=self.rows(a, block),
            send_sem=self.send_sems.at[a, k],
            recv_sem=self.recv_sems.at[a, k],
            device_id=to,
            device_id_type=MESH,
        )

    def local(self, a):
        return pltpu.make_async_copy(self.x_refs[a], self.rows(a, self.me), self.local_sems.at[a])

    def first(self, a):
        cps = [self.copy(a, 0, self.me, self.sibling, own=True)]
        return cps + [self.copy(a, 1 + j, self.me, (*chip, self.c), own=True) for j, chip in enumerate(self.chips)]

    def passed(self, a):
        return [self.copy(a, 4 + j, (*chip, self.c), self.sibling) for j, chip in enumerate(self.chips)]

    def start(self):
        for a in range(len(self.x_refs)):
            self.local(a).start()
            for cp in self.first(a):
                cp.start()

    def forward(self):
        for a in range(len(self.x_refs)):
            passed = self.passed(a)
            for j, chip in enumerate(self.chips):
                self.copy(a, 1 + j, (*chip, self.c), self.me).wait_recv()
                passed[j].start()

    def finish(self):
        for a in range(len(self.x_refs)):
            self.copy(a, 0, self.sibling, self.me).wait_recv()
            for j, chip in enumerate(self.chips):
                self.copy(a, 4 + j, (*chip, 1 - self.c), self.me).wait_recv()
            for cp in self.first(a) + self.passed(a):
                cp.wait_send()
            self.local(a).wait()


class _Exchange:
    def __init__(self, x_refs, o_refs, send_sems, recv_sems, local_sems):
        self.x_refs, self.o_refs = x_refs, o_refs
        self.send_sems, self.recv_sems, self.local_sems = send_sems, recv_sems, local_sems
        x, y, c = _me()
        self.me = 4 * x + 2 * y + c
        self.peers = []
        for k in range(1, N_DEV):
            px = 1 - x if k & 4 else x
            py = 1 - y if k & 2 else y
            pc = 1 - c if k & 1 else c
            self.peers.append((px, py, pc))

    @staticmethod
    def out_shapes(arrs):
        return [_sds(a.shape, a.dtype) for a in arrs]

    def local(self, a):
        return pltpu.make_async_copy(self.x_refs[a].at[self.me], self.o_refs[a].at[self.me], self.local_sems.at[a])

    def copy(self, a, k, send):
        px, py, pc = self.peers[k]
        peer = 4 * px + 2 * py + pc
        return pltpu.make_async_remote_copy(
            src_ref=self.x_refs[a].at[peer],
            dst_ref=self.o_refs[a].at[self.me if send else peer],
            send_sem=self.send_sems.at[a, k], recv_sem=self.recv_sems.at[a, k],
            device_id=(px, py, pc), device_id_type=MESH)

    def start(self):
        for a in range(len(self.x_refs)):
            self.local(a).start()
            for k in range(N_DEV - 1):
                self.copy(a, k, True).start()

    def finish(self):
        for a in range(len(self.x_refs)):
            for k in range(N_DEV - 1):
                self.copy(a, k, False).wait_recv()
            for k in range(N_DEV - 1):
                self.copy(a, k, True).wait_send()
            self.local(a).wait()


def _all_to_all(arrs, name):
    n = len(arrs)

    def body(*refs):
        exchange = _Exchange(refs[:n], refs[n:2 * n], *refs[2 * n:])
        exchange.start()
        exchange.finish()

    any_spec = pl.BlockSpec(memory_space=pl.ANY)
    outs = _call(
        body,
        name=name,
        out_shape=[_sds(a.shape, a.dtype) for a in arrs],
        in_specs=[any_spec] * n,
        out_specs=[any_spec] * n,
        scratch_shapes=_comm_sems(n),
    )(*arrs)
    return list(outs)


NN = (((1,), (0,)), ((), ()))
NT = (((1,), (1,)), ((), ()))
TN = (((0,), (0,)), ((), ()))


def _all_of(conds):
    out = conds[0]
    for cond in conds[1:]:
        out = out & cond
    return out


def _rider_call(body, operands, riders, *, name, out_shape, grid, in_specs, out_specs, scratch_shapes, sem):
    if not riders or not riders[1]:
        res = _call(body, name=name, out_shape=out_shape, grid=grid, in_specs=in_specs, out_specs=out_specs,
                    scratch_shapes=list(scratch_shapes), compiler_params=_params(*sem))(*operands)
        return list(res), []
    cls, arrs = riders
    nr, ni, no, ns = len(arrs), len(in_specs), len(out_specs), len(scratch_shapes)
    forward_at = (3 * grid[0]) // 4 if grid[0] >= 4 else None

    def wrapped(*refs):
        ids = [pl.program_id(ax) for ax in range(len(grid))]
        comm = cls(refs[ni:ni + nr], refs[ni + nr + no:ni + 2 * nr + no], *refs[ni + 2 * nr + no + ns:])
        pl.when(_all_of([i == 0 for i in ids]))(comm.start)
        if cls is _Gather and forward_at is not None:
            pl.when(_all_of([ids[0] == forward_at] + [i == 0 for i in ids[1:]]))(comm.forward)
        body(*refs[:ni], *refs[ni + nr:ni + nr + no], *refs[ni + 2 * nr + no:ni + 2 * nr + no + ns])

        def end():
            if cls is _Gather and forward_at is None:
                comm.forward()
            comm.finish()

        pl.when(_all_of([i == g - 1 for i, g in zip(ids, grid)]))(end)

    any_spec = pl.BlockSpec(memory_space=pl.ANY)
    res = _call(wrapped, name=name, out_shape=list(out_shape) + cls.out_shapes(arrs), grid=grid,
                in_specs=list(in_specs) + [any_spec] * nr, out_specs=list(out_specs) + [any_spec] * nr,
                scratch_shapes=list(scratch_shapes) + _comm_sems(nr),
                compiler_params=_params(*(["arbitrary"] * len(grid))))(*operands, *arrs)
    return list(res[:no]), list(res[no:])


def _gemm(a, b, out_shape, out_dtype, grid, a_spec, b_spec, o_spec, acc_shape, dims, name, riders=None):
    nk = grid[2]

    if nk == 1:
        def body(a_ref, b_ref, o_ref):
            r = lax.dot_general(a_ref[...].astype(BF16), b_ref[...].astype(BF16), dims,
                                preferred_element_type=F32)
            o_ref[...] = r.astype(o_ref.dtype)
        scratch = []
    else:
        def body(a_ref, b_ref, o_ref, acc_ref):
            k = pl.program_id(2)

            @pl.when(k == 0)
            def _():
                acc_ref[...] = jnp.zeros_like(acc_ref)

            acc_ref[...] += lax.dot_general(a_ref[...].astype(BF16), b_ref[...].astype(BF16), dims,
                                            preferred_element_type=F32)

            @pl.when(k == nk - 1)
            def _():
                o_ref[...] = acc_ref[...].astype(o_ref.dtype)
        scratch = [pltpu.VMEM(acc_shape, F32)]

    res, ridden = _rider_call(
        body, (a, b), riders, name=name, out_shape=[_sds(out_shape, out_dtype)], grid=grid,
        in_specs=[a_spec, b_spec], out_specs=[o_spec], scratch_shapes=scratch,
        sem=("parallel", "parallel", "arbitrary"))
    return (res[0], ridden) if riders else res[0]


def _div_tile(n, want):
    if n <= want:
        return n
    t = want - want % 128
    while n % t:
        t -= 128
    return t


def mm(a, b, name, out_dtype=F32, tm=512, tn=1024, tk=1024):
    m, k = a.shape
    n = b.shape[1]
    tm, tn, tk = _div_tile(m, tm), _div_tile(n, tn), _div_tile(k, tk)
    return _gemm(a, b, (m, n), out_dtype, (m // tm, n // tn, k // tk),
                 pl.BlockSpec((tm, tk), lambda i, j, kk: (i, kk)),
                 pl.BlockSpec((tk, tn), lambda i, j, kk: (kk, j)),
                 pl.BlockSpec((tm, tn), lambda i, j, kk: (i, j)),
                 (tm, tn), NN, name)


def mm_nt(a, b, name, out_dtype=F32, tm=512, tn=1024, tk=1024):
    m, n = a.shape
    k = b.shape[0]
    tm, tk_out, tred = _div_tile(m, tm), _div_tile(k, tn), _div_tile(n, tk)
    return _gemm(a, b, (m, k), out_dtype, (m // tm, k // tk_out, n // tred),
                 pl.BlockSpec((tm, tred), lambda i, j, kk: (i, kk)),
                 pl.BlockSpec((tk_out, tred), lambda i, j, kk: (j, kk)),
                 pl.BlockSpec((tm, tk_out), lambda i, j, kk: (i, j)),
                 (tm, tk_out), NT, name)


def mm_tn(a, b, name, out_dtype=F32, tm=512, tn=1024, tk=1024):
    m, k = a.shape
    n = b.shape[1]
    tk_out, tn, tred = _div_tile(k, tk), _div_tile(n, tn), _div_tile(m, tm)
    return _gemm(a, b, (k, n), out_dtype, (k // tk_out, n // tn, m // tred),
                 pl.BlockSpec((tred, tk_out), lambda i, j, kk: (kk, i)),
                 pl.BlockSpec((tred, tn), lambda i, j, kk: (kk, j)),
                 pl.BlockSpec((tk_out, tn), lambda i, j, kk: (i, j)),
                 (tk_out, tn), TN, name)


def mm_cs(a, wg, name, act_major=False, out_dtype=F32, tm=1024, riders=None):
    m, k = a.shape
    s, _, n = wg.shape
    tm = _div_tile(m, tm)
    if act_major:
        out_shape, o_spec = (s, m, n), pl.BlockSpec((None, tm, n), lambda i, j, kk: (j, i, 0))
    else:
        out_shape, o_spec = (m, s * n), pl.BlockSpec((tm, n), lambda i, j, kk: (i, j))
    return _gemm(a, wg, out_shape, out_dtype, (m // tm, s, 1),
                 pl.BlockSpec((tm, k), lambda i, j, kk: (i, 0)),
                 pl.BlockSpec((None, k, n), lambda i, j, kk: (j, 0, 0)),
                 o_spec, (tm, n), NN, name, riders)


def mm_cs_dx(da, wg, name, act_major=False, out_dtype=F32, tm=1024, riders=None):
    s, k, n = wg.shape
    m = da.shape[1] if act_major else da.shape[0]
    tm = _div_tile(m, tm)
    if act_major:
        a_spec = pl.BlockSpec((None, tm, n), lambda i, j, kk: (kk, i, 0))
    else:
        a_spec = pl.BlockSpec((tm, n), lambda i, j, kk: (i, kk))
    return _gemm(da, wg, (m, k), out_dtype, (m // tm, 1, s), a_spec,
                 pl.BlockSpec((None, k, n), lambda i, j, kk: (kk, 0, 0)),
                 pl.BlockSpec((tm, k), lambda i, j, kk: (i, 0)),
                 (tm, k), NT, name, riders)


def mm_cs_dw(a, da, name, act_major=False, out_dtype=F32, tm=1024, riders=None):
    m, k = a.shape
    if act_major:
        s, _, n = da.shape
    else:
        s, n = N_DEV, da.shape[1] // N_DEV
    tm = _div_tile(m, tm)
    if act_major:
        b_spec = pl.BlockSpec((None, tm, n), lambda i, j, kk: (i, kk, 0))
    else:
        b_spec = pl.BlockSpec((tm, n), lambda i, j, kk: (kk, i))
    return _gemm(a, da, (s, k, n), out_dtype, (s, 1, m // tm),
                 pl.BlockSpec((tm, k), lambda i, j, kk: (kk, 0)), b_spec,
                 pl.BlockSpec((None, k, n), lambda i, j, kk: (i, 0, 0)),
                 (k, n), TN, name, riders)


def mm_rs(s3, w3, name, out_dtype=F32, tm=1024):
    s, m, n = s3.shape
    nn = w3.shape[2]
    tm = _div_tile(m, tm)
    return _gemm(s3, w3, (m, nn), out_dtype, (m // tm, 1, s),
                 pl.BlockSpec((None, tm, n), lambda i, j, kk: (kk, i, 0)),
                 pl.BlockSpec((None, n, nn), lambda i, j, kk: (kk, 0, 0)),
                 pl.BlockSpec((tm, nn), lambda i, j, kk: (i, 0)),
                 (tm, nn), NN, name)


def mm_rs_dx(dy, w3, name, out_dtype=F32, tm=1024):
    m, nn = dy.shape
    s, n, _ = w3.shape
    tm = _div_tile(m, tm)
    return _gemm(dy, w3, (s, m, n), out_dtype, (m // tm, s, 1),
                 pl.BlockSpec((tm, nn), lambda i, j, kk: (i, 0)),
                 pl.BlockSpec((None, n, nn), lambda i, j, kk: (j, 0, 0)),
                 pl.BlockSpec((None, tm, n), lambda i, j, kk: (j, i, 0)),
                 (tm, n), NT, name)


def mm_rs_dw(s3, dy, name, out_dtype=F32, tm=1024):
    s, m, n = s3.shape
    nn = dy.shape[1]
    tm = _div_tile(m, tm)
    return _gemm(s3, dy, (s, n, nn), out_dtype, (s, 1, m // tm),
                 pl.BlockSpec((None, tm, n), lambda i, j, kk: (i, kk, 0)),
                 pl.BlockSpec((tm, nn), lambda i, j, kk: (kk, 0)),
                 pl.BlockSpec((None, n, nn), lambda i, j, kk: (i, 0, 0)),
                 (n, nn), TN, name)


def _colsum8(v):
    tr, d = v.shape
    return v.reshape(tr // 8, 8, d).sum(axis=0)


def _rstd(v):
    return lax.rsqrt(jnp.mean(v * v, axis=-1, keepdims=True) + EPS)


def _vec_spec(d):
    return pl.BlockSpec((1, d), lambda i: (0, 0))


def _acc_spec(d):
    return pl.BlockSpec((8, d), lambda i: (0, 0))


def _pre_rows(xv, g, shift, scale):
    return ((xv * _rstd(xv)) * g) * (1 + scale) + shift


def _post_rows(xv, yv, g, gate):
    return xv + gate * ((yv * _rstd(yv)) * g)


def _post_bwd_rows(dxv, yv, g, gate):
    r = _rstd(yv)
    yhat = yv * r
    dgate = _colsum8(dxv * (yhat * g))
    dyn = gate * dxv
    dg = _colsum8(dyn * yhat)
    dyhat = dyn * g
    dy = r * (dyhat - yhat * jnp.mean(dyhat * yhat, axis=-1, keepdims=True))
    return dy, dgate, dg


def _pre_bwd_rows(dhv, xv, g, scale, dxn):
    r = _rstd(xv)
    xhat = xv * r
    dshift = _colsum8(dhv)
    dscale = _colsum8(dhv * (xhat * g))
    dmod = dhv * (1 + scale)
    dg = _colsum8(dmod * xhat)
    dxhat = dmod * g
    dx = r * (dxhat - xhat * jnp.mean(dxhat * xhat, axis=-1, keepdims=True)) + dxn
    return dx, dshift, dscale, dg


def _row_call(body, name, t, d, rows_in, vecs_in, rows_out, n_acc):
    tr = _row_tile(t)
    nri, nvi, nro = len(rows_in), len(vecs_in), len(rows_out)

    def wrapped(*refs):
        accs = refs[nri + nvi + nro:]
        if n_acc:
            @pl.when(pl.program_id(0) == 0)
            def _():
                for acc in accs:
                    acc[...] = jnp.zeros_like(acc)
        body(*refs)

    row = pl.BlockSpec((tr, d), lambda i: (i, 0))
    return _call(wrapped, name=name,
                 out_shape=[_sds((t, d), dt) for dt in rows_out] + [_sds((8, d), F32)] * n_acc,
                 grid=(t // tr,), in_specs=[row] * nri + [_vec_spec(d)] * nvi,
                 out_specs=[row] * nro + [_acc_spec(d)] * n_acc,
                 compiler_params=_params("arbitrary" if n_acc else "parallel"))(*rows_in, *vecs_in)


def pre_fwd(x, g, shift, scale, name):
    def body(x_ref, g_ref, sh_ref, sc_ref, h_ref):
        h_ref[...] = _pre_rows(x_ref[...], g_ref[...], sh_ref[...], sc_ref[...]).astype(BF16)

    return _row_call(body, name, *x.shape, [x], [g, shift, scale], [BF16], 0)[0]


def post_pre_fwd(x, y, g_post, gate, g_pre, shift, scale, name):
    def body(x_ref, y_ref, gp_ref, gt_ref, g_ref, sh_ref, sc_ref, xn_ref, h_ref):
        xn = _post_rows(x_ref[...], y_ref[...], gp_ref[...], gt_ref[...])
        xn_ref[...] = xn
        h_ref[...] = _pre_rows(xn, g_ref[...], sh_ref[...], sc_ref[...]).astype(BF16)

    return _row_call(body, name, *x.shape, [x, y], [g_post, gate, g_pre, shift, scale], [F32, BF16], 0)


def post_loss_bwd(x, y, g, gate, target, name):
    d = x.shape[1]

    def body(x_ref, y_ref, t_ref, g_ref, gt_ref, dx_ref, dy_ref, l_ref, dgate_ref, dg_ref):
        yv, gv, gate_v = y_ref[...], g_ref[...], gt_ref[...]
        err = _post_rows(x_ref[...], yv, gv, gate_v) - t_ref[...]
        l_ref[...] += _colsum8(err * err)
        dxv = err * (1.0 / d)
        dx_ref[...] = dxv
        dy, dgate, dg = _post_bwd_rows(dxv, yv, gv, gate_v)
        dy_ref[...] = dy.astype(BF16)
        dgate_ref[...] += dgate
        dg_ref[...] += dg

    return _row_call(body, name, *x.shape, [x, y, target], [g, gate], [F32, BF16], 3)


def pre_post_bwd(dh, x, g_pre, scale, dxn, y, g_post, gate, name):
    def body(dh_ref, x_ref, dxn_ref, y_ref, g_ref, sc_ref, gp_ref, gt_ref,
             dx_ref, dy_ref, dsh_ref, dsc_ref, dg_ref, dgate_ref, dgp_ref):
        dx, dsh, dsc, dg = _pre_bwd_rows(dh_ref[...].astype(F32), x_ref[...], g_ref[...], sc_ref[...], dxn_ref[...])
        dx_ref[...] = dx
        dsh_ref[...] += dsh
        dsc_ref[...] += dsc
        dg_ref[...] += dg
        dy, dgate, dgp = _post_bwd_rows(dx, y_ref[...], gp_ref[...], gt_ref[...])
        dy_ref[...] = dy.astype(BF16)
        dgate_ref[...] += dgate
        dgp_ref[...] += dgp

    return _row_call(body, name, *x.shape, [dh, x, dxn, y], [g_pre, scale, g_post, gate], [F32, BF16], 5)


def pre_bwd(dh, x, g, scale, dxn, name):
    def body(dh_ref, x_ref, dxn_ref, g_ref, sc_ref, dx_ref, dsh_ref, dsc_ref, dg_ref):
        dx, dsh, dsc, dg = _pre_bwd_rows(dh_ref[...].astype(F32), x_ref[...], g_ref[...], sc_ref[...], dxn_ref[...])
        dx_ref[...] = dx
        dsh_ref[...] += dsh
        dsc_ref[...] += dsc
        dg_ref[...] += dg

    return _row_call(body, name, *x.shape, [dh, x, dxn], [g, scale], [F32], 3)


def _sigmoid(x):
    return 1.0 / (1.0 + jnp.exp(-x))


def ffn_in_swiglu(h, wg, name, tm=1024, riders=None):
    m, k = h.shape
    s, _, n = wg.shape
    half = s // 2
    tm = _div_tile(m, tm)

    def body(h_ref, wg_ref, wu_ref, g_ref, u_ref, s_ref):
        hv = h_ref[...]
        g = jnp.dot(hv, wg_ref[...], preferred_element_type=F32)
        u = jnp.dot(hv, wu_ref[...], preferred_element_type=F32)
        g_ref[...] = g
        u_ref[...] = u
        s_ref[...] = ((g * _sigmoid(g)) * u).astype(BF16)

    act = pl.BlockSpec((None, tm, n), lambda i, j: (j, i, 0))
    res, ridden = _rider_call(
        body, (h, wg, wg), riders, name=name,
        out_shape=[_sds((half, m, n), F32), _sds((half, m, n), F32), _sds((half, m, n), BF16)],
        grid=(m // tm, half),
        in_specs=[pl.BlockSpec((tm, k), lambda i, j: (i, 0)),
                  pl.BlockSpec((None, k, n), lambda i, j: (j, 0, 0)),
                  pl.BlockSpec((None, k, n), lambda i, j: (j + half, 0, 0))],
        out_specs=[act, act, act], scratch_shapes=[], sem=("parallel", "parallel"))
    return res[0], res[1], res[2], ridden


def ffn_out_dx_swiglu(dy, w4, gate, up, name, tm=1024, riders=None):
    m, nn = dy.shape
    half, n, _ = w4.shape
    tm = _div_tile(m, tm)

    def body(dy_ref, w_ref, g_ref, u_ref, o_ref):
        ds = lax.dot_general(dy_ref[...], w_ref[...], NT, preferred_element_type=F32)
        g, u = g_ref[...], u_ref[...]
        sig = _sigmoid(g)
        o_ref[0] = (ds * u * (sig * (1 + g * (1 - sig)))).astype(BF16)
        o_ref[1] = (ds * (g * sig)).astype(BF16)

    act = pl.BlockSpec((None, tm, n), lambda i, j: (j, i, 0))
    res, ridden = _rider_call(
        body, (dy, w4, gate, up), riders, name=name, out_shape=[_sds((2, half, m, n), BF16)],
        grid=(m // tm, half),
        in_specs=[pl.BlockSpec((tm, nn), lambda i, j: (i, 0)),
                  pl.BlockSpec((None, n, nn), lambda i, j: (j, 0, 0)), act, act],
        out_specs=[pl.BlockSpec((2, None, tm, n), lambda i, j: (0, j, i, 0))],
        scratch_shapes=[], sem=("parallel", "parallel"))
    return res[0].reshape(2 * half, m, n), ridden


def _split_hi_lo(v):
    hi = v.astype(BF16)
    lo = (v - hi.astype(F32)).astype(BF16)
    return hi, lo


SB_G = 2
SB_EXP_ZERO = 104.0
SB_UNSEEN = 3e38
SB_GW = SB_G * BLK


def _sb_specs(t):
    nq = t // BLK
    npair = N_HEADS // 2
    q_spec = pl.BlockSpec((BLK, BLK), lambda p, qb: (qb, p))
    k_spec = pl.BlockSpec((t, BLK), lambda p, qb: (0, npair + p))
    v_spec = pl.BlockSpec((t, BLK), lambda p, qb: (0, 2 * npair + p))
    kt_spec = pl.BlockSpec((t // SB_GW, BLK, SB_GW), lambda p, qb: (0, npair + p, 0))
    vt_spec = pl.BlockSpec((t // SB_GW, BLK, SB_GW), lambda p, qb: (0, 2 * npair + p, 0))
    c_spec = pl.BlockSpec((None, nq, 8, 2 * BLK), lambda p, qb: (p, 0, 0, qb))
    return nq, npair, q_spec, k_spec, v_spec, kt_spec, vt_spec, c_spec


def _sb_consts():
    row = lax.broadcasted_iota(jnp.int32, (BLK, BLK), 0)
    col = lax.broadcasted_iota(jnp.int32, (BLK, BLK), 1)
    lane0 = (col < HEAD_DIM).astype(F32)
    sub0 = (row < HEAD_DIM).astype(F32)
    return row, col, lane0, sub0


def _sb_valid(ks, qb):
    row = lax.broadcasted_iota(jnp.int32, (SB_GW, 2 * BLK), 0)
    col = lax.broadcasted_iota(jnp.int32, (SB_GW, 2 * BLK), 1)
    return (ks + row) < (qb * BLK + (col & (BLK - 1)))


def _blocks_on_lanes(v4):
    return jnp.concatenate([v4[b * BLK:(b + 1) * BLK] for b in range(SB_G)], axis=1)


def _tri2_dot(tri2, v):
    hi, lo = _split_hi_lo(v)
    return jnp.dot(tri2, jnp.concatenate([hi, lo], axis=0), preferred_element_type=F32)


def _sb_pair_loop(first, count, step, group, skip, carry):
    def pair(it, cy):
        g1 = first + 2 * step * it
        cy = group(g1, 0, 1, cy)
        return lax.cond(2 * it + 1 < count, lambda c: group(g1 + step, 1, 0, c), skip, cy)
    return lax.fori_loop(0, (count + 1) // 2, pair, carry)


def sb_fwd(qkv, qkv_t, riders, name):
    t = qkv.shape[0]
    assert t % SB_GW == 0
    nq, npair, q_spec, k_spec, _, _, vt_spec, c_spec = _sb_specs(t)
    nr = len(riders)

    def body(*refs):
        q_ref, k_ref, vt_ref = refs[:3]
        o_ref, c_ref = refs[3 + nr:5 + nr]
        oacc, zbuf0, zbuf1, kmax = refs[5 + 2 * nr:9 + 2 * nr]
        pp = pl.program_id(0)
        qb = pl.program_id(1)
        if nr:
            gather = _Gather(refs[3:3 + nr], refs[5 + nr:5 + 2 * nr], *refs[9 + 2 * nr:])
            pl.when((pp == 0) & (qb == 0))(gather.start)
            pl.when((pp == npair - 2) & (qb == 0))(gather.forward)
        _sb_fwd_step(q_ref, k_ref, vt_ref, o_ref, c_ref, oacc, zbuf0, zbuf1, kmax, qb)
        if nr:
            pl.when((pp == npair - 1) & (qb == nq - 1))(gather.finish)

    any_spec = pl.BlockSpec(memory_space=pl.ANY)
    outs = _call(
        body, name=name,
        out_shape=[_sds((t, D_MODEL), BF16), _sds((npair, nq, 8, 2 * t), F32)] + _Gather.out_shapes(riders),
        grid=(npair, nq), in_specs=[q_spec, k_spec, vt_spec] + [any_spec] * nr,
        out_specs=[pl.BlockSpec((BLK, BLK), lambda p, qb: (qb, p)), c_spec] + [any_spec] * nr,
        scratch_shapes=[pltpu.VMEM((BLK, 2 * BLK), F32), pltpu.VMEM((SB_GW, 2 * BLK), F32),
                        pltpu.VMEM((SB_GW, 2 * BLK), F32), pltpu.VMEM((8, BLK), F32)] + _comm_sems(nr),
        compiler_params=_params("arbitrary", "arbitrary"),
    )(qkv, qkv, qkv_t, *riders)
    return outs[0], outs[1], list(outs[2:])


def _sb_fwd_step(q_ref, k_ref, vt_ref, o_ref, c_ref, oacc, zbuf0, zbuf1, kmax, qb):
    row, col, lane0, sub0 = _sb_consts()
    tri = (col >= row).astype(BF16)
    tri2 = jnp.concatenate([tri, tri], axis=1)
    q2 = _two_heads(q_ref[...], lane0, QK_SCALE)
    zbufs = (zbuf0, zbuf1)
    c_ref[...] = jnp.full(c_ref.shape, SB_UNSEEN, F32)
    oacc[...] = jnp.zeros_like(oacc)

    @pl.when(qb == 0)
    def _():
        ksq = jnp.square(k_ref[...].astype(F32))
        head0 = (lax.broadcasted_iota(jnp.int32, (1, BLK), 1) < HEAD_DIM).astype(F32)
        norms = jnp.maximum(jnp.sum(ksq * head0, axis=1, keepdims=True),
                            jnp.sum(ksq * (1.0 - head0), axis=1, keepdims=True))
        kmax[...] = jnp.broadcast_to(jnp.max(norms, axis=0, keepdims=True), kmax.shape)

    qsq = jnp.square(q2.astype(F32)).astype(BF16)
    qn2 = jnp.max(lax.dot_general(jnp.ones((8, BLK), BF16), qsq, NT, preferred_element_type=F32),
                  axis=0, keepdims=True)
    kk = kmax[0:1, :]
    zbound = jnp.sqrt(qn2 * jnp.concatenate([kk, kk], axis=1)) * 1.02

    def matters(cr):
        return (jnp.min(cr - zbound) <= SB_EXP_ZERO).astype(jnp.int32)

    def scores(g):
        ks = pl.multiple_of(g * SB_GW, SB_GW)
        return lax.dot_general(k_ref[pl.ds(ks, SB_GW), :], q2, NT, preferred_element_type=F32)

    def group(g, cur, nxt, cr, masked=False):
        z = zbufs[cur][...]
        zbufs[nxt][...] = scores(jnp.maximum(g - 1, 0))
        e = jnp.exp(-jnp.abs(z))
        sp = jnp.maximum(z, 0.0) + jnp.log(1.0 + e)
        if masked:
            valid = _sb_valid(g * SB_GW, qb)
            sp = jnp.where(valid, sp, 0.0)
        loc = _tri2_dot(tri2, _blocks_on_lanes(sp))
        parts = [None] * SB_G
        for b in reversed(range(SB_G)):
            rows = slice(b * BLK, (b + 1) * BLK)
            c_ref[g * SB_G + b] = jnp.broadcast_to(cr, (8, 2 * BLK))
            a = jnp.exp(z[rows] - (loc[:, 2 * b * BLK:2 * (b + 1) * BLK] + cr))
            if masked:
                a = jnp.where(valid[rows], a, 0.0)
            parts[b] = a.astype(BF16)
            cr = cr + jnp.sum(sp[rows], axis=0, keepdims=True)
        oacc[...] += jnp.dot(vt_ref[g], jnp.concatenate(parts, axis=0), preferred_element_type=F32)
        return cr

    last = qb // SB_G
    zbuf1[...] = scores(last)
    cr = group(last, 1, 0, jnp.zeros((1, 2 * BLK), F32), masked=True)

    def pair(state):
        g, cr, _ = state
        cr = group(g, 0, 1, cr)
        more = (g >= 1).astype(jnp.int32) * matters(cr)
        cr = lax.cond(more > 0, lambda c: group(g - 1, 1, 0, c), lambda c: c, cr)
        return jnp.where(more > 0, g - 2, -1), cr, matters(cr)

    lax.while_loop(lambda st: (st[0] >= 0) & (st[2] > 0), pair, (last - 1, cr, matters(cr)))
    o_t = oacc[:, :BLK] * sub0 + oacc[:, BLK:] * (1.0 - sub0)
    o_ref[...] = o_t.T.astype(BF16)


def sb_bwd(qkv, qkv_t, do, cmass, riders, name):
    t = qkv.shape[0]
    nq, npair, q_spec, k_spec, v_spec, kt_spec, _, c_spec = _sb_specs(t)
    nr = len(riders)

    def body(*refs):
        pp = pl.program_id(0)
        qb = pl.program_id(1)
        if nr:
            exchange = _Exchange(refs[6:6 + nr], refs[9 + nr:9 + 2 * nr], *refs[14 + 2 * nr:])
            pl.when((pp == 0) & (qb == 0))(exchange.start)
        step(*refs[:6], *refs[6 + nr:9 + nr], *refs[9 + 2 * nr:14 + 2 * nr])
        if nr:
            pl.when((pp == npair - 1) & (qb == nq - 1))(exchange.finish)

    def step(q_ref, k_ref, kt_ref, v_ref, do_ref, c_ref, dq_ref, dk_ref, dv_ref, dqacc, dkacc, dvacc,
             zbuf0, zbuf1):
        qb = pl.program_id(1)

        @pl.when(qb == 0)
        def _():
            dkacc[...] = jnp.zeros_like(dkacc)
            dvacc[...] = jnp.zeros_like(dvacc)

        row, col, lane0, sub0 = _sb_consts()
        tri_suf = (col >= row).astype(BF16)
        tri_pre = (col <= row).astype(BF16)
        tri2_suf = jnp.concatenate([tri_suf, tri_suf], axis=1)
        tri2_pre = jnp.concatenate([tri_pre, tri_pre], axis=1)
        q2 = _two_heads(q_ref[...], lane0, QK_SCALE)
        do2 = _two_heads(do_ref[...], lane0, 1.0)
        zbufs = (zbuf0, zbuf1)
        dqacc[...] = jnp.zeros_like(dqacc)
        last = qb // SB_G

        def scores(g):
            ks = pl.multiple_of(g * SB_GW, SB_GW)
            return lax.dot_general(k_ref[pl.ds(ks, SB_GW), :], q2, NT, preferred_element_type=F32)

        def group(g, cur, nxt, gc, masked=False):
            ks = pl.multiple_of(g * SB_GW, SB_GW)
            z = zbufs[cur][...]
            zbufs[nxt][...] = scores(jnp.minimum(g + 1, last))
            e = jnp.exp(-jnp.abs(z))
            sig = 0.5 * jnp.tanh(0.5 * z) + 0.5
            sp = jnp.maximum(z, 0.0) + jnp.log(1.0 + e)
            if masked:
                valid = _sb_valid(ks, qb)
                sp = jnp.where(valid, sp, 0.0)
            loc = _tri2_dot(tri2_suf, _blocks_on_lanes(sp))
            parts = []
            for b in range(SB_G):
                rows = slice(b * BLK, (b + 1) * BLK)
                mass = loc[:, 2 * b * BLK:2 * (b + 1) * BLK] + c_ref[g * SB_G + b, 0:1, :]
                parts.append(jnp.exp(z[rows] - mass))
            a = jnp.concatenate(parts, axis=0)
            if masked:
                a = jnp.where(valid, a, 0.0)
            gr = lax.dot_general(v_ref[pl.ds(ks, SB_GW), :], do2, NT, preferred_element_type=F32) * a
            pre = _tri2_dot(tri2_pre, _blocks_on_lanes(gr))
            parts = []
            for b in range(SB_G):
                rows = slice(b * BLK, (b + 1) * BLK)
                parts.append(pre[:, 2 * b * BLK:2 * (b + 1) * BLK] + gc)
                gc = gc + jnp.sum(gr[rows], axis=0, keepdims=True)
            dz = gr - sig * jnp.concatenate(parts, axis=0)
            if masked:
                dz = jnp.where(valid, dz, 0.0)
            dz = dz.astype(BF16)
            dkacc[pl.ds(ks, SB_GW), :] += jnp.dot(dz, q2, preferred_element_type=F32)
            dqacc[...] += jnp.dot(kt_ref[g], dz, preferred_element_type=F32)
            dvacc[pl.ds(ks, SB_GW), :] += jnp.dot(a.astype(BF16), do2, preferred_element_type=F32)
            return gc

        def skip(gc):
            zbuf0[...] = zbuf1[...]
            return gc

        def unseen(g):
            return (jnp.max(c_ref[g * SB_G + SB_G - 1, 0:1, :]) > 0.5 * SB_UNSEEN).astype(jnp.int32)

        first, _ = lax.while_loop(lambda st: (st[0] < last) & (st[1] > 0),
                                  lambda st: (st[0] + 1, unseen(st[0] + 1)), (jnp.int32(0), unseen(0)))
        zbuf0[...] = scores(first)
        gc = _sb_pair_loop(first, last - first, 1, group, skip, jnp.zeros((1, 2 * BLK), F32))
        group(last, 0, 1, gc, masked=True)
        dq_t = (dqacc[:, :BLK] * sub0 + dqacc[:, BLK:] * (1.0 - sub0)) * QK_SCALE
        dq_ref[...] = dq_t.T.astype(BF16)

        @pl.when(qb == nq - 1)
        def _():
            dk_ref[...] = dkacc[...].astype(BF16)
            dv_ref[...] = dvacc[...].astype(BF16)

    col_spec = pl.BlockSpec((t, BLK), lambda p, qb: (0, p))
    blk_spec = pl.BlockSpec((BLK, BLK), lambda p, qb: (qb, p))
    any_spec = pl.BlockSpec(memory_space=pl.ANY)
    outs = _call(
        body, name=name,
        out_shape=[_sds((t, D_MODEL), BF16)] * 3 + [_sds(r.shape, r.dtype) for r in riders],
        grid=(npair, nq), in_specs=[q_spec, k_spec, kt_spec, v_spec, blk_spec, c_spec] + [any_spec] * nr,
        out_specs=[blk_spec, col_spec, col_spec] + [any_spec] * nr,
        scratch_shapes=[pltpu.VMEM((BLK, 2 * BLK), F32), pltpu.VMEM((t, BLK), F32), pltpu.VMEM((t, BLK), F32),
                        pltpu.VMEM((SB_GW, 2 * BLK), F32), pltpu.VMEM((SB_GW, 2 * BLK), F32)] + _comm_sems(nr),
        compiler_params=_params("arbitrary", "arbitrary"),
    )(qkv, qkv, qkv_t, qkv, do, cmass, *riders)
    return outs[0], outs[1], outs[2], list(outs[3:])


BAND_QPS = 4


def _band_static_mask(jj):
    row = lax.broadcasted_iota(jnp.int32, (2 * BLK, BLK), 0)
    col = lax.broadcasted_iota(jnp.int32, (2 * BLK, BLK), 1)
    qc = (row & (BLK - 1)) // 64
    kc = 2 * jj + col // 64
    return (kc >= qc) & (kc <= qc + 8)


def _band_key_start(qb, jj):
    kb = qb - (BAND_BLOCKS - 1) + jj
    return kb, pl.multiple_of(jnp.maximum(kb, 0) * BLK, BLK)


def _band_probs(q2, k_ref, bias, qb):
    blocks = []
    for jj in range(BAND_BLOCKS):
        kb, ks = _band_key_start(qb, jj)
        s = lax.dot_general(q2, k_ref[pl.ds(ks, BLK), :], NT, preferred_element_type=F32)
        s = s + bias[:, jj * BLK:(jj + 1) * BLK]
        ok = (kb >= 0) if 0 < jj < BAND_BLOCKS - 1 else _band_static_mask(jj) & (kb >= 0)
        blocks.append(jnp.where(ok, s, NEG))
    s = jnp.concatenate(blocks, axis=1)
    m = jnp.max(s, axis=-1, keepdims=True)
    e = jnp.exp(s - m)
    return e / jnp.sum(e, axis=-1, keepdims=True)


def _band_specs(t):
    npair = N_HEADS // 2
    rows = BAND_QPS * BLK
    q_spec = pl.BlockSpec((rows, BLK), lambda p, i: (i, p))
    k_spec = pl.BlockSpec((t, BLK), lambda p, i: (0, npair + p))
    v_spec = pl.BlockSpec((t, BLK), lambda p, i: (0, 2 * npair + p))
    b_spec = pl.BlockSpec((2, BLK, BAND_W), lambda p, i: (p, 0, 0))
    return npair, t // rows, q_spec, k_spec, v_spec, b_spec


def _two_heads(xv, lane0, scale):
    xf = xv.astype(F32)
    if scale != 1.0:
        xf = xf * scale
    return jnp.concatenate([xf * lane0, xf * (1.0 - lane0)], axis=0).astype(BF16)


def _one_of_two_heads(r, lane0):
    return r[:BLK] * lane0 + r[BLK:] * (1.0 - lane0)


def band_fwd(qkv, bias, name):
    t = qkv.shape[0]
    assert t % (BAND_QPS * BLK) == 0
    npair, nsteps, q_spec, k_spec, v_spec, b_spec = _band_specs(t)

    def body(q_ref, k_ref, v_ref, b_ref, o_ref):
        step = pl.program_id(1)
        _, _, lane0, _ = _sb_consts()
        bias2 = b_ref[...].reshape(2 * BLK, BAND_W)
        for u in range(BAND_QPS):
            qb = step * BAND_QPS + u
            rows = slice(u * BLK, (u + 1) * BLK)
            q2 = _two_heads(q_ref[rows, :], lane0, QK_SCALE)
            p = _band_probs(q2, k_ref, bias2, qb)
            acc = jnp.zeros((2 * BLK, BLK), F32)
            for jj in range(BAND_BLOCKS):
                _, ks = _band_key_start(qb, jj)
                acc += jnp.dot(p[:, jj * BLK:(jj + 1) * BLK].astype(BF16), v_ref[pl.ds(ks, BLK), :],
                               preferred_element_type=F32)
            o_ref[rows, :] = _one_of_two_heads(acc, lane0).astype(BF16)

    return _call(
        body, name=name, out_shape=_sds((t, D_MODEL), BF16), grid=(npair, nsteps),
        in_specs=[q_spec, k_spec, v_spec, b_spec],
        out_specs=pl.BlockSpec((BAND_QPS * BLK, BLK), lambda p, i: (i, p)),
        compiler_params=_params("parallel", "parallel"),
    )(qkv, qkv, qkv, bias)


def band_bwd(qkv, do, bias, name, riders=None):
    t = qkv.shape[0]
    npair, nsteps, q_spec, k_spec, v_spec, b_spec = _band_specs(t)

    def body(q_ref, k_ref, v_ref, do_ref, b_ref, dq_ref, dk_ref, dv_ref, db_ref, dkacc, dvacc):
        step = pl.program_id(1)

        @pl.when(step == 0)
        def _():
            dkacc[...] = jnp.zeros_like(dkacc)
            dvacc[...] = jnp.zeros_like(dvacc)
            db_ref[...] = jnp.zeros_like(db_ref)

        _, _, lane0, _ = _sb_consts()
        bias2 = b_ref[...].reshape(2 * BLK, BAND_W)
        updates = []
        for u in range(BAND_QPS):
            qb = step * BAND_QPS + u
            rows = slice(u * BLK, (u + 1) * BLK)
            q2 = _two_heads(q_ref[rows, :], lane0, QK_SCALE)
            do2 = _two_heads(do_ref[rows, :], lane0, 1.0)
            p = _band_probs(q2, k_ref, bias2, qb)
            dp = jnp.concatenate(
                [lax.dot_general(do2, v_ref[pl.ds(_band_key_start(qb, jj)[1], BLK), :], NT,
                                 preferred_element_type=F32) for jj in range(BAND_BLOCKS)], axis=1)
            ds = p * (dp - jnp.sum(p * dp, axis=-1, keepdims=True))
            db_ref[...] += ds.reshape(2, BLK, BAND_W)
            dqa = jnp.zeros((2 * BLK, BLK), F32)
            for jj in range(BAND_BLOCKS):
                _, ks = _band_key_start(qb, jj)
                dsb = ds[:, jj * BLK:(jj + 1) * BLK].astype(BF16)
                pb = p[:, jj * BLK:(jj + 1) * BLK].astype(BF16)
                dqa += jnp.dot(dsb, k_ref[pl.ds(ks, BLK), :], preferred_element_type=F32)
                updates.append((ks, lax.dot_general(dsb, q2, TN, preferred_element_type=F32),
                                lax.dot_general(pb, do2, TN, preferred_element_type=F32)))
            dq_ref[rows, :] = (_one_of_two_heads(dqa, lane0) * QK_SCALE).astype(BF16)
        for ks, dk_part, dv_part in updates:
            dkacc[pl.ds(ks, BLK), :] += dk_part
            dvacc[pl.ds(ks, BLK), :] += dv_part

        @pl.when(step == nsteps - 1)
        def _():
            dk_ref[...] = dkacc[...].astype(BF16)
            dv_ref[...] = dvacc[...].astype(BF16)

    col_spec = pl.BlockSpec((t, BLK), lambda p, i: (0, p))
    blk_spec = pl.BlockSpec((BAND_QPS * BLK, BLK), lambda p, i: (i, p))
    res, ridden = _rider_call(
        body, (qkv, qkv, qkv, do, bias), riders, name=name,
        out_shape=[_sds((t, D_MODEL), BF16)] * 3 + [_sds((N_HEADS, BLK, BAND_W), F32)],
        grid=(npair, nsteps), in_specs=[q_spec, k_spec, v_spec, blk_spec, b_spec],
        out_specs=[blk_spec, col_spec, col_spec, b_spec],
        scratch_shapes=[pltpu.VMEM((t, BLK), F32), pltpu.VMEM((t, BLK), F32)],
        sem=("parallel", "arbitrary"))
    return res[0], res[1], res[2], res[3], ridden


def band_bias_window(rel_bias):
    far = BAND_W + BLK - 1 - 2 * REL_CLIP
    width = BAND_W + BLK
    ext = jnp.concatenate(
        [jnp.broadcast_to(rel_bias[:, 2 * REL_CLIP:], (N_HEADS, far)), rel_bias[:, 2 * REL_CLIP:0:-1],
         jnp.zeros((N_HEADS, 2), F32)], axis=1)
    tiled = jnp.broadcast_to(ext[:, None, :], (N_HEADS, BLK, width + 1)).reshape(N_HEADS, BLK * (width + 1))
    return tiled[:, BLK - 1:BLK - 1 + BLK * width].reshape(N_HEADS, BLK, width)[:, :, :BAND_W]


def band_bias_window_grad(dwin):
    width = BAND_W + BLK
    far = BAND_W + BLK - 1 - 2 * REL_CLIP
    flat = jnp.pad(dwin, ((0, 0), (0, 0), (0, BLK))).reshape(N_HEADS, BLK * width)
    skew = jnp.pad(flat, ((0, 0), (BLK - 1, 1))).reshape(N_HEADS, BLK, width + 1)
    dext = jnp.sum(skew, axis=1)[:, :width - 1]
    return jnp.concatenate(
        [jnp.zeros((N_HEADS, 1), F32), dext[:, :far - 1:-1][:, :2 * REL_CLIP - 1],
         dext[:, far:far + 1] + jnp.sum(dext[:, :far], axis=1, keepdims=True)], axis=1)


SG_GROUPS = 8


def _gelu_parts(x):
    inner = GELU_C0 * (x + GELU_C1 * (x * x * x))
    th = jnp.tanh(inner)
    return th, 0.5 * x * (1.0 + th)


def _sg_gate_mask():
    row = lax.broadcasted_iota(jnp.int32, (BLK, BLK), 0)
    col = lax.broadcasted_iota(jnp.int32, (BLK, BLK), 1)
    return (row // 64) >= (col // 64)


def _sg_forward_parts(a, lng):
    w = a.shape[1] // 2
    th, z = _gelu_parts(a)
    u, v = z[:, :w], z[:, w:]
    mu = jnp.mean(v, axis=-1, keepdims=True)
    xc = v - mu
    rstd = lax.rsqrt(jnp.mean(xc * xc, axis=-1, keepdims=True) + EPS)
    vhat = xc * rstd
    return th, u, vhat, rstd, vhat * lng


def sg_fwd(a, lng, ws, bias_t, name):
    t, w2 = a.shape
    w = w2 // 2
    gc = w // SG_GROUPS

    def body(a_ref, lng_ref, ws_ref, bt_ref, y_ref):
        _, u, _, _, vln = _sg_forward_parts(a_ref[...], lng_ref[...])
        mask = _sg_gate_mask()
        bt = bt_ref[...]
        lane = lax.broadcasted_iota(jnp.int32, (BLK, BLK), 1)
        for g in range(SG_GROUPS):
            wg = jnp.where(mask, ws_ref[g], 0.0).astype(BF16)
            sv = jnp.dot(wg, vln[:, g * gc:(g + 1) * gc].astype(BF16), preferred_element_type=F32)
            bg = jnp.sum(jnp.where(lane == g, bt, 0.0), axis=-1, keepdims=True)
            y_ref[:, g * gc:(g + 1) * gc] = (u[:, g * gc:(g + 1) * gc] * (sv + bg)).astype(BF16)

    return _call(
        body, name=name, out_shape=_sds((t, w), BF16), grid=(t // BLK,),
        in_specs=[pl.BlockSpec((BLK, w2), lambda i: (i, 0)), pl.BlockSpec((1, w), lambda i: (0, 0)),
                  pl.BlockSpec((SG_GROUPS, BLK, BLK), lambda i: (0, 0, 0)),
                  pl.BlockSpec((BLK, BLK), lambda i: (0, 0))],
        out_specs=pl.BlockSpec((BLK, w), lambda i: (i, 0)),
        compiler_params=_params("parallel"),
    )(a, lng, ws, bias_t)


def sg_bwd(a, dy, lng, ws, bias_t, name):
    t, w2 = a.shape
    w = w2 // 2
    gc = w // SG_GROUPS

    def body(a_ref, dy_ref, lng_ref, ws_ref, bt_ref, da_ref, dlng_ref, dws_ref, dbt_ref):
        @pl.when(pl.program_id(0) == 0)
        def _():
            dlng_ref[...] = jnp.zeros_like(dlng_ref)
            dws_ref[...] = jnp.zeros_like(dws_ref)
            dbt_ref[...] = jnp.zeros_like(dbt_ref)

        av, lng = a_ref[...], lng_ref[...]
        th, u, vhat, rstd, vln = _sg_forward_parts(av, lng)
        mask = _sg_gate_mask()
        bt = bt_ref[...]
        lane = lax.broadcasted_iota(jnp.int32, (BLK, BLK), 1)
        dyv = dy_ref[...]
        du_parts, dvln_parts = [], []
        dbt = jnp.zeros((BLK, BLK), F32)
        for g in range(SG_GROUPS):
            sl = slice(g * gc, (g + 1) * gc)
            wg = jnp.where(mask, ws_ref[g], 0.0).astype(BF16)
            vg = vln[:, sl].astype(BF16)
            sv = jnp.dot(wg, vg, preferred_element_type=F32)
            bg = jnp.sum(jnp.where(lane == g, bt, 0.0), axis=-1, keepdims=True)
            dyg = dyv[:, sl]
            du_parts.append(dyg * (sv + bg))
            dsv = dyg * u[:, sl]
            dbt += jnp.where(lane == g, jnp.sum(dsv, axis=-1, keepdims=True), 0.0)
            dsvb = dsv.astype(BF16)
            dws_ref[g] += jnp.where(mask, lax.dot_general(dsvb, vg, NT, preferred_element_type=F32), 0.0)
            dvln_parts.append(lax.dot_general(wg, dsvb, TN, preferred_element_type=F32))
        dbt_ref[...] += dbt
        du = jnp.concatenate(du_parts, axis=1)
        dvln = jnp.concatenate(dvln_parts, axis=1)
        dlng_ref[...] += _colsum8(dvln * vhat)
        dvhat = dvln * lng
        dv = rstd * (dvhat - jnp.mean(dvhat, axis=-1, keepdims=True)
                     - vhat * jnp.mean(dvhat * vhat, axis=-1, keepdims=True))
        dz = jnp.concatenate([du, dv], axis=1)
        dgelu = 0.5 * (1.0 + th) + (0.5 * av) * (1.0 - th * th) * (GELU_C0 * (1.0 + 3.0 * GELU_C1 * (av * av)))
        da_ref[...] = (dz * dgelu).astype(BF16)

    return _call(
        body, name=name,
        out_shape=[_sds((t, w2), BF16), _sds((8, w), F32), _sds((SG_GROUPS, BLK, BLK), F32), _sds((BLK, BLK), F32)],
        grid=(t // BLK,),
        in_specs=[pl.BlockSpec((BLK, w2), lambda i: (i, 0)), pl.BlockSpec((BLK, w), lambda i: (i, 0)),
                  pl.BlockSpec((1, w), lambda i: (0, 0)),
                  pl.BlockSpec((SG_GROUPS, BLK, BLK), lambda i: (0, 0, 0)),
                  pl.BlockSpec((BLK, BLK), lambda i: (0, 0))],
        out_specs=[pl.BlockSpec((BLK, w2), lambda i: (i, 0)), pl.BlockSpec((8, w), lambda i: (0, 0)),
                   pl.BlockSpec((SG_GROUPS, BLK, BLK), lambda i: (0, 0, 0)),
                   pl.BlockSpec((BLK, BLK), lambda i: (0, 0))],
        compiler_params=_params("arbitrary"),
    )(a, dy, lng, ws, bias_t)


def _shift_down(cat, n, tr):
    return pltpu.roll(cat, n, 0)[8:8 + tr]


def _shift_up(cat, n, tr):
    return pltpu.roll(cat, tr + 8 - n, 0)[0:tr]


def conv_fwd(p, cw, name):
    t, d3 = p.shape
    d = d3 // 3
    tr = min(256, t)
    hb = tr // 8

    def body(p_ref, ph_ref, cw_ref, o_ref):
        i = pl.program_id(0)
        pv = p_ref[...]
        y = pv[:, d:2 * d] * pv[:, 2 * d:]
        ph = ph_ref[...]
        yh = jnp.where(i > 0, ph[:, d:2 * d] * ph[:, 2 * d:], 0.0)
        cat = jnp.concatenate([yh, y], axis=0)
        yc = (cw_ref[0:1, :] * _shift_down(cat, 2, tr) + cw_ref[1:2, :] * _shift_down(cat, 1, tr)
              + cw_ref[2:3, :] * y)
        o_ref[...] = (pv[:, :d] * yc).astype(BF16)

    return _call(
        body, name=name, out_shape=_sds((t, d), BF16), grid=(t // tr,),
        in_specs=[pl.BlockSpec((tr, d3), lambda i: (i, 0)),
                  pl.BlockSpec((8, d3), lambda i: (jnp.maximum(i * hb - 1, 0), 0)),
                  pl.BlockSpec((8, d), lambda i: (0, 0))],
        out_specs=pl.BlockSpec((tr, d), lambda i: (i, 0)),
        compiler_params=_params("parallel"),
    )(p, p, cw)


def conv_bwd(p, dz, cw, name):
    t, d3 = p.shape
    d = d3 // 3
    tr = min(256, t)
    hb = tr // 8
    nt = t // tr

    def body(p_ref, ph_ref, pn_ref, dz_ref, dzn_ref, cw_ref, dp_ref, dcw_ref):
        i = pl.program_id(0)

        @pl.when(i == 0)
        def _():
            dcw_ref[...] = jnp.zeros_like(dcw_ref)

        pv = p_ref[...]
        gb, gcv, xt = pv[:, :d], pv[:, d:2 * d], pv[:, 2 * d:]
        y = gcv * xt
        ph = ph_ref[...]
        yh = jnp.where(i > 0, ph[:, d:2 * d] * ph[:, 2 * d:], 0.0)
        cat = jnp.concatenate([yh, y], axis=0)
        y2, y1 = _shift_down(cat, 2, tr), _shift_down(cat, 1, tr)
        w0, w1, w2 = cw_ref[0:1, :], cw_ref[1:2, :], cw_ref[2:3, :]
        yc = w0 * y2 + w1 * y1 + w2 * y
        dzv = dz_ref[...]
        dyc = dzv * gb
        dcw_ref[0] += _colsum8(dyc * y2)
        dcw_ref[1] += _colsum8(dyc * y1)
        dcw_ref[2] += _colsum8(dyc * y)
        dycn = jnp.where(i < nt - 1, dzn_ref[...] * pn_ref[...][:, :d], 0.0)
        catn = jnp.concatenate([dyc, dycn], axis=0)
        dy = w2 * dyc + w1 * _shift_up(catn, 1, tr) + w0 * _shift_up(catn, 2, tr)
        dp_ref[:, :d] = (dzv * yc).astype(BF16)
        dp_ref[:, d:2 * d] = (dy * xt).astype(BF16)
        dp_ref[:, 2 * d:] = (dy * gcv).astype(BF16)

    nxt = lambda i: (jnp.minimum((i + 1) * hb, t // 8 - 1), 0)
    return _call(
        body, name=name, out_shape=[_sds((t, d3), BF16), _sds((3, 8, d), F32)], grid=(nt,),
        in_specs=[pl.BlockSpec((tr, d3), lambda i: (i, 0)),
                  pl.BlockSpec((8, d3), lambda i: (jnp.maximum(i * hb - 1, 0), 0)),
                  pl.BlockSpec((8, d3), nxt),
                  pl.BlockSpec((tr, d), lambda i: (i, 0)),
                  pl.BlockSpec((8, d), nxt),
                  pl.BlockSpec((8, d), lambda i: (0, 0))],
        out_specs=[pl.BlockSpec((tr, d3), lambda i: (i, 0)), pl.BlockSpec((3, 8, d), lambda i: (0, 0, 0))],
        compiler_params=_params("arbitrary"),
    )(p, p, p, dz, dz, cw)


def ada_fwd(c_all, w, b, name):
    nl, d, n = w.shape

    def body(c_ref, w_ref, b_ref, o_ref):
        cv = c_ref[...]
        s = (cv * _sigmoid(cv)).astype(BF16)
        o_ref[...] = jnp.dot(s, w_ref[...].astype(BF16), preferred_element_type=F32) + b_ref[...]

    return _call(
        body, name=name, out_shape=_sds((nl, N_DEV, n), F32), grid=(nl,),
        in_specs=[pl.BlockSpec((N_DEV, d), lambda l: (0, 0)), pl.BlockSpec((None, d, n), lambda l: (l, 0, 0)),
                  pl.BlockSpec((None, 1, n), lambda l: (l, 0, 0))],
        out_specs=pl.BlockSpec((None, N_DEV, n), lambda l: (l, 0, 0)),
        compiler_params=_params("parallel"),
    )(c_all, w, b)


def ada_bwd(c_all, dmod, name):
    nl, _, n = dmod.shape
    d = c_all.shape[1]

    def body(c_ref, dm_ref, o_ref):
        cv = c_ref[...]
        s = (cv * _sigmoid(cv)).astype(BF16)
        o_ref[...] = lax.dot_general(s, dm_ref[...].astype(BF16), TN, preferred_element_type=F32)

    return _call(
        body, name=name, out_shape=_sds((nl, d, n), F32), grid=(nl,),
        in_specs=[pl.BlockSpec((N_DEV, d), lambda l: (0, 0)), pl.BlockSpec((None, N_DEV, n), lambda l: (l, 0, 0))],
        out_specs=pl.BlockSpec((None, d, n), lambda l: (l, 0, 0)),
        compiler_params=_params("parallel"),
    )(c_all, dmod)


def adamw(pieces, w, m, v, name, riders=None):
    nl = len(pieces)
    npc, r, c = pieces[0].shape
    tr = r
    for cand in (1024, 512, 256, 128, 64, 32, 16, 8):
        if r % cand == 0 and cand * c * 4 <= (1 << 20):
            tr = cand
            break
    nt = r // tr

    def update(p_ref, w_ref, m_ref, v_ref, g_ref, d_ref, nm_ref, nv_ref):
        g = p_ref[0].astype(F32)
        for i in range(1, npc):
            g = g + p_ref[i].astype(F32)
        wv = w_ref[...]
        nm = ADAM_B1 * m_ref[...] + (1.0 - ADAM_B1) * g
        nv = ADAM_B2 * v_ref[...] + (1.0 - ADAM_B2) * (g * g)
        m_hat = nm / (1.0 - ADAM_B1 ** ADAM_STEP)
        v_hat = nv / (1.0 - ADAM_B2 ** ADAM_STEP)
        g_ref[...] = g
        d_ref[...] = -ADAM_LR * (m_hat / (jnp.sqrt(v_hat) + ADAM_EPS) + ADAM_WD * wv)
        nm_ref[...] = nm
        nv_ref[...] = nv

    def body(*refs):
        if nl == 1:
            update(*refs)
        else:
            for j in range(nl):
                pl.when(pl.program_id(0) == j)(lambda j=j: update(refs[j], *refs[nl:]))

    row = pl.BlockSpec((tr, c), lambda l, i: (l * nt + i, 0))
    piece_specs = [pl.BlockSpec((npc, tr, c), lambda l, i, j=j: (0, jnp.where(l == j, i, 0), 0))
                   for j in range(nl)]
    res, ridden = _rider_call(
        body, (*pieces, w, m, v), riders, name=name, out_shape=[_sds((nl * r, c), F32)] * 4, grid=(nl, nt),
        in_specs=piece_specs + [row, row, row], out_specs=[row] * 4, scratch_shapes=[],
        sem=("parallel", "parallel"))
    return res, ridden


def sum_pieces(pieces, name):
    npc, r, c = pieces.shape

    def body(p_ref, o_ref):
        g = p_ref[0]
        for i in range(1, npc):
            g = g + p_ref[i]
        o_ref[...] = g

    return _call(body, name=name, out_shape=_sds((r, c), F32),
                 in_specs=[pl.BlockSpec(memory_space=pltpu.VMEM)],
                 out_specs=pl.BlockSpec(memory_space=pltpu.VMEM),
                 compiler_params=pltpu.CompilerParams(vmem_limit_bytes=VMEM_LIMIT))(pieces)


PACK_W = 1024


def _pack(arrs):
    flat = jnp.concatenate([a.reshape(-1).astype(F32) for a in arrs])
    rows = -(-flat.shape[0] // (8 * PACK_W)) * 8
    return jnp.pad(flat, (0, rows * PACK_W - flat.shape[0])).reshape(rows, PACK_W)


def _unpack(slab, shapes):
    flat = slab.reshape(-1)
    out, off = [], 0
    for s in shapes:
        n = 1
        for q in s:
            n *= q
        out.append(flat[off:off + n].reshape(s))
        off += n
    return out


def kernel(x, c, ada_w, ada_b, norm_g, ffn_w_in, ffn_w_out, sb_w_qkv, sb_w_o, sg_w_in, sg_ln_g, sg_w_s, sg_bias, sg_w_out, sc_w_in, sc_conv_w, sc_w_out, cb_w_qkv, cb_rel_bias, cb_w_o, loss_target, m_ada_w, m_ada_b, m_norm_g, m_ffn_w_in, m_ffn_w_out, m_sb_w_qkv, m_sb_w_o, m_sg_w_in, m_sg_ln_g, m_sg_w_s, m_sg_bias, m_sg_w_out, m_sc_w_in, m_sc_conv_w, m_sc_w_out, m_cb_w_qkv, m_cb_rel_bias, m_cb_w_o, v_ada_w, v_ada_b, v_norm_g, v_ffn_w_in, v_ffn_w_out, v_sb_w_qkv, v_sb_w_o, v_sg_w_in, v_sg_ln_g, v_sg_w_s, v_sg_bias, v_sg_w_out, v_sc_w_in, v_sc_conv_w, v_sc_w_out, v_cb_w_qkv, v_cb_rel_bias, v_cb_w_o):
    depth = ada_w.shape[0]
    d = D_MODEL
    xi, yi, ci = lax.axis_index("x"), lax.axis_index("y"), lax.axis_index("c")
    me = 4 * xi + 2 * yi + ci
    x0 = x[0]
    t = x0.shape[0]
    target = loss_target[0]

    c_g, ng, small, w_qkv0 = _all_gather(
        [jnp.pad(c, ((0, 7), (0, 0))), norm_g.reshape(depth * 4, d // N_DEV), _pack([sg_ln_g, sc_conv_w]),
         sb_w_qkv[0].astype(BF16)], "gather_setup")

    c_all = c_g[:, 0, :]
    na = ada_w.shape[2]
    b_cols = lax.dynamic_slice_in_dim(ada_b, me * na, na, axis=1)[:, None, :]
    mod_part = ada_fwd(c_all, ada_w, b_cols, "ada_fwd")
    mod_g = _all_gather([mod_part.reshape(depth * N_DEV, na)], "gather_mod")[0]
    mod_g = mod_g.reshape(N_DEV, depth, N_DEV, na)
    mod_me = lax.dynamic_index_in_dim(mod_g, me, axis=2, keepdims=False)
    mod = jnp.transpose(mod_me, (1, 0, 2)).reshape(depth, 6, 1, d)

    norm_full = jnp.transpose(ng, (1, 0, 2)).reshape(depth, 4, 1, d)
    small = small.reshape(N_DEV, -1)
    nl_g = sg_ln_g.shape[1]
    ln_full = small[:, :nl_g].reshape(1, N_DEV * nl_g)
    cwn = sc_conv_w.shape[2]
    cw_sh = small[:, nl_g:nl_g + 3 * cwn].reshape(N_DEV, 3, cwn)
    cw_full = jnp.transpose(cw_sh, (1, 0, 2)).reshape(3, d)
    cw_pad = jnp.pad(cw_full, ((0, 5), (0, 0)))

    bf = lambda a: a.astype(BF16)
    mixers = [
        [bf(sb_w_qkv[0]), bf(sb_w_o[0])],
        [bf(sg_w_in[0]), bf(sg_w_out[0])],
        [bf(sc_w_in[0]), bf(sc_w_out[0])],
        [bf(cb_w_qkv[0]), bf(cb_w_o[0])],
    ]
    shards = [[bf(ffn_w_in[i]), bf(ffn_w_out[i])] + mixers[i % 4] for i in range(depth)]
    gathered = [[None] * 4 for _ in range(depth)]
    gathered[0][2] = w_qkv0
    riding_shards = [shards[0][0], shards[0][1], shards[0][3]] + shards[1]

    bias_win = band_bias_window(cb_rel_bias[0])
    ws = sg_w_s[0]
    bias_t = jnp.pad(sg_bias[0].T, ((0, 0), (0, BLK - SG_GROUPS)))

    saved = []
    xcur = x0
    h = pre_fwd(x0, norm_full[0, 0], mod[0, 0], mod[0, 1], "L0_pre_m")
    for i in range(depth):
        mi = i % 4
        sh_m, sc_m, gt_m, sh_f, sc_f, gt_f = [mod[i, j] for j in range(6)]
        g0, g1, g2, g3 = [norm_full[i, j] for j in range(4)]
        tag = "L%d_" % i
        sv = {"x_in": xcur, "h_m": h}
        nxt = shards[i + 1] if 0 < i < depth - 1 else None
        if mi == 0:
            qkv = mm_cs(h, gathered[0][2], tag + "qkv", out_dtype=BF16)
            qkv_t = jnp.transpose(qkv.reshape(t // SB_GW, SB_GW, 3 * d), (0, 2, 1))
            o, cmass, riding = sb_fwd(qkv, qkv_t, riding_shards, tag + "sb_fwd")
            gathered[0][0], gathered[0][1], gathered[0][3] = riding[:3]
            gathered[1] = riding[3:]
            sv.update(qkv=qkv, qkv_t=qkv_t, o=o, cmass=cmass)
            mixed = o
        else:
            w_in = gathered[i][2]
            out_dtype = BF16 if mi == 3 else F32
            if nxt is not None:
                pre, (gathered[i + 1][0],) = mm_cs(h, w_in, tag + "mix_in", out_dtype=out_dtype,
                                                   riders=(_Gather, [nxt[0]]))
            else:
                pre = mm_cs(h, w_in, tag + "mix_in", out_dtype=out_dtype)
            if mi == 1:
                mixed = sg_fwd(pre, ln_full, ws, bias_t, tag + "sg_fwd")
                sv.update(a=pre, yy=mixed)
            elif mi == 2:
                mixed = conv_fwd(pre, cw_pad, tag + "conv_fwd")
                sv.update(p=pre, gz=mixed)
            else:
                mixed = band_fwd(pre, bias_win, tag + "band_fwd")
                sv.update(qkv=pre, o=mixed)
        wfi, wfo, _, wmo = gathered[i]
        wfo4 = wfo.reshape(4, -1, d)
        y = mm(mixed, wmo.reshape(-1, d), tag + "mix_out")
        sv["y_m"] = y
        xmid, h2 = post_pre_fwd(xcur, y, g1, gt_m, g2, sh_f, sc_f, tag + "post_m_pre_f")
        sv["x_mid"] = xmid
        ag, au, s3, ridden = ffn_in_swiglu(h2, wfi, tag + "ffn_in",
                                           riders=(_Gather, nxt[1:]) if nxt is not None else None)
        if nxt is not None:
            gathered[i + 1][1:] = ridden
        y2 = mm_rs(s3, wfo4, tag + "ffn_out")
        sv.update(h_f=h2, ag=ag, au=au, s3=s3, y_f=y2)
        saved.append(sv)
        if i + 1 < depth:
            xcur, h = post_pre_fwd(xmid, y2, g3, gt_f, norm_full[i + 1, 0], mod[i + 1, 0], mod[i + 1, 1],
                                   tag + "post_f_pre_m")

    last = depth - 1
    dx, dy2, lpart, dgt_f, dg3 = post_loss_bwd(saved[last]["x_mid"], saved[last]["y_f"], norm_full[last, 3],
                                               mod[last, 5], target, "loss")
    loss = lax.psum(0.5 * jnp.sum(lpart) / d, ("x", "y", "c"))

    dmod_rows = [None] * depth
    dnorm_rows = [None] * depth
    big_pieces = [[None] * 4 for _ in range(depth)]
    pending_dwmi, pending_dwfi, pending_layers = None, [], []
    small_grads = {}
    for i in reversed(range(depth)):
        wfi, wfo, wmi, wmo = gathered[i]
        wfo4 = wfo.reshape(4, -1, d)
        wmo2 = wmo.reshape(-1, d)
        mi = i % 4
        sh_m, sc_m, gt_m, sh_f, sc_f, gt_f = [mod[i, j] for j in range(6)]
        g0, g1, g2, g3 = [norm_full[i, j] for j in range(4)]
        tag = "L%d_b_" % i
        sv = saved[i]
        da3, ridden = ffn_out_dx_swiglu(dy2, wfo4, sv["ag"], sv["au"], tag + "ffn_out_dx",
                                        riders=(_Exchange, [pending_dwmi]) if pending_dwmi is not None else None)
        if pending_dwmi is not None:
            big_pieces[i + 1][2] = ridden[0]
        dwfo = mm_rs_dw(sv["s3"], dy2, tag + "ffn_out_dw", out_dtype=BF16).reshape(N_DEV, -1, d)
        dh2, (big_pieces[i][1],) = mm_cs_dx(da3, wfi, tag + "ffn_in_dx", act_major=True, riders=(_Exchange, [dwfo]))
        dwfi = mm_cs_dw(sv["h_f"], da3, tag + "ffn_in_dw", act_major=True, out_dtype=BF16)
        dx, dy, dsh_f, dsc_f, dg2, dgt_m, dg1 = pre_post_bwd(
            dh2, sv["x_mid"], g2, sc_f, dx, sv["y_m"], g1, gt_m, tag + "pre_f_post_m")
        if mi == 0:
            do = mm_nt(dy, wmo2, tag + "wo_dx", out_dtype=BF16)
            dwmo = mm_tn(sv["o"], dy, tag + "wo_dw", out_dtype=BF16).reshape(N_DEV, -1, d)
            dq, dk, dv, ridden = sb_bwd(sv["qkv"], sv["qkv_t"], do, sv["cmass"],
                                        pending_dwfi + [dwfi, dwmo], tag + "sb_bwd")
            for n, j in enumerate(pending_layers):
                big_pieces[j][0] = ridden[n]
            big_pieces[0][0], big_pieces[0][3] = ridden[-2:]
            dmid = jnp.concatenate([dq, dk, dv], axis=1)
        elif mi == 1:
            dyy = mm_nt(dy, wmo2, tag + "sg_out_dx")
            dwmo = mm_tn(sv["yy"], dy, tag + "sg_out_dw", out_dtype=BF16).reshape(N_DEV, -1, d)
            dmid, dlng, dws, dbt = sg_bwd(sv["a"], dyy, ln_full, ws, bias_t, tag + "sg_bwd")
            small_grads.update(ln_g=jnp.sum(dlng, axis=0), w_s=dws, bias=dbt[:, :SG_GROUPS].T)
        elif mi == 2:
            dgz = mm_nt(dy, wmo2, tag + "sc_out_dx")
            dwmo = mm_tn(sv["gz"], dy, tag + "sc_out_dw", out_dtype=BF16).reshape(N_DEV, -1, d)
            dmid, dcw = conv_bwd(sv["p"], dgz, cw_pad, tag + "conv_bwd")
            small_grads.update(conv_w=jnp.sum(dcw, axis=1))
        else:
            do = mm_nt(dy, wmo2, tag + "wo_dx", out_dtype=BF16)
            dwmo = mm_tn(sv["o"], dy, tag + "wo_dw", out_dtype=BF16).reshape(N_DEV, -1, d)
            dq, dk, dv, dwin, (big_pieces[i][0], big_pieces[i][3]) = band_bwd(
                sv["qkv"], do, bias_win, tag + "band_bwd", riders=(_Exchange, [dwfi, dwmo]))
            dmid = jnp.concatenate([dq, dk, dv], axis=1)
            small_grads.update(rel_bias=band_bias_window_grad(dwin))
        if mi in (1, 2):
            dh, (big_pieces[i][3],) = mm_cs_dx(dmid, wmi, tag + "mix_in_dx", riders=(_Exchange, [dwmo]))
            pending_dwfi.append(dwfi)
            pending_layers.append(i)
        else:
            dh = mm_cs_dx(dmid, wmi, tag + "mix_in_dx")
        pending_dwmi = mm_cs_dw(sv["h_m"], dmid, tag + "mix_in_dw", out_dtype=BF16)
        if i > 0:
            dx, dy2_prev, dsh_m, dsc_m, dg0, dgt_f_prev, dg3_prev = pre_post_bwd(
                dh, sv["x_in"], g0, sc_m, dx, saved[i - 1]["y_f"], norm_full[i - 1, 3], mod[i - 1, 5],
                tag + "pre_m_post_f")
        else:
            dx, dsh_m, dsc_m, dg0 = pre_bwd(dh, sv["x_in"], g0, sc_m, dx, tag + "pre_m")
        dmod_rows[i] = jnp.stack([jnp.sum(q, axis=0) for q in (dsh_m, dsc_m, dgt_m, dsh_f, dsc_f, dgt_f)])
        dnorm_rows[i] = jnp.stack([jnp.sum(q, axis=0) for q in (dg0, dg1, dg2, dg3)])
        if i > 0:
            dy2, dgt_f, dg3 = dy2_prev, dgt_f_prev, dg3_prev
    grad_x = dx[None]

    out_g, out_d, out_m, out_v = {}, {}, {}, {}

    def upd(name, pieces, w, m, v, riders=None):
        rows_cols = (len(pieces) * pieces[0].shape[1], pieces[0].shape[2])
        res, ridden = adamw(pieces, w.reshape(rows_cols), m.reshape(rows_cols), v.reshape(rows_cols),
                            "adamw_" + name, riders)
        out_g[name], out_d[name], out_m[name], out_v[name] = [q.reshape(w.shape) for q in res]
        return ridden

    dmod_mine = jnp.stack(dmod_rows).reshape(depth, 6 * d)
    dnorm_mine = jnp.stack(dnorm_rows)
    small_list = [dnorm_mine, small_grads["ln_g"], small_grads["w_s"], small_grads["bias"],
                  small_grads["conv_w"], small_grads["rel_bias"]]
    small_shapes = [dmod_mine.shape] + [a.shape for a in small_list]
    slab = _pack([dmod_mine] + small_list)
    big_pieces[0][2], = upd("ffn_w_in", [big_pieces[i][0] for i in range(depth)], ffn_w_in, m_ffn_w_in, v_ffn_w_in,
                            riders=(_Exchange, [pending_dwmi]))
    slab_g, = upd("ffn_w_out", [big_pieces[i][1] for i in range(depth)], ffn_w_out, m_ffn_w_out, v_ffn_w_out,
                  riders=(_Gather, [slab]))
    upd("sb_w_qkv", [big_pieces[0][2]], sb_w_qkv, m_sb_w_qkv, v_sb_w_qkv)
    upd("sb_w_o", [big_pieces[0][3]], sb_w_o, m_sb_w_o, v_sb_w_o)
    upd("sg_w_in", [big_pieces[1][2]], sg_w_in, m_sg_w_in, v_sg_w_in)
    upd("sg_w_out", [big_pieces[1][3]], sg_w_out, m_sg_w_out, v_sg_w_out)
    upd("sc_w_in", [big_pieces[2][2]], sc_w_in, m_sc_w_in, v_sc_w_in)
    upd("sc_w_out", [big_pieces[2][3]], sc_w_out, m_sc_w_out, v_sc_w_out)
    upd("cb_w_qkv", [big_pieces[3][2]], cb_w_qkv, m_cb_w_qkv, v_cb_w_qkv)
    upd("cb_w_o", [big_pieces[3][3]], cb_w_o, m_cb_w_o, v_cb_w_o)

    tot = sum_pieces(slab_g, "sum_small_grads")
    g_ada_b_full, g_norm, g_ln, g_ws, g_sbias, g_cw, g_rb = _unpack(tot, small_shapes)
    dmod_all = slab_g.reshape(N_DEV, -1)[:, :depth * 6 * d].reshape(N_DEV, depth, 6 * d)
    dmod_cols = lax.dynamic_slice_in_dim(dmod_all, me * na, na, axis=2)
    g_ada_w = ada_bwd(c_all, jnp.transpose(dmod_cols, (1, 0, 2)), "ada_bwd")

    nsh = d // N_DEV
    g_norm_sh = lax.dynamic_slice_in_dim(g_norm, me * nsh, nsh, axis=2)
    g_ln_sh = lax.dynamic_slice_in_dim(g_ln.reshape(1, -1), me * nl_g, nl_g, axis=1)
    g_cw_sh = lax.dynamic_slice_in_dim(g_cw, me * cwn, cwn, axis=1)[None]

    upd("ada_w", [g_ada_w.reshape(1, depth * d, na)], ada_w, m_ada_w, v_ada_w)

    small_names = ["ada_b", "norm_g", "sg_ln_g", "sg_w_s", "sg_bias", "sc_conv_w", "cb_rel_bias"]
    small_g = [g_ada_b_full, g_norm_sh, g_ln_sh, g_ws[None], g_sbias[None], g_cw_sh, g_rb[None]]
    small_w = [ada_b, norm_g, sg_ln_g, sg_w_s, sg_bias, sc_conv_w, cb_rel_bias]
    small_m = [m_ada_b, m_norm_g, m_sg_ln_g, m_sg_w_s, m_sg_bias, m_sc_conv_w, m_cb_rel_bias]
    small_v = [v_ada_b, v_norm_g, v_sg_ln_g, v_sg_w_s, v_sg_bias, v_sc_conv_w, v_cb_rel_bias]
    shapes = [w.shape for w in small_w]
    res, _ = adamw([_pack(small_g)[None]], _pack(small_w), _pack(small_m), _pack(small_v), "adamw_small")
    for nm_, gs, ds_, ms, vs in zip(small_names, *[_unpack(r, shapes) for r in res]):
        out_g[nm_], out_d[nm_], out_m[nm_], out_v[nm_] = gs, ds_, ms, vs

    order = ["ada_w", "ada_b", "norm_g", "ffn_w_in", "ffn_w_out", "sb_w_qkv", "sb_w_o", "sg_w_in", "sg_ln_g",
             "sg_w_s", "sg_bias", "sg_w_out", "sc_w_in", "sc_conv_w", "sc_w_out", "cb_w_qkv", "cb_rel_bias", "cb_w_o"]
    return (loss, grad_x, *[out_g[n] for n in order], *[out_d[n] for n in order],
            *[out_m[n] for n in order], *[out_v[n] for n in order])
```

```python
import jax
import jax.numpy as jnp
from jax import lax
from jax.experimental import pallas as pl
from jax.experimental.pallas import tpu as pltpu

F32 = jnp.float32
BF16 = jnp.bfloat16
MESH = pl.DeviceIdType.MESH

N_DEV = 8
D_MODEL = 1024
N_HEADS = 16
HEAD_DIM = 64
QK_SCALE = HEAD_DIM ** -0.5
BLK = 128
BAND_BLOCKS = 5
BAND_W = BAND_BLOCKS * BLK
REL_CLIP = 128
EPS = 1e-6
NEG = -1e30
GELU_C0 = 0.7978845608028654
GELU_C1 = 0.044715
ADAM_LR = 0.001
ADAM_B1 = 0.9
ADAM_B2 = 0.999
ADAM_EPS = 1e-08
ADAM_WD = 0.01
ADAM_STEP = 10
VMEM_LIMIT = 56 * 1024 * 1024


def _call(body, **kw):
    return pl.pallas_call(body, **kw)


def _params(*sem):
    return pltpu.CompilerParams(dimension_semantics=sem, vmem_limit_bytes=VMEM_LIMIT)


def _sds(shape, dtype):
    return jax.ShapeDtypeStruct(tuple(shape), dtype)


def _row_tile(t):
    return min(512, t)


def _me():
    x, y, c = lax.axis_index("x"), lax.axis_index("y"), lax.axis_index("c")
    return x, y, c


def _all_gather(arrs, name):
    n = len(arrs)

    def body(*refs):
        gather = _Gather(refs[:n], refs[n:2 * n], *refs[2 * n:])
        gather.start()
        gather.forward()
        gather.finish()

    any_spec = pl.BlockSpec(memory_space=pl.ANY)
    outs = _call(
        body,
        name=name,
        out_shape=_Gather.out_shapes(arrs),
        in_specs=[any_spec] * n,
        out_specs=[any_spec] * n,
        scratch_shapes=_comm_sems(n),
    )(*arrs)
    return list(outs)


def _comm_sems(n):
    if n == 0:
        return []
    return [pltpu.SemaphoreType.DMA((n, 7)), pltpu.SemaphoreType.DMA((n, 7)), pltpu.SemaphoreType.DMA((n,))]


class _Gather:
    def __init__(self, x_refs, o_refs, send_sems, recv_sems, local_sems):
        self.x_refs, self.o_refs = x_refs, o_refs
        self.send_sems, self.recv_sems, self.local_sems = send_sems, recv_sems, local_sems
        x, y, c = _me()
        self.c = c
        self.me, self.sibling = (x, y, c), (x, y, 1 - c)
        self.chips = [(1 - x, y), (x, 1 - y), (1 - x, 1 - y)]

    @staticmethod
    def out_shapes(arrs):
        return [_sds((N_DEV,) + a.shape, a.dtype) for a in arrs]

    def rows(self, a, block):
        px, py, pc = block
        return self.o_refs[a].at[4 * px + 2 * py + pc]

    def copy(self, a, k, block, to, own=False):
        return pltpu.make_async_remote_copy(
            src_ref=self.x_refs[a] if own else self.rows(a, block),
            dst_ref=self.rows(a, block),
            send_sem=self.send_sems.at[a, k],
            recv_sem=self.recv_sems.at[a, k],
            device_id=to,
            device_id_type=MESH,
        )

    def local(self, a):
        return pltpu.make_async_copy(self.x_refs[a], self.rows(a, self.me), self.local_sems.at[a])

    def first(self, a):
        cps = [self.copy(a, 0, self.me, self.sibling, own=True)]
        return cps + [self.copy(a, 1 + j, self.me, (*chip, self.c), own=True) for j, chip in enumerate(self.chips)]

    def passed(self, a):
        return [self.copy(a, 4 + j, (*chip, self.c), self.sibling) for j, chip in enumerate(self.chips)]

    def start(self):
        for a in range(len(self.x_refs)):
            self.local(a).start()
            for cp in self.first(a):
                cp.start()

    def forward(self):
        for a in range(len(self.x_refs)):
            passed = self.passed(a)
            for j, chip in enumerate(self.chips):
                self.copy(a, 1 + j, (*chip, self.c), self.me).wait_recv()
                passed[j].start()

    def finish(self):
        for a in range(len(self.x_refs)):
            self.copy(a, 0, self.sibling, self.me).wait_recv()
            for j, chip in enumerate(self.chips):
                self.copy(a, 4 + j, (*chip, 1 - self.c), self.me).wait_recv()
            for cp in self.first(a) + self.passed(a):
                cp.wait_send()
            self.local(a).wait()


class _Exchange:
    def __init__(self, x_refs, o_refs, send_sems, recv_sems, local_sems):
        self.x_refs, self.o_refs = x_refs, o_refs
        self.send_sems, self.recv_sems, self.local_sems = send_sems, recv_sems, local_sems
        x, y, c = _me()
        self.me = 4 * x + 2 * y + c
        self.peers = []
        for k in range(1, N_DEV):
            px = 1 - x if k & 4 else x
            py = 1 - y if k & 2 else y
            pc = 1 - c if k & 1 else c
            self.peers.append((px, py, pc))

    @staticmethod
    def out_shapes(arrs):
        return [_sds(a.shape, a.dtype) for a in arrs]

    def local(self, a):
        return pltpu.make_async_copy(self.x_refs[a].at[self.me], self.o_refs[a].at[self.me], self.local_sems.at[a])

    def copy(self, a, k, send):
        px, py, pc = self.peers[k]
        peer = 4 * px + 2 * py + pc
        return pltpu.make_async_remote_copy(
            src_ref=self.x_refs[a].at[peer],
            dst_ref=self.o_refs[a].at[self.me if send else peer],
            send_sem=self.send_sems.at[a, k], recv_sem=self.recv_sems.at[a, k],
            device_id=(px, py, pc), device_id_type=MESH)

    def start(self):
        for a in range(len(self.x_refs)):
            self.local(a).start()
            for k in range(N_DEV - 1):
                self.copy(a, k, True).start()

    def finish(self):
        for a in range(len(self.x_refs)):
            for k in range(N_DEV - 1):
                self.copy(a, k, False).wait_recv()
            for k in range(N_DEV - 1):
                self.copy(a, k, True).wait_send()
            self.local(a).wait()


def _all_to_all(arrs, name):
    n = len(arrs)

    def body(*refs):
        exchange = _Exchange(refs[:n], refs[n:2 * n], *refs[2 * n:])
        exchange.start()
        exchange.finish()

    any_spec = pl.BlockSpec(memory_space=pl.ANY)
    outs = _call(
        body,
        name=name,
        out_shape=[_sds(a.shape, a.dtype) for a in arrs],
        in_specs=[any_spec] * n,
        out_specs=[any_spec] * n,
        scratch_shapes=_comm_sems(n),
    )(*arrs)
    return list(outs)


NN = (((1,), (0,)), ((), ()))
NT = (((1,), (1,)), ((), ()))
TN = (((0,), (0,)), ((), ()))


def _all_of(conds):
    out = conds[0]
    for cond in conds[1:]:
        out = out & cond
    return out


def _rider_call(body, operands, riders, *, name, out_shape, grid, in_specs, out_specs, scratch_shapes, sem):
    if not riders or not riders[1]:
        res = _call(body, name=name, out_shape=out_shape, grid=grid, in_specs=in_specs, out_specs=out_specs,
                    scratch_shapes=list(scratch_shapes), compiler_params=_params(*sem))(*operands)
        return list(res), []
    cls, arrs = riders
    nr, ni, no, ns = len(arrs), len(in_specs), len(out_specs), len(scratch_shapes)
    forward_at = (3 * grid[0]) // 4 if grid[0] >= 4 else None

    def wrapped(*refs):
        ids = [pl.program_id(ax) for ax in range(len(grid))]
        comm = cls(refs[ni:ni + nr], refs[ni + nr + no:ni + 2 * nr + no], *refs[ni + 2 * nr + no + ns:])
        pl.when(_all_of([i == 0 for i in ids]))(comm.start)
        if cls is _Gather and forward_at is not None:
            pl.when(_all_of([ids[0] == forward_at] + [i == 0 for i in ids[1:]]))(comm.forward)
        body(*refs[:ni], *refs[ni + nr:ni + nr + no], *refs[ni + 2 * nr + no:ni + 2 * nr + no + ns])

        def end():
            if cls is _Gather and forward_at is None:
                comm.forward()
            comm.finish()

        pl.when(_all_of([i == g - 1 for i, g in zip(ids, grid)]))(end)

    any_spec = pl.BlockSpec(memory_space=pl.ANY)
    res = _call(wrapped, name=name, out_shape=list(out_shape) + cls.out_shapes(arrs), grid=grid,
                in_specs=list(in_specs) + [any_spec] * nr, out_specs=list(out_specs) + [any_spec] * nr,
                scratch_shapes=list(scratch_shapes) + _comm_sems(nr),
                compiler_params=_params(*(["arbitrary"] * len(grid))))(*operands, *arrs)
    return list(res[:no]), list(res[no:])


def _gemm(a, b, out_shape, out_dtype, grid, a_spec, b_spec, o_spec, acc_shape, dims, name, riders=None):
    nk = grid[2]

    if nk == 1:
        def body(a_ref, b_ref, o_ref):
            r = lax.dot_general(a_ref[...].astype(BF16), b_ref[...].astype(BF16), dims,
                                preferred_element_type=F32)
            o_ref[...] = r.astype(o_ref.dtype)
        scratch = []
    else:
        def body(a_ref, b_ref, o_ref, acc_ref):
            k = pl.program_id(2)

            @pl.when(k == 0)
            def _():
                acc_ref[...] = jnp.zeros_like(acc_ref)

            acc_ref[...] += lax.dot_general(a_ref[...].astype(BF16), b_ref[...].astype(BF16), dims,
                                            preferred_element_type=F32)

            @pl.when(k == nk - 1)
            def _():
                o_ref[...] = acc_ref[...].astype(o_ref.dtype)
        scratch = [pltpu.VMEM(acc_shape, F32)]

    res, ridden = _rider_call(
        body, (a, b), riders, name=name, out_shape=[_sds(out_shape, out_dtype)], grid=grid,
        in_specs=[a_spec, b_spec], out_specs=[o_spec], scratch_shapes=scratch,
        sem=("parallel", "parallel", "arbitrary"))
    return (res[0], ridden) if riders else res[0]


def _div_tile(n, want):
    if n <= want:
        return n
    t = want - want % 128
    while n % t:
        t -= 128
    return t


def mm(a, b, name, out_dtype=F32, tm=512, tn=1024, tk=1024):
    m, k = a.shape
    n = b.shape[1]
    tm, tn, tk = _div_tile(m, tm), _div_tile(n, tn), _div_tile(k, tk)
    return _gemm(a, b, (m, n), out_dtype, (m // tm, n // tn, k // tk),
                 pl.BlockSpec((tm, tk), lambda i, j, kk: (i, kk)),
                 pl.BlockSpec((tk, tn), lambda i, j, kk: (kk, j)),
                 pl.BlockSpec((tm, tn), lambda i, j, kk: (i, j)),
                 (tm, tn), NN, name)


def mm_nt(a, b, name, out_dtype=F32, tm=512, tn=1024, tk=1024):
    m, n = a.shape
    k = b.shape[0]
    tm, tk_out, tred = _div_tile(m, tm), _div_tile(k, tn), _div_tile(n, tk)
    return _gemm(a, b, (m, k), out_dtype, (m // tm, k // tk_out, n // tred),
                 pl.BlockSpec((tm, tred), lambda i, j, kk: (i, kk)),
                 pl.BlockSpec((tk_out, tred), lambda i, j, kk: (j, kk)),
                 pl.BlockSpec((tm, tk_out), lambda i, j, kk: (i, j)),
                 (tm, tk_out), NT, name)


def mm_tn(a, b, name, out_dtype=F32, tm=512, tn=1024, tk=1024):
    m, k = a.shape
    n = b.shape[1]
    tk_out, tn, tred = _div_tile(k, tk), _div_tile(n, tn), _div_tile(m, tm)
    return _gemm(a, b, (k, n), out_dtype, (k // tk_out, n // tn, m // tred),
                 pl.BlockSpec((tred, tk_out), lambda i, j, kk: (kk, i)),
                 pl.BlockSpec((tred, tn), lambda i, j, kk: (kk, j)),
                 pl.BlockSpec((tk_out, tn), lambda i, j, kk: (i, j)),
                 (tk_out, tn), TN, name)


def mm_cs(a, wg, name, act_major=False, out_dtype=F32, tm=1024, riders=None):
    m, k = a.shape
    s, _, n = wg.shape
    tm = _div_tile(m, tm)
    if act_major:
        out_shape, o_spec = (s, m, n), pl.BlockSpec((None, tm, n), lambda i, j, kk: (j, i, 0))
    else:
        out_shape, o_spec = (m, s * n), pl.BlockSpec((tm, n), lambda i, j, kk: (i, j))
    return _gemm(a, wg, out_shape, out_dtype, (m // tm, s, 1),
                 pl.BlockSpec((tm, k), lambda i, j, kk: (i, 0)),
                 pl.BlockSpec((None, k, n), lambda i, j, kk: (j, 0, 0)),
                 o_spec, (tm, n), NN, name, riders)


def mm_cs_dx(da, wg, name, act_major=False, out_dtype=F32, tm=1024, riders=None):
    s, k, n = wg.shape
    m = da.shape[1] if act_major else da.shape[0]
    tm = _div_tile(m, tm)
    if act_major:
        a_spec = pl.BlockSpec((None, tm, n), lambda i, j, kk: (kk, i, 0))
    else:
        a_spec = pl.BlockSpec((tm, n), lambda i, j, kk: (i, kk))
    return _gemm(da, wg, (m, k), out_dtype, (m // tm, 1, s), a_spec,
                 pl.BlockSpec((None, k, n), lambda i, j, kk: (kk, 0, 0)),
                 pl.BlockSpec((tm, k), lambda i, j, kk: (i, 0)),
                 (tm, k), NT, name, riders)


def mm_cs_dw(a, da, name, act_major=False, out_dtype=F32, tm=1024, riders=None):
    m, k = a.shape
    if act_major:
        s, _, n = da.shape
    else:
        s, n = N_DEV, da.shape[1] // N_DEV
    tm = _div_tile(m, tm)
    if act_major:
        b_spec = pl.BlockSpec((None, tm, n), lambda i, j, kk: (i, kk, 0))
    else:
        b_spec = pl.BlockSpec((tm, n), lambda i, j, kk: (kk, i))
    return _gemm(a, da, (s, k, n), out_dtype, (s, 1, m // tm),
                 pl.BlockSpec((tm, k), lambda i, j, kk: (kk, 0)), b_spec,
                 pl.BlockSpec((None, k, n), lambda i, j, kk: (i, 0, 0)),
                 (k, n), TN, name, riders)


def mm_rs(s3, w3, name, out_dtype=F32, tm=1024, riders=None):
    s, m, n = s3.shape
    nn = w3.shape[2]
    tm = _div_tile(m, tm)
    return _gemm(s3, w3, (m, nn), out_dtype, (m // tm, 1, s),
                 pl.BlockSpec((None, tm, n), lambda i, j, kk: (kk, i, 0)),
                 pl.BlockSpec((None, n, nn), lambda i, j, kk: (kk, 0, 0)),
                 pl.BlockSpec((tm, nn), lambda i, j, kk: (i, 0)),
                 (tm, nn), NN, name, riders)


def mm_rs_dx(dy, w3, name, out_dtype=F32, tm=1024):
    m, nn = dy.shape
    s, n, _ = w3.shape
    tm = _div_tile(m, tm)
    return _gemm(dy, w3, (s, m, n), out_dtype, (m // tm, s, 1),
                 pl.BlockSpec((tm, nn), lambda i, j, kk: (i, 0)),
                 pl.BlockSpec((None, n, nn), lambda i, j, kk: (j, 0, 0)),
                 pl.BlockSpec((None, tm, n), lambda i, j, kk: (j, i, 0)),
                 (tm, n), NT, name)


def mm_rs_dw(s3, dy, name, out_dtype=F32, tm=1024):
    s, m, n = s3.shape
    nn = dy.shape[1]
    tm = _div_tile(m, tm)
    return _gemm(s3, dy, (s, n, nn), out_dtype, (s, 1, m // tm),
                 pl.BlockSpec((None, tm, n), lambda i, j, kk: (i, kk, 0)),
                 pl.BlockSpec((tm, nn), lambda i, j, kk: (kk, 0)),
                 pl.BlockSpec((None, n, nn), lambda i, j, kk: (i, 0, 0)),
                 (n, nn), TN, name)


def _colsum8(v):
    tr, d = v.shape
    return v.reshape(tr // 8, 8, d).sum(axis=0)


def _rstd(v):
    return lax.rsqrt(jnp.mean(v * v, axis=-1, keepdims=True) + EPS)


def _vec_spec(d):
    return pl.BlockSpec((1, d), lambda i: (0, 0))


def _acc_spec(d):
    return pl.BlockSpec((8, d), lambda i: (0, 0))


def _pre_rows(xv, g, shift, scale):
    return ((xv * _rstd(xv)) * g) * (1 + scale) + shift


def _post_rows(xv, yv, g, gate):
    return xv + gate * ((yv * _rstd(yv)) * g)


def _post_bwd_rows(dxv, yv, g, gate):
    r = _rstd(yv)
    yhat = yv * r
    dgate = _colsum8(dxv * (yhat * g))
    dyn = gate * dxv
    dg = _colsum8(dyn * yhat)
    dyhat = dyn * g
    dy = r * (dyhat - yhat * jnp.mean(dyhat * yhat, axis=-1, keepdims=True))
    return dy, dgate, dg


def _pre_bwd_rows(dhv, xv, g, scale, dxn):
    r = _rstd(xv)
    xhat = xv * r
    dshift = _colsum8(dhv)
    dscale = _colsum8(dhv * (xhat * g))
    dmod = dhv * (1 + scale)
    dg = _colsum8(dmod * xhat)
    dxhat = dmod * g
    dx = r * (dxhat - xhat * jnp.mean(dxhat * xhat, axis=-1, keepdims=True)) + dxn
    return dx, dshift, dscale, dg


def _row_call(body, name, t, d, rows_in, vecs_in, rows_out, n_acc):
    tr = _row_tile(t)
    nri, nvi, nro = len(rows_in), len(vecs_in), len(rows_out)

    def wrapped(*refs):
        accs = refs[nri + nvi + nro:]
        if n_acc:
            @pl.when(pl.program_id(0) == 0)
            def _():
                for acc in accs:
                    acc[...] = jnp.zeros_like(acc)
        body(*refs)

    row = pl.BlockSpec((tr, d), lambda i: (i, 0))
    return _call(wrapped, name=name,
                 out_shape=[_sds((t, d), dt) for dt in rows_out] + [_sds((8, d), F32)] * n_acc,
                 grid=(t // tr,), in_specs=[row] * nri + [_vec_spec(d)] * nvi,
                 out_specs=[row] * nro + [_acc_spec(d)] * n_acc,
                 compiler_params=_params("arbitrary" if n_acc else "parallel"))(*rows_in, *vecs_in)


def pre_fwd(x, g, shift, scale, name):
    def body(x_ref, g_ref, sh_ref, sc_ref, h_ref):
        h_ref[...] = _pre_rows(x_ref[...], g_ref[...], sh_ref[...], sc_ref[...]).astype(BF16)

    return _row_call(body, name, *x.shape, [x], [g, shift, scale], [BF16], 0)[0]


def post_pre_fwd(x, y, g_post, gate, g_pre, shift, scale, name):
    def body(x_ref, y_ref, gp_ref, gt_ref, g_ref, sh_ref, sc_ref, xn_ref, h_ref):
        xn = _post_rows(x_ref[...], y_ref[...], gp_ref[...], gt_ref[...])
        xn_ref[...] = xn
        h_ref[...] = _pre_rows(xn, g_ref[...], sh_ref[...], sc_ref[...]).astype(BF16)

    return _row_call(body, name, *x.shape, [x, y], [g_post, gate, g_pre, shift, scale], [F32, BF16], 0)


def post_loss_bwd(x, y, g, gate, target, name):
    d = x.shape[1]

    def body(x_ref, y_ref, t_ref, g_ref, gt_ref, dx_ref, dy_ref, l_ref, dgate_ref, dg_ref):
        yv, gv, gate_v = y_ref[...], g_ref[...], gt_ref[...]
        err = _post_rows(x_ref[...], yv, gv, gate_v) - t_ref[...]
        l_ref[...] += _colsum8(err * err)
        dxv = err * (1.0 / d)
        dx_ref[...] = dxv
        dy, dgate, dg = _post_bwd_rows(dxv, yv, gv, gate_v)
        dy_ref[...] = dy.astype(BF16)
        dgate_ref[...] += dgate
        dg_ref[...] += dg

    return _row_call(body, name, *x.shape, [x, y, target], [g, gate], [F32, BF16], 3)


def pre_post_bwd(dh, x, g_pre, scale, dxn, y, g_post, gate, name):
    def body(dh_ref, x_ref, dxn_ref, y_ref, g_ref, sc_ref, gp_ref, gt_ref,
             dx_ref, dy_ref, dsh_ref, dsc_ref, dg_ref, dgate_ref, dgp_ref):
        dx, dsh, dsc, dg = _pre_bwd_rows(dh_ref[...].astype(F32), x_ref[...], g_ref[...], sc_ref[...], dxn_ref[...])
        dx_ref[...] = dx
        dsh_ref[...] += dsh
        dsc_ref[...] += dsc
        dg_ref[...] += dg
        dy, dgate, dgp = _post_bwd_rows(dx, y_ref[...], gp_ref[...], gt_ref[...])
        dy_ref[...] = dy.astype(BF16)
        dgate_ref[...] += dgate
        dgp_ref[...] += dgp

    return _row_call(body, name, *x.shape, [dh, x, dxn, y], [g_pre, scale, g_post, gate], [F32, BF16], 5)


def pre_bwd(dh, x, g, scale, dxn, name):
    def body(dh_ref, x_ref, dxn_ref, g_ref, sc_ref, dx_ref, dsh_ref, dsc_ref, dg_ref):
        dx, dsh, dsc, dg = _pre_bwd_rows(dh_ref[...].astype(F32), x_ref[...], g_ref[...], sc_ref[...], dxn_ref[...])
        dx_ref[...] = dx
        dsh_ref[...] += dsh
        dsc_ref[...] += dsc
        dg_ref[...] += dg

    return _row_call(body, name, *x.shape, [dh, x, dxn], [g, scale], [F32], 3)


def _sigmoid(x):
    return 1.0 / (1.0 + jnp.exp(-x))


def ffn_in_swiglu(h, wg, name, tm=1024, riders=None):
    m, k = h.shape
    s, _, n = wg.shape
    half = s // 2
    tm = _div_tile(m, tm)

    def body(h_ref, wg_ref, wu_ref, g_ref, u_ref, s_ref):
        hv = h_ref[...]
        g = jnp.dot(hv, wg_ref[...], preferred_element_type=F32)
        u = jnp.dot(hv, wu_ref[...], preferred_element_type=F32)
        g_ref[...] = g
        u_ref[...] = u
        s_ref[...] = ((g * _sigmoid(g)) * u).astype(BF16)

    act = pl.BlockSpec((None, tm, n), lambda i, j: (j, i, 0))
    res, ridden = _rider_call(
        body, (h, wg, wg), riders, name=name,
        out_shape=[_sds((half, m, n), F32), _sds((half, m, n), F32), _sds((half, m, n), BF16)],
        grid=(m // tm, half),
        in_specs=[pl.BlockSpec((tm, k), lambda i, j: (i, 0)),
                  pl.BlockSpec((None, k, n), lambda i, j: (j, 0, 0)),
                  pl.BlockSpec((None, k, n), lambda i, j: (j + half, 0, 0))],
        out_specs=[act, act, act], scratch_shapes=[], sem=("parallel", "parallel"))
    return res[0], res[1], res[2], ridden


def ffn_out_dx_swiglu(dy, w4, gate, up, name, tm=1024, riders=None):
    m, nn = dy.shape
    half, n, _ = w4.shape
    tm = _div_tile(m, tm)

    def body(dy_ref, w_ref, g_ref, u_ref, o_ref):
        ds = lax.dot_general(dy_ref[...], w_ref[...], NT, preferred_element_type=F32)
        g, u = g_ref[...], u_ref[...]
        sig = _sigmoid(g)
        o_ref[0] = (ds * u * (sig * (1 + g * (1 - sig)))).astype(BF16)
        o_ref[1] = (ds * (g * sig)).astype(BF16)

    act = pl.BlockSpec((None, tm, n), lambda i, j: (j, i, 0))
    res, ridden = _rider_call(
        body, (dy, w4, gate, up), riders, name=name, out_shape=[_sds((2, half, m, n), BF16)],
        grid=(m // tm, half),
        in_specs=[pl.BlockSpec((tm, nn), lambda i, j: (i, 0)),
                  pl.BlockSpec((None, n, nn), lambda i, j: (j, 0, 0)), act, act],
        out_specs=[pl.BlockSpec((2, None, tm, n), lambda i, j: (0, j, i, 0))],
        scratch_shapes=[], sem=("parallel", "parallel"))
    return res[0].reshape(2 * half, m, n), ridden


def _split_hi_lo(v):
    hi = v.astype(BF16)
    lo = (v - hi.astype(F32)).astype(BF16)
    return hi, lo


SB_G = 2
SB_EXP_ZERO = 104.0
SB_UNSEEN = 3e38
SB_GW = SB_G * BLK


def _sb_specs(t):
    nq = t // BLK
    npair = N_HEADS // 2
    q_spec = pl.BlockSpec((BLK, BLK), lambda p, qb: (qb, p))
    k_spec = pl.BlockSpec((t, BLK), lambda p, qb: (0, npair + p))
    v_spec = pl.BlockSpec((t, BLK), lambda p, qb: (0, 2 * npair + p))
    kt_spec = pl.BlockSpec((t // SB_GW, BLK, SB_GW), lambda p, qb: (0, npair + p, 0))
    vt_spec = pl.BlockSpec((t // SB_GW, BLK, SB_GW), lambda p, qb: (0, 2 * npair + p, 0))
    c_spec = pl.BlockSpec((None, nq, 8, 2 * BLK), lambda p, qb: (p, 0, 0, qb))
    return nq, npair, q_spec, k_spec, v_spec, kt_spec, vt_spec, c_spec


def _sb_consts():
    row = lax.broadcasted_iota(jnp.int32, (BLK, BLK), 0)
    col = lax.broadcasted_iota(jnp.int32, (BLK, BLK), 1)
    lane0 = (col < HEAD_DIM).astype(F32)
    sub0 = (row < HEAD_DIM).astype(F32)
    return row, col, lane0, sub0


def _sb_valid(ks, qb):
    row = lax.broadcasted_iota(jnp.int32, (SB_GW, 2 * BLK), 0)
    col = lax.broadcasted_iota(jnp.int32, (SB_GW, 2 * BLK), 1)
    return (ks + row) < (qb * BLK + (col & (BLK - 1)))


def _blocks_on_lanes(v4):
    return jnp.concatenate([v4[b * BLK:(b + 1) * BLK] for b in range(SB_G)], axis=1)


def _tri2_dot(tri2, v):
    hi, lo = _split_hi_lo(v)
    return jnp.dot(tri2, jnp.concatenate([hi, lo], axis=0), preferred_element_type=F32)


def _sb_pair_loop(first, count, step, group, skip, carry):
    def pair(it, cy):
        g1 = first + 2 * step * it
        cy = group(g1, 0, 1, cy)
        return lax.cond(2 * it + 1 < count, lambda c: group(g1 + step, 1, 0, c), skip, cy)
    return lax.fori_loop(0, (count + 1) // 2, pair, carry)


def sb_fwd(qkv, qkv_t, riders, name):
    t = qkv.shape[0]
    assert t % SB_GW == 0
    nq, npair, q_spec, k_spec, _, _, vt_spec, c_spec = _sb_specs(t)
    nr = len(riders)

    def body(*refs):
        q_ref, k_ref, vt_ref = refs[:3]
        o_ref, c_ref = refs[3 + nr:5 + nr]
        oacc, zbuf0, zbuf1, kmax = refs[5 + 2 * nr:9 + 2 * nr]
        pp = pl.program_id(0)
        qb = pl.program_id(1)
        if nr:
            gather = _Gather(refs[3:3 + nr], refs[5 + nr:5 + 2 * nr], *refs[9 + 2 * nr:])
            pl.when((pp == 0) & (qb == 0))(gather.start)
            pl.when((pp == npair - 2) & (qb == 0))(gather.forward)
        _sb_fwd_step(q_ref, k_ref, vt_ref, o_ref, c_ref, oacc, zbuf0, zbuf1, kmax, qb)
        if nr:
            pl.when((pp == npair - 1) & (qb == nq - 1))(gather.finish)

    any_spec = pl.BlockSpec(memory_space=pl.ANY)
    outs = _call(
        body, name=name,
        out_shape=[_sds((t, D_MODEL), BF16), _sds((npair, nq, 8, 2 * t), F32)] + _Gather.out_shapes(riders),
        grid=(npair, nq), in_specs=[q_spec, k_spec, vt_spec] + [any_spec] * nr,
        out_specs=[pl.BlockSpec((BLK, BLK), lambda p, qb: (qb, p)), c_spec] + [any_spec] * nr,
        scratch_shapes=[pltpu.VMEM((BLK, 2 * BLK), F32), pltpu.VMEM((SB_GW, 2 * BLK), F32),
                        pltpu.VMEM((SB_GW, 2 * BLK), F32), pltpu.VMEM((8, BLK), F32)] + _comm_sems(nr),
        compiler_params=_params("arbitrary", "arbitrary"),
    )(qkv, qkv, qkv_t, *riders)
    return outs[0], outs[1], list(outs[2:])


def _sb_fwd_step(q_ref, k_ref, vt_ref, o_ref, c_ref, oacc, zbuf0, zbuf1, kmax, qb):
    row, col, lane0, sub0 = _sb_consts()
    tri = (col >= row).astype(BF16)
    tri2 = jnp.concatenate([tri, tri], axis=1)
    q2 = _two_heads(q_ref[...], lane0, QK_SCALE)
    zbufs = (zbuf0, zbuf1)
    c_ref[...] = jnp.full(c_ref.shape, SB_UNSEEN, F32)
    oacc[...] = jnp.zeros_like(oacc)

    @pl.when(qb == 0)
    def _():
        ksq = jnp.square(k_ref[...].astype(F32))
        head0 = (lax.broadcasted_iota(jnp.int32, (1, BLK), 1) < HEAD_DIM).astype(F32)
        norms = jnp.maximum(jnp.sum(ksq * head0, axis=1, keepdims=True),
                            jnp.sum(ksq * (1.0 - head0), axis=1, keepdims=True))
        kmax[...] = jnp.broadcast_to(jnp.max(norms, axis=0, keepdims=True), kmax.shape)

    qsq = jnp.square(q2.astype(F32)).astype(BF16)
    qn2 = jnp.max(lax.dot_general(jnp.ones((8, BLK), BF16), qsq, NT, preferred_element_type=F32),
                  axis=0, keepdims=True)
    kk = kmax[0:1, :]
    zbound = jnp.sqrt(qn2 * jnp.concatenate([kk, kk], axis=1)) * 1.02

    def matters(cr):
        return (jnp.min(cr - zbound) <= SB_EXP_ZERO).astype(jnp.int32)

    def scores(g):
        ks = pl.multiple_of(g * SB_GW, SB_GW)
        return lax.dot_general(k_ref[pl.ds(ks, SB_GW), :], q2, NT, preferred_element_type=F32)

    def group(g, cur, nxt, cr, masked=False):
        z = zbufs[cur][...]
        zbufs[nxt][...] = scores(jnp.maximum(g - 1, 0))
        e = jnp.exp(-jnp.abs(z))
        sp = jnp.maximum(z, 0.0) + jnp.log(1.0 + e)
        if masked:
            valid = _sb_valid(g * SB_GW, qb)
            sp = jnp.where(valid, sp, 0.0)
        loc = _tri2_dot(tri2, _blocks_on_lanes(sp))
        parts = [None] * SB_G
        for b in reversed(range(SB_G)):
            rows = slice(b * BLK, (b + 1) * BLK)
            c_ref[g * SB_G + b] = jnp.broadcast_to(cr, (8, 2 * BLK))
            a = jnp.exp(z[rows] - (loc[:, 2 * b * BLK:2 * (b + 1) * BLK] + cr))
            if masked:
                a = jnp.where(valid[rows], a, 0.0)
            parts[b] = a.astype(BF16)
            cr = cr + jnp.sum(sp[rows], axis=0, keepdims=True)
        oacc[...] += jnp.dot(vt_ref[g], jnp.concatenate(parts, axis=0), preferred_element_type=F32)
        return cr

    last = qb // SB_G
    zbuf1[...] = scores(last)
    cr = group(last, 1, 0, jnp.zeros((1, 2 * BLK), F32), masked=True)

    def pair(state):
        g, cr, _ = state
        cr = group(g, 0, 1, cr)
        more = (g >= 1).astype(jnp.int32) * matters(cr)
        cr = lax.cond(more > 0, lambda c: group(g - 1, 1, 0, c), lambda c: c, cr)
        return jnp.where(more > 0, g - 2, -1), cr, matters(cr)

    lax.while_loop(lambda st: (st[0] >= 0) & (st[2] > 0), pair, (last - 1, cr, matters(cr)))
    o_t = oacc[:, :BLK] * sub0 + oacc[:, BLK:] * (1.0 - sub0)
    o_ref[...] = o_t.T.astype(BF16)


def sb_bwd(qkv, qkv_t, do, cmass, riders, name):
    t = qkv.shape[0]
    nq, npair, q_spec, k_spec, v_spec, kt_spec, _, c_spec = _sb_specs(t)
    nr = len(riders)

    def body(*refs):
        pp = pl.program_id(0)
        qb = pl.program_id(1)
        if nr:
            exchange = _Exchange(refs[6:6 + nr], refs[9 + nr:9 + 2 * nr], *refs[14 + 2 * nr:])
            pl.when((pp == 0) & (qb == 0))(exchange.start)
        step(*refs[:6], *refs[6 + nr:9 + nr], *refs[9 + 2 * nr:14 + 2 * nr])
        if nr:
            pl.when((pp == npair - 1) & (qb == nq - 1))(exchange.finish)

    def step(q_ref, k_ref, kt_ref, v_ref, do_ref, c_ref, dq_ref, dk_ref, dv_ref, dqacc, dkacc, dvacc,
             zbuf0, zbuf1):
        qb = pl.program_id(1)

        @pl.when(qb == 0)
        def _():
            dkacc[...] = jnp.zeros_like(dkacc)
            dvacc[...] = jnp.zeros_like(dvacc)

        row, col, lane0, sub0 = _sb_consts()
        tri_suf = (col >= row).astype(BF16)
        tri_pre = (col <= row).astype(BF16)
        tri2_suf = jnp.concatenate([tri_suf, tri_suf], axis=1)
        tri2_pre = jnp.concatenate([tri_pre, tri_pre], axis=1)
        q2 = _two_heads(q_ref[...], lane0, QK_SCALE)
        do2 = _two_heads(do_ref[...], lane0, 1.0)
        zbufs = (zbuf0, zbuf1)
        dqacc[...] = jnp.zeros_like(dqacc)
        last = qb // SB_G

        def scores(g):
            ks = pl.multiple_of(g * SB_GW, SB_GW)
            return lax.dot_general(k_ref[pl.ds(ks, SB_GW), :], q2, NT, preferred_element_type=F32)

        def group(g, cur, nxt, gc, masked=False):
            ks = pl.multiple_of(g * SB_GW, SB_GW)
            z = zbufs[cur][...]
            zbufs[nxt][...] = scores(jnp.minimum(g + 1, last))
            e = jnp.exp(-jnp.abs(z))
            sig = 0.5 * jnp.tanh(0.5 * z) + 0.5
            sp = jnp.maximum(z, 0.0) + jnp.log(1.0 + e)
            if masked:
                valid = _sb_valid(ks, qb)
                sp = jnp.where(valid, sp, 0.0)
            loc = _tri2_dot(tri2_suf, _blocks_on_lanes(sp))
            parts = []
            for b in range(SB_G):
                rows = slice(b * BLK, (b + 1) * BLK)
                mass = loc[:, 2 * b * BLK:2 * (b + 1) * BLK] + c_ref[g * SB_G + b, 0:1, :]
                parts.append(jnp.exp(z[rows] - mass))
            a = jnp.concatenate(parts, axis=0)
            if masked:
                a = jnp.where(valid, a, 0.0)
            gr = lax.dot_general(v_ref[pl.ds(ks, SB_GW), :], do2, NT, preferred_element_type=F32) * a
            pre = _tri2_dot(tri2_pre, _blocks_on_lanes(gr))
            parts = []
            for b in range(SB_G):
                rows = slice(b * BLK, (b + 1) * BLK)
                parts.append(pre[:, 2 * b * BLK:2 * (b + 1) * BLK] + gc)
                gc = gc + jnp.sum(gr[rows], axis=0, keepdims=True)
            dz = gr - sig * jnp.concatenate(parts, axis=0)
            if masked:
                dz = jnp.where(valid, dz, 0.0)
            dz = dz.astype(BF16)
            dkacc[pl.ds(ks, SB_GW), :] += jnp.dot(dz, q2, preferred_element_type=F32)
            dqacc[...] += jnp.dot(kt_ref[g], dz, preferred_element_type=F32)
            dvacc[pl.ds(ks, SB_GW), :] += jnp.dot(a.astype(BF16), do2, preferred_element_type=F32)
            return gc

        def skip(gc):
            zbuf0[...] = zbuf1[...]
            return gc

        def unseen(g):
            return (jnp.max(c_ref[g * SB_G + SB_G - 1, 0:1, :]) > 0.5 * SB_UNSEEN).astype(jnp.int32)

        first, _ = lax.while_loop(lambda st: (st[0] > 0) & (st[1] == 0),
                                  lambda st: (st[0] - 1, unseen(jnp.maximum(st[0] - 2, 0))),
                                  (last, unseen(jnp.maximum(last - 1, 0))))
        zbuf0[...] = scores(first)
        gc = _sb_pair_loop(first, last - first, 1, group, skip, jnp.zeros((1, 2 * BLK), F32))
        group(last, 0, 1, gc, masked=True)
        dq_t = (dqacc[:, :BLK] * sub0 + dqacc[:, BLK:] * (1.0 - sub0)) * QK_SCALE
        dq_ref[...] = dq_t.T.astype(BF16)

        @pl.when(qb == nq - 1)
        def _():
            dk_ref[...] = dkacc[...].astype(BF16)
            dv_ref[...] = dvacc[...].astype(BF16)

    col_spec = pl.BlockSpec((t, BLK), lambda p, qb: (0, p))
    blk_spec = pl.BlockSpec((BLK, BLK), lambda p, qb: (qb, p))
    any_spec = pl.BlockSpec(memory_space=pl.ANY)
    outs = _call(
        body, name=name,
        out_shape=[_sds((t, D_MODEL), BF16)] * 3 + [_sds(r.shape, r.dtype) for r in riders],
        grid=(npair, nq), in_specs=[q_spec, k_spec, kt_spec, v_spec, blk_spec, c_spec] + [any_spec] * nr,
        out_specs=[blk_spec, col_spec, col_spec] + [any_spec] * nr,
        scratch_shapes=[pltpu.VMEM((BLK, 2 * BLK), F32), pltpu.VMEM((t, BLK), F32), pltpu.VMEM((t, BLK), F32),
                        pltpu.VMEM((SB_GW, 2 * BLK), F32), pltpu.VMEM((SB_GW, 2 * BLK), F32)] + _comm_sems(nr),
        compiler_params=_params("arbitrary", "arbitrary"),
    )(qkv, qkv, qkv_t, qkv, do, cmass, *riders)
    return outs[0], outs[1], outs[2], list(outs[3:])


BAND_QPS = 4


def _band_static_mask(jj):
    row = lax.broadcasted_iota(jnp.int32, (2 * BLK, BLK), 0)
    col = lax.broadcasted_iota(jnp.int32, (2 * BLK, BLK), 1)
    qc = (row & (BLK - 1)) // 64
    kc = 2 * jj + col // 64
    return (kc >= qc) & (kc <= qc + 8)


def _band_key_start(qb, jj):
    kb = qb - (BAND_BLOCKS - 1) + jj
    return kb, pl.multiple_of(jnp.maximum(kb, 0) * BLK, BLK)


def _band_probs(q2, k_ref, bias, qb):
    blocks = []
    for jj in range(BAND_BLOCKS):
        kb, ks = _band_key_start(qb, jj)
        s = lax.dot_general(q2, k_ref[pl.ds(ks, BLK), :], NT, preferred_element_type=F32)
        s = s + bias[:, jj * BLK:(jj + 1) * BLK]
        ok = (kb >= 0) if 0 < jj < BAND_BLOCKS - 1 else _band_static_mask(jj) & (kb >= 0)
        blocks.append(jnp.where(ok, s, NEG))
    s = jnp.concatenate(blocks, axis=1)
    m = jnp.max(s, axis=-1, keepdims=True)
    e = jnp.exp(s - m)
    return e / jnp.sum(e, axis=-1, keepdims=True)


def _band_specs(t):
    npair = N_HEADS // 2
    rows = BAND_QPS * BLK
    q_spec = pl.BlockSpec((rows, BLK), lambda p, i: (i, p))
    k_spec = pl.BlockSpec((t, BLK), lambda p, i: (0, npair + p))
    v_spec = pl.BlockSpec((t, BLK), lambda p, i: (0, 2 * npair + p))
    b_spec = pl.BlockSpec((2, BLK, BAND_W), lambda p, i: (p, 0, 0))
    return npair, t // rows, q_spec, k_spec, v_spec, b_spec


def _two_heads(xv, lane0, scale):
    xf = xv.astype(F32)
    if scale != 1.0:
        xf = xf * scale
    return jnp.concatenate([xf * lane0, xf * (1.0 - lane0)], axis=0).astype(BF16)


def _one_of_two_heads(r, lane0):
    return r[:BLK] * lane0 + r[BLK:] * (1.0 - lane0)


def band_fwd(qkv, bias, name):
    t = qkv.shape[0]
    assert t % (BAND_QPS * BLK) == 0
    npair, nsteps, q_spec, k_spec, v_spec, b_spec = _band_specs(t)

    def body(q_ref, k_ref, v_ref, b_ref, o_ref):
        step = pl.program_id(1)
        _, _, lane0, _ = _sb_consts()
        bias2 = b_ref[...].reshape(2 * BLK, BAND_W)
        for u in range(BAND_QPS):
            qb = step * BAND_QPS + u
            rows = slice(u * BLK, (u + 1) * BLK)
            q2 = _two_heads(q_ref[rows, :], lane0, QK_SCALE)
            p = _band_probs(q2, k_ref, bias2, qb)
            acc = jnp.zeros((2 * BLK, BLK), F32)
            for jj in range(BAND_BLOCKS):
                _, ks = _band_key_start(qb, jj)
                acc += jnp.dot(p[:, jj * BLK:(jj + 1) * BLK].astype(BF16), v_ref[pl.ds(ks, BLK), :],
                               preferred_element_type=F32)
            o_ref[rows, :] = _one_of_two_heads(acc, lane0).astype(BF16)

    return _call(
        body, name=name, out_shape=_sds((t, D_MODEL), BF16), grid=(npair, nsteps),
        in_specs=[q_spec, k_spec, v_spec, b_spec],
        out_specs=pl.BlockSpec((BAND_QPS * BLK, BLK), lambda p, i: (i, p)),
        compiler_params=_params("parallel", "parallel"),
    )(qkv, qkv, qkv, bias)


def band_bwd(qkv, do, bias, name, riders=None):
    t = qkv.shape[0]
    npair, nsteps, q_spec, k_spec, v_spec, b_spec = _band_specs(t)

    def body(q_ref, k_ref, v_ref, do_ref, b_ref, dq_ref, dk_ref, dv_ref, db_ref, dkacc, dvacc):
        step = pl.program_id(1)

        @pl.when(step == 0)
        def _():
            dkacc[...] = jnp.zeros_like(dkacc)
            dvacc[...] = jnp.zeros_like(dvacc)
            db_ref[...] = jnp.zeros_like(db_ref)

        _, _, lane0, _ = _sb_consts()
        bias2 = b_ref[...].reshape(2 * BLK, BAND_W)
        updates = []
        for u in range(BAND_QPS):
            qb = step * BAND_QPS + u
            rows = slice(u * BLK, (u + 1) * BLK)
            q2 = _two_heads(q_ref[rows, :], lane0, QK_SCALE)
            do2 = _two_heads(do_ref[rows, :], lane0, 1.0)
            p = _band_probs(q2, k_ref, bias2, qb)
            dp = jnp.concatenate(
                [lax.dot_general(do2, v_ref[pl.ds(_band_key_start(qb, jj)[1], BLK), :], NT,
                                 preferred_element_type=F32) for jj in range(BAND_BLOCKS)], axis=1)
            ds = p * (dp - jnp.sum(p * dp, axis=-1, keepdims=True))
            db_ref[...] += ds.reshape(2, BLK, BAND_W)
            dqa = jnp.zeros((2 * BLK, BLK), F32)
            for jj in range(BAND_BLOCKS):
                _, ks = _band_key_start(qb, jj)
                dsb = ds[:, jj * BLK:(jj + 1) * BLK].astype(BF16)
                pb = p[:, jj * BLK:(jj + 1) * BLK].astype(BF16)
                dqa += jnp.dot(dsb, k_ref[pl.ds(ks, BLK), :], preferred_element_type=F32)
                updates.append((ks, lax.dot_general(dsb, q2, TN, preferred_element_type=F32),
                                lax.dot_general(pb, do2, TN, preferred_element_type=F32)))
            dq_ref[rows, :] = (_one_of_two_heads(dqa, lane0) * QK_SCALE).astype(BF16)
        for ks, dk_part, dv_part in updates:
            dkacc[pl.ds(ks, BLK), :] += dk_part
            dvacc[pl.ds(ks, BLK), :] += dv_part

        @pl.when(step == nsteps - 1)
        def _():
            dk_ref[...] = dkacc[...].astype(BF16)
            dv_ref[...] = dvacc[...].astype(BF16)

    col_spec = pl.BlockSpec((t, BLK), lambda p, i: (0, p))
    blk_spec = pl.BlockSpec((BAND_QPS * BLK, BLK), lambda p, i: (i, p))
    res, ridden = _rider_call(
        body, (qkv, qkv, qkv, do, bias), riders, name=name,
        out_shape=[_sds((t, D_MODEL), BF16)] * 3 + [_sds((N_HEADS, BLK, BAND_W), F32)],
        grid=(npair, nsteps), in_specs=[q_spec, k_spec, v_spec, blk_spec, b_spec],
        out_specs=[blk_spec, col_spec, col_spec, b_spec],
        scratch_shapes=[pltpu.VMEM((t, BLK), F32), pltpu.VMEM((t, BLK), F32)],
        sem=("parallel", "arbitrary"))
    return res[0], res[1], res[2], res[3], ridden


def band_bias_window(rel_bias):
    far = BAND_W + BLK - 1 - 2 * REL_CLIP
    width = BAND_W + BLK
    ext = jnp.concatenate(
        [jnp.broadcast_to(rel_bias[:, 2 * REL_CLIP:], (N_HEADS, far)), rel_bias[:, 2 * REL_CLIP:0:-1],
         jnp.zeros((N_HEADS, 2), F32)], axis=1)
    tiled = jnp.broadcast_to(ext[:, None, :], (N_HEADS, BLK, width + 1)).reshape(N_HEADS, BLK * (width + 1))
    return tiled[:, BLK - 1:BLK - 1 + BLK * width].reshape(N_HEADS, BLK, width)[:, :, :BAND_W]


def band_bias_window_grad(dwin):
    width = BAND_W + BLK
    far = BAND_W + BLK - 1 - 2 * REL_CLIP
    flat = jnp.pad(dwin, ((0, 0), (0, 0), (0, BLK))).reshape(N_HEADS, BLK * width)
    skew = jnp.pad(flat, ((0, 0), (BLK - 1, 1))).reshape(N_HEADS, BLK, width + 1)
    dext = jnp.sum(skew, axis=1)[:, :width - 1]
    return jnp.concatenate(
        [jnp.zeros((N_HEADS, 1), F32), dext[:, :far - 1:-1][:, :2 * REL_CLIP - 1],
         dext[:, far:far + 1] + jnp.sum(dext[:, :far], axis=1, keepdims=True)], axis=1)


SG_GROUPS = 8


def _gelu_parts(x):
    inner = GELU_C0 * (x + GELU_C1 * (x * x * x))
    th = jnp.tanh(inner)
    return th, 0.5 * x * (1.0 + th)


def _sg_gate_mask():
    row = lax.broadcasted_iota(jnp.int32, (BLK, BLK), 0)
    col = lax.broadcasted_iota(jnp.int32, (BLK, BLK), 1)
    return (row // 64) >= (col // 64)


def _sg_forward_parts(a, lng):
    w = a.shape[1] // 2
    th, z = _gelu_parts(a)
    u, v = z[:, :w], z[:, w:]
    mu = jnp.mean(v, axis=-1, keepdims=True)
    xc = v - mu
    rstd = lax.rsqrt(jnp.mean(xc * xc, axis=-1, keepdims=True) + EPS)
    vhat = xc * rstd
    return th, u, vhat, rstd, vhat * lng


def sg_fwd(a, lng, ws, bias_t, name):
    t, w2 = a.shape
    w = w2 // 2
    gc = w // SG_GROUPS

    def body(a_ref, lng_ref, ws_ref, bt_ref, y_ref):
        _, u, _, _, vln = _sg_forward_parts(a_ref[...], lng_ref[...])
        mask = _sg_gate_mask()
        bt = bt_ref[...]
        lane = lax.broadcasted_iota(jnp.int32, (BLK, BLK), 1)
        for g in range(SG_GROUPS):
            wg = jnp.where(mask, ws_ref[g], 0.0).astype(BF16)
            sv = jnp.dot(wg, vln[:, g * gc:(g + 1) * gc].astype(BF16), preferred_element_type=F32)
            bg = jnp.sum(jnp.where(lane == g, bt, 0.0), axis=-1, keepdims=True)
            y_ref[:, g * gc:(g + 1) * gc] = (u[:, g * gc:(g + 1) * gc] * (sv + bg)).astype(BF16)

    return _call(
        body, name=name, out_shape=_sds((t, w), BF16), grid=(t // BLK,),
        in_specs=[pl.BlockSpec((BLK, w2), lambda i: (i, 0)), pl.BlockSpec((1, w), lambda i: (0, 0)),
                  pl.BlockSpec((SG_GROUPS, BLK, BLK), lambda i: (0, 0, 0)),
                  pl.BlockSpec((BLK, BLK), lambda i: (0, 0))],
        out_specs=pl.BlockSpec((BLK, w), lambda i: (i, 0)),
        compiler_params=_params("parallel"),
    )(a, lng, ws, bias_t)


def sg_bwd(a, dy, lng, ws, bias_t, name):
    t, w2 = a.shape
    w = w2 // 2
    gc = w // SG_GROUPS

    def body(a_ref, dy_ref, lng_ref, ws_ref, bt_ref, da_ref, dlng_ref, dws_ref, dbt_ref):
        @pl.when(pl.program_id(0) == 0)
        def _():
            dlng_ref[...] = jnp.zeros_like(dlng_ref)
            dws_ref[...] = jnp.zeros_like(dws_ref)
            dbt_ref[...] = jnp.zeros_like(dbt_ref)

        av, lng = a_ref[...], lng_ref[...]
        th, u, vhat, rstd, vln = _sg_forward_parts(av, lng)
        mask = _sg_gate_mask()
        bt = bt_ref[...]
        lane = lax.broadcasted_iota(jnp.int32, (BLK, BLK), 1)
        dyv = dy_ref[...]
        du_parts, dvln_parts = [], []
        dbt = jnp.zeros((BLK, BLK), F32)
        for g in range(SG_GROUPS):
            sl = slice(g * gc, (g + 1) * gc)
            wg = jnp.where(mask, ws_ref[g], 0.0).astype(BF16)
            vg = vln[:, sl].astype(BF16)
            sv = jnp.dot(wg, vg, preferred_element_type=F32)
            bg = jnp.sum(jnp.where(lane == g, bt, 0.0), axis=-1, keepdims=True)
            dyg = dyv[:, sl]
            du_parts.append(dyg * (sv + bg))
            dsv = dyg * u[:, sl]
            dbt += jnp.where(lane == g, jnp.sum(dsv, axis=-1, keepdims=True), 0.0)
            dsvb = dsv.astype(BF16)
            dws_ref[g] += jnp.where(mask, lax.dot_general(dsvb, vg, NT, preferred_element_type=F32), 0.0)
            dvln_parts.append(lax.dot_general(wg, dsvb, TN, preferred_element_type=F32))
        dbt_ref[...] += dbt
        du = jnp.concatenate(du_parts, axis=1)
        dvln = jnp.concatenate(dvln_parts, axis=1)
        dlng_ref[...] += _colsum8(dvln * vhat)
        dvhat = dvln * lng
        dv = rstd * (dvhat - jnp.mean(dvhat, axis=-1, keepdims=True)
                     - vhat * jnp.mean(dvhat * vhat, axis=-1, keepdims=True))
        dz = jnp.concatenate([du, dv], axis=1)
        dgelu = 0.5 * (1.0 + th) + (0.5 * av) * (1.0 - th * th) * (GELU_C0 * (1.0 + 3.0 * GELU_C1 * (av * av)))
        da_ref[...] = (dz * dgelu).astype(BF16)

    return _call(
        body, name=name,
        out_shape=[_sds((t, w2), BF16), _sds((8, w), F32), _sds((SG_GROUPS, BLK, BLK), F32), _sds((BLK, BLK), F32)],
        grid=(t // BLK,),
        in_specs=[pl.BlockSpec((BLK, w2), lambda i: (i, 0)), pl.BlockSpec((BLK, w), lambda i: (i, 0)),
                  pl.BlockSpec((1, w), lambda i: (0, 0)),
                  pl.BlockSpec((SG_GROUPS, BLK, BLK), lambda i: (0, 0, 0)),
                  pl.BlockSpec((BLK, BLK), lambda i: (0, 0))],
        out_specs=[pl.BlockSpec((BLK, w2), lambda i: (i, 0)), pl.BlockSpec((8, w), lambda i: (0, 0)),
                   pl.BlockSpec((SG_GROUPS, BLK, BLK), lambda i: (0, 0, 0)),
                   pl.BlockSpec((BLK, BLK), lambda i: (0, 0))],
        compiler_params=_params("arbitrary"),
    )(a, dy, lng, ws, bias_t)


def _shift_down(cat, n, tr):
    return pltpu.roll(cat, n, 0)[8:8 + tr]


def _shift_up(cat, n, tr):
    return pltpu.roll(cat, tr + 8 - n, 0)[0:tr]


def conv_fwd(p, cw, name):
    t, d3 = p.shape
    d = d3 // 3
    tr = min(256, t)
    hb = tr // 8

    def body(p_ref, ph_ref, cw_ref, o_ref):
        i = pl.program_id(0)
        pv = p_ref[...]
        y = pv[:, d:2 * d] * pv[:, 2 * d:]
        ph = ph_ref[...]
        yh = jnp.where(i > 0, ph[:, d:2 * d] * ph[:, 2 * d:], 0.0)
        cat = jnp.concatenate([yh, y], axis=0)
        yc = (cw_ref[0:1, :] * _shift_down(cat, 2, tr) + cw_ref[1:2, :] * _shift_down(cat, 1, tr)
              + cw_ref[2:3, :] * y)
        o_ref[...] = (pv[:, :d] * yc).astype(BF16)

    return _call(
        body, name=name, out_shape=_sds((t, d), BF16), grid=(t // tr,),
        in_specs=[pl.BlockSpec((tr, d3), lambda i: (i, 0)),
                  pl.BlockSpec((8, d3), lambda i: (jnp.maximum(i * hb - 1, 0), 0)),
                  pl.BlockSpec((8, d), lambda i: (0, 0))],
        out_specs=pl.BlockSpec((tr, d), lambda i: (i, 0)),
        compiler_params=_params("parallel"),
    )(p, p, cw)


def conv_bwd(p, dz, cw, name):
    t, d3 = p.shape
    d = d3 // 3
    tr = min(256, t)
    hb = tr // 8
    nt = t // tr

    def body(p_ref, ph_ref, pn_ref, dz_ref, dzn_ref, cw_ref, dp_ref, dcw_ref):
        i = pl.program_id(0)

        @pl.when(i == 0)
        def _():
            dcw_ref[...] = jnp.zeros_like(dcw_ref)

        pv = p_ref[...]
        gb, gcv, xt = pv[:, :d], pv[:, d:2 * d], pv[:, 2 * d:]
        y = gcv * xt
        ph = ph_ref[...]
        yh = jnp.where(i > 0, ph[:, d:2 * d] * ph[:, 2 * d:], 0.0)
        cat = jnp.concatenate([yh, y], axis=0)
        y2, y1 = _shift_down(cat, 2, tr), _shift_down(cat, 1, tr)
        w0, w1, w2 = cw_ref[0:1, :], cw_ref[1:2, :], cw_ref[2:3, :]
        yc = w0 * y2 + w1 * y1 + w2 * y
        dzv = dz_ref[...]
        dyc = dzv * gb
        dcw_ref[0] += _colsum8(dyc * y2)
        dcw_ref[1] += _colsum8(dyc * y1)
        dcw_ref[2] += _colsum8(dyc * y)
        dycn = jnp.where(i < nt - 1, dzn_ref[...] * pn_ref[...][:, :d], 0.0)
        catn = jnp.concatenate([dyc, dycn], axis=0)
        dy = w2 * dyc + w1 * _shift_up(catn, 1, tr) + w0 * _shift_up(catn, 2, tr)
        dp_ref[:, :d] = (dzv * yc).astype(BF16)
        dp_ref[:, d:2 * d] = (dy * xt).astype(BF16)
        dp_ref[:, 2 * d:] = (dy * gcv).astype(BF16)

    nxt = lambda i: (jnp.minimum((i + 1) * hb, t // 8 - 1), 0)
    return _call(
        body, name=name, out_shape=[_sds((t, d3), BF16), _sds((3, 8, d), F32)], grid=(nt,),
        in_specs=[pl.BlockSpec((tr, d3), lambda i: (i, 0)),
                  pl.BlockSpec((8, d3), lambda i: (jnp.maximum(i * hb - 1, 0), 0)),
                  pl.BlockSpec((8, d3), nxt),
                  pl.BlockSpec((tr, d), lambda i: (i, 0)),
                  pl.BlockSpec((8, d), nxt),
                  pl.BlockSpec((8, d), lambda i: (0, 0))],
        out_specs=[pl.BlockSpec((tr, d3), lambda i: (i, 0)), pl.BlockSpec((3, 8, d), lambda i: (0, 0, 0))],
        compiler_params=_params("arbitrary"),
    )(p, p, p, dz, dz, cw)


def ada_fwd(c_all, w, b, name):
    nl, d, n = w.shape

    def body(c_ref, w_ref, b_ref, o_ref):
        cv = c_ref[...]
        s = (cv * _sigmoid(cv)).astype(BF16)
        o_ref[...] = jnp.dot(s, w_ref[...].astype(BF16), preferred_element_type=F32) + b_ref[...]

    return _call(
        body, name=name, out_shape=_sds((nl, N_DEV, n), F32), grid=(nl,),
        in_specs=[pl.BlockSpec((N_DEV, d), lambda l: (0, 0)), pl.BlockSpec((None, d, n), lambda l: (l, 0, 0)),
                  pl.BlockSpec((None, 1, n), lambda l: (l, 0, 0))],
        out_specs=pl.BlockSpec((None, N_DEV, n), lambda l: (l, 0, 0)),
        compiler_params=_params("parallel"),
    )(c_all, w, b)


def ada_bwd(c_all, dmod, name):
    nl, _, n = dmod.shape
    d = c_all.shape[1]

    def body(c_ref, dm_ref, o_ref):
        cv = c_ref[...]
        s = (cv * _sigmoid(cv)).astype(BF16)
        o_ref[...] = lax.dot_general(s, dm_ref[...].astype(BF16), TN, preferred_element_type=F32)

    return _call(
        body, name=name, out_shape=_sds((nl, d, n), F32), grid=(nl,),
        in_specs=[pl.BlockSpec((N_DEV, d), lambda l: (0, 0)), pl.BlockSpec((None, N_DEV, n), lambda l: (l, 0, 0))],
        out_specs=pl.BlockSpec((None, d, n), lambda l: (l, 0, 0)),
        compiler_params=_params("parallel"),
    )(c_all, dmod)


def adamw(pieces, w, m, v, name, riders=None):
    nl = len(pieces)
    npc, r, c = pieces[0].shape
    tr = r
    for cand in (1024, 512, 256, 128, 64, 32, 16, 8):
        if r % cand == 0 and cand * c * 4 <= (1 << 20):
            tr = cand
            break
    nt = r // tr

    def update(p_ref, w_ref, m_ref, v_ref, g_ref, d_ref, nm_ref, nv_ref):
        g = p_ref[0].astype(F32)
        for i in range(1, npc):
            g = g + p_ref[i].astype(F32)
        wv = w_ref[...]
        nm = ADAM_B1 * m_ref[...] + (1.0 - ADAM_B1) * g
        nv = ADAM_B2 * v_ref[...] + (1.0 - ADAM_B2) * (g * g)
        m_hat = nm / (1.0 - ADAM_B1 ** ADAM_STEP)
        v_hat = nv / (1.0 - ADAM_B2 ** ADAM_STEP)
        g_ref[...] = g
        d_ref[...] = -ADAM_LR * (m_hat / (jnp.sqrt(v_hat) + ADAM_EPS) + ADAM_WD * wv)
        nm_ref[...] = nm
        nv_ref[...] = nv

    def body(*refs):
        if nl == 1:
            update(*refs)
        else:
            for j in range(nl):
                pl.when(pl.program_id(0) == j)(lambda j=j: update(refs[j], *refs[nl:]))

    row = pl.BlockSpec((tr, c), lambda l, i: (l * nt + i, 0))
    piece_specs = [pl.BlockSpec((npc, tr, c), lambda l, i, j=j: (0, jnp.where(l == j, i, 0), 0))
                   for j in range(nl)]
    res, ridden = _rider_call(
        body, (*pieces, w, m, v), riders, name=name, out_shape=[_sds((nl * r, c), F32)] * 4, grid=(nl, nt),
        in_specs=piece_specs + [row, row, row], out_specs=[row] * 4, scratch_shapes=[],
        sem=("parallel", "parallel"))
    return res, ridden


def sum_pieces(pieces, name):
    npc, r, c = pieces.shape

    def body(p_ref, o_ref):
        g = p_ref[0]
        for i in range(1, npc):
            g = g + p_ref[i]
        o_ref[...] = g

    return _call(body, name=name, out_shape=_sds((r, c), F32),
                 in_specs=[pl.BlockSpec(memory_space=pltpu.VMEM)],
                 out_specs=pl.BlockSpec(memory_space=pltpu.VMEM),
                 compiler_params=pltpu.CompilerParams(vmem_limit_bytes=VMEM_LIMIT))(pieces)


PACK_W = 1024


def _pack(arrs):
    flat = jnp.concatenate([a.reshape(-1).astype(F32) for a in arrs])
    rows = -(-flat.shape[0] // (8 * PACK_W)) * 8
    return jnp.pad(flat, (0, rows * PACK_W - flat.shape[0])).reshape(rows, PACK_W)


def _unpack(slab, shapes):
    flat = slab.reshape(-1)
    out, off = [], 0
    for s in shapes:
        n = 1
        for q in s:
            n *= q
        out.append(flat[off:off + n].reshape(s))
        off += n
    return out


def kernel(x, c, ada_w, ada_b, norm_g, ffn_w_in, ffn_w_out, sb_w_qkv, sb_w_o, sg_w_in, sg_ln_g, sg_w_s, sg_bias, sg_w_out, sc_w_in, sc_conv_w, sc_w_out, cb_w_qkv, cb_rel_bias, cb_w_o, loss_target, m_ada_w, m_ada_b, m_norm_g, m_ffn_w_in, m_ffn_w_out, m_sb_w_qkv, m_sb_w_o, m_sg_w_in, m_sg_ln_g, m_sg_w_s, m_sg_bias, m_sg_w_out, m_sc_w_in, m_sc_conv_w, m_sc_w_out, m_cb_w_qkv, m_cb_rel_bias, m_cb_w_o, v_ada_w, v_ada_b, v_norm_g, v_ffn_w_in, v_ffn_w_out, v_sb_w_qkv, v_sb_w_o, v_sg_w_in, v_sg_ln_g, v_sg_w_s, v_sg_bias, v_sg_w_out, v_sc_w_in, v_sc_conv_w, v_sc_w_out, v_cb_w_qkv, v_cb_rel_bias, v_cb_w_o):
    depth = ada_w.shape[0]
    d = D_MODEL
    xi, yi, ci = lax.axis_index("x"), lax.axis_index("y"), lax.axis_index("c")
    me = 4 * xi + 2 * yi + ci
    x0 = x[0]
    t = x0.shape[0]
    target = loss_target[0]

    c_g, ng, small, w_qkv0 = _all_gather(
        [jnp.pad(c, ((0, 7), (0, 0))), norm_g.reshape(depth * 4, d // N_DEV), _pack([sg_ln_g, sc_conv_w]),
         sb_w_qkv[0].astype(BF16)], "gather_setup")

    c_all = c_g[:, 0, :]
    na = ada_w.shape[2]
    b_cols = lax.dynamic_slice_in_dim(ada_b, me * na, na, axis=1)[:, None, :]
    mod_part = ada_fwd(c_all, ada_w, b_cols, "ada_fwd")
    mod_g = _all_gather([mod_part.reshape(depth * N_DEV, na)], "gather_mod")[0]
    mod_g = mod_g.reshape(N_DEV, depth, N_DEV, na)
    mod_me = lax.dynamic_index_in_dim(mod_g, me, axis=2, keepdims=False)
    mod = jnp.transpose(mod_me, (1, 0, 2)).reshape(depth, 6, 1, d)

    norm_full = jnp.transpose(ng, (1, 0, 2)).reshape(depth, 4, 1, d)
    small = small.reshape(N_DEV, -1)
    nl_g = sg_ln_g.shape[1]
    ln_full = small[:, :nl_g].reshape(1, N_DEV * nl_g)
    cwn = sc_conv_w.shape[2]
    cw_sh = small[:, nl_g:nl_g + 3 * cwn].reshape(N_DEV, 3, cwn)
    cw_full = jnp.transpose(cw_sh, (1, 0, 2)).reshape(3, d)
    cw_pad = jnp.pad(cw_full, ((0, 5), (0, 0)))

    bf = lambda a: a.astype(BF16)
    mixers = [
        [bf(sb_w_qkv[0]), bf(sb_w_o[0])],
        [bf(sg_w_in[0]), bf(sg_w_out[0])],
        [bf(sc_w_in[0]), bf(sc_w_out[0])],
        [bf(cb_w_qkv[0]), bf(cb_w_o[0])],
    ]
    shards = [[bf(ffn_w_in[i]), bf(ffn_w_out[i])] + mixers[i % 4] for i in range(depth)]
    gathered = [[None] * 4 for _ in range(depth)]
    gathered[0][2] = w_qkv0
    riding_shards = [shards[0][0], shards[0][1], shards[0][3]] + shards[1]

    bias_win = band_bias_window(cb_rel_bias[0])
    ws = sg_w_s[0]
    bias_t = jnp.pad(sg_bias[0].T, ((0, 0), (0, BLK - SG_GROUPS)))

    saved = []
    xcur = x0
    h = pre_fwd(x0, norm_full[0, 0], mod[0, 0], mod[0, 1], "L0_pre_m")
    for i in range(depth):
        mi = i % 4
        sh_m, sc_m, gt_m, sh_f, sc_f, gt_f = [mod[i, j] for j in range(6)]
        g0, g1, g2, g3 = [norm_full[i, j] for j in range(4)]
        tag = "L%d_" % i
        sv = {"x_in": xcur, "h_m": h}
        nxt = shards[i + 1] if 0 < i < depth - 1 else None
        if mi == 0:
            qkv = mm_cs(h, gathered[0][2], tag + "qkv", out_dtype=BF16)
            qkv_t = jnp.transpose(qkv.reshape(t // SB_GW, SB_GW, 3 * d), (0, 2, 1))
            o, cmass, riding = sb_fwd(qkv, qkv_t, riding_shards, tag + "sb_fwd")
            gathered[0][0], gathered[0][1], gathered[0][3] = riding[:3]
            gathered[1] = riding[3:]
            sv.update(qkv=qkv, qkv_t=qkv_t, o=o, cmass=cmass)
            mixed = o
        else:
            w_in = gathered[i][2]
            out_dtype = BF16 if mi == 3 else F32
            if nxt is not None:
                pre, (gathered[i + 1][0],) = mm_cs(h, w_in, tag + "mix_in", out_dtype=out_dtype,
                                                   riders=(_Gather, [nxt[0]]))
            else:
                pre = mm_cs(h, w_in, tag + "mix_in", out_dtype=out_dtype)
            if mi == 1:
                mixed = sg_fwd(pre, ln_full, ws, bias_t, tag + "sg_fwd")
                sv.update(a=pre, yy=mixed)
            elif mi == 2:
                mixed = conv_fwd(pre, cw_pad, tag + "conv_fwd")
                sv.update(p=pre, gz=mixed)
            else:
                mixed = band_fwd(pre, bias_win, tag + "band_fwd")
                sv.update(qkv=pre, o=mixed)
        wfi, wfo, _, wmo = gathered[i]
        wfo4 = wfo.reshape(4, -1, d)
        y = mm(mixed, wmo.reshape(-1, d), tag + "mix_out")
        sv["y_m"] = y
        xmid, h2 = post_pre_fwd(xcur, y, g1, gt_m, g2, sh_f, sc_f, tag + "post_m_pre_f")
        sv["x_mid"] = xmid
        if nxt is not None:
            ag, au, s3, (gathered[i + 1][1],) = ffn_in_swiglu(h2, wfi, tag + "ffn_in", riders=(_Gather, nxt[1:2]))
            y2, gathered[i + 1][2:] = mm_rs(s3, wfo4, tag + "ffn_out", riders=(_Gather, nxt[2:]))
        else:
            ag, au, s3, _ = ffn_in_swiglu(h2, wfi, tag + "ffn_in")
            y2 = mm_rs(s3, wfo4, tag + "ffn_out")
        sv.update(h_f=h2, ag=ag, au=au, s3=s3, y_f=y2)
        saved.append(sv)
        if i + 1 < depth:
            xcur, h = post_pre_fwd(xmid, y2, g3, gt_f, norm_full[i + 1, 0], mod[i + 1, 0], mod[i + 1, 1],
                                   tag + "post_f_pre_m")

    last = depth - 1
    dx, dy2, lpart, dgt_f, dg3 = post_loss_bwd(saved[last]["x_mid"], saved[last]["y_f"], norm_full[last, 3],
                                               mod[last, 5], target, "loss")
    loss = lax.psum(0.5 * jnp.sum(lpart) / d, ("x", "y", "c"))

    dmod_rows = [None] * depth
    dnorm_rows = [None] * depth
    big_pieces = [[None] * 4 for _ in range(depth)]
    pending_dwmi, pending_dwfi, pending_layers = None, [], []
    small_grads = {}
    for i in reversed(range(depth)):
        wfi, wfo, wmi, wmo = gathered[i]
        wfo4 = wfo.reshape(4, -1, d)
        wmo2 = wmo.reshape(-1, d)
        mi = i % 4
        sh_m, sc_m, gt_m, sh_f, sc_f, gt_f = [mod[i, j] for j in range(6)]
        g0, g1, g2, g3 = [norm_full[i, j] for j in range(4)]
        tag = "L%d_b_" % i
        sv = saved[i]
        da3, _ = ffn_out_dx_swiglu(dy2, wfo4, sv["ag"], sv["au"], tag + "ffn_out_dx")
        dwfo = mm_rs_dw(sv["s3"], dy2, tag + "ffn_out_dw", out_dtype=BF16).reshape(N_DEV, -1, d)
        dh2, (big_pieces[i][1],) = mm_cs_dx(da3, wfi, tag + "ffn_in_dx", act_major=True, riders=(_Exchange, [dwfo]))
        if pending_dwmi is not None:
            dwfi, (big_pieces[i + 1][2],) = mm_cs_dw(sv["h_f"], da3, tag + "ffn_in_dw", act_major=True,
                                                     out_dtype=BF16, riders=(_Exchange, [pending_dwmi]))
        else:
            dwfi = mm_cs_dw(sv["h_f"], da3, tag + "ffn_in_dw", act_major=True, out_dtype=BF16)
        dx, dy, dsh_f, dsc_f, dg2, dgt_m, dg1 = pre_post_bwd(
            dh2, sv["x_mid"], g2, sc_f, dx, sv["y_m"], g1, gt_m, tag + "pre_f_post_m")
        if mi == 0:
            do = mm_nt(dy, wmo2, tag + "wo_dx", out_dtype=BF16)
            dwmo = mm_tn(sv["o"], dy, tag + "wo_dw", out_dtype=BF16).reshape(N_DEV, -1, d)
            dq, dk, dv, ridden = sb_bwd(sv["qkv"], sv["qkv_t"], do, sv["cmass"],
                                        pending_dwfi + [dwfi, dwmo], tag + "sb_bwd")
            for n, j in enumerate(pending_layers):
                big_pieces[j][0] = ridden[n]
            big_pieces[0][0], big_pieces[0][3] = ridden[-2:]
            dmid = jnp.concatenate([dq, dk, dv], axis=1)
        elif mi == 1:
            dyy = mm_nt(dy, wmo2, tag + "sg_out_dx")
            dwmo = mm_tn(sv["yy"], dy, tag + "sg_out_dw", out_dtype=BF16).reshape(N_DEV, -1, d)
            dmid, dlng, dws, dbt = sg_bwd(sv["a"], dyy, ln_full, ws, bias_t, tag + "sg_bwd")
            small_grads.update(ln_g=jnp.sum(dlng, axis=0), w_s=dws, bias=dbt[:, :SG_GROUPS].T)
        elif mi == 2:
            dgz = mm_nt(dy, wmo2, tag + "sc_out_dx")
            dwmo = mm_tn(sv["gz"], dy, tag + "sc_out_dw", out_dtype=BF16).reshape(N_DEV, -1, d)
            dmid, dcw = conv_bwd(sv["p"], dgz, cw_pad, tag + "conv_bwd")
            small_grads.update(conv_w=jnp.sum(dcw, axis=1))
        else:
            do = mm_nt(dy, wmo2, tag + "wo_dx", out_dtype=BF16)
            dwmo = mm_tn(sv["o"], dy, tag + "wo_dw", out_dtype=BF16).reshape(N_DEV, -1, d)
            dq, dk, dv, dwin, (big_pieces[i][0], big_pieces[i][3]) = band_bwd(
                sv["qkv"], do, bias_win, tag + "band_bwd", riders=(_Exchange, [dwfi, dwmo]))
            dmid = jnp.concatenate([dq, dk, dv], axis=1)
            small_grads.update(rel_bias=band_bias_window_grad(dwin))
        if mi in (1, 2):
            dh, (big_pieces[i][3],) = mm_cs_dx(dmid, wmi, tag + "mix_in_dx", riders=(_Exchange, [dwmo]))
            pending_dwfi.append(dwfi)
            pending_layers.append(i)
        else:
            dh = mm_cs_dx(dmid, wmi, tag + "mix_in_dx")
        pending_dwmi = mm_cs_dw(sv["h_m"], dmid, tag + "mix_in_dw", out_dtype=BF16)
        if i > 0:
            dx, dy2_prev, dsh_m, dsc_m, dg0, dgt_f_prev, dg3_prev = pre_post_bwd(
                dh, sv["x_in"], g0, sc_m, dx, saved[i - 1]["y_f"], norm_full[i - 1, 3], mod[i - 1, 5],
                tag + "pre_m_post_f")
        else:
            dx, dsh_m, dsc_m, dg0 = pre_bwd(dh, sv["x_in"], g0, sc_m, dx, tag + "pre_m")
        dmod_rows[i] = jnp.stack([jnp.sum(q, axis=0) for q in (dsh_m, dsc_m, dgt_m, dsh_f, dsc_f, dgt_f)])
        dnorm_rows[i] = jnp.stack([jnp.sum(q, axis=0) for q in (dg0, dg1, dg2, dg3)])
        if i > 0:
            dy2, dgt_f, dg3 = dy2_prev, dgt_f_prev, dg3_prev
    grad_x = dx[None]

    out_g, out_d, out_m, out_v = {}, {}, {}, {}

    def upd(name, pieces, w, m, v, riders=None):
        rows_cols = (len(pieces) * pieces[0].shape[1], pieces[0].shape[2])
        res, ridden = adamw(pieces, w.reshape(rows_cols), m.reshape(rows_cols), v.reshape(rows_cols),
                            "adamw_" + name, riders)
        out_g[name], out_d[name], out_m[name], out_v[name] = [q.reshape(w.shape) for q in res]
        return ridden

    dmod_mine = jnp.stack(dmod_rows).reshape(depth, 6 * d)
    dnorm_mine = jnp.stack(dnorm_rows)
    small_list = [dnorm_mine, small_grads["ln_g"], small_grads["w_s"], small_grads["bias"],
                  small_grads["conv_w"], small_grads["rel_bias"]]
    small_shapes = [dmod_mine.shape] + [a.shape for a in small_list]
    slab = _pack([dmod_mine] + small_list)
    big_pieces[0][2], = upd("ffn_w_in", [big_pieces[i][0] for i in range(depth)], ffn_w_in, m_ffn_w_in, v_ffn_w_in,
                            riders=(_Exchange, [pending_dwmi]))
    slab_g, = upd("ffn_w_out", [big_pieces[i][1] for i in range(depth)], ffn_w_out, m_ffn_w_out, v_ffn_w_out,
                  riders=(_Gather, [slab]))
    upd("sb_w_qkv", [big_pieces[0][2]], sb_w_qkv, m_sb_w_qkv, v_sb_w_qkv)
    upd("sb_w_o", [big_pieces[0][3]], sb_w_o, m_sb_w_o, v_sb_w_o)
    upd("sg_w_in", [big_pieces[1][2]], sg_w_in, m_sg_w_in, v_sg_w_in)
    upd("sg_w_out", [big_pieces[1][3]], sg_w_out, m_sg_w_out, v_sg_w_out)
    upd("sc_w_in", [big_pieces[2][2]], sc_w_in, m_sc_w_in, v_sc_w_in)
    upd("sc_w_out", [big_pieces[2][3]], sc_w_out, m_sc_w_out, v_sc_w_out)
    upd("cb_w_qkv", [big_pieces[3][2]], cb_w_qkv, m_cb_w_qkv, v_cb_w_qkv)
    upd("cb_w_o", [big_pieces[3][3]], cb_w_o, m_cb_w_o, v_cb_w_o)

    tot = sum_pieces(slab_g, "sum_small_grads")
    g_ada_b_full, g_norm, g_ln, g_ws, g_sbias, g_cw, g_rb = _unpack(tot, small_shapes)
    dmod_all = slab_g.reshape(N_DEV, -1)[:, :depth * 6 * d].reshape(N_DEV, depth, 6 * d)
    dmod_cols = lax.dynamic_slice_in_dim(dmod_all, me * na, na, axis=2)
    g_ada_w = ada_bwd(c_all, jnp.transpose(dmod_cols, (1, 0, 2)), "ada_bwd")

    nsh = d // N_DEV
    g_norm_sh = lax.dynamic_slice_in_dim(g_norm, me * nsh, nsh, axis=2)
    g_ln_sh = lax.dynamic_slice_in_dim(g_ln.reshape(1, -1), me * nl_g, nl_g, axis=1)
    g_cw_sh = lax.dynamic_slice_in_dim(g_cw, me * cwn, cwn, axis=1)[None]

    upd("ada_w", [g_ada_w.reshape(1, depth * d, na)], ada_w, m_ada_w, v_ada_w)

    small_names = ["ada_b", "norm_g", "sg_ln_g", "sg_w_s", "sg_bias", "sc_conv_w", "cb_rel_bias"]
    small_g = [g_ada_b_full, g_norm_sh, g_ln_sh, g_ws[None], g_sbias[None], g_cw_sh, g_rb[None]]
    small_w = [ada_b, norm_g, sg_ln_g, sg_w_s, sg_bias, sc_conv_w, cb_rel_bias]
    small_m = [m_ada_b, m_norm_g, m_sg_ln_g, m_sg_w_s, m_sg_bias, m_sc_conv_w, m_cb_rel_bias]
    small_v = [v_ada_b, v_norm_g, v_sg_ln_g, v_sg_w_s, v_sg_bias, v_sc_conv_w, v_cb_rel_bias]
    shapes = [w.shape for w in small_w]
    res, _ = adamw([_pack(small_g)[None]], _pack(small_w), _pack(small_m), _pack(small_v), "adamw_small")
    for nm_, gs, ds_, ms, vs in zip(small_names, *[_unpack(r, shapes) for r in res]):
        out_g[nm_], out_d[nm_], out_m[nm_], out_v[nm_] = gs, ds_, ms, vs

    order = ["ada_w", "ada_b", "norm_g", "ffn_w_in", "ffn_w_out", "sb_w_qkv", "sb_w_o", "sg_w_in", "sg_ln_g",
             "sg_w_s", "sg_bias", "sg_w_out", "sc_w_in", "sc_conv_w", "sc_w_out", "cb_w_qkv", "cb_rel_bias", "cb_w_o"]
    return (loss, grad_x, *[out_g[n] for n in order], *[out_d[n] for n in order],
            *[out_m[n] for n in order], *[out_v[n] for n in order])
```

```python
import jax
import jax.numpy as jnp
from jax import lax
from jax.experimental import pallas as pl
from jax.experimental.pallas import tpu as pltpu

F32 = jnp.float32
BF16 = jnp.bfloat16
MESH = pl.DeviceIdType.MESH

N_DEV = 8
D_MODEL = 1024
N_HEADS = 16
HEAD_DIM = 64
QK_SCALE = HEAD_DIM ** -0.5
BLK = 128
BAND_BLOCKS = 5
BAND_W = BAND_BLOCKS * BLK
REL_CLIP = 128
EPS = 1e-6
NEG = -1e30
GELU_C0 = 0.7978845608028654
GELU_C1 = 0.044715
ADAM_LR = 0.001
ADAM_B1 = 0.9
ADAM_B2 = 0.999
ADAM_EPS = 1e-08
ADAM_WD = 0.01
ADAM_STEP = 10
VMEM_LIMIT = 56 * 1024 * 1024


def _call(body, **kw):
    return pl.pallas_call(body, **kw)


def _params(*sem):
    return pltpu.CompilerParams(dimension_semantics=sem, vmem_limit_bytes=VMEM_LIMIT)


def _sds(shape, dtype):
    return jax.ShapeDtypeStruct(tuple(shape), dtype)


def _row_tile(t):
    return min(512, t)


def _me():
    x, y, c = lax.axis_index("x"), lax.axis_index("y"), lax.axis_index("c")
    return x, y, c


def _all_gather(arrs, name):
    n = len(arrs)

    def body(*refs):
        gather = _Gather(refs[:n], refs[n:2 * n], *refs[2 * n:])
        gather.start()
        gather.forward()
        gather.finish()

    any_spec = pl.BlockSpec(memory_space=pl.ANY)
    outs = _call(
        body,
        name=name,
        out_shape=_Gather.out_shapes(arrs),
        in_specs=[any_spec] * n,
        out_specs=[any_spec] * n,
        scratch_shapes=_comm_sems(n),
    )(*arrs)
    return list(outs)


def _comm_sems(n):
    if n == 0:
        return []
    return [pltpu.SemaphoreType.DMA((n, 7)), pltpu.SemaphoreType.DMA((n, 7)), pltpu.SemaphoreType.DMA((n,))]


class _Gather:
    def __init__(self, x_refs, o_refs, send_sems, recv_sems, local_sems):
        self.x_refs, self.o_refs = x_refs, o_refs
        self.send_sems, self.recv_sems, self.local_sems = send_sems, recv_sems, local_sems
        x, y, c = _me()
        self.c = c
        self.me, self.sibling = (x, y, c), (x, y, 1 - c)
        self.chips = [(1 - x, y), (x, 1 - y), (1 - x, 1 - y)]

    @staticmethod
    def out_shapes(arrs):
        return [_sds((N_DEV,) + a.shape, a.dtype) for a in arrs]

    def rows(self, a, block):
        px, py, pc = block
        return self.o_refs[a].at[4 * px + 2 * py + pc]

    def copy(self, a, k, block, to, own=False):
        return pltpu.make_async_remote_copy(
            src_ref=self.x_refs[a] if own else self.rows(a, block),
            dst_ref=self.rows(a, block),
            send_sem=self.send_sems.at[a, k],
            recv_sem=self.recv_sems.at[a, k],
            device_id=to,
            device_id_type=MESH,
        )

    def local(self, a):
        return pltpu.make_async_copy(self.x_refs[a], self.rows(a, self.me), self.local_sems.at[a])

    def first(self, a):
        cps = [self.copy(a, 0, self.me, self.sibling, own=True)]
        return cps + [self.copy(a, 1 + j, self.me, (*chip, self.c), own=True) for j, chip in enumerate(self.chips)]

    def passed(self, a):
        return [self.copy(a, 4 + j, (*chip, self.c), self.sibling) for j, chip in enumerate(self.chips)]

    def start(self):
        for a in range(len(self.x_refs)):
            self.local(a).start()
            for cp in self.first(a):
                cp.start()

    def forward(self):
        for a in range(len(self.x_refs)):
            passed = self.passed(a)
            for j, chip in enumerate(self.chips):
                self.copy(a, 1 + j, (*chip, self.c), self.me).wait_recv()
                passed[j].start()

    def finish(self):
        for a in range(len(self.x_refs)):
            self.copy(a, 0, self.sibling, self.me).wait_recv()
            for j, chip in enumerate(self.chips):
                self.copy(a, 4 + j, (*chip, 1 - self.c), self.me).wait_recv()
            for cp in self.first(a) + self.passed(a):
                cp.wait_send()
            self.local(a).wait()


class _Exchange:
    def __init__(self, x_refs, o_refs, send_sems, recv_sems, local_sems):
        self.x_refs, self.o_refs = x_refs, o_refs
        self.send_sems, self.recv_sems, self.local_sems = send_sems, recv_sems, local_sems
        x, y, c = _me()
        self.me = 4 * x + 2 * y + c
        self.peers = []
        for k in range(1, N_DEV):
            px = 1 - x if k & 4 else x
            py = 1 - y if k & 2 else y
            pc = 1 - c if k & 1 else c
            self.peers.append((px, py, pc))

    @staticmethod
    def out_shapes(arrs):
        return [_sds(a.shape, a.dtype) for a in arrs]

    def local(self, a):
        return pltpu.make_async_copy(self.x_refs[a].at[self.me], self.o_refs[a].at[self.me], self.local_sems.at[a])

    def copy(self, a, k, send):
        px, py, pc = self.peers[k]
        peer = 4 * px + 2 * py + pc
        return pltpu.make_async_remote_copy(
            src_ref=self.x_refs[a].at[peer],
            dst_ref=self.o_refs[a].at[self.me if send else peer],
            send_sem=self.send_sems.at[a, k], recv_sem=self.recv_sems.at[a, k],
            device_id=(px, py, pc), device_id_type=MESH)

    def start(self):
        for a in range(len(self.x_refs)):
            self.local(a).start()
            for k in range(N_DEV - 1):
                self.copy(a, k, True).start()

    def finish(self):
        for a in range(len(self.x_refs)):
            for k in range(N_DEV - 1):
                self.copy(a, k, False).wait_recv()
            for k in range(N_DEV - 1):
                self.copy(a, k, True).wait_send()
            self.local(a).wait()


def _all_to_all(arrs, name):
    n = len(arrs)

    def body(*refs):
        exchange = _Exchange(refs[:n], refs[n:2 * n], *refs[2 * n:])
        exchange.start()
        exchange.finish()

    any_spec = pl.BlockSpec(memory_space=pl.ANY)
    outs = _call(
        body,
        name=name,
        out_shape=[_sds(a.shape, a.dtype) for a in arrs],
        in_specs=[any_spec] * n,
        out_specs=[any_spec] * n,
        scratch_shapes=_comm_sems(n),
    )(*arrs)
    return list(outs)


NN = (((1,), (0,)), ((), ()))
NT = (((1,), (1,)), ((), ()))
TN = (((0,), (0,)), ((), ()))


def _all_of(conds):
    out = conds[0]
    for cond in conds[1:]:
        out = out & cond
    return out


def _rider_call(body, operands, riders, *, name, out_shape, grid, in_specs, out_specs, scratch_shapes, sem):
    if not riders or not riders[1]:
        res = _call(body, name=name, out_shape=out_shape, grid=grid, in_specs=in_specs, out_specs=out_specs,
                    scratch_shapes=list(scratch_shapes), compiler_params=_params(*sem))(*operands)
        return list(res), []
    cls, arrs = riders
    nr, ni, no, ns = len(arrs), len(in_specs), len(out_specs), len(scratch_shapes)
    forward_at = (3 * grid[0]) // 4 if grid[0] >= 4 else None

    def wrapped(*refs):
        ids = [pl.program_id(ax) for ax in range(len(grid))]
        comm = cls(refs[ni:ni + nr], refs[ni + nr + no:ni + 2 * nr + no], *refs[ni + 2 * nr + no + ns:])
        pl.when(_all_of([i == 0 for i in ids]))(comm.start)
        if cls is _Gather and forward_at is not None:
            pl.when(_all_of([ids[0] == forward_at] + [i == 0 for i in ids[1:]]))(comm.forward)
        body(*refs[:ni], *refs[ni + nr:ni + nr + no], *refs[ni + 2 * nr + no:ni + 2 * nr + no + ns])

        def end():
            if cls is _Gather and forward_at is None:
                comm.forward()
            comm.finish()

        pl.when(_all_of([i == g - 1 for i, g in zip(ids, grid)]))(end)

    any_spec = pl.BlockSpec(memory_space=pl.ANY)
    res = _call(wrapped, name=name, out_shape=list(out_shape) + cls.out_shapes(arrs), grid=grid,
                in_specs=list(in_specs) + [any_spec] * nr, out_specs=list(out_specs) + [any_spec] * nr,
                scratch_shapes=list(scratch_shapes) + _comm_sems(nr),
                compiler_params=_params(*(["arbitrary"] * len(grid))))(*operands, *arrs)
    return list(res[:no]), list(res[no:])


def _gemm(a, b, out_shape, out_dtype, grid, a_spec, b_spec, o_spec, acc_shape, dims, name, riders=None):
    nk = grid[2]

    if nk == 1:
        def body(a_ref, b_ref, o_ref):
            r = lax.dot_general(a_ref[...].astype(BF16), b_ref[...].astype(BF16), dims,
                                preferred_element_type=F32)
            o_ref[...] = r.astype(o_ref.dtype)
        scratch = []
    else:
        def body(a_ref, b_ref, o_ref, acc_ref):
            k = pl.program_id(2)

            @pl.when(k == 0)
            def _():
                acc_ref[...] = jnp.zeros_like(acc_ref)

            acc_ref[...] += lax.dot_general(a_ref[...].astype(BF16), b_ref[...].astype(BF16), dims,
                                            preferred_element_type=F32)

            @pl.when(k == nk - 1)
            def _():
                o_ref[...] = acc_ref[...].astype(o_ref.dtype)
        scratch = [pltpu.VMEM(acc_shape, F32)]

    res, ridden = _rider_call(
        body, (a, b), riders, name=name, out_shape=[_sds(out_shape, out_dtype)], grid=grid,
        in_specs=[a_spec, b_spec], out_specs=[o_spec], scratch_shapes=scratch,
        sem=("parallel", "parallel", "arbitrary"))
    return (res[0], ridden) if riders else res[0]


def _div_tile(n, want):
    if n <= want:
        return n
    t = want - want % 128
    while n % t:
        t -= 128
    return t


def mm(a, b, name, out_dtype=F32, tm=512, tn=1024, tk=1024):
    m, k = a.shape
    n = b.shape[1]
    tm, tn, tk = _div_tile(m, tm), _div_tile(n, tn), _div_tile(k, tk)
    return _gemm(a, b, (m, n), out_dtype, (m // tm, n // tn, k // tk),
                 pl.BlockSpec((tm, tk), lambda i, j, kk: (i, kk)),
                 pl.BlockSpec((tk, tn), lambda i, j, kk: (kk, j)),
                 pl.BlockSpec((tm, tn), lambda i, j, kk: (i, j)),
                 (tm, tn), NN, name)


def mm_nt(a, b, name, out_dtype=F32, tm=512, tn=1024, tk=1024):
    m, n = a.shape
    k = b.shape[0]
    tm, tk_out, tred = _div_tile(m, tm), _div_tile(k, tn), _div_tile(n, tk)
    return _gemm(a, b, (m, k), out_dtype, (m // tm, k // tk_out, n // tred),
                 pl.BlockSpec((tm, tred), lambda i, j, kk: (i, kk)),
                 pl.BlockSpec((tk_out, tred), lambda i, j, kk: (j, kk)),
                 pl.BlockSpec((tm, tk_out), lambda i, j, kk: (i, j)),
                 (tm, tk_out), NT, name)


def mm_tn(a, b, name, out_dtype=F32, tm=512, tn=1024, tk=1024):
    m, k = a.shape
    n = b.shape[1]
    tk_out, tn, tred = _div_tile(k, tk), _div_tile(n, tn), _div_tile(m, tm)
    return _gemm(a, b, (k, n), out_dtype, (k // tk_out, n // tn, m // tred),
                 pl.BlockSpec((tred, tk_out), lambda i, j, kk: (kk, i)),
                 pl.BlockSpec((tred, tn), lambda i, j, kk: (kk, j)),
                 pl.BlockSpec((tk_out, tn), lambda i, j, kk: (i, j)),
                 (tk_out, tn), TN, name)


def mm_cs(a, wg, name, act_major=False, out_dtype=F32, tm=1024, riders=None):
    m, k = a.shape
    s, _, n = wg.shape
    tm = _div_tile(m, tm)
    if act_major:
        out_shape, o_spec = (s, m, n), pl.BlockSpec((None, tm, n), lambda i, j, kk: (j, i, 0))
    else:
        out_shape, o_spec = (m, s * n), pl.BlockSpec((tm, n), lambda i, j, kk: (i, j))
    return _gemm(a, wg, out_shape, out_dtype, (m // tm, s, 1),
                 pl.BlockSpec((tm, k), lambda i, j, kk: (i, 0)),
                 pl.BlockSpec((None, k, n), lambda i, j, kk: (j, 0, 0)),
                 o_spec, (tm, n), NN, name, riders)


def mm_cs_dx(da, wg, name, act_major=False, out_dtype=F32, tm=1024, riders=None):
    s, k, n = wg.shape
    m = da.shape[1] if act_major else da.shape[0]
    tm = _div_tile(m, tm)
    if act_major:
        a_spec = pl.BlockSpec((None, tm, n), lambda i, j, kk: (kk, i, 0))
    else:
        a_spec = pl.BlockSpec((tm, n), lambda i, j, kk: (i, kk))
    return _gemm(da, wg, (m, k), out_dtype, (m // tm, 1, s), a_spec,
                 pl.BlockSpec((None, k, n), lambda i, j, kk: (kk, 0, 0)),
                 pl.BlockSpec((tm, k), lambda i, j, kk: (i, 0)),
                 (tm, k), NT, name, riders)


def mm_cs_dw(a, da, name, act_major=False, out_dtype=F32, tm=1024, riders=None):
    m, k = a.shape
    if act_major:
        s, _, n = da.shape
    else:
        s, n = N_DEV, da.shape[1] // N_DEV
    tm = _div_tile(m, tm)
    if act_major:
        b_spec = pl.BlockSpec((None, tm, n), lambda i, j, kk: (i, kk, 0))
    else:
        b_spec = pl.BlockSpec((tm, n), lambda i, j, kk: (kk, i))
    return _gemm(a, da, (s, k, n), out_dtype, (s, 1, m // tm),
                 pl.BlockSpec((tm, k), lambda i, j, kk: (kk, 0)), b_spec,
                 pl.BlockSpec((None, k, n), lambda i, j, kk: (i, 0, 0)),
                 (k, n), TN, name, riders)


def mm_rs(s3, w3, name, out_dtype=F32, tm=1024, riders=None):
    s, m, n = s3.shape
    nn = w3.shape[2]
    tm = _div_tile(m, tm)
    return _gemm(s3, w3, (m, nn), out_dtype, (m // tm, 1, s),
                 pl.BlockSpec((None, tm, n), lambda i, j, kk: (kk, i, 0)),
                 pl.BlockSpec((None, n, nn), lambda i, j, kk: (kk, 0, 0)),
                 pl.BlockSpec((tm, nn), lambda i, j, kk: (i, 0)),
                 (tm, nn), NN, name, riders)


def mm_rs_dx(dy, w3, name, out_dtype=F32, tm=1024):
    m, nn = dy.shape
    s, n, _ = w3.shape
    tm = _div_tile(m, tm)
    return _gemm(dy, w3, (s, m, n), out_dtype, (m // tm, s, 1),
                 pl.BlockSpec((tm, nn), lambda i, j, kk: (i, 0)),
                 pl.BlockSpec((None, n, nn), lambda i, j, kk: (j, 0, 0)),
                 pl.BlockSpec((None, tm, n), lambda i, j, kk: (j, i, 0)),
                 (tm, n), NT, name)


def mm_rs_dw(s3, dy, name, out_dtype=F32, tm=1024):
    s, m, n = s3.shape
    nn = dy.shape[1]
    tm = _div_tile(m, tm)
    return _gemm(s3, dy, (s, n, nn), out_dtype, (s, 1, m // tm),
                 pl.BlockSpec((None, tm, n), lambda i, j, kk: (i, kk, 0)),
                 pl.BlockSpec((tm, nn), lambda i, j, kk: (kk, 0)),
                 pl.BlockSpec((None, n, nn), lambda i, j, kk: (i, 0, 0)),
                 (n, nn), TN, name)


def _colsum8(v):
    tr, d = v.shape
    return v.reshape(tr // 8, 8, d).sum(axis=0)


def _rstd(v):
    return lax.rsqrt(jnp.mean(v * v, axis=-1, keepdims=True) + EPS)


def _vec_spec(d):
    return pl.BlockSpec((1, d), lambda i: (0, 0))


def _acc_spec(d):
    return pl.BlockSpec((8, d), lambda i: (0, 0))


def _pre_rows(xv, g, shift, scale):
    return ((xv * _rstd(xv)) * g) * (1 + scale) + shift


def _post_rows(xv, yv, g, gate):
    return xv + gate * ((yv * _rstd(yv)) * g)


def _post_bwd_rows(dxv, yv, g, gate):
    r = _rstd(yv)
    yhat = yv * r
    dgate = _colsum8(dxv * (yhat * g))
    dyn = gate * dxv
    dg = _colsum8(dyn * yhat)
    dyhat = dyn * g
    dy = r * (dyhat - yhat * jnp.mean(dyhat * yhat, axis=-1, keepdims=True))
    return dy, dgate, dg


def _pre_bwd_rows(dhv, xv, g, scale, dxn):
    r = _rstd(xv)
    xhat = xv * r
    dshift = _colsum8(dhv)
    dscale = _colsum8(dhv * (xhat * g))
    dmod = dhv * (1 + scale)
    dg = _colsum8(dmod * xhat)
    dxhat = dmod * g
    dx = r * (dxhat - xhat * jnp.mean(dxhat * xhat, axis=-1, keepdims=True)) + dxn
    return dx, dshift, dscale, dg


def _row_call(body, name, t, d, rows_in, vecs_in, rows_out, n_acc):
    tr = _row_tile(t)
    nri, nvi, nro = len(rows_in), len(vecs_in), len(rows_out)

    def wrapped(*refs):
        accs = refs[nri + nvi + nro:]
        if n_acc:
            @pl.when(pl.program_id(0) == 0)
            def _():
                for acc in accs:
                    acc[...] = jnp.zeros_like(acc)
        body(*refs)

    row = pl.BlockSpec((tr, d), lambda i: (i, 0))
    return _call(wrapped, name=name,
                 out_shape=[_sds((t, d), dt) for dt in rows_out] + [_sds((8, d), F32)] * n_acc,
                 grid=(t // tr,), in_specs=[row] * nri + [_vec_spec(d)] * nvi,
                 out_specs=[row] * nro + [_acc_spec(d)] * n_acc,
                 compiler_params=_params("arbitrary" if n_acc else "parallel"))(*rows_in, *vecs_in)


def pre_fwd(x, g, shift, scale, name):
    def body(x_ref, g_ref, sh_ref, sc_ref, h_ref):
        h_ref[...] = _pre_rows(x_ref[...], g_ref[...], sh_ref[...], sc_ref[...]).astype(BF16)

    return _row_call(body, name, *x.shape, [x], [g, shift, scale], [BF16], 0)[0]


def post_pre_fwd(x, y, g_post, gate, g_pre, shift, scale, name):
    def body(x_ref, y_ref, gp_ref, gt_ref, g_ref, sh_ref, sc_ref, xn_ref, h_ref):
        xn = _post_rows(x_ref[...], y_ref[...], gp_ref[...], gt_ref[...])
        xn_ref[...] = xn
        h_ref[...] = _pre_rows(xn, g_ref[...], sh_ref[...], sc_ref[...]).astype(BF16)

    return _row_call(body, name, *x.shape, [x, y], [g_post, gate, g_pre, shift, scale], [F32, BF16], 0)


def post_loss_bwd(x, y, g, gate, target, name):
    d = x.shape[1]

    def body(x_ref, y_ref, t_ref, g_ref, gt_ref, dx_ref, dy_ref, l_ref, dgate_ref, dg_ref):
        yv, gv, gate_v = y_ref[...], g_ref[...], gt_ref[...]
        err = _post_rows(x_ref[...], yv, gv, gate_v) - t_ref[...]
        l_ref[...] += _colsum8(err * err)
        dxv = err * (1.0 / d)
        dx_ref[...] = dxv
        dy, dgate, dg = _post_bwd_rows(dxv, yv, gv, gate_v)
        dy_ref[...] = dy.astype(BF16)
        dgate_ref[...] += dgate
        dg_ref[...] += dg

    return _row_call(body, name, *x.shape, [x, y, target], [g, gate], [F32, BF16], 3)


def pre_post_bwd(dh, x, g_pre, scale, dxn, y, g_post, gate, name):
    def body(dh_ref, x_ref, dxn_ref, y_ref, g_ref, sc_ref, gp_ref, gt_ref,
             dx_ref, dy_ref, dsh_ref, dsc_ref, dg_ref, dgate_ref, dgp_ref):
        dx, dsh, dsc, dg = _pre_bwd_rows(dh_ref[...].astype(F32), x_ref[...], g_ref[...], sc_ref[...], dxn_ref[...])
        dx_ref[...] = dx
        dsh_ref[...] += dsh
        dsc_ref[...] += dsc
        dg_ref[...] += dg
        dy, dgate, dgp = _post_bwd_rows(dx, y_ref[...], gp_ref[...], gt_ref[...])
        dy_ref[...] = dy.astype(BF16)
        dgate_ref[...] += dgate
        dgp_ref[...] += dgp

    return _row_call(body, name, *x.shape, [dh, x, dxn, y], [g_pre, scale, g_post, gate], [F32, BF16], 5)


def pre_bwd(dh, x, g, scale, dxn, name):
    def body(dh_ref, x_ref, dxn_ref, g_ref, sc_ref, dx_ref, dsh_ref, dsc_ref, dg_ref):
        dx, dsh, dsc, dg = _pre_bwd_rows(dh_ref[...].astype(F32), x_ref[...], g_ref[...], sc_ref[...], dxn_ref[...])
        dx_ref[...] = dx
        dsh_ref[...] += dsh
        dsc_ref[...] += dsc
        dg_ref[...] += dg

    return _row_call(body, name, *x.shape, [dh, x, dxn], [g, scale], [F32], 3)


def _sigmoid(x):
    return 1.0 / (1.0 + jnp.exp(-x))


def ffn_in_swiglu(h, wg, name, tm=1024, riders=None):
    m, k = h.shape
    s, _, n = wg.shape
    half = s // 2
    tm = _div_tile(m, tm)

    def body(h_ref, wg_ref, wu_ref, g_ref, u_ref, s_ref):
        hv = h_ref[...]
        g = jnp.dot(hv, wg_ref[...], preferred_element_type=F32)
        u = jnp.dot(hv, wu_ref[...], preferred_element_type=F32)
        g_ref[...] = g
        u_ref[...] = u
        s_ref[...] = ((g * _sigmoid(g)) * u).astype(BF16)

    act = pl.BlockSpec((None, tm, n), lambda i, j: (j, i, 0))
    res, ridden = _rider_call(
        body, (h, wg, wg), riders, name=name,
        out_shape=[_sds((half, m, n), F32), _sds((half, m, n), F32), _sds((half, m, n), BF16)],
        grid=(m // tm, half),
        in_specs=[pl.BlockSpec((tm, k), lambda i, j: (i, 0)),
                  pl.BlockSpec((None, k, n), lambda i, j: (j, 0, 0)),
                  pl.BlockSpec((None, k, n), lambda i, j: (j + half, 0, 0))],
        out_specs=[act, act, act], scratch_shapes=[], sem=("parallel", "parallel"))
    return res[0], res[1], res[2], ridden


def ffn_out_dx_swiglu(dy, w4, gate, up, name, tm=1024, riders=None):
    m, nn = dy.shape
    half, n, _ = w4.shape
    tm = _div_tile(m, tm)

    def body(dy_ref, w_ref, g_ref, u_ref, o_ref):
        ds = lax.dot_general(dy_ref[...], w_ref[...], NT, preferred_element_type=F32)
        g, u = g_ref[...], u_ref[...]
        sig = _sigmoid(g)
        o_ref[0] = (ds * u * (sig * (1 + g * (1 - sig)))).astype(BF16)
        o_ref[1] = (ds * (g * sig)).astype(BF16)

    act = pl.BlockSpec((None, tm, n), lambda i, j: (j, i, 0))
    res, ridden = _rider_call(
        body, (dy, w4, gate, up), riders, name=name, out_shape=[_sds((2, half, m, n), BF16)],
        grid=(m // tm, half),
        in_specs=[pl.BlockSpec((tm, nn), lambda i, j: (i, 0)),
                  pl.BlockSpec((None, n, nn), lambda i, j: (j, 0, 0)), act, act],
        out_specs=[pl.BlockSpec((2, None, tm, n), lambda i, j: (0, j, i, 0))],
        scratch_shapes=[], sem=("parallel", "parallel"))
    return res[0].reshape(2 * half, m, n), ridden


def _split_hi_lo(v):
    hi = v.astype(BF16)
    lo = (v - hi.astype(F32)).astype(BF16)
    return hi, lo


SB_G = 2
SB_EXP_ZERO = 104.0
SB_UNSEEN = 3e38
SB_GW = SB_G * BLK


def _sb_specs(t):
    nq = t // BLK
    npair = N_HEADS // 2
    q_spec = pl.BlockSpec((BLK, BLK), lambda p, qb: (qb, p))
    k_spec = pl.BlockSpec((t, BLK), lambda p, qb: (0, npair + p))
    v_spec = pl.BlockSpec((t, BLK), lambda p, qb: (0, 2 * npair + p))
    kt_spec = pl.BlockSpec((t // SB_GW, BLK, SB_GW), lambda p, qb: (0, npair + p, 0))
    vt_spec = pl.BlockSpec((t // SB_GW, BLK, SB_GW), lambda p, qb: (0, 2 * npair + p, 0))
    c_spec = pl.BlockSpec((None, nq, 8, 2 * BLK), lambda p, qb: (p, 0, 0, qb))
    return nq, npair, q_spec, k_spec, v_spec, kt_spec, vt_spec, c_spec


def _sb_consts():
    row = lax.broadcasted_iota(jnp.int32, (BLK, BLK), 0)
    col = lax.broadcasted_iota(jnp.int32, (BLK, BLK), 1)
    lane0 = (col < HEAD_DIM).astype(F32)
    sub0 = (row < HEAD_DIM).astype(F32)
    return row, col, lane0, sub0


def _sb_valid(ks, qb):
    row = lax.broadcasted_iota(jnp.int32, (SB_GW, 2 * BLK), 0)
    col = lax.broadcasted_iota(jnp.int32, (SB_GW, 2 * BLK), 1)
    return (ks + row) < (qb * BLK + (col & (BLK - 1)))


def _blocks_on_lanes(v4):
    return jnp.concatenate([v4[b * BLK:(b + 1) * BLK] for b in range(SB_G)], axis=1)


def _tri2_dot(tri2, v):
    hi, lo = _split_hi_lo(v)
    return jnp.dot(tri2, jnp.concatenate([hi, lo], axis=0), preferred_element_type=F32)


def _sb_pair_loop(first, count, step, group, skip, carry):
    def pair(it, cy):
        g1 = first + 2 * step * it
        cy = group(g1, 0, 1, cy)
        return lax.cond(2 * it + 1 < count, lambda c: group(g1 + step, 1, 0, c), skip, cy)
    return lax.fori_loop(0, (count + 1) // 2, pair, carry)


def sb_fwd(qkv, qkv_t, riders, name):
    t = qkv.shape[0]
    assert t % SB_GW == 0
    nq, npair, q_spec, k_spec, _, _, vt_spec, c_spec = _sb_specs(t)
    nr = len(riders)

    def body(*refs):
        q_ref, k_ref, vt_ref = refs[:3]
        o_ref, c_ref = refs[3 + nr:5 + nr]
        oacc, zbuf0, zbuf1, kmax = refs[5 + 2 * nr:9 + 2 * nr]
        pp = pl.program_id(0)
        qb = pl.program_id(1)
        if nr:
            gather = _Gather(refs[3:3 + nr], refs[5 + nr:5 + 2 * nr], *refs[9 + 2 * nr:])
            pl.when((pp == 0) & (qb == 0))(gather.start)
            pl.when((pp == npair - 2) & (qb == 0))(gather.forward)
        _sb_fwd_step(q_ref, k_ref, vt_ref, o_ref, c_ref, oacc, zbuf0, zbuf1, kmax, qb)
        if nr:
            pl.when((pp == npair - 1) & (qb == nq - 1))(gather.finish)

    any_spec = pl.BlockSpec(memory_space=pl.ANY)
    outs = _call(
        body, name=name,
        out_shape=[_sds((t, D_MODEL), BF16), _sds((npair, nq, 8, 2 * t), F32)] + _Gather.out_shapes(riders),
        grid=(npair, nq), in_specs=[q_spec, k_spec, vt_spec] + [any_spec] * nr,
        out_specs=[pl.BlockSpec((BLK, BLK), lambda p, qb: (qb, p)), c_spec] + [any_spec] * nr,
        scratch_shapes=[pltpu.VMEM((BLK, 2 * BLK), F32), pltpu.VMEM((SB_GW, 2 * BLK), F32),
                        pltpu.VMEM((SB_GW, 2 * BLK), F32), pltpu.VMEM((8, BLK), F32)] + _comm_sems(nr),
        compiler_params=_params("arbitrary", "arbitrary"),
    )(qkv, qkv, qkv_t, *riders)
    return outs[0], outs[1], list(outs[2:])


def _sb_fwd_step(q_ref, k_ref, vt_ref, o_ref, c_ref, oacc, zbuf0, zbuf1, kmax, qb):
    row, col, lane0, sub0 = _sb_consts()
    tri = (col >= row).astype(BF16)
    tri2 = jnp.concatenate([tri, tri], axis=1)
    q2 = _two_heads(q_ref[...], lane0, QK_SCALE)
    zbufs = (zbuf0, zbuf1)
    c_ref[...] = jnp.full(c_ref.shape, SB_UNSEEN, F32)
    oacc[...] = jnp.zeros_like(oacc)

    @pl.when(qb == 0)
    def _():
        ksq = jnp.square(k_ref[...].astype(F32))
        head0 = (lax.broadcasted_iota(jnp.int32, (1, BLK), 1) < HEAD_DIM).astype(F32)
        norms = jnp.maximum(jnp.sum(ksq * head0, axis=1, keepdims=True),
                            jnp.sum(ksq * (1.0 - head0), axis=1, keepdims=True))
        kmax[...] = jnp.broadcast_to(jnp.max(norms, axis=0, keepdims=True), kmax.shape)

    qsq = jnp.square(q2.astype(F32)).astype(BF16)
    qn2 = jnp.max(lax.dot_general(jnp.ones((8, BLK), BF16), qsq, NT, preferred_element_type=F32),
                  axis=0, keepdims=True)
    kk = kmax[0:1, :]
    zbound = jnp.sqrt(qn2 * jnp.concatenate([kk, kk], axis=1)) * 1.02

    def matters(cr):
        return (jnp.min(cr - zbound) <= SB_EXP_ZERO).astype(jnp.int32)

    def scores(g):
        ks = pl.multiple_of(g * SB_GW, SB_GW)
        return lax.dot_general(k_ref[pl.ds(ks, SB_GW), :], q2, NT, preferred_element_type=F32)

    def group(g, cur, nxt, cr, masked=False):
        z = zbufs[cur][...]
        zbufs[nxt][...] = scores(jnp.maximum(g - 1, 0))
        e = jnp.exp(-jnp.abs(z))
        sp = jnp.maximum(z, 0.0) + jnp.log(1.0 + e)
        if masked:
            valid = _sb_valid(g * SB_GW, qb)
            sp = jnp.where(valid, sp, 0.0)
        loc = _tri2_dot(tri2, _blocks_on_lanes(sp))
        parts = [None] * SB_G
        for b in reversed(range(SB_G)):
            rows = slice(b * BLK, (b + 1) * BLK)
            c_ref[g * SB_G + b] = jnp.broadcast_to(cr, (8, 2 * BLK))
            a = jnp.exp(z[rows] - (loc[:, 2 * b * BLK:2 * (b + 1) * BLK] + cr))
            if masked:
                a = jnp.where(valid[rows], a, 0.0)
            parts[b] = a.astype(BF16)
            cr = cr + jnp.sum(sp[rows], axis=0, keepdims=True)
        oacc[...] += jnp.dot(vt_ref[g], jnp.concatenate(parts, axis=0), preferred_element_type=F32)
        return cr

    last = qb // SB_G
    zbuf1[...] = scores(last)
    cr = group(last, 1, 0, jnp.zeros((1, 2 * BLK), F32), masked=True)

    def pair(state):
        g, cr, _ = state
        cr = group(g, 0, 1, cr)
        more = (g >= 1).astype(jnp.int32) * matters(cr)
        cr = lax.cond(more > 0, lambda c: group(g - 1, 1, 0, c), lambda c: c, cr)
        return jnp.where(more > 0, g - 2, -1), cr, matters(cr)

    lax.while_loop(lambda st: (st[0] >= 0) & (st[2] > 0), pair, (last - 1, cr, matters(cr)))
    o_t = oacc[:, :BLK] * sub0 + oacc[:, BLK:] * (1.0 - sub0)
    o_ref[...] = o_t.T.astype(BF16)


def sb_bwd(qkv, qkv_t, do, cmass, riders, name):
    t = qkv.shape[0]
    nq, npair, q_spec, k_spec, v_spec, kt_spec, _, c_spec = _sb_specs(t)
    nr = len(riders)

    def body(*refs):
        pp = pl.program_id(0)
        qb = pl.program_id(1)
        if nr:
            exchange = _Exchange(refs[6:6 + nr], refs[9 + nr:9 + 2 * nr], *refs[14 + 2 * nr:])
            pl.when((pp == 0) & (qb == 0))(exchange.start)
        step(*refs[:6], *refs[6 + nr:9 + nr], *refs[9 + 2 * nr:14 + 2 * nr])
        if nr:
            pl.when((pp == npair - 1) & (qb == nq - 1))(exchange.finish)

    def step(q_ref, k_ref, kt_ref, v_ref, do_ref, c_ref, dq_ref, dk_ref, dv_ref, dqacc, dkacc, dvacc,
             zbuf0, zbuf1):
        qb = pl.program_id(1)

        @pl.when(qb == 0)
        def _():
            dkacc[...] = jnp.zeros_like(dkacc)
            dvacc[...] = jnp.zeros_like(dvacc)

        row, col, lane0, sub0 = _sb_consts()
        tri_suf = (col >= row).astype(BF16)
        tri_pre = (col <= row).astype(BF16)
        tri2_suf = jnp.concatenate([tri_suf, tri_suf], axis=1)
        tri2_pre = jnp.concatenate([tri_pre, tri_pre], axis=1)
        q2 = _two_heads(q_ref[...], lane0, QK_SCALE)
        do2 = _two_heads(do_ref[...], lane0, 1.0)
        zbufs = (zbuf0, zbuf1)
        dqacc[...] = jnp.zeros_like(dqacc)
        last = qb // SB_G

        def scores(g):
            ks = pl.multiple_of(g * SB_GW, SB_GW)
            return lax.dot_general(k_ref[pl.ds(ks, SB_GW), :], q2, NT, preferred_element_type=F32)

        def group(g, cur, nxt, gc, masked=False):
            ks = pl.multiple_of(g * SB_GW, SB_GW)
            z = zbufs[cur][...]
            zbufs[nxt][...] = scores(jnp.minimum(g + 1, last))
            e = jnp.exp(-jnp.abs(z))
            sig = 0.5 * jnp.tanh(0.5 * z) + 0.5
            sp = jnp.maximum(z, 0.0) + jnp.log(1.0 + e)
            if masked:
                valid = _sb_valid(ks, qb)
                sp = jnp.where(valid, sp, 0.0)
            loc = _tri2_dot(tri2_suf, _blocks_on_lanes(sp))
            parts = []
            for b in range(SB_G):
                rows = slice(b * BLK, (b + 1) * BLK)
                mass = loc[:, 2 * b * BLK:2 * (b + 1) * BLK] + c_ref[g * SB_G + b, 0:1, :]
                parts.append(jnp.exp(z[rows] - mass))
            a = jnp.concatenate(parts, axis=0)
            if masked:
                a = jnp.where(valid, a, 0.0)
            gr = lax.dot_general(v_ref[pl.ds(ks, SB_GW), :], do2, NT, preferred_element_type=F32) * a
            pre = _tri2_dot(tri2_pre, _blocks_on_lanes(gr))
            parts = []
            for b in range(SB_G):
                rows = slice(b * BLK, (b + 1) * BLK)
                parts.append(pre[:, 2 * b * BLK:2 * (b + 1) * BLK] + gc)
                gc = gc + jnp.sum(gr[rows], axis=0, keepdims=True)
            dz = gr - sig * jnp.concatenate(parts, axis=0)
            if masked:
                dz = jnp.where(valid, dz, 0.0)
            dz = dz.astype(BF16)
            dkacc[pl.ds(ks, SB_GW), :] += jnp.dot(dz, q2, preferred_element_type=F32)
            dqacc[...] += jnp.dot(kt_ref[g], dz, preferred_element_type=F32)
            dvacc[pl.ds(ks, SB_GW), :] += jnp.dot(a.astype(BF16), do2, preferred_element_type=F32)
            return gc

        def skip(gc):
            zbuf0[...] = zbuf1[...]
            return gc

        def unseen(g):
            return (jnp.max(c_ref[g * SB_G + SB_G - 1, 0:1, :]) > 0.5 * SB_UNSEEN).astype(jnp.int32)

        first, _ = lax.while_loop(lambda st: (st[0] > 0) & (st[1] == 0),
                                  lambda st: (st[0] - 1, unseen(jnp.maximum(st[0] - 2, 0))),
                                  (last, unseen(jnp.maximum(last - 1, 0))))
        zbuf0[...] = scores(first)
        gc = _sb_pair_loop(first, last - first, 1, group, skip, jnp.zeros((1, 2 * BLK), F32))
        group(last, 0, 1, gc, masked=True)
        dq_t = (dqacc[:, :BLK] * sub0 + dqacc[:, BLK:] * (1.0 - sub0)) * QK_SCALE
        dq_ref[...] = dq_t.T.astype(BF16)

        @pl.when(qb == nq - 1)
        def _():
            dk_ref[...] = dkacc[...].astype(BF16)
            dv_ref[...] = dvacc[...].astype(BF16)

    col_spec = pl.BlockSpec((t, BLK), lambda p, qb: (0, p))
    blk_spec = pl.BlockSpec((BLK, BLK), lambda p, qb: (qb, p))
    any_spec = pl.BlockSpec(memory_space=pl.ANY)
    outs = _call(
        body, name=name,
        out_shape=[_sds((t, D_MODEL), BF16)] * 3 + [_sds(r.shape, r.dtype) for r in riders],
        grid=(npair, nq), in_specs=[q_spec, k_spec, kt_spec, v_spec, blk_spec, c_spec] + [any_spec] * nr,
        out_specs=[blk_spec, col_spec, col_spec] + [any_spec] * nr,
        scratch_shapes=[pltpu.VMEM((BLK, 2 * BLK), F32), pltpu.VMEM((t, BLK), F32), pltpu.VMEM((t, BLK), F32),
                        pltpu.VMEM((SB_GW, 2 * BLK), F32), pltpu.VMEM((SB_GW, 2 * BLK), F32)] + _comm_sems(nr),
        compiler_params=_params("arbitrary", "arbitrary"),
    )(qkv, qkv, qkv_t, qkv, do, cmass, *riders)
    return outs[0], outs[1], outs[2], list(outs[3:])


BAND_QPS = 4


def _band_static_mask(jj):
    row = lax.broadcasted_iota(jnp.int32, (2 * BLK, BLK), 0)
    col = lax.broadcasted_iota(jnp.int32, (2 * BLK, BLK), 1)
    qc = (row & (BLK - 1)) // 64
    kc = 2 * jj + col // 64
    return (kc >= qc) & (kc <= qc + 8)


def _band_key_start(qb, jj):
    kb = qb - (BAND_BLOCKS - 1) + jj
    return kb, pl.multiple_of(jnp.maximum(kb, 0) * BLK, BLK)


def _band_probs(q2, k_ref, bias, qb):
    blocks = []
    for jj in range(BAND_BLOCKS):
        kb, ks = _band_key_start(qb, jj)
        s = lax.dot_general(q2, k_ref[pl.ds(ks, BLK), :], NT, preferred_element_type=F32)
        s = s + bias[:, jj * BLK:(jj + 1) * BLK]
        ok = (kb >= 0) if 0 < jj < BAND_BLOCKS - 1 else _band_static_mask(jj) & (kb >= 0)
        blocks.append(jnp.where(ok, s, NEG))
    s = jnp.concatenate(blocks, axis=1)
    m = jnp.max(s, axis=-1, keepdims=True)
    e = jnp.exp(s - m)
    return e / jnp.sum(e, axis=-1, keepdims=True)


def _band_specs(t):
    npair = N_HEADS // 2
    rows = BAND_QPS * BLK
    q_spec = pl.BlockSpec((rows, BLK), lambda p, i: (i, p))
    k_spec = pl.BlockSpec((t, BLK), lambda p, i: (0, npair + p))
    v_spec = pl.BlockSpec((t, BLK), lambda p, i: (0, 2 * npair + p))
    b_spec = pl.BlockSpec((2, BLK, BAND_W), lambda p, i: (p, 0, 0))
    return npair, t // rows, q_spec, k_spec, v_spec, b_spec


def _two_heads(xv, lane0, scale):
    xf = xv.astype(F32)
    if scale != 1.0:
        xf = xf * scale
    return jnp.concatenate([xf * lane0, xf * (1.0 - lane0)], axis=0).astype(BF16)


def _one_of_two_heads(r, lane0):
    return r[:BLK] * lane0 + r[BLK:] * (1.0 - lane0)


def band_fwd(qkv, bias, name):
    t = qkv.shape[0]
    assert t % (BAND_QPS * BLK) == 0
    npair, nsteps, q_spec, k_spec, v_spec, b_spec = _band_specs(t)

    def body(q_ref, k_ref, v_ref, b_ref, o_ref):
        step = pl.program_id(1)
        _, _, lane0, _ = _sb_consts()
        bias2 = b_ref[...].reshape(2 * BLK, BAND_W)
        for u in range(BAND_QPS):
            qb = step * BAND_QPS + u
            rows = slice(u * BLK, (u + 1) * BLK)
            q2 = _two_heads(q_ref[rows, :], lane0, QK_SCALE)
            p = _band_probs(q2, k_ref, bias2, qb)
            acc = jnp.zeros((2 * BLK, BLK), F32)
            for jj in range(BAND_BLOCKS):
                _, ks = _band_key_start(qb, jj)
                acc += jnp.dot(p[:, jj * BLK:(jj + 1) * BLK].astype(BF16), v_ref[pl.ds(ks, BLK), :],
                               preferred_element_type=F32)
            o_ref[rows, :] = _one_of_two_heads(acc, lane0).astype(BF16)

    return _call(
        body, name=name, out_shape=_sds((t, D_MODEL), BF16), grid=(npair, nsteps),
        in_specs=[q_spec, k_spec, v_spec, b_spec],
        out_specs=pl.BlockSpec((BAND_QPS * BLK, BLK), lambda p, i: (i, p)),
        compiler_params=_params("parallel", "parallel"),
    )(qkv, qkv, qkv, bias)


def band_bwd(qkv, do, bias, name, riders=None):
    t = qkv.shape[0]
    npair, nsteps, q_spec, k_spec, v_spec, b_spec = _band_specs(t)

    def body(q_ref, k_ref, v_ref, do_ref, b_ref, dq_ref, dk_ref, dv_ref, db_ref, dkacc, dvacc):
        step = pl.program_id(1)

        @pl.when(step == 0)
        def _():
            dkacc[...] = jnp.zeros_like(dkacc)
            dvacc[...] = jnp.zeros_like(dvacc)
            db_ref[...] = jnp.zeros_like(db_ref)

        _, _, lane0, _ = _sb_consts()
        bias2 = b_ref[...].reshape(2 * BLK, BAND_W)
        updates = []
        for u in range(BAND_QPS):
            qb = step * BAND_QPS + u
            rows = slice(u * BLK, (u + 1) * BLK)
            q2 = _two_heads(q_ref[rows, :], lane0, QK_SCALE)
            do2 = _two_heads(do_ref[rows, :], lane0, 1.0)
            p = _band_probs(q2, k_ref, bias2, qb)
            dp = jnp.concatenate(
                [lax.dot_general(do2, v_ref[pl.ds(_band_key_start(qb, jj)[1], BLK), :], NT,
                                 preferred_element_type=F32) for jj in range(BAND_BLOCKS)], axis=1)
            ds = p * (dp - jnp.sum(p * dp, axis=-1, keepdims=True))
            db_ref[...] += ds.reshape(2, BLK, BAND_W)
            dqa = jnp.zeros((2 * BLK, BLK), F32)
            for jj in range(BAND_BLOCKS):
                _, ks = _band_key_start(qb, jj)
                dsb = ds[:, jj * BLK:(jj + 1) * BLK].astype(BF16)
                pb = p[:, jj * BLK:(jj + 1) * BLK].astype(BF16)
                dqa += jnp.dot(dsb, k_ref[pl.ds(ks, BLK), :], preferred_element_type=F32)
                updates.append((ks, lax.dot_general(dsb, q2, TN, preferred_element_type=F32),
                                lax.dot_general(pb, do2, TN, preferred_element_type=F32)))
            dq_ref[rows, :] = (_one_of_two_heads(dqa, lane0) * QK_SCALE).astype(BF16)
        for ks, dk_part, dv_part in updates:
            dkacc[pl.ds(ks, BLK), :] += dk_part
            dvacc[pl.ds(ks, BLK), :] += dv_part

        @pl.when(step == nsteps - 1)
        def _():
            dk_ref[...] = dkacc[...].astype(BF16)
            dv_ref[...] = dvacc[...].astype(BF16)

    col_spec = pl.BlockSpec((t, BLK), lambda p, i: (0, p))
    blk_spec = pl.BlockSpec((BAND_QPS * BLK, BLK), lambda p, i: (i, p))
    res, ridden = _rider_call(
        body, (qkv, qkv, qkv, do, bias), riders, name=name,
        out_shape=[_sds((t, D_MODEL), BF16)] * 3 + [_sds((N_HEADS, BLK, BAND_W), F32)],
        grid=(npair, nsteps), in_specs=[q_spec, k_spec, v_spec, blk_spec, b_spec],
        out_specs=[blk_spec, col_spec, col_spec, b_spec],
        scratch_shapes=[pltpu.VMEM((t, BLK), F32), pltpu.VMEM((t, BLK), F32)],
        sem=("parallel", "arbitrary"))
    return res[0], res[1], res[2], res[3], ridden


def band_bias_window(rel_bias):
    far = BAND_W + BLK - 1 - 2 * REL_CLIP
    width = BAND_W + BLK
    ext = jnp.concatenate(
        [jnp.broadcast_to(rel_bias[:, 2 * REL_CLIP:], (N_HEADS, far)), rel_bias[:, 2 * REL_CLIP:0:-1],
         jnp.zeros((N_HEADS, 2), F32)], axis=1)
    tiled = jnp.broadcast_to(ext[:, None, :], (N_HEADS, BLK, width + 1)).reshape(N_HEADS, BLK * (width + 1))
    return tiled[:, BLK - 1:BLK - 1 + BLK * width].reshape(N_HEADS, BLK, width)[:, :, :BAND_W]


def band_bias_window_grad(dwin):
    width = BAND_W + BLK
    far = BAND_W + BLK - 1 - 2 * REL_CLIP
    flat = jnp.pad(dwin, ((0, 0), (0, 0), (0, BLK))).reshape(N_HEADS, BLK * width)
    skew = jnp.pad(flat, ((0, 0), (BLK - 1, 1))).reshape(N_HEADS, BLK, width + 1)
    dext = jnp.sum(skew, axis=1)[:, :width - 1]
    return jnp.concatenate(
        [jnp.zeros((N_HEADS, 1), F32), dext[:, :far - 1:-1][:, :2 * REL_CLIP - 1],
         dext[:, far:far + 1] + jnp.sum(dext[:, :far], axis=1, keepdims=True)], axis=1)


SG_GROUPS = 8


def _gelu_parts(x):
    inner = GELU_C0 * (x + GELU_C1 * (x * x * x))
    th = jnp.tanh(inner)
    return th, 0.5 * x * (1.0 + th)


def _sg_gate_mask():
    row = lax.broadcasted_iota(jnp.int32, (BLK, BLK), 0)
    col = lax.broadcasted_iota(jnp.int32, (BLK, BLK), 1)
    return (row // 64) >= (col // 64)


def _sg_forward_parts(a, lng):
    w = a.shape[1] // 2
    th, z = _gelu_parts(a)
    u, v = z[:, :w], z[:, w:]
    mu = jnp.mean(v, axis=-1, keepdims=True)
    xc = v - mu
    rstd = lax.rsqrt(jnp.mean(xc * xc, axis=-1, keepdims=True) + EPS)
    vhat = xc * rstd
    return th, u, vhat, rstd, vhat * lng


def sg_fwd(a, lng, ws, bias_t, name):
    t, w2 = a.shape
    w = w2 // 2
    gc = w // SG_GROUPS

    def body(a_ref, lng_ref, ws_ref, bt_ref, y_ref):
        _, u, _, _, vln = _sg_forward_parts(a_ref[...], lng_ref[...])
        mask = _sg_gate_mask()
        bt = bt_ref[...]
        lane = lax.broadcasted_iota(jnp.int32, (BLK, BLK), 1)
        for g in range(SG_GROUPS):
            wg = jnp.where(mask, ws_ref[g], 0.0).astype(BF16)
            sv = jnp.dot(wg, vln[:, g * gc:(g + 1) * gc].astype(BF16), preferred_element_type=F32)
            bg = jnp.sum(jnp.where(lane == g, bt, 0.0), axis=-1, keepdims=True)
            y_ref[:, g * gc:(g + 1) * gc] = (u[:, g * gc:(g + 1) * gc] * (sv + bg)).astype(BF16)

    return _call(
        body, name=name, out_shape=_sds((t, w), BF16), grid=(t // BLK,),
        in_specs=[pl.BlockSpec((BLK, w2), lambda i: (i, 0)), pl.BlockSpec((1, w), lambda i: (0, 0)),
                  pl.BlockSpec((SG_GROUPS, BLK, BLK), lambda i: (0, 0, 0)),
                  pl.BlockSpec((BLK, BLK), lambda i: (0, 0))],
        out_specs=pl.BlockSpec((BLK, w), lambda i: (i, 0)),
        compiler_params=_params("parallel"),
    )(a, lng, ws, bias_t)


def sg_bwd(a, dy, lng, ws, bias_t, name):
    t, w2 = a.shape
    w = w2 // 2
    gc = w // SG_GROUPS

    def body(a_ref, dy_ref, lng_ref, ws_ref, bt_ref, da_ref, dlng_ref, dws_ref, dbt_ref):
        @pl.when(pl.program_id(0) == 0)
        def _():
            dlng_ref[...] = jnp.zeros_like(dlng_ref)
            dws_ref[...] = jnp.zeros_like(dws_ref)
            dbt_ref[...] = jnp.zeros_like(dbt_ref)

        av, lng = a_ref[...], lng_ref[...]
        th, u, vhat, rstd, vln = _sg_forward_parts(av, lng)
        mask = _sg_gate_mask()
        bt = bt_ref[...]
        lane = lax.broadcasted_iota(jnp.int32, (BLK, BLK), 1)
        dyv = dy_ref[...]
        du_parts, dvln_parts = [], []
        dbt = jnp.zeros((BLK, BLK), F32)
        for g in range(SG_GROUPS):
            sl = slice(g * gc, (g + 1) * gc)
            wg = jnp.where(mask, ws_ref[g], 0.0).astype(BF16)
            vg = vln[:, sl].astype(BF16)
            sv = jnp.dot(wg, vg, preferred_element_type=F32)
            bg = jnp.sum(jnp.where(lane == g, bt, 0.0), axis=-1, keepdims=True)
            dyg = dyv[:, sl]
            du_parts.append(dyg * (sv + bg))
            dsv = dyg * u[:, sl]
            dbt += jnp.where(lane == g, jnp.sum(dsv, axis=-1, keepdims=True), 0.0)
            dsvb = dsv.astype(BF16)
            dws_ref[g] += jnp.where(mask, lax.dot_general(dsvb, vg, NT, preferred_element_type=F32), 0.0)
            dvln_parts.append(lax.dot_general(wg, dsvb, TN, preferred_element_type=F32))
        dbt_ref[...] += dbt
        du = jnp.concatenate(du_parts, axis=1)
        dvln = jnp.concatenate(dvln_parts, axis=1)
        dlng_ref[...] += _colsum8(dvln * vhat)
        dvhat = dvln * lng
        dv = rstd * (dvhat - jnp.mean(dvhat, axis=-1, keepdims=True)
                     - vhat * jnp.mean(dvhat * vhat, axis=-1, keepdims=True))
        dz = jnp.concatenate([du, dv], axis=1)
        dgelu = 0.5 * (1.0 + th) + (0.5 * av) * (1.0 - th * th) * (GELU_C0 * (1.0 + 3.0 * GELU_C1 * (av * av)))
        da_ref[...] = (dz * dgelu).astype(BF16)

    return _call(
        body, name=name,
        out_shape=[_sds((t, w2), BF16), _sds((8, w), F32), _sds((SG_GROUPS, BLK, BLK), F32), _sds((BLK, BLK), F32)],
        grid=(t // BLK,),
        in_specs=[pl.BlockSpec((BLK, w2), lambda i: (i, 0)), pl.BlockSpec((BLK, w), lambda i: (i, 0)),
                  pl.BlockSpec((1, w), lambda i: (0, 0)),
                  pl.BlockSpec((SG_GROUPS, BLK, BLK), lambda i: (0, 0, 0)),
                  pl.BlockSpec((BLK, BLK), lambda i: (0, 0))],
        out_specs=[pl.BlockSpec((BLK, w2), lambda i: (i, 0)), pl.BlockSpec((8, w), lambda i: (0, 0)),
                   pl.BlockSpec((SG_GROUPS, BLK, BLK), lambda i: (0, 0, 0)),
                   pl.BlockSpec((BLK, BLK), lambda i: (0, 0))],
        compiler_params=_params("arbitrary"),
    )(a, dy, lng, ws, bias_t)


def _shift_down(cat, n, tr):
    return pltpu.roll(cat, n, 0)[8:8 + tr]


def _shift_up(cat, n, tr):
    return pltpu.roll(cat, tr + 8 - n, 0)[0:tr]


def conv_fwd(p, cw, name):
    t, d3 = p.shape
    d = d3 // 3
    tr = min(256, t)
    hb = tr // 8

    def body(p_ref, ph_ref, cw_ref, o_ref):
        i = pl.program_id(0)
        pv = p_ref[...]
        y = pv[:, d:2 * d] * pv[:, 2 * d:]
        ph = ph_ref[...]
        yh = jnp.where(i > 0, ph[:, d:2 * d] * ph[:, 2 * d:], 0.0)
        cat = jnp.concatenate([yh, y], axis=0)
        yc = (cw_ref[0:1, :] * _shift_down(cat, 2, tr) + cw_ref[1:2, :] * _shift_down(cat, 1, tr)
              + cw_ref[2:3, :] * y)
        o_ref[...] = (pv[:, :d] * yc).astype(BF16)

    return _call(
        body, name=name, out_shape=_sds((t, d), BF16), grid=(t // tr,),
        in_specs=[pl.BlockSpec((tr, d3), lambda i: (i, 0)),
                  pl.BlockSpec((8, d3), lambda i: (jnp.maximum(i * hb - 1, 0), 0)),
                  pl.BlockSpec((8, d), lambda i: (0, 0))],
        out_specs=pl.BlockSpec((tr, d), lambda i: (i, 0)),
        compiler_params=_params("parallel"),
    )(p, p, cw)


def conv_bwd(p, dz, cw, name):
    t, d3 = p.shape
    d = d3 // 3
    tr = min(256, t)
    hb = tr // 8
    nt = t // tr

    def body(p_ref, ph_ref, pn_ref, dz_ref, dzn_ref, cw_ref, dp_ref, dcw_ref):
        i = pl.program_id(0)

        @pl.when(i == 0)
        def _():
            dcw_ref[...] = jnp.zeros_like(dcw_ref)

        pv = p_ref[...]
        gb, gcv, xt = pv[:, :d], pv[:, d:2 * d], pv[:, 2 * d:]
        y = gcv * xt
        ph = ph_ref[...]
        yh = jnp.where(i > 0, ph[:, d:2 * d] * ph[:, 2 * d:], 0.0)
        cat = jnp.concatenate([yh, y], axis=0)
        y2, y1 = _shift_down(cat, 2, tr), _shift_down(cat, 1, tr)
        w0, w1, w2 = cw_ref[0:1, :], cw_ref[1:2, :], cw_ref[2:3, :]
        yc = w0 * y2 + w1 * y1 + w2 * y
        dzv = dz_ref[...]
        dyc = dzv * gb
        dcw_ref[0] += _colsum8(dyc * y2)
        dcw_ref[1] += _colsum8(dyc * y1)
        dcw_ref[2] += _colsum8(dyc * y)
        dycn = jnp.where(i < nt - 1, dzn_ref[...] * pn_ref[...][:, :d], 0.0)
        catn = jnp.concatenate([dyc, dycn], axis=0)
        dy = w2 * dyc + w1 * _shift_up(catn, 1, tr) + w0 * _shift_up(catn, 2, tr)
        dp_ref[:, :d] = (dzv * yc).astype(BF16)
        dp_ref[:, d:2 * d] = (dy * xt).astype(BF16)
        dp_ref[:, 2 * d:] = (dy * gcv).astype(BF16)

    nxt = lambda i: (jnp.minimum((i + 1) * hb, t // 8 - 1), 0)
    return _call(
        body, name=name, out_shape=[_sds((t, d3), BF16), _sds((3, 8, d), F32)], grid=(nt,),
        in_specs=[pl.BlockSpec((tr, d3), lambda i: (i, 0)),
                  pl.BlockSpec((8, d3), lambda i: (jnp.maximum(i * hb - 1, 0), 0)),
                  pl.BlockSpec((8, d3), nxt),
                  pl.BlockSpec((tr, d), lambda i: (i, 0)),
                  pl.BlockSpec((8, d), nxt),
                  pl.BlockSpec((8, d), lambda i: (0, 0))],
        out_specs=[pl.BlockSpec((tr, d3), lambda i: (i, 0)), pl.BlockSpec((3, 8, d), lambda i: (0, 0, 0))],
        compiler_params=_params("arbitrary"),
    )(p, p, p, dz, dz, cw)


def ada_fwd(c_all, w, b, name):
    nl, d, n = w.shape

    def body(c_ref, w_ref, b_ref, o_ref):
        cv = c_ref[...]
        s = (cv * _sigmoid(cv)).astype(BF16)
        o_ref[...] = jnp.dot(s, w_ref[...].astype(BF16), preferred_element_type=F32) + b_ref[...]

    return _call(
        body, name=name, out_shape=_sds((nl, N_DEV, n), F32), grid=(nl,),
        in_specs=[pl.BlockSpec((N_DEV, d), lambda l: (0, 0)), pl.BlockSpec((None, d, n), lambda l: (l, 0, 0)),
                  pl.BlockSpec((None, 1, n), lambda l: (l, 0, 0))],
        out_specs=pl.BlockSpec((None, N_DEV, n), lambda l: (l, 0, 0)),
        compiler_params=_params("parallel"),
    )(c_all, w, b)


def ada_bwd(c_all, dmod, name):
    nl, _, n = dmod.shape
    d = c_all.shape[1]

    def body(c_ref, dm_ref, o_ref):
        cv = c_ref[...]
        s = (cv * _sigmoid(cv)).astype(BF16)
        o_ref[...] = lax.dot_general(s, dm_ref[...].astype(BF16), TN, preferred_element_type=F32)

    return _call(
        body, name=name, out_shape=_sds((nl, d, n), F32), grid=(nl,),
        in_specs=[pl.BlockSpec((N_DEV, d), lambda l: (0, 0)), pl.BlockSpec((None, N_DEV, n), lambda l: (l, 0, 0))],
        out_specs=pl.BlockSpec((None, d, n), lambda l: (l, 0, 0)),
        compiler_params=_params("parallel"),
    )(c_all, dmod)


def adamw(pieces, w, m, v, name, riders=None):
    nl = len(pieces)
    npc, r, c = pieces[0].shape
    tr = r
    for cand in (1024, 512, 256, 128, 64, 32, 16, 8):
        if r % cand == 0 and cand * c * 4 <= (1 << 20):
            tr = cand
            break
    nt = r // tr

    def update(p_ref, w_ref, m_ref, v_ref, g_ref, d_ref, nm_ref, nv_ref):
        g = p_ref[0].astype(F32)
        for i in range(1, npc):
            g = g + p_ref[i].astype(F32)
        wv = w_ref[...]
        nm = ADAM_B1 * m_ref[...] + (1.0 - ADAM_B1) * g
        nv = ADAM_B2 * v_ref[...] + (1.0 - ADAM_B2) * (g * g)
        m_hat = nm / (1.0 - ADAM_B1 ** ADAM_STEP)
        v_hat = nv / (1.0 - ADAM_B2 ** ADAM_STEP)
        g_ref[...] = g
        d_ref[...] = -ADAM_LR * (m_hat / (jnp.sqrt(v_hat) + ADAM_EPS) + ADAM_WD * wv)
        nm_ref[...] = nm
        nv_ref[...] = nv

    def body(*refs):
        if nl == 1:
            update(*refs)
        else:
            for j in range(nl):
                pl.when(pl.program_id(0) == j)(lambda j=j: update(refs[j], *refs[nl:]))

    row = pl.BlockSpec((tr, c), lambda l, i: (l * nt + i, 0))
    piece_specs = [pl.BlockSpec((npc, tr, c), lambda l, i, j=j: (0, jnp.where(l == j, i, 0), 0))
                   for j in range(nl)]
    res, ridden = _rider_call(
        body, (*pieces, w, m, v), riders, name=name, out_shape=[_sds((nl * r, c), F32)] * 4, grid=(nl, nt),
        in_specs=piece_specs + [row, row, row], out_specs=[row] * 4, scratch_shapes=[],
        sem=("parallel", "parallel"))
    return res, ridden


def sum_pieces(pieces, name):
    npc, r, c = pieces.shape

    def body(p_ref, o_ref):
        g = p_ref[0]
        for i in range(1, npc):
            g = g + p_ref[i]
        o_ref[...] = g

    return _call(body, name=name, out_shape=_sds((r, c), F32),
                 in_specs=[pl.BlockSpec(memory_space=pltpu.VMEM)],
                 out_specs=pl.BlockSpec(memory_space=pltpu.VMEM),
                 compiler_params=pltpu.CompilerParams(vmem_limit_bytes=VMEM_LIMIT))(pieces)


PACK_W = 1024


def _pack(arrs):
    flat = jnp.concatenate([a.reshape(-1).astype(F32) for a in arrs])
    rows = -(-flat.shape[0] // (8 * PACK_W)) * 8
    return jnp.pad(flat, (0, rows * PACK_W - flat.shape[0])).reshape(rows, PACK_W)


def _unpack(slab, shapes):
    flat = slab.reshape(-1)
    out, off = [], 0
    for s in shapes:
        n = 1
        for q in s:
            n *= q
        out.append(flat[off:off + n].reshape(s))
        off += n
    return out


def kernel(x, c, ada_w, ada_b, norm_g, ffn_w_in, ffn_w_out, sb_w_qkv, sb_w_o, sg_w_in, sg_ln_g, sg_w_s, sg_bias, sg_w_out, sc_w_in, sc_conv_w, sc_w_out, cb_w_qkv, cb_rel_bias, cb_w_o, loss_target, m_ada_w, m_ada_b, m_norm_g, m_ffn_w_in, m_ffn_w_out, m_sb_w_qkv, m_sb_w_o, m_sg_w_in, m_sg_ln_g, m_sg_w_s, m_sg_bias, m_sg_w_out, m_sc_w_in, m_sc_conv_w, m_sc_w_out, m_cb_w_qkv, m_cb_rel_bias, m_cb_w_o, v_ada_w, v_ada_b, v_norm_g, v_ffn_w_in, v_ffn_w_out, v_sb_w_qkv, v_sb_w_o, v_sg_w_in, v_sg_ln_g, v_sg_w_s, v_sg_bias, v_sg_w_out, v_sc_w_in, v_sc_conv_w, v_sc_w_out, v_cb_w_qkv, v_cb_rel_bias, v_cb_w_o):
    depth = ada_w.shape[0]
    d = D_MODEL
    xi, yi, ci = lax.axis_index("x"), lax.axis_index("y"), lax.axis_index("c")
    me = 4 * xi + 2 * yi + ci
    x0 = x[0]
    t = x0.shape[0]
    target = loss_target[0]

    c_g, ng, small, w_qkv0 = _all_gather(
        [jnp.pad(c, ((0, 7), (0, 0))), norm_g.reshape(depth * 4, d // N_DEV), _pack([sg_ln_g, sc_conv_w]),
         sb_w_qkv[0].astype(BF16)], "gather_setup")

    c_all = c_g[:, 0, :]
    na = ada_w.shape[2]
    b_cols = lax.dynamic_slice_in_dim(ada_b, me * na, na, axis=1)[:, None, :]
    mod_part = ada_fwd(c_all, ada_w, b_cols, "ada_fwd")
    mod_g = _all_gather([mod_part.reshape(depth * N_DEV, na)], "gather_mod")[0]
    mod_g = mod_g.reshape(N_DEV, depth, N_DEV, na)
    mod_me = lax.dynamic_index_in_dim(mod_g, me, axis=2, keepdims=False)
    mod = jnp.transpose(mod_me, (1, 0, 2)).reshape(depth, 6, 1, d)

    norm_full = jnp.transpose(ng, (1, 0, 2)).reshape(depth, 4, 1, d)
    small = small.reshape(N_DEV, -1)
    nl_g = sg_ln_g.shape[1]
    ln_full = small[:, :nl_g].reshape(1, N_DEV * nl_g)
    cwn = sc_conv_w.shape[2]
    cw_sh = small[:, nl_g:nl_g + 3 * cwn].reshape(N_DEV, 3, cwn)
    cw_full = jnp.transpose(cw_sh, (1, 0, 2)).reshape(3, d)
    cw_pad = jnp.pad(cw_full, ((0, 5), (0, 0)))

    bf = lambda a: a.astype(BF16)
    mixers = [
        [bf(sb_w_qkv[0]), bf(sb_w_o[0])],
        [bf(sg_w_in[0]), bf(sg_w_out[0])],
        [bf(sc_w_in[0]), bf(sc_w_out[0])],
        [bf(cb_w_qkv[0]), bf(cb_w_o[0])],
    ]
    shards = [[bf(ffn_w_in[i]), bf(ffn_w_out[i])] + mixers[i % 4] for i in range(depth)]
    gathered = [[None] * 4 for _ in range(depth)]
    gathered[0][2] = w_qkv0
    riding_shards = [shards[0][0], shards[0][1], shards[0][3]] + shards[1] + [shards[i][0] for i in range(2, depth)]

    bias_win = band_bias_window(cb_rel_bias[0])
    ws = sg_w_s[0]
    bias_t = jnp.pad(sg_bias[0].T, ((0, 0), (0, BLK - SG_GROUPS)))

    saved = []
    xcur = x0
    h = pre_fwd(x0, norm_full[0, 0], mod[0, 0], mod[0, 1], "L0_pre_m")
    for i in range(depth):
        mi = i % 4
        sh_m, sc_m, gt_m, sh_f, sc_f, gt_f = [mod[i, j] for j in range(6)]
        g0, g1, g2, g3 = [norm_full[i, j] for j in range(4)]
        tag = "L%d_" % i
        sv = {"x_in": xcur, "h_m": h}
        nxt = shards[i + 1] if 0 < i < depth - 1 else None
        if mi == 0:
            qkv = mm_cs(h, gathered[0][2], tag + "qkv", out_dtype=BF16)
            qkv_t = jnp.transpose(qkv.reshape(t // SB_GW, SB_GW, 3 * d), (0, 2, 1))
            o, cmass, riding = sb_fwd(qkv, qkv_t, riding_shards, tag + "sb_fwd")
            gathered[0][0], gathered[0][1], gathered[0][3] = riding[:3]
            gathered[1] = riding[3:7]
            for j in range(2, depth):
                gathered[j][0] = riding[5 + j]
            sv.update(qkv=qkv, qkv_t=qkv_t, o=o, cmass=cmass)
            mixed = o
        else:
            w_in = gathered[i][2]
            out_dtype = BF16 if mi == 3 else F32
            pre = mm_cs(h, w_in, tag + "mix_in", out_dtype=out_dtype)
            if mi == 1:
                mixed = sg_fwd(pre, ln_full, ws, bias_t, tag + "sg_fwd")
                sv.update(a=pre, yy=mixed)
            elif mi == 2:
                mixed = conv_fwd(pre, cw_pad, tag + "conv_fwd")
                sv.update(p=pre, gz=mixed)
            else:
                mixed = band_fwd(pre, bias_win, tag + "band_fwd")
                sv.update(qkv=pre, o=mixed)
        wfi, wfo, _, wmo = gathered[i]
        wfo4 = wfo.reshape(4, -1, d)
        y = mm(mixed, wmo.reshape(-1, d), tag + "mix_out")
        sv["y_m"] = y
        xmid, h2 = post_pre_fwd(xcur, y, g1, gt_m, g2, sh_f, sc_f, tag + "post_m_pre_f")
        sv["x_mid"] = xmid
        if nxt is not None:
            ag, au, s3, (gathered[i + 1][1], gathered[i + 1][3]) = ffn_in_swiglu(
                h2, wfi, tag + "ffn_in", riders=(_Gather, [nxt[1], nxt[3]]))
            y2, (gathered[i + 1][2],) = mm_rs(s3, wfo4, tag + "ffn_out", riders=(_Gather, [nxt[2]]))
        else:
            ag, au, s3, _ = ffn_in_swiglu(h2, wfi, tag + "ffn_in")
            y2 = mm_rs(s3, wfo4, tag + "ffn_out")
        sv.update(h_f=h2, ag=ag, au=au, s3=s3, y_f=y2)
        saved.append(sv)
        if i + 1 < depth:
            xcur, h = post_pre_fwd(xmid, y2, g3, gt_f, norm_full[i + 1, 0], mod[i + 1, 0], mod[i + 1, 1],
                                   tag + "post_f_pre_m")

    last = depth - 1
    dx, dy2, lpart, dgt_f, dg3 = post_loss_bwd(saved[last]["x_mid"], saved[last]["y_f"], norm_full[last, 3],
                                               mod[last, 5], target, "loss")
    loss = lax.psum(0.5 * jnp.sum(lpart) / d, ("x", "y", "c"))

    dmod_rows = [None] * depth
    dnorm_rows = [None] * depth
    big_pieces = [[None] * 4 for _ in range(depth)]
    pending_dwmi, pending_dwfi, pending_layers = None, [], []
    small_grads = {}
    for i in reversed(range(depth)):
        wfi, wfo, wmi, wmo = gathered[i]
        wfo4 = wfo.reshape(4, -1, d)
        wmo2 = wmo.reshape(-1, d)
        mi = i % 4
        sh_m, sc_m, gt_m, sh_f, sc_f, gt_f = [mod[i, j] for j in range(6)]
        g0, g1, g2, g3 = [norm_full[i, j] for j in range(4)]
        tag = "L%d_b_" % i
        sv = saved[i]
        da3, _ = ffn_out_dx_swiglu(dy2, wfo4, sv["ag"], sv["au"], tag + "ffn_out_dx")
        dwfo = mm_rs_dw(sv["s3"], dy2, tag + "ffn_out_dw", out_dtype=BF16).reshape(N_DEV, -1, d)
        dh2, (big_pieces[i][1],) = mm_cs_dx(da3, wfi, tag + "ffn_in_dx", act_major=True, riders=(_Exchange, [dwfo]))
        if pending_dwmi is not None:
            dwfi, (big_pieces[i + 1][2],) = mm_cs_dw(sv["h_f"], da3, tag + "ffn_in_dw", act_major=True,
                                                     out_dtype=BF16, riders=(_Exchange, [pending_dwmi]))
        else:
            dwfi = mm_cs_dw(sv["h_f"], da3, tag + "ffn_in_dw", act_major=True, out_dtype=BF16)
        dx, dy, dsh_f, dsc_f, dg2, dgt_m, dg1 = pre_post_bwd(
            dh2, sv["x_mid"], g2, sc_f, dx, sv["y_m"], g1, gt_m, tag + "pre_f_post_m")
        if mi == 0:
            do = mm_nt(dy, wmo2, tag + "wo_dx", out_dtype=BF16)
            dwmo = mm_tn(sv["o"], dy, tag + "wo_dw", out_dtype=BF16).reshape(N_DEV, -1, d)
            dq, dk, dv, ridden = sb_bwd(sv["qkv"], sv["qkv_t"], do, sv["cmass"],
                                        pending_dwfi + [dwfi, dwmo], tag + "sb_bwd")
            for n, j in enumerate(pending_layers):
                big_pieces[j][0] = ridden[n]
            big_pieces[0][0], big_pieces[0][3] = ridden[-2:]
            dmid = jnp.concatenate([dq, dk, dv], axis=1)
        elif mi == 1:
            dyy = mm_nt(dy, wmo2, tag + "sg_out_dx")
            dwmo = mm_tn(sv["yy"], dy, tag + "sg_out_dw", out_dtype=BF16).reshape(N_DEV, -1, d)
            dmid, dlng, dws, dbt = sg_bwd(sv["a"], dyy, ln_full, ws, bias_t, tag + "sg_bwd")
            small_grads.update(ln_g=jnp.sum(dlng, axis=0), w_s=dws, bias=dbt[:, :SG_GROUPS].T)
        elif mi == 2:
            dgz = mm_nt(dy, wmo2, tag + "sc_out_dx")
            dwmo = mm_tn(sv["gz"], dy, tag + "sc_out_dw", out_dtype=BF16).reshape(N_DEV, -1, d)
            dmid, dcw = conv_bwd(sv["p"], dgz, cw_pad, tag + "conv_bwd")
            small_grads.update(conv_w=jnp.sum(dcw, axis=1))
        else:
            do = mm_nt(dy, wmo2, tag + "wo_dx", out_dtype=BF16)
            dwmo = mm_tn(sv["o"], dy, tag + "wo_dw", out_dtype=BF16).reshape(N_DEV, -1, d)
            dq, dk, dv, dwin, (big_pieces[i][0], big_pieces[i][3]) = band_bwd(
                sv["qkv"], do, bias_win, tag + "band_bwd", riders=(_Exchange, [dwfi, dwmo]))
            dmid = jnp.concatenate([dq, dk, dv], axis=1)
            small_grads.update(rel_bias=band_bias_window_grad(dwin))
        if mi in (1, 2):
            dh, (big_pieces[i][3],) = mm_cs_dx(dmid, wmi, tag + "mix_in_dx", riders=(_Exchange, [dwmo]))
            pending_dwfi.append(dwfi)
            pending_layers.append(i)
        else:
            dh = mm_cs_dx(dmid, wmi, tag + "mix_in_dx")
        pending_dwmi = mm_cs_dw(sv["h_m"], dmid, tag + "mix_in_dw", out_dtype=BF16)
        if i > 0:
            dx, dy2_prev, dsh_m, dsc_m, dg0, dgt_f_prev, dg3_prev = pre_post_bwd(
                dh, sv["x_in"], g0, sc_m, dx, saved[i - 1]["y_f"], norm_full[i - 1, 3], mod[i - 1, 5],
                tag + "pre_m_post_f")
        else:
            dx, dsh_m, dsc_m, dg0 = pre_bwd(dh, sv["x_in"], g0, sc_m, dx, tag + "pre_m")
        dmod_rows[i] = jnp.stack([jnp.sum(q, axis=0) for q in (dsh_m, dsc_m, dgt_m, dsh_f, dsc_f, dgt_f)])
        dnorm_rows[i] = jnp.stack([jnp.sum(q, axis=0) for q in (dg0, dg1, dg2, dg3)])
        if i > 0:
            dy2, dgt_f, dg3 = dy2_prev, dgt_f_prev, dg3_prev
    grad_x = dx[None]

    out_g, out_d, out_m, out_v = {}, {}, {}, {}

    def upd(name, pieces, w, m, v, riders=None):
        rows_cols = (len(pieces) * pieces[0].shape[1], pieces[0].shape[2])
        res, ridden = adamw(pieces, w.reshape(rows_cols), m.reshape(rows_cols), v.reshape(rows_cols),
                            "adamw_" + name, riders)
        out_g[name], out_d[name], out_m[name], out_v[name] = [q.reshape(w.shape) for q in res]
        return ridden

    dmod_mine = jnp.stack(dmod_rows).reshape(depth, 6 * d)
    dnorm_mine = jnp.stack(dnorm_rows)
    small_list = [dnorm_mine, small_grads["ln_g"], small_grads["bias"], small_grads["conv_w"],
                  small_grads["rel_bias"]]
    small_shapes = [dmod_mine.shape] + [a.shape for a in small_list]
    slab = _pack([dmod_mine] + small_list)
    big_pieces[0][2], = upd("ffn_w_in", [big_pieces[i][0] for i in range(depth)], ffn_w_in, m_ffn_w_in, v_ffn_w_in,
                            riders=(_Exchange, [pending_dwmi]))
    slab_g, ws_g = upd("ffn_w_out", [big_pieces[i][1] for i in range(depth)], ffn_w_out, m_ffn_w_out, v_ffn_w_out,
                       riders=(_Gather, [slab, small_grads["w_s"].reshape(-1, BLK)]))
    upd("sg_w_s", [ws_g], sg_w_s, m_sg_w_s, v_sg_w_s)
    upd("sb_w_qkv", [big_pieces[0][2]], sb_w_qkv, m_sb_w_qkv, v_sb_w_qkv)
    upd("sb_w_o", [big_pieces[0][3]], sb_w_o, m_sb_w_o, v_sb_w_o)
    upd("sg_w_in", [big_pieces[1][2]], sg_w_in, m_sg_w_in, v_sg_w_in)
    upd("sg_w_out", [big_pieces[1][3]], sg_w_out, m_sg_w_out, v_sg_w_out)
    upd("sc_w_in", [big_pieces[2][2]], sc_w_in, m_sc_w_in, v_sc_w_in)
    upd("sc_w_out", [big_pieces[2][3]], sc_w_out, m_sc_w_out, v_sc_w_out)
    upd("cb_w_qkv", [big_pieces[3][2]], cb_w_qkv, m_cb_w_qkv, v_cb_w_qkv)
    upd("cb_w_o", [big_pieces[3][3]], cb_w_o, m_cb_w_o, v_cb_w_o)

    tot = sum_pieces(slab_g, "sum_small_grads")
    g_ada_b_full, g_norm, g_ln, g_sbias, g_cw, g_rb = _unpack(tot, small_shapes)
    dmod_all = slab_g.reshape(N_DEV, -1)[:, :depth * 6 * d].reshape(N_DEV, depth, 6 * d)
    dmod_cols = lax.dynamic_slice_in_dim(dmod_all, me * na, na, axis=2)
    g_ada_w = ada_bwd(c_all, jnp.transpose(dmod_cols, (1, 0, 2)), "ada_bwd")

    nsh = d // N_DEV
    g_norm_sh = lax.dynamic_slice_in_dim(g_norm, me * nsh, nsh, axis=2)
    g_ln_sh = lax.dynamic_slice_in_dim(g_ln.reshape(1, -1), me * nl_g, nl_g, axis=1)
    g_cw_sh = lax.dynamic_slice_in_dim(g_cw, me * cwn, cwn, axis=1)[None]

    upd("ada_w", [g_ada_w.reshape(1, depth * d, na)], ada_w, m_ada_w, v_ada_w)

    small_names = ["ada_b", "norm_g", "sg_ln_g", "sg_bias", "sc_conv_w", "cb_rel_bias"]
    small_g = [g_ada_b_full, g_norm_sh, g_ln_sh, g_sbias[None], g_cw_sh, g_rb[None]]
    small_w = [ada_b, norm_g, sg_ln_g, sg_bias, sc_conv_w, cb_rel_bias]
    small_m = [m_ada_b, m_norm_g, m_sg_ln_g, m_sg_bias, m_sc_conv_w, m_cb_rel_bias]
    small_v = [v_ada_b, v_norm_g, v_sg_ln_g, v_sg_bias, v_sc_conv_w, v_cb_rel_bias]
    shapes = [w.shape for w in small_w]
    res, _ = adamw([_pack(small_g)[None]], _pack(small_w), _pack(small_m), _pack(small_v), "adamw_small")
    for nm_, gs, ds_, ms, vs in zip(small_names, *[_unpack(r, shapes) for r in res]):
        out_g[nm_], out_d[nm_], out_m[nm_], out_v[nm_] = gs, ds_, ms, vs

    order = ["ada_w", "ada_b", "norm_g", "ffn_w_in", "ffn_w_out", "sb_w_qkv", "sb_w_o", "sg_w_in", "sg_ln_g",
             "sg_w_s", "sg_bias", "sg_w_out", "sc_w_in", "sc_conv_w", "sc_w_out", "cb_w_qkv", "cb_rel_bias", "cb_w_o"]
    return (loss, grad_x, *[out_g[n] for n in order], *[out_d[n] for n in order],
            *[out_m[n] for n in order], *[out_v[n] for n in order])
```

```python
import jax
import jax.numpy as jnp
from jax import lax
from jax.experimental import pallas as pl
from jax.experimental.pallas import tpu as pltpu

F32 = jnp.float32
BF16 = jnp.bfloat16
MESH = pl.DeviceIdType.MESH

N_DEV = 8
D_MODEL = 1024
N_HEADS = 16
HEAD_DIM = 64
QK_SCALE = HEAD_DIM ** -0.5
BLK = 128
BAND_BLOCKS = 5
BAND_W = BAND_BLOCKS * BLK
REL_CLIP = 128
EPS = 1e-6
NEG = -1e30
GELU_C0 = 0.7978845608028654
GELU_C1 = 0.044715
ADAM_LR = 0.001
ADAM_B1 = 0.9
ADAM_B2 = 0.999
ADAM_EPS = 1e-08
ADAM_WD = 0.01
ADAM_STEP = 10
VMEM_LIMIT = 56 * 1024 * 1024


def _call(body, **kw):
    return pl.pallas_call(body, **kw)


def _params(*sem):
    return pltpu.CompilerParams(dimension_semantics=sem, vmem_limit_bytes=VMEM_LIMIT)


def _sds(shape, dtype):
    return jax.ShapeDtypeStruct(tuple(shape), dtype)


def _row_tile(t):
    return min(512, t)


def _me():
    x, y, c = lax.axis_index("x"), lax.axis_index("y"), lax.axis_index("c")
    return x, y, c


def _all_gather(arrs, name):
    n = len(arrs)

    def body(*refs):
        gather = _Gather(refs[:n], refs[n:2 * n], *refs[2 * n:])
        gather.start()
        gather.forward()
        gather.finish()

    any_spec = pl.BlockSpec(memory_space=pl.ANY)
    outs = _call(
        body,
        name=name,
        out_shape=_Gather.out_shapes(arrs),
        in_specs=[any_spec] * n,
        out_specs=[any_spec] * n,
        scratch_shapes=_comm_sems(n),
    )(*arrs)
    return list(outs)


def _comm_sems(n):
    if n == 0:
        return []
    return [pltpu.SemaphoreType.DMA((n, 7)), pltpu.SemaphoreType.DMA((n, 7)), pltpu.SemaphoreType.DMA((n,))]


class _Gather:
    def __init__(self, x_refs, o_refs, send_sems, recv_sems, local_sems):
        self.x_refs, self.o_refs = x_refs, o_refs
        self.send_sems, self.recv_sems, self.local_sems = send_sems, recv_sems, local_sems
        x, y, c = _me()
        self.c = c
        self.me, self.sibling = (x, y, c), (x, y, 1 - c)
        self.chips = [(1 - x, y), (x, 1 - y), (1 - x, 1 - y)]

    @staticmethod
    def out_shapes(arrs):
        return [_sds((N_DEV,) + a.shape, a.dtype) for a in arrs]

    def rows(self, a, block):
        px, py, pc = block
        return self.o_refs[a].at[4 * px + 2 * py + pc]

    def copy(self, a, k, block, to, own=False):
        return pltpu.make_async_remote_copy(
            src_ref=self.x_refs[a] if own else self.rows(a, block),
            dst_ref=self.rows(a, block),
            send_sem=self.send_sems.at[a, k],
            recv_sem=self.recv_sems.at[a, k],
            device_id=to,
            device_id_type=MESH,
        )

    def local(self, a):
        return pltpu.make_async_copy(self.x_refs[a], self.rows(a, self.me), self.local_sems.at[a])

    def first(self, a):
        cps = [self.copy(a, 0, self.me, self.sibling, own=True)]
        return cps + [self.copy(a, 1 + j, self.me, (*chip, self.c), own=True) for j, chip in enumerate(self.chips)]

    def passed(self, a):
        return [self.copy(a, 4 + j, (*chip, self.c), self.sibling) for j, chip in enumerate(self.chips)]

    def start(self):
        for a in range(len(self.x_refs)):
            self.local(a).start()
            for cp in self.first(a):
                cp.start()

    def forward(self):
        for a in range(len(self.x_refs)):
            passed = self.passed(a)
            for j, chip in enumerate(self.chips):
                self.copy(a, 1 + j, (*chip, self.c), self.me).wait_recv()
                passed[j].start()

    def finish(self):
        for a in range(len(self.x_refs)):
            self.copy(a, 0, self.sibling, self.me).wait_recv()
            for j, chip in enumerate(self.chips):
                self.copy(a, 4 + j, (*chip, 1 - self.c), self.me).wait_recv()
            for cp in self.first(a) + self.passed(a):
                cp.wait_send()
            self.local(a).wait()


class _Exchange:
    def __init__(self, x_refs, o_refs, send_sems, recv_sems, local_sems):
        self.x_refs, self.o_refs = x_refs, o_refs
        self.send_sems, self.recv_sems, self.local_sems = send_sems, recv_sems, local_sems
        x, y, c = _me()
        self.me = 4 * x + 2 * y + c
        self.peers = []
        for k in range(1, N_DEV):
            px = 1 - x if k & 4 else x
            py = 1 - y if k & 2 else y
            pc = 1 - c if k & 1 else c
            self.peers.append((px, py, pc))

    @staticmethod
    def out_shapes(arrs):
        return [_sds(a.shape, a.dtype) for a in arrs]

    def local(self, a):
        return pltpu.make_async_copy(self.x_refs[a].at[self.me], self.o_refs[a].at[self.me], self.local_sems.at[a])

    def copy(self, a, k, send):
        px, py, pc = self.peers[k]
        peer = 4 * px + 2 * py + pc
        return pltpu.make_async_remote_copy(
            src_ref=self.x_refs[a].at[peer],
            dst_ref=self.o_refs[a].at[self.me if send else peer],
            send_sem=self.send_sems.at[a, k], recv_sem=self.recv_sems.at[a, k],
            device_id=(px, py, pc), device_id_type=MESH)

    def start(self):
        for a in range(len(self.x_refs)):
            self.local(a).start()
            for k in range(N_DEV - 1):
                self.copy(a, k, True).start()

    def finish(self):
        for a in range(len(self.x_refs)):
            for k in range(N_DEV - 1):
                self.copy(a, k, False).wait_recv()
            for k in range(N_DEV - 1):
                self.copy(a, k, True).wait_send()
            self.local(a).wait()


def _all_to_all(arrs, name):
    n = len(arrs)

    def body(*refs):
        exchange = _Exchange(refs[:n], refs[n:2 * n], *refs[2 * n:])
        exchange.start()
        exchange.finish()

    any_spec = pl.BlockSpec(memory_space=pl.ANY)
    outs = _call(
        body,
        name=name,
        out_shape=[_sds(a.shape, a.dtype) for a in arrs],
        in_specs=[any_spec] * n,
        out_specs=[any_spec] * n,
        scratch_shapes=_comm_sems(n),
    )(*arrs)
    return list(outs)


NN = (((1,), (0,)), ((), ()))
NT = (((1,), (1,)), ((), ()))
TN = (((0,), (0,)), ((), ()))


def _all_of(conds):
    out = conds[0]
    for cond in conds[1:]:
        out = out & cond
    return out


def _rider_call(body, operands, riders, *, name, out_shape, grid, in_specs, out_specs, scratch_shapes, sem):
    if not riders or not riders[1]:
        res = _call(body, name=name, out_shape=out_shape, grid=grid, in_specs=in_specs, out_specs=out_specs,
                    scratch_shapes=list(scratch_shapes), compiler_params=_params(*sem))(*operands)
        return list(res), []
    cls, arrs = riders
    nr, ni, no, ns = len(arrs), len(in_specs), len(out_specs), len(scratch_shapes)
    forward_at = (3 * grid[0]) // 4 if grid[0] >= 4 else None

    def wrapped(*refs):
        ids = [pl.program_id(ax) for ax in range(len(grid))]
        comm = cls(refs[ni:ni + nr], refs[ni + nr + no:ni + 2 * nr + no], *refs[ni + 2 * nr + no + ns:])
        pl.when(_all_of([i == 0 for i in ids]))(comm.start)
        if cls is _Gather and forward_at is not None:
            pl.when(_all_of([ids[0] == forward_at] + [i == 0 for i in ids[1:]]))(comm.forward)
        body(*refs[:ni], *refs[ni + nr:ni + nr + no], *refs[ni + 2 * nr + no:ni + 2 * nr + no + ns])

        def end():
            if cls is _Gather and forward_at is None:
                comm.forward()
            comm.finish()

        pl.when(_all_of([i == g - 1 for i, g in zip(ids, grid)]))(end)

    any_spec = pl.BlockSpec(memory_space=pl.ANY)
    res = _call(wrapped, name=name, out_shape=list(out_shape) + cls.out_shapes(arrs), grid=grid,
                in_specs=list(in_specs) + [any_spec] * nr, out_specs=list(out_specs) + [any_spec] * nr,
                scratch_shapes=list(scratch_shapes) + _comm_sems(nr),
                compiler_params=_params(*(["arbitrary"] * len(grid))))(*operands, *arrs)
    return list(res[:no]), list(res[no:])


def _gemm(a, b, out_shape, out_dtype, grid, a_spec, b_spec, o_spec, acc_shape, dims, name, riders=None):
    nk = grid[2]

    if nk == 1:
        def body(a_ref, b_ref, o_ref):
            r = lax.dot_general(a_ref[...].astype(BF16), b_ref[...].astype(BF16), dims,
                                preferred_element_type=F32)
            o_ref[...] = r.astype(o_ref.dtype)
        scratch = []
    else:
        def body(a_ref, b_ref, o_ref, acc_ref):
            k = pl.program_id(2)

            @pl.when(k == 0)
            def _():
                acc_ref[...] = jnp.zeros_like(acc_ref)

            acc_ref[...] += lax.dot_general(a_ref[...].astype(BF16), b_ref[...].astype(BF16), dims,
                                            preferred_element_type=F32)

            @pl.when(k == nk - 1)
            def _():
                o_ref[...] = acc_ref[...].astype(o_ref.dtype)
        scratch = [pltpu.VMEM(acc_shape, F32)]

    res, ridden = _rider_call(
        body, (a, b), riders, name=name, out_shape=[_sds(out_shape, out_dtype)], grid=grid,
        in_specs=[a_spec, b_spec], out_specs=[o_spec], scratch_shapes=scratch,
        sem=("parallel", "parallel", "arbitrary"))
    return (res[0], ridden) if riders else res[0]


def _div_tile(n, want):
    if n <= want:
        return n
    t = want - want % 128
    while n % t:
        t -= 128
    return t


def mm(a, b, name, out_dtype=F32, tm=512, tn=1024, tk=1024, riders=None):
    m, k = a.shape
    n = b.shape[1]
    tm, tn, tk = _div_tile(m, tm), _div_tile(n, tn), _div_tile(k, tk)
    return _gemm(a, b, (m, n), out_dtype, (m // tm, n // tn, k // tk),
                 pl.BlockSpec((tm, tk), lambda i, j, kk: (i, kk)),
                 pl.BlockSpec((tk, tn), lambda i, j, kk: (kk, j)),
                 pl.BlockSpec((tm, tn), lambda i, j, kk: (i, j)),
                 (tm, tn), NN, name, riders)


def mm_nt(a, b, name, out_dtype=F32, tm=512, tn=1024, tk=1024):
    m, n = a.shape
    k = b.shape[0]
    tm, tk_out, tred = _div_tile(m, tm), _div_tile(k, tn), _div_tile(n, tk)
    return _gemm(a, b, (m, k), out_dtype, (m // tm, k // tk_out, n // tred),
                 pl.BlockSpec((tm, tred), lambda i, j, kk: (i, kk)),
                 pl.BlockSpec((tk_out, tred), lambda i, j, kk: (j, kk)),
                 pl.BlockSpec((tm, tk_out), lambda i, j, kk: (i, j)),
                 (tm, tk_out), NT, name)


def mm_tn(a, b, name, out_dtype=F32, tm=512, tn=1024, tk=1024):
    m, k = a.shape
    n = b.shape[1]
    tk_out, tn, tred = _div_tile(k, tk), _div_tile(n, tn), _div_tile(m, tm)
    return _gemm(a, b, (k, n), out_dtype, (k // tk_out, n // tn, m // tred),
                 pl.BlockSpec((tred, tk_out), lambda i, j, kk: (kk, i)),
                 pl.BlockSpec((tred, tn), lambda i, j, kk: (kk, j)),
                 pl.BlockSpec((tk_out, tn), lambda i, j, kk: (i, j)),
                 (tk_out, tn), TN, name)


def mm_cs(a, wg, name, act_major=False, out_dtype=F32, tm=1024, riders=None):
    m, k = a.shape
    s, _, n = wg.shape
    tm = _div_tile(m, tm)
    if act_major:
        out_shape, o_spec = (s, m, n), pl.BlockSpec((None, tm, n), lambda i, j, kk: (j, i, 0))
    else:
        out_shape, o_spec = (m, s * n), pl.BlockSpec((tm, n), lambda i, j, kk: (i, j))
    return _gemm(a, wg, out_shape, out_dtype, (m // tm, s, 1),
                 pl.BlockSpec((tm, k), lambda i, j, kk: (i, 0)),
                 pl.BlockSpec((None, k, n), lambda i, j, kk: (j, 0, 0)),
                 o_spec, (tm, n), NN, name, riders)


def mm_cs_dx(da, wg, name, act_major=False, out_dtype=F32, tm=1024, riders=None):
    s, k, n = wg.shape
    m = da.shape[1] if act_major else da.shape[0]
    tm = _div_tile(m, tm)
    if act_major:
        a_spec = pl.BlockSpec((None, tm, n), lambda i, j, kk: (kk, i, 0))
    else:
        a_spec = pl.BlockSpec((tm, n), lambda i, j, kk: (i, kk))
    return _gemm(da, wg, (m, k), out_dtype, (m // tm, 1, s), a_spec,
                 pl.BlockSpec((None, k, n), lambda i, j, kk: (kk, 0, 0)),
                 pl.BlockSpec((tm, k), lambda i, j, kk: (i, 0)),
                 (tm, k), NT, name, riders)


def mm_cs_dw(at, da, name, act_major=False, out_dtype=F32, tm=1024, riders=None):
    k, m = at.shape
    if act_major:
        s, _, n = da.shape
    else:
        s, n = N_DEV, da.shape[1] // N_DEV
    tm = _div_tile(m, tm)
    if act_major:
        b_spec = pl.BlockSpec((None, tm, n), lambda i, j, kk: (i, kk, 0))
    else:
        b_spec = pl.BlockSpec((tm, n), lambda i, j, kk: (kk, i))
    return _gemm(at, da, (s, k, n), out_dtype, (s, 1, m // tm),
                 pl.BlockSpec((k, tm), lambda i, j, kk: (0, kk)), b_spec,
                 pl.BlockSpec((None, k, n), lambda i, j, kk: (i, 0, 0)),
                 (k, n), NN, name, riders)


def mm_rs(s3, w3, name, out_dtype=F32, tm=1024, riders=None):
    s, m, n = s3.shape
    nn = w3.shape[2]
    tm = _div_tile(m, tm)
    return _gemm(s3, w3, (m, nn), out_dtype, (m // tm, 1, s),
                 pl.BlockSpec((None, tm, n), lambda i, j, kk: (kk, i, 0)),
                 pl.BlockSpec((None, n, nn), lambda i, j, kk: (kk, 0, 0)),
                 pl.BlockSpec((tm, nn), lambda i, j, kk: (i, 0)),
                 (tm, nn), NN, name, riders)


def mm_rs_dx(dy, w3, name, out_dtype=F32, tm=1024):
    m, nn = dy.shape
    s, n, _ = w3.shape
    tm = _div_tile(m, tm)
    return _gemm(dy, w3, (s, m, n), out_dtype, (m // tm, s, 1),
                 pl.BlockSpec((tm, nn), lambda i, j, kk: (i, 0)),
                 pl.BlockSpec((None, n, nn), lambda i, j, kk: (j, 0, 0)),
                 pl.BlockSpec((None, tm, n), lambda i, j, kk: (j, i, 0)),
                 (tm, n), NT, name)


def mm_rs_dw(s3, dy, name, out_dtype=F32, tm=1024):
    s, m, n = s3.shape
    nn = dy.shape[1]
    tm = _div_tile(m, tm)
    return _gemm(s3, dy, (s, n, nn), out_dtype, (s, 1, m // tm),
                 pl.BlockSpec((None, tm, n), lambda i, j, kk: (i, kk, 0)),
                 pl.BlockSpec((tm, nn), lambda i, j, kk: (kk, 0)),
                 pl.BlockSpec((None, n, nn), lambda i, j, kk: (i, 0, 0)),
                 (n, nn), TN, name)


def _colsum8(v):
    tr, d = v.shape
    return v.reshape(tr // 8, 8, d).sum(axis=0)


def _rstd(v):
    return lax.rsqrt(jnp.mean(v * v, axis=-1, keepdims=True) + EPS)


def _vec_spec(d):
    return pl.BlockSpec((1, d), lambda i: (0, 0))


def _acc_spec(d):
    return pl.BlockSpec((8, d), lambda i: (0, 0))


def _pre_rows(xv, g, shift, scale):
    return ((xv * _rstd(xv)) * g) * (1 + scale) + shift


def _post_rows(xv, yv, g, gate):
    return xv + gate * ((yv * _rstd(yv)) * g)


def _post_bwd_rows(dxv, yv, g, gate):
    r = _rstd(yv)
    yhat = yv * r
    dgate = _colsum8(dxv * (yhat * g))
    dyn = gate * dxv
    dg = _colsum8(dyn * yhat)
    dyhat = dyn * g
    dy = r * (dyhat - yhat * jnp.mean(dyhat * yhat, axis=-1, keepdims=True))
    return dy, dgate, dg


def _pre_bwd_rows(dhv, xv, g, scale, dxn):
    r = _rstd(xv)
    xhat = xv * r
    dshift = _colsum8(dhv)
    dscale = _colsum8(dhv * (xhat * g))
    dmod = dhv * (1 + scale)
    dg = _colsum8(dmod * xhat)
    dxhat = dmod * g
    dx = r * (dxhat - xhat * jnp.mean(dxhat * xhat, axis=-1, keepdims=True)) + dxn
    return dx, dshift, dscale, dg


def _row_call(body, name, t, d, rows_in, vecs_in, rows_out, n_acc, n_transposed=0):
    tr = _row_tile(t)
    nri, nvi, nro = len(rows_in), len(vecs_in), len(rows_out)

    def wrapped(*refs):
        accs = refs[nri + nvi + nro + n_transposed:]
        if n_acc:
            @pl.when(pl.program_id(0) == 0)
            def _():
                for acc in accs:
                    acc[...] = jnp.zeros_like(acc)
        body(*refs)

    row = pl.BlockSpec((tr, d), lambda i: (i, 0))
    return _call(wrapped, name=name,
                 out_shape=([_sds((t, d), dt) for dt in rows_out] + [_sds((d, t), BF16)] * n_transposed
                            + [_sds((8, d), F32)] * n_acc),
                 grid=(t // tr,), in_specs=[row] * nri + [_vec_spec(d)] * nvi,
                 out_specs=([row] * nro + [pl.BlockSpec((d, tr), lambda i: (0, i))] * n_transposed
                            + [_acc_spec(d)] * n_acc),
                 compiler_params=_params("arbitrary" if n_acc else "parallel"))(*rows_in, *vecs_in)


def pre_fwd(x, g, shift, scale, name):
    def body(x_ref, g_ref, sh_ref, sc_ref, h_ref, ht_ref):
        h = _pre_rows(x_ref[...], g_ref[...], sh_ref[...], sc_ref[...])
        h_ref[...] = h.astype(BF16)
        ht_ref[...] = h.T.astype(BF16)

    return _row_call(body, name, *x.shape, [x], [g, shift, scale], [BF16], 0, n_transposed=1)


def post_pre_fwd(x, y, g_post, gate, g_pre, shift, scale, name):
    def body(x_ref, y_ref, gp_ref, gt_ref, g_ref, sh_ref, sc_ref, xn_ref, h_ref, ht_ref):
        xn = _post_rows(x_ref[...], y_ref[...], gp_ref[...], gt_ref[...])
        xn_ref[...] = xn
        h = _pre_rows(xn, g_ref[...], sh_ref[...], sc_ref[...])
        h_ref[...] = h.astype(BF16)
        ht_ref[...] = h.T.astype(BF16)

    return _row_call(body, name, *x.shape, [x, y], [g_post, gate, g_pre, shift, scale], [F32, BF16], 0,
                     n_transposed=1)


def post_loss_bwd(x, y, g, gate, target, name):
    d = x.shape[1]

    def body(x_ref, y_ref, t_ref, g_ref, gt_ref, dx_ref, dy_ref, l_ref, dgate_ref, dg_ref):
        yv, gv, gate_v = y_ref[...], g_ref[...], gt_ref[...]
        err = _post_rows(x_ref[...], yv, gv, gate_v) - t_ref[...]
        l_ref[...] += _colsum8(err * err)
        dxv = err * (1.0 / d)
        dx_ref[...] = dxv
        dy, dgate, dg = _post_bwd_rows(dxv, yv, gv, gate_v)
        dy_ref[...] = dy.astype(BF16)
        dgate_ref[...] += dgate
        dg_ref[...] += dg

    return _row_call(body, name, *x.shape, [x, y, target], [g, gate], [F32, BF16], 3)


def pre_post_bwd(dh, x, g_pre, scale, dxn, y, g_post, gate, name):
    def body(dh_ref, x_ref, dxn_ref, y_ref, g_ref, sc_ref, gp_ref, gt_ref,
             dx_ref, dy_ref, dsh_ref, dsc_ref, dg_ref, dgate_ref, dgp_ref):
        dx, dsh, dsc, dg = _pre_bwd_rows(dh_ref[...].astype(F32), x_ref[...], g_ref[...], sc_ref[...], dxn_ref[...])
        dx_ref[...] = dx
        dsh_ref[...] += dsh
        dsc_ref[...] += dsc
        dg_ref[...] += dg
        dy, dgate, dgp = _post_bwd_rows(dx, y_ref[...], gp_ref[...], gt_ref[...])
        dy_ref[...] = dy.astype(BF16)
        dgate_ref[...] += dgate
        dgp_ref[...] += dgp

    return _row_call(body, name, *x.shape, [dh, x, dxn, y], [g_pre, scale, g_post, gate], [F32, BF16], 5)


def pre_bwd(dh, x, g, scale, dxn, name):
    def body(dh_ref, x_ref, dxn_ref, g_ref, sc_ref, dx_ref, dsh_ref, dsc_ref, dg_ref):
        dx, dsh, dsc, dg = _pre_bwd_rows(dh_ref[...].astype(F32), x_ref[...], g_ref[...], sc_ref[...], dxn_ref[...])
        dx_ref[...] = dx
        dsh_ref[...] += dsh
        dsc_ref[...] += dsc
        dg_ref[...] += dg

    return _row_call(body, name, *x.shape, [dh, x, dxn], [g, scale], [F32], 3)


def _sigmoid(x):
    return 1.0 / (1.0 + jnp.exp(-x))


def ffn_in_swiglu(h, wg, name, tm=1024, riders=None):
    m, k = h.shape
    s, _, n = wg.shape
    half = s // 2
    tm = _div_tile(m, tm)

    def body(h_ref, wg_ref, wu_ref, g_ref, u_ref, s_ref):
        hv = h_ref[...]
        g = jnp.dot(hv, wg_ref[...], preferred_element_type=F32)
        u = jnp.dot(hv, wu_ref[...], preferred_element_type=F32)
        g_ref[...] = g
        u_ref[...] = u
        s_ref[...] = ((g * _sigmoid(g)) * u).astype(BF16)

    act = pl.BlockSpec((None, tm, n), lambda i, j: (j, i, 0))
    res, ridden = _rider_call(
        body, (h, wg, wg), riders, name=name,
        out_shape=[_sds((half, m, n), F32), _sds((half, m, n), F32), _sds((half, m, n), BF16)],
        grid=(m // tm, half),
        in_specs=[pl.BlockSpec((tm, k), lambda i, j: (i, 0)),
                  pl.BlockSpec((None, k, n), lambda i, j: (j, 0, 0)),
                  pl.BlockSpec((None, k, n), lambda i, j: (j + half, 0, 0))],
        out_specs=[act, act, act], scratch_shapes=[], sem=("parallel", "parallel"))
    return res[0], res[1], res[2], ridden


def ffn_out_dx_swiglu(dy, w4, gate, up, name, tm=1024, riders=None):
    m, nn = dy.shape
    half, n, _ = w4.shape
    tm = _div_tile(m, tm)

    def body(dy_ref, w_ref, g_ref, u_ref, o_ref):
        ds = lax.dot_general(dy_ref[...], w_ref[...], NT, preferred_element_type=F32)
        g, u = g_ref[...], u_ref[...]
        sig = _sigmoid(g)
        o_ref[0] = (ds * u * (sig * (1 + g * (1 - sig)))).astype(BF16)
        o_ref[1] = (ds * (g * sig)).astype(BF16)

    act = pl.BlockSpec((None, tm, n), lambda i, j: (j, i, 0))
    res, ridden = _rider_call(
        body, (dy, w4, gate, up), riders, name=name, out_shape=[_sds((2, half, m, n), BF16)],
        grid=(m // tm, half),
        in_specs=[pl.BlockSpec((tm, nn), lambda i, j: (i, 0)),
                  pl.BlockSpec((None, n, nn), lambda i, j: (j, 0, 0)), act, act],
        out_specs=[pl.BlockSpec((2, None, tm, n), lambda i, j: (0, j, i, 0))],
        scratch_shapes=[], sem=("parallel", "parallel"))
    return res[0].reshape(2 * half, m, n), ridden


def _split_hi_lo(v):
    hi = v.astype(BF16)
    lo = (v - hi.astype(F32)).astype(BF16)
    return hi, lo


SB_G = 2
SB_EXP_ZERO = 104.0
SB_UNSEEN = 3e38
SB_GW = SB_G * BLK


def _sb_specs(t):
    nq = t // BLK
    npair = N_HEADS // 2
    q_spec = pl.BlockSpec((BLK, BLK), lambda p, qb: (qb, p))
    k_spec = pl.BlockSpec((t, BLK), lambda p, qb: (0, npair + p))
    v_spec = pl.BlockSpec((t, BLK), lambda p, qb: (0, 2 * npair + p))
    kt_spec = pl.BlockSpec((t // SB_GW, BLK, SB_GW), lambda p, qb: (0, npair + p, 0))
    vt_spec = pl.BlockSpec((t // SB_GW, BLK, SB_GW), lambda p, qb: (0, 2 * npair + p, 0))
    c_spec = pl.BlockSpec((None, nq, 8, 2 * BLK), lambda p, qb: (p, 0, 0, qb))
    return nq, npair, q_spec, k_spec, v_spec, kt_spec, vt_spec, c_spec


def _sb_consts():
    row = lax.broadcasted_iota(jnp.int32, (BLK, BLK), 0)
    col = lax.broadcasted_iota(jnp.int32, (BLK, BLK), 1)
    lane0 = (col < HEAD_DIM).astype(F32)
    sub0 = (row < HEAD_DIM).astype(F32)
    return row, col, lane0, sub0


def _sb_valid(ks, qb):
    row = lax.broadcasted_iota(jnp.int32, (SB_GW, 2 * BLK), 0)
    col = lax.broadcasted_iota(jnp.int32, (SB_GW, 2 * BLK), 1)
    return (ks + row) < (qb * BLK + (col & (BLK - 1)))


def _blocks_on_lanes(v4):
    return jnp.concatenate([v4[b * BLK:(b + 1) * BLK] for b in range(SB_G)], axis=1)


def _tri2_dot(tri2, v):
    hi, lo = _split_hi_lo(v)
    return jnp.dot(tri2, jnp.concatenate([hi, lo], axis=0), preferred_element_type=F32)


def _sb_pair_loop(first, count, step, group, skip, carry):
    def pair(it, cy):
        g1 = first + 2 * step * it
        cy = group(g1, 0, 1, cy)
        return lax.cond(2 * it + 1 < count, lambda c: group(g1 + step, 1, 0, c), skip, cy)
    return lax.fori_loop(0, (count + 1) // 2, pair, carry)


def sb_fwd(qkv, qkv_t, riders, name):
    t = qkv.shape[0]
    assert t % SB_GW == 0
    nq, npair, q_spec, k_spec, _, _, vt_spec, c_spec = _sb_specs(t)
    nr = len(riders)

    def body(*refs):
        q_ref, k_ref, vt_ref = refs[:3]
        o_ref, c_ref = refs[3 + nr:5 + nr]
        oacc, zbuf0, zbuf1, kmax = refs[5 + 2 * nr:9 + 2 * nr]
        pp = pl.program_id(0)
        qb = pl.program_id(1)
        if nr:
            gather = _Gather(refs[3:3 + nr], refs[5 + nr:5 + 2 * nr], *refs[9 + 2 * nr:])
            pl.when((pp == 0) & (qb == 0))(gather.start)
            pl.when((pp == npair - 2) & (qb == 0))(gather.forward)
        _sb_fwd_step(q_ref, k_ref, vt_ref, o_ref, c_ref, oacc, zbuf0, zbuf1, kmax, qb)
        if nr:
            pl.when((pp == npair - 1) & (qb == nq - 1))(gather.finish)

    any_spec = pl.BlockSpec(memory_space=pl.ANY)
    outs = _call(
        body, name=name,
        out_shape=[_sds((t, D_MODEL), BF16), _sds((npair, nq, 8, 2 * t), F32)] + _Gather.out_shapes(riders),
        grid=(npair, nq), in_specs=[q_spec, k_spec, vt_spec] + [any_spec] * nr,
        out_specs=[pl.BlockSpec((BLK, BLK), lambda p, qb: (qb, p)), c_spec] + [any_spec] * nr,
        scratch_shapes=[pltpu.VMEM((BLK, 2 * BLK), F32), pltpu.VMEM((SB_GW, 2 * BLK), F32),
                        pltpu.VMEM((SB_GW, 2 * BLK), F32), pltpu.VMEM((8, BLK), F32)] + _comm_sems(nr),
        compiler_params=_params("arbitrary", "arbitrary"),
    )(qkv, qkv, qkv_t, *riders)
    return outs[0], outs[1], list(outs[2:])


def _sb_fwd_step(q_ref, k_ref, vt_ref, o_ref, c_ref, oacc, zbuf0, zbuf1, kmax, qb):
    row, col, lane0, sub0 = _sb_consts()
    tri = (col >= row).astype(BF16)
    tri2 = jnp.concatenate([tri, tri], axis=1)
    q2 = _two_heads(q_ref[...], lane0, QK_SCALE)
    zbufs = (zbuf0, zbuf1)
    c_ref[...] = jnp.full(c_ref.shape, SB_UNSEEN, F32)
    oacc[...] = jnp.zeros_like(oacc)

    @pl.when(qb == 0)
    def _():
        ksq = jnp.square(k_ref[...].astype(F32))
        head0 = (lax.broadcasted_iota(jnp.int32, (1, BLK), 1) < HEAD_DIM).astype(F32)
        norms = jnp.maximum(jnp.sum(ksq * head0, axis=1, keepdims=True),
                            jnp.sum(ksq * (1.0 - head0), axis=1, keepdims=True))
        kmax[...] = jnp.broadcast_to(jnp.max(norms, axis=0, keepdims=True), kmax.shape)

    qsq = jnp.square(q2.astype(F32)).astype(BF16)
    qn2 = jnp.max(lax.dot_general(jnp.ones((8, BLK), BF16), qsq, NT, preferred_element_type=F32),
                  axis=0, keepdims=True)
    kk = kmax[0:1, :]
    zbound = jnp.sqrt(qn2 * jnp.concatenate([kk, kk], axis=1)) * 1.02

    def matters(cr):
        return (jnp.min(cr - zbound) <= SB_EXP_ZERO).astype(jnp.int32)

    def scores(g):
        ks = pl.multiple_of(g * SB_GW, SB_GW)
        return lax.dot_general(k_ref[pl.ds(ks, SB_GW), :], q2, NT, preferred_element_type=F32)

    def group(g, cur, nxt, cr, masked=False):
        z = zbufs[cur][...]
        zbufs[nxt][...] = scores(jnp.maximum(g - 1, 0))
        e = jnp.exp(-jnp.abs(z))
        sp = jnp.maximum(z, 0.0) + jnp.log(1.0 + e)
        if masked:
            valid = _sb_valid(g * SB_GW, qb)
            sp = jnp.where(valid, sp, 0.0)
        loc = _tri2_dot(tri2, _blocks_on_lanes(sp))
        parts = [None] * SB_G
        for b in reversed(range(SB_G)):
            rows = slice(b * BLK, (b + 1) * BLK)
            c_ref[g * SB_G + b] = jnp.broadcast_to(cr, (8, 2 * BLK))
            a = jnp.exp(z[rows] - (loc[:, 2 * b * BLK:2 * (b + 1) * BLK] + cr))
            if masked:
                a = jnp.where(valid[rows], a, 0.0)
            parts[b] = a.astype(BF16)
            cr = cr + jnp.sum(sp[rows], axis=0, keepdims=True)
        oacc[...] += jnp.dot(vt_ref[g], jnp.concatenate(parts, axis=0), preferred_element_type=F32)
        return cr

    last = qb // SB_G
    zbuf1[...] = scores(last)
    cr = group(last, 1, 0, jnp.zeros((1, 2 * BLK), F32), masked=True)

    def pair(state):
        g, cr, _ = state
        cr = group(g, 0, 1, cr)
        more = (g >= 1).astype(jnp.int32) * matters(cr)
        cr = lax.cond(more > 0, lambda c: group(g - 1, 1, 0, c), lambda c: c, cr)
        return jnp.where(more > 0, g - 2, -1), cr, matters(cr)

    lax.while_loop(lambda st: (st[0] >= 0) & (st[2] > 0), pair, (last - 1, cr, matters(cr)))
    o_t = oacc[:, :BLK] * sub0 + oacc[:, BLK:] * (1.0 - sub0)
    o_ref[...] = o_t.T.astype(BF16)


def sb_bwd(qkv, qkv_t, do, cmass, riders, name):
    t = qkv.shape[0]
    nq, npair, q_spec, k_spec, v_spec, kt_spec, _, c_spec = _sb_specs(t)
    nr = len(riders)

    def body(*refs):
        pp = pl.program_id(0)
        qb = pl.program_id(1)
        if nr:
            exchange = _Exchange(refs[6:6 + nr], refs[9 + nr:9 + 2 * nr], *refs[14 + 2 * nr:])
            pl.when((pp == 0) & (qb == 0))(exchange.start)
        step(*refs[:6], *refs[6 + nr:9 + nr], *refs[9 + 2 * nr:14 + 2 * nr])
        if nr:
            pl.when((pp == npair - 1) & (qb == nq - 1))(exchange.finish)

    def step(q_ref, k_ref, kt_ref, v_ref, do_ref, c_ref, dq_ref, dk_ref, dv_ref, dqacc, dkacc, dvacc,
             zbuf0, zbuf1):
        qb = pl.program_id(1)

        @pl.when(qb == 0)
        def _():
            dkacc[...] = jnp.zeros_like(dkacc)
            dvacc[...] = jnp.zeros_like(dvacc)

        row, col, lane0, sub0 = _sb_consts()
        tri_suf = (col >= row).astype(BF16)
        tri_pre = (col <= row).astype(BF16)
        tri2_suf = jnp.concatenate([tri_suf, tri_suf], axis=1)
        tri2_pre = jnp.concatenate([tri_pre, tri_pre], axis=1)
        q2 = _two_heads(q_ref[...], lane0, QK_SCALE)
        do2 = _two_heads(do_ref[...], lane0, 1.0)
        zbufs = (zbuf0, zbuf1)
        dqacc[...] = jnp.zeros_like(dqacc)
        last = qb // SB_G

        def scores(g):
            ks = pl.multiple_of(g * SB_GW, SB_GW)
            return lax.dot_general(k_ref[pl.ds(ks, SB_GW), :], q2, NT, preferred_element_type=F32)

        def group(g, cur, nxt, gc, masked=False):
            ks = pl.multiple_of(g * SB_GW, SB_GW)
            z = zbufs[cur][...]
            zbufs[nxt][...] = scores(jnp.minimum(g + 1, last))
            e = jnp.exp(-jnp.abs(z))
            sig = 0.5 * jnp.tanh(0.5 * z) + 0.5
            sp = jnp.maximum(z, 0.0) + jnp.log(1.0 + e)
            if masked:
                valid = _sb_valid(ks, qb)
                sp = jnp.where(valid, sp, 0.0)
            loc = _tri2_dot(tri2_suf, _blocks_on_lanes(sp))
            parts = []
            for b in range(SB_G):
                rows = slice(b * BLK, (b + 1) * BLK)
                mass = loc[:, 2 * b * BLK:2 * (b + 1) * BLK] + c_ref[g * SB_G + b, 0:1, :]
                parts.append(jnp.exp(z[rows] - mass))
            a = jnp.concatenate(parts, axis=0)
            if masked:
                a = jnp.where(valid, a, 0.0)
            gr = lax.dot_general(v_ref[pl.ds(ks, SB_GW), :], do2, NT, preferred_element_type=F32) * a
            pre = _tri2_dot(tri2_pre, _blocks_on_lanes(gr))
            parts = []
            for b in range(SB_G):
                rows = slice(b * BLK, (b + 1) * BLK)
                parts.append(pre[:, 2 * b * BLK:2 * (b + 1) * BLK] + gc)
                gc = gc + jnp.sum(gr[rows], axis=0, keepdims=True)
            dz = gr - sig * jnp.concatenate(parts, axis=0)
            if masked:
                dz = jnp.where(valid, dz, 0.0)
            dz = dz.astype(BF16)
            dkacc[pl.ds(ks, SB_GW), :] += jnp.dot(dz, q2, preferred_element_type=F32)
            dqacc[...] += jnp.dot(kt_ref[g], dz, preferred_element_type=F32)
            dvacc[pl.ds(ks, SB_GW), :] += jnp.dot(a.astype(BF16), do2, preferred_element_type=F32)
            return gc

        def skip(gc):
            zbuf0[...] = zbuf1[...]
            return gc

        def unseen(g):
            return (jnp.max(c_ref[g * SB_G + SB_G - 1, 0:1, :]) > 0.5 * SB_UNSEEN).astype(jnp.int32)

        first, _ = lax.while_loop(lambda st: (st[0] > 0) & (st[1] == 0),
                                  lambda st: (st[0] - 1, unseen(jnp.maximum(st[0] - 2, 0))),
                                  (last, unseen(jnp.maximum(last - 1, 0))))
        zbuf0[...] = scores(first)
        gc = _sb_pair_loop(first, last - first, 1, group, skip, jnp.zeros((1, 2 * BLK), F32))
        group(last, 0, 1, gc, masked=True)
        dq_t = (dqacc[:, :BLK] * sub0 + dqacc[:, BLK:] * (1.0 - sub0)) * QK_SCALE
        dq_ref[...] = dq_t.T.astype(BF16)

        @pl.when(qb == nq - 1)
        def _():
            dk_ref[...] = dkacc[...].astype(BF16)
            dv_ref[...] = dvacc[...].astype(BF16)

    col_spec = pl.BlockSpec((t, BLK), lambda p, qb: (0, p))
    blk_spec = pl.BlockSpec((BLK, BLK), lambda p, qb: (qb, p))
    any_spec = pl.BlockSpec(memory_space=pl.ANY)
    outs = _call(
        body, name=name,
        out_shape=[_sds((t, D_MODEL), BF16)] * 3 + [_sds(r.shape, r.dtype) for r in riders],
        grid=(npair, nq), in_specs=[q_spec, k_spec, kt_spec, v_spec, blk_spec, c_spec] + [any_spec] * nr,
        out_specs=[blk_spec, col_spec, col_spec] + [any_spec] * nr,
        scratch_shapes=[pltpu.VMEM((BLK, 2 * BLK), F32), pltpu.VMEM((t, BLK), F32), pltpu.VMEM((t, BLK), F32),
                        pltpu.VMEM((SB_GW, 2 * BLK), F32), pltpu.VMEM((SB_GW, 2 * BLK), F32)] + _comm_sems(nr),
        compiler_params=_params("arbitrary", "arbitrary"),
    )(qkv, qkv, qkv_t, qkv, do, cmass, *riders)
    return outs[0], outs[1], outs[2], list(outs[3:])


BAND_QPS = 4


def _band_static_mask(jj):
    row = lax.broadcasted_iota(jnp.int32, (2 * BLK, BLK), 0)
    col = lax.broadcasted_iota(jnp.int32, (2 * BLK, BLK), 1)
    qc = (row & (BLK - 1)) // 64
    kc = 2 * jj + col // 64
    return (kc >= qc) & (kc <= qc + 8)


def _band_key_start(qb, jj):
    kb = qb - (BAND_BLOCKS - 1) + jj
    return kb, pl.multiple_of(jnp.maximum(kb, 0) * BLK, BLK)


def _band_probs(q2, k_ref, bias, qb):
    blocks = []
    for jj in range(BAND_BLOCKS):
        kb, ks = _band_key_start(qb, jj)
        s = lax.dot_general(q2, k_ref[pl.ds(ks, BLK), :], NT, preferred_element_type=F32)
        s = s + bias[:, jj * BLK:(jj + 1) * BLK]
        ok = (kb >= 0) if 0 < jj < BAND_BLOCKS - 1 else _band_static_mask(jj) & (kb >= 0)
        blocks.append(jnp.where(ok, s, NEG))
    s = jnp.concatenate(blocks, axis=1)
    m = jnp.max(s, axis=-1, keepdims=True)
    e = jnp.exp(s - m)
    return e / jnp.sum(e, axis=-1, keepdims=True)


def _band_specs(t):
    npair = N_HEADS // 2
    rows = BAND_QPS * BLK
    q_spec = pl.BlockSpec((rows, BLK), lambda p, i: (i, p))
    k_spec = pl.BlockSpec((t, BLK), lambda p, i: (0, npair + p))
    v_spec = pl.BlockSpec((t, BLK), lambda p, i: (0, 2 * npair + p))
    b_spec = pl.BlockSpec((2, BLK, BAND_W), lambda p, i: (p, 0, 0))
    return npair, t // rows, q_spec, k_spec, v_spec, b_spec


def _two_heads(xv, lane0, scale):
    xf = xv.astype(F32)
    if scale != 1.0:
        xf = xf * scale
    return jnp.concatenate([xf * lane0, xf * (1.0 - lane0)], axis=0).astype(BF16)


def _one_of_two_heads(r, lane0):
    return r[:BLK] * lane0 + r[BLK:] * (1.0 - lane0)


def band_fwd(qkv, bias, name):
    t = qkv.shape[0]
    assert t % (BAND_QPS * BLK) == 0
    npair, nsteps, q_spec, k_spec, v_spec, b_spec = _band_specs(t)

    def body(q_ref, k_ref, v_ref, b_ref, o_ref):
        step = pl.program_id(1)
        _, _, lane0, _ = _sb_consts()
        bias2 = b_ref[...].reshape(2 * BLK, BAND_W)
        for u in range(BAND_QPS):
            qb = step * BAND_QPS + u
            rows = slice(u * BLK, (u + 1) * BLK)
            q2 = _two_heads(q_ref[rows, :], lane0, QK_SCALE)
            p = _band_probs(q2, k_ref, bias2, qb)
            acc = jnp.zeros((2 * BLK, BLK), F32)
            for jj in range(BAND_BLOCKS):
                _, ks = _band_key_start(qb, jj)
                acc += jnp.dot(p[:, jj * BLK:(jj + 1) * BLK].astype(BF16), v_ref[pl.ds(ks, BLK), :],
                               preferred_element_type=F32)
            o_ref[rows, :] = _one_of_two_heads(acc, lane0).astype(BF16)

    return _call(
        body, name=name, out_shape=_sds((t, D_MODEL), BF16), grid=(npair, nsteps),
        in_specs=[q_spec, k_spec, v_spec, b_spec],
        out_specs=pl.BlockSpec((BAND_QPS * BLK, BLK), lambda p, i: (i, p)),
        compiler_params=_params("parallel", "parallel"),
    )(qkv, qkv, qkv, bias)


def band_bwd(qkv, do, bias, name, riders=None):
    t = qkv.shape[0]
    npair, nsteps, q_spec, k_spec, v_spec, b_spec = _band_specs(t)

    def body(q_ref, k_ref, v_ref, do_ref, b_ref, dq_ref, dk_ref, dv_ref, db_ref, dkacc, dvacc):
        step = pl.program_id(1)

        @pl.when(step == 0)
        def _():
            dkacc[...] = jnp.zeros_like(dkacc)
            dvacc[...] = jnp.zeros_like(dvacc)
            db_ref[...] = jnp.zeros_like(db_ref)

        _, _, lane0, _ = _sb_consts()
        bias2 = b_ref[...].reshape(2 * BLK, BAND_W)
        updates = []
        for u in range(BAND_QPS):
            qb = step * BAND_QPS + u
            rows = slice(u * BLK, (u + 1) * BLK)
            q2 = _two_heads(q_ref[rows, :], lane0, QK_SCALE)
            do2 = _two_heads(do_ref[rows, :], lane0, 1.0)
            p = _band_probs(q2, k_ref, bias2, qb)
            dp = jnp.concatenate(
                [lax.dot_general(do2, v_ref[pl.ds(_band_key_start(qb, jj)[1], BLK), :], NT,
                                 preferred_element_type=F32) for jj in range(BAND_BLOCKS)], axis=1)
            ds = p * (dp - jnp.sum(p * dp, axis=-1, keepdims=True))
            db_ref[...] += ds.reshape(2, BLK, BAND_W)
            dqa = jnp.zeros((2 * BLK, BLK), F32)
            for jj in range(BAND_BLOCKS):
                _, ks = _band_key_start(qb, jj)
                dsb = ds[:, jj * BLK:(jj + 1) * BLK].astype(BF16)
                pb = p[:, jj * BLK:(jj + 1) * BLK].astype(BF16)
                dqa += jnp.dot(dsb, k_ref[pl.ds(ks, BLK), :], preferred_element_type=F32)
                updates.append((ks, lax.dot_general(dsb, q2, TN, preferred_element_type=F32),
                                lax.dot_general(pb, do2, TN, preferred_element_type=F32)))
            dq_ref[rows, :] = (_one_of_two_heads(dqa, lane0) * QK_SCALE).astype(BF16)
        for ks, dk_part, dv_part in updates:
            dkacc[pl.ds(ks, BLK), :] += dk_part
            dvacc[pl.ds(ks, BLK), :] += dv_part

        @pl.when(step == nsteps - 1)
        def _():
            dk_ref[...] = dkacc[...].astype(BF16)
            dv_ref[...] = dvacc[...].astype(BF16)

    col_spec = pl.BlockSpec((t, BLK), lambda p, i: (0, p))
    blk_spec = pl.BlockSpec((BAND_QPS * BLK, BLK), lambda p, i: (i, p))
    res, ridden = _rider_call(
        body, (qkv, qkv, qkv, do, bias), riders, name=name,
        out_shape=[_sds((t, D_MODEL), BF16)] * 3 + [_sds((N_HEADS, BLK, BAND_W), F32)],
        grid=(npair, nsteps), in_specs=[q_spec, k_spec, v_spec, blk_spec, b_spec],
        out_specs=[blk_spec, col_spec, col_spec, b_spec],
        scratch_shapes=[pltpu.VMEM((t, BLK), F32), pltpu.VMEM((t, BLK), F32)],
        sem=("parallel", "arbitrary"))
    return res[0], res[1], res[2], res[3], ridden


def band_bias_window(rel_bias):
    far = BAND_W + BLK - 1 - 2 * REL_CLIP
    width = BAND_W + BLK
    ext = jnp.concatenate(
        [jnp.broadcast_to(rel_bias[:, 2 * REL_CLIP:], (N_HEADS, far)), rel_bias[:, 2 * REL_CLIP:0:-1],
         jnp.zeros((N_HEADS, 2), F32)], axis=1)
    tiled = jnp.broadcast_to(ext[:, None, :], (N_HEADS, BLK, width + 1)).reshape(N_HEADS, BLK * (width + 1))
    return tiled[:, BLK - 1:BLK - 1 + BLK * width].reshape(N_HEADS, BLK, width)[:, :, :BAND_W]


def band_bias_window_grad(dwin):
    width = BAND_W + BLK
    far = BAND_W + BLK - 1 - 2 * REL_CLIP
    flat = jnp.pad(dwin, ((0, 0), (0, 0), (0, BLK))).reshape(N_HEADS, BLK * width)
    skew = jnp.pad(flat, ((0, 0), (BLK - 1, 1))).reshape(N_HEADS, BLK, width + 1)
    dext = jnp.sum(skew, axis=1)[:, :width - 1]
    return jnp.concatenate(
        [jnp.zeros((N_HEADS, 1), F32), dext[:, :far - 1:-1][:, :2 * REL_CLIP - 1],
         dext[:, far:far + 1] + jnp.sum(dext[:, :far], axis=1, keepdims=True)], axis=1)


SG_GROUPS = 8


def _gelu_parts(x):
    inner = GELU_C0 * (x + GELU_C1 * (x * x * x))
    th = jnp.tanh(inner)
    return th, 0.5 * x * (1.0 + th)


def _sg_gate_mask():
    row = lax.broadcasted_iota(jnp.int32, (BLK, BLK), 0)
    col = lax.broadcasted_iota(jnp.int32, (BLK, BLK), 1)
    return (row // 64) >= (col // 64)


def _sg_forward_parts(a, lng):
    w = a.shape[1] // 2
    th, z = _gelu_parts(a)
    u, v = z[:, :w], z[:, w:]
    mu = jnp.mean(v, axis=-1, keepdims=True)
    xc = v - mu
    rstd = lax.rsqrt(jnp.mean(xc * xc, axis=-1, keepdims=True) + EPS)
    vhat = xc * rstd
    return th, u, vhat, rstd, vhat * lng


def sg_fwd(a, lng, ws, bias_t, name):
    t, w2 = a.shape
    w = w2 // 2
    gc = w // SG_GROUPS

    def body(a_ref, lng_ref, ws_ref, bt_ref, y_ref):
        _, u, _, _, vln = _sg_forward_parts(a_ref[...], lng_ref[...])
        mask = _sg_gate_mask()
        bt = bt_ref[...]
        lane = lax.broadcasted_iota(jnp.int32, (BLK, BLK), 1)
        for g in range(SG_GROUPS):
            wg = jnp.where(mask, ws_ref[g], 0.0).astype(BF16)
            sv = jnp.dot(wg, vln[:, g * gc:(g + 1) * gc].astype(BF16), preferred_element_type=F32)
            bg = jnp.sum(jnp.where(lane == g, bt, 0.0), axis=-1, keepdims=True)
            y_ref[:, g * gc:(g + 1) * gc] = (u[:, g * gc:(g + 1) * gc] * (sv + bg)).astype(BF16)

    return _call(
        body, name=name, out_shape=_sds((t, w), BF16), grid=(t // BLK,),
        in_specs=[pl.BlockSpec((BLK, w2), lambda i: (i, 0)), pl.BlockSpec((1, w), lambda i: (0, 0)),
                  pl.BlockSpec((SG_GROUPS, BLK, BLK), lambda i: (0, 0, 0)),
                  pl.BlockSpec((BLK, BLK), lambda i: (0, 0))],
        out_specs=pl.BlockSpec((BLK, w), lambda i: (i, 0)),
        compiler_params=_params("parallel"),
    )(a, lng, ws, bias_t)


def sg_bwd(a, dy, lng, ws, bias_t, name):
    t, w2 = a.shape
    w = w2 // 2
    gc = w // SG_GROUPS

    def body(a_ref, dy_ref, lng_ref, ws_ref, bt_ref, da_ref, dlng_ref, dws_ref, dbt_ref):
        @pl.when(pl.program_id(0) == 0)
        def _():
            dlng_ref[...] = jnp.zeros_like(dlng_ref)
            dws_ref[...] = jnp.zeros_like(dws_ref)
            dbt_ref[...] = jnp.zeros_like(dbt_ref)

        av, lng = a_ref[...], lng_ref[...]
        th, u, vhat, rstd, vln = _sg_forward_parts(av, lng)
        mask = _sg_gate_mask()
        bt = bt_ref[...]
        lane = lax.broadcasted_iota(jnp.int32, (BLK, BLK), 1)
        dyv = dy_ref[...]
        du_parts, dvln_parts = [], []
        dbt = jnp.zeros((BLK, BLK), F32)
        for g in range(SG_GROUPS):
            sl = slice(g * gc, (g + 1) * gc)
            wg = jnp.where(mask, ws_ref[g], 0.0).astype(BF16)
            vg = vln[:, sl].astype(BF16)
            sv = jnp.dot(wg, vg, preferred_element_type=F32)
            bg = jnp.sum(jnp.where(lane == g, bt, 0.0), axis=-1, keepdims=True)
            dyg = dyv[:, sl]
            du_parts.append(dyg * (sv + bg))
            dsv = dyg * u[:, sl]
            dbt += jnp.where(lane == g, jnp.sum(dsv, axis=-1, keepdims=True), 0.0)
            dsvb = dsv.astype(BF16)
            dws_ref[g] += jnp.where(mask, lax.dot_general(dsvb, vg, NT, preferred_element_type=F32), 0.0)
            dvln_parts.append(lax.dot_general(wg, dsvb, TN, preferred_element_type=F32))
        dbt_ref[...] += dbt
        du = jnp.concatenate(du_parts, axis=1)
        dvln = jnp.concatenate(dvln_parts, axis=1)
        dlng_ref[...] += _colsum8(dvln * vhat)
        dvhat = dvln * lng
        dv = rstd * (dvhat - jnp.mean(dvhat, axis=-1, keepdims=True)
                     - vhat * jnp.mean(dvhat * vhat, axis=-1, keepdims=True))
        dz = jnp.concatenate([du, dv], axis=1)
        dgelu = 0.5 * (1.0 + th) + (0.5 * av) * (1.0 - th * th) * (GELU_C0 * (1.0 + 3.0 * GELU_C1 * (av * av)))
        da_ref[...] = (dz * dgelu).astype(BF16)

    return _call(
        body, name=name,
        out_shape=[_sds((t, w2), BF16), _sds((8, w), F32), _sds((SG_GROUPS, BLK, BLK), F32), _sds((BLK, BLK), F32)],
        grid=(t // BLK,),
        in_specs=[pl.BlockSpec((BLK, w2), lambda i: (i, 0)), pl.BlockSpec((BLK, w), lambda i: (i, 0)),
                  pl.BlockSpec((1, w), lambda i: (0, 0)),
                  pl.BlockSpec((SG_GROUPS, BLK, BLK), lambda i: (0, 0, 0)),
                  pl.BlockSpec((BLK, BLK), lambda i: (0, 0))],
        out_specs=[pl.BlockSpec((BLK, w2), lambda i: (i, 0)), pl.BlockSpec((8, w), lambda i: (0, 0)),
                   pl.BlockSpec((SG_GROUPS, BLK, BLK), lambda i: (0, 0, 0)),
                   pl.BlockSpec((BLK, BLK), lambda i: (0, 0))],
        compiler_params=_params("arbitrary"),
    )(a, dy, lng, ws, bias_t)


def _shift_down(cat, n, tr):
    return pltpu.roll(cat, n, 0)[8:8 + tr]


def _shift_up(cat, n, tr):
    return pltpu.roll(cat, tr + 8 - n, 0)[0:tr]


def conv_fwd(p, cw, name):
    t, d3 = p.shape
    d = d3 // 3
    tr = min(256, t)
    hb = tr // 8

    def body(p_ref, ph_ref, cw_ref, o_ref):
        i = pl.program_id(0)
        pv = p_ref[...]
        y = pv[:, d:2 * d] * pv[:, 2 * d:]
        ph = ph_ref[...]
        yh = jnp.where(i > 0, ph[:, d:2 * d] * ph[:, 2 * d:], 0.0)
        cat = jnp.concatenate([yh, y], axis=0)
        yc = (cw_ref[0:1, :] * _shift_down(cat, 2, tr) + cw_ref[1:2, :] * _shift_down(cat, 1, tr)
              + cw_ref[2:3, :] * y)
        o_ref[...] = (pv[:, :d] * yc).astype(BF16)

    return _call(
        body, name=name, out_shape=_sds((t, d), BF16), grid=(t // tr,),
        in_specs=[pl.BlockSpec((tr, d3), lambda i: (i, 0)),
                  pl.BlockSpec((8, d3), lambda i: (jnp.maximum(i * hb - 1, 0), 0)),
                  pl.BlockSpec((8, d), lambda i: (0, 0))],
        out_specs=pl.BlockSpec((tr, d), lambda i: (i, 0)),
        compiler_params=_params("parallel"),
    )(p, p, cw)


def conv_bwd(p, dz, cw, name):
    t, d3 = p.shape
    d = d3 // 3
    tr = min(256, t)
    hb = tr // 8
    nt = t // tr

    def body(p_ref, ph_ref, pn_ref, dz_ref, dzn_ref, cw_ref, dp_ref, dcw_ref):
        i = pl.program_id(0)

        @pl.when(i == 0)
        def _():
            dcw_ref[...] = jnp.zeros_like(dcw_ref)

        pv = p_ref[...]
        gb, gcv, xt = pv[:, :d], pv[:, d:2 * d], pv[:, 2 * d:]
        y = gcv * xt
        ph = ph_ref[...]
        yh = jnp.where(i > 0, ph[:, d:2 * d] * ph[:, 2 * d:], 0.0)
        cat = jnp.concatenate([yh, y], axis=0)
        y2, y1 = _shift_down(cat, 2, tr), _shift_down(cat, 1, tr)
        w0, w1, w2 = cw_ref[0:1, :], cw_ref[1:2, :], cw_ref[2:3, :]
        yc = w0 * y2 + w1 * y1 + w2 * y
        dzv = dz_ref[...]
        dyc = dzv * gb
        dcw_ref[0] += _colsum8(dyc * y2)
        dcw_ref[1] += _colsum8(dyc * y1)
        dcw_ref[2] += _colsum8(dyc * y)
        dycn = jnp.where(i < nt - 1, dzn_ref[...] * pn_ref[...][:, :d], 0.0)
        catn = jnp.concatenate([dyc, dycn], axis=0)
        dy = w2 * dyc + w1 * _shift_up(catn, 1, tr) + w0 * _shift_up(catn, 2, tr)
        dp_ref[:, :d] = (dzv * yc).astype(BF16)
        dp_ref[:, d:2 * d] = (dy * xt).astype(BF16)
        dp_ref[:, 2 * d:] = (dy * gcv).astype(BF16)

    nxt = lambda i: (jnp.minimum((i + 1) * hb, t // 8 - 1), 0)
    return _call(
        body, name=name, out_shape=[_sds((t, d3), BF16), _sds((3, 8, d), F32)], grid=(nt,),
        in_specs=[pl.BlockSpec((tr, d3), lambda i: (i, 0)),
                  pl.BlockSpec((8, d3), lambda i: (jnp.maximum(i * hb - 1, 0), 0)),
                  pl.BlockSpec((8, d3), nxt),
                  pl.BlockSpec((tr, d), lambda i: (i, 0)),
                  pl.BlockSpec((8, d), nxt),
                  pl.BlockSpec((8, d), lambda i: (0, 0))],
        out_specs=[pl.BlockSpec((tr, d3), lambda i: (i, 0)), pl.BlockSpec((3, 8, d), lambda i: (0, 0, 0))],
        compiler_params=_params("arbitrary"),
    )(p, p, p, dz, dz, cw)


def ada_fwd(c_all, w, b, name):
    nl, d, n = w.shape

    def body(c_ref, w_ref, b_ref, o_ref):
        cv = c_ref[...]
        s = (cv * _sigmoid(cv)).astype(BF16)
        o_ref[...] = jnp.dot(s, w_ref[...].astype(BF16), preferred_element_type=F32) + b_ref[...]

    return _call(
        body, name=name, out_shape=_sds((nl, N_DEV, n), F32), grid=(nl,),
        in_specs=[pl.BlockSpec((N_DEV, d), lambda l: (0, 0)), pl.BlockSpec((None, d, n), lambda l: (l, 0, 0)),
                  pl.BlockSpec((None, 1, n), lambda l: (l, 0, 0))],
        out_specs=pl.BlockSpec((None, N_DEV, n), lambda l: (l, 0, 0)),
        compiler_params=_params("parallel"),
    )(c_all, w, b)


def ada_bwd(c_all, dmod, name):
    nl, _, n = dmod.shape
    d = c_all.shape[1]

    def body(c_ref, dm_ref, o_ref):
        cv = c_ref[...]
        s = (cv * _sigmoid(cv)).astype(BF16)
        o_ref[...] = lax.dot_general(s, dm_ref[...].astype(BF16), TN, preferred_element_type=F32)

    return _call(
        body, name=name, out_shape=_sds((nl, d, n), F32), grid=(nl,),
        in_specs=[pl.BlockSpec((N_DEV, d), lambda l: (0, 0)), pl.BlockSpec((None, N_DEV, n), lambda l: (l, 0, 0))],
        out_specs=pl.BlockSpec((None, d, n), lambda l: (l, 0, 0)),
        compiler_params=_params("parallel"),
    )(c_all, dmod)


def adamw(pieces, w, m, v, name, riders=None):
    nl = len(pieces)
    npc, r, c = pieces[0].shape
    tr = r
    for cand in (1024, 512, 256, 128, 64, 32, 16, 8):
        if r % cand == 0 and cand * c * 4 <= (1 << 20):
            tr = cand
            break
    nt = r // tr

    def update(p_ref, w_ref, m_ref, v_ref, g_ref, d_ref, nm_ref, nv_ref):
        g = p_ref[0].astype(F32)
        for i in range(1, npc):
            g = g + p_ref[i].astype(F32)
        wv = w_ref[...]
        nm = ADAM_B1 * m_ref[...] + (1.0 - ADAM_B1) * g
        nv = ADAM_B2 * v_ref[...] + (1.0 - ADAM_B2) * (g * g)
        m_hat = nm / (1.0 - ADAM_B1 ** ADAM_STEP)
        v_hat = nv / (1.0 - ADAM_B2 ** ADAM_STEP)
        g_ref[...] = g
        d_ref[...] = -ADAM_LR * (m_hat / (jnp.sqrt(v_hat) + ADAM_EPS) + ADAM_WD * wv)
        nm_ref[...] = nm
        nv_ref[...] = nv

    def body(*refs):
        if nl == 1:
            update(*refs)
        else:
            for j in range(nl):
                pl.when(pl.program_id(0) == j)(lambda j=j: update(refs[j], *refs[nl:]))

    row = pl.BlockSpec((tr, c), lambda l, i: (l * nt + i, 0))
    piece_specs = [pl.BlockSpec((npc, tr, c), lambda l, i, j=j: (0, jnp.where(l == j, i, 0), 0))
                   for j in range(nl)]
    res, ridden = _rider_call(
        body, (*pieces, w, m, v), riders, name=name, out_shape=[_sds((nl * r, c), F32)] * 4, grid=(nl, nt),
        in_specs=piece_specs + [row, row, row], out_specs=[row] * 4, scratch_shapes=[],
        sem=("parallel", "parallel"))
    return res, ridden


def sum_pieces(pieces, name):
    npc, r, c = pieces.shape

    def body(p_ref, o_ref):
        g = p_ref[0]
        for i in range(1, npc):
            g = g + p_ref[i]
        o_ref[...] = g

    return _call(body, name=name, out_shape=_sds((r, c), F32),
                 in_specs=[pl.BlockSpec(memory_space=pltpu.VMEM)],
                 out_specs=pl.BlockSpec(memory_space=pltpu.VMEM),
                 compiler_params=pltpu.CompilerParams(vmem_limit_bytes=VMEM_LIMIT))(pieces)


PACK_W = 1024


def _pack(arrs):
    flat = jnp.concatenate([a.reshape(-1).astype(F32) for a in arrs])
    rows = -(-flat.shape[0] // (8 * PACK_W)) * 8
    return jnp.pad(flat, (0, rows * PACK_W - flat.shape[0])).reshape(rows, PACK_W)


def _unpack(slab, shapes):
    flat = slab.reshape(-1)
    out, off = [], 0
    for s in shapes:
        n = 1
        for q in s:
            n *= q
        out.append(flat[off:off + n].reshape(s))
        off += n
    return out


def kernel(x, c, ada_w, ada_b, norm_g, ffn_w_in, ffn_w_out, sb_w_qkv, sb_w_o, sg_w_in, sg_ln_g, sg_w_s, sg_bias, sg_w_out, sc_w_in, sc_conv_w, sc_w_out, cb_w_qkv, cb_rel_bias, cb_w_o, loss_target, m_ada_w, m_ada_b, m_norm_g, m_ffn_w_in, m_ffn_w_out, m_sb_w_qkv, m_sb_w_o, m_sg_w_in, m_sg_ln_g, m_sg_w_s, m_sg_bias, m_sg_w_out, m_sc_w_in, m_sc_conv_w, m_sc_w_out, m_cb_w_qkv, m_cb_rel_bias, m_cb_w_o, v_ada_w, v_ada_b, v_norm_g, v_ffn_w_in, v_ffn_w_out, v_sb_w_qkv, v_sb_w_o, v_sg_w_in, v_sg_ln_g, v_sg_w_s, v_sg_bias, v_sg_w_out, v_sc_w_in, v_sc_conv_w, v_sc_w_out, v_cb_w_qkv, v_cb_rel_bias, v_cb_w_o):
    depth = ada_w.shape[0]
    d = D_MODEL
    xi, yi, ci = lax.axis_index("x"), lax.axis_index("y"), lax.axis_index("c")
    me = 4 * xi + 2 * yi + ci
    x0 = x[0]
    t = x0.shape[0]
    target = loss_target[0]

    c_g, ng, small, w_qkv0 = _all_gather(
        [jnp.pad(c, ((0, 7), (0, 0))), norm_g.reshape(depth * 4, d // N_DEV), _pack([sg_ln_g, sc_conv_w]),
         sb_w_qkv[0].astype(BF16)], "gather_setup")

    c_all = c_g[:, 0, :]
    na = ada_w.shape[2]
    b_cols = lax.dynamic_slice_in_dim(ada_b, me * na, na, axis=1)[:, None, :]
    mod_part = ada_fwd(c_all, ada_w, b_cols, "ada_fwd")
    mod_g = _all_gather([mod_part.reshape(depth * N_DEV, na)], "gather_mod")[0]
    mod_g = mod_g.reshape(N_DEV, depth, N_DEV, na)
    mod_me = lax.dynamic_index_in_dim(mod_g, me, axis=2, keepdims=False)
    mod = jnp.transpose(mod_me, (1, 0, 2)).reshape(depth, 6, 1, d)

    norm_full = jnp.transpose(ng, (1, 0, 2)).reshape(depth, 4, 1, d)
    small = small.reshape(N_DEV, -1)
    nl_g = sg_ln_g.shape[1]
    ln_full = small[:, :nl_g].reshape(1, N_DEV * nl_g)
    cwn = sc_conv_w.shape[2]
    cw_sh = small[:, nl_g:nl_g + 3 * cwn].reshape(N_DEV, 3, cwn)
    cw_full = jnp.transpose(cw_sh, (1, 0, 2)).reshape(3, d)
    cw_pad = jnp.pad(cw_full, ((0, 5), (0, 0)))

    bf = lambda a: a.astype(BF16)
    mixers = [
        [bf(sb_w_qkv[0]), bf(sb_w_o[0])],
        [bf(sg_w_in[0]), bf(sg_w_out[0])],
        [bf(sc_w_in[0]), bf(sc_w_out[0])],
        [bf(cb_w_qkv[0]), bf(cb_w_o[0])],
    ]
    shards = [[bf(ffn_w_in[i]), bf(ffn_w_out[i])] + mixers[i % 4] for i in range(depth)]
    gathered = [[None] * 4 for _ in range(depth)]
    gathered[0][2] = w_qkv0
    riding_shards = [shards[0][0], shards[0][1], shards[0][3]] + shards[1] + [shards[i][0] for i in range(2, depth)]

    bias_win = band_bias_window(cb_rel_bias[0])
    ws = sg_w_s[0]
    bias_t = jnp.pad(sg_bias[0].T, ((0, 0), (0, BLK - SG_GROUPS)))

    saved = []
    xcur = x0
    h, h_t = pre_fwd(x0, norm_full[0, 0], mod[0, 0], mod[0, 1], "L0_pre_m")
    for i in range(depth):
        mi = i % 4
        sh_m, sc_m, gt_m, sh_f, sc_f, gt_f = [mod[i, j] for j in range(6)]
        g0, g1, g2, g3 = [norm_full[i, j] for j in range(4)]
        tag = "L%d_" % i
        sv = {"x_in": xcur, "ht_m": h_t}
        nxt = shards[i + 1] if 0 < i < depth - 1 else None
        if mi == 0:
            qkv = mm_cs(h, gathered[0][2], tag + "qkv", out_dtype=BF16)
            qkv_t = jnp.transpose(qkv.reshape(t // SB_GW, SB_GW, 3 * d), (0, 2, 1))
            o, cmass, riding = sb_fwd(qkv, qkv_t, riding_shards, tag + "sb_fwd")
            gathered[0][0], gathered[0][1], gathered[0][3] = riding[:3]
            gathered[1] = riding[3:7]
            for j in range(2, depth):
                gathered[j][0] = riding[5 + j]
            sv.update(qkv=qkv, qkv_t=qkv_t, o=o, cmass=cmass)
            mixed = o
        else:
            w_in = gathered[i][2]
            out_dtype = BF16 if mi == 3 else F32
            pre = mm_cs(h, w_in, tag + "mix_in", out_dtype=out_dtype)
            if mi == 1:
                mixed = sg_fwd(pre, ln_full, ws, bias_t, tag + "sg_fwd")
                sv.update(a=pre, yy=mixed)
            elif mi == 2:
                mixed = conv_fwd(pre, cw_pad, tag + "conv_fwd")
                sv.update(p=pre, gz=mixed)
            else:
                mixed = band_fwd(pre, bias_win, tag + "band_fwd")
                sv.update(qkv=pre, o=mixed)
        wfi, wfo, _, wmo = gathered[i]
        wfo4 = wfo.reshape(4, -1, d)
        y = mm(mixed, wmo.reshape(-1, d), tag + "mix_out")
        sv["y_m"] = y
        xmid, h2, h2_t = post_pre_fwd(xcur, y, g1, gt_m, g2, sh_f, sc_f, tag + "post_m_pre_f")
        sv["x_mid"] = xmid
        if nxt is not None:
            ag, au, s3, (gathered[i + 1][1], gathered[i + 1][3]) = ffn_in_swiglu(
                h2, wfi, tag + "ffn_in", riders=(_Gather, [nxt[1], nxt[3]]))
            y2, (gathered[i + 1][2],) = mm_rs(s3, wfo4, tag + "ffn_out", riders=(_Gather, [nxt[2]]))
        else:
            ag, au, s3, _ = ffn_in_swiglu(h2, wfi, tag + "ffn_in")
            y2 = mm_rs(s3, wfo4, tag + "ffn_out")
        sv.update(ht_f=h2_t, ag=ag, au=au, s3=s3, y_f=y2)
        saved.append(sv)
        if i + 1 < depth:
            xcur, h, h_t = post_pre_fwd(xmid, y2, g3, gt_f, norm_full[i + 1, 0], mod[i + 1, 0], mod[i + 1, 1],
                                   tag + "post_f_pre_m")

    last = depth - 1
    dx, dy2, lpart, dgt_f, dg3 = post_loss_bwd(saved[last]["x_mid"], saved[last]["y_f"], norm_full[last, 3],
                                               mod[last, 5], target, "loss")
    loss = lax.psum(0.5 * jnp.sum(lpart) / d, ("x", "y", "c"))

    dmod_rows = [None] * depth
    dnorm_rows = [None] * depth
    big_pieces = [[None] * 4 for _ in range(depth)]
    pending_dwmi, pending_dwfi, pending_layers = None, [], []
    small_grads = {}
    for i in reversed(range(depth)):
        wfi, wfo, wmi, wmo = gathered[i]
        wfo4 = wfo.reshape(4, -1, d)
        wmo2 = wmo.reshape(-1, d)
        wmi_t = jnp.transpose(wmi, (0, 2, 1)).reshape(-1, d)
        mi = i % 4
        sh_m, sc_m, gt_m, sh_f, sc_f, gt_f = [mod[i, j] for j in range(6)]
        g0, g1, g2, g3 = [norm_full[i, j] for j in range(4)]
        tag = "L%d_b_" % i
        sv = saved[i]
        da3, _ = ffn_out_dx_swiglu(dy2, wfo4, sv["ag"], sv["au"], tag + "ffn_out_dx")
        dwfo = mm_rs_dw(sv["s3"], dy2, tag + "ffn_out_dw", out_dtype=BF16).reshape(N_DEV, -1, d)
        dh2, (big_pieces[i][1],) = mm_cs_dx(da3, wfi, tag + "ffn_in_dx", act_major=True, riders=(_Exchange, [dwfo]))
        if pending_dwmi is not None:
            dwfi, (big_pieces[i + 1][2],) = mm_cs_dw(sv["ht_f"], da3, tag + "ffn_in_dw", act_major=True,
                                                     out_dtype=BF16, riders=(_Exchange, [pending_dwmi]))
        else:
            dwfi = mm_cs_dw(sv["ht_f"], da3, tag + "ffn_in_dw", act_major=True, out_dtype=BF16)
        dx, dy, dsh_f, dsc_f, dg2, dgt_m, dg1 = pre_post_bwd(
            dh2, sv["x_mid"], g2, sc_f, dx, sv["y_m"], g1, gt_m, tag + "pre_f_post_m")
        if mi == 0:
            do = mm_nt(dy, wmo2, tag + "wo_dx", out_dtype=BF16)
            dwmo = mm_tn(sv["o"], dy, tag + "wo_dw", out_dtype=BF16).reshape(N_DEV, -1, d)
            dq, dk, dv, ridden = sb_bwd(sv["qkv"], sv["qkv_t"], do, sv["cmass"],
                                        pending_dwfi + [dwfi, dwmo], tag + "sb_bwd")
            for n, j in enumerate(pending_layers):
                big_pieces[j][0] = ridden[n]
            big_pieces[0][0], big_pieces[0][3] = ridden[-2:]
            dmid = jnp.concatenate([dq, dk, dv], axis=1)
        elif mi == 1:
            dyy = mm_nt(dy, wmo2, tag + "sg_out_dx")
            dwmo = mm_tn(sv["yy"], dy, tag + "sg_out_dw", out_dtype=BF16).reshape(N_DEV, -1, d)
            dmid, dlng, dws, dbt = sg_bwd(sv["a"], dyy, ln_full, ws, bias_t, tag + "sg_bwd")
            small_grads.update(ln_g=jnp.sum(dlng, axis=0), w_s=dws, bias=dbt[:, :SG_GROUPS].T)
        elif mi == 2:
            dgz = mm_nt(dy, wmo2, tag + "sc_out_dx")
            dwmo = mm_tn(sv["gz"], dy, tag + "sc_out_dw", out_dtype=BF16).reshape(N_DEV, -1, d)
            dmid, dcw = conv_bwd(sv["p"], dgz, cw_pad, tag + "conv_bwd")
            small_grads.update(conv_w=jnp.sum(dcw, axis=1))
        else:
            do = mm_nt(dy, wmo2, tag + "wo_dx", out_dtype=BF16)
            dwmo = mm_tn(sv["o"], dy, tag + "wo_dw", out_dtype=BF16).reshape(N_DEV, -1, d)
            dq, dk, dv, dwin, (big_pieces[i][0], big_pieces[i][3]) = band_bwd(
                sv["qkv"], do, bias_win, tag + "band_bwd", riders=(_Exchange, [dwfi, dwmo]))
            dmid = jnp.concatenate([dq, dk, dv], axis=1)
            small_grads.update(rel_bias=band_bias_window_grad(dwin))
        if mi in (1, 2):
            dh, (big_pieces[i][3],) = mm(dmid, wmi_t, tag + "mix_in_dx", riders=(_Exchange, [dwmo]))
            pending_dwfi.append(dwfi)
            pending_layers.append(i)
        else:
            dh = mm(dmid, wmi_t, tag + "mix_in_dx")
        pending_dwmi = mm_cs_dw(sv["ht_m"], dmid, tag + "mix_in_dw", out_dtype=BF16)
        if i > 0:
            dx, dy2_prev, dsh_m, dsc_m, dg0, dgt_f_prev, dg3_prev = pre_post_bwd(
                dh, sv["x_in"], g0, sc_m, dx, saved[i - 1]["y_f"], norm_full[i - 1, 3], mod[i - 1, 5],
                tag + "pre_m_post_f")
        else:
            dx, dsh_m, dsc_m, dg0 = pre_bwd(dh, sv["x_in"], g0, sc_m, dx, tag + "pre_m")
        dmod_rows[i] = jnp.stack([jnp.sum(q, axis=0) for q in (dsh_m, dsc_m, dgt_m, dsh_f, dsc_f, dgt_f)])
        dnorm_rows[i] = jnp.stack([jnp.sum(q, axis=0) for q in (dg0, dg1, dg2, dg3)])
        if i > 0:
            dy2, dgt_f, dg3 = dy2_prev, dgt_f_prev, dg3_prev
    grad_x = dx[None]

    out_g, out_d, out_m, out_v = {}, {}, {}, {}

    def upd(name, pieces, w, m, v, riders=None):
        rows_cols = (len(pieces) * pieces[0].shape[1], pieces[0].shape[2])
        res, ridden = adamw(pieces, w.reshape(rows_cols), m.reshape(rows_cols), v.reshape(rows_cols),
                            "adamw_" + name, riders)
        out_g[name], out_d[name], out_m[name], out_v[name] = [q.reshape(w.shape) for q in res]
        return ridden

    dmod_mine = jnp.stack(dmod_rows).reshape(depth, 6 * d)
    dnorm_mine = jnp.stack(dnorm_rows)
    small_list = [dnorm_mine, small_grads["ln_g"], small_grads["bias"], small_grads["conv_w"],
                  small_grads["rel_bias"]]
    small_shapes = [dmod_mine.shape] + [a.shape for a in small_list]
    slab = _pack([dmod_mine] + small_list)
    big_pieces[0][2], = upd("ffn_w_in", [big_pieces[i][0] for i in range(depth)], ffn_w_in, m_ffn_w_in, v_ffn_w_in,
                            riders=(_Exchange, [pending_dwmi]))
    slab_g, ws_g = upd("ffn_w_out", [big_pieces[i][1] for i in range(depth)], ffn_w_out, m_ffn_w_out, v_ffn_w_out,
                       riders=(_Gather, [slab, small_grads["w_s"].reshape(-1, BLK)]))
    upd("sg_w_s", [ws_g], sg_w_s, m_sg_w_s, v_sg_w_s)
    upd("sb_w_qkv", [big_pieces[0][2]], sb_w_qkv, m_sb_w_qkv, v_sb_w_qkv)
    upd("sb_w_o", [big_pieces[0][3]], sb_w_o, m_sb_w_o, v_sb_w_o)
    upd("sg_w_in", [big_pieces[1][2]], sg_w_in, m_sg_w_in, v_sg_w_in)
    upd("sg_w_out", [big_pieces[1][3]], sg_w_out, m_sg_w_out, v_sg_w_out)
    upd("sc_w_in", [big_pieces[2][2]], sc_w_in, m_sc_w_in, v_sc_w_in)
    upd("sc_w_out", [big_pieces[2][3]], sc_w_out, m_sc_w_out, v_sc_w_out)
    upd("cb_w_qkv", [big_pieces[3][2]], cb_w_qkv, m_cb_w_qkv, v_cb_w_qkv)
    upd("cb_w_o", [big_pieces[3][3]], cb_w_o, m_cb_w_o, v_cb_w_o)

    tot = sum_pieces(slab_g, "sum_small_grads")
    g_ada_b_full, g_norm, g_ln, g_sbias, g_cw, g_rb = _unpack(tot, small_shapes)
    dmod_all = slab_g.reshape(N_DEV, -1)[:, :depth * 6 * d].reshape(N_DEV, depth, 6 * d)
    dmod_cols = lax.dynamic_slice_in_dim(dmod_all, me * na, na, axis=2)
    g_ada_w = ada_bwd(c_all, jnp.transpose(dmod_cols, (1, 0, 2)), "ada_bwd")

    nsh = d // N_DEV
    g_norm_sh = lax.dynamic_slice_in_dim(g_norm, me * nsh, nsh, axis=2)
    g_ln_sh = lax.dynamic_slice_in_dim(g_ln.reshape(1, -1), me * nl_g, nl_g, axis=1)
    g_cw_sh = lax.dynamic_slice_in_dim(g_cw, me * cwn, cwn, axis=1)[None]

    upd("ada_w", [g_ada_w.reshape(1, depth * d, na)], ada_w, m_ada_w, v_ada_w)

    small_names = ["ada_b", "norm_g", "sg_ln_g", "sg_bias", "sc_conv_w", "cb_rel_bias"]
    small_g = [g_ada_b_full, g_norm_sh, g_ln_sh, g_sbias[None], g_cw_sh, g_rb[None]]
    small_w = [ada_b, norm_g, sg_ln_g, sg_bias, sc_conv_w, cb_rel_bias]
    small_m = [m_ada_b, m_norm_g, m_sg_ln_g, m_sg_bias, m_sc_conv_w, m_cb_rel_bias]
    small_v = [v_ada_b, v_norm_g, v_sg_ln_g, v_sg_bias, v_sc_conv_w, v_cb_rel_bias]
    shapes = [w.shape for w in small_w]
    res, _ = adamw([_pack(small_g)[None]], _pack(small_w), _pack(small_m), _pack(small_v), "adamw_small")
    for nm_, gs, ds_, ms, vs in zip(small_names, *[_unpack(r, shapes) for r in res]):
        out_g[nm_], out_d[nm_], out_m[nm_], out_v[nm_] = gs, ds_, ms, vs

    order = ["ada_w", "ada_b", "norm_g", "ffn_w_in", "ffn_w_out", "sb_w_qkv", "sb_w_o", "sg_w_in", "sg_ln_g",
             "sg_w_s", "sg_bias", "sg_w_out", "sc_w_in", "sc_conv_w", "sc_w_out", "cb_w_qkv", "cb_rel_bias", "cb_w_o"]
    return (loss, grad_x, *[out_g[n] for n in order], *[out_d[n] for n in order],
            *[out_m[n] for n in order], *[out_v[n] for n in order])
```

```python
import jax
import jax.numpy as jnp
from jax import lax
from jax.experimental import pallas as pl
from jax.experimental.pallas import tpu as pltpu

F32 = jnp.float32
BF16 = jnp.bfloat16
MESH = pl.DeviceIdType.MESH

N_DEV = 8
D_MODEL = 1024
N_HEADS = 16
HEAD_DIM = 64
QK_SCALE = HEAD_DIM ** -0.5
BLK = 128
BAND_BLOCKS = 5
BAND_W = BAND_BLOCKS * BLK
REL_CLIP = 128
EPS = 1e-6
NEG = -1e30
GELU_C0 = 0.7978845608028654
GELU_C1 = 0.044715
ADAM_LR = 0.001
ADAM_B1 = 0.9
ADAM_B2 = 0.999
ADAM_EPS = 1e-08
ADAM_WD = 0.01
ADAM_STEP = 10
VMEM_LIMIT = 56 * 1024 * 1024


def _call(body, **kw):
    return pl.pallas_call(body, **kw)


def _params(*sem):
    return pltpu.CompilerParams(dimension_semantics=sem, vmem_limit_bytes=VMEM_LIMIT)


def _sds(shape, dtype):
    return jax.ShapeDtypeStruct(tuple(shape), dtype)


def _row_tile(t):
    return min(512, t)


def _me():
    x, y, c = lax.axis_index("x"), lax.axis_index("y"), lax.axis_index("c")
    return x, y, c


def _all_gather(arrs, name):
    n = len(arrs)

    def body(*refs):
        gather = _Gather(refs[:n], refs[n:2 * n], *refs[2 * n:])
        gather.start()
        gather.forward()
        gather.finish()

    any_spec = pl.BlockSpec(memory_space=pl.ANY)
    outs = _call(
        body,
        name=name,
        out_shape=_Gather.out_shapes(arrs),
        in_specs=[any_spec] * n,
        out_specs=[any_spec] * n,
        scratch_shapes=_comm_sems(n),
    )(*arrs)
    return list(outs)


def _comm_sems(n):
    if n == 0:
        return []
    return [pltpu.SemaphoreType.DMA((n, 7)), pltpu.SemaphoreType.DMA((n, 7)), pltpu.SemaphoreType.DMA((n,))]


class _Gather:
    def __init__(self, x_refs, o_refs, send_sems, recv_sems, local_sems):
        self.x_refs, self.o_refs = x_refs, o_refs
        self.send_sems, self.recv_sems, self.local_sems = send_sems, recv_sems, local_sems
        x, y, c = _me()
        self.c = c
        self.me, self.sibling = (x, y, c), (x, y, 1 - c)
        self.chips = [(1 - x, y), (x, 1 - y), (1 - x, 1 - y)]

    @staticmethod
    def out_shapes(arrs):
        return [_sds((N_DEV,) + a.shape, a.dtype) for a in arrs]

    def rows(self, a, block):
        px, py, pc = block
        return self.o_refs[a].at[4 * px + 2 * py + pc]

    def copy(self, a, k, block, to, own=False):
        return pltpu.make_async_remote_copy(
            src_ref=self.x_refs[a] if own else self.rows(a, block),
            dst_ref=self.rows(a, block),
            send_sem=self.send_sems.at[a, k],
            recv_sem=self.recv_sems.at[a, k],
            device_id=to,
            device_id_type=MESH,
        )

    def local(self, a):
        return pltpu.make_async_copy(self.x_refs[a], self.rows(a, self.me), self.local_sems.at[a])

    def first(self, a):
        cps = [self.copy(a, 0, self.me, self.sibling, own=True)]
        return cps + [self.copy(a, 1 + j, self.me, (*chip, self.c), own=True) for j, chip in enumerate(self.chips)]

    def passed(self, a):
        return [self.copy(a, 4 + j, (*chip, self.c), self.sibling) for j, chip in enumerate(self.chips)]

    def start(self):
        for a in range(len(self.x_refs)):
            self.local(a).start()
            for cp in self.first(a):
                cp.start()

    def forward(self):
        for a in range(len(self.x_refs)):
            passed = self.passed(a)
            for j, chip in enumerate(self.chips):
                self.copy(a, 1 + j, (*chip, self.c), self.me).wait_recv()
                passed[j].start()

    def finish(self):
        for a in range(len(self.x_refs)):
            self.copy(a, 0, self.sibling, self.me).wait_recv()
            for j, chip in enumerate(self.chips):
                self.copy(a, 4 + j, (*chip, 1 - self.c), self.me).wait_recv()
            for cp in self.first(a) + self.passed(a):
                cp.wait_send()
            self.local(a).wait()


class _Exchange:
    def __init__(self, x_refs, o_refs, send_sems, recv_sems, local_sems):
        self.x_refs, self.o_refs = x_refs, o_refs
        self.send_sems, self.recv_sems, self.local_sems = send_sems, recv_sems, local_sems
        x, y, c = _me()
        self.me = 4 * x + 2 * y + c
        self.peers = []
        for k in range(1, N_DEV):
            px = 1 - x if k & 4 else x
            py = 1 - y if k & 2 else y
            pc = 1 - c if k & 1 else c
            self.peers.append((px, py, pc))

    @staticmethod
    def out_shapes(arrs):
        return [_sds(a.shape, a.dtype) for a in arrs]

    def local(self, a):
        return pltpu.make_async_copy(self.x_refs[a].at[self.me], self.o_refs[a].at[self.me], self.local_sems.at[a])

    def copy(self, a, k, send):
        px, py, pc = self.peers[k]
        peer = 4 * px + 2 * py + pc
        return pltpu.make_async_remote_copy(
            src_ref=self.x_refs[a].at[peer],
            dst_ref=self.o_refs[a].at[self.me if send else peer],
            send_sem=self.send_sems.at[a, k], recv_sem=self.recv_sems.at[a, k],
            device_id=(px, py, pc), device_id_type=MESH)

    def start(self):
        for a in range(len(self.x_refs)):
            self.local(a).start()
            for k in range(N_DEV - 1):
                self.copy(a, k, True).start()

    def finish(self):
        for a in range(len(self.x_refs)):
            for k in range(N_DEV - 1):
                self.copy(a, k, False).wait_recv()
            for k in range(N_DEV - 1):
                self.copy(a, k, True).wait_send()
            self.local(a).wait()


def _all_to_all(arrs, name):
    n = len(arrs)

    def body(*refs):
        exchange = _Exchange(refs[:n], refs[n:2 * n], *refs[2 * n:])
        exchange.start()
        exchange.finish()

    any_spec = pl.BlockSpec(memory_space=pl.ANY)
    outs = _call(
        body,
        name=name,
        out_shape=[_sds(a.shape, a.dtype) for a in arrs],
        in_specs=[any_spec] * n,
        out_specs=[any_spec] * n,
        scratch_shapes=_comm_sems(n),
    )(*arrs)
    return list(outs)


NN = (((1,), (0,)), ((), ()))
NT = (((1,), (1,)), ((), ()))
TN = (((0,), (0,)), ((), ()))


def _all_of(conds):
    out = conds[0]
    for cond in conds[1:]:
        out = out & cond
    return out


def _rider_call(body, operands, riders, *, name, out_shape, grid, in_specs, out_specs, scratch_shapes, sem):
    if not riders or not riders[1]:
        res = _call(body, name=name, out_shape=out_shape, grid=grid, in_specs=in_specs, out_specs=out_specs,
                    scratch_shapes=list(scratch_shapes), compiler_params=_params(*sem))(*operands)
        return list(res), []
    cls, arrs = riders
    nr, ni, no, ns = len(arrs), len(in_specs), len(out_specs), len(scratch_shapes)
    forward_at = (3 * grid[0]) // 4 if grid[0] >= 4 else None

    def wrapped(*refs):
        ids = [pl.program_id(ax) for ax in range(len(grid))]
        comm = cls(refs[ni:ni + nr], refs[ni + nr + no:ni + 2 * nr + no], *refs[ni + 2 * nr + no + ns:])
        pl.when(_all_of([i == 0 for i in ids]))(comm.start)
        if cls is _Gather and forward_at is not None:
            pl.when(_all_of([ids[0] == forward_at] + [i == 0 for i in ids[1:]]))(comm.forward)
        body(*refs[:ni], *refs[ni + nr:ni + nr + no], *refs[ni + 2 * nr + no:ni + 2 * nr + no + ns])

        def end():
            if cls is _Gather and forward_at is None:
                comm.forward()
            comm.finish()

        pl.when(_all_of([i == g - 1 for i, g in zip(ids, grid)]))(end)

    any_spec = pl.BlockSpec(memory_space=pl.ANY)
    res = _call(wrapped, name=name, out_shape=list(out_shape) + cls.out_shapes(arrs), grid=grid,
                in_specs=list(in_specs) + [any_spec] * nr, out_specs=list(out_specs) + [any_spec] * nr,
                scratch_shapes=list(scratch_shapes) + _comm_sems(nr),
                compiler_params=_params(*(["arbitrary"] * len(grid))))(*operands, *arrs)
    return list(res[:no]), list(res[no:])


def _gemm(a, b, out_shape, out_dtype, grid, a_spec, b_spec, o_spec, acc_shape, dims, name, riders=None):
    nk = grid[2]

    if nk == 1:
        def body(a_ref, b_ref, o_ref):
            r = lax.dot_general(a_ref[...].astype(BF16), b_ref[...].astype(BF16), dims,
                                preferred_element_type=F32)
            o_ref[...] = r.astype(o_ref.dtype)
        scratch = []
    else:
        def body(a_ref, b_ref, o_ref, acc_ref):
            k = pl.program_id(2)

            @pl.when(k == 0)
            def _():
                acc_ref[...] = jnp.zeros_like(acc_ref)

            acc_ref[...] += lax.dot_general(a_ref[...].astype(BF16), b_ref[...].astype(BF16), dims,
                                            preferred_element_type=F32)

            @pl.when(k == nk - 1)
            def _():
                o_ref[...] = acc_ref[...].astype(o_ref.dtype)
        scratch = [pltpu.VMEM(acc_shape, F32)]

    res, ridden = _rider_call(
        body, (a, b), riders, name=name, out_shape=[_sds(out_shape, out_dtype)], grid=grid,
        in_specs=[a_spec, b_spec], out_specs=[o_spec], scratch_shapes=scratch,
        sem=("parallel", "parallel", "arbitrary"))
    return (res[0], ridden) if riders else res[0]


def _div_tile(n, want):
    if n <= want:
        return n
    t = want - want % 128
    while n % t:
        t -= 128
    return t


def mm(a, b, name, out_dtype=F32, tm=512, tn=1024, tk=1024, riders=None):
    m, k = a.shape
    n = b.shape[1]
    tm, tn, tk = _div_tile(m, tm), _div_tile(n, tn), _div_tile(k, tk)
    return _gemm(a, b, (m, n), out_dtype, (m // tm, n // tn, k // tk),
                 pl.BlockSpec((tm, tk), lambda i, j, kk: (i, kk)),
                 pl.BlockSpec((tk, tn), lambda i, j, kk: (kk, j)),
                 pl.BlockSpec((tm, tn), lambda i, j, kk: (i, j)),
                 (tm, tn), NN, name, riders)


def mm_nt(a, b, name, out_dtype=F32, tm=512, tn=1024, tk=1024):
    m, n = a.shape
    k = b.shape[0]
    tm, tk_out, tred = _div_tile(m, tm), _div_tile(k, tn), _div_tile(n, tk)
    return _gemm(a, b, (m, k), out_dtype, (m // tm, k // tk_out, n // tred),
                 pl.BlockSpec((tm, tred), lambda i, j, kk: (i, kk)),
                 pl.BlockSpec((tk_out, tred), lambda i, j, kk: (j, kk)),
                 pl.BlockSpec((tm, tk_out), lambda i, j, kk: (i, j)),
                 (tm, tk_out), NT, name)


def mm_tn(a, b, name, out_dtype=F32, tm=512, tn=1024, tk=1024):
    m, k = a.shape
    n = b.shape[1]
    tk_out, tn, tred = _div_tile(k, tk), _div_tile(n, tn), _div_tile(m, tm)
    return _gemm(a, b, (k, n), out_dtype, (k // tk_out, n // tn, m // tred),
                 pl.BlockSpec((tred, tk_out), lambda i, j, kk: (kk, i)),
                 pl.BlockSpec((tred, tn), lambda i, j, kk: (kk, j)),
                 pl.BlockSpec((tk_out, tn), lambda i, j, kk: (i, j)),
                 (tk_out, tn), TN, name)


def mm_cs(a, wg, name, act_major=False, out_dtype=F32, tm=1024, riders=None):
    m, k = a.shape
    s, _, n = wg.shape
    tm = _div_tile(m, tm)
    if act_major:
        out_shape, o_spec = (s, m, n), pl.BlockSpec((None, tm, n), lambda i, j, kk: (j, i, 0))
    else:
        out_shape, o_spec = (m, s * n), pl.BlockSpec((tm, n), lambda i, j, kk: (i, j))
    return _gemm(a, wg, out_shape, out_dtype, (m // tm, s, 1),
                 pl.BlockSpec((tm, k), lambda i, j, kk: (i, 0)),
                 pl.BlockSpec((None, k, n), lambda i, j, kk: (j, 0, 0)),
                 o_spec, (tm, n), NN, name, riders)


def mm_cs_dx(da, wg, name, act_major=False, out_dtype=F32, tm=1024, riders=None):
    s, k, n = wg.shape
    m = da.shape[1] if act_major else da.shape[0]
    tm = _div_tile(m, tm)
    if act_major:
        a_spec = pl.BlockSpec((None, tm, n), lambda i, j, kk: (kk, i, 0))
    else:
        a_spec = pl.BlockSpec((tm, n), lambda i, j, kk: (i, kk))
    return _gemm(da, wg, (m, k), out_dtype, (m // tm, 1, s), a_spec,
                 pl.BlockSpec((None, k, n), lambda i, j, kk: (kk, 0, 0)),
                 pl.BlockSpec((tm, k), lambda i, j, kk: (i, 0)),
                 (tm, k), NT, name, riders)


def mm_cs_dw(a, da, name, act_major=False, out_dtype=F32, tm=1024, riders=None):
    m, k = a.shape
    if act_major:
        s, _, n = da.shape
    else:
        s, n = N_DEV, da.shape[1] // N_DEV
    tm = _div_tile(m, tm)
    if act_major:
        b_spec = pl.BlockSpec((None, tm, n), lambda i, j, kk: (i, kk, 0))
    else:
        b_spec = pl.BlockSpec((tm, n), lambda i, j, kk: (kk, i))
    return _gemm(a, da, (s, k, n), out_dtype, (s, 1, m // tm),
                 pl.BlockSpec((tm, k), lambda i, j, kk: (kk, 0)), b_spec,
                 pl.BlockSpec((None, k, n), lambda i, j, kk: (i, 0, 0)),
                 (k, n), TN, name, riders)


def mm_rs(s3, w3, name, out_dtype=F32, tm=1024, riders=None):
    s, m, n = s3.shape
    nn = w3.shape[2]
    tm = _div_tile(m, tm)
    return _gemm(s3, w3, (m, nn), out_dtype, (m // tm, 1, s),
                 pl.BlockSpec((None, tm, n), lambda i, j, kk: (kk, i, 0)),
                 pl.BlockSpec((None, n, nn), lambda i, j, kk: (kk, 0, 0)),
                 pl.BlockSpec((tm, nn), lambda i, j, kk: (i, 0)),
                 (tm, nn), NN, name, riders)


def mm_rs_dx(dy, w3, name, out_dtype=F32, tm=1024):
    m, nn = dy.shape
    s, n, _ = w3.shape
    tm = _div_tile(m, tm)
    return _gemm(dy, w3, (s, m, n), out_dtype, (m // tm, s, 1),
                 pl.BlockSpec((tm, nn), lambda i, j, kk: (i, 0)),
                 pl.BlockSpec((None, n, nn), lambda i, j, kk: (j, 0, 0)),
                 pl.BlockSpec((None, tm, n), lambda i, j, kk: (j, i, 0)),
                 (tm, n), NT, name)


def mm_rs_dw(s3, dy, name, out_dtype=F32, tm=1024):
    s, m, n = s3.shape
    nn = dy.shape[1]
    tm = _div_tile(m, tm)
    return _gemm(s3, dy, (s, n, nn), out_dtype, (s, 1, m // tm),
                 pl.BlockSpec((None, tm, n), lambda i, j, kk: (i, kk, 0)),
                 pl.BlockSpec((tm, nn), lambda i, j, kk: (kk, 0)),
                 pl.BlockSpec((None, n, nn), lambda i, j, kk: (i, 0, 0)),
                 (n, nn), TN, name)


def _colsum8(v):
    tr, d = v.shape
    return v.reshape(tr // 8, 8, d).sum(axis=0)


def _rstd(v):
    return lax.rsqrt(jnp.mean(v * v, axis=-1, keepdims=True) + EPS)


def _vec_spec(d):
    return pl.BlockSpec((1, d), lambda i: (0, 0))


def _acc_spec(d):
    return pl.BlockSpec((8, d), lambda i: (0, 0))


def _pre_rows(xv, g, shift, scale):
    return ((xv * _rstd(xv)) * g) * (1 + scale) + shift


def _post_rows(xv, yv, g, gate):
    return xv + gate * ((yv * _rstd(yv)) * g)


def _post_bwd_rows(dxv, yv, g, gate):
    r = _rstd(yv)
    yhat = yv * r
    dgate = _colsum8(dxv * (yhat * g))
    dyn = gate * dxv
    dg = _colsum8(dyn * yhat)
    dyhat = dyn * g
    dy = r * (dyhat - yhat * jnp.mean(dyhat * yhat, axis=-1, keepdims=True))
    return dy, dgate, dg


def _pre_bwd_rows(dhv, xv, g, scale, dxn):
    r = _rstd(xv)
    xhat = xv * r
    dshift = _colsum8(dhv)
    dscale = _colsum8(dhv * (xhat * g))
    dmod = dhv * (1 + scale)
    dg = _colsum8(dmod * xhat)
    dxhat = dmod * g
    dx = r * (dxhat - xhat * jnp.mean(dxhat * xhat, axis=-1, keepdims=True)) + dxn
    return dx, dshift, dscale, dg


def _row_call(body, name, t, d, rows_in, vecs_in, rows_out, n_acc):
    tr = _row_tile(t)
    nri, nvi, nro = len(rows_in), len(vecs_in), len(rows_out)

    def wrapped(*refs):
        accs = refs[nri + nvi + nro:]
        if n_acc:
            @pl.when(pl.program_id(0) == 0)
            def _():
                for acc in accs:
                    acc[...] = jnp.zeros_like(acc)
        body(*refs)

    row = pl.BlockSpec((tr, d), lambda i: (i, 0))
    return _call(wrapped, name=name,
                 out_shape=[_sds((t, d), dt) for dt in rows_out] + [_sds((8, d), F32)] * n_acc,
                 grid=(t // tr,), in_specs=[row] * nri + [_vec_spec(d)] * nvi,
                 out_specs=[row] * nro + [_acc_spec(d)] * n_acc,
                 compiler_params=_params("arbitrary" if n_acc else "parallel"))(*rows_in, *vecs_in)


def pre_fwd(x, g, shift, scale, name):
    def body(x_ref, g_ref, sh_ref, sc_ref, h_ref):
        h_ref[...] = _pre_rows(x_ref[...], g_ref[...], sh_ref[...], sc_ref[...]).astype(BF16)

    return _row_call(body, name, *x.shape, [x], [g, shift, scale], [BF16], 0)[0]


def post_pre_fwd(x, y, g_post, gate, g_pre, shift, scale, name):
    def body(x_ref, y_ref, gp_ref, gt_ref, g_ref, sh_ref, sc_ref, xn_ref, h_ref):
        xn = _post_rows(x_ref[...], y_ref[...], gp_ref[...], gt_ref[...])
        xn_ref[...] = xn
        h_ref[...] = _pre_rows(xn, g_ref[...], sh_ref[...], sc_ref[...]).astype(BF16)

    return _row_call(body, name, *x.shape, [x, y], [g_post, gate, g_pre, shift, scale], [F32, BF16], 0)


def post_loss_bwd(x, y, g, gate, target, name):
    d = x.shape[1]

    def body(x_ref, y_ref, t_ref, g_ref, gt_ref, dx_ref, dy_ref, l_ref, dgate_ref, dg_ref):
        yv, gv, gate_v = y_ref[...], g_ref[...], gt_ref[...]
        err = _post_rows(x_ref[...], yv, gv, gate_v) - t_ref[...]
        l_ref[...] += _colsum8(err * err)
        dxv = err * (1.0 / d)
        dx_ref[...] = dxv
        dy, dgate, dg = _post_bwd_rows(dxv, yv, gv, gate_v)
        dy_ref[...] = dy.astype(BF16)
        dgate_ref[...] += dgate
        dg_ref[...] += dg

    return _row_call(body, name, *x.shape, [x, y, target], [g, gate], [F32, BF16], 3)


def pre_post_bwd(dh, x, g_pre, scale, dxn, y, g_post, gate, name):
    def body(dh_ref, x_ref, dxn_ref, y_ref, g_ref, sc_ref, gp_ref, gt_ref,
             dx_ref, dy_ref, dsh_ref, dsc_ref, dg_ref, dgate_ref, dgp_ref):
        dx, dsh, dsc, dg = _pre_bwd_rows(dh_ref[...].astype(F32), x_ref[...], g_ref[...], sc_ref[...], dxn_ref[...])
        dx_ref[...] = dx
        dsh_ref[...] += dsh
        dsc_ref[...] += dsc
        dg_ref[...] += dg
        dy, dgate, dgp = _post_bwd_rows(dx, y_ref[...], gp_ref[...], gt_ref[...])
        dy_ref[...] = dy.astype(BF16)
        dgate_ref[...] += dgate
        dgp_ref[...] += dgp

    return _row_call(body, name, *x.shape, [dh, x, dxn, y], [g_pre, scale, g_post, gate], [F32, BF16], 5)


def pre_bwd(dh, x, g, scale, dxn, name):
    def body(dh_ref, x_ref, dxn_ref, g_ref, sc_ref, dx_ref, dsh_ref, dsc_ref, dg_ref):
        dx, dsh, dsc, dg = _pre_bwd_rows(dh_ref[...].astype(F32), x_ref[...], g_ref[...], sc_ref[...], dxn_ref[...])
        dx_ref[...] = dx
        dsh_ref[...] += dsh
        dsc_ref[...] += dsc
        dg_ref[...] += dg

    return _row_call(body, name, *x.shape, [dh, x, dxn], [g, scale], [F32], 3)


def _sigmoid(x):
    return 1.0 / (1.0 + jnp.exp(-x))


def ffn_in_swiglu(h, wg, name, tm=1024, riders=None):
    m, k = h.shape
    s, _, n = wg.shape
    half = s // 2
    tm = _div_tile(m, tm)

    def body(h_ref, wg_ref, wu_ref, g_ref, u_ref, s_ref):
        hv = h_ref[...]
        g = jnp.dot(hv, wg_ref[...], preferred_element_type=F32)
        u = jnp.dot(hv, wu_ref[...], preferred_element_type=F32)
        g_ref[...] = g
        u_ref[...] = u
        s_ref[...] = ((g * _sigmoid(g)) * u).astype(BF16)

    act = pl.BlockSpec((None, tm, n), lambda i, j: (j, i, 0))
    res, ridden = _rider_call(
        body, (h, wg, wg), riders, name=name,
        out_shape=[_sds((half, m, n), F32), _sds((half, m, n), F32), _sds((half, m, n), BF16)],
        grid=(m // tm, half),
        in_specs=[pl.BlockSpec((tm, k), lambda i, j: (i, 0)),
                  pl.BlockSpec((None, k, n), lambda i, j: (j, 0, 0)),
                  pl.BlockSpec((None, k, n), lambda i, j: (j + half, 0, 0))],
        out_specs=[act, act, act], scratch_shapes=[], sem=("parallel", "parallel"))
    return res[0], res[1], res[2], ridden


def ffn_out_dx_swiglu(dy, w4, gate, up, name, tm=1024, riders=None):
    m, nn = dy.shape
    half, n, _ = w4.shape
    tm = _div_tile(m, tm)

    def body(dy_ref, w_ref, g_ref, u_ref, o_ref):
        ds = lax.dot_general(dy_ref[...], w_ref[...], NT, preferred_element_type=F32)
        g, u = g_ref[...], u_ref[...]
        sig = _sigmoid(g)
        o_ref[0] = (ds * u * (sig * (1 + g * (1 - sig)))).astype(BF16)
        o_ref[1] = (ds * (g * sig)).astype(BF16)

    act = pl.BlockSpec((None, tm, n), lambda i, j: (j, i, 0))
    res, ridden = _rider_call(
        body, (dy, w4, gate, up), riders, name=name, out_shape=[_sds((2, half, m, n), BF16)],
        grid=(m // tm, half),
        in_specs=[pl.BlockSpec((tm, nn), lambda i, j: (i, 0)),
                  pl.BlockSpec((None, n, nn), lambda i, j: (j, 0, 0)), act, act],
        out_specs=[pl.BlockSpec((2, None, tm, n), lambda i, j: (0, j, i, 0))],
        scratch_shapes=[], sem=("parallel", "parallel"))
    return res[0].reshape(2 * half, m, n), ridden


def _split_hi_lo(v):
    hi = v.astype(BF16)
    lo = (v - hi.astype(F32)).astype(BF16)
    return hi, lo


SB_G = 2
SB_EXP_ZERO = 104.0
SB_UNSEEN = 3e38
SB_GW = SB_G * BLK


def _sb_specs(t):
    nq = t // BLK
    npair = N_HEADS // 2
    q_spec = pl.BlockSpec((BLK, BLK), lambda p, qb: (qb, p))
    k_spec = pl.BlockSpec((t, BLK), lambda p, qb: (0, npair + p))
    v_spec = pl.BlockSpec((t, BLK), lambda p, qb: (0, 2 * npair + p))
    kt_spec = pl.BlockSpec((t // SB_GW, BLK, SB_GW), lambda p, qb: (0, npair + p, 0))
    vt_spec = pl.BlockSpec((t // SB_GW, BLK, SB_GW), lambda p, qb: (0, 2 * npair + p, 0))
    c_spec = pl.BlockSpec((None, nq, 8, 2 * BLK), lambda p, qb: (p, 0, 0, qb))
    return nq, npair, q_spec, k_spec, v_spec, kt_spec, vt_spec, c_spec


def _sb_consts():
    row = lax.broadcasted_iota(jnp.int32, (BLK, BLK), 0)
    col = lax.broadcasted_iota(jnp.int32, (BLK, BLK), 1)
    lane0 = (col < HEAD_DIM).astype(F32)
    sub0 = (row < HEAD_DIM).astype(F32)
    return row, col, lane0, sub0


def _sb_valid(ks, qb):
    row = lax.broadcasted_iota(jnp.int32, (SB_GW, 2 * BLK), 0)
    col = lax.broadcasted_iota(jnp.int32, (SB_GW, 2 * BLK), 1)
    return (ks + row) < (qb * BLK + (col & (BLK - 1)))


def _blocks_on_lanes(v4):
    return jnp.concatenate([v4[b * BLK:(b + 1) * BLK] for b in range(SB_G)], axis=1)


def _tri2_dot(tri2, v):
    hi, lo = _split_hi_lo(v)
    return jnp.dot(tri2, jnp.concatenate([hi, lo], axis=0), preferred_element_type=F32)


def _sb_pair_loop(first, count, step, group, skip, carry):
    def pair(it, cy):
        g1 = first + 2 * step * it
        cy = group(g1, 0, 1, cy)
        return lax.cond(2 * it + 1 < count, lambda c: group(g1 + step, 1, 0, c), skip, cy)
    return lax.fori_loop(0, (count + 1) // 2, pair, carry)


def sb_fwd(qkv, qkv_t, riders, name):
    t = qkv.shape[0]
    assert t % SB_GW == 0
    nq, npair, q_spec, k_spec, _, _, vt_spec, c_spec = _sb_specs(t)
    nr = len(riders)

    def body(*refs):
        q_ref, k_ref, vt_ref = refs[:3]
        o_ref, c_ref = refs[3 + nr:5 + nr]
        oacc, zbuf0, zbuf1, kmax = refs[5 + 2 * nr:9 + 2 * nr]
        pp = pl.program_id(0)
        qb = pl.program_id(1)
        if nr:
            gather = _Gather(refs[3:3 + nr], refs[5 + nr:5 + 2 * nr], *refs[9 + 2 * nr:])
            pl.when((pp == 0) & (qb == 0))(gather.start)
            pl.when((pp == npair - 2) & (qb == 0))(gather.forward)
        _sb_fwd_step(q_ref, k_ref, vt_ref, o_ref, c_ref, oacc, zbuf0, zbuf1, kmax, qb)
        if nr:
            pl.when((pp == npair - 1) & (qb == nq - 1))(gather.finish)

    any_spec = pl.BlockSpec(memory_space=pl.ANY)
    outs = _call(
        body, name=name,
        out_shape=[_sds((t, D_MODEL), BF16), _sds((npair, nq, 8, 2 * t), F32)] + _Gather.out_shapes(riders),
        grid=(npair, nq), in_specs=[q_spec, k_spec, vt_spec] + [any_spec] * nr,
        out_specs=[pl.BlockSpec((BLK, BLK), lambda p, qb: (qb, p)), c_spec] + [any_spec] * nr,
        scratch_shapes=[pltpu.VMEM((BLK, 2 * BLK), F32), pltpu.VMEM((SB_GW, 2 * BLK), F32),
                        pltpu.VMEM((SB_GW, 2 * BLK), F32), pltpu.VMEM((8, BLK), F32)] + _comm_sems(nr),
        compiler_params=_params("arbitrary", "arbitrary"),
    )(qkv, qkv, qkv_t, *riders)
    return outs[0], outs[1], list(outs[2:])


def _sb_fwd_step(q_ref, k_ref, vt_ref, o_ref, c_ref, oacc, zbuf0, zbuf1, kmax, qb):
    row, col, lane0, sub0 = _sb_consts()
    tri = (col >= row).astype(BF16)
    tri2 = jnp.concatenate([tri, tri], axis=1)
    q2 = _two_heads(q_ref[...], lane0, QK_SCALE)
    zbufs = (zbuf0, zbuf1)
    c_ref[...] = jnp.full(c_ref.shape, SB_UNSEEN, F32)
    oacc[...] = jnp.zeros_like(oacc)

    @pl.when(qb == 0)
    def _():
        ksq = jnp.square(k_ref[...].astype(F32))
        head0 = (lax.broadcasted_iota(jnp.int32, (1, BLK), 1) < HEAD_DIM).astype(F32)
        norms = jnp.maximum(jnp.sum(ksq * head0, axis=1, keepdims=True),
                            jnp.sum(ksq * (1.0 - head0), axis=1, keepdims=True))
        kmax[...] = jnp.broadcast_to(jnp.max(norms, axis=0, keepdims=True), kmax.shape)

    qsq = jnp.square(q2.astype(F32)).astype(BF16)
    qn2 = jnp.max(lax.dot_general(jnp.ones((8, BLK), BF16), qsq, NT, preferred_element_type=F32),
                  axis=0, keepdims=True)
    kk = kmax[0:1, :]
    zbound = jnp.sqrt(qn2 * jnp.concatenate([kk, kk], axis=1)) * 1.02

    def matters(cr):
        return (jnp.min(cr - zbound) <= SB_EXP_ZERO).astype(jnp.int32)

    def scores(g):
        ks = pl.multiple_of(g * SB_GW, SB_GW)
        return lax.dot_general(k_ref[pl.ds(ks, SB_GW), :], q2, NT, preferred_element_type=F32)

    def group(g, cur, nxt, cr, masked=False):
        z = zbufs[cur][...]
        zbufs[nxt][...] = scores(jnp.maximum(g - 1, 0))
        e = jnp.exp(-jnp.abs(z))
        sp = jnp.maximum(z, 0.0) + jnp.log(1.0 + e)
        if masked:
            valid = _sb_valid(g * SB_GW, qb)
            sp = jnp.where(valid, sp, 0.0)
        loc = _tri2_dot(tri2, _blocks_on_lanes(sp))
        parts = [None] * SB_G
        for b in reversed(range(SB_G)):
            rows = slice(b * BLK, (b + 1) * BLK)
            c_ref[g * SB_G + b] = jnp.broadcast_to(cr, (8, 2 * BLK))
            a = jnp.exp(z[rows] - (loc[:, 2 * b * BLK:2 * (b + 1) * BLK] + cr))
            if masked:
                a = jnp.where(valid[rows], a, 0.0)
            parts[b] = a.astype(BF16)
            cr = cr + jnp.sum(sp[rows], axis=0, keepdims=True)
        oacc[...] += jnp.dot(vt_ref[g], jnp.concatenate(parts, axis=0), preferred_element_type=F32)
        return cr

    last = qb // SB_G
    zbuf1[...] = scores(last)
    cr = group(last, 1, 0, jnp.zeros((1, 2 * BLK), F32), masked=True)

    def pair(state):
        g, cr, _ = state
        cr = group(g, 0, 1, cr)
        more = (g >= 1).astype(jnp.int32) * matters(cr)
        cr = lax.cond(more > 0, lambda c: group(g - 1, 1, 0, c), lambda c: c, cr)
        return jnp.where(more > 0, g - 2, -1), cr, matters(cr)

    lax.while_loop(lambda st: (st[0] >= 0) & (st[2] > 0), pair, (last - 1, cr, matters(cr)))
    o_t = oacc[:, :BLK] * sub0 + oacc[:, BLK:] * (1.0 - sub0)
    o_ref[...] = o_t.T.astype(BF16)


def sb_bwd(qkv, qkv_t, do, cmass, riders, name):
    t = qkv.shape[0]
    nq, npair, q_spec, k_spec, v_spec, kt_spec, _, c_spec = _sb_specs(t)
    nr = len(riders)

    def body(*refs):
        pp = pl.program_id(0)
        qb = pl.program_id(1)
        if nr:
            exchange = _Exchange(refs[6:6 + nr], refs[9 + nr:9 + 2 * nr], *refs[14 + 2 * nr:])
            pl.when((pp == 0) & (qb == 0))(exchange.start)
        step(*refs[:6], *refs[6 + nr:9 + nr], *refs[9 + 2 * nr:14 + 2 * nr])
        if nr:
            pl.when((pp == npair - 1) & (qb == nq - 1))(exchange.finish)

    def step(q_ref, k_ref, kt_ref, v_ref, do_ref, c_ref, dq_ref, dk_ref, dv_ref, dqacc, dkacc, dvacc,
             zbuf0, zbuf1):
        qb = pl.program_id(1)

        @pl.when(qb == 0)
        def _():
            dkacc[...] = jnp.zeros_like(dkacc)
            dvacc[...] = jnp.zeros_like(dvacc)

        row, col, lane0, sub0 = _sb_consts()
        tri_suf = (col >= row).astype(BF16)
        tri_pre = (col <= row).astype(BF16)
        tri2_suf = jnp.concatenate([tri_suf, tri_suf], axis=1)
        tri2_pre = jnp.concatenate([tri_pre, tri_pre], axis=1)
        q2 = _two_heads(q_ref[...], lane0, QK_SCALE)
        do2 = _two_heads(do_ref[...], lane0, 1.0)
        zbufs = (zbuf0, zbuf1)
        dqacc[...] = jnp.zeros_like(dqacc)
        last = qb // SB_G

        def scores(g):
            ks = pl.multiple_of(g * SB_GW, SB_GW)
            return lax.dot_general(k_ref[pl.ds(ks, SB_GW), :], q2, NT, preferred_element_type=F32)

        def group(g, cur, nxt, gc, masked=False):
            ks = pl.multiple_of(g * SB_GW, SB_GW)
            z = zbufs[cur][...]
            zbufs[nxt][...] = scores(jnp.minimum(g + 1, last))
            e = jnp.exp(-jnp.abs(z))
            sig = 0.5 * jnp.tanh(0.5 * z) + 0.5
            sp = jnp.maximum(z, 0.0) + jnp.log(1.0 + e)
            if masked:
                valid = _sb_valid(ks, qb)
                sp = jnp.where(valid, sp, 0.0)
            loc = _tri2_dot(tri2_suf, _blocks_on_lanes(sp))
            parts = []
            for b in range(SB_G):
                rows = slice(b * BLK, (b + 1) * BLK)
                mass = loc[:, 2 * b * BLK:2 * (b + 1) * BLK] + c_ref[g * SB_G + b, 0:1, :]
                parts.append(jnp.exp(z[rows] - mass))
            a = jnp.concatenate(parts, axis=0)
            if masked:
                a = jnp.where(valid, a, 0.0)
            gr = lax.dot_general(v_ref[pl.ds(ks, SB_GW), :], do2, NT, preferred_element_type=F32) * a
            pre = _tri2_dot(tri2_pre, _blocks_on_lanes(gr))
            parts = []
            for b in range(SB_G):
                rows = slice(b * BLK, (b + 1) * BLK)
                parts.append(pre[:, 2 * b * BLK:2 * (b + 1) * BLK] + gc)
                gc = gc + jnp.sum(gr[rows], axis=0, keepdims=True)
            dz = gr - sig * jnp.concatenate(parts, axis=0)
            if masked:
                dz = jnp.where(valid, dz, 0.0)
            dz = dz.astype(BF16)
            dkacc[pl.ds(ks, SB_GW), :] += jnp.dot(dz, q2, preferred_element_type=F32)
            dqacc[...] += jnp.dot(kt_ref[g], dz, preferred_element_type=F32)
            dvacc[pl.ds(ks, SB_GW), :] += jnp.dot(a.astype(BF16), do2, preferred_element_type=F32)
            return gc

        def skip(gc):
            zbuf0[...] = zbuf1[...]
            return gc

        def unseen(g):
            return (jnp.max(c_ref[g * SB_G + SB_G - 1, 0:1, :]) > 0.5 * SB_UNSEEN).astype(jnp.int32)

        first, _ = lax.while_loop(lambda st: (st[0] > 0) & (st[1] == 0),
                                  lambda st: (st[0] - 1, unseen(jnp.maximum(st[0] - 2, 0))),
                                  (last, unseen(jnp.maximum(last - 1, 0))))
        zbuf0[...] = scores(first)
        gc = _sb_pair_loop(first, last - first, 1, group, skip, jnp.zeros((1, 2 * BLK), F32))
        group(last, 0, 1, gc, masked=True)
        dq_t = (dqacc[:, :BLK] * sub0 + dqacc[:, BLK:] * (1.0 - sub0)) * QK_SCALE
        dq_ref[...] = dq_t.T.astype(BF16)

        @pl.when(qb == nq - 1)
        def _():
            dk_ref[...] = dkacc[...].astype(BF16)
            dv_ref[...] = dvacc[...].astype(BF16)

    col_spec = pl.BlockSpec((t, BLK), lambda p, qb: (0, p))
    blk_spec = pl.BlockSpec((BLK, BLK), lambda p, qb: (qb, p))
    any_spec = pl.BlockSpec(memory_space=pl.ANY)
    outs = _call(
        body, name=name,
        out_shape=[_sds((t, D_MODEL), BF16)] * 3 + [_sds(r.shape, r.dtype) for r in riders],
        grid=(npair, nq), in_specs=[q_spec, k_spec, kt_spec, v_spec, blk_spec, c_spec] + [any_spec] * nr,
        out_specs=[blk_spec, col_spec, col_spec] + [any_spec] * nr,
        scratch_shapes=[pltpu.VMEM((BLK, 2 * BLK), F32), pltpu.VMEM((t, BLK), F32), pltpu.VMEM((t, BLK), F32),
                        pltpu.VMEM((SB_GW, 2 * BLK), F32), pltpu.VMEM((SB_GW, 2 * BLK), F32)] + _comm_sems(nr),
        compiler_params=_params("arbitrary", "arbitrary"),
    )(qkv, qkv, qkv_t, qkv, do, cmass, *riders)
    return outs[0], outs[1], outs[2], list(outs[3:])


BAND_QPS = 4


def _band_static_mask(jj):
    row = lax.broadcasted_iota(jnp.int32, (2 * BLK, BLK), 0)
    col = lax.broadcasted_iota(jnp.int32, (2 * BLK, BLK), 1)
    qc = (row & (BLK - 1)) // 64
    kc = 2 * jj + col // 64
    return (kc >= qc) & (kc <= qc + 8)


def _band_key_start(qb, jj):
    kb = qb - (BAND_BLOCKS - 1) + jj
    return kb, pl.multiple_of(jnp.maximum(kb, 0) * BLK, BLK)


def _band_probs(q2, k_ref, bias, qb):
    blocks = []
    for jj in range(BAND_BLOCKS):
        kb, ks = _band_key_start(qb, jj)
        s = lax.dot_general(q2, k_ref[pl.ds(ks, BLK), :], NT, preferred_element_type=F32)
        s = s + bias[:, jj * BLK:(jj + 1) * BLK]
        ok = (kb >= 0) if 0 < jj < BAND_BLOCKS - 1 else _band_static_mask(jj) & (kb >= 0)
        blocks.append(jnp.where(ok, s, NEG))
    s = jnp.concatenate(blocks, axis=1)
    m = jnp.max(s, axis=-1, keepdims=True)
    e = jnp.exp(s - m)
    return e / jnp.sum(e, axis=-1, keepdims=True)


def _band_specs(t):
    npair = N_HEADS // 2
    rows = BAND_QPS * BLK
    q_spec = pl.BlockSpec((rows, BLK), lambda p, i: (i, p))
    k_spec = pl.BlockSpec((t, BLK), lambda p, i: (0, npair + p))
    v_spec = pl.BlockSpec((t, BLK), lambda p, i: (0, 2 * npair + p))
    b_spec = pl.BlockSpec((2, BLK, BAND_W), lambda p, i: (p, 0, 0))
    return npair, t // rows, q_spec, k_spec, v_spec, b_spec


def _two_heads(xv, lane0, scale):
    xf = xv.astype(F32)
    if scale != 1.0:
        xf = xf * scale
    return jnp.concatenate([xf * lane0, xf * (1.0 - lane0)], axis=0).astype(BF16)


def _one_of_two_heads(r, lane0):
    return r[:BLK] * lane0 + r[BLK:] * (1.0 - lane0)


def band_fwd(qkv, bias, name):
    t = qkv.shape[0]
    assert t % (BAND_QPS * BLK) == 0
    npair, nsteps, q_spec, k_spec, v_spec, b_spec = _band_specs(t)

    def body(q_ref, k_ref, v_ref, b_ref, o_ref):
        step = pl.program_id(1)
        _, _, lane0, _ = _sb_consts()
        bias2 = b_ref[...].reshape(2 * BLK, BAND_W)
        for u in range(BAND_QPS):
            qb = step * BAND_QPS + u
            rows = slice(u * BLK, (u + 1) * BLK)
            q2 = _two_heads(q_ref[rows, :], lane0, QK_SCALE)
            p = _band_probs(q2, k_ref, bias2, qb)
            acc = jnp.zeros((2 * BLK, BLK), F32)
            for jj in range(BAND_BLOCKS):
                _, ks = _band_key_start(qb, jj)
                acc += jnp.dot(p[:, jj * BLK:(jj + 1) * BLK].astype(BF16), v_ref[pl.ds(ks, BLK), :],
                               preferred_element_type=F32)
            o_ref[rows, :] = _one_of_two_heads(acc, lane0).astype(BF16)

    return _call(
        body, name=name, out_shape=_sds((t, D_MODEL), BF16), grid=(npair, nsteps),
        in_specs=[q_spec, k_spec, v_spec, b_spec],
        out_specs=pl.BlockSpec((BAND_QPS * BLK, BLK), lambda p, i: (i, p)),
        compiler_params=_params("parallel", "parallel"),
    )(qkv, qkv, qkv, bias)


def band_bwd(qkv, do, bias, name, riders=None):
    t = qkv.shape[0]
    npair, nsteps, q_spec, k_spec, v_spec, b_spec = _band_specs(t)

    def body(q_ref, k_ref, v_ref, do_ref, b_ref, dq_ref, dk_ref, dv_ref, db_ref, dkacc, dvacc):
        step = pl.program_id(1)

        @pl.when(step == 0)
        def _():
            dkacc[...] = jnp.zeros_like(dkacc)
            dvacc[...] = jnp.zeros_like(dvacc)
            db_ref[...] = jnp.zeros_like(db_ref)

        _, _, lane0, _ = _sb_consts()
        bias2 = b_ref[...].reshape(2 * BLK, BAND_W)
        updates = []
        for u in range(BAND_QPS):
            qb = step * BAND_QPS + u
            rows = slice(u * BLK, (u + 1) * BLK)
            q2 = _two_heads(q_ref[rows, :], lane0, QK_SCALE)
            do2 = _two_heads(do_ref[rows, :], lane0, 1.0)
            p = _band_probs(q2, k_ref, bias2, qb)
            dp = jnp.concatenate(
                [lax.dot_general(do2, v_ref[pl.ds(_band_key_start(qb, jj)[1], BLK), :], NT,
                                 preferred_element_type=F32) for jj in range(BAND_BLOCKS)], axis=1)
            ds = p * (dp - jnp.sum(p * dp, axis=-1, keepdims=True))
            db_ref[...] += ds.reshape(2, BLK, BAND_W)
            dqa = jnp.zeros((2 * BLK, BLK), F32)
            for jj in range(BAND_BLOCKS):
                _, ks = _band_key_start(qb, jj)
                dsb = ds[:, jj * BLK:(jj + 1) * BLK].astype(BF16)
                pb = p[:, jj * BLK:(jj + 1) * BLK].astype(BF16)
                dqa += jnp.dot(dsb, k_ref[pl.ds(ks, BLK), :], preferred_element_type=F32)
                updates.append((ks, lax.dot_general(dsb, q2, TN, preferred_element_type=F32),
                                lax.dot_general(pb, do2, TN, preferred_element_type=F32)))
            dq_ref[rows, :] = (_one_of_two_heads(dqa, lane0) * QK_SCALE).astype(BF16)
        for ks, dk_part, dv_part in updates:
            dkacc[pl.ds(ks, BLK), :] += dk_part
            dvacc[pl.ds(ks, BLK), :] += dv_part

        @pl.when(step == nsteps - 1)
        def _():
            dk_ref[...] = dkacc[...].astype(BF16)
            dv_ref[...] = dvacc[...].astype(BF16)

    col_spec = pl.BlockSpec((t, BLK), lambda p, i: (0, p))
    blk_spec = pl.BlockSpec((BAND_QPS * BLK, BLK), lambda p, i: (i, p))
    res, ridden = _rider_call(
        body, (qkv, qkv, qkv, do, bias), riders, name=name,
        out_shape=[_sds((t, D_MODEL), BF16)] * 3 + [_sds((N_HEADS, BLK, BAND_W), F32)],
        grid=(npair, nsteps), in_specs=[q_spec, k_spec, v_spec, blk_spec, b_spec],
        out_specs=[blk_spec, col_spec, col_spec, b_spec],
        scratch_shapes=[pltpu.VMEM((t, BLK), F32), pltpu.VMEM((t, BLK), F32)],
        sem=("parallel", "arbitrary"))
    return res[0], res[1], res[2], res[3], ridden


def band_bias_window(rel_bias):
    far = BAND_W + BLK - 1 - 2 * REL_CLIP
    width = BAND_W + BLK
    ext = jnp.concatenate(
        [jnp.broadcast_to(rel_bias[:, 2 * REL_CLIP:], (N_HEADS, far)), rel_bias[:, 2 * REL_CLIP:0:-1],
         jnp.zeros((N_HEADS, 2), F32)], axis=1)
    tiled = jnp.broadcast_to(ext[:, None, :], (N_HEADS, BLK, width + 1)).reshape(N_HEADS, BLK * (width + 1))
    return tiled[:, BLK - 1:BLK - 1 + BLK * width].reshape(N_HEADS, BLK, width)[:, :, :BAND_W]


def band_bias_window_grad(dwin):
    width = BAND_W + BLK
    far = BAND_W + BLK - 1 - 2 * REL_CLIP
    flat = jnp.pad(dwin, ((0, 0), (0, 0), (0, BLK))).reshape(N_HEADS, BLK * width)
    skew = jnp.pad(flat, ((0, 0), (BLK - 1, 1))).reshape(N_HEADS, BLK, width + 1)
    dext = jnp.sum(skew, axis=1)[:, :width - 1]
    return jnp.concatenate(
        [jnp.zeros((N_HEADS, 1), F32), dext[:, :far - 1:-1][:, :2 * REL_CLIP - 1],
         dext[:, far:far + 1] + jnp.sum(dext[:, :far], axis=1, keepdims=True)], axis=1)


SG_GROUPS = 8


def _gelu_parts(x):
    inner = GELU_C0 * (x + GELU_C1 * (x * x * x))
    th = jnp.tanh(inner)
    return th, 0.5 * x * (1.0 + th)


def _sg_gate_mask():
    row = lax.broadcasted_iota(jnp.int32, (BLK, BLK), 0)
    col = lax.broadcasted_iota(jnp.int32, (BLK, BLK), 1)
    return (row // 64) >= (col // 64)


def _sg_forward_parts(a, lng):
    w = a.shape[1] // 2
    th, z = _gelu_parts(a)
    u, v = z[:, :w], z[:, w:]
    mu = jnp.mean(v, axis=-1, keepdims=True)
    xc = v - mu
    rstd = lax.rsqrt(jnp.mean(xc * xc, axis=-1, keepdims=True) + EPS)
    vhat = xc * rstd
    return th, u, vhat, rstd, vhat * lng


def sg_fwd(a, lng, ws, bias_t, name):
    t, w2 = a.shape
    w = w2 // 2
    gc = w // SG_GROUPS

    def body(a_ref, lng_ref, ws_ref, bt_ref, y_ref):
        _, u, _, _, vln = _sg_forward_parts(a_ref[...], lng_ref[...])
        mask = _sg_gate_mask()
        bt = bt_ref[...]
        lane = lax.broadcasted_iota(jnp.int32, (BLK, BLK), 1)
        for g in range(SG_GROUPS):
            wg = jnp.where(mask, ws_ref[g], 0.0).astype(BF16)
            sv = jnp.dot(wg, vln[:, g * gc:(g + 1) * gc].astype(BF16), preferred_element_type=F32)
            bg = jnp.sum(jnp.where(lane == g, bt, 0.0), axis=-1, keepdims=True)
            y_ref[:, g * gc:(g + 1) * gc] = (u[:, g * gc:(g + 1) * gc] * (sv + bg)).astype(BF16)

    return _call(
        body, name=name, out_shape=_sds((t, w), BF16), grid=(t // BLK,),
        in_specs=[pl.BlockSpec((BLK, w2), lambda i: (i, 0)), pl.BlockSpec((1, w), lambda i: (0, 0)),
                  pl.BlockSpec((SG_GROUPS, BLK, BLK), lambda i: (0, 0, 0)),
                  pl.BlockSpec((BLK, BLK), lambda i: (0, 0))],
        out_specs=pl.BlockSpec((BLK, w), lambda i: (i, 0)),
        compiler_params=_params("parallel"),
    )(a, lng, ws, bias_t)


def sg_bwd(a, dy, lng, ws, bias_t, name):
    t, w2 = a.shape
    w = w2 // 2
    gc = w // SG_GROUPS

    def body(a_ref, dy_ref, lng_ref, ws_ref, bt_ref, da_ref, dlng_ref, dws_ref, dbt_ref):
        @pl.when(pl.program_id(0) == 0)
        def _():
            dlng_ref[...] = jnp.zeros_like(dlng_ref)
            dws_ref[...] = jnp.zeros_like(dws_ref)
            dbt_ref[...] = jnp.zeros_like(dbt_ref)

        av, lng = a_ref[...], lng_ref[...]
        th, u, vhat, rstd, vln = _sg_forward_parts(av, lng)
        mask = _sg_gate_mask()
        bt = bt_ref[...]
        lane = lax.broadcasted_iota(jnp.int32, (BLK, BLK), 1)
        dyv = dy_ref[...]
        du_parts, dvln_parts = [], []
        dbt = jnp.zeros((BLK, BLK), F32)
        for g in range(SG_GROUPS):
            sl = slice(g * gc, (g + 1) * gc)
            wg = jnp.where(mask, ws_ref[g], 0.0).astype(BF16)
            vg = vln[:, sl].astype(BF16)
            sv = jnp.dot(wg, vg, preferred_element_type=F32)
            bg = jnp.sum(jnp.where(lane == g, bt, 0.0), axis=-1, keepdims=True)
            dyg = dyv[:, sl]
            du_parts.append(dyg * (sv + bg))
            dsv = dyg * u[:, sl]
            dbt += jnp.where(lane == g, jnp.sum(dsv, axis=-1, keepdims=True), 0.0)
            dsvb = dsv.astype(BF16)
            dws_ref[g] += jnp.where(mask, lax.dot_general(dsvb, vg, NT, preferred_element_type=F32), 0.0)
            dvln_parts.append(lax.dot_general(wg, dsvb, TN, preferred_element_type=F32))
        dbt_ref[...] += dbt
        du = jnp.concatenate(du_parts, axis=1)
        dvln = jnp.concatenate(dvln_parts, axis=1)
        dlng_ref[...] += _colsum8(dvln * vhat)
        dvhat = dvln * lng
        dv = rstd * (dvhat - jnp.mean(dvhat, axis=-1, keepdims=True)
                     - vhat * jnp.mean(dvhat * vhat, axis=-1, keepdims=True))
        dz = jnp.concatenate([du, dv], axis=1)
        dgelu = 0.5 * (1.0 + th) + (0.5 * av) * (1.0 - th * th) * (GELU_C0 * (1.0 + 3.0 * GELU_C1 * (av * av)))
        da_ref[...] = (dz * dgelu).astype(BF16)

    return _call(
        body, name=name,
        out_shape=[_sds((t, w2), BF16), _sds((8, w), F32), _sds((SG_GROUPS, BLK, BLK), F32), _sds((BLK, BLK), F32)],
        grid=(t // BLK,),
        in_specs=[pl.BlockSpec((BLK, w2), lambda i: (i, 0)), pl.BlockSpec((BLK, w), lambda i: (i, 0)),
                  pl.BlockSpec((1, w), lambda i: (0, 0)),
                  pl.BlockSpec((SG_GROUPS, BLK, BLK), lambda i: (0, 0, 0)),
                  pl.BlockSpec((BLK, BLK), lambda i: (0, 0))],
        out_specs=[pl.BlockSpec((BLK, w2), lambda i: (i, 0)), pl.BlockSpec((8, w), lambda i: (0, 0)),
                   pl.BlockSpec((SG_GROUPS, BLK, BLK), lambda i: (0, 0, 0)),
                   pl.BlockSpec((BLK, BLK), lambda i: (0, 0))],
        compiler_params=_params("arbitrary"),
    )(a, dy, lng, ws, bias_t)


def _shift_down(cat, n, tr):
    return pltpu.roll(cat, n, 0)[8:8 + tr]


def _shift_up(cat, n, tr):
    return pltpu.roll(cat, tr + 8 - n, 0)[0:tr]


def conv_fwd(p, cw, name):
    t, d3 = p.shape
    d = d3 // 3
    tr = min(256, t)
    hb = tr // 8

    def body(p_ref, ph_ref, cw_ref, o_ref):
        i = pl.program_id(0)
        pv = p_ref[...]
        y = pv[:, d:2 * d] * pv[:, 2 * d:]
        ph = ph_ref[...]
        yh = jnp.where(i > 0, ph[:, d:2 * d] * ph[:, 2 * d:], 0.0)
        cat = jnp.concatenate([yh, y], axis=0)
        yc = (cw_ref[0:1, :] * _shift_down(cat, 2, tr) + cw_ref[1:2, :] * _shift_down(cat, 1, tr)
              + cw_ref[2:3, :] * y)
        o_ref[...] = (pv[:, :d] * yc).astype(BF16)

    return _call(
        body, name=name, out_shape=_sds((t, d), BF16), grid=(t // tr,),
        in_specs=[pl.BlockSpec((tr, d3), lambda i: (i, 0)),
                  pl.BlockSpec((8, d3), lambda i: (jnp.maximum(i * hb - 1, 0), 0)),
                  pl.BlockSpec((8, d), lambda i: (0, 0))],
        out_specs=pl.BlockSpec((tr, d), lambda i: (i, 0)),
        compiler_params=_params("parallel"),
    )(p, p, cw)


def conv_bwd(p, dz, cw, name):
    t, d3 = p.shape
    d = d3 // 3
    tr = min(256, t)
    hb = tr // 8
    nt = t // tr

    def body(p_ref, ph_ref, pn_ref, dz_ref, dzn_ref, cw_ref, dp_ref, dcw_ref):
        i = pl.program_id(0)

        @pl.when(i == 0)
        def _():
            dcw_ref[...] = jnp.zeros_like(dcw_ref)

        pv = p_ref[...]
        gb, gcv, xt = pv[:, :d], pv[:, d:2 * d], pv[:, 2 * d:]
        y = gcv * xt
        ph = ph_ref[...]
        yh = jnp.where(i > 0, ph[:, d:2 * d] * ph[:, 2 * d:], 0.0)
        cat = jnp.concatenate([yh, y], axis=0)
        y2, y1 = _shift_down(cat, 2, tr), _shift_down(cat, 1, tr)
        w0, w1, w2 = cw_ref[0:1, :], cw_ref[1:2, :], cw_ref[2:3, :]
        yc = w0 * y2 + w1 * y1 + w2 * y
        dzv = dz_ref[...]
        dyc = dzv * gb
        dcw_ref[0] += _colsum8(dyc * y2)
        dcw_ref[1] += _colsum8(dyc * y1)
        dcw_ref[2] += _colsum8(dyc * y)
        dycn = jnp.where(i < nt - 1, dzn_ref[...] * pn_ref[...][:, :d], 0.0)
        catn = jnp.concatenate([dyc, dycn], axis=0)
        dy = w2 * dyc + w1 * _shift_up(catn, 1, tr) + w0 * _shift_up(catn, 2, tr)
        dp_ref[:, :d] = (dzv * yc).astype(BF16)
        dp_ref[:, d:2 * d] = (dy * xt).astype(BF16)
        dp_ref[:, 2 * d:] = (dy * gcv).astype(BF16)

    nxt = lambda i: (jnp.minimum((i + 1) * hb, t // 8 - 1), 0)
    return _call(
        body, name=name, out_shape=[_sds((t, d3), BF16), _sds((3, 8, d), F32)], grid=(nt,),
        in_specs=[pl.BlockSpec((tr, d3), lambda i: (i, 0)),
                  pl.BlockSpec((8, d3), lambda i: (jnp.maximum(i * hb - 1, 0), 0)),
                  pl.BlockSpec((8, d3), nxt),
                  pl.BlockSpec((tr, d), lambda i: (i, 0)),
                  pl.BlockSpec((8, d), nxt),
                  pl.BlockSpec((8, d), lambda i: (0, 0))],
        out_specs=[pl.BlockSpec((tr, d3), lambda i: (i, 0)), pl.BlockSpec((3, 8, d), lambda i: (0, 0, 0))],
        compiler_params=_params("arbitrary"),
    )(p, p, p, dz, dz, cw)


def ada_fwd(c_all, w, b, name):
    nl, d, n = w.shape

    def body(c_ref, w_ref, b_ref, o_ref):
        cv = c_ref[...]
        s = (cv * _sigmoid(cv)).astype(BF16)
        o_ref[...] = jnp.dot(s, w_ref[...].astype(BF16), preferred_element_type=F32) + b_ref[...]

    return _call(
        body, name=name, out_shape=_sds((nl, N_DEV, n), F32), grid=(nl,),
        in_specs=[pl.BlockSpec((N_DEV, d), lambda l: (0, 0)), pl.BlockSpec((None, d, n), lambda l: (l, 0, 0)),
                  pl.BlockSpec((None, 1, n), lambda l: (l, 0, 0))],
        out_specs=pl.BlockSpec((None, N_DEV, n), lambda l: (l, 0, 0)),
        compiler_params=_params("parallel"),
    )(c_all, w, b)


def ada_bwd(c_all, dmod, name):
    nl, _, n = dmod.shape
    d = c_all.shape[1]

    def body(c_ref, dm_ref, o_ref):
        cv = c_ref[...]
        s = (cv * _sigmoid(cv)).astype(BF16)
        o_ref[...] = lax.dot_general(s, dm_ref[...].astype(BF16), TN, preferred_element_type=F32)

    return _call(
        body, name=name, out_shape=_sds((nl, d, n), F32), grid=(nl,),
        in_specs=[pl.BlockSpec((N_DEV, d), lambda l: (0, 0)), pl.BlockSpec((None, N_DEV, n), lambda l: (l, 0, 0))],
        out_specs=pl.BlockSpec((None, d, n), lambda l: (l, 0, 0)),
        compiler_params=_params("parallel"),
    )(c_all, dmod)


def adamw(pieces, w, m, v, name, riders=None):
    nl = len(pieces)
    npc, r, c = pieces[0].shape
    tr = r
    for cand in (1024, 512, 256, 128, 64, 32, 16, 8):
        if r % cand == 0 and cand * c * 4 <= (1 << 20):
            tr = cand
            break
    nt = r // tr

    def update(p_ref, w_ref, m_ref, v_ref, g_ref, d_ref, nm_ref, nv_ref):
        g = p_ref[0].astype(F32)
        for i in range(1, npc):
            g = g + p_ref[i].astype(F32)
        wv = w_ref[...]
        nm = ADAM_B1 * m_ref[...] + (1.0 - ADAM_B1) * g
        nv = ADAM_B2 * v_ref[...] + (1.0 - ADAM_B2) * (g * g)
        m_hat = nm / (1.0 - ADAM_B1 ** ADAM_STEP)
        v_hat = nv / (1.0 - ADAM_B2 ** ADAM_STEP)
        g_ref[...] = g
        d_ref[...] = -ADAM_LR * (m_hat / (jnp.sqrt(v_hat) + ADAM_EPS) + ADAM_WD * wv)
        nm_ref[...] = nm
        nv_ref[...] = nv

    def body(*refs):
        if nl == 1:
            update(*refs)
        else:
            for j in range(nl):
                pl.when(pl.program_id(0) == j)(lambda j=j: update(refs[j], *refs[nl:]))

    row = pl.BlockSpec((tr, c), lambda l, i: (l * nt + i, 0))
    piece_specs = [pl.BlockSpec((npc, tr, c), lambda l, i, j=j: (0, jnp.where(l == j, i, 0), 0))
                   for j in range(nl)]
    res, ridden = _rider_call(
        body, (*pieces, w, m, v), riders, name=name, out_shape=[_sds((nl * r, c), F32)] * 4, grid=(nl, nt),
        in_specs=piece_specs + [row, row, row], out_specs=[row] * 4, scratch_shapes=[],
        sem=("parallel", "parallel"))
    return res, ridden


def sum_pieces(pieces, name):
    npc, r, c = pieces.shape

    def body(p_ref, o_ref):
        g = p_ref[0]
        for i in range(1, npc):
            g = g + p_ref[i]
        o_ref[...] = g

    return _call(body, name=name, out_shape=_sds((r, c), F32),
                 in_specs=[pl.BlockSpec(memory_space=pltpu.VMEM)],
                 out_specs=pl.BlockSpec(memory_space=pltpu.VMEM),
                 compiler_params=pltpu.CompilerParams(vmem_limit_bytes=VMEM_LIMIT))(pieces)


PACK_W = 1024


def _pack(arrs):
    flat = jnp.concatenate([a.reshape(-1).astype(F32) for a in arrs])
    rows = -(-flat.shape[0] // (8 * PACK_W)) * 8
    return jnp.pad(flat, (0, rows * PACK_W - flat.shape[0])).reshape(rows, PACK_W)


def _unpack(slab, shapes):
    flat = slab.reshape(-1)
    out, off = [], 0
    for s in shapes:
        n = 1
        for q in s:
            n *= q
        out.append(flat[off:off + n].reshape(s))
        off += n
    return out


def kernel(x, c, ada_w, ada_b, norm_g, ffn_w_in, ffn_w_out, sb_w_qkv, sb_w_o, sg_w_in, sg_ln_g, sg_w_s, sg_bias, sg_w_out, sc_w_in, sc_conv_w, sc_w_out, cb_w_qkv, cb_rel_bias, cb_w_o, loss_target, m_ada_w, m_ada_b, m_norm_g, m_ffn_w_in, m_ffn_w_out, m_sb_w_qkv, m_sb_w_o, m_sg_w_in, m_sg_ln_g, m_sg_w_s, m_sg_bias, m_sg_w_out, m_sc_w_in, m_sc_conv_w, m_sc_w_out, m_cb_w_qkv, m_cb_rel_bias, m_cb_w_o, v_ada_w, v_ada_b, v_norm_g, v_ffn_w_in, v_ffn_w_out, v_sb_w_qkv, v_sb_w_o, v_sg_w_in, v_sg_ln_g, v_sg_w_s, v_sg_bias, v_sg_w_out, v_sc_w_in, v_sc_conv_w, v_sc_w_out, v_cb_w_qkv, v_cb_rel_bias, v_cb_w_o):
    depth = ada_w.shape[0]
    d = D_MODEL
    xi, yi, ci = lax.axis_index("x"), lax.axis_index("y"), lax.axis_index("c")
    me = 4 * xi + 2 * yi + ci
    x0 = x[0]
    t = x0.shape[0]
    target = loss_target[0]

    c_g, ng, small, w_qkv0 = _all_gather(
        [jnp.pad(c, ((0, 7), (0, 0))), norm_g.reshape(depth * 4, d // N_DEV), _pack([sg_ln_g, sc_conv_w]),
         sb_w_qkv[0].astype(BF16)], "gather_setup")

    c_all = c_g[:, 0, :]
    na = ada_w.shape[2]
    b_cols = lax.dynamic_slice_in_dim(ada_b, me * na, na, axis=1)[:, None, :]
    mod_part = ada_fwd(c_all, ada_w, b_cols, "ada_fwd")
    mod_g = _all_gather([mod_part.reshape(depth * N_DEV, na)], "gather_mod")[0]
    mod_g = mod_g.reshape(N_DEV, depth, N_DEV, na)
    mod_me = lax.dynamic_index_in_dim(mod_g, me, axis=2, keepdims=False)
    mod = jnp.transpose(mod_me, (1, 0, 2)).reshape(depth, 6, 1, d)

    norm_full = jnp.transpose(ng, (1, 0, 2)).reshape(depth, 4, 1, d)
    small = small.reshape(N_DEV, -1)
    nl_g = sg_ln_g.shape[1]
    ln_full = small[:, :nl_g].reshape(1, N_DEV * nl_g)
    cwn = sc_conv_w.shape[2]
    cw_sh = small[:, nl_g:nl_g + 3 * cwn].reshape(N_DEV, 3, cwn)
    cw_full = jnp.transpose(cw_sh, (1, 0, 2)).reshape(3, d)
    cw_pad = jnp.pad(cw_full, ((0, 5), (0, 0)))

    bf = lambda a: a.astype(BF16)
    mixers = [
        [bf(sb_w_qkv[0]), bf(sb_w_o[0])],
        [bf(sg_w_in[0]), bf(sg_w_out[0])],
        [bf(sc_w_in[0]), bf(sc_w_out[0])],
        [bf(cb_w_qkv[0]), bf(cb_w_o[0])],
    ]
    shards = [[bf(ffn_w_in[i]), bf(ffn_w_out[i])] + mixers[i % 4] for i in range(depth)]
    gathered = [[None] * 4 for _ in range(depth)]
    gathered[0][2] = w_qkv0
    gather_plan = {
        ("sb_fwd", 0): [(0, 0), (0, 1), (0, 3), (1, 2), (2, 0)],
        ("ffn_in", 0): [(1, 0)], ("ffn_out", 0): [(1, 1), (1, 3)],
        ("mix_in", 1): [(3, 0)], ("ffn_in", 1): [(2, 1), (2, 3)], ("ffn_out", 1): [(2, 2)],
        ("ffn_in", 2): [(3, 1), (3, 3)], ("ffn_out", 2): [(3, 2)],
    }

    def gather_riders(host, layer):
        slots = gather_plan.get((host, layer), [])
        return (_Gather, [shards[l][s] for l, s in slots]) if slots else None

    def place_gathered(host, layer, ridden):
        for (l, s), arr in zip(gather_plan.get((host, layer), []), ridden):
            gathered[l][s] = arr

    bias_win = band_bias_window(cb_rel_bias[0])
    ws = sg_w_s[0]
    bias_t = jnp.pad(sg_bias[0].T, ((0, 0), (0, BLK - SG_GROUPS)))

    saved = []
    xcur = x0
    h = pre_fwd(x0, norm_full[0, 0], mod[0, 0], mod[0, 1], "L0_pre_m")
    for i in range(depth):
        mi = i % 4
        sh_m, sc_m, gt_m, sh_f, sc_f, gt_f = [mod[i, j] for j in range(6)]
        g0, g1, g2, g3 = [norm_full[i, j] for j in range(4)]
        tag = "L%d_" % i
        sv = {"x_in": xcur, "h_m": h}
        if mi == 0:
            qkv = mm_cs(h, gathered[0][2], tag + "qkv", out_dtype=BF16)
            qkv_t = jnp.transpose(qkv.reshape(t // SB_GW, SB_GW, 3 * d), (0, 2, 1))
            o, cmass, ridden = sb_fwd(qkv, qkv_t, gather_riders("sb_fwd", i)[1], tag + "sb_fwd")
            place_gathered("sb_fwd", i, ridden)
            sv.update(qkv=qkv, qkv_t=qkv_t, o=o, cmass=cmass)
            mixed = o
        else:
            w_in = gathered[i][2]
            out_dtype = BF16 if mi == 3 else F32
            if gather_riders("mix_in", i):
                pre, ridden = mm_cs(h, w_in, tag + "mix_in", out_dtype=out_dtype, riders=gather_riders("mix_in", i))
                place_gathered("mix_in", i, ridden)
            else:
                pre = mm_cs(h, w_in, tag + "mix_in", out_dtype=out_dtype)
            if mi == 1:
                mixed = sg_fwd(pre, ln_full, ws, bias_t, tag + "sg_fwd")
                sv.update(a=pre, yy=mixed)
            elif mi == 2:
                mixed = conv_fwd(pre, cw_pad, tag + "conv_fwd")
                sv.update(p=pre, gz=mixed)
            else:
                mixed = band_fwd(pre, bias_win, tag + "band_fwd")
                sv.update(qkv=pre, o=mixed)
        wfi, wfo, _, wmo = gathered[i]
        wfo4 = wfo.reshape(4, -1, d)
        y = mm(mixed, wmo.reshape(-1, d), tag + "mix_out")
        sv["y_m"] = y
        xmid, h2 = post_pre_fwd(xcur, y, g1, gt_m, g2, sh_f, sc_f, tag + "post_m_pre_f")
        sv["x_mid"] = xmid
        ag, au, s3, ridden = ffn_in_swiglu(h2, wfi, tag + "ffn_in", riders=gather_riders("ffn_in", i))
        place_gathered("ffn_in", i, ridden)
        if gather_riders("ffn_out", i):
            y2, ridden = mm_rs(s3, wfo4, tag + "ffn_out", riders=gather_riders("ffn_out", i))
            place_gathered("ffn_out", i, ridden)
        else:
            y2 = mm_rs(s3, wfo4, tag + "ffn_out")
        sv.update(h_f=h2, ag=ag, au=au, s3=s3, y_f=y2)
        saved.append(sv)
        if i + 1 < depth:
            xcur, h = post_pre_fwd(xmid, y2, g3, gt_f, norm_full[i + 1, 0], mod[i + 1, 0], mod[i + 1, 1],
                                   tag + "post_f_pre_m")

    last = depth - 1
    dx, dy2, lpart, dgt_f, dg3 = post_loss_bwd(saved[last]["x_mid"], saved[last]["y_f"], norm_full[last, 3],
                                               mod[last, 5], target, "loss")
    loss = lax.psum(0.5 * jnp.sum(lpart) / d, ("x", "y", "c"))

    dmod_rows = [None] * depth
    dnorm_rows = [None] * depth
    big_pieces = [[None] * 4 for _ in range(depth)]
    pending_dwmi, pending_dwfi, pending_layers = None, [], []
    small_grads = {}
    for i in reversed(range(depth)):
        wfi, wfo, wmi, wmo = gathered[i]
        wfo4 = wfo.reshape(4, -1, d)
        wmo2 = wmo.reshape(-1, d)
        wmi_t = jnp.transpose(wmi, (0, 2, 1)).reshape(-1, d)
        mi = i % 4
        sh_m, sc_m, gt_m, sh_f, sc_f, gt_f = [mod[i, j] for j in range(6)]
        g0, g1, g2, g3 = [norm_full[i, j] for j in range(4)]
        tag = "L%d_b_" % i
        sv = saved[i]
        da3, _ = ffn_out_dx_swiglu(dy2, wfo4, sv["ag"], sv["au"], tag + "ffn_out_dx")
        dwfo = mm_rs_dw(sv["s3"], dy2, tag + "ffn_out_dw", out_dtype=BF16).reshape(N_DEV, -1, d)
        dh2, (big_pieces[i][1],) = mm_cs_dx(da3, wfi, tag + "ffn_in_dx", act_major=True, riders=(_Exchange, [dwfo]))
        if pending_dwmi is not None:
            dwfi, (big_pieces[i + 1][2],) = mm_cs_dw(sv["h_f"], da3, tag + "ffn_in_dw", act_major=True,
                                                     out_dtype=BF16, riders=(_Exchange, [pending_dwmi]))
        else:
            dwfi = mm_cs_dw(sv["h_f"], da3, tag + "ffn_in_dw", act_major=True, out_dtype=BF16)
        dx, dy, dsh_f, dsc_f, dg2, dgt_m, dg1 = pre_post_bwd(
            dh2, sv["x_mid"], g2, sc_f, dx, sv["y_m"], g1, gt_m, tag + "pre_f_post_m")
        if mi == 0:
            do = mm_nt(dy, wmo2, tag + "wo_dx", out_dtype=BF16)
            dwmo = mm_tn(sv["o"], dy, tag + "wo_dw", out_dtype=BF16).reshape(N_DEV, -1, d)
            dq, dk, dv, ridden = sb_bwd(sv["qkv"], sv["qkv_t"], do, sv["cmass"],
                                        pending_dwfi + [dwfi, dwmo], tag + "sb_bwd")
            for n, j in enumerate(pending_layers):
                big_pieces[j][0] = ridden[n]
            big_pieces[0][0], big_pieces[0][3] = ridden[-2:]
            dmid = jnp.concatenate([dq, dk, dv], axis=1)
        elif mi == 1:
            dyy = mm_nt(dy, wmo2, tag + "sg_out_dx")
            dwmo = mm_tn(sv["yy"], dy, tag + "sg_out_dw", out_dtype=BF16).reshape(N_DEV, -1, d)
            dmid, dlng, dws, dbt = sg_bwd(sv["a"], dyy, ln_full, ws, bias_t, tag + "sg_bwd")
            small_grads.update(ln_g=jnp.sum(dlng, axis=0), w_s=dws, bias=dbt[:, :SG_GROUPS].T)
        elif mi == 2:
            dgz = mm_nt(dy, wmo2, tag + "sc_out_dx")
            dwmo = mm_tn(sv["gz"], dy, tag + "sc_out_dw", out_dtype=BF16).reshape(N_DEV, -1, d)
            dmid, dcw = conv_bwd(sv["p"], dgz, cw_pad, tag + "conv_bwd")
            small_grads.update(conv_w=jnp.sum(dcw, axis=1))
        else:
            do = mm_nt(dy, wmo2, tag + "wo_dx", out_dtype=BF16)
            dwmo = mm_tn(sv["o"], dy, tag + "wo_dw", out_dtype=BF16).reshape(N_DEV, -1, d)
            dq, dk, dv, dwin, (big_pieces[i][0], big_pieces[i][3]) = band_bwd(
                sv["qkv"], do, bias_win, tag + "band_bwd", riders=(_Exchange, [dwfi, dwmo]))
            dmid = jnp.concatenate([dq, dk, dv], axis=1)
            small_grads.update(rel_bias=band_bias_window_grad(dwin))
        if mi in (1, 2):
            dh, (big_pieces[i][3],) = mm(dmid, wmi_t, tag + "mix_in_dx", riders=(_Exchange, [dwmo]))
            pending_dwfi.append(dwfi)
            pending_layers.append(i)
        else:
            dh = mm(dmid, wmi_t, tag + "mix_in_dx")
        pending_dwmi = mm_cs_dw(sv["h_m"], dmid, tag + "mix_in_dw", out_dtype=BF16)
        if i > 0:
            dx, dy2_prev, dsh_m, dsc_m, dg0, dgt_f_prev, dg3_prev = pre_post_bwd(
                dh, sv["x_in"], g0, sc_m, dx, saved[i - 1]["y_f"], norm_full[i - 1, 3], mod[i - 1, 5],
                tag + "pre_m_post_f")
        else:
            dx, dsh_m, dsc_m, dg0 = pre_bwd(dh, sv["x_in"], g0, sc_m, dx, tag + "pre_m")
        dmod_rows[i] = jnp.stack([jnp.sum(q, axis=0) for q in (dsh_m, dsc_m, dgt_m, dsh_f, dsc_f, dgt_f)])
        dnorm_rows[i] = jnp.stack([jnp.sum(q, axis=0) for q in (dg0, dg1, dg2, dg3)])
        if i > 0:
            dy2, dgt_f, dg3 = dy2_prev, dgt_f_prev, dg3_prev
    grad_x = dx[None]

    out_g, out_d, out_m, out_v = {}, {}, {}, {}

    def upd(name, pieces, w, m, v, riders=None):
        rows_cols = (len(pieces) * pieces[0].shape[1], pieces[0].shape[2])
        res, ridden = adamw(pieces, w.reshape(rows_cols), m.reshape(rows_cols), v.reshape(rows_cols),
                            "adamw_" + name, riders)
        out_g[name], out_d[name], out_m[name], out_v[name] = [q.reshape(w.shape) for q in res]
        return ridden

    dmod_mine = jnp.stack(dmod_rows).reshape(depth, 6 * d)
    dnorm_mine = jnp.stack(dnorm_rows)
    small_list = [dnorm_mine, small_grads["ln_g"], small_grads["bias"], small_grads["conv_w"],
                  small_grads["rel_bias"]]
    small_shapes = [dmod_mine.shape] + [a.shape for a in small_list]
    slab = _pack([dmod_mine] + small_list)
    big_pieces[0][2], = upd("ffn_w_in", [big_pieces[i][0] for i in range(depth)], ffn_w_in, m_ffn_w_in, v_ffn_w_in,
                            riders=(_Exchange, [pending_dwmi]))
    slab_g, ws_g = upd("ffn_w_out", [big_pieces[i][1] for i in range(depth)], ffn_w_out, m_ffn_w_out, v_ffn_w_out,
                       riders=(_Gather, [slab, small_grads["w_s"].reshape(-1, BLK)]))
    upd("sg_w_s", [ws_g], sg_w_s, m_sg_w_s, v_sg_w_s)
    upd("sb_w_qkv", [big_pieces[0][2]], sb_w_qkv, m_sb_w_qkv, v_sb_w_qkv)
    upd("sb_w_o", [big_pieces[0][3]], sb_w_o, m_sb_w_o, v_sb_w_o)
    upd("sg_w_in", [big_pieces[1][2]], sg_w_in, m_sg_w_in, v_sg_w_in)
    upd("sg_w_out", [big_pieces[1][3]], sg_w_out, m_sg_w_out, v_sg_w_out)
    upd("sc_w_in", [big_pieces[2][2]], sc_w_in, m_sc_w_in, v_sc_w_in)
    upd("sc_w_out", [big_pieces[2][3]], sc_w_out, m_sc_w_out, v_sc_w_out)
    upd("cb_w_qkv", [big_pieces[3][2]], cb_w_qkv, m_cb_w_qkv, v_cb_w_qkv)
    upd("cb_w_o", [big_pieces[3][3]], cb_w_o, m_cb_w_o, v_cb_w_o)

    tot = sum_pieces(slab_g, "sum_small_grads")
    g_ada_b_full, g_norm, g_ln, g_sbias, g_cw, g_rb = _unpack(tot, small_shapes)
    dmod_all = slab_g.reshape(N_DEV, -1)[:, :depth * 6 * d].reshape(N_DEV, depth, 6 * d)
    dmod_cols = lax.dynamic_slice_in_dim(dmod_all, me * na, na, axis=2)
    g_ada_w = ada_bwd(c_all, jnp.transpose(dmod_cols, (1, 0, 2)), "ada_bwd")

    nsh = d // N_DEV
    g_norm_sh = lax.dynamic_slice_in_dim(g_norm, me * nsh, nsh, axis=2)
    g_ln_sh = lax.dynamic_slice_in_dim(g_ln.reshape(1, -1), me * nl_g, nl_g, axis=1)
    g_cw_sh = lax.dynamic_slice_in_dim(g_cw, me * cwn, cwn, axis=1)[None]

    upd("ada_w", [g_ada_w.reshape(1, depth * d, na)], ada_w, m_ada_w, v_ada_w)

    small_names = ["ada_b", "norm_g", "sg_ln_g", "sg_bias", "sc_conv_w", "cb_rel_bias"]
    small_g = [g_ada_b_full, g_norm_sh, g_ln_sh, g_sbias[None], g_cw_sh, g_rb[None]]
    small_w = [ada_b, norm_g, sg_ln_g, sg_bias, sc_conv_w, cb_rel_bias]
    small_m = [m_ada_b, m_norm_g, m_sg_ln_g, m_sg_bias, m_sc_conv_w, m_cb_rel_bias]
    small_v = [v_ada_b, v_norm_g, v_sg_ln_g, v_sg_bias, v_sc_conv_w, v_cb_rel_bias]
    shapes = [w.shape for w in small_w]
    res, _ = adamw([_pack(small_g)[None]], _pack(small_w), _pack(small_m), _pack(small_v), "adamw_small")
    for nm_, gs, ds_, ms, vs in zip(small_names, *[_unpack(r, shapes) for r in res]):
        out_g[nm_], out_d[nm_], out_m[nm_], out_v[nm_] = gs, ds_, ms, vs

    order = ["ada_w", "ada_b", "norm_g", "ffn_w_in", "ffn_w_out", "sb_w_qkv", "sb_w_o", "sg_w_in", "sg_ln_g",
             "sg_w_s", "sg_bias", "sg_w_out", "sc_w_in", "sc_conv_w", "sc_w_out", "cb_w_qkv", "cb_rel_bias", "cb_w_o"]
    return (loss, grad_x, *[out_g[n] for n in order], *[out_d[n] for n in order],
            *[out_m[n] for n in order], *[out_v[n] for n in order])
```

```python
import jax
import jax.numpy as jnp
from jax import lax
from jax.experimental import pallas as pl
from jax.experimental.pallas import tpu as pltpu

F32 = jnp.float32
BF16 = jnp.bfloat16
MESH = pl.DeviceIdType.MESH

N_DEV = 8
D_MODEL = 1024
N_HEADS = 16
HEAD_DIM = 64
QK_SCALE = HEAD_DIM ** -0.5
BLK = 128
BAND_BLOCKS = 5
BAND_W = BAND_BLOCKS * BLK
REL_CLIP = 128
EPS = 1e-6
NEG = -1e30
GELU_C0 = 0.7978845608028654
GELU_C1 = 0.044715
ADAM_LR = 0.001
ADAM_B1 = 0.9
ADAM_B2 = 0.999
ADAM_EPS = 1e-08
ADAM_WD = 0.01
ADAM_STEP = 10
VMEM_LIMIT = 56 * 1024 * 1024


def _call(body, **kw):
    return pl.pallas_call(body, **kw)


def _params(*sem):
    return pltpu.CompilerParams(dimension_semantics=sem, vmem_limit_bytes=VMEM_LIMIT)


def _sds(shape, dtype):
    return jax.ShapeDtypeStruct(tuple(shape), dtype)


def _row_tile(t):
    return min(512, t)


def _me():
    x, y, c = lax.axis_index("x"), lax.axis_index("y"), lax.axis_index("c")
    return x, y, c


def _all_gather(arrs, name):
    n = len(arrs)

    def body(*refs):
        gather = _Gather(refs[:n], refs[n:2 * n], *refs[2 * n:])
        gather.start()
        gather.forward()
        gather.finish()

    any_spec = pl.BlockSpec(memory_space=pl.ANY)
    outs = _call(
        body,
        name=name,
        out_shape=_Gather.out_shapes(arrs),
        in_specs=[any_spec] * n,
        out_specs=[any_spec] * n,
        scratch_shapes=_comm_sems(n),
    )(*arrs)
    return list(outs)


def _comm_sems(n):
    if n == 0:
        return []
    return [pltpu.SemaphoreType.DMA((n, 7)), pltpu.SemaphoreType.DMA((n, 7)), pltpu.SemaphoreType.DMA((n,))]


class _Gather:
    def __init__(self, x_refs, o_refs, send_sems, recv_sems, local_sems):
        self.x_refs, self.o_refs = x_refs, o_refs
        self.send_sems, self.recv_sems, self.local_sems = send_sems, recv_sems, local_sems
        x, y, c = _me()
        self.c = c
        self.me, self.sibling = (x, y, c), (x, y, 1 - c)
        self.chips = [(1 - x, y), (x, 1 - y), (1 - x, 1 - y)]

    @staticmethod
    def out_shapes(arrs):
        return [_sds((N_DEV,) + a.shape, a.dtype) for a in arrs]

    def rows(self, a, block):
        px, py, pc = block
        return self.o_refs[a].at[4 * px + 2 * py + pc]

    def copy(self, a, k, block, to, own=False):
        return pltpu.make_async_remote_copy(
            src_ref=self.x_refs[a] if own else self.rows(a, block),
            dst_ref=self.rows(a, block),
            send_sem=self.send_sems.at[a, k],
            recv_sem=self.recv_sems.at[a, k],
            device_id=to,
            device_id_type=MESH,
        )

    def local(self, a):
        return pltpu.make_async_copy(self.x_refs[a], self.rows(a, self.me), self.local_sems.at[a])

    def first(self, a):
        cps = [self.copy(a, 0, self.me, self.sibling, own=True)]
        return cps + [self.copy(a, 1 + j, self.me, (*chip, self.c), own=True) for j, chip in enumerate(self.chips)]

    def passed(self, a):
        return [self.copy(a, 4 + j, (*chip, self.c), self.sibling) for j, chip in enumerate(self.chips)]

    def start(self):
        for a in range(len(self.x_refs)):
            self.local(a).start()
            for cp in self.first(a):
                cp.start()

    def forward(self):
        for a in range(len(self.x_refs)):
            passed = self.passed(a)
            for j, chip in enumerate(self.chips):
                self.copy(a, 1 + j, (*chip, self.c), self.me).wait_recv()
                passed[j].start()

    def finish(self):
        for a in range(len(self.x_refs)):
            self.copy(a, 0, self.sibling, self.me).wait_recv()
            for j, chip in enumerate(self.chips):
                self.copy(a, 4 + j, (*chip, 1 - self.c), self.me).wait_recv()
            for cp in self.first(a) + self.passed(a):
                cp.wait_send()
            self.local(a).wait()


class _Exchange:
    def __init__(self, x_refs, o_refs, send_sems, recv_sems, local_sems):
        self.x_refs, self.o_refs = x_refs, o_refs
        self.send_sems, self.recv_sems, self.local_sems = send_sems, recv_sems, local_sems
        x, y, c = _me()
        self.me = 4 * x + 2 * y + c
        self.peers = []
        for k in range(1, N_DEV):
            px = 1 - x if k & 4 else x
            py = 1 - y if k & 2 else y
            pc = 1 - c if k & 1 else c
            self.peers.append((px, py, pc))

    @staticmethod
    def out_shapes(arrs):
        return [_sds(a.shape, a.dtype) for a in arrs]

    def local(self, a):
        return pltpu.make_async_copy(self.x_refs[a].at[self.me], self.o_refs[a].at[self.me], self.local_sems.at[a])

    def copy(self, a, k, send):
        px, py, pc = self.peers[k]
        peer = 4 * px + 2 * py + pc
        return pltpu.make_async_remote_copy(
            src_ref=self.x_refs[a].at[peer],
            dst_ref=self.o_refs[a].at[self.me if send else peer],
            send_sem=self.send_sems.at[a, k], recv_sem=self.recv_sems.at[a, k],
            device_id=(px, py, pc), device_id_type=MESH)

    def start(self):
        for a in range(len(self.x_refs)):
            self.local(a).start()
            for k in range(N_DEV - 1):
                self.copy(a, k, True).start()

    def finish(self):
        for a in range(len(self.x_refs)):
            for k in range(N_DEV - 1):
                self.copy(a, k, False).wait_recv()
            for k in range(N_DEV - 1):
                self.copy(a, k, True).wait_send()
            self.local(a).wait()


NN = (((1,), (0,)), ((), ()))
NT = (((1,), (1,)), ((), ()))
TN = (((0,), (0,)), ((), ()))


def _all_of(conds):
    out = conds[0]
    for cond in conds[1:]:
        out = out & cond
    return out


def _rider_call(body, operands, riders, *, name, out_shape, grid, in_specs, out_specs, scratch_shapes, sem):
    if not riders or not riders[1]:
        res = _call(body, name=name, out_shape=out_shape, grid=grid, in_specs=in_specs, out_specs=out_specs,
                    scratch_shapes=list(scratch_shapes), compiler_params=_params(*sem))(*operands)
        return list(res), []
    cls, arrs = riders
    nr, ni, no, ns = len(arrs), len(in_specs), len(out_specs), len(scratch_shapes)
    forward_at = (3 * grid[0]) // 4 if grid[0] >= 4 else None

    def wrapped(*refs):
        ids = [pl.program_id(ax) for ax in range(len(grid))]
        comm = cls(refs[ni:ni + nr], refs[ni + nr + no:ni + 2 * nr + no], *refs[ni + 2 * nr + no + ns:])
        pl.when(_all_of([i == 0 for i in ids]))(comm.start)
        if cls is _Gather and forward_at is not None:
            pl.when(_all_of([ids[0] == forward_at] + [i == 0 for i in ids[1:]]))(comm.forward)
        body(*refs[:ni], *refs[ni + nr:ni + nr + no], *refs[ni + 2 * nr + no:ni + 2 * nr + no + ns])

        def end():
            if cls is _Gather and forward_at is None:
                comm.forward()
            comm.finish()

        pl.when(_all_of([i == g - 1 for i, g in zip(ids, grid)]))(end)

    any_spec = pl.BlockSpec(memory_space=pl.ANY)
    res = _call(wrapped, name=name, out_shape=list(out_shape) + cls.out_shapes(arrs), grid=grid,
                in_specs=list(in_specs) + [any_spec] * nr, out_specs=list(out_specs) + [any_spec] * nr,
                scratch_shapes=list(scratch_shapes) + _comm_sems(nr),
                compiler_params=_params(*(["arbitrary"] * len(grid))))(*operands, *arrs)
    return list(res[:no]), list(res[no:])


def _gemm(a, b, out_shape, out_dtype, grid, a_spec, b_spec, o_spec, acc_shape, dims, name, riders=None):
    nk = grid[2]

    if nk == 1:
        def body(a_ref, b_ref, o_ref):
            r = lax.dot_general(a_ref[...].astype(BF16), b_ref[...].astype(BF16), dims,
                                preferred_element_type=F32)
            o_ref[...] = r.astype(o_ref.dtype)
        scratch = []
    else:
        def body(a_ref, b_ref, o_ref, acc_ref):
            k = pl.program_id(2)

            @pl.when(k == 0)
            def _():
                acc_ref[...] = jnp.zeros_like(acc_ref)

            acc_ref[...] += lax.dot_general(a_ref[...].astype(BF16), b_ref[...].astype(BF16), dims,
                                            preferred_element_type=F32)

            @pl.when(k == nk - 1)
            def _():
                o_ref[...] = acc_ref[...].astype(o_ref.dtype)
        scratch = [pltpu.VMEM(acc_shape, F32)]

    res, ridden = _rider_call(
        body, (a, b), riders, name=name, out_shape=[_sds(out_shape, out_dtype)], grid=grid,
        in_specs=[a_spec, b_spec], out_specs=[o_spec], scratch_shapes=scratch,
        sem=("parallel", "parallel", "arbitrary"))
    return (res[0], ridden) if riders else res[0]


def _div_tile(n, want):
    if n <= want:
        return n
    t = want - want % 128
    while n % t:
        t -= 128
    return t


def mm(a, b, name, out_dtype=F32, tm=512, tn=1024, tk=1024, riders=None):
    m, k = a.shape
    n = b.shape[1]
    tm, tn, tk = _div_tile(m, tm), _div_tile(n, tn), _div_tile(k, tk)
    return _gemm(a, b, (m, n), out_dtype, (m // tm, n // tn, k // tk),
                 pl.BlockSpec((tm, tk), lambda i, j, kk: (i, kk)),
                 pl.BlockSpec((tk, tn), lambda i, j, kk: (kk, j)),
                 pl.BlockSpec((tm, tn), lambda i, j, kk: (i, j)),
                 (tm, tn), NN, name, riders)


def mm_nt(a, b, name, out_dtype=F32, tm=512, tn=1024, tk=1024):
    m, n = a.shape
    k = b.shape[0]
    tm, tk_out, tred = _div_tile(m, tm), _div_tile(k, tn), _div_tile(n, tk)
    return _gemm(a, b, (m, k), out_dtype, (m // tm, k // tk_out, n // tred),
                 pl.BlockSpec((tm, tred), lambda i, j, kk: (i, kk)),
                 pl.BlockSpec((tk_out, tred), lambda i, j, kk: (j, kk)),
                 pl.BlockSpec((tm, tk_out), lambda i, j, kk: (i, j)),
                 (tm, tk_out), NT, name)


def mm_tn(a, b, name, out_dtype=F32, tm=512, tn=1024, tk=1024):
    m, k = a.shape
    n = b.shape[1]
    tk_out, tn, tred = _div_tile(k, tk), _div_tile(n, tn), _div_tile(m, tm)
    return _gemm(a, b, (k, n), out_dtype, (k // tk_out, n // tn, m // tred),
                 pl.BlockSpec((tred, tk_out), lambda i, j, kk: (kk, i)),
                 pl.BlockSpec((tred, tn), lambda i, j, kk: (kk, j)),
                 pl.BlockSpec((tk_out, tn), lambda i, j, kk: (i, j)),
                 (tk_out, tn), TN, name)


def mm_cs(a, wg, name, act_major=False, out_dtype=F32, tm=1024, riders=None):
    m, k = a.shape
    s, _, n = wg.shape
    tm = _div_tile(m, tm)
    if act_major:
        out_shape, o_spec = (s, m, n), pl.BlockSpec((None, tm, n), lambda i, j, kk: (j, i, 0))
    else:
        out_shape, o_spec = (m, s * n), pl.BlockSpec((tm, n), lambda i, j, kk: (i, j))
    return _gemm(a, wg, out_shape, out_dtype, (m // tm, s, 1),
                 pl.BlockSpec((tm, k), lambda i, j, kk: (i, 0)),
                 pl.BlockSpec((None, k, n), lambda i, j, kk: (j, 0, 0)),
                 o_spec, (tm, n), NN, name, riders)


def mm_cs_dx(da, wg, name, act_major=False, out_dtype=F32, tm=1024, riders=None):
    s, k, n = wg.shape
    m = da.shape[1] if act_major else da.shape[0]
    tm = _div_tile(m, tm)
    if act_major and s % 2 == 0:
        steps = s // 2

        def body(a_ref, w_ref, o_ref, acc_ref):
            kk = pl.program_id(2)

            @pl.when(kk == 0)
            def _():
                acc_ref[...] = jnp.zeros_like(acc_ref)

            acc_ref[...] += (lax.dot_general(a_ref[0], w_ref[0], NT, preferred_element_type=F32)
                             + lax.dot_general(a_ref[1], w_ref[1], NT, preferred_element_type=F32))

            @pl.when(kk == steps - 1)
            def _():
                o_ref[...] = acc_ref[...].astype(o_ref.dtype)

        res, ridden = _rider_call(
            body, (da, wg), riders, name=name, out_shape=[_sds((m, k), out_dtype)], grid=(m // tm, 1, steps),
            in_specs=[pl.BlockSpec((2, tm, n), lambda i, j, kk: (kk, i, 0)),
                      pl.BlockSpec((2, k, n), lambda i, j, kk: (kk, 0, 0))],
            out_specs=[pl.BlockSpec((tm, k), lambda i, j, kk: (i, 0))],
            scratch_shapes=[pltpu.VMEM((tm, k), F32)], sem=("parallel", "parallel", "arbitrary"))
        return (res[0], ridden) if riders else res[0]
    if act_major:
        a_spec = pl.BlockSpec((None, tm, n), lambda i, j, kk: (kk, i, 0))
    else:
        a_spec = pl.BlockSpec((tm, n), lambda i, j, kk: (i, kk))
    return _gemm(da, wg, (m, k), out_dtype, (m // tm, 1, s), a_spec,
                 pl.BlockSpec((None, k, n), lambda i, j, kk: (kk, 0, 0)),
                 pl.BlockSpec((tm, k), lambda i, j, kk: (i, 0)),
                 (tm, k), NT, name, riders)


def mm_cs_dw(a, da, name, act_major=False, out_dtype=F32, tm=1024, riders=None):
    m, k = a.shape
    if act_major:
        s, _, n = da.shape
    else:
        s, n = N_DEV, da.shape[1] // N_DEV
    tm = _div_tile(m, tm)
    if act_major:
        b_spec = pl.BlockSpec((None, tm, n), lambda i, j, kk: (i, kk, 0))
    else:
        b_spec = pl.BlockSpec((tm, n), lambda i, j, kk: (kk, i))
    return _gemm(a, da, (s, k, n), out_dtype, (s, 1, m // tm),
                 pl.BlockSpec((tm, k), lambda i, j, kk: (kk, 0)), b_spec,
                 pl.BlockSpec((None, k, n), lambda i, j, kk: (i, 0, 0)),
                 (k, n), TN, name, riders)


def mm_rs(s3, w3, name, out_dtype=F32, tm=1024, riders=None):
    s, m, n = s3.shape
    nn = w3.shape[2]
    tm = _div_tile(m, tm)
    return _gemm(s3, w3, (m, nn), out_dtype, (m // tm, 1, s),
                 pl.BlockSpec((None, tm, n), lambda i, j, kk: (kk, i, 0)),
                 pl.BlockSpec((None, n, nn), lambda i, j, kk: (kk, 0, 0)),
                 pl.BlockSpec((tm, nn), lambda i, j, kk: (i, 0)),
                 (tm, nn), NN, name, riders)


def mm_rs_dw(s3, dy, name, out_dtype=F32, tm=1024):
    s, m, n = s3.shape
    nn = dy.shape[1]
    tm = _div_tile(m, tm)
    return _gemm(s3, dy, (s, n, nn), out_dtype, (s, 1, m // tm),
                 pl.BlockSpec((None, tm, n), lambda i, j, kk: (i, kk, 0)),
                 pl.BlockSpec((tm, nn), lambda i, j, kk: (kk, 0)),
                 pl.BlockSpec((None, n, nn), lambda i, j, kk: (i, 0, 0)),
                 (n, nn), TN, name)


def _colsum8(v):
    tr, d = v.shape
    return v.reshape(tr // 8, 8, d).sum(axis=0)


def _rstd(v):
    return lax.rsqrt(jnp.mean(v * v, axis=-1, keepdims=True) + EPS)


def _vec_spec(d):
    return pl.BlockSpec((1, d), lambda i: (0, 0))


def _acc_spec(d):
    return pl.BlockSpec((8, d), lambda i: (0, 0))


def _pre_rows(xv, g, shift, scale):
    return ((xv * _rstd(xv)) * g) * (1 + scale) + shift


def _post_rows(xv, yv, g, gate):
    return xv + gate * ((yv * _rstd(yv)) * g)


def _post_bwd_rows(dxv, yv, g, gate):
    r = _rstd(yv)
    yhat = yv * r
    dgate = _colsum8(dxv * (yhat * g))
    dyn = gate * dxv
    dg = _colsum8(dyn * yhat)
    dyhat = dyn * g
    dy = r * (dyhat - yhat * jnp.mean(dyhat * yhat, axis=-1, keepdims=True))
    return dy, dgate, dg


def _pre_bwd_rows(dhv, xv, g, scale, dxn):
    r = _rstd(xv)
    xhat = xv * r
    dshift = _colsum8(dhv)
    dscale = _colsum8(dhv * (xhat * g))
    dmod = dhv * (1 + scale)
    dg = _colsum8(dmod * xhat)
    dxhat = dmod * g
    dx = r * (dxhat - xhat * jnp.mean(dxhat * xhat, axis=-1, keepdims=True)) + dxn
    return dx, dshift, dscale, dg


def _row_call(body, name, t, d, rows_in, vecs_in, rows_out, n_acc):
    tr = _row_tile(t)
    nri, nvi, nro = len(rows_in), len(vecs_in), len(rows_out)

    def wrapped(*refs):
        accs = refs[nri + nvi + nro:]
        if n_acc:
            @pl.when(pl.program_id(0) == 0)
            def _():
                for acc in accs:
                    acc[...] = jnp.zeros_like(acc)
        body(*refs)

    row = pl.BlockSpec((tr, d), lambda i: (i, 0))
    return _call(wrapped, name=name,
                 out_shape=[_sds((t, d), dt) for dt in rows_out] + [_sds((8, d), F32)] * n_acc,
                 grid=(t // tr,), in_specs=[row] * nri + [_vec_spec(d)] * nvi,
                 out_specs=[row] * nro + [_acc_spec(d)] * n_acc,
                 compiler_params=_params("arbitrary" if n_acc else "parallel"))(*rows_in, *vecs_in)


def pre_fwd(x, g, shift, scale, name):
    def body(x_ref, g_ref, sh_ref, sc_ref, h_ref):
        h_ref[...] = _pre_rows(x_ref[...], g_ref[...], sh_ref[...], sc_ref[...]).astype(BF16)

    return _row_call(body, name, *x.shape, [x], [g, shift, scale], [BF16], 0)[0]


def post_pre_fwd(x, y, g_post, gate, g_pre, shift, scale, name):
    def body(x_ref, y_ref, gp_ref, gt_ref, g_ref, sh_ref, sc_ref, xn_ref, h_ref):
        xn = _post_rows(x_ref[...], y_ref[...], gp_ref[...], gt_ref[...])
        xn_ref[...] = xn
        h_ref[...] = _pre_rows(xn, g_ref[...], sh_ref[...], sc_ref[...]).astype(BF16)

    return _row_call(body, name, *x.shape, [x, y], [g_post, gate, g_pre, shift, scale], [F32, BF16], 0)


def post_loss_bwd(x, y, g, gate, target, name):
    d = x.shape[1]

    def body(x_ref, y_ref, t_ref, g_ref, gt_ref, dx_ref, dy_ref, l_ref, dgate_ref, dg_ref):
        yv, gv, gate_v = y_ref[...], g_ref[...], gt_ref[...]
        err = _post_rows(x_ref[...], yv, gv, gate_v) - t_ref[...]
        l_ref[...] += _colsum8(err * err)
        dxv = err * (1.0 / d)
        dx_ref[...] = dxv
        dy, dgate, dg = _post_bwd_rows(dxv, yv, gv, gate_v)
        dy_ref[...] = dy.astype(BF16)
        dgate_ref[...] += dgate
        dg_ref[...] += dg

    return _row_call(body, name, *x.shape, [x, y, target], [g, gate], [F32, BF16], 3)


def pre_post_bwd(dh, x, g_pre, scale, dxn, y, g_post, gate, name):
    def body(dh_ref, x_ref, dxn_ref, y_ref, g_ref, sc_ref, gp_ref, gt_ref,
             dx_ref, dy_ref, dsh_ref, dsc_ref, dg_ref, dgate_ref, dgp_ref):
        dx, dsh, dsc, dg = _pre_bwd_rows(dh_ref[...].astype(F32), x_ref[...], g_ref[...], sc_ref[...], dxn_ref[...])
        dx_ref[...] = dx
        dsh_ref[...] += dsh
        dsc_ref[...] += dsc
        dg_ref[...] += dg
        dy, dgate, dgp = _post_bwd_rows(dx, y_ref[...], gp_ref[...], gt_ref[...])
        dy_ref[...] = dy.astype(BF16)
        dgate_ref[...] += dgate
        dgp_ref[...] += dgp

    return _row_call(body, name, *x.shape, [dh, x, dxn, y], [g_pre, scale, g_post, gate], [F32, BF16], 5)


def pre_bwd(dh, x, g, scale, dxn, name):
    def body(dh_ref, x_ref, dxn_ref, g_ref, sc_ref, dx_ref, dsh_ref, dsc_ref, dg_ref):
        dx, dsh, dsc, dg = _pre_bwd_rows(dh_ref[...].astype(F32), x_ref[...], g_ref[...], sc_ref[...], dxn_ref[...])
        dx_ref[...] = dx
        dsh_ref[...] += dsh
        dsc_ref[...] += dsc
        dg_ref[...] += dg

    return _row_call(body, name, *x.shape, [dh, x, dxn], [g, scale], [F32], 3)


def _sigmoid(x):
    return 1.0 / (1.0 + jnp.exp(-x))


def ffn_in_swiglu(h, wg, name, tm=1024, riders=None):
    m, k = h.shape
    s, _, n = wg.shape
    half = s // 2
    tm = _div_tile(m, tm)

    def body(h_ref, wg_ref, wu_ref, g_ref, u_ref, s_ref):
        hv = h_ref[...]
        g = jnp.dot(hv, wg_ref[...], preferred_element_type=F32)
        u = jnp.dot(hv, wu_ref[...], preferred_element_type=F32)
        g_ref[...] = g
        u_ref[...] = u
        s_ref[...] = ((g * _sigmoid(g)) * u).astype(BF16)

    act = pl.BlockSpec((None, tm, n), lambda i, j: (j, i, 0))
    res, ridden = _rider_call(
        body, (h, wg, wg), riders, name=name,
        out_shape=[_sds((half, m, n), F32), _sds((half, m, n), F32), _sds((half, m, n), BF16)],
        grid=(m // tm, half),
        in_specs=[pl.BlockSpec((tm, k), lambda i, j: (i, 0)),
                  pl.BlockSpec((None, k, n), lambda i, j: (j, 0, 0)),
                  pl.BlockSpec((None, k, n), lambda i, j: (j + half, 0, 0))],
        out_specs=[act, act, act], scratch_shapes=[], sem=("parallel", "parallel"))
    return res[0], res[1], res[2], ridden


def ffn_out_dx_swiglu(dy, w4, gate, up, name, tm=1024, riders=None):
    m, nn = dy.shape
    half, n, _ = w4.shape
    tm = _div_tile(m, tm)

    def body(dy_ref, w_ref, g_ref, u_ref, o_ref):
        ds = lax.dot_general(dy_ref[...], w_ref[...], NT, preferred_element_type=F32)
        g, u = g_ref[...], u_ref[...]
        sig = _sigmoid(g)
        o_ref[0] = (ds * u * (sig * (1 + g * (1 - sig)))).astype(BF16)
        o_ref[1] = (ds * (g * sig)).astype(BF16)

    act = pl.BlockSpec((None, tm, n), lambda i, j: (j, i, 0))
    res, ridden = _rider_call(
        body, (dy, w4, gate, up), riders, name=name, out_shape=[_sds((2, half, m, n), BF16)],
        grid=(m // tm, half),
        in_specs=[pl.BlockSpec((tm, nn), lambda i, j: (i, 0)),
                  pl.BlockSpec((None, n, nn), lambda i, j: (j, 0, 0)), act, act],
        out_specs=[pl.BlockSpec((2, None, tm, n), lambda i, j: (0, j, i, 0))],
        scratch_shapes=[], sem=("parallel", "parallel"))
    return res[0].reshape(2 * half, m, n), ridden


def _split_hi_lo(v):
    hi = v.astype(BF16)
    lo = (v - hi.astype(F32)).astype(BF16)
    return hi, lo


SB_G = 2
SB_EXP_ZERO = 104.0
SB_UNSEEN = 3e38
SB_GW = SB_G * BLK


def _sb_specs(t):
    nq = t // BLK
    npair = N_HEADS // 2
    q_spec = pl.BlockSpec((BLK, BLK), lambda p, qb: (qb, p))
    k_spec = pl.BlockSpec((t, BLK), lambda p, qb: (0, npair + p))
    v_spec = pl.BlockSpec((t, BLK), lambda p, qb: (0, 2 * npair + p))
    kt_spec = pl.BlockSpec((t // SB_GW, BLK, SB_GW), lambda p, qb: (0, npair + p, 0))
    vt_spec = pl.BlockSpec((t // SB_GW, BLK, SB_GW), lambda p, qb: (0, 2 * npair + p, 0))
    c_spec = pl.BlockSpec((None, nq, 8, 2 * BLK), lambda p, qb: (p, 0, 0, qb))
    return nq, npair, q_spec, k_spec, v_spec, kt_spec, vt_spec, c_spec


def _sb_consts():
    row = lax.broadcasted_iota(jnp.int32, (BLK, BLK), 0)
    col = lax.broadcasted_iota(jnp.int32, (BLK, BLK), 1)
    lane0 = (col < HEAD_DIM).astype(F32)
    sub0 = (row < HEAD_DIM).astype(F32)
    return row, col, lane0, sub0


def _sb_valid(ks, qb):
    row = lax.broadcasted_iota(jnp.int32, (SB_GW, 2 * BLK), 0)
    col = lax.broadcasted_iota(jnp.int32, (SB_GW, 2 * BLK), 1)
    return (ks + row) < (qb * BLK + (col & (BLK - 1)))


def _blocks_on_lanes(v4):
    return jnp.concatenate([v4[b * BLK:(b + 1) * BLK] for b in range(SB_G)], axis=1)


def _tri2_dot(tri2, v):
    hi, lo = _split_hi_lo(v)
    return jnp.dot(tri2, jnp.concatenate([hi, lo], axis=0), preferred_element_type=F32)


def _sb_pair_loop(first, count, step, group, skip, carry):
    def pair(it, cy):
        g1 = first + 2 * step * it
        cy = group(g1, 0, 1, cy)
        return lax.cond(2 * it + 1 < count, lambda c: group(g1 + step, 1, 0, c), skip, cy)
    return lax.fori_loop(0, (count + 1) // 2, pair, carry)


def sb_fwd(qkv, qkv_t, riders, name):
    t = qkv.shape[0]
    assert t % SB_GW == 0
    nq, npair, q_spec, k_spec, _, _, vt_spec, c_spec = _sb_specs(t)
    nr = len(riders)

    def body(*refs):
        q_ref, k_ref, vt_ref = refs[:3]
        o_ref, c_ref = refs[3 + nr:5 + nr]
        oacc, zbuf0, zbuf1, kmax = refs[5 + 2 * nr:9 + 2 * nr]
        pp = pl.program_id(0)
        qb = pl.program_id(1)
        if nr:
            gather = _Gather(refs[3:3 + nr], refs[5 + nr:5 + 2 * nr], *refs[9 + 2 * nr:])
            pl.when((pp == 0) & (qb == 0))(gather.start)
            pl.when((pp == npair - 2) & (qb == 0))(gather.forward)
        _sb_fwd_step(q_ref, k_ref, vt_ref, o_ref, c_ref, oacc, zbuf0, zbuf1, kmax, qb)
        if nr:
            pl.when((pp == npair - 1) & (qb == nq - 1))(gather.finish)

    any_spec = pl.BlockSpec(memory_space=pl.ANY)
    outs = _call(
        body, name=name,
        out_shape=[_sds((t, D_MODEL), BF16), _sds((npair, nq, 8, 2 * t), F32)] + _Gather.out_shapes(riders),
        grid=(npair, nq), in_specs=[q_spec, k_spec, vt_spec] + [any_spec] * nr,
        out_specs=[pl.BlockSpec((BLK, BLK), lambda p, qb: (qb, p)), c_spec] + [any_spec] * nr,
        scratch_shapes=[pltpu.VMEM((BLK, 2 * BLK), F32), pltpu.VMEM((SB_GW, 2 * BLK), F32),
                        pltpu.VMEM((SB_GW, 2 * BLK), F32), pltpu.VMEM((8, BLK), F32)] + _comm_sems(nr),
        compiler_params=_params("arbitrary", "arbitrary"),
    )(qkv, qkv, qkv_t, *riders)
    return outs[0], outs[1], list(outs[2:])


def _sb_fwd_step(q_ref, k_ref, vt_ref, o_ref, c_ref, oacc, zbuf0, zbuf1, kmax, qb):
    row, col, lane0, sub0 = _sb_consts()
    tri = (col >= row).astype(BF16)
    tri2 = jnp.concatenate([tri, tri], axis=1)
    q2 = _two_heads(q_ref[...], lane0, QK_SCALE)
    zbufs = (zbuf0, zbuf1)
    c_ref[...] = jnp.full(c_ref.shape, SB_UNSEEN, F32)
    oacc[...] = jnp.zeros_like(oacc)

    @pl.when(qb == 0)
    def _():
        ksq = jnp.square(k_ref[...].astype(F32))
        head0 = (lax.broadcasted_iota(jnp.int32, (1, BLK), 1) < HEAD_DIM).astype(F32)
        norms = jnp.maximum(jnp.sum(ksq * head0, axis=1, keepdims=True),
                            jnp.sum(ksq * (1.0 - head0), axis=1, keepdims=True))
        kmax[...] = jnp.broadcast_to(jnp.max(norms, axis=0, keepdims=True), kmax.shape)

    qsq = jnp.square(q2.astype(F32)).astype(BF16)
    qn2 = jnp.max(lax.dot_general(jnp.ones((8, BLK), BF16), qsq, NT, preferred_element_type=F32),
                  axis=0, keepdims=True)
    kk = kmax[0:1, :]
    zbound = jnp.sqrt(qn2 * jnp.concatenate([kk, kk], axis=1)) * 1.02

    def matters(cr):
        return (jnp.min(cr - zbound) <= SB_EXP_ZERO).astype(jnp.int32)

    def scores(g):
        ks = pl.multiple_of(g * SB_GW, SB_GW)
        return lax.dot_general(k_ref[pl.ds(ks, SB_GW), :], q2, NT, preferred_element_type=F32)

    def group(g, cur, nxt, cr, masked=False):
        z = zbufs[cur][...]
        zbufs[nxt][...] = scores(jnp.maximum(g - 1, 0))
        e = jnp.exp(-jnp.abs(z))
        sp = jnp.maximum(z, 0.0) + jnp.log(1.0 + e)
        if masked:
            valid = _sb_valid(g * SB_GW, qb)
            sp = jnp.where(valid, sp, 0.0)
        loc = _tri2_dot(tri2, _blocks_on_lanes(sp))
        parts = [None] * SB_G
        for b in reversed(range(SB_G)):
            rows = slice(b * BLK, (b + 1) * BLK)
            c_ref[g * SB_G + b] = jnp.broadcast_to(cr, (8, 2 * BLK))
            a = jnp.exp(z[rows] - (loc[:, 2 * b * BLK:2 * (b + 1) * BLK] + cr))
            if masked:
                a = jnp.where(valid[rows], a, 0.0)
            parts[b] = a.astype(BF16)
            cr = cr + jnp.sum(sp[rows], axis=0, keepdims=True)
        oacc[...] += jnp.dot(vt_ref[g], jnp.concatenate(parts, axis=0), preferred_element_type=F32)
        return cr

    last = qb // SB_G
    zbuf1[...] = scores(last)
    cr = group(last, 1, 0, jnp.zeros((1, 2 * BLK), F32), masked=True)

    def pair(state):
        g, cr, _ = state
        cr = group(g, 0, 1, cr)
        more = (g >= 1).astype(jnp.int32) * matters(cr)
        cr = lax.cond(more > 0, lambda c: group(g - 1, 1, 0, c), lambda c: c, cr)
        return jnp.where(more > 0, g - 2, -1), cr, matters(cr)

    lax.while_loop(lambda st: (st[0] >= 0) & (st[2] > 0), pair, (last - 1, cr, matters(cr)))
    o_t = oacc[:, :BLK] * sub0 + oacc[:, BLK:] * (1.0 - sub0)
    o_ref[...] = o_t.T.astype(BF16)


def sb_bwd(qkv, qkv_t, do, cmass, riders, name):
    t = qkv.shape[0]
    nq, npair, q_spec, k_spec, v_spec, kt_spec, _, c_spec = _sb_specs(t)
    nr = len(riders)

    def body(*refs):
        pp = pl.program_id(0)
        qb = pl.program_id(1)
        if nr:
            exchange = _Exchange(refs[6:6 + nr], refs[9 + nr:9 + 2 * nr], *refs[14 + 2 * nr:])
            pl.when((pp == 0) & (qb == 0))(exchange.start)
        step(*refs[:6], *refs[6 + nr:9 + nr], *refs[9 + 2 * nr:14 + 2 * nr])
        if nr:
            pl.when((pp == npair - 1) & (qb == nq - 1))(exchange.finish)

    def step(q_ref, k_ref, kt_ref, v_ref, do_ref, c_ref, dq_ref, dk_ref, dv_ref, dqacc, dkacc, dvacc,
             zbuf0, zbuf1):
        qb = pl.program_id(1)

        @pl.when(qb == 0)
        def _():
            dkacc[...] = jnp.zeros_like(dkacc)
            dvacc[...] = jnp.zeros_like(dvacc)

        row, col, lane0, sub0 = _sb_consts()
        tri_suf = (col >= row).astype(BF16)
        tri_pre = (col <= row).astype(BF16)
        tri2_suf = jnp.concatenate([tri_suf, tri_suf], axis=1)
        tri2_pre = jnp.concatenate([tri_pre, tri_pre], axis=1)
        q2 = _two_heads(q_ref[...], lane0, QK_SCALE)
        do2 = _two_heads(do_ref[...], lane0, 1.0)
        zbufs = (zbuf0, zbuf1)
        dqacc[...] = jnp.zeros_like(dqacc)
        last = qb // SB_G

        def scores(g):
            ks = pl.multiple_of(g * SB_GW, SB_GW)
            return lax.dot_general(k_ref[pl.ds(ks, SB_GW), :], q2, NT, preferred_element_type=F32)

        def group(g, cur, nxt, gc, masked=False):
            ks = pl.multiple_of(g * SB_GW, SB_GW)
            z = zbufs[cur][...]
            zbufs[nxt][...] = scores(jnp.minimum(g + 1, last))
            e = jnp.exp(-jnp.abs(z))
            sig = 0.5 * jnp.tanh(0.5 * z) + 0.5
            sp = jnp.maximum(z, 0.0) + jnp.log(1.0 + e)
            if masked:
                valid = _sb_valid(ks, qb)
                sp = jnp.where(valid, sp, 0.0)
            loc = _tri2_dot(tri2_suf, _blocks_on_lanes(sp))
            parts = []
            for b in range(SB_G):
                rows = slice(b * BLK, (b + 1) * BLK)
                mass = loc[:, 2 * b * BLK:2 * (b + 1) * BLK] + c_ref[g * SB_G + b, 0:1, :]
                parts.append(jnp.exp(z[rows] - mass))
            a = jnp.concatenate(parts, axis=0)
            if masked:
                a = jnp.where(valid, a, 0.0)
            gr = lax.dot_general(v_ref[pl.ds(ks, SB_GW), :], do2, NT, preferred_element_type=F32) * a
            pre = _tri2_dot(tri2_pre, _blocks_on_lanes(gr))
            parts = []
            for b in range(SB_G):
                rows = slice(b * BLK, (b + 1) * BLK)
                parts.append(pre[:, 2 * b * BLK:2 * (b + 1) * BLK] + gc)
                gc = gc + jnp.sum(gr[rows], axis=0, keepdims=True)
            dz = gr - sig * jnp.concatenate(parts, axis=0)
            if masked:
                dz = jnp.where(valid, dz, 0.0)
            dz = dz.astype(BF16)
            dkacc[pl.ds(ks, SB_GW), :] += jnp.dot(dz, q2, preferred_element_type=F32)
            dqacc[...] += jnp.dot(kt_ref[g], dz, preferred_element_type=F32)
            dvacc[pl.ds(ks, SB_GW), :] += jnp.dot(a.astype(BF16), do2, preferred_element_type=F32)
            return gc

        def skip(gc):
            zbuf0[...] = zbuf1[...]
            return gc

        def unseen(g):
            return (jnp.max(c_ref[g * SB_G + SB_G - 1, 0:1, :]) > 0.5 * SB_UNSEEN).astype(jnp.int32)

        first, _ = lax.while_loop(lambda st: (st[0] > 0) & (st[1] == 0),
                                  lambda st: (st[0] - 1, unseen(jnp.maximum(st[0] - 2, 0))),
                                  (last, unseen(jnp.maximum(last - 1, 0))))
        zbuf0[...] = scores(first)
        gc = _sb_pair_loop(first, last - first, 1, group, skip, jnp.zeros((1, 2 * BLK), F32))
        group(last, 0, 1, gc, masked=True)
        dq_t = (dqacc[:, :BLK] * sub0 + dqacc[:, BLK:] * (1.0 - sub0)) * QK_SCALE
        dq_ref[...] = dq_t.T.astype(BF16)

        @pl.when(qb == nq - 1)
        def _():
            dk_ref[...] = dkacc[...].astype(BF16)
            dv_ref[...] = dvacc[...].astype(BF16)

    col_spec = pl.BlockSpec((t, BLK), lambda p, qb: (0, p))
    blk_spec = pl.BlockSpec((BLK, BLK), lambda p, qb: (qb, p))
    any_spec = pl.BlockSpec(memory_space=pl.ANY)
    outs = _call(
        body, name=name,
        out_shape=[_sds((t, D_MODEL), BF16)] * 3 + [_sds(r.shape, r.dtype) for r in riders],
        grid=(npair, nq), in_specs=[q_spec, k_spec, kt_spec, v_spec, blk_spec, c_spec] + [any_spec] * nr,
        out_specs=[blk_spec, col_spec, col_spec] + [any_spec] * nr,
        scratch_shapes=[pltpu.VMEM((BLK, 2 * BLK), F32), pltpu.VMEM((t, BLK), F32), pltpu.VMEM((t, BLK), F32),
                        pltpu.VMEM((SB_GW, 2 * BLK), F32), pltpu.VMEM((SB_GW, 2 * BLK), F32)] + _comm_sems(nr),
        compiler_params=_params("arbitrary", "arbitrary"),
    )(qkv, qkv, qkv_t, qkv, do, cmass, *riders)
    return outs[0], outs[1], outs[2], list(outs[3:])


BAND_QPS = 4


def _band_static_mask(jj):
    row = lax.broadcasted_iota(jnp.int32, (2 * BLK, BLK), 0)
    col = lax.broadcasted_iota(jnp.int32, (2 * BLK, BLK), 1)
    qc = (row & (BLK - 1)) // 64
    kc = 2 * jj + col // 64
    return (kc >= qc) & (kc <= qc + 8)


def _band_key_start(qb, jj):
    kb = qb - (BAND_BLOCKS - 1) + jj
    return kb, pl.multiple_of(jnp.maximum(kb, 0) * BLK, BLK)


def _band_probs(q2, k_ref, bias, qb):
    blocks = []
    for jj in range(BAND_BLOCKS):
        kb, ks = _band_key_start(qb, jj)
        s = lax.dot_general(q2, k_ref[pl.ds(ks, BLK), :], NT, preferred_element_type=F32)
        s = s + bias[:, jj * BLK:(jj + 1) * BLK]
        ok = (kb >= 0) if 0 < jj < BAND_BLOCKS - 1 else _band_static_mask(jj) & (kb >= 0)
        blocks.append(jnp.where(ok, s, NEG))
    s = jnp.concatenate(blocks, axis=1)
    m = jnp.max(s, axis=-1, keepdims=True)
    e = jnp.exp(s - m)
    return e / jnp.sum(e, axis=-1, keepdims=True)


def _band_specs(t):
    npair = N_HEADS // 2
    rows = BAND_QPS * BLK
    q_spec = pl.BlockSpec((rows, BLK), lambda p, i: (i, p))
    k_spec = pl.BlockSpec((t, BLK), lambda p, i: (0, npair + p))
    v_spec = pl.BlockSpec((t, BLK), lambda p, i: (0, 2 * npair + p))
    b_spec = pl.BlockSpec((2, BLK, BAND_W), lambda p, i: (p, 0, 0))
    return npair, t // rows, q_spec, k_spec, v_spec, b_spec


def _two_heads(xv, lane0, scale):
    xf = xv.astype(F32)
    if scale != 1.0:
        xf = xf * scale
    return jnp.concatenate([xf * lane0, xf * (1.0 - lane0)], axis=0).astype(BF16)


def _one_of_two_heads(r, lane0):
    return r[:BLK] * lane0 + r[BLK:] * (1.0 - lane0)


def band_fwd(qkv, bias, name):
    t = qkv.shape[0]
    assert t % (BAND_QPS * BLK) == 0
    npair, nsteps, q_spec, k_spec, v_spec, b_spec = _band_specs(t)

    def body(q_ref, k_ref, v_ref, b_ref, o_ref):
        step = pl.program_id(1)
        _, _, lane0, _ = _sb_consts()
        bias2 = b_ref[...].reshape(2 * BLK, BAND_W)
        for u in range(BAND_QPS):
            qb = step * BAND_QPS + u
            rows = slice(u * BLK, (u + 1) * BLK)
            q2 = _two_heads(q_ref[rows, :], lane0, QK_SCALE)
            p = _band_probs(q2, k_ref, bias2, qb)
            acc = jnp.zeros((2 * BLK, BLK), F32)
            for jj in range(BAND_BLOCKS):
                _, ks = _band_key_start(qb, jj)
                acc += jnp.dot(p[:, jj * BLK:(jj + 1) * BLK].astype(BF16), v_ref[pl.ds(ks, BLK), :],
                               preferred_element_type=F32)
            o_ref[rows, :] = _one_of_two_heads(acc, lane0).astype(BF16)

    return _call(
        body, name=name, out_shape=_sds((t, D_MODEL), BF16), grid=(npair, nsteps),
        in_specs=[q_spec, k_spec, v_spec, b_spec],
        out_specs=pl.BlockSpec((BAND_QPS * BLK, BLK), lambda p, i: (i, p)),
        compiler_params=_params("parallel", "parallel"),
    )(qkv, qkv, qkv, bias)


def band_bwd(qkv, do, bias, name, riders=None):
    t = qkv.shape[0]
    npair, nsteps, q_spec, k_spec, v_spec, b_spec = _band_specs(t)

    def body(q_ref, k_ref, v_ref, do_ref, b_ref, dq_ref, dk_ref, dv_ref, db_ref, dkacc, dvacc):
        step = pl.program_id(1)

        @pl.when(step == 0)
        def _():
            dkacc[...] = jnp.zeros_like(dkacc)
            dvacc[...] = jnp.zeros_like(dvacc)
            db_ref[...] = jnp.zeros_like(db_ref)

        _, _, lane0, _ = _sb_consts()
        bias2 = b_ref[...].reshape(2 * BLK, BAND_W)
        updates = []
        for u in range(BAND_QPS):
            qb = step * BAND_QPS + u
            rows = slice(u * BLK, (u + 1) * BLK)
            q2 = _two_heads(q_ref[rows, :], lane0, QK_SCALE)
            do2 = _two_heads(do_ref[rows, :], lane0, 1.0)
            p = _band_probs(q2, k_ref, bias2, qb)
            dp = jnp.concatenate(
                [lax.dot_general(do2, v_ref[pl.ds(_band_key_start(qb, jj)[1], BLK), :], NT,
                                 preferred_element_type=F32) for jj in range(BAND_BLOCKS)], axis=1)
            ds = p * (dp - jnp.sum(p * dp, axis=-1, keepdims=True))
            db_ref[...] += ds.reshape(2, BLK, BAND_W)
            dqa = jnp.zeros((2 * BLK, BLK), F32)
            for jj in range(BAND_BLOCKS):
                _, ks = _band_key_start(qb, jj)
                dsb = ds[:, jj * BLK:(jj + 1) * BLK].astype(BF16)
                pb = p[:, jj * BLK:(jj + 1) * BLK].astype(BF16)
                dqa += jnp.dot(dsb, k_ref[pl.ds(ks, BLK), :], preferred_element_type=F32)
                updates.append((ks, lax.dot_general(dsb, q2, TN, preferred_element_type=F32),
                                lax.dot_general(pb, do2, TN, preferred_element_type=F32)))
            dq_ref[rows, :] = (_one_of_two_heads(dqa, lane0) * QK_SCALE).astype(BF16)
        for ks, dk_part, dv_part in updates:
            dkacc[pl.ds(ks, BLK), :] += dk_part
            dvacc[pl.ds(ks, BLK), :] += dv_part

        @pl.when(step == nsteps - 1)
        def _():
            dk_ref[...] = dkacc[...].astype(BF16)
            dv_ref[...] = dvacc[...].astype(BF16)

    col_spec = pl.BlockSpec((t, BLK), lambda p, i: (0, p))
    blk_spec = pl.BlockSpec((BAND_QPS * BLK, BLK), lambda p, i: (i, p))
    res, ridden = _rider_call(
        body, (qkv, qkv, qkv, do, bias), riders, name=name,
        out_shape=[_sds((t, D_MODEL), BF16)] * 3 + [_sds((N_HEADS, BLK, BAND_W), F32)],
        grid=(npair, nsteps), in_specs=[q_spec, k_spec, v_spec, blk_spec, b_spec],
        out_specs=[blk_spec, col_spec, col_spec, b_spec],
        scratch_shapes=[pltpu.VMEM((t, BLK), F32), pltpu.VMEM((t, BLK), F32)],
        sem=("parallel", "arbitrary"))
    return res[0], res[1], res[2], res[3], ridden


def band_bias_window(rel_bias):
    far = BAND_W + BLK - 1 - 2 * REL_CLIP
    width = BAND_W + BLK
    ext = jnp.concatenate(
        [jnp.broadcast_to(rel_bias[:, 2 * REL_CLIP:], (N_HEADS, far)), rel_bias[:, 2 * REL_CLIP:0:-1],
         jnp.zeros((N_HEADS, 2), F32)], axis=1)
    tiled = jnp.broadcast_to(ext[:, None, :], (N_HEADS, BLK, width + 1)).reshape(N_HEADS, BLK * (width + 1))
    return tiled[:, BLK - 1:BLK - 1 + BLK * width].reshape(N_HEADS, BLK, width)[:, :, :BAND_W]


def band_bias_window_grad(dwin):
    width = BAND_W + BLK
    far = BAND_W + BLK - 1 - 2 * REL_CLIP
    flat = jnp.pad(dwin, ((0, 0), (0, 0), (0, BLK))).reshape(N_HEADS, BLK * width)
    skew = jnp.pad(flat, ((0, 0), (BLK - 1, 1))).reshape(N_HEADS, BLK, width + 1)
    dext = jnp.sum(skew, axis=1)[:, :width - 1]
    return jnp.concatenate(
        [jnp.zeros((N_HEADS, 1), F32), dext[:, :far - 1:-1][:, :2 * REL_CLIP - 1],
         dext[:, far:far + 1] + jnp.sum(dext[:, :far], axis=1, keepdims=True)], axis=1)


SG_GROUPS = 8


def _gelu_parts(x):
    inner = GELU_C0 * (x + GELU_C1 * (x * x * x))
    th = jnp.tanh(inner)
    return th, 0.5 * x * (1.0 + th)


def _sg_gate_mask():
    row = lax.broadcasted_iota(jnp.int32, (BLK, BLK), 0)
    col = lax.broadcasted_iota(jnp.int32, (BLK, BLK), 1)
    return (row // 64) >= (col // 64)


def _sg_forward_parts(a, lng):
    w = a.shape[1] // 2
    th, z = _gelu_parts(a)
    u, v = z[:, :w], z[:, w:]
    mu = jnp.mean(v, axis=-1, keepdims=True)
    xc = v - mu
    rstd = lax.rsqrt(jnp.mean(xc * xc, axis=-1, keepdims=True) + EPS)
    vhat = xc * rstd
    return th, u, vhat, rstd, vhat * lng


def sg_fwd(a, lng, ws, bias_t, name):
    t, w2 = a.shape
    w = w2 // 2
    gc = w // SG_GROUPS

    def body(a_ref, lng_ref, ws_ref, bt_ref, y_ref):
        _, u, _, _, vln = _sg_forward_parts(a_ref[...], lng_ref[...])
        mask = _sg_gate_mask()
        bt = bt_ref[...]
        lane = lax.broadcasted_iota(jnp.int32, (BLK, BLK), 1)
        for g in range(SG_GROUPS):
            wg = jnp.where(mask, ws_ref[g], 0.0).astype(BF16)
            sv = jnp.dot(wg, vln[:, g * gc:(g + 1) * gc].astype(BF16), preferred_element_type=F32)
            bg = jnp.sum(jnp.where(lane == g, bt, 0.0), axis=-1, keepdims=True)
            y_ref[:, g * gc:(g + 1) * gc] = (u[:, g * gc:(g + 1) * gc] * (sv + bg)).astype(BF16)

    return _call(
        body, name=name, out_shape=_sds((t, w), BF16), grid=(t // BLK,),
        in_specs=[pl.BlockSpec((BLK, w2), lambda i: (i, 0)), pl.BlockSpec((1, w), lambda i: (0, 0)),
                  pl.BlockSpec((SG_GROUPS, BLK, BLK), lambda i: (0, 0, 0)),
                  pl.BlockSpec((BLK, BLK), lambda i: (0, 0))],
        out_specs=pl.BlockSpec((BLK, w), lambda i: (i, 0)),
        compiler_params=_params("parallel"),
    )(a, lng, ws, bias_t)


def sg_bwd(a, dy, lng, ws, bias_t, name):
    t, w2 = a.shape
    w = w2 // 2
    gc = w // SG_GROUPS

    def body(a_ref, dy_ref, lng_ref, ws_ref, bt_ref, da_ref, dlng_ref, dws_ref, dbt_ref):
        @pl.when(pl.program_id(0) == 0)
        def _():
            dlng_ref[...] = jnp.zeros_like(dlng_ref)
            dws_ref[...] = jnp.zeros_like(dws_ref)
            dbt_ref[...] = jnp.zeros_like(dbt_ref)

        av, lng = a_ref[...], lng_ref[...]
        th, u, vhat, rstd, vln = _sg_forward_parts(av, lng)
        mask = _sg_gate_mask()
        bt = bt_ref[...]
        lane = lax.broadcasted_iota(jnp.int32, (BLK, BLK), 1)
        dyv = dy_ref[...]
        du_parts, dvln_parts = [], []
        dbt = jnp.zeros((BLK, BLK), F32)
        for g in range(SG_GROUPS):
            sl = slice(g * gc, (g + 1) * gc)
            wg = jnp.where(mask, ws_ref[g], 0.0).astype(BF16)
            vg = vln[:, sl].astype(BF16)
            sv = jnp.dot(wg, vg, preferred_element_type=F32)
            bg = jnp.sum(jnp.where(lane == g, bt, 0.0), axis=-1, keepdims=True)
            dyg = dyv[:, sl]
            du_parts.append(dyg * (sv + bg))
            dsv = dyg * u[:, sl]
            dbt += jnp.where(lane == g, jnp.sum(dsv, axis=-1, keepdims=True), 0.0)
            dsvb = dsv.astype(BF16)
            dws_ref[g] += jnp.where(mask, lax.dot_general(dsvb, vg, NT, preferred_element_type=F32), 0.0)
            dvln_parts.append(lax.dot_general(wg, dsvb, TN, preferred_element_type=F32))
        dbt_ref[...] += dbt
        du = jnp.concatenate(du_parts, axis=1)
        dvln = jnp.concatenate(dvln_parts, axis=1)
        dlng_ref[...] += _colsum8(dvln * vhat)
        dvhat = dvln * lng
        dv = rstd * (dvhat - jnp.mean(dvhat, axis=-1, keepdims=True)
                     - vhat * jnp.mean(dvhat * vhat, axis=-1, keepdims=True))
        dz = jnp.concatenate([du, dv], axis=1)
        dgelu = 0.5 * (1.0 + th) + (0.5 * av) * (1.0 - th * th) * (GELU_C0 * (1.0 + 3.0 * GELU_C1 * (av * av)))
        da_ref[...] = (dz * dgelu).astype(BF16)

    return _call(
        body, name=name,
        out_shape=[_sds((t, w2), BF16), _sds((8, w), F32), _sds((SG_GROUPS, BLK, BLK), F32), _sds((BLK, BLK), F32)],
        grid=(t // BLK,),
        in_specs=[pl.BlockSpec((BLK, w2), lambda i: (i, 0)), pl.BlockSpec((BLK, w), lambda i: (i, 0)),
                  pl.BlockSpec((1, w), lambda i: (0, 0)),
                  pl.BlockSpec((SG_GROUPS, BLK, BLK), lambda i: (0, 0, 0)),
                  pl.BlockSpec((BLK, BLK), lambda i: (0, 0))],
        out_specs=[pl.BlockSpec((BLK, w2), lambda i: (i, 0)), pl.BlockSpec((8, w), lambda i: (0, 0)),
                   pl.BlockSpec((SG_GROUPS, BLK, BLK), lambda i: (0, 0, 0)),
                   pl.BlockSpec((BLK, BLK), lambda i: (0, 0))],
        compiler_params=_params("arbitrary"),
    )(a, dy, lng, ws, bias_t)


def _shift_down(cat, n, tr):
    return pltpu.roll(cat, n, 0)[8:8 + tr]


def _shift_up(cat, n, tr):
    return pltpu.roll(cat, tr + 8 - n, 0)[0:tr]


def conv_fwd(p, cw, name):
    t, d3 = p.shape
    d = d3 // 3
    tr = min(256, t)
    hb = tr // 8

    def body(p_ref, ph_ref, cw_ref, o_ref):
        i = pl.program_id(0)
        pv = p_ref[...]
        y = pv[:, d:2 * d] * pv[:, 2 * d:]
        ph = ph_ref[...]
        yh = jnp.where(i > 0, ph[:, d:2 * d] * ph[:, 2 * d:], 0.0)
        cat = jnp.concatenate([yh, y], axis=0)
        yc = (cw_ref[0:1, :] * _shift_down(cat, 2, tr) + cw_ref[1:2, :] * _shift_down(cat, 1, tr)
              + cw_ref[2:3, :] * y)
        o_ref[...] = (pv[:, :d] * yc).astype(BF16)

    return _call(
        body, name=name, out_shape=_sds((t, d), BF16), grid=(t // tr,),
        in_specs=[pl.BlockSpec((tr, d3), lambda i: (i, 0)),
                  pl.BlockSpec((8, d3), lambda i: (jnp.maximum(i * hb - 1, 0), 0)),
                  pl.BlockSpec((8, d), lambda i: (0, 0))],
        out_specs=pl.BlockSpec((tr, d), lambda i: (i, 0)),
        compiler_params=_params("parallel"),
    )(p, p, cw)


def conv_bwd(p, dz, cw, name):
    t, d3 = p.shape
    d = d3 // 3
    tr = min(256, t)
    hb = tr // 8
    nt = t // tr

    def body(p_ref, ph_ref, pn_ref, dz_ref, dzn_ref, cw_ref, dp_ref, dcw_ref):
        i = pl.program_id(0)

        @pl.when(i == 0)
        def _():
            dcw_ref[...] = jnp.zeros_like(dcw_ref)

        pv = p_ref[...]
        gb, gcv, xt = pv[:, :d], pv[:, d:2 * d], pv[:, 2 * d:]
        y = gcv * xt
        ph = ph_ref[...]
        yh = jnp.where(i > 0, ph[:, d:2 * d] * ph[:, 2 * d:], 0.0)
        cat = jnp.concatenate([yh, y], axis=0)
        y2, y1 = _shift_down(cat, 2, tr), _shift_down(cat, 1, tr)
        w0, w1, w2 = cw_ref[0:1, :], cw_ref[1:2, :], cw_ref[2:3, :]
        yc = w0 * y2 + w1 * y1 + w2 * y
        dzv = dz_ref[...]
        dyc = dzv * gb
        dcw_ref[0] += _colsum8(dyc * y2)
        dcw_ref[1] += _colsum8(dyc * y1)
        dcw_ref[2] += _colsum8(dyc * y)
        dycn = jnp.where(i < nt - 1, dzn_ref[...] * pn_ref[...][:, :d], 0.0)
        catn = jnp.concatenate([dyc, dycn], axis=0)
        dy = w2 * dyc + w1 * _shift_up(catn, 1, tr) + w0 * _shift_up(catn, 2, tr)
        dp_ref[:, :d] = (dzv * yc).astype(BF16)
        dp_ref[:, d:2 * d] = (dy * xt).astype(BF16)
        dp_ref[:, 2 * d:] = (dy * gcv).astype(BF16)

    nxt = lambda i: (jnp.minimum((i + 1) * hb, t // 8 - 1), 0)
    return _call(
        body, name=name, out_shape=[_sds((t, d3), BF16), _sds((3, 8, d), F32)], grid=(nt,),
        in_specs=[pl.BlockSpec((tr, d3), lambda i: (i, 0)),
                  pl.BlockSpec((8, d3), lambda i: (jnp.maximum(i * hb - 1, 0), 0)),
                  pl.BlockSpec((8, d3), nxt),
                  pl.BlockSpec((tr, d), lambda i: (i, 0)),
                  pl.BlockSpec((8, d), nxt),
                  pl.BlockSpec((8, d), lambda i: (0, 0))],
        out_specs=[pl.BlockSpec((tr, d3), lambda i: (i, 0)), pl.BlockSpec((3, 8, d), lambda i: (0, 0, 0))],
        compiler_params=_params("arbitrary"),
    )(p, p, p, dz, dz, cw)


def ada_fwd(c_all, w, b, name):
    nl, d, n = w.shape

    def body(c_ref, w_ref, b_ref, o_ref):
        cv = c_ref[...]
        s = (cv * _sigmoid(cv)).astype(BF16)
        o_ref[...] = jnp.dot(s, w_ref[...].astype(BF16), preferred_element_type=F32) + b_ref[...]

    return _call(
        body, name=name, out_shape=_sds((nl, N_DEV, n), F32), grid=(nl,),
        in_specs=[pl.BlockSpec((N_DEV, d), lambda l: (0, 0)), pl.BlockSpec((None, d, n), lambda l: (l, 0, 0)),
                  pl.BlockSpec((None, 1, n), lambda l: (l, 0, 0))],
        out_specs=pl.BlockSpec((None, N_DEV, n), lambda l: (l, 0, 0)),
        compiler_params=_params("parallel"),
    )(c_all, w, b)


def ada_bwd(c_all, dmod, name):
    nl, _, n = dmod.shape
    d = c_all.shape[1]

    def body(c_ref, dm_ref, o_ref):
        cv = c_ref[...]
        s = (cv * _sigmoid(cv)).astype(BF16)
        o_ref[...] = lax.dot_general(s, dm_ref[...].astype(BF16), TN, preferred_element_type=F32)

    return _call(
        body, name=name, out_shape=_sds((nl, d, n), F32), grid=(nl,),
        in_specs=[pl.BlockSpec((N_DEV, d), lambda l: (0, 0)), pl.BlockSpec((None, N_DEV, n), lambda l: (l, 0, 0))],
        out_specs=pl.BlockSpec((None, d, n), lambda l: (l, 0, 0)),
        compiler_params=_params("parallel"),
    )(c_all, dmod)


def adamw(pieces, w, m, v, name, riders=None):
    nl = len(pieces)
    npc, r, c = pieces[0].shape
    tr = r
    for cand in (1024, 512, 256, 128, 64, 32, 16, 8):
        if r % cand == 0 and cand * c * 4 <= (1 << 20):
            tr = cand
            break
    nt = r // tr

    def update(p_ref, w_ref, m_ref, v_ref, g_ref, d_ref, nm_ref, nv_ref):
        g = p_ref[0].astype(F32)
        for i in range(1, npc):
            g = g + p_ref[i].astype(F32)
        wv = w_ref[...]
        nm = ADAM_B1 * m_ref[...] + (1.0 - ADAM_B1) * g
        nv = ADAM_B2 * v_ref[...] + (1.0 - ADAM_B2) * (g * g)
        m_hat = nm / (1.0 - ADAM_B1 ** ADAM_STEP)
        v_hat = nv / (1.0 - ADAM_B2 ** ADAM_STEP)
        g_ref[...] = g
        d_ref[...] = -ADAM_LR * (m_hat / (jnp.sqrt(v_hat) + ADAM_EPS) + ADAM_WD * wv)
        nm_ref[...] = nm
        nv_ref[...] = nv

    def body(*refs):
        if nl == 1:
            update(*refs)
        else:
            for j in range(nl):
                pl.when(pl.program_id(0) == j)(lambda j=j: update(refs[j], *refs[nl:]))

    row = pl.BlockSpec((tr, c), lambda l, i: (l * nt + i, 0))
    piece_specs = [pl.BlockSpec((npc, tr, c), lambda l, i, j=j: (0, jnp.where(l == j, i, 0), 0))
                   for j in range(nl)]
    res, ridden = _rider_call(
        body, (*pieces, w, m, v), riders, name=name, out_shape=[_sds((nl * r, c), F32)] * 4, grid=(nl, nt),
        in_specs=piece_specs + [row, row, row], out_specs=[row] * 4, scratch_shapes=[],
        sem=("parallel", "parallel"))
    return res, ridden


def sum_pieces(pieces, name):
    npc, r, c = pieces.shape

    def body(p_ref, o_ref):
        g = p_ref[0]
        for i in range(1, npc):
            g = g + p_ref[i]
        o_ref[...] = g

    return _call(body, name=name, out_shape=_sds((r, c), F32),
                 in_specs=[pl.BlockSpec(memory_space=pltpu.VMEM)],
                 out_specs=pl.BlockSpec(memory_space=pltpu.VMEM),
                 compiler_params=pltpu.CompilerParams(vmem_limit_bytes=VMEM_LIMIT))(pieces)


PACK_W = 1024


def _pack(arrs):
    flat = jnp.concatenate([a.reshape(-1).astype(F32) for a in arrs])
    rows = -(-flat.shape[0] // (8 * PACK_W)) * 8
    return jnp.pad(flat, (0, rows * PACK_W - flat.shape[0])).reshape(rows, PACK_W)


def _unpack(slab, shapes):
    flat = slab.reshape(-1)
    out, off = [], 0
    for s in shapes:
        n = 1
        for q in s:
            n *= q
        out.append(flat[off:off + n].reshape(s))
        off += n
    return out


def kernel(x, c, ada_w, ada_b, norm_g, ffn_w_in, ffn_w_out, sb_w_qkv, sb_w_o, sg_w_in, sg_ln_g, sg_w_s, sg_bias, sg_w_out, sc_w_in, sc_conv_w, sc_w_out, cb_w_qkv, cb_rel_bias, cb_w_o, loss_target, m_ada_w, m_ada_b, m_norm_g, m_ffn_w_in, m_ffn_w_out, m_sb_w_qkv, m_sb_w_o, m_sg_w_in, m_sg_ln_g, m_sg_w_s, m_sg_bias, m_sg_w_out, m_sc_w_in, m_sc_conv_w, m_sc_w_out, m_cb_w_qkv, m_cb_rel_bias, m_cb_w_o, v_ada_w, v_ada_b, v_norm_g, v_ffn_w_in, v_ffn_w_out, v_sb_w_qkv, v_sb_w_o, v_sg_w_in, v_sg_ln_g, v_sg_w_s, v_sg_bias, v_sg_w_out, v_sc_w_in, v_sc_conv_w, v_sc_w_out, v_cb_w_qkv, v_cb_rel_bias, v_cb_w_o):
    depth = ada_w.shape[0]
    d = D_MODEL
    xi, yi, ci = lax.axis_index("x"), lax.axis_index("y"), lax.axis_index("c")
    me = 4 * xi + 2 * yi + ci
    x0 = x[0]
    t = x0.shape[0]
    target = loss_target[0]

    c_g, ng, small, w_qkv0 = _all_gather(
        [jnp.pad(c, ((0, 7), (0, 0))), norm_g.reshape(depth * 4, d // N_DEV), _pack([sg_ln_g, sc_conv_w]),
         sb_w_qkv[0].astype(BF16)], "gather_setup")

    c_all = c_g[:, 0, :]
    na = ada_w.shape[2]
    b_cols = lax.dynamic_slice_in_dim(ada_b, me * na, na, axis=1)[:, None, :]
    mod_part = ada_fwd(c_all, ada_w, b_cols, "ada_fwd")
    mod_g = _all_gather([mod_part.reshape(depth * N_DEV, na)], "gather_mod")[0]
    mod_g = mod_g.reshape(N_DEV, depth, N_DEV, na)
    mod_me = lax.dynamic_index_in_dim(mod_g, me, axis=2, keepdims=False)
    mod = jnp.transpose(mod_me, (1, 0, 2)).reshape(depth, 6, 1, d)

    norm_full = jnp.transpose(ng, (1, 0, 2)).reshape(depth, 4, 1, d)
    small = small.reshape(N_DEV, -1)
    nl_g = sg_ln_g.shape[1]
    ln_full = small[:, :nl_g].reshape(1, N_DEV * nl_g)
    cwn = sc_conv_w.shape[2]
    cw_sh = small[:, nl_g:nl_g + 3 * cwn].reshape(N_DEV, 3, cwn)
    cw_full = jnp.transpose(cw_sh, (1, 0, 2)).reshape(3, d)
    cw_pad = jnp.pad(cw_full, ((0, 5), (0, 0)))

    bf = lambda a: a.astype(BF16)
    mixers = [
        [bf(sb_w_qkv[0]), bf(sb_w_o[0])],
        [bf(sg_w_in[0]), bf(sg_w_out[0])],
        [bf(sc_w_in[0]), bf(sc_w_out[0])],
        [bf(cb_w_qkv[0]), bf(cb_w_o[0])],
    ]
    shards = [[bf(ffn_w_in[i]), bf(ffn_w_out[i])] + mixers[i % 4] for i in range(depth)]
    gathered = [[None] * 4 for _ in range(depth)]
    gathered[0][2] = w_qkv0
    gather_plan = {
        ("sb_fwd", 0): [(0, 0), (0, 1), (0, 3), (1, 2), (2, 0)],
        ("ffn_in", 0): [(1, 0)], ("ffn_out", 0): [(1, 1), (1, 3)],
        ("mix_in", 1): [(3, 0)], ("ffn_in", 1): [(2, 1), (2, 3)], ("ffn_out", 1): [(2, 2)],
        ("ffn_in", 2): [(3, 1), (3, 3)], ("ffn_out", 2): [(3, 2)],
    }

    def gather_riders(host, layer):
        slots = gather_plan.get((host, layer), [])
        return (_Gather, [shards[l][s] for l, s in slots]) if slots else None

    def place_gathered(host, layer, ridden):
        for (l, s), arr in zip(gather_plan.get((host, layer), []), ridden):
            gathered[l][s] = arr

    bias_win = band_bias_window(cb_rel_bias[0])
    ws = sg_w_s[0]
    bias_t = jnp.pad(sg_bias[0].T, ((0, 0), (0, BLK - SG_GROUPS)))

    saved = []
    xcur = x0
    h = pre_fwd(x0, norm_full[0, 0], mod[0, 0], mod[0, 1], "L0_pre_m")
    for i in range(depth):
        mi = i % 4
        sh_m, sc_m, gt_m, sh_f, sc_f, gt_f = [mod[i, j] for j in range(6)]
        g0, g1, g2, g3 = [norm_full[i, j] for j in range(4)]
        tag = "L%d_" % i
        sv = {"x_in": xcur, "h_m": h}
        if mi == 0:
            qkv = mm_cs(h, gathered[0][2], tag + "qkv", out_dtype=BF16)
            qkv_t = jnp.transpose(qkv.reshape(t // SB_GW, SB_GW, 3 * d), (0, 2, 1))
            o, cmass, ridden = sb_fwd(qkv, qkv_t, gather_riders("sb_fwd", i)[1], tag + "sb_fwd")
            place_gathered("sb_fwd", i, ridden)
            sv.update(qkv=qkv, qkv_t=qkv_t, o=o, cmass=cmass)
            mixed = o
        else:
            w_in = gathered[i][2]
            out_dtype = BF16 if mi == 3 else F32
            if gather_riders("mix_in", i):
                pre, ridden = mm_cs(h, w_in, tag + "mix_in", out_dtype=out_dtype, riders=gather_riders("mix_in", i))
                place_gathered("mix_in", i, ridden)
            else:
                pre = mm_cs(h, w_in, tag + "mix_in", out_dtype=out_dtype)
            if mi == 1:
                mixed = sg_fwd(pre, ln_full, ws, bias_t, tag + "sg_fwd")
                sv.update(a=pre, yy=mixed)
            elif mi == 2:
                mixed = conv_fwd(pre, cw_pad, tag + "conv_fwd")
                sv.update(p=pre, gz=mixed)
            else:
                mixed = band_fwd(pre, bias_win, tag + "band_fwd")
                sv.update(qkv=pre, o=mixed)
        wfi, wfo, _, wmo = gathered[i]
        wfo4 = wfo.reshape(4, -1, d)
        y = mm(mixed, wmo.reshape(-1, d), tag + "mix_out")
        sv["y_m"] = y
        xmid, h2 = post_pre_fwd(xcur, y, g1, gt_m, g2, sh_f, sc_f, tag + "post_m_pre_f")
        sv["x_mid"] = xmid
        ag, au, s3, ridden = ffn_in_swiglu(h2, wfi, tag + "ffn_in", riders=gather_riders("ffn_in", i))
        place_gathered("ffn_in", i, ridden)
        if gather_riders("ffn_out", i):
            y2, ridden = mm_rs(s3, wfo4, tag + "ffn_out", riders=gather_riders("ffn_out", i))
            place_gathered("ffn_out", i, ridden)
        else:
            y2 = mm_rs(s3, wfo4, tag + "ffn_out")
        sv.update(h_f=h2, ag=ag, au=au, s3=s3, y_f=y2)
        saved.append(sv)
        if i + 1 < depth:
            xcur, h = post_pre_fwd(xmid, y2, g3, gt_f, norm_full[i + 1, 0], mod[i + 1, 0], mod[i + 1, 1],
                                   tag + "post_f_pre_m")

    last = depth - 1
    dx, dy2, lpart, dgt_f, dg3 = post_loss_bwd(saved[last]["x_mid"], saved[last]["y_f"], norm_full[last, 3],
                                               mod[last, 5], target, "loss")
    loss = lax.psum(0.5 * jnp.sum(lpart) / d, ("x", "y", "c"))

    dmod_rows = [None] * depth
    dnorm_rows = [None] * depth
    big_pieces = [[None] * 4 for _ in range(depth)]
    pending_dwmi, pending_dwfi, pending_layers = None, [], []
    small_grads = {}
    for i in reversed(range(depth)):
        wfi, wfo, wmi, wmo = gathered[i]
        wfo4 = wfo.reshape(4, -1, d)
        wmo2 = wmo.reshape(-1, d)
        wmi_t = jnp.transpose(wmi, (0, 2, 1)).reshape(-1, d)
        mi = i % 4
        sh_m, sc_m, gt_m, sh_f, sc_f, gt_f = [mod[i, j] for j in range(6)]
        g0, g1, g2, g3 = [norm_full[i, j] for j in range(4)]
        tag = "L%d_b_" % i
        sv = saved[i]
        da3, _ = ffn_out_dx_swiglu(dy2, wfo4, sv["ag"], sv["au"], tag + "ffn_out_dx")
        dwfo = mm_rs_dw(sv["s3"], dy2, tag + "ffn_out_dw", out_dtype=BF16).reshape(N_DEV, -1, d)
        dh2, (big_pieces[i][1],) = mm_cs_dx(da3, wfi, tag + "ffn_in_dx", act_major=True, riders=(_Exchange, [dwfo]))
        if pending_dwmi is not None:
            dwfi, (big_pieces[i + 1][2],) = mm_cs_dw(sv["h_f"], da3, tag + "ffn_in_dw", act_major=True,
                                                     out_dtype=BF16, riders=(_Exchange, [pending_dwmi]))
        else:
            dwfi = mm_cs_dw(sv["h_f"], da3, tag + "ffn_in_dw", act_major=True, out_dtype=BF16)
        dx, dy, dsh_f, dsc_f, dg2, dgt_m, dg1 = pre_post_bwd(
            dh2, sv["x_mid"], g2, sc_f, dx, sv["y_m"], g1, gt_m, tag + "pre_f_post_m")
        if mi == 0:
            do = mm_nt(dy, wmo2, tag + "wo_dx", out_dtype=BF16)
            dwmo = mm_tn(sv["o"], dy, tag + "wo_dw", out_dtype=BF16).reshape(N_DEV, -1, d)
            dq, dk, dv, ridden = sb_bwd(sv["qkv"], sv["qkv_t"], do, sv["cmass"],
                                        pending_dwfi + [dwfi, dwmo], tag + "sb_bwd")
            for n, j in enumerate(pending_layers):
                big_pieces[j][0] = ridden[n]
            big_pieces[0][0], big_pieces[0][3] = ridden[-2:]
            dmid = jnp.concatenate([dq, dk, dv], axis=1)
        elif mi == 1:
            dyy = mm_nt(dy, wmo2, tag + "sg_out_dx")
            dwmo = mm_tn(sv["yy"], dy, tag + "sg_out_dw", out_dtype=BF16).reshape(N_DEV, -1, d)
            dmid, dlng, dws, dbt = sg_bwd(sv["a"], dyy, ln_full, ws, bias_t, tag + "sg_bwd")
            small_grads.update(ln_g=jnp.sum(dlng, axis=0), w_s=dws, bias=dbt[:, :SG_GROUPS].T)
        elif mi == 2:
            dgz = mm_nt(dy, wmo2, tag + "sc_out_dx")
            dwmo = mm_tn(sv["gz"], dy, tag + "sc_out_dw", out_dtype=BF16).reshape(N_DEV, -1, d)
            dmid, dcw = conv_bwd(sv["p"], dgz, cw_pad, tag + "conv_bwd")
            small_grads.update(conv_w=jnp.sum(dcw, axis=1))
        else:
            do = mm_nt(dy, wmo2, tag + "wo_dx", out_dtype=BF16)
            dwmo = mm_tn(sv["o"], dy, tag + "wo_dw", out_dtype=BF16).reshape(N_DEV, -1, d)
            dq, dk, dv, dwin, (big_pieces[i][0], big_pieces[i][3]) = band_bwd(
                sv["qkv"], do, bias_win, tag + "band_bwd", riders=(_Exchange, [dwfi, dwmo]))
            dmid = jnp.concatenate([dq, dk, dv], axis=1)
            small_grads.update(rel_bias=band_bias_window_grad(dwin))
        pending_dwmi = mm_cs_dw(sv["h_m"], dmid, tag + "mix_in_dw", out_dtype=BF16)
        if mi in (1, 2):
            dh, (big_pieces[i][3],) = mm(dmid, wmi_t, tag + "mix_in_dx", riders=(_Exchange, [dwmo]))
            pending_dwfi.append(dwfi)
            pending_layers.append(i)
        elif i == 0:
            dh, (big_pieces[0][2],) = mm(dmid, wmi_t, tag + "mix_in_dx", riders=(_Exchange, [pending_dwmi]))
        else:
            dh = mm(dmid, wmi_t, tag + "mix_in_dx")
        if i > 0:
            dx, dy2_prev, dsh_m, dsc_m, dg0, dgt_f_prev, dg3_prev = pre_post_bwd(
                dh, sv["x_in"], g0, sc_m, dx, saved[i - 1]["y_f"], norm_full[i - 1, 3], mod[i - 1, 5],
                tag + "pre_m_post_f")
        else:
            dx, dsh_m, dsc_m, dg0 = pre_bwd(dh, sv["x_in"], g0, sc_m, dx, tag + "pre_m")
        dmod_rows[i] = jnp.stack([jnp.sum(q, axis=0) for q in (dsh_m, dsc_m, dgt_m, dsh_f, dsc_f, dgt_f)])
        dnorm_rows[i] = jnp.stack([jnp.sum(q, axis=0) for q in (dg0, dg1, dg2, dg3)])
        if i > 0:
            dy2, dgt_f, dg3 = dy2_prev, dgt_f_prev, dg3_prev
    grad_x = dx[None]

    out_g, out_d, out_m, out_v = {}, {}, {}, {}

    def upd(name, pieces, w, m, v, riders=None):
        rows_cols = (len(pieces) * pieces[0].shape[1], pieces[0].shape[2])
        res, ridden = adamw(pieces, w.reshape(rows_cols), m.reshape(rows_cols), v.reshape(rows_cols),
                            "adamw_" + name, riders)
        out_g[name], out_d[name], out_m[name], out_v[name] = [q.reshape(w.shape) for q in res]
        return ridden

    dmod_mine = jnp.stack(dmod_rows).reshape(depth, 6 * d)
    dnorm_mine = jnp.stack(dnorm_rows)
    small_list = [dnorm_mine, small_grads["ln_g"], small_grads["bias"], small_grads["conv_w"],
                  small_grads["rel_bias"]]
    small_shapes = [dmod_mine.shape] + [a.shape for a in small_list]
    slab = _pack([dmod_mine] + small_list)
    upd("ffn_w_in", [big_pieces[i][0] for i in range(depth)], ffn_w_in, m_ffn_w_in, v_ffn_w_in)
    slab_g, ws_g = upd("ffn_w_out", [big_pieces[i][1] for i in range(depth)], ffn_w_out, m_ffn_w_out, v_ffn_w_out,
                       riders=(_Gather, [slab, small_grads["w_s"].reshape(-1, BLK)]))
    upd("sg_w_s", [ws_g], sg_w_s, m_sg_w_s, v_sg_w_s)
    upd("sb_w_qkv", [big_pieces[0][2]], sb_w_qkv, m_sb_w_qkv, v_sb_w_qkv)
    upd("sb_w_o", [big_pieces[0][3]], sb_w_o, m_sb_w_o, v_sb_w_o)
    upd("sg_w_in", [big_pieces[1][2]], sg_w_in, m_sg_w_in, v_sg_w_in)
    upd("sg_w_out", [big_pieces[1][3]], sg_w_out, m_sg_w_out, v_sg_w_out)
    upd("sc_w_in", [big_pieces[2][2]], sc_w_in, m_sc_w_in, v_sc_w_in)
    upd("sc_w_out", [big_pieces[2][3]], sc_w_out, m_sc_w_out, v_sc_w_out)
    upd("cb_w_qkv", [big_pieces[3][2]], cb_w_qkv, m_cb_w_qkv, v_cb_w_qkv)
    upd("cb_w_o", [big_pieces[3][3]], cb_w_o, m_cb_w_o, v_cb_w_o)

    tot = sum_pieces(slab_g, "sum_small_grads")
    g_ada_b_full, g_norm, g_ln, g_sbias, g_cw, g_rb = _unpack(tot, small_shapes)
    dmod_all = slab_g.reshape(N_DEV, -1)[:, :depth * 6 * d].reshape(N_DEV, depth, 6 * d)
    dmod_cols = lax.dynamic_slice_in_dim(dmod_all, me * na, na, axis=2)
    g_ada_w = ada_bwd(c_all, jnp.transpose(dmod_cols, (1, 0, 2)), "ada_bwd")

    nsh = d // N_DEV
    g_norm_sh = lax.dynamic_slice_in_dim(g_norm, me * nsh, nsh, axis=2)
    g_ln_sh = lax.dynamic_slice_in_dim(g_ln.reshape(1, -1), me * nl_g, nl_g, axis=1)
    g_cw_sh = lax.dynamic_slice_in_dim(g_cw, me * cwn, cwn, axis=1)[None]

    upd("ada_w", [g_ada_w.reshape(1, depth * d, na)], ada_w, m_ada_w, v_ada_w)

    small_names = ["ada_b", "norm_g", "sg_ln_g", "sg_bias", "sc_conv_w", "cb_rel_bias"]
    small_g = [g_ada_b_full, g_norm_sh, g_ln_sh, g_sbias[None], g_cw_sh, g_rb[None]]
    small_w = [ada_b, norm_g, sg_ln_g, sg_bias, sc_conv_w, cb_rel_bias]
    small_m = [m_ada_b, m_norm_g, m_sg_ln_g, m_sg_bias, m_sc_conv_w, m_cb_rel_bias]
    small_v = [v_ada_b, v_norm_g, v_sg_ln_g, v_sg_bias, v_sc_conv_w, v_cb_rel_bias]
    shapes = [w.shape for w in small_w]
    res, _ = adamw([_pack(small_g)[None]], _pack(small_w), _pack(small_m), _pack(small_v), "adamw_small")
    for nm_, gs, ds_, ms, vs in zip(small_names, *[_unpack(r, shapes) for r in res]):
        out_g[nm_], out_d[nm_], out_m[nm_], out_v[nm_] = gs, ds_, ms, vs

    order = ["ada_w", "ada_b", "norm_g", "ffn_w_in", "ffn_w_out", "sb_w_qkv", "sb_w_o", "sg_w_in", "sg_ln_g",
             "sg_w_s", "sg_bias", "sg_w_out", "sc_w_in", "sc_conv_w", "sc_w_out", "cb_w_qkv", "cb_rel_bias", "cb_w_o"]
    return (loss, grad_x, *[out_g[n] for n in order], *[out_d[n] for n in order],
            *[out_m[n] for n in order], *[out_v[n] for n in order])
```

```python
import jax
import jax.numpy as jnp
from jax import lax
from jax.experimental import pallas as pl
from jax.experimental.pallas import tpu as pltpu

F32 = jnp.float32
BF16 = jnp.bfloat16
MESH = pl.DeviceIdType.MESH

N_DEV = 8
D_MODEL = 1024
N_HEADS = 16
HEAD_DIM = 64
QK_SCALE = HEAD_DIM ** -0.5
BLK = 128
BAND_BLOCKS = 5
BAND_W = BAND_BLOCKS * BLK
REL_CLIP = 128
EPS = 1e-6
NEG = -1e30
GELU_C0 = 0.7978845608028654
GELU_C1 = 0.044715
ADAM_LR = 0.001
ADAM_B1 = 0.9
ADAM_B2 = 0.999
ADAM_EPS = 1e-08
ADAM_WD = 0.01
ADAM_STEP = 10
VMEM_LIMIT = 56 * 1024 * 1024


def _call(body, **kw):
    return pl.pallas_call(body, **kw)


def _params(*sem):
    return pltpu.CompilerParams(dimension_semantics=sem, vmem_limit_bytes=VMEM_LIMIT)


def _sds(shape, dtype):
    return jax.ShapeDtypeStruct(tuple(shape), dtype)


def _row_tile(t):
    return min(512, t)


def _me():
    x, y, c = lax.axis_index("x"), lax.axis_index("y"), lax.axis_index("c")
    return x, y, c


def _all_gather(arrs, name):
    n = len(arrs)

    def body(*refs):
        gather = _Gather(refs[:n], refs[n:2 * n], *refs[2 * n:])
        gather.start()
        gather.forward()
        gather.finish()

    any_spec = pl.BlockSpec(memory_space=pl.ANY)
    outs = _call(
        body,
        name=name,
        out_shape=_Gather.out_shapes(arrs),
        in_specs=[any_spec] * n,
        out_specs=[any_spec] * n,
        scratch_shapes=_comm_sems(n),
    )(*arrs)
    return list(outs)


def _comm_sems(n):
    if n == 0:
        return []
    return [pltpu.SemaphoreType.DMA((n, 7)), pltpu.SemaphoreType.DMA((n, 7)), pltpu.SemaphoreType.DMA((n,))]


class _Gather:
    def __init__(self, x_refs, o_refs, send_sems, recv_sems, local_sems):
        self.x_refs, self.o_refs = x_refs, o_refs
        self.send_sems, self.recv_sems, self.local_sems = send_sems, recv_sems, local_sems
        x, y, c = _me()
        self.c = c
        self.me, self.sibling = (x, y, c), (x, y, 1 - c)
        self.chips = [(1 - x, y), (x, 1 - y), (1 - x, 1 - y)]

    @staticmethod
    def out_shapes(arrs):
        return [_sds((N_DEV,) + a.shape, a.dtype) for a in arrs]

    def rows(self, a, block):
        px, py, pc = block
        return self.o_refs[a].at[4 * px + 2 * py + pc]

    def copy(self, a, k, block, to, own=False):
        return pltpu.make_async_remote_copy(
            src_ref=self.x_refs[a] if own else self.rows(a, block),
            dst_ref=self.rows(a, block),
            send_sem=self.send_sems.at[a, k],
            recv_sem=self.recv_sems.at[a, k],
            device_id=to,
            device_id_type=MESH,
        )

    def local(self, a):
        return pltpu.make_async_copy(self.x_refs[a], self.rows(a, self.me), self.local_sems.at[a])

    def first(self, a):
        cps = [self.copy(a, 0, self.me, self.sibling, own=True)]
        return cps + [self.copy(a, 1 + j, self.me, (*chip, self.c), own=True) for j, chip in enumerate(self.chips)]

    def passed(self, a):
        return [self.copy(a, 4 + j, (*chip, self.c), self.sibling) for j, chip in enumerate(self.chips)]

    def start(self):
        for a in range(len(self.x_refs)):
            self.local(a).start()
            for cp in self.first(a):
                cp.start()

    def forward(self):
        for a in range(len(self.x_refs)):
            passed = self.passed(a)
            for j, chip in enumerate(self.chips):
                self.copy(a, 1 + j, (*chip, self.c), self.me).wait_recv()
                passed[j].start()

    def finish(self):
        for a in range(len(self.x_refs)):
            self.copy(a, 0, self.sibling, self.me).wait_recv()
            for j, chip in enumerate(self.chips):
                self.copy(a, 4 + j, (*chip, 1 - self.c), self.me).wait_recv()
            for cp in self.first(a) + self.passed(a):
                cp.wait_send()
            self.local(a).wait()


class _Exchange:
    def __init__(self, x_refs, o_refs, send_sems, recv_sems, local_sems):
        self.x_refs, self.o_refs = x_refs, o_refs
        self.send_sems, self.recv_sems, self.local_sems = send_sems, recv_sems, local_sems
        x, y, c = _me()
        self.me = 4 * x + 2 * y + c
        self.peers = []
        for k in range(1, N_DEV):
            px = 1 - x if k & 4 else x
            py = 1 - y if k & 2 else y
            pc = 1 - c if k & 1 else c
            self.peers.append((px, py, pc))

    @staticmethod
    def out_shapes(arrs):
        return [_sds(a.shape, a.dtype) for a in arrs]

    def local(self, a):
        return pltpu.make_async_copy(self.x_refs[a].at[self.me], self.o_refs[a].at[self.me], self.local_sems.at[a])

    def copy(self, a, k, send):
        px, py, pc = self.peers[k]
        peer = 4 * px + 2 * py + pc
        return pltpu.make_async_remote_copy(
            src_ref=self.x_refs[a].at[peer],
            dst_ref=self.o_refs[a].at[self.me if send else peer],
            send_sem=self.send_sems.at[a, k], recv_sem=self.recv_sems.at[a, k],
            device_id=(px, py, pc), device_id_type=MESH)

    def start(self):
        for a in range(len(self.x_refs)):
            self.local(a).start()
            for k in range(N_DEV - 1):
                self.copy(a, k, True).start()

    def finish(self):
        for a in range(len(self.x_refs)):
            for k in range(N_DEV - 1):
                self.copy(a, k, False).wait_recv()
            for k in range(N_DEV - 1):
                self.copy(a, k, True).wait_send()
            self.local(a).wait()


NN = (((1,), (0,)), ((), ()))
NT = (((1,), (1,)), ((), ()))
TN = (((0,), (0,)), ((), ()))


def _all_of(conds):
    out = conds[0]
    for cond in conds[1:]:
        out = out & cond
    return out


def _rider_call(body, operands, riders, *, name, out_shape, grid, in_specs, out_specs, scratch_shapes, sem):
    if not riders or not riders[1]:
        res = _call(body, name=name, out_shape=out_shape, grid=grid, in_specs=in_specs, out_specs=out_specs,
                    scratch_shapes=list(scratch_shapes), compiler_params=_params(*sem))(*operands)
        return list(res), []
    cls, arrs = riders
    nr, ni, no, ns = len(arrs), len(in_specs), len(out_specs), len(scratch_shapes)
    forward_at = (3 * grid[0]) // 4 if grid[0] >= 4 else None

    def wrapped(*refs):
        ids = [pl.program_id(ax) for ax in range(len(grid))]
        comm = cls(refs[ni:ni + nr], refs[ni + nr + no:ni + 2 * nr + no], *refs[ni + 2 * nr + no + ns:])
        pl.when(_all_of([i == 0 for i in ids]))(comm.start)
        if cls is _Gather and forward_at is not None:
            pl.when(_all_of([ids[0] == forward_at] + [i == 0 for i in ids[1:]]))(comm.forward)
        body(*refs[:ni], *refs[ni + nr:ni + nr + no], *refs[ni + 2 * nr + no:ni + 2 * nr + no + ns])

        def end():
            if cls is _Gather and forward_at is None:
                comm.forward()
            comm.finish()

        pl.when(_all_of([i == g - 1 for i, g in zip(ids, grid)]))(end)

    any_spec = pl.BlockSpec(memory_space=pl.ANY)
    res = _call(wrapped, name=name, out_shape=list(out_shape) + cls.out_shapes(arrs), grid=grid,
                in_specs=list(in_specs) + [any_spec] * nr, out_specs=list(out_specs) + [any_spec] * nr,
                scratch_shapes=list(scratch_shapes) + _comm_sems(nr),
                compiler_params=_params(*(["arbitrary"] * len(grid))))(*operands, *arrs)
    return list(res[:no]), list(res[no:])


def _gemm(a, b, out_shape, out_dtype, grid, a_spec, b_spec, o_spec, acc_shape, dims, name, riders=None):
    nk = grid[2]

    if nk == 1:
        def body(a_ref, b_ref, o_ref):
            r = lax.dot_general(a_ref[...].astype(BF16), b_ref[...].astype(BF16), dims,
                                preferred_element_type=F32)
            o_ref[...] = r.astype(o_ref.dtype)
        scratch = []
    else:
        def body(a_ref, b_ref, o_ref, acc_ref):
            k = pl.program_id(2)

            @pl.when(k == 0)
            def _():
                acc_ref[...] = jnp.zeros_like(acc_ref)

            acc_ref[...] += lax.dot_general(a_ref[...].astype(BF16), b_ref[...].astype(BF16), dims,
                                            preferred_element_type=F32)

            @pl.when(k == nk - 1)
            def _():
                o_ref[...] = acc_ref[...].astype(o_ref.dtype)
        scratch = [pltpu.VMEM(acc_shape, F32)]

    res, ridden = _rider_call(
        body, (a, b), riders, name=name, out_shape=[_sds(out_shape, out_dtype)], grid=grid,
        in_specs=[a_spec, b_spec], out_specs=[o_spec], scratch_shapes=scratch,
        sem=("parallel", "parallel", "arbitrary"))
    return (res[0], ridden) if riders else res[0]


def _div_tile(n, want):
    if n <= want:
        return n
    t = want - want % 128
    while n % t:
        t -= 128
    return t


def mm(a, b, name, out_dtype=F32, tm=512, tn=1024, tk=1024, riders=None):
    m, k = a.shape
    n = b.shape[1]
    tm, tn, tk = _div_tile(m, tm), _div_tile(n, tn), _div_tile(k, tk)
    return _gemm(a, b, (m, n), out_dtype, (m // tm, n // tn, k // tk),
                 pl.BlockSpec((tm, tk), lambda i, j, kk: (i, kk)),
                 pl.BlockSpec((tk, tn), lambda i, j, kk: (kk, j)),
                 pl.BlockSpec((tm, tn), lambda i, j, kk: (i, j)),
                 (tm, tn), NN, name, riders)


def mm_nt(a, b, name, out_dtype=F32, tm=512, tn=1024, tk=1024):
    m, n = a.shape
    k = b.shape[0]
    tm, tk_out, tred = _div_tile(m, tm), _div_tile(k, tn), _div_tile(n, tk)
    return _gemm(a, b, (m, k), out_dtype, (m // tm, k // tk_out, n // tred),
                 pl.BlockSpec((tm, tred), lambda i, j, kk: (i, kk)),
                 pl.BlockSpec((tk_out, tred), lambda i, j, kk: (j, kk)),
                 pl.BlockSpec((tm, tk_out), lambda i, j, kk: (i, j)),
                 (tm, tk_out), NT, name)


def mm_tn(a, b, name, out_dtype=F32, tm=512, tn=1024, tk=1024):
    m, k = a.shape
    n = b.shape[1]
    tk_out, tn, tred = _div_tile(k, tk), _div_tile(n, tn), _div_tile(m, tm)
    return _gemm(a, b, (k, n), out_dtype, (k // tk_out, n // tn, m // tred),
                 pl.BlockSpec((tred, tk_out), lambda i, j, kk: (kk, i)),
                 pl.BlockSpec((tred, tn), lambda i, j, kk: (kk, j)),
                 pl.BlockSpec((tk_out, tn), lambda i, j, kk: (i, j)),
                 (tk_out, tn), TN, name)


def mm_cs(a, wg, name, act_major=False, out_dtype=F32, tm=1024, riders=None):
    m, k = a.shape
    s, _, n = wg.shape
    tm = _div_tile(m, tm)
    if act_major:
        out_shape, o_spec = (s, m, n), pl.BlockSpec((None, tm, n), lambda i, j, kk: (j, i, 0))
    else:
        out_shape, o_spec = (m, s * n), pl.BlockSpec((tm, n), lambda i, j, kk: (i, j))
    return _gemm(a, wg, out_shape, out_dtype, (m // tm, s, 1),
                 pl.BlockSpec((tm, k), lambda i, j, kk: (i, 0)),
                 pl.BlockSpec((None, k, n), lambda i, j, kk: (j, 0, 0)),
                 o_spec, (tm, n), NN, name, riders)


def mm_cs_dx(da, wg, name, act_major=False, out_dtype=F32, tm=1024, riders=None):
    s, k, n = wg.shape
    m = da.shape[1] if act_major else da.shape[0]
    tm = _div_tile(m, tm)
    if act_major and s % 2 == 0:
        steps = s // 2

        def body(a_ref, w_ref, o_ref, acc_ref):
            kk = pl.program_id(2)

            @pl.when(kk == 0)
            def _():
                acc_ref[...] = jnp.zeros_like(acc_ref)

            acc_ref[...] += (lax.dot_general(a_ref[0], w_ref[0], NT, preferred_element_type=F32)
                             + lax.dot_general(a_ref[1], w_ref[1], NT, preferred_element_type=F32))

            @pl.when(kk == steps - 1)
            def _():
                o_ref[...] = acc_ref[...].astype(o_ref.dtype)

        res, ridden = _rider_call(
            body, (da, wg), riders, name=name, out_shape=[_sds((m, k), out_dtype)], grid=(m // tm, 1, steps),
            in_specs=[pl.BlockSpec((2, tm, n), lambda i, j, kk: (kk, i, 0)),
                      pl.BlockSpec((2, k, n), lambda i, j, kk: (kk, 0, 0))],
            out_specs=[pl.BlockSpec((tm, k), lambda i, j, kk: (i, 0))],
            scratch_shapes=[pltpu.VMEM((tm, k), F32)], sem=("parallel", "parallel", "arbitrary"))
        return (res[0], ridden) if riders else res[0]
    if act_major:
        a_spec = pl.BlockSpec((None, tm, n), lambda i, j, kk: (kk, i, 0))
    else:
        a_spec = pl.BlockSpec((tm, n), lambda i, j, kk: (i, kk))
    return _gemm(da, wg, (m, k), out_dtype, (m // tm, 1, s), a_spec,
                 pl.BlockSpec((None, k, n), lambda i, j, kk: (kk, 0, 0)),
                 pl.BlockSpec((tm, k), lambda i, j, kk: (i, 0)),
                 (tm, k), NT, name, riders)


def mm_cs_dw(a, da, name, act_major=False, out_dtype=F32, tm=1024, riders=None):
    m, k = a.shape
    if act_major:
        s, _, n = da.shape
    else:
        s, n = N_DEV, da.shape[1] // N_DEV
    tm = _div_tile(m, tm)
    if act_major:
        b_spec = pl.BlockSpec((None, tm, n), lambda i, j, kk: (i, kk, 0))
    else:
        b_spec = pl.BlockSpec((tm, n), lambda i, j, kk: (kk, i))
    return _gemm(a, da, (s, k, n), out_dtype, (s, 1, m // tm),
                 pl.BlockSpec((tm, k), lambda i, j, kk: (kk, 0)), b_spec,
                 pl.BlockSpec((None, k, n), lambda i, j, kk: (i, 0, 0)),
                 (k, n), TN, name, riders)


def mm_rs(s3, w3, name, out_dtype=F32, tm=1024, riders=None):
    s, m, n = s3.shape
    nn = w3.shape[2]
    tm = _div_tile(m, tm)
    return _gemm(s3, w3, (m, nn), out_dtype, (m // tm, 1, s),
                 pl.BlockSpec((None, tm, n), lambda i, j, kk: (kk, i, 0)),
                 pl.BlockSpec((None, n, nn), lambda i, j, kk: (kk, 0, 0)),
                 pl.BlockSpec((tm, nn), lambda i, j, kk: (i, 0)),
                 (tm, nn), NN, name, riders)


def mm_rs_dw(s3, dy, name, out_dtype=F32, tm=1024):
    s, m, n = s3.shape
    nn = dy.shape[1]
    tm = _div_tile(m, tm)
    return _gemm(s3, dy, (s, n, nn), out_dtype, (s, 1, m // tm),
                 pl.BlockSpec((None, tm, n), lambda i, j, kk: (i, kk, 0)),
                 pl.BlockSpec((tm, nn), lambda i, j, kk: (kk, 0)),
                 pl.BlockSpec((None, n, nn), lambda i, j, kk: (i, 0, 0)),
                 (n, nn), TN, name)


def _colsum8(v):
    tr, d = v.shape
    return v.reshape(tr // 8, 8, d).sum(axis=0)


def _rstd(v):
    return lax.rsqrt(jnp.mean(v * v, axis=-1, keepdims=True) + EPS)


def _vec_spec(d):
    return pl.BlockSpec((1, d), lambda i: (0, 0))


def _acc_spec(d):
    return pl.BlockSpec((8, d), lambda i: (0, 0))


def _pre_rows(xv, g, shift, scale):
    return ((xv * _rstd(xv)) * g) * (1 + scale) + shift


def _post_rows(xv, yv, g, gate):
    return xv + gate * ((yv * _rstd(yv)) * g)


def _post_bwd_rows(dxv, yv, g, gate):
    r = _rstd(yv)
    yhat = yv * r
    dgate = _colsum8(dxv * (yhat * g))
    dyn = gate * dxv
    dg = _colsum8(dyn * yhat)
    dyhat = dyn * g
    dy = r * (dyhat - yhat * jnp.mean(dyhat * yhat, axis=-1, keepdims=True))
    return dy, dgate, dg


def _pre_bwd_rows(dhv, xv, g, scale, dxn):
    r = _rstd(xv)
    xhat = xv * r
    dshift = _colsum8(dhv)
    dscale = _colsum8(dhv * (xhat * g))
    dmod = dhv * (1 + scale)
    dg = _colsum8(dmod * xhat)
    dxhat = dmod * g
    dx = r * (dxhat - xhat * jnp.mean(dxhat * xhat, axis=-1, keepdims=True)) + dxn
    return dx, dshift, dscale, dg


def _row_call(body, name, t, d, rows_in, vecs_in, rows_out, n_acc):
    tr = _row_tile(t)
    nri, nvi, nro = len(rows_in), len(vecs_in), len(rows_out)

    def wrapped(*refs):
        accs = refs[nri + nvi + nro:]
        if n_acc:
            @pl.when(pl.program_id(0) == 0)
            def _():
                for acc in accs:
                    acc[...] = jnp.zeros_like(acc)
        body(*refs)

    row = pl.BlockSpec((tr, d), lambda i: (i, 0))
    return _call(wrapped, name=name,
                 out_shape=[_sds((t, d), dt) for dt in rows_out] + [_sds((8, d), F32)] * n_acc,
                 grid=(t // tr,), in_specs=[row] * nri + [_vec_spec(d)] * nvi,
                 out_specs=[row] * nro + [_acc_spec(d)] * n_acc,
                 compiler_params=_params("arbitrary" if n_acc else "parallel"))(*rows_in, *vecs_in)


def pre_fwd(x, g, shift, scale, name):
    def body(x_ref, g_ref, sh_ref, sc_ref, h_ref):
        h_ref[...] = _pre_rows(x_ref[...], g_ref[...], sh_ref[...], sc_ref[...]).astype(BF16)

    return _row_call(body, name, *x.shape, [x], [g, shift, scale], [BF16], 0)[0]


def post_pre_fwd(x, y, g_post, gate, g_pre, shift, scale, name):
    def body(x_ref, y_ref, gp_ref, gt_ref, g_ref, sh_ref, sc_ref, xn_ref, h_ref):
        xn = _post_rows(x_ref[...], y_ref[...], gp_ref[...], gt_ref[...])
        xn_ref[...] = xn
        h_ref[...] = _pre_rows(xn, g_ref[...], sh_ref[...], sc_ref[...]).astype(BF16)

    return _row_call(body, name, *x.shape, [x, y], [g_post, gate, g_pre, shift, scale], [F32, BF16], 0)


def post_loss_bwd(x, y, g, gate, target, name):
    d = x.shape[1]

    def body(x_ref, y_ref, t_ref, g_ref, gt_ref, dx_ref, dy_ref, l_ref, dgate_ref, dg_ref):
        yv, gv, gate_v = y_ref[...], g_ref[...], gt_ref[...]
        err = _post_rows(x_ref[...], yv, gv, gate_v) - t_ref[...]
        l_ref[...] += _colsum8(err * err)
        dxv = err * (1.0 / d)
        dx_ref[...] = dxv
        dy, dgate, dg = _post_bwd_rows(dxv, yv, gv, gate_v)
        dy_ref[...] = dy.astype(BF16)
        dgate_ref[...] += dgate
        dg_ref[...] += dg

    return _row_call(body, name, *x.shape, [x, y, target], [g, gate], [F32, BF16], 3)


def pre_post_bwd(dh, x, g_pre, scale, dxn, y, g_post, gate, name):
    def body(dh_ref, x_ref, dxn_ref, y_ref, g_ref, sc_ref, gp_ref, gt_ref,
             dx_ref, dy_ref, dsh_ref, dsc_ref, dg_ref, dgate_ref, dgp_ref):
        dx, dsh, dsc, dg = _pre_bwd_rows(dh_ref[...].astype(F32), x_ref[...], g_ref[...], sc_ref[...], dxn_ref[...])
        dx_ref[...] = dx
        dsh_ref[...] += dsh
        dsc_ref[...] += dsc
        dg_ref[...] += dg
        dy, dgate, dgp = _post_bwd_rows(dx, y_ref[...], gp_ref[...], gt_ref[...])
        dy_ref[...] = dy.astype(BF16)
        dgate_ref[...] += dgate
        dgp_ref[...] += dgp

    return _row_call(body, name, *x.shape, [dh, x, dxn, y], [g_pre, scale, g_post, gate], [F32, BF16], 5)


def pre_bwd(dh, x, g, scale, dxn, name):
    def body(dh_ref, x_ref, dxn_ref, g_ref, sc_ref, dx_ref, dsh_ref, dsc_ref, dg_ref):
        dx, dsh, dsc, dg = _pre_bwd_rows(dh_ref[...].astype(F32), x_ref[...], g_ref[...], sc_ref[...], dxn_ref[...])
        dx_ref[...] = dx
        dsh_ref[...] += dsh
        dsc_ref[...] += dsc
        dg_ref[...] += dg

    return _row_call(body, name, *x.shape, [dh, x, dxn], [g, scale], [F32], 3)


def _sigmoid(x):
    return 1.0 / (1.0 + jnp.exp(-x))


def ffn_in_swiglu(h, wg, name, tm=1024, riders=None):
    m, k = h.shape
    s, _, n = wg.shape
    half = s // 2
    tm = _div_tile(m, tm)

    def body(h_ref, wg_ref, wu_ref, g_ref, u_ref, s_ref):
        hv = h_ref[...]
        g = jnp.dot(hv, wg_ref[...], preferred_element_type=F32)
        u = jnp.dot(hv, wu_ref[...], preferred_element_type=F32)
        g_ref[...] = g
        u_ref[...] = u
        s_ref[...] = ((g * _sigmoid(g)) * u).astype(BF16)

    act = pl.BlockSpec((None, tm, n), lambda i, j: (j, i, 0))
    res, ridden = _rider_call(
        body, (h, wg, wg), riders, name=name,
        out_shape=[_sds((half, m, n), F32), _sds((half, m, n), F32), _sds((half, m, n), BF16)],
        grid=(m // tm, half),
        in_specs=[pl.BlockSpec((tm, k), lambda i, j: (i, 0)),
                  pl.BlockSpec((None, k, n), lambda i, j: (j, 0, 0)),
                  pl.BlockSpec((None, k, n), lambda i, j: (j + half, 0, 0))],
        out_specs=[act, act, act], scratch_shapes=[], sem=("parallel", "parallel"))
    return res[0], res[1], res[2], ridden


def ffn_out_dx_swiglu(dy, w4, gate, up, name, tm=1024, riders=None):
    m, nn = dy.shape
    half, n, _ = w4.shape
    tm = _div_tile(m, tm)

    def body(dy_ref, w_ref, g_ref, u_ref, o_ref):
        ds = lax.dot_general(dy_ref[...], w_ref[...], NT, preferred_element_type=F32)
        g, u = g_ref[...], u_ref[...]
        sig = _sigmoid(g)
        o_ref[0] = (ds * u * (sig * (1 + g * (1 - sig)))).astype(BF16)
        o_ref[1] = (ds * (g * sig)).astype(BF16)

    act = pl.BlockSpec((None, tm, n), lambda i, j: (j, i, 0))
    res, ridden = _rider_call(
        body, (dy, w4, gate, up), riders, name=name, out_shape=[_sds((2, half, m, n), BF16)],
        grid=(m // tm, half),
        in_specs=[pl.BlockSpec((tm, nn), lambda i, j: (i, 0)),
                  pl.BlockSpec((None, n, nn), lambda i, j: (j, 0, 0)), act, act],
        out_specs=[pl.BlockSpec((2, None, tm, n), lambda i, j: (0, j, i, 0))],
        scratch_shapes=[], sem=("parallel", "parallel"))
    return res[0].reshape(2 * half, m, n), ridden


def _split_hi_lo(v):
    hi = v.astype(BF16)
    lo = (v - hi.astype(F32)).astype(BF16)
    return hi, lo


SB_G = 2
SB_EXP_ZERO = 104.0
SB_UNSEEN = 3e38
SB_GW = SB_G * BLK


def _sb_specs(t):
    nq = t // BLK
    npair = N_HEADS // 2
    q_spec = pl.BlockSpec((BLK, BLK), lambda p, qb: (qb, p))
    k_spec = pl.BlockSpec((t, BLK), lambda p, qb: (0, npair + p))
    v_spec = pl.BlockSpec((t, BLK), lambda p, qb: (0, 2 * npair + p))
    kt_spec = pl.BlockSpec((t // SB_GW, BLK, SB_GW), lambda p, qb: (0, npair + p, 0))
    vt_spec = pl.BlockSpec((t // SB_GW, BLK, SB_GW), lambda p, qb: (0, 2 * npair + p, 0))
    c_spec = pl.BlockSpec((None, nq, 8, 2 * BLK), lambda p, qb: (p, 0, 0, qb))
    return nq, npair, q_spec, k_spec, v_spec, kt_spec, vt_spec, c_spec


def _sb_consts():
    row = lax.broadcasted_iota(jnp.int32, (BLK, BLK), 0)
    col = lax.broadcasted_iota(jnp.int32, (BLK, BLK), 1)
    lane0 = (col < HEAD_DIM).astype(F32)
    sub0 = (row < HEAD_DIM).astype(F32)
    return row, col, lane0, sub0


def _sb_valid(ks, qb):
    row = lax.broadcasted_iota(jnp.int32, (SB_GW, 2 * BLK), 0)
    col = lax.broadcasted_iota(jnp.int32, (SB_GW, 2 * BLK), 1)
    return (ks + row) < (qb * BLK + (col & (BLK - 1)))


def _blocks_on_lanes(v4):
    return jnp.concatenate([v4[b * BLK:(b + 1) * BLK] for b in range(SB_G)], axis=1)


def _tri2_dot(tri2, v):
    hi, lo = _split_hi_lo(v)
    return jnp.dot(tri2, jnp.concatenate([hi, lo], axis=0), preferred_element_type=F32)


def _sb_pair_loop(first, count, step, group, skip, carry):
    def pair(it, cy):
        g1 = first + 2 * step * it
        cy = group(g1, 0, 1, cy)
        return lax.cond(2 * it + 1 < count, lambda c: group(g1 + step, 1, 0, c), skip, cy)
    return lax.fori_loop(0, (count + 1) // 2, pair, carry)


def sb_fwd(qkv, qkv_t, riders, name):
    t = qkv.shape[0]
    assert t % SB_GW == 0
    nq, npair, q_spec, k_spec, _, _, vt_spec, c_spec = _sb_specs(t)
    nr = len(riders)

    def body(*refs):
        q_ref, k_ref, vt_ref = refs[:3]
        o_ref, c_ref = refs[3 + nr:5 + nr]
        oacc, zbuf0, zbuf1, kmax = refs[5 + 2 * nr:9 + 2 * nr]
        pp = pl.program_id(0)
        qb = pl.program_id(1)
        if nr:
            gather = _Gather(refs[3:3 + nr], refs[5 + nr:5 + 2 * nr], *refs[9 + 2 * nr:])
            pl.when((pp == 0) & (qb == 0))(gather.start)
            pl.when((pp == npair - 2) & (qb == 0))(gather.forward)
        _sb_fwd_step(q_ref, k_ref, vt_ref, o_ref, c_ref, oacc, zbuf0, zbuf1, kmax, qb)
        if nr:
            pl.when((pp == npair - 1) & (qb == nq - 1))(gather.finish)

    any_spec = pl.BlockSpec(memory_space=pl.ANY)
    outs = _call(
        body, name=name,
        out_shape=[_sds((t, D_MODEL), BF16), _sds((npair, nq, 8, 2 * t), F32)] + _Gather.out_shapes(riders),
        grid=(npair, nq), in_specs=[q_spec, k_spec, vt_spec] + [any_spec] * nr,
        out_specs=[pl.BlockSpec((BLK, BLK), lambda p, qb: (qb, p)), c_spec] + [any_spec] * nr,
        scratch_shapes=[pltpu.VMEM((BLK, 2 * BLK), F32), pltpu.VMEM((SB_GW, 2 * BLK), F32),
                        pltpu.VMEM((SB_GW, 2 * BLK), F32), pltpu.VMEM((8, BLK), F32)] + _comm_sems(nr),
        compiler_params=_params("arbitrary", "arbitrary"),
    )(qkv, qkv, qkv_t, *riders)
    return outs[0], outs[1], list(outs[2:])


def _sb_fwd_step(q_ref, k_ref, vt_ref, o_ref, c_ref, oacc, zbuf0, zbuf1, kmax, qb):
    row, col, lane0, sub0 = _sb_consts()
    tri = (col >= row).astype(BF16)
    tri2 = jnp.concatenate([tri, tri], axis=1)
    q2 = _two_heads(q_ref[...], lane0, QK_SCALE)
    zbufs = (zbuf0, zbuf1)
    c_ref[...] = jnp.full(c_ref.shape, SB_UNSEEN, F32)
    oacc[...] = jnp.zeros_like(oacc)

    @pl.when(qb == 0)
    def _():
        ksq = jnp.square(k_ref[...].astype(F32))
        head0 = (lax.broadcasted_iota(jnp.int32, (1, BLK), 1) < HEAD_DIM).astype(F32)
        norms = jnp.maximum(jnp.sum(ksq * head0, axis=1, keepdims=True),
                            jnp.sum(ksq * (1.0 - head0), axis=1, keepdims=True))
        kmax[...] = jnp.broadcast_to(jnp.max(norms, axis=0, keepdims=True), kmax.shape)

    qsq = jnp.square(q2.astype(F32)).astype(BF16)
    qn2 = jnp.max(lax.dot_general(jnp.ones((8, BLK), BF16), qsq, NT, preferred_element_type=F32),
                  axis=0, keepdims=True)
    kk = kmax[0:1, :]
    zbound = jnp.sqrt(qn2 * jnp.concatenate([kk, kk], axis=1)) * 1.02

    def matters(cr):
        return (jnp.min(cr - zbound) <= SB_EXP_ZERO).astype(jnp.int32)

    def scores(g):
        ks = pl.multiple_of(g * SB_GW, SB_GW)
        return lax.dot_general(k_ref[pl.ds(ks, SB_GW), :], q2, NT, preferred_element_type=F32)

    def group(g, cur, nxt, cr, masked=False):
        z = zbufs[cur][...]
        zbufs[nxt][...] = scores(jnp.maximum(g - 1, 0))
        e = jnp.exp(-jnp.abs(z))
        sp = jnp.maximum(z, 0.0) + jnp.log(1.0 + e)
        if masked:
            valid = _sb_valid(g * SB_GW, qb)
            sp = jnp.where(valid, sp, 0.0)
        loc = _tri2_dot(tri2, _blocks_on_lanes(sp))
        parts = [None] * SB_G
        for b in reversed(range(SB_G)):
            rows = slice(b * BLK, (b + 1) * BLK)
            c_ref[g * SB_G + b] = jnp.broadcast_to(cr, (8, 2 * BLK))
            a = jnp.exp(z[rows] - (loc[:, 2 * b * BLK:2 * (b + 1) * BLK] + cr))
            if masked:
                a = jnp.where(valid[rows], a, 0.0)
            parts[b] = a.astype(BF16)
            cr = cr + jnp.sum(sp[rows], axis=0, keepdims=True)
        oacc[...] += jnp.dot(vt_ref[g], jnp.concatenate(parts, axis=0), preferred_element_type=F32)
        return cr

    last = qb // SB_G
    zbuf1[...] = scores(last)
    cr = group(last, 1, 0, jnp.zeros((1, 2 * BLK), F32), masked=True)

    def pair(state):
        g, cr, _ = state
        cr = group(g, 0, 1, cr)
        more = (g >= 1).astype(jnp.int32) * matters(cr)
        cr = lax.cond(more > 0, lambda c: group(g - 1, 1, 0, c), lambda c: c, cr)
        return jnp.where(more > 0, g - 2, -1), cr, matters(cr)

    lax.while_loop(lambda st: (st[0] >= 0) & (st[2] > 0), pair, (last - 1, cr, matters(cr)))
    o_t = oacc[:, :BLK] * sub0 + oacc[:, BLK:] * (1.0 - sub0)
    o_ref[...] = o_t.T.astype(BF16)


def sb_bwd(qkv, qkv_t, do, cmass, riders, name):
    t = qkv.shape[0]
    nq, npair, q_spec, k_spec, v_spec, kt_spec, _, c_spec = _sb_specs(t)
    nr = len(riders)

    def body(*refs):
        pp = pl.program_id(0)
        qb = pl.program_id(1)
        if nr:
            exchange = _Exchange(refs[6:6 + nr], refs[9 + nr:9 + 2 * nr], *refs[14 + 2 * nr:])
            pl.when((pp == 0) & (qb == 0))(exchange.start)
        step(*refs[:6], *refs[6 + nr:9 + nr], *refs[9 + 2 * nr:14 + 2 * nr])
        if nr:
            pl.when((pp == npair - 1) & (qb == nq - 1))(exchange.finish)

    def step(q_ref, k_ref, kt_ref, v_ref, do_ref, c_ref, dq_ref, dk_ref, dv_ref, dqacc, dkacc, dvacc,
             zbuf0, zbuf1):
        qb = pl.program_id(1)

        @pl.when(qb == 0)
        def _():
            dkacc[...] = jnp.zeros_like(dkacc)
            dvacc[...] = jnp.zeros_like(dvacc)

        row, col, lane0, sub0 = _sb_consts()
        tri_suf = (col >= row).astype(BF16)
        tri_pre = (col <= row).astype(BF16)
        tri2_suf = jnp.concatenate([tri_suf, tri_suf], axis=1)
        tri2_pre = jnp.concatenate([tri_pre, tri_pre], axis=1)
        q2 = _two_heads(q_ref[...], lane0, QK_SCALE)
        do2 = _two_heads(do_ref[...], lane0, 1.0)
        zbufs = (zbuf0, zbuf1)
        dqacc[...] = jnp.zeros_like(dqacc)
        last = qb // SB_G

        def scores(g):
            ks = pl.multiple_of(g * SB_GW, SB_GW)
            return lax.dot_general(k_ref[pl.ds(ks, SB_GW), :], q2, NT, preferred_element_type=F32)

        def group(g, cur, nxt, gc, masked=False):
            ks = pl.multiple_of(g * SB_GW, SB_GW)
            z = zbufs[cur][...]
            zbufs[nxt][...] = scores(jnp.minimum(g + 1, last))
            e = jnp.exp(-jnp.abs(z))
            sig = 0.5 * jnp.tanh(0.5 * z) + 0.5
            sp = jnp.maximum(z, 0.0) + jnp.log(1.0 + e)
            if masked:
                valid = _sb_valid(ks, qb)
                sp = jnp.where(valid, sp, 0.0)
            loc = _tri2_dot(tri2_suf, _blocks_on_lanes(sp))
            parts = []
            for b in range(SB_G):
                rows = slice(b * BLK, (b + 1) * BLK)
                mass = loc[:, 2 * b * BLK:2 * (b + 1) * BLK] + c_ref[g * SB_G + b, 0:1, :]
                parts.append(jnp.exp(z[rows] - mass))
            a = jnp.concatenate(parts, axis=0)
            if masked:
                a = jnp.where(valid, a, 0.0)
            gr = lax.dot_general(v_ref[pl.ds(ks, SB_GW), :], do2, NT, preferred_element_type=F32) * a
            pre = _tri2_dot(tri2_pre, _blocks_on_lanes(gr))
            parts = []
            for b in range(SB_G):
                rows = slice(b * BLK, (b + 1) * BLK)
                parts.append(pre[:, 2 * b * BLK:2 * (b + 1) * BLK] + gc)
                gc = gc + jnp.sum(gr[rows], axis=0, keepdims=True)
            dz = gr - sig * jnp.concatenate(parts, axis=0)
            if masked:
                dz = jnp.where(valid, dz, 0.0)
            dz = dz.astype(BF16)
            dkacc[pl.ds(ks, SB_GW), :] += jnp.dot(dz, q2, preferred_element_type=F32)
            dqacc[...] += jnp.dot(kt_ref[g], dz, preferred_element_type=F32)
            dvacc[pl.ds(ks, SB_GW), :] += jnp.dot(a.astype(BF16), do2, preferred_element_type=F32)
            return gc

        def skip(gc):
            zbuf0[...] = zbuf1[...]
            return gc

        def unseen(g):
            return (jnp.max(c_ref[g * SB_G + SB_G - 1, 0:1, :]) > 0.5 * SB_UNSEEN).astype(jnp.int32)

        first, _ = lax.while_loop(lambda st: (st[0] > 0) & (st[1] == 0),
                                  lambda st: (st[0] - 1, unseen(jnp.maximum(st[0] - 2, 0))),
                                  (last, unseen(jnp.maximum(last - 1, 0))))
        zbuf0[...] = scores(first)
        gc = _sb_pair_loop(first, last - first, 1, group, skip, jnp.zeros((1, 2 * BLK), F32))
        group(last, 0, 1, gc, masked=True)
        dq_t = (dqacc[:, :BLK] * sub0 + dqacc[:, BLK:] * (1.0 - sub0)) * QK_SCALE
        dq_ref[...] = dq_t.T.astype(BF16)

        @pl.when(qb == nq - 1)
        def _():
            dk_ref[...] = dkacc[...].astype(BF16)
            dv_ref[...] = dvacc[...].astype(BF16)

    col_spec = pl.BlockSpec((t, BLK), lambda p, qb: (0, p))
    blk_spec = pl.BlockSpec((BLK, BLK), lambda p, qb: (qb, p))
    any_spec = pl.BlockSpec(memory_space=pl.ANY)
    outs = _call(
        body, name=name,
        out_shape=[_sds((t, D_MODEL), BF16)] * 3 + [_sds(r.shape, r.dtype) for r in riders],
        grid=(npair, nq), in_specs=[q_spec, k_spec, kt_spec, v_spec, blk_spec, c_spec] + [any_spec] * nr,
        out_specs=[blk_spec, col_spec, col_spec] + [any_spec] * nr,
        scratch_shapes=[pltpu.VMEM((BLK, 2 * BLK), F32), pltpu.VMEM((t, BLK), F32), pltpu.VMEM((t, BLK), F32),
                        pltpu.VMEM((SB_GW, 2 * BLK), F32), pltpu.VMEM((SB_GW, 2 * BLK), F32)] + _comm_sems(nr),
        compiler_params=_params("arbitrary", "arbitrary"),
    )(qkv, qkv, qkv_t, qkv, do, cmass, *riders)
    return outs[0], outs[1], outs[2], list(outs[3:])


BAND_QPS = 4


def _band_static_mask(jj):
    row = lax.broadcasted_iota(jnp.int32, (2 * BLK, BLK), 0)
    col = lax.broadcasted_iota(jnp.int32, (2 * BLK, BLK), 1)
    qc = (row & (BLK - 1)) // 64
    kc = 2 * jj + col // 64
    return (kc >= qc) & (kc <= qc + 8)


def _band_key_start(qb, jj):
    kb = qb - (BAND_BLOCKS - 1) + jj
    return kb, pl.multiple_of(jnp.maximum(kb, 0) * BLK, BLK)


def _band_probs(q2, k_ref, bias, qb):
    blocks = []
    for jj in range(BAND_BLOCKS):
        kb, ks = _band_key_start(qb, jj)
        s = lax.dot_general(q2, k_ref[pl.ds(ks, BLK), :], NT, preferred_element_type=F32)
        s = s + bias[:, jj * BLK:(jj + 1) * BLK]
        ok = (kb >= 0) if 0 < jj < BAND_BLOCKS - 1 else _band_static_mask(jj) & (kb >= 0)
        blocks.append(jnp.where(ok, s, NEG))
    s = jnp.concatenate(blocks, axis=1)
    m = jnp.max(s, axis=-1, keepdims=True)
    e = jnp.exp(s - m)
    return e / jnp.sum(e, axis=-1, keepdims=True)


def _band_specs(t):
    npair = N_HEADS // 2
    rows = BAND_QPS * BLK
    q_spec = pl.BlockSpec((rows, BLK), lambda p, i: (i, p))
    k_spec = pl.BlockSpec((t, BLK), lambda p, i: (0, npair + p))
    v_spec = pl.BlockSpec((t, BLK), lambda p, i: (0, 2 * npair + p))
    b_spec = pl.BlockSpec((2, BLK, BAND_W), lambda p, i: (p, 0, 0))
    return npair, t // rows, q_spec, k_spec, v_spec, b_spec


def _two_heads(xv, lane0, scale):
    xf = xv.astype(F32)
    if scale != 1.0:
        xf = xf * scale
    return jnp.concatenate([xf * lane0, xf * (1.0 - lane0)], axis=0).astype(BF16)


def _one_of_two_heads(r, lane0):
    return r[:BLK] * lane0 + r[BLK:] * (1.0 - lane0)


def band_fwd(qkv, bias, name):
    t = qkv.shape[0]
    assert t % (BAND_QPS * BLK) == 0
    npair, nsteps, q_spec, k_spec, v_spec, b_spec = _band_specs(t)

    def body(q_ref, k_ref, v_ref, b_ref, o_ref):
        step = pl.program_id(1)
        _, _, lane0, _ = _sb_consts()
        bias2 = b_ref[...].reshape(2 * BLK, BAND_W)
        for u in range(BAND_QPS):
            qb = step * BAND_QPS + u
            rows = slice(u * BLK, (u + 1) * BLK)
            q2 = _two_heads(q_ref[rows, :], lane0, QK_SCALE)
            p = _band_probs(q2, k_ref, bias2, qb)
            acc = jnp.zeros((2 * BLK, BLK), F32)
            for jj in range(BAND_BLOCKS):
                _, ks = _band_key_start(qb, jj)
                acc += jnp.dot(p[:, jj * BLK:(jj + 1) * BLK].astype(BF16), v_ref[pl.ds(ks, BLK), :],
                               preferred_element_type=F32)
            o_ref[rows, :] = _one_of_two_heads(acc, lane0).astype(BF16)

    return _call(
        body, name=name, out_shape=_sds((t, D_MODEL), BF16), grid=(npair, nsteps),
        in_specs=[q_spec, k_spec, v_spec, b_spec],
        out_specs=pl.BlockSpec((BAND_QPS * BLK, BLK), lambda p, i: (i, p)),
        compiler_params=_params("parallel", "parallel"),
    )(qkv, qkv, qkv, bias)


def band_bwd(qkv, do, bias, name, riders=None):
    t = qkv.shape[0]
    npair, nsteps, q_spec, k_spec, v_spec, b_spec = _band_specs(t)

    def body(q_ref, k_ref, v_ref, do_ref, b_ref, dq_ref, dk_ref, dv_ref, db_ref, dkacc, dvacc):
        step = pl.program_id(1)

        @pl.when(step == 0)
        def _():
            dkacc[...] = jnp.zeros_like(dkacc)
            dvacc[...] = jnp.zeros_like(dvacc)
            db_ref[...] = jnp.zeros_like(db_ref)

        _, _, lane0, _ = _sb_consts()
        bias2 = b_ref[...].reshape(2 * BLK, BAND_W)
        updates = []
        for u in range(BAND_QPS):
            qb = step * BAND_QPS + u
            rows = slice(u * BLK, (u + 1) * BLK)
            q2 = _two_heads(q_ref[rows, :], lane0, QK_SCALE)
            do2 = _two_heads(do_ref[rows, :], lane0, 1.0)
            p = _band_probs(q2, k_ref, bias2, qb)
            dp = jnp.concatenate(
                [lax.dot_general(do2, v_ref[pl.ds(_band_key_start(qb, jj)[1], BLK), :], NT,
                                 preferred_element_type=F32) for jj in range(BAND_BLOCKS)], axis=1)
            ds = p * (dp - jnp.sum(p * dp, axis=-1, keepdims=True))
            db_ref[...] += ds.reshape(2, BLK, BAND_W)
            dqa = jnp.zeros((2 * BLK, BLK), F32)
            for jj in range(BAND_BLOCKS):
                _, ks = _band_key_start(qb, jj)
                dsb = ds[:, jj * BLK:(jj + 1) * BLK].astype(BF16)
                pb = p[:, jj * BLK:(jj + 1) * BLK].astype(BF16)
                dqa += jnp.dot(dsb, k_ref[pl.ds(ks, BLK), :], preferred_element_type=F32)
                updates.append((ks, lax.dot_general(dsb, q2, TN, preferred_element_type=F32),
                                lax.dot_general(pb, do2, TN, preferred_element_type=F32)))
            dq_ref[rows, :] = (_one_of_two_heads(dqa, lane0) * QK_SCALE).astype(BF16)
        for ks, dk_part, dv_part in updates:
            dkacc[pl.ds(ks, BLK), :] += dk_part
            dvacc[pl.ds(ks, BLK), :] += dv_part

        @pl.when(step == nsteps - 1)
        def _():
            dk_ref[...] = dkacc[...].astype(BF16)
            dv_ref[...] = dvacc[...].astype(BF16)

    col_spec = pl.BlockSpec((t, BLK), lambda p, i: (0, p))
    blk_spec = pl.BlockSpec((BAND_QPS * BLK, BLK), lambda p, i: (i, p))
    res, ridden = _rider_call(
        body, (qkv, qkv, qkv, do, bias), riders, name=name,
        out_shape=[_sds((t, D_MODEL), BF16)] * 3 + [_sds((N_HEADS, BLK, BAND_W), F32)],
        grid=(npair, nsteps), in_specs=[q_spec, k_spec, v_spec, blk_spec, b_spec],
        out_specs=[blk_spec, col_spec, col_spec, b_spec],
        scratch_shapes=[pltpu.VMEM((t, BLK), F32), pltpu.VMEM((t, BLK), F32)],
        sem=("parallel", "arbitrary"))
    return res[0], res[1], res[2], res[3], ridden


def band_bias_window(rel_bias):
    far = BAND_W + BLK - 1 - 2 * REL_CLIP
    width = BAND_W + BLK
    ext = jnp.concatenate(
        [jnp.broadcast_to(rel_bias[:, 2 * REL_CLIP:], (N_HEADS, far)), rel_bias[:, 2 * REL_CLIP:0:-1],
         jnp.zeros((N_HEADS, 2), F32)], axis=1)
    tiled = jnp.broadcast_to(ext[:, None, :], (N_HEADS, BLK, width + 1)).reshape(N_HEADS, BLK * (width + 1))
    return tiled[:, BLK - 1:BLK - 1 + BLK * width].reshape(N_HEADS, BLK, width)[:, :, :BAND_W]


def band_bias_window_grad(dwin):
    width = BAND_W + BLK
    far = BAND_W + BLK - 1 - 2 * REL_CLIP
    flat = jnp.pad(dwin, ((0, 0), (0, 0), (0, BLK))).reshape(N_HEADS, BLK * width)
    skew = jnp.pad(flat, ((0, 0), (BLK - 1, 1))).reshape(N_HEADS, BLK, width + 1)
    dext = jnp.sum(skew, axis=1)[:, :width - 1]
    return jnp.concatenate(
        [jnp.zeros((N_HEADS, 1), F32), dext[:, :far - 1:-1][:, :2 * REL_CLIP - 1],
         dext[:, far:far + 1] + jnp.sum(dext[:, :far], axis=1, keepdims=True)], axis=1)


SG_GROUPS = 8


def _gelu_parts(x):
    inner = GELU_C0 * (x + GELU_C1 * (x * x * x))
    th = jnp.tanh(inner)
    return th, 0.5 * x * (1.0 + th)


def _sg_gate_mask():
    row = lax.broadcasted_iota(jnp.int32, (BLK, BLK), 0)
    col = lax.broadcasted_iota(jnp.int32, (BLK, BLK), 1)
    return (row // 64) >= (col // 64)


def _sg_normed_half(av, lng):
    th, v = _gelu_parts(av)
    mu = jnp.mean(v, axis=-1, keepdims=True)
    xc = v - mu
    rstd = lax.rsqrt(jnp.mean(xc * xc, axis=-1, keepdims=True) + EPS)
    vhat = xc * rstd
    return th, vhat, rstd, vhat * lng


def _gelu_grad(x, th):
    return 0.5 * (1.0 + th) + (0.5 * x) * (1.0 - th * th) * (GELU_C0 * (1.0 + 3.0 * GELU_C1 * (x * x)))


def sg_fwd(a, lng, ws, bias_t, name):
    t, w2 = a.shape
    w = w2 // 2
    gc = w // SG_GROUPS

    def body(a_ref, lng_ref, ws_ref, bt_ref, y_ref):
        _, _, _, vln = _sg_normed_half(a_ref[:, w:], lng_ref[...])
        mask = _sg_gate_mask()
        bt = bt_ref[...]
        lane = lax.broadcasted_iota(jnp.int32, (BLK, BLK), 1)
        for g in range(SG_GROUPS):
            sl = slice(g * gc, (g + 1) * gc)
            wg = jnp.where(mask, ws_ref[g], 0.0).astype(BF16)
            sv = jnp.dot(wg, vln[:, sl].astype(BF16), preferred_element_type=F32)
            bg = jnp.sum(jnp.where(lane == g, bt, 0.0), axis=-1, keepdims=True)
            y_ref[:, sl] = (_gelu_parts(a_ref[:, sl])[1] * (sv + bg)).astype(BF16)

    return _call(
        body, name=name, out_shape=_sds((t, w), BF16), grid=(t // BLK,),
        in_specs=[pl.BlockSpec((BLK, w2), lambda i: (i, 0)), pl.BlockSpec((1, w), lambda i: (0, 0)),
                  pl.BlockSpec((SG_GROUPS, BLK, BLK), lambda i: (0, 0, 0)),
                  pl.BlockSpec((BLK, BLK), lambda i: (0, 0))],
        out_specs=pl.BlockSpec((BLK, w), lambda i: (i, 0)),
        compiler_params=_params("parallel"),
    )(a, lng, ws, bias_t)


def sg_bwd(a, dy, lng, ws, bias_t, name):
    t, w2 = a.shape
    w = w2 // 2
    gc = w // SG_GROUPS

    def body(a_ref, dy_ref, lng_ref, ws_ref, bt_ref, da_ref, dlng_ref, dws_ref, dbt_ref):
        @pl.when(pl.program_id(0) == 0)
        def _():
            dlng_ref[...] = jnp.zeros_like(dlng_ref)
            dws_ref[...] = jnp.zeros_like(dws_ref)
            dbt_ref[...] = jnp.zeros_like(dbt_ref)

        lng = lng_ref[...]
        av_v = a_ref[:, w:]
        th_v, vhat, rstd, vln = _sg_normed_half(av_v, lng)
        mask = _sg_gate_mask()
        bt = bt_ref[...]
        lane = lax.broadcasted_iota(jnp.int32, (BLK, BLK), 1)
        dvln_parts = []
        dbt = jnp.zeros((BLK, BLK), F32)
        for g in range(SG_GROUPS):
            sl = slice(g * gc, (g + 1) * gc)
            wg = jnp.where(mask, ws_ref[g], 0.0).astype(BF16)
            vg = vln[:, sl].astype(BF16)
            sv = jnp.dot(wg, vg, preferred_element_type=F32)
            bg = jnp.sum(jnp.where(lane == g, bt, 0.0), axis=-1, keepdims=True)
            dyg = dy_ref[:, sl]
            au = a_ref[:, sl]
            th_u, u = _gelu_parts(au)
            da_ref[:, sl] = (dyg * (sv + bg) * _gelu_grad(au, th_u)).astype(BF16)
            dsv = dyg * u
            dbt += jnp.where(lane == g, jnp.sum(dsv, axis=-1, keepdims=True), 0.0)
            dsvb = dsv.astype(BF16)
            dws_ref[g] += jnp.where(mask, lax.dot_general(dsvb, vg, NT, preferred_element_type=F32), 0.0)
            dvln_parts.append(lax.dot_general(wg, dsvb, TN, preferred_element_type=F32))
        dbt_ref[...] += dbt
        dvln = jnp.concatenate(dvln_parts, axis=1)
        dlng_ref[...] += _colsum8(dvln * vhat)
        dvhat = dvln * lng
        dv = rstd * (dvhat - jnp.mean(dvhat, axis=-1, keepdims=True)
                     - vhat * jnp.mean(dvhat * vhat, axis=-1, keepdims=True))
        da_ref[:, w:] = (dv * _gelu_grad(av_v, th_v)).astype(BF16)

    return _call(
        body, name=name,
        out_shape=[_sds((t, w2), BF16), _sds((8, w), F32), _sds((SG_GROUPS, BLK, BLK), F32), _sds((BLK, BLK), F32)],
        grid=(t // BLK,),
        in_specs=[pl.BlockSpec((BLK, w2), lambda i: (i, 0)), pl.BlockSpec((BLK, w), lambda i: (i, 0)),
                  pl.BlockSpec((1, w), lambda i: (0, 0)),
                  pl.BlockSpec((SG_GROUPS, BLK, BLK), lambda i: (0, 0, 0)),
                  pl.BlockSpec((BLK, BLK), lambda i: (0, 0))],
        out_specs=[pl.BlockSpec((BLK, w2), lambda i: (i, 0)), pl.BlockSpec((8, w), lambda i: (0, 0)),
                   pl.BlockSpec((SG_GROUPS, BLK, BLK), lambda i: (0, 0, 0)),
                   pl.BlockSpec((BLK, BLK), lambda i: (0, 0))],
        compiler_params=_params("arbitrary"),
    )(a, dy, lng, ws, bias_t)


def _shift_down(cat, n, tr):
    return pltpu.roll(cat, n, 0)[8:8 + tr]


def _shift_up(cat, n, tr):
    return pltpu.roll(cat, tr + 8 - n, 0)[0:tr]


def conv_fwd(p, cw, name):
    t, d3 = p.shape
    d = d3 // 3
    tr = min(256, t)
    hb = tr // 8

    def body(p_ref, ph_ref, cw_ref, o_ref):
        i = pl.program_id(0)
        pv = p_ref[...]
        y = pv[:, d:2 * d] * pv[:, 2 * d:]
        ph = ph_ref[...]
        yh = jnp.where(i > 0, ph[:, d:2 * d] * ph[:, 2 * d:], 0.0)
        cat = jnp.concatenate([yh, y], axis=0)
        yc = (cw_ref[0:1, :] * _shift_down(cat, 2, tr) + cw_ref[1:2, :] * _shift_down(cat, 1, tr)
              + cw_ref[2:3, :] * y)
        o_ref[...] = (pv[:, :d] * yc).astype(BF16)

    return _call(
        body, name=name, out_shape=_sds((t, d), BF16), grid=(t // tr,),
        in_specs=[pl.BlockSpec((tr, d3), lambda i: (i, 0)),
                  pl.BlockSpec((8, d3), lambda i: (jnp.maximum(i * hb - 1, 0), 0)),
                  pl.BlockSpec((8, d), lambda i: (0, 0))],
        out_specs=pl.BlockSpec((tr, d), lambda i: (i, 0)),
        compiler_params=_params("parallel"),
    )(p, p, cw)


def conv_bwd(p, dz, cw, name):
    t, d3 = p.shape
    d = d3 // 3
    tr = min(256, t)
    hb = tr // 8
    nt = t // tr

    def body(p_ref, ph_ref, pn_ref, dz_ref, dzn_ref, cw_ref, dp_ref, dcw_ref):
        i = pl.program_id(0)

        @pl.when(i == 0)
        def _():
            dcw_ref[...] = jnp.zeros_like(dcw_ref)

        pv = p_ref[...]
        gb, gcv, xt = pv[:, :d], pv[:, d:2 * d], pv[:, 2 * d:]
        y = gcv * xt
        ph = ph_ref[...]
        yh = jnp.where(i > 0, ph[:, d:2 * d] * ph[:, 2 * d:], 0.0)
        cat = jnp.concatenate([yh, y], axis=0)
        y2, y1 = _shift_down(cat, 2, tr), _shift_down(cat, 1, tr)
        w0, w1, w2 = cw_ref[0:1, :], cw_ref[1:2, :], cw_ref[2:3, :]
        yc = w0 * y2 + w1 * y1 + w2 * y
        dzv = dz_ref[...]
        dyc = dzv * gb
        dcw_ref[0] += _colsum8(dyc * y2)
        dcw_ref[1] += _colsum8(dyc * y1)
        dcw_ref[2] += _colsum8(dyc * y)
        dycn = jnp.where(i < nt - 1, dzn_ref[...] * pn_ref[...][:, :d], 0.0)
        catn = jnp.concatenate([dyc, dycn], axis=0)
        dy = w2 * dyc + w1 * _shift_up(catn, 1, tr) + w0 * _shift_up(catn, 2, tr)
        dp_ref[:, :d] = (dzv * yc).astype(BF16)
        dp_ref[:, d:2 * d] = (dy * xt).astype(BF16)
        dp_ref[:, 2 * d:] = (dy * gcv).astype(BF16)

    nxt = lambda i: (jnp.minimum((i + 1) * hb, t // 8 - 1), 0)
    return _call(
        body, name=name, out_shape=[_sds((t, d3), BF16), _sds((3, 8, d), F32)], grid=(nt,),
        in_specs=[pl.BlockSpec((tr, d3), lambda i: (i, 0)),
                  pl.BlockSpec((8, d3), lambda i: (jnp.maximum(i * hb - 1, 0), 0)),
                  pl.BlockSpec((8, d3), nxt),
                  pl.BlockSpec((tr, d), lambda i: (i, 0)),
                  pl.BlockSpec((8, d), nxt),
                  pl.BlockSpec((8, d), lambda i: (0, 0))],
        out_specs=[pl.BlockSpec((tr, d3), lambda i: (i, 0)), pl.BlockSpec((3, 8, d), lambda i: (0, 0, 0))],
        compiler_params=_params("arbitrary"),
    )(p, p, p, dz, dz, cw)


def ada_fwd(c_all, w, b, name):
    nl, d, n = w.shape

    def body(c_ref, w_ref, b_ref, o_ref):
        cv = c_ref[...]
        s = (cv * _sigmoid(cv)).astype(BF16)
        o_ref[...] = jnp.dot(s, w_ref[...].astype(BF16), preferred_element_type=F32) + b_ref[...]

    return _call(
        body, name=name, out_shape=_sds((nl, N_DEV, n), F32), grid=(nl,),
        in_specs=[pl.BlockSpec((N_DEV, d), lambda l: (0, 0)), pl.BlockSpec((None, d, n), lambda l: (l, 0, 0)),
                  pl.BlockSpec((None, 1, n), lambda l: (l, 0, 0))],
        out_specs=pl.BlockSpec((None, N_DEV, n), lambda l: (l, 0, 0)),
        compiler_params=_params("parallel"),
    )(c_all, w, b)


def ada_bwd(c_all, dmod, name):
    nl, _, n = dmod.shape
    d = c_all.shape[1]

    def body(c_ref, dm_ref, o_ref):
        cv = c_ref[...]
        s = (cv * _sigmoid(cv)).astype(BF16)
        o_ref[...] = lax.dot_general(s, dm_ref[...].astype(BF16), TN, preferred_element_type=F32)

    return _call(
        body, name=name, out_shape=_sds((nl, d, n), F32), grid=(nl,),
        in_specs=[pl.BlockSpec((N_DEV, d), lambda l: (0, 0)), pl.BlockSpec((None, N_DEV, n), lambda l: (l, 0, 0))],
        out_specs=pl.BlockSpec((None, d, n), lambda l: (l, 0, 0)),
        compiler_params=_params("parallel"),
    )(c_all, dmod)


def adamw(pieces, w, m, v, name, riders=None):
    nl = len(pieces)
    npc, r, c = pieces[0].shape
    tr = r
    for cand in (1024, 512, 256, 128, 64, 32, 16, 8):
        if r % cand == 0 and cand * c * 4 <= (1 << 20):
            tr = cand
            break
    nt = r // tr

    def update(p_ref, w_ref, m_ref, v_ref, g_ref, d_ref, nm_ref, nv_ref):
        g = p_ref[0].astype(F32)
        for i in range(1, npc):
            g = g + p_ref[i].astype(F32)
        wv = w_ref[...]
        nm = ADAM_B1 * m_ref[...] + (1.0 - ADAM_B1) * g
        nv = ADAM_B2 * v_ref[...] + (1.0 - ADAM_B2) * (g * g)
        m_hat = nm / (1.0 - ADAM_B1 ** ADAM_STEP)
        v_hat = nv / (1.0 - ADAM_B2 ** ADAM_STEP)
        g_ref[...] = g
        d_ref[...] = -ADAM_LR * (m_hat / (jnp.sqrt(v_hat) + ADAM_EPS) + ADAM_WD * wv)
        nm_ref[...] = nm
        nv_ref[...] = nv

    def body(*refs):
        if nl == 1:
            update(*refs)
        else:
            for j in range(nl):
                pl.when(pl.program_id(0) == j)(lambda j=j: update(refs[j], *refs[nl:]))

    row = pl.BlockSpec((tr, c), lambda l, i: (l * nt + i, 0))
    piece_specs = [pl.BlockSpec((npc, tr, c), lambda l, i, j=j: (0, jnp.where(l == j, i, 0), 0))
                   for j in range(nl)]
    res, ridden = _rider_call(
        body, (*pieces, w, m, v), riders, name=name, out_shape=[_sds((nl * r, c), F32)] * 4, grid=(nl, nt),
        in_specs=piece_specs + [row, row, row], out_specs=[row] * 4, scratch_shapes=[],
        sem=("parallel", "parallel"))
    return res, ridden


def sum_pieces(pieces, name):
    npc, r, c = pieces.shape

    def body(p_ref, o_ref):
        g = p_ref[0]
        for i in range(1, npc):
            g = g + p_ref[i]
        o_ref[...] = g

    return _call(body, name=name, out_shape=_sds((r, c), F32),
                 in_specs=[pl.BlockSpec(memory_space=pltpu.VMEM)],
                 out_specs=pl.BlockSpec(memory_space=pltpu.VMEM),
                 compiler_params=pltpu.CompilerParams(vmem_limit_bytes=VMEM_LIMIT))(pieces)


PACK_W = 1024


def _pack(arrs):
    flat = jnp.concatenate([a.reshape(-1).astype(F32) for a in arrs])
    rows = -(-flat.shape[0] // (8 * PACK_W)) * 8
    return jnp.pad(flat, (0, rows * PACK_W - flat.shape[0])).reshape(rows, PACK_W)


def _unpack(slab, shapes):
    flat = slab.reshape(-1)
    out, off = [], 0
    for s in shapes:
        n = 1
        for q in s:
            n *= q
        out.append(flat[off:off + n].reshape(s))
        off += n
    return out


def kernel(x, c, ada_w, ada_b, norm_g, ffn_w_in, ffn_w_out, sb_w_qkv, sb_w_o, sg_w_in, sg_ln_g, sg_w_s, sg_bias, sg_w_out, sc_w_in, sc_conv_w, sc_w_out, cb_w_qkv, cb_rel_bias, cb_w_o, loss_target, m_ada_w, m_ada_b, m_norm_g, m_ffn_w_in, m_ffn_w_out, m_sb_w_qkv, m_sb_w_o, m_sg_w_in, m_sg_ln_g, m_sg_w_s, m_sg_bias, m_sg_w_out, m_sc_w_in, m_sc_conv_w, m_sc_w_out, m_cb_w_qkv, m_cb_rel_bias, m_cb_w_o, v_ada_w, v_ada_b, v_norm_g, v_ffn_w_in, v_ffn_w_out, v_sb_w_qkv, v_sb_w_o, v_sg_w_in, v_sg_ln_g, v_sg_w_s, v_sg_bias, v_sg_w_out, v_sc_w_in, v_sc_conv_w, v_sc_w_out, v_cb_w_qkv, v_cb_rel_bias, v_cb_w_o):
    depth = ada_w.shape[0]
    d = D_MODEL
    xi, yi, ci = lax.axis_index("x"), lax.axis_index("y"), lax.axis_index("c")
    me = 4 * xi + 2 * yi + ci
    x0 = x[0]
    t = x0.shape[0]
    target = loss_target[0]

    c_g, ng, small, w_qkv0 = _all_gather(
        [jnp.pad(c, ((0, 7), (0, 0))), norm_g.reshape(depth * 4, d // N_DEV), _pack([sg_ln_g, sc_conv_w]),
         sb_w_qkv[0].astype(BF16)], "gather_setup")

    c_all = c_g[:, 0, :]
    na = ada_w.shape[2]
    b_cols = lax.dynamic_slice_in_dim(ada_b, me * na, na, axis=1)[:, None, :]
    mod_part = ada_fwd(c_all, ada_w, b_cols, "ada_fwd")
    mod_g = _all_gather([mod_part.reshape(depth * N_DEV, na)], "gather_mod")[0]
    mod_g = mod_g.reshape(N_DEV, depth, N_DEV, na)
    mod_me = lax.dynamic_index_in_dim(mod_g, me, axis=2, keepdims=False)
    mod = jnp.transpose(mod_me, (1, 0, 2)).reshape(depth, 6, 1, d)

    norm_full = jnp.transpose(ng, (1, 0, 2)).reshape(depth, 4, 1, d)
    small = small.reshape(N_DEV, -1)
    nl_g = sg_ln_g.shape[1]
    ln_full = small[:, :nl_g].reshape(1, N_DEV * nl_g)
    cwn = sc_conv_w.shape[2]
    cw_sh = small[:, nl_g:nl_g + 3 * cwn].reshape(N_DEV, 3, cwn)
    cw_full = jnp.transpose(cw_sh, (1, 0, 2)).reshape(3, d)
    cw_pad = jnp.pad(cw_full, ((0, 5), (0, 0)))

    bf = lambda a: a.astype(BF16)
    mixers = [
        [bf(sb_w_qkv[0]), bf(sb_w_o[0])],
        [bf(sg_w_in[0]), bf(sg_w_out[0])],
        [bf(sc_w_in[0]), bf(sc_w_out[0])],
        [bf(cb_w_qkv[0]), bf(cb_w_o[0])],
    ]
    shards = [[bf(ffn_w_in[i]), bf(ffn_w_out[i])] + mixers[i % 4] for i in range(depth)]
    gathered = [[None] * 4 for _ in range(depth)]
    gathered[0][2] = w_qkv0
    gather_plan = {
        ("sb_fwd", 0): [(0, 0), (0, 1), (0, 3), (1, 2), (2, 0)],
        ("ffn_in", 0): [(1, 0)], ("ffn_out", 0): [(1, 1), (1, 3)],
        ("mix_in", 1): [(3, 0)], ("ffn_in", 1): [(2, 1), (2, 3)], ("ffn_out", 1): [(2, 2)],
        ("ffn_in", 2): [(3, 1), (3, 3)], ("ffn_out", 2): [(3, 2)],
    }

    def gather_riders(host, layer):
        slots = gather_plan.get((host, layer), [])
        return (_Gather, [shards[l][s] for l, s in slots]) if slots else None

    def place_gathered(host, layer, ridden):
        for (l, s), arr in zip(gather_plan.get((host, layer), []), ridden):
            gathered[l][s] = arr

    bias_win = band_bias_window(cb_rel_bias[0])
    ws = sg_w_s[0]
    bias_t = jnp.pad(sg_bias[0].T, ((0, 0), (0, BLK - SG_GROUPS)))

    saved = []
    xcur = x0
    h = pre_fwd(x0, norm_full[0, 0], mod[0, 0], mod[0, 1], "L0_pre_m")
    for i in range(depth):
        mi = i % 4
        sh_m, sc_m, gt_m, sh_f, sc_f, gt_f = [mod[i, j] for j in range(6)]
        g0, g1, g2, g3 = [norm_full[i, j] for j in range(4)]
        tag = "L%d_" % i
        sv = {"x_in": xcur, "h_m": h}
        if mi == 0:
            qkv = mm_cs(h, gathered[0][2], tag + "qkv", out_dtype=BF16)
            qkv_t = jnp.transpose(qkv.reshape(t // SB_GW, SB_GW, 3 * d), (0, 2, 1))
            o, cmass, ridden = sb_fwd(qkv, qkv_t, gather_riders("sb_fwd", i)[1], tag + "sb_fwd")
            place_gathered("sb_fwd", i, ridden)
            sv.update(qkv=qkv, qkv_t=qkv_t, o=o, cmass=cmass)
            mixed = o
        else:
            w_in = gathered[i][2]
            out_dtype = BF16 if mi == 3 else F32
            if gather_riders("mix_in", i):
                pre, ridden = mm_cs(h, w_in, tag + "mix_in", out_dtype=out_dtype, riders=gather_riders("mix_in", i))
                place_gathered("mix_in", i, ridden)
            else:
                pre = mm_cs(h, w_in, tag + "mix_in", out_dtype=out_dtype)
            if mi == 1:
                mixed = sg_fwd(pre, ln_full, ws, bias_t, tag + "sg_fwd")
                sv.update(a=pre, yy=mixed)
            elif mi == 2:
                mixed = conv_fwd(pre, cw_pad, tag + "conv_fwd")
                sv.update(p=pre, gz=mixed)
            else:
                mixed = band_fwd(pre, bias_win, tag + "band_fwd")
                sv.update(qkv=pre, o=mixed)
        wfi, wfo, _, wmo = gathered[i]
        wfo4 = wfo.reshape(4, -1, d)
        y = mm(mixed, wmo.reshape(-1, d), tag + "mix_out")
        sv["y_m"] = y
        xmid, h2 = post_pre_fwd(xcur, y, g1, gt_m, g2, sh_f, sc_f, tag + "post_m_pre_f")
        sv["x_mid"] = xmid
        ag, au, s3, ridden = ffn_in_swiglu(h2, wfi, tag + "ffn_in", riders=gather_riders("ffn_in", i))
        place_gathered("ffn_in", i, ridden)
        if gather_riders("ffn_out", i):
            y2, ridden = mm_rs(s3, wfo4, tag + "ffn_out", riders=gather_riders("ffn_out", i))
            place_gathered("ffn_out", i, ridden)
        else:
            y2 = mm_rs(s3, wfo4, tag + "ffn_out")
        sv.update(h_f=h2, ag=ag, au=au, s3=s3, y_f=y2)
        saved.append(sv)
        if i + 1 < depth:
            xcur, h = post_pre_fwd(xmid, y2, g3, gt_f, norm_full[i + 1, 0], mod[i + 1, 0], mod[i + 1, 1],
                                   tag + "post_f_pre_m")

    last = depth - 1
    dx, dy2, lpart, dgt_f, dg3 = post_loss_bwd(saved[last]["x_mid"], saved[last]["y_f"], norm_full[last, 3],
                                               mod[last, 5], target, "loss")
    loss = lax.psum(0.5 * jnp.sum(lpart) / d, ("x", "y", "c"))

    dmod_rows = [None] * depth
    dnorm_rows = [None] * depth
    big_pieces = [[None] * 4 for _ in range(depth)]
    pending_dwmi, pending_dwfi, pending_layers = None, [], []
    small_grads = {}
    for i in reversed(range(depth)):
        wfi, wfo, wmi, wmo = gathered[i]
        wfo4 = wfo.reshape(4, -1, d)
        wmo2 = wmo.reshape(-1, d)
        wmi_t = jnp.transpose(wmi, (0, 2, 1)).reshape(-1, d)
        mi = i % 4
        sh_m, sc_m, gt_m, sh_f, sc_f, gt_f = [mod[i, j] for j in range(6)]
        g0, g1, g2, g3 = [norm_full[i, j] for j in range(4)]
        tag = "L%d_b_" % i
        sv = saved[i]
        da3, _ = ffn_out_dx_swiglu(dy2, wfo4, sv["ag"], sv["au"], tag + "ffn_out_dx")
        dwfo = mm_rs_dw(sv["s3"], dy2, tag + "ffn_out_dw", out_dtype=BF16).reshape(N_DEV, -1, d)
        dh2, (big_pieces[i][1],) = mm_cs_dx(da3, wfi, tag + "ffn_in_dx", act_major=True, riders=(_Exchange, [dwfo]))
        if pending_dwmi is not None:
            dwfi, (big_pieces[i + 1][2],) = mm_cs_dw(sv["h_f"], da3, tag + "ffn_in_dw", act_major=True,
                                                     out_dtype=BF16, riders=(_Exchange, [pending_dwmi]))
        else:
            dwfi = mm_cs_dw(sv["h_f"], da3, tag + "ffn_in_dw", act_major=True, out_dtype=BF16)
        dx, dy, dsh_f, dsc_f, dg2, dgt_m, dg1 = pre_post_bwd(
            dh2, sv["x_mid"], g2, sc_f, dx, sv["y_m"], g1, gt_m, tag + "pre_f_post_m")
        if mi == 0:
            do = mm_nt(dy, wmo2, tag + "wo_dx", out_dtype=BF16)
            dwmo = mm_tn(sv["o"], dy, tag + "wo_dw", out_dtype=BF16).reshape(N_DEV, -1, d)
            dq, dk, dv, ridden = sb_bwd(sv["qkv"], sv["qkv_t"], do, sv["cmass"],
                                        pending_dwfi + [dwfi, dwmo], tag + "sb_bwd")
            for n, j in enumerate(pending_layers):
                big_pieces[j][0] = ridden[n]
            big_pieces[0][0], big_pieces[0][3] = ridden[-2:]
            dmid = jnp.concatenate([dq, dk, dv], axis=1)
        elif mi == 1:
            dyy = mm_nt(dy, wmo2, tag + "sg_out_dx")
            dwmo = mm_tn(sv["yy"], dy, tag + "sg_out_dw", out_dtype=BF16).reshape(N_DEV, -1, d)
            dmid, dlng, dws, dbt = sg_bwd(sv["a"], dyy, ln_full, ws, bias_t, tag + "sg_bwd")
            small_grads.update(ln_g=jnp.sum(dlng, axis=0), w_s=dws, bias=dbt[:, :SG_GROUPS].T)
        elif mi == 2:
            dgz = mm_nt(dy, wmo2, tag + "sc_out_dx")
            dwmo = mm_tn(sv["gz"], dy, tag + "sc_out_dw", out_dtype=BF16).reshape(N_DEV, -1, d)
            dmid, dcw = conv_bwd(sv["p"], dgz, cw_pad, tag + "conv_bwd")
            small_grads.update(conv_w=jnp.sum(dcw, axis=1))
        else:
            do = mm_nt(dy, wmo2, tag + "wo_dx", out_dtype=BF16)
            dwmo = mm_tn(sv["o"], dy, tag + "wo_dw", out_dtype=BF16).reshape(N_DEV, -1, d)
            dq, dk, dv, dwin, (big_pieces[i][0], big_pieces[i][3]) = band_bwd(
                sv["qkv"], do, bias_win, tag + "band_bwd", riders=(_Exchange, [dwfi, dwmo]))
            dmid = jnp.concatenate([dq, dk, dv], axis=1)
            small_grads.update(rel_bias=band_bias_window_grad(dwin))
        pending_dwmi = mm_cs_dw(sv["h_m"], dmid, tag + "mix_in_dw", out_dtype=BF16)
        if mi in (1, 2):
            dh, (big_pieces[i][3],) = mm(dmid, wmi_t, tag + "mix_in_dx", riders=(_Exchange, [dwmo]))
            pending_dwfi.append(dwfi)
            pending_layers.append(i)
        elif i == 0:
            dh, (big_pieces[0][2],) = mm(dmid, wmi_t, tag + "mix_in_dx", riders=(_Exchange, [pending_dwmi]))
        else:
            dh = mm(dmid, wmi_t, tag + "mix_in_dx")
        if i > 0:
            dx, dy2_prev, dsh_m, dsc_m, dg0, dgt_f_prev, dg3_prev = pre_post_bwd(
                dh, sv["x_in"], g0, sc_m, dx, saved[i - 1]["y_f"], norm_full[i - 1, 3], mod[i - 1, 5],
                tag + "pre_m_post_f")
        else:
            dx, dsh_m, dsc_m, dg0 = pre_bwd(dh, sv["x_in"], g0, sc_m, dx, tag + "pre_m")
        dmod_rows[i] = jnp.stack([jnp.sum(q, axis=0) for q in (dsh_m, dsc_m, dgt_m, dsh_f, dsc_f, dgt_f)])
        dnorm_rows[i] = jnp.stack([jnp.sum(q, axis=0) for q in (dg0, dg1, dg2, dg3)])
        if i > 0:
            dy2, dgt_f, dg3 = dy2_prev, dgt_f_prev, dg3_prev
    grad_x = dx[None]

    out_g, out_d, out_m, out_v = {}, {}, {}, {}

    def upd(name, pieces, w, m, v, riders=None):
        rows_cols = (len(pieces) * pieces[0].shape[1], pieces[0].shape[2])
        res, ridden = adamw(pieces, w.reshape(rows_cols), m.reshape(rows_cols), v.reshape(rows_cols),
                            "adamw_" + name, riders)
        out_g[name], out_d[name], out_m[name], out_v[name] = [q.reshape(w.shape) for q in res]
        return ridden

    dmod_mine = jnp.stack(dmod_rows).reshape(depth, 6 * d)
    dnorm_mine = jnp.stack(dnorm_rows)
    small_list = [dnorm_mine, small_grads["ln_g"], small_grads["bias"], small_grads["conv_w"],
                  small_grads["rel_bias"]]
    small_shapes = [dmod_mine.shape] + [a.shape for a in small_list]
    slab = _pack([dmod_mine] + small_list)
    upd("ffn_w_in", [big_pieces[i][0] for i in range(depth)], ffn_w_in, m_ffn_w_in, v_ffn_w_in)
    slab_g, ws_g = upd("ffn_w_out", [big_pieces[i][1] for i in range(depth)], ffn_w_out, m_ffn_w_out, v_ffn_w_out,
                       riders=(_Gather, [slab, small_grads["w_s"].reshape(-1, BLK)]))
    upd("sg_w_s", [ws_g], sg_w_s, m_sg_w_s, v_sg_w_s)
    upd("sb_w_qkv", [big_pieces[0][2]], sb_w_qkv, m_sb_w_qkv, v_sb_w_qkv)
    upd("sb_w_o", [big_pieces[0][3]], sb_w_o, m_sb_w_o, v_sb_w_o)
    upd("sg_w_in", [big_pieces[1][2]], sg_w_in, m_sg_w_in, v_sg_w_in)
    upd("sg_w_out", [big_pieces[1][3]], sg_w_out, m_sg_w_out, v_sg_w_out)
    upd("sc_w_in", [big_pieces[2][2]], sc_w_in, m_sc_w_in, v_sc_w_in)
    upd("sc_w_out", [big_pieces[2][3]], sc_w_out, m_sc_w_out, v_sc_w_out)
    upd("cb_w_qkv", [big_pieces[3][2]], cb_w_qkv, m_cb_w_qkv, v_cb_w_qkv)
    upd("cb_w_o", [big_pieces[3][3]], cb_w_o, m_cb_w_o, v_cb_w_o)

    tot = sum_pieces(slab_g, "sum_small_grads")
    g_ada_b_full, g_norm, g_ln, g_sbias, g_cw, g_rb = _unpack(tot, small_shapes)
    dmod_all = slab_g.reshape(N_DEV, -1)[:, :depth * 6 * d].reshape(N_DEV, depth, 6 * d)
    dmod_cols = lax.dynamic_slice_in_dim(dmod_all, me * na, na, axis=2)
    g_ada_w = ada_bwd(c_all, jnp.transpose(dmod_cols, (1, 0, 2)), "ada_bwd")

    nsh = d // N_DEV
    g_norm_sh = lax.dynamic_slice_in_dim(g_norm, me * nsh, nsh, axis=2)
    g_ln_sh = lax.dynamic_slice_in_dim(g_ln.reshape(1, -1), me * nl_g, nl_g, axis=1)
    g_cw_sh = lax.dynamic_slice_in_dim(g_cw, me * cwn, cwn, axis=1)[None]

    upd("ada_w", [g_ada_w.reshape(1, depth * d, na)], ada_w, m_ada_w, v_ada_w)

    small_names = ["ada_b", "norm_g", "sg_ln_g", "sg_bias", "sc_conv_w", "cb_rel_bias"]
    small_g = [g_ada_b_full, g_norm_sh, g_ln_sh, g_sbias[None], g_cw_sh, g_rb[None]]
    small_w = [ada_b, norm_g, sg_ln_g, sg_bias, sc_conv_w, cb_rel_bias]
    small_m = [m_ada_b, m_norm_g, m_sg_ln_g, m_sg_bias, m_sc_conv_w, m_cb_rel_bias]
    small_v = [v_ada_b, v_norm_g, v_sg_ln_g, v_sg_bias, v_sc_conv_w, v_cb_rel_bias]
    shapes = [w.shape for w in small_w]
    res, _ = adamw([_pack(small_g)[None]], _pack(small_w), _pack(small_m), _pack(small_v), "adamw_small")
    for nm_, gs, ds_, ms, vs in zip(small_names, *[_unpack(r, shapes) for r in res]):
        out_g[nm_], out_d[nm_], out_m[nm_], out_v[nm_] = gs, ds_, ms, vs

    order = ["ada_w", "ada_b", "norm_g", "ffn_w_in", "ffn_w_out", "sb_w_qkv", "sb_w_o", "sg_w_in", "sg_ln_g",
             "sg_w_s", "sg_bias", "sg_w_out", "sc_w_in", "sc_conv_w", "sc_w_out", "cb_w_qkv", "cb_rel_bias", "cb_w_o"]
    return (loss, grad_x, *[out_g[n] for n in order], *[out_d[n] for n in order],
            *[out_m[n] for n in order], *[out_v[n] for n in order])
```
